```python
import jax, jax.numpy as jnp
from jax import lax
import numpy as np

D_MODEL = 1024
BATCH = 8
SEQ = 4096
DEPTH = 1

N_META = 16
D_MIX = D_MODEL
D_ATTN = D_MIX // 2
D_CONV = D_MIX - D_ATTN
N_HEADS = 8
HEAD_DIM = D_ATTN // N_HEADS
CONV_WIDTH = 31
Q_BLOCK = 128
LN_EPS = 1e-5
ALPHA = (2.0 * DEPTH) ** 0.25
BETA = (8.0 * DEPTH) ** -0.25

OFF_K = D_ATTN
OFF_V = 2 * D_ATTN
OFF_F = 3 * D_ATTN
OFF_GA = OFF_F + N_HEADS
OFF_U = OFF_GA + D_ATTN
OFF_UG = OFF_U + D_CONV
OFF_GC = OFF_UG + D_CONV
D_IN = OFF_GC + D_CONV
SPLITS = (OFF_K, OFF_V, OFF_F, OFF_GA, OFF_U, OFF_UG, OFF_GC)

kernel_name = "fox_conformer_hymba_deepnorm"


def layer_norm(x, g, b):
    xf = x.astype(jnp.float32)
    mu = jnp.mean(xf, axis=-1, keepdims=True)
    var = jnp.mean(jnp.square(xf - mu), axis=-1, keepdims=True)
    return ((xf - mu) * lax.rsqrt(var + LN_EPS) * g + b).astype(x.dtype)


def forgetting_attention(q, k, v, log_f):
    L = q.shape[1]
    scale = HEAD_DIM ** -0.5
    c = jnp.cumsum(log_f, axis=1).transpose(0, 2, 1)
    starts = [0] + list(range(N_META, L, Q_BLOCK))
    ends = starts[1:] + [L]
    outs = []
    for qs, qe in zip(starts, ends):
        qb, kb, vb = q[:, qs:qe], k[:, :qe], v[:, :qe]
        s = jnp.einsum('bqhd,bkhd->bhqk', qb, kb, preferred_element_type=jnp.float32) * scale
        bias = c[:, :, qs:qe, None] - c[:, :, None, :qe]
        qpos = jnp.arange(qs, qe)[:, None]
        kpos = jnp.arange(qe)[None, :]
        s = jnp.where(kpos <= qpos, s + bias, -jnp.inf)
        p = jax.nn.softmax(s, axis=-1)
        outs.append(jnp.einsum('bhqk,bkhd->bqhd', p.astype(vb.dtype), vb))
    return jnp.concatenate(outs, axis=1)


def conformer_conv(u, u_gate, conv_w, conv_b, g, b, w_pw):
    h = u * jax.nn.sigmoid(u_gate)
    h = lax.conv_general_dilated(
        h, conv_w[:, None, :].astype(h.dtype), window_strides=(1,),
        padding=[(CONV_WIDTH - 1, 0)], dimension_numbers=('NWC', 'WIO', 'NWC'),
        feature_group_count=D_CONV) + conv_b
    h = jax.nn.silu(layer_norm(h, g, b))
    return h @ w_pw


def hybrid_layer(h, w_in, b_f, conv_w, conv_b, ln_conv_g, ln_conv_b, w_pw, w_out, ln_out_g, ln_out_b):
    B, L, _ = h.shape
    proj = h @ w_in
    q, k, v, f_logit, g_attn, u, u_gate, g_conv = jnp.split(proj, SPLITS, axis=-1)
    q = q.reshape(B, L, N_HEADS, HEAD_DIM)
    k = k.reshape(B, L, N_HEADS, HEAD_DIM)
    v = v.reshape(B, L, N_HEADS, HEAD_DIM)
    log_f = jax.nn.log_sigmoid((f_logit + b_f).astype(jnp.float32))
    y_attn = forgetting_attention(q, k, v, log_f).reshape(B, L, D_ATTN) * jax.nn.silu(g_attn)
    y_conv = conformer_conv(u, u_gate, conv_w, conv_b, ln_conv_g, ln_conv_b, w_pw) * jax.nn.silu(g_conv)
    y = jnp.concatenate([y_attn, y_conv], axis=-1) @ w_out
    return layer_norm(ALPHA * h + y, ln_out_g, ln_out_b)


def _fwd_setup_inputs(seed: int = 0) -> dict:
    key = jax.random.key(seed)
    ks = jax.random.split(key, 16)
    f32 = jnp.float32
    x = jax.random.normal(ks[0], (BATCH, SEQ, D_MODEL), f32)
    meta = jax.random.normal(ks[1], (N_META, D_MODEL), f32)
    ln_in_g = 1.0 + 0.02 * jax.random.normal(ks[2], (D_MODEL,), f32)
    ln_in_b = 0.02 * jax.random.normal(ks[3], (D_MODEL,), f32)
    col_scale = jnp.ones((D_IN,), f32).at[OFF_V:OFF_F].set(BETA)
    w_in = jax.random.normal(ks[4], (DEPTH, D_MODEL, D_IN), f32) * (D_MODEL ** -0.5) * col_scale
    b_f = jnp.linspace(1.0, 6.0, N_HEADS, dtype=f32)[None, :] + 0.1 * jax.random.normal(ks[5], (DEPTH, N_HEADS), f32)
    conv_w = jax.random.normal(ks[6], (DEPTH, CONV_WIDTH, D_CONV), f32) * (CONV_WIDTH ** -0.5)
    conv_b = 0.02 * jax.random.normal(ks[7], (DEPTH, D_CONV), f32)
    ln_conv_g = 1.0 + 0.02 * jax.random.normal(ks[8], (DEPTH, D_CONV), f32)
    ln_conv_b = 0.02 * jax.random.normal(ks[9], (DEPTH, D_CONV), f32)
    w_pw = jax.random.normal(ks[10], (DEPTH, D_CONV, D_CONV), f32) * (D_CONV ** -0.5) * BETA
    w_out = jax.random.normal(ks[11], (DEPTH, D_MIX, D_MODEL), f32) * (D_MIX ** -0.5) * BETA
    ln_out_g = 1.0 + 0.02 * jax.random.normal(ks[12], (DEPTH, D_MODEL), f32)
    ln_out_b = 0.02 * jax.random.normal(ks[13], (DEPTH, D_MODEL), f32)
    return {"x": x, "meta": meta, "ln_in_g": ln_in_g, "ln_in_b": ln_in_b, "w_in": w_in,
            "b_f": b_f, "conv_w": conv_w, "conv_b": conv_b, "ln_conv_g": ln_conv_g,
            "ln_conv_b": ln_conv_b, "w_pw": w_pw, "w_out": w_out,
            "ln_out_g": ln_out_g, "ln_out_b": ln_out_b}


def _fwd_reference(x, meta, ln_in_g, ln_in_b, w_in, b_f, conv_w, conv_b, ln_conv_g, ln_conv_b,
              w_pw, w_out, ln_out_g, ln_out_b):
    B = x.shape[0]
    meta_b = jnp.broadcast_to(meta[None].astype(x.dtype), (B, N_META, D_MODEL))
    h = jnp.concatenate([meta_b, x], axis=1)
    h = layer_norm(h, ln_in_g, ln_in_b)
    for l in range(DEPTH):
        h = hybrid_layer(h, w_in[l], b_f[l], conv_w[l], conv_b[l], ln_conv_g[l], ln_conv_b[l],
                         w_pw[l], w_out[l], ln_out_g[l], ln_out_b[l])
    return h[:, N_META:]


import jax as _jax
import jax.numpy as _jnp

TWIN_FORMAT = 'train_step'
FWD_PARAMS = ['x', 'meta', 'ln_in_g', 'ln_in_b', 'w_in', 'b_f', 'conv_w', 'conv_b', 'ln_conv_g', 'ln_conv_b', 'w_pw', 'w_out', 'ln_out_g', 'ln_out_b']
TWIN_WEIGHTS = ['meta', 'ln_in_g', 'ln_in_b', 'w_in', 'b_f', 'conv_w', 'conv_b', 'ln_conv_g', 'ln_conv_b', 'w_pw', 'w_out', 'ln_out_g', 'ln_out_b']
TWIN_DIFF_INPUT = 'x'
TWIN_INPUTS = ['x', 'meta', 'ln_in_g', 'ln_in_b', 'w_in', 'b_f', 'conv_w', 'conv_b', 'ln_conv_g', 'ln_conv_b', 'w_pw', 'w_out', 'ln_out_g', 'ln_out_b', 'loss_target', 'm_meta', 'm_ln_in_g', 'm_ln_in_b', 'm_w_in', 'm_b_f', 'm_conv_w', 'm_conv_b', 'm_ln_conv_g', 'm_ln_conv_b', 'm_w_pw', 'm_w_out', 'm_ln_out_g', 'm_ln_out_b', 'v_meta', 'v_ln_in_g', 'v_ln_in_b', 'v_w_in', 'v_b_f', 'v_conv_w', 'v_conv_b', 'v_ln_conv_g', 'v_ln_conv_b', 'v_w_pw', 'v_w_out', 'v_ln_out_g', 'v_ln_out_b']
TWIN_OUTPUTS = ['loss', 'grad_x', 'grad_meta', 'grad_ln_in_g', 'grad_ln_in_b', 'grad_w_in', 'grad_b_f', 'grad_conv_w', 'grad_conv_b', 'grad_ln_conv_g', 'grad_ln_conv_b', 'grad_w_pw', 'grad_w_out', 'grad_ln_out_g', 'grad_ln_out_b', 'delta_meta', 'delta_ln_in_g', 'delta_ln_in_b', 'delta_w_in', 'delta_b_f', 'delta_conv_w', 'delta_conv_b', 'delta_ln_conv_g', 'delta_ln_conv_b', 'delta_w_pw', 'delta_w_out', 'delta_ln_out_g', 'delta_ln_out_b', 'new_m_meta', 'new_m_ln_in_g', 'new_m_ln_in_b', 'new_m_w_in', 'new_m_b_f', 'new_m_conv_w', 'new_m_conv_b', 'new_m_ln_conv_g', 'new_m_ln_conv_b', 'new_m_w_pw', 'new_m_w_out', 'new_m_ln_out_g', 'new_m_ln_out_b', 'new_v_meta', 'new_v_ln_in_g', 'new_v_ln_in_b', 'new_v_w_in', 'new_v_b_f', 'new_v_conv_w', 'new_v_conv_b', 'new_v_ln_conv_g', 'new_v_ln_conv_b', 'new_v_w_pw', 'new_v_w_out', 'new_v_ln_out_g', 'new_v_ln_out_b']
TWIN_LEAF_KINDS = {'loss': 'loss', 'grad_x': 'grad_x', 'grad_meta': 'grad_w', 'grad_ln_in_g': 'grad_w', 'grad_ln_in_b': 'grad_w', 'grad_w_in': 'grad_w', 'grad_b_f': 'grad_w', 'grad_conv_w': 'grad_w', 'grad_conv_b': 'grad_w', 'grad_ln_conv_g': 'grad_w', 'grad_ln_conv_b': 'grad_w', 'grad_w_pw': 'grad_w', 'grad_w_out': 'grad_w', 'grad_ln_out_g': 'grad_w', 'grad_ln_out_b': 'grad_w', 'delta_meta': 'delta_w', 'delta_ln_in_g': 'delta_w', 'delta_ln_in_b': 'delta_w', 'delta_w_in': 'delta_w', 'delta_b_f': 'delta_w', 'delta_conv_w': 'delta_w', 'delta_conv_b': 'delta_w', 'delta_ln_conv_g': 'delta_w', 'delta_ln_conv_b': 'delta_w', 'delta_w_pw': 'delta_w', 'delta_w_out': 'delta_w', 'delta_ln_out_g': 'delta_w', 'delta_ln_out_b': 'delta_w', 'new_m_meta': 'new_m', 'new_m_ln_in_g': 'new_m', 'new_m_ln_in_b': 'new_m', 'new_m_w_in': 'new_m', 'new_m_b_f': 'new_m', 'new_m_conv_w': 'new_m', 'new_m_conv_b': 'new_m', 'new_m_ln_conv_g': 'new_m', 'new_m_ln_conv_b': 'new_m', 'new_m_w_pw': 'new_m', 'new_m_w_out': 'new_m', 'new_m_ln_out_g': 'new_m', 'new_m_ln_out_b': 'new_m', 'new_v_meta': 'new_v', 'new_v_ln_in_g': 'new_v', 'new_v_ln_in_b': 'new_v', 'new_v_w_in': 'new_v', 'new_v_b_f': 'new_v', 'new_v_conv_w': 'new_v', 'new_v_conv_b': 'new_v', 'new_v_ln_conv_g': 'new_v', 'new_v_ln_conv_b': 'new_v', 'new_v_w_pw': 'new_v', 'new_v_w_out': 'new_v', 'new_v_ln_out_g': 'new_v', 'new_v_ln_out_b': 'new_v'}


def _forward(args):
    return _fwd_reference(*[args[k] for k in FWD_PARAMS])


def _output_shape():
    out = _jax.eval_shape(lambda: _forward(_fwd_setup_inputs(0)))
    return out.shape, out.dtype

N_MICROBATCH = 1
ADAM_LR = 0.001
ADAM_B1 = 0.9
ADAM_B2 = 0.999
ADAM_EPS = 1e-08
ADAM_WD = 0.01
ADAM_STEP = 10
PER_EXAMPLE_BATCH_AXIS = {'x': 0, 'loss_target': 0}
SHARED_INPUTS = []
_WEIGHT_DTYPES = {'meta': _jnp.float32, 'ln_in_g': _jnp.float32, 'ln_in_b': _jnp.float32, 'w_in': _jnp.float32, 'b_f': _jnp.float32, 'conv_w': _jnp.float32, 'conv_b': _jnp.float32, 'ln_conv_g': _jnp.float32, 'ln_conv_b': _jnp.float32, 'w_pw': _jnp.float32, 'w_out': _jnp.float32, 'ln_out_g': _jnp.float32, 'ln_out_b': _jnp.float32}
MOMENT_SCALE = {'meta': 1.001668e-03, 'ln_in_g': 1.020959e+00, 'ln_in_b': 4.721524e-01, 'w_in': 1.379161e-02, 'b_f': 8.577902e-02, 'conv_w': 1.987310e-02, 'conv_b': 3.964470e-02, 'ln_conv_g': 2.764030e-02, 'ln_conv_b': 2.173082e-02, 'w_pw': 3.278439e-02, 'w_out': 2.549433e-02, 'ln_out_g': 3.203166e+01, 'ln_out_b': 6.664466e-01}


def _to_microbatches(a, axis):
    t = _jnp.moveaxis(a, axis, 0)
    t = t.reshape((N_MICROBATCH, t.shape[0] // N_MICROBATCH) + t.shape[1:])
    return _jnp.moveaxis(t, 1, axis + 1)


def setup_inputs(seed: int = 0) -> dict:
    inp = _fwd_setup_inputs(seed)
    key = _jax.random.fold_in(_jax.random.key(seed), 7919)
    shape, _ = _output_shape()
    out = dict(inp)
    out["loss_target"] = _jax.random.normal(_jax.random.fold_in(key, 0), shape, _jnp.float32)
    for i, name in enumerate(TWIN_WEIGHTS):
        w = inp[name].astype(_jnp.float32)
        if MOMENT_SCALE is None:
            s = _jnp.sqrt(_jnp.mean(_jnp.square(w)) + 1e-30)
        else:
            s = MOMENT_SCALE[name]
        km, kv = _jax.random.split(_jax.random.fold_in(key, i + 1))
        out[name] = w
        out["m_" + name] = s * _jax.random.normal(km, w.shape, _jnp.float32)
        out["v_" + name] = (s * s) * _jax.random.uniform(kv, w.shape, _jnp.float32, 0.5, 1.5)
    if N_MICROBATCH > 1:
        for name, axis in PER_EXAMPLE_BATCH_AXIS.items():
            out[name] = _to_microbatches(out[name], axis)
    return {'x': out['x'], 'meta': out['meta'], 'ln_in_g': out['ln_in_g'], 'ln_in_b': out['ln_in_b'], 'w_in': out['w_in'], 'b_f': out['b_f'], 'conv_w': out['conv_w'], 'conv_b': out['conv_b'], 'ln_conv_g': out['ln_conv_g'], 'ln_conv_b': out['ln_conv_b'], 'w_pw': out['w_pw'], 'w_out': out['w_out'], 'ln_out_g': out['ln_out_g'], 'ln_out_b': out['ln_out_b'], 'loss_target': out['loss_target'], 'm_meta': out['m_meta'], 'm_ln_in_g': out['m_ln_in_g'], 'm_ln_in_b': out['m_ln_in_b'], 'm_w_in': out['m_w_in'], 'm_b_f': out['m_b_f'], 'm_conv_w': out['m_conv_w'], 'm_conv_b': out['m_conv_b'], 'm_ln_conv_g': out['m_ln_conv_g'], 'm_ln_conv_b': out['m_ln_conv_b'], 'm_w_pw': out['m_w_pw'], 'm_w_out': out['m_w_out'], 'm_ln_out_g': out['m_ln_out_g'], 'm_ln_out_b': out['m_ln_out_b'], 'v_meta': out['v_meta'], 'v_ln_in_g': out['v_ln_in_g'], 'v_ln_in_b': out['v_ln_in_b'], 'v_w_in': out['v_w_in'], 'v_b_f': out['v_b_f'], 'v_conv_w': out['v_conv_w'], 'v_conv_b': out['v_conv_b'], 'v_ln_conv_g': out['v_ln_conv_g'], 'v_ln_conv_b': out['v_ln_conv_b'], 'v_w_pw': out['v_w_pw'], 'v_w_out': out['v_w_out'], 'v_ln_out_g': out['v_ln_out_g'], 'v_ln_out_b': out['v_ln_out_b']}


def _loss(weights, diff, rest, loss_target):
    with _jax.named_scope("forward"):
        args = {**rest, TWIN_DIFF_INPUT: diff, **{k: w.astype(_WEIGHT_DTYPES[k]) for k, w in weights.items()}}
        y = _forward(args)
    with _jax.named_scope("loss_head"):
        err = _jnp.square(y.astype(_jnp.float32) - loss_target)
        return 0.5 * _jnp.sum(_jnp.mean(err, axis=-1)) if err.ndim else 0.5 * err


def _adamw(w, g, m, v):
    m = ADAM_B1 * m + (1.0 - ADAM_B1) * g
    v = ADAM_B2 * v + (1.0 - ADAM_B2) * _jnp.square(g)
    m_hat = m / (1.0 - ADAM_B1 ** ADAM_STEP)
    v_hat = v / (1.0 - ADAM_B2 ** ADAM_STEP)
    delta = -ADAM_LR * (m_hat / (_jnp.sqrt(v_hat) + ADAM_EPS) + ADAM_WD * w)
    return delta, m, v


def reference(x, meta, ln_in_g, ln_in_b, w_in, b_f, conv_w, conv_b, ln_conv_g, ln_conv_b, w_pw, w_out, ln_out_g, ln_out_b, loss_target, m_meta, m_ln_in_g, m_ln_in_b, m_w_in, m_b_f, m_conv_w, m_conv_b, m_ln_conv_g, m_ln_conv_b, m_w_pw, m_w_out, m_ln_out_g, m_ln_out_b, v_meta, v_ln_in_g, v_ln_in_b, v_w_in, v_b_f, v_conv_w, v_conv_b, v_ln_conv_g, v_ln_conv_b, v_w_pw, v_w_out, v_ln_out_g, v_ln_out_b):
    given = dict(x=x, meta=meta, ln_in_g=ln_in_g, ln_in_b=ln_in_b, w_in=w_in, b_f=b_f, conv_w=conv_w, conv_b=conv_b, ln_conv_g=ln_conv_g, ln_conv_b=ln_conv_b, w_pw=w_pw, w_out=w_out, ln_out_g=ln_out_g, ln_out_b=ln_out_b, loss_target=loss_target, m_meta=m_meta, m_ln_in_g=m_ln_in_g, m_ln_in_b=m_ln_in_b, m_w_in=m_w_in, m_b_f=m_b_f, m_conv_w=m_conv_w, m_conv_b=m_conv_b, m_ln_conv_g=m_ln_conv_g, m_ln_conv_b=m_ln_conv_b, m_w_pw=m_w_pw, m_w_out=m_w_out, m_ln_out_g=m_ln_out_g, m_ln_out_b=m_ln_out_b, v_meta=v_meta, v_ln_in_g=v_ln_in_g, v_ln_in_b=v_ln_in_b, v_w_in=v_w_in, v_b_f=v_b_f, v_conv_w=v_conv_w, v_conv_b=v_conv_b, v_ln_conv_g=v_ln_conv_g, v_ln_conv_b=v_ln_conv_b, v_w_pw=v_w_pw, v_w_out=v_w_out, v_ln_out_g=v_ln_out_g, v_ln_out_b=v_ln_out_b)
    weights = {n: given[n] for n in TWIN_WEIGHTS}
    shared = {n: given[n] for n in SHARED_INPUTS}
    per_example = {n: given[n] for n in ['x']}
    grad_fn = _jax.value_and_grad(_loss, argnums=(0, 1))

    def one_microbatch(ex, loss_target):
        ex = dict(ex)
        diff = ex.pop(TWIN_DIFF_INPUT)
        return grad_fn(weights, diff, {**shared, **ex}, loss_target)

    if N_MICROBATCH == 1:
        loss, (grad_w, grad_x) = one_microbatch(per_example, given["loss_target"])
    else:
        def body(carry, xs):
            loss_sum, grad_sum = carry
            l_k, (gw_k, gx_k) = one_microbatch(xs[0], xs[1])
            with _jax.named_scope("update"):
                return (loss_sum + l_k, _jax.tree.map(_jnp.add, grad_sum, gw_k)), gx_k

        init = (_jnp.zeros((), _jnp.float32), _jax.tree.map(_jnp.zeros_like, weights))
        (loss, grad_w), grad_x = _jax.lax.scan(body, init, (per_example, given["loss_target"]))
    with _jax.named_scope("update"):
        delta_w, new_m, new_v = {}, {}, {}
        for n in TWIN_WEIGHTS:
            delta_w[n], new_m[n], new_v[n] = _adamw(weights[n], grad_w[n], given["m_" + n], given["v_" + n])
    return (loss, grad_x, *[grad_w[n] for n in TWIN_WEIGHTS], *[delta_w[n] for n in TWIN_WEIGHTS],
            *[new_m[n] for n in TWIN_WEIGHTS], *[new_v[n] for n in TWIN_WEIGHTS])
```

```python
import functools

import jax
import jax.numpy as jnp
import numpy as np
from jax import lax
from jax.experimental import pallas as pl
from jax.experimental.pallas import tpu as pltpu

F32 = jnp.float32
BF16 = jnp.bfloat16

D_MODEL = 1024
D_ATTN = 512
D_CONV = 512
N_HEADS = 8
HEAD_DIM = 64
N_META = 16
CONV_WIDTH = 31
LN_EPS = 1e-5
ALPHA = 2.0 ** 0.25
SCALE = HEAD_DIM ** -0.5
ADAM_LR, ADAM_B1, ADAM_B2, ADAM_EPS, ADAM_WD, ADAM_STEP = 0.001, 0.9, 0.999, 1e-08, 0.01, 10

N_DEV = 8
D_IN = 3592
SHARD_IN = D_IN // N_DEV
TILE = 256
PAD = TILE - N_META
HALO = 32
NEG = -1e30
LANES = 128
W_COLS = 7 * 512 + LANES
OFF_GA_R, OFF_F_R = 1536, 3584
MIB = 1024 * 1024


def _cp(vmem_mib, sem=None):
    kw = dict(vmem_limit_bytes=vmem_mib * MIB)
    if sem is not None:
        kw["dimension_semantics"] = sem
    return pltpu.CompilerParams(**kw)


def _sigmoid(x):
    return 1.0 / (1.0 + jnp.exp(-x))


def _silu_and_grad(x):
    s = _sigmoid(x)
    return x * s, s * (1.0 + x * (1.0 - s))


def _ln_stats(x):
    mu = jnp.mean(x, axis=-1, keepdims=True)
    xc = x - mu
    var = jnp.mean(xc * xc, axis=-1, keepdims=True)
    rstd = lax.rsqrt(var + LN_EPS)
    return xc * rstd, rstd


def _ln_bwd(dy, xhat, rstd, g):
    dxh = dy * g
    m1 = jnp.mean(dxh, axis=-1, keepdims=True)
    m2 = jnp.mean(dxh * xhat, axis=-1, keepdims=True)
    return rstd * (dxh - m1 - xhat * m2)


def _row_spec(cols, shift=False):
    if shift:
        return pl.BlockSpec((TILE, cols), lambda i: (jnp.maximum(i - 1, 0), 0))
    return pl.BlockSpec((TILE, cols), lambda i: (i, 0))


def _full_spec(shape):
    nd = len(shape)
    return pl.BlockSpec(shape, lambda i: (0,) * nd)


def _t3_spec(ch):
    return pl.BlockSpec((1, ch, TILE), lambda i: (i, 0, 0))


def _proj_fwd(x, metapad, g_in, b_in, w_r, nt):
    lp = nt * TILE

    def body(x_ref, mp_ref, g_ref, b_ref, w_ref, hb_ref, qT_ref, kT_ref, vT_ref, k_ref, v_ref,
             ga_ref, u_ref, ug_ref, gc_ref, fl_ref):
        i = pl.program_id(0)
        x0 = jnp.where(i == 0, mp_ref[...], x_ref[...])
        xhat, _ = _ln_stats(x0)
        hb = (xhat * g_ref[...] + b_ref[...]).astype(BF16)
        hb_ref[...] = hb

        def sec(off, n=512):
            return jnp.dot(hb, w_ref[:, off:off + n], preferred_element_type=F32)

        qT_ref[0] = (sec(0) * SCALE).T.astype(BF16)
        k = sec(512)
        kT_ref[0] = k.T.astype(BF16)
        k_ref[...] = k.astype(BF16)
        v = sec(1024)
        vT_ref[0] = v.T.astype(BF16)
        v_ref[...] = v.astype(BF16)
        ga_ref[...] = sec(OFF_GA_R)
        u_ref[...] = sec(OFF_GA_R + 512)
        ug_ref[...] = sec(OFF_GA_R + 1024)
        gc_ref[...] = sec(OFF_GA_R + 1536)
        fl_ref[...] = sec(OFF_F_R, LANES)

    t3 = jax.ShapeDtypeStruct((nt, 512, TILE), BF16)
    rm = lambda dt: jax.ShapeDtypeStruct((lp, 512), dt)
    return pl.pallas_call(
        body, name="proj_fwd", grid=(nt,),
        in_specs=[_row_spec(D_MODEL, shift=True), _full_spec((TILE, D_MODEL)), _full_spec((1, D_MODEL)),
                  _full_spec((1, D_MODEL)), _full_spec((D_MODEL, W_COLS))],
        out_specs=[_row_spec(D_MODEL), _t3_spec(512), _t3_spec(512), _t3_spec(512), _row_spec(512), _row_spec(512),
                   _row_spec(512), _row_spec(512), _row_spec(512), _row_spec(512), _row_spec(LANES)],
        out_shape=[jax.ShapeDtypeStruct((lp, D_MODEL), BF16), t3, t3, t3, rm(BF16), rm(BF16),
                   rm(F32), rm(F32), rm(F32), rm(F32), jax.ShapeDtypeStruct((lp, LANES), F32)],
        compiler_params=_cp(56, ("arbitrary",)),
    )(x, metapad, g_in, b_in, w_r)


def _row_mask(i, shape):
    r = lax.broadcasted_iota(jnp.int32, shape, 0)
    return (r >= PAD) | (i > 0)


def _cumsum_fwd(fl, bf_pad, nt):
    lp = nt * TILE

    def body(fl_ref, bf_ref, ck_ref, cq_ref, carry):
        i = pl.program_id(0)

        @pl.when(i == 0)
        def _():
            carry[...] = jnp.zeros_like(carry)

        z = fl_ref[...] + bf_ref[...]
        lf = jnp.minimum(z, 0.0) - jnp.log(1.0 + jnp.exp(-jnp.abs(z)))
        lane = lax.broadcasted_iota(jnp.int32, (TILE, LANES), 1)
        lf = jnp.where(_row_mask(i, (TILE, LANES)) & (lane < N_HEADS), lf, 0.0)
        r = lax.broadcasted_iota(jnp.int32, (TILE, TILE), 0)
        c = lax.broadcasted_iota(jnp.int32, (TILE, TILE), 1)
        tril = (c <= r).astype(F32)
        cs = jnp.dot(tril, lf, precision=lax.Precision.HIGHEST, preferred_element_type=F32) + carry[...]
        carry[...] = cs[TILE - 1:TILE, :]
        cst = cs.T
        for p in range(N_HEADS // 2):
            ck_ref[p] = cs if p == 0 else pltpu.roll(cs, LANES - 2 * p, 1)
            cq_ref[p, 0] = cst[2 * p:2 * p + 8, :]

    return pl.pallas_call(
        body, name="cumsum_fwd", grid=(nt,),
        in_specs=[_row_spec(LANES), _full_spec((1, LANES))],
        out_specs=[pl.BlockSpec((N_HEADS // 2, TILE, LANES), lambda i: (0, i, 0)),
                   pl.BlockSpec((N_HEADS // 2, 1, 8, TILE), lambda i: (0, i, 0, 0))],
        out_shape=[jax.ShapeDtypeStruct((N_HEADS // 2, lp, LANES), F32),
                   jax.ShapeDtypeStruct((N_HEADS // 2, nt, 8, TILE), F32)],
        scratch_shapes=[pltpu.VMEM((1, LANES), F32)],
        compiler_params=_cp(32, ("arbitrary",)),
    )(fl, bf_pad)


def _head_rows(blk, hh):
    r = lax.broadcasted_iota(jnp.int32, blk.shape, 0)
    return jnp.where((r >= hh * HEAD_DIM) & (r < (hh + 1) * HEAD_DIM), blk, jnp.zeros_like(blk))


def _head_cols(blk, hh):
    c = lax.broadcasted_iota(jnp.int32, blk.shape, 1)
    return jnp.where((c >= hh * HEAD_DIM) & (c < (hh + 1) * HEAD_DIM), blk, jnp.zeros_like(blk))


def _score_mask(s, kt, qt):
    kpos = lax.broadcasted_iota(jnp.int32, (TILE, TILE), 0) + kt * TILE
    qpos = lax.broadcasted_iota(jnp.int32, (TILE, TILE), 1) + qt * TILE
    return jnp.where((kpos <= qpos) & (kpos >= PAD), s, NEG)


def _attn_fwd(qT3, k, vT3, cq4, ck3, nt):
    lp = nt * TILE
    npair = N_HEADS // 2

    def body(qT_ref, k_ref, vT_ref, cq_ref, ck_ref, oT_ref, o_ref, lse_ref, m_s, l_s, acc_s):
        i = pl.program_id(1)
        qblk = qT_ref[0]
        for hh in range(2):
            qm = _head_rows(qblk, hh)
            cq = cq_ref[0, 0, hh:hh + 1, :]
            m_s[...] = jnp.full((1, TILE), NEG, F32)
            l_s[...] = jnp.zeros((1, TILE), F32)
            acc_s[...] = jnp.zeros((HEAD_DIM, TILE), F32)

            def step(j, masked, hh=hh, qm=qm, cq=cq):
                rows = pl.ds(pl.multiple_of(j * TILE, TILE), TILE)
                s = jnp.dot(k_ref[rows, :], qm, preferred_element_type=F32)
                s = s + cq - ck_ref[0, rows, hh:hh + 1]
                if masked:
                    s = _score_mask(s, j, i)
                m_prev = m_s[...]
                m_new = jnp.maximum(m_prev, jnp.max(s, axis=0, keepdims=True))
                a = jnp.exp(m_prev - m_new)
                p = jnp.exp(s - m_new)
                l_s[...] = a * l_s[...] + jnp.sum(p, axis=0, keepdims=True)
                vj = vT_ref[j, hh * HEAD_DIM:(hh + 1) * HEAD_DIM, :]
                acc_s[...] = a * acc_s[...] + jnp.dot(vj, p.astype(BF16), preferred_element_type=F32)
                m_s[...] = m_new

            step(0, True)

            def loop_body(j, carry, step=step):
                step(j, False)
                return carry

            lax.fori_loop(1, i, loop_body, 0)

            @pl.when(i > 0)
            def _(step=step):
                step(i, True)

            l = l_s[...]
            oT_ref[0, hh * HEAD_DIM:(hh + 1) * HEAD_DIM, :] = acc_s[...] / l
            lse_ref[0, 0, hh:hh + 1, :] = m_s[...] + jnp.log(l)
        o_ref[...] = oT_ref[0].T

    return pl.pallas_call(
        body, name="attn_fwd", grid=(npair, nt),
        in_specs=[pl.BlockSpec((1, LANES, TILE), lambda p, i: (i, p, 0)),
                  pl.BlockSpec((lp, LANES), lambda p, i: (0, p)),
                  pl.BlockSpec((nt, LANES, TILE), lambda p, i: (0, p, 0)),
                  pl.BlockSpec((1, 1, 8, TILE), lambda p, i: (p, i, 0, 0)),
                  pl.BlockSpec((1, lp, LANES), lambda p, i: (p, 0, 0))],
        out_specs=[pl.BlockSpec((1, LANES, TILE), lambda p, i: (i, p, 0)),
                   pl.BlockSpec((TILE, LANES), lambda p, i: (i, p)),
                   pl.BlockSpec((1, 1, 8, TILE), lambda p, i: (p, i, 0, 0))],
        out_shape=[jax.ShapeDtypeStruct((nt, D_ATTN, TILE), F32),
                   jax.ShapeDtypeStruct((lp, D_ATTN), F32),
                   jax.ShapeDtypeStruct((npair, nt, 8, TILE), F32)],
        scratch_shapes=[pltpu.VMEM((1, TILE), F32), pltpu.VMEM((1, TILE), F32), pltpu.VMEM((HEAD_DIM, TILE), F32)],
        compiler_params=_cp(40, ("arbitrary", "arbitrary")),
    )(qT3, k, vT3, cq4, ck3)


def _attn_bwd(qT3, kT3, k, v, oT3, doT3, lse4, cq4, ck3, nt):
    lp = nt * TILE
    npair = N_HEADS // 2

    def body(qT_ref, kT_ref, k_ref, v_ref, oT_ref, doT_ref, lse_ref, cq_ref, ck_ref,
             dqT_ref, dkT_ref, dvT_ref, dck_ref, dcq_ref, dq_s, dk_s, dv_s, dc_s, dcq_s):
        j = pl.program_id(1)

        @pl.when(j == 0)
        def _():
            dq_s[...] = jnp.zeros_like(dq_s)
            dcq_s[...] = jnp.zeros_like(dcq_s)

        kblk = k_ref[...]
        vblk = v_ref[...]
        for hh in range(2):
            hs = slice(hh * HEAD_DIM, (hh + 1) * HEAD_DIM)
            kTh = kT_ref[0, hs, :]
            vm = _head_cols(vblk, hh)
            ck = ck_ref[0, :, hh:hh + 1]
            dk_s[...] = jnp.zeros_like(dk_s)
            dv_s[...] = jnp.zeros_like(dv_s)
            dc_s[...] = jnp.zeros_like(dc_s)

            def step(i, masked, hh=hh, hs=hs, kTh=kTh, vm=vm, ck=ck):
                qblk = qT_ref[i]
                doblk = doT_ref[i]
                s = jnp.dot(kblk, _head_rows(qblk, hh), preferred_element_type=F32)
                s = s + cq_ref[0, i, hh:hh + 1, :] - ck
                if masked:
                    s = _score_mask(s, j, i)
                p = jnp.exp(s - lse_ref[0, i, hh:hh + 1, :])
                doh = doT_ref[i, hs, :]
                dp = jnp.dot(vm, _head_rows(doblk, hh), preferred_element_type=F32)
                delta = jnp.sum(doh.astype(F32) * oT_ref[i, hs, :], axis=0, keepdims=True)
                ds = p * (dp - delta)
                pb = p.astype(BF16)
                dsb = ds.astype(BF16)
                nt_dims = (((1,), (1,)), ((), ()))
                dv_s[...] += lax.dot_general(doh, pb, nt_dims, preferred_element_type=F32)
                dk_s[...] += lax.dot_general(qT_ref[i, hs, :], dsb, nt_dims, preferred_element_type=F32)
                dq_s[i, hs, :] += jnp.dot(kTh, dsb, preferred_element_type=F32)
                dc_s[...] += ds[:, :LANES] + ds[:, LANES:]
                dcq_s[i, hh:hh + 1, :] += jnp.sum(ds, axis=0, keepdims=True)

            def masked_body(i, carry, step=step):
                step(i, True)
                return carry

            def plain_body(i, carry, step=step):
                step(i, False)
                return carry

            @pl.when(j == 0)
            def _(masked_body=masked_body):
                lax.fori_loop(0, nt, masked_body, 0)

            @pl.when(j > 0)
            def _(step=step, plain_body=plain_body):
                step(j, True)
                lax.fori_loop(j + 1, nt, plain_body, 0)

            dkT_ref[0, hs, :] = dk_s[...].astype(BF16)
            dvT_ref[0, hs, :] = dv_s[...].astype(BF16)
            colsum = -jnp.sum(dc_s[...], axis=1, keepdims=True)
            lane = lax.broadcasted_iota(jnp.int32, (TILE, LANES), 1)
            contrib = jnp.where(lane == hh, colsum, 0.0)
            if hh == 0:
                dck_ref[0] = contrib
            else:
                dck_ref[0] += contrib

        @pl.when(j == nt - 1)
        def _():
            dqT_ref[...] = (dq_s[...] * SCALE).astype(BF16)
            dcq_ref[0] = dcq_s[...]

    blk_t = pl.BlockSpec((nt, LANES, TILE), lambda p, j: (0, p, 0))
    blk_1 = pl.BlockSpec((1, LANES, TILE), lambda p, j: (j, p, 0))
    blk_rm = pl.BlockSpec((TILE, LANES), lambda p, j: (j, p))
    blk_st = pl.BlockSpec((1, nt, 8, TILE), lambda p, j: (p, 0, 0, 0))
    t3 = jax.ShapeDtypeStruct((nt, D_ATTN, TILE), BF16)
    return pl.pallas_call(
        body, name="attn_bwd", grid=(npair, nt),
        in_specs=[blk_t, blk_1, blk_rm, blk_rm, blk_t, blk_t, blk_st, blk_st,
                  pl.BlockSpec((1, TILE, LANES), lambda p, j: (p, j, 0))],
        out_specs=[blk_t, blk_1, blk_1, pl.BlockSpec((1, TILE, LANES), lambda p, j: (p, j, 0)), blk_st],
        out_shape=[t3, t3, t3, jax.ShapeDtypeStruct((npair, lp, LANES), F32),
                   jax.ShapeDtypeStruct((npair, nt, 8, TILE), F32)],
        scratch_shapes=[pltpu.VMEM((nt, LANES, TILE), F32), pltpu.VMEM((HEAD_DIM, TILE), F32),
                        pltpu.VMEM((HEAD_DIM, TILE), F32), pltpu.VMEM((TILE, LANES), F32),
                        pltpu.VMEM((nt, 8, TILE), F32)],
        compiler_params=_cp(48, ("arbitrary", "arbitrary")),
    )(qT3, kT3, k, v, oT3, doT3, lse4, cq4, ck3)


def _glu(u, ug, i):
    return jnp.where(_row_mask(i, u.shape), u * _sigmoid(ug), 0.0)


def _conv_fwd(u, ug, conv_w, conv_b, g, b, w_pw, nt):
    lp = nt * TILE

    def body(u_ref, ug_ref, up_ref, ugp_ref, w_ref, cb_ref, g_ref, b_ref, wpw_ref,
             co_ref, hc_ref, pw_ref, ext):
        i = pl.program_id(0)
        prev = _glu(up_ref[...], ugp_ref[...], i - 1)
        ext[0:HALO, :] = jnp.where(i > 0, prev[TILE - HALO:, :], 0.0)
        ext[HALO:, :] = _glu(u_ref[...], ug_ref[...], i)
        acc = jnp.zeros((TILE, D_CONV), F32) + cb_ref[...]
        for t in range(CONV_WIDTH):
            off = HALO - (CONV_WIDTH - 1) + t
            acc = acc + w_ref[t:t + 1, :] * ext[off:off + TILE, :]
        co_ref[...] = acc
        xhat, _ = _ln_stats(acc)
        a, _ = _silu_and_grad(xhat * g_ref[...] + b_ref[...])
        hc = a.astype(BF16)
        hc_ref[...] = hc
        pw_ref[...] = jnp.dot(hc, wpw_ref[...], preferred_element_type=F32)

    rm = lambda dt: jax.ShapeDtypeStruct((lp, D_CONV), dt)
    return pl.pallas_call(
        body, name="conv_fwd", grid=(nt,),
        in_specs=[_row_spec(512), _row_spec(512), _row_spec(512, shift=True), _row_spec(512, shift=True),
                  _full_spec((32, 512)), _full_spec((1, 512)), _full_spec((1, 512)), _full_spec((1, 512)),
                  _full_spec((512, 512))],
        out_specs=[_row_spec(512), _row_spec(512), _row_spec(512)],
        out_shape=[rm(F32), rm(BF16), rm(F32)],
        scratch_shapes=[pltpu.VMEM((TILE + HALO, D_CONV), F32)],
        compiler_params=_cp(32, ("arbitrary",)),
    )(u, ug, u, ug, conv_w, conv_b, g, b, w_pw)


def _out_fwd(o, ga, pw, gc, x, metapad, g_in, b_in, w_out, g_out, b_out, target, nt):
    lp = nt * TILE

    def body(o_ref, ga_ref, pw_ref, gc_ref, x_ref, mp_ref, gi_ref, bi_ref, wo_ref, go_ref, bo_ref, t_ref,
             y_ref, dz_ref, loss_ref, dgo_ref, dbo_ref):
        i = pl.program_id(0)

        @pl.when(i == 0)
        def _():
            loss_ref[...] = jnp.zeros_like(loss_ref)
            dgo_ref[...] = jnp.zeros_like(dgo_ref)
            dbo_ref[...] = jnp.zeros_like(dbo_ref)

        x0 = jnp.where(i == 0, mp_ref[...], x_ref[...])
        xhat, _ = _ln_stats(x0)
        h = xhat * gi_ref[...] + bi_ref[...]
        ya, _ = _silu_and_grad(ga_ref[...])
        yc, _ = _silu_and_grad(gc_ref[...])
        ya = (o_ref[...] * ya).astype(BF16)
        yc = (pw_ref[...] * yc).astype(BF16)
        y_ref[:, :D_ATTN] = ya
        y_ref[:, D_ATTN:] = yc
        z = ALPHA * h + jnp.dot(ya, wo_ref[:D_ATTN, :], preferred_element_type=F32) \
            + jnp.dot(yc, wo_ref[D_ATTN:, :], preferred_element_type=F32)
        zhat, rstd = _ln_stats(z)
        out = zhat * go_ref[...] + bo_ref[...]
        live = (i > 0).astype(F32)
        err = (out - t_ref[...]) * live
        dout = err * (1.0 / D_MODEL)
        loss_ref[...] += 0.5 * jnp.sum(jnp.sum(err * dout, axis=0, keepdims=True), axis=1, keepdims=True)
        dgo_ref[...] += jnp.sum(dout * zhat, axis=0, keepdims=True)
        dbo_ref[...] += jnp.sum(dout, axis=0, keepdims=True)
        dz_ref[...] = _ln_bwd(dout, zhat, rstd, go_ref[...])

    return pl.pallas_call(
        body, name="out_fwd", grid=(nt,),
        in_specs=[_row_spec(512), _row_spec(512), _row_spec(512), _row_spec(512),
                  _row_spec(D_MODEL, shift=True), _full_spec((TILE, D_MODEL)), _full_spec((1, D_MODEL)),
                  _full_spec((1, D_MODEL)), _full_spec((D_MODEL, D_MODEL)), _full_spec((1, D_MODEL)),
                  _full_spec((1, D_MODEL)), _row_spec(D_MODEL, shift=True)],
        out_specs=[_row_spec(D_MODEL), _row_spec(D_MODEL), _full_spec((1, LANES)), _full_spec((1, D_MODEL)),
                   _full_spec((1, D_MODEL))],
        out_shape=[jax.ShapeDtypeStruct((lp, D_MODEL), BF16), jax.ShapeDtypeStruct((lp, D_MODEL), F32),
                   jax.ShapeDtypeStruct((1, LANES), F32), jax.ShapeDtypeStruct((1, D_MODEL), F32),
                   jax.ShapeDtypeStruct((1, D_MODEL), F32)],
        compiler_params=_cp(40, ("arbitrary",)),
    )(o, ga, pw, gc, x, metapad, g_in, b_in, w_out, g_out, b_out, target)


def _out_bwd(dz, y, o, ga, pw, gc, w_out, nt):
    lp = nt * TILE

    def body(dz_ref, y_ref, o_ref, ga_ref, pw_ref, gc_ref, wo_ref,
             doT_ref, dga_ref, dpw_ref, dgc_ref, dwo_ref):
        i = pl.program_id(0)

        @pl.when(i == 0)
        def _():
            dwo_ref[...] = jnp.zeros_like(dwo_ref)

        dzb = dz_ref[...].astype(BF16)
        nt_dims = (((1,), (1,)), ((), ()))
        dya = lax.dot_general(dzb, wo_ref[:D_ATTN, :], nt_dims, preferred_element_type=F32)
        dyc = lax.dot_general(dzb, wo_ref[D_ATTN:, :], nt_dims, preferred_element_type=F32)
        sa, sga = _silu_and_grad(ga_ref[...])
        sc, sgc = _silu_and_grad(gc_ref[...])
        doT_ref[0] = (dya * sa).T.astype(BF16)
        dga_ref[...] = (dya * o_ref[...] * sga).astype(BF16)
        dpw_ref[...] = (dyc * sc).astype(BF16)
        dgc_ref[...] = (dyc * pw_ref[...] * sgc).astype(BF16)
        dwo_ref[...] += lax.dot_general(y_ref[...], dzb, (((0,), (0,)), ((), ())), preferred_element_type=F32)

    rm = jax.ShapeDtypeStruct((lp, 512), BF16)
    return pl.pallas_call(
        body, name="out_bwd", grid=(nt,),
        in_specs=[_row_spec(D_MODEL), _row_spec(D_MODEL), _row_spec(512), _row_spec(512), _row_spec(512),
                  _row_spec(512), _full_spec((D_MODEL, D_MODEL))],
        out_specs=[_t3_spec(512), _row_spec(512), _row_spec(512), _row_spec(512), _full_spec((D_MODEL, D_MODEL))],
        out_shape=[jax.ShapeDtypeStruct((nt, 512, TILE), BF16), rm, rm, rm,
                   jax.ShapeDtypeStruct((D_MODEL, D_MODEL), F32)],
        compiler_params=_cp(48, ("arbitrary",)),
    )(dz, y, o, ga, pw, gc, w_out)


def _conv_bwd_ln(dpw, hc, co, w_pw, g, b, nt):
    lp = nt * TILE

    def body(dpw_ref, hc_ref, co_ref, wpw_ref, g_ref, b_ref, dco_ref, dwpw_ref, dg_ref, db_ref, dcb_ref):
        i = pl.program_id(0)

        @pl.when(i == 0)
        def _():
            dwpw_ref[...] = jnp.zeros_like(dwpw_ref)
            dg_ref[...] = jnp.zeros_like(dg_ref)
            db_ref[...] = jnp.zeros_like(db_ref)
            dcb_ref[...] = jnp.zeros_like(dcb_ref)

        dpw_b = dpw_ref[...]
        dhc = lax.dot_general(dpw_b, wpw_ref[...], (((1,), (1,)), ((), ())), preferred_element_type=F32)
        xhat, rstd = _ln_stats(co_ref[...])
        _, sg = _silu_and_grad(xhat * g_ref[...] + b_ref[...])
        dln = dhc * sg
        dg_ref[...] += jnp.sum(dln * xhat, axis=0, keepdims=True)
        db_ref[...] += jnp.sum(dln, axis=0, keepdims=True)
        dco = _ln_bwd(dln, xhat, rstd, g_ref[...])
        dco_ref[...] = dco
        dcb_ref[...] += jnp.sum(dco, axis=0, keepdims=True)
        dwpw_ref[...] += lax.dot_general(hc_ref[...], dpw_b, (((0,), (0,)), ((), ())), preferred_element_type=F32)

    vec = jax.ShapeDtypeStruct((1, D_CONV), F32)
    return pl.pallas_call(
        body, name="conv_bwd_ln", grid=(nt,),
        in_specs=[_row_spec(512), _row_spec(512), _row_spec(512), _full_spec((512, 512)), _full_spec((1, 512)),
                  _full_spec((1, 512))],
        out_specs=[_row_spec(512), _full_spec((512, 512)), _full_spec((1, 512)), _full_spec((1, 512)),
                   _full_spec((1, 512))],
        out_shape=[jax.ShapeDtypeStruct((lp, D_CONV), F32), jax.ShapeDtypeStruct((512, 512), F32), vec, vec, vec],
        compiler_params=_cp(32, ("arbitrary",)),
    )(dpw, hc, co, w_pw, g, b)


def _conv_bwd_taps(dco, u, ug, conv_w, nt):
    lp = nt * TILE

    def body(dco_ref, dcon_ref, u_ref, ug_ref, up_ref, ugp_ref, w_ref, du_ref, dug_ref, dw_ref, ext, dext):
        i = pl.program_id(0)

        @pl.when(i == 0)
        def _():
            dw_ref[...] = jnp.zeros_like(dw_ref)

        prev = _glu(up_ref[...], ugp_ref[...], i - 1)
        ext[0:HALO, :] = jnp.where(i > 0, prev[TILE - HALO:, :], 0.0)
        ext[HALO:, :] = _glu(u_ref[...], ug_ref[...], i)
        dco = dco_ref[...]
        dext[0:TILE, :] = dco
        dext[TILE:, :] = jnp.where(i < nt - 1, dcon_ref[0:HALO, :], 0.0)
        dhg = jnp.zeros((TILE, D_CONV), F32)
        for t in range(CONV_WIDTH):
            off = HALO - (CONV_WIDTH - 1) + t
            dw_ref[t:t + 1, :] += jnp.sum(dco * ext[off:off + TILE, :], axis=0, keepdims=True)
            back = CONV_WIDTH - 1 - t
            dhg = dhg + w_ref[t:t + 1, :] * dext[back:back + TILE, :]
        dhg = jnp.where(_row_mask(i, dhg.shape), dhg, 0.0)
        sg = _sigmoid(ug_ref[...])
        du_ref[...] = (dhg * sg).astype(BF16)
        dug_ref[...] = (dhg * u_ref[...] * sg * (1.0 - sg)).astype(BF16)

    rm = jax.ShapeDtypeStruct((lp, D_CONV), BF16)
    nxt = pl.BlockSpec((TILE, 512), lambda i: (jnp.minimum(i + 1, nt - 1), 0))
    return pl.pallas_call(
        body, name="conv_bwd_taps", grid=(nt,),
        in_specs=[_row_spec(512), nxt, _row_spec(512), _row_spec(512), _row_spec(512, shift=True),
                  _row_spec(512, shift=True), _full_spec((32, 512))],
        out_specs=[_row_spec(512), _row_spec(512), _full_spec((32, 512))],
        out_shape=[rm, rm, jax.ShapeDtypeStruct((32, D_CONV), F32)],
        scratch_shapes=[pltpu.VMEM((TILE + HALO, D_CONV), F32), pltpu.VMEM((TILE + HALO, D_CONV), F32)],
        compiler_params=_cp(32, ("arbitrary",)),
    )(dco, dco, u, ug, u, ug, conv_w)


def _cumsum_bwd(dck, dcq4, fl, bf_pad, nt):
    lp = nt * TILE

    def body(dck_ref, dcq_ref, fl_ref, bf_ref, dfl_ref, dbf_ref, carry):
        i = pl.program_id(0)
        tile = nt - 1 - i

        @pl.when(i == 0)
        def _():
            carry[...] = jnp.zeros_like(carry)
            dbf_ref[...] = jnp.zeros_like(dbf_ref)

        dc = jnp.zeros((TILE, LANES), F32)
        for p in range(N_HEADS // 2):
            dq_rows = jnp.concatenate([dcq_ref[p, 0], jnp.zeros((LANES - 8, TILE), F32)], axis=0)
            both = dck_ref[p] + dq_rows.T
            dc = dc + (both if p == 0 else pltpu.roll(both, 2 * p, 1))
        r = lax.broadcasted_iota(jnp.int32, (TILE, TILE), 0)
        c = lax.broadcasted_iota(jnp.int32, (TILE, TILE), 1)
        triu = (c >= r).astype(F32)
        dlf = jnp.dot(triu, dc, precision=lax.Precision.HIGHEST, preferred_element_type=F32) + carry[...]
        carry[...] = dlf[0:1, :]
        z = fl_ref[...] + bf_ref[...]
        lane = lax.broadcasted_iota(jnp.int32, (TILE, LANES), 1)
        dfl = jnp.where(_row_mask(tile, (TILE, LANES)) & (lane < N_HEADS), dlf * _sigmoid(-z), 0.0)
        dfl_ref[...] = dfl.astype(BF16)
        dbf_ref[...] += jnp.sum(dfl, axis=0, keepdims=True)

    rev = lambda i: (nt - 1 - i, 0)
    return pl.pallas_call(
        body, name="cumsum_bwd", grid=(nt,),
        in_specs=[pl.BlockSpec((N_HEADS // 2, TILE, LANES), lambda i: (0, nt - 1 - i, 0)),
                  pl.BlockSpec((N_HEADS // 2, 1, 8, TILE), lambda i: (0, nt - 1 - i, 0, 0)),
                  pl.BlockSpec((TILE, LANES), rev), _full_spec((1, LANES))],
        out_specs=[pl.BlockSpec((TILE, LANES), rev), _full_spec((1, LANES))],
        out_shape=[jax.ShapeDtypeStruct((lp, LANES), BF16), jax.ShapeDtypeStruct((1, LANES), F32)],
        scratch_shapes=[pltpu.VMEM((1, LANES), F32)],
        compiler_params=_cp(32, ("arbitrary",)),
    )(dck, dcq4, fl, bf_pad)


def _dw_rowmajor(hb, secs, nt):
    n = len(secs)

    def body(*refs):
        hb_ref, sec_refs, out_refs = refs[0], refs[1:1 + n], refs[1 + n:]
        i = pl.program_id(0)

        @pl.when(i == 0)
        def _():
            for o_ref in out_refs:
                o_ref[...] = jnp.zeros_like(o_ref)

        hb_t = hb_ref[...]
        for s_ref, o_ref in zip(sec_refs, out_refs):
            o_ref[...] += lax.dot_general(hb_t, s_ref[...], (((0,), (0,)), ((), ())), preferred_element_type=F32)

    return pl.pallas_call(
        body, name="dw_rowmajor", grid=(nt,),
        in_specs=[_row_spec(D_MODEL)] + [_row_spec(s.shape[1]) for s in secs],
        out_specs=[_full_spec((D_MODEL, s.shape[1])) for s in secs],
        out_shape=[jax.ShapeDtypeStruct((D_MODEL, s.shape[1]), F32) for s in secs],
        compiler_params=_cp(48, ("arbitrary",)),
    )(hb, *secs)


def _dw_transposed(hb, secs_t3, nt):
    n = len(secs_t3)

    def body(*refs):
        hb_ref, sec_refs, out_refs = refs[0], refs[1:1 + n], refs[1 + n:]
        i = pl.program_id(0)

        @pl.when(i == 0)
        def _():
            for o_ref in out_refs:
                o_ref[...] = jnp.zeros_like(o_ref)

        hb_t = hb_ref[...]
        for s_ref, o_ref in zip(sec_refs, out_refs):
            o_ref[...] += jnp.dot(s_ref[0], hb_t, preferred_element_type=F32)

    return pl.pallas_call(
        body, name="dw_transposed", grid=(nt,),
        in_specs=[_row_spec(D_MODEL)] + [_t3_spec(512) for _ in secs_t3],
        out_specs=[_full_spec((512, D_MODEL)) for _ in secs_t3],
        out_shape=[jax.ShapeDtypeStruct((512, D_MODEL), F32) for _ in secs_t3],
        compiler_params=_cp(40, ("arbitrary",)),
    )(hb, *secs_t3)


def _dh_bwd(secs, secs_t3, w_rm, w_t, dz, x, metapad, g_in, nt):
    n, m = len(secs), len(secs_t3)
    offs = np.cumsum([0] + [s.shape[1] for s in secs])

    def body(*refs):
        sec_refs, t3_refs = refs[:n], refs[n:n + m]
        wrm_ref, wt_ref, dz_ref, x_ref, mp_ref, g_ref = refs[n + m:n + m + 6]
        dx_ref, dmeta_ref, dg_ref, db_ref = refs[n + m + 6:]
        i = pl.program_id(0)

        @pl.when(i == 0)
        def _():
            dg_ref[...] = jnp.zeros_like(dg_ref)
            db_ref[...] = jnp.zeros_like(db_ref)

        dh = ALPHA * dz_ref[...]
        for s_ref, lo, hi in zip(sec_refs, offs[:-1], offs[1:]):
            dh = dh + lax.dot_general(s_ref[...], wrm_ref[:, lo:hi], (((1,), (1,)), ((), ())),
                                      preferred_element_type=F32)
        for idx, t_ref in enumerate(t3_refs):
            dh = dh + lax.dot_general(t_ref[0], wt_ref[idx * 512:(idx + 1) * 512, :], (((0,), (0,)), ((), ())),
                                      preferred_element_type=F32)
        x0 = jnp.where(i == 0, mp_ref[...], x_ref[...])
        xhat, rstd = _ln_stats(x0)
        dg_ref[...] += jnp.sum(dh * xhat, axis=0, keepdims=True)
        db_ref[...] += jnp.sum(dh, axis=0, keepdims=True)
        dx = _ln_bwd(dh, xhat, rstd, g_ref[...])
        dx_ref[...] = dx

        @pl.when(i == 0)
        def _():
            dmeta_ref[...] = dx

    seq = (nt - 1) * TILE
    return pl.pallas_call(
        body, name="dh_bwd", grid=(nt,),
        in_specs=[_row_spec(s.shape[1]) for s in secs] + [_t3_spec(512) for _ in secs_t3]
        + [_full_spec(w_rm.shape), _full_spec(w_t.shape), _row_spec(D_MODEL), _row_spec(D_MODEL, shift=True),
           _full_spec((TILE, D_MODEL)), _full_spec((1, D_MODEL))],
        out_specs=[_row_spec(D_MODEL, shift=True), _full_spec((TILE, D_MODEL)), _full_spec((1, D_MODEL)),
                   _full_spec((1, D_MODEL))],
        out_shape=[jax.ShapeDtypeStruct((seq, D_MODEL), F32), jax.ShapeDtypeStruct((TILE, D_MODEL), F32),
                   jax.ShapeDtypeStruct((1, D_MODEL), F32), jax.ShapeDtypeStruct((1, D_MODEL), F32)],
        compiler_params=_cp(56, ("arbitrary",)),
    )(*secs, *secs_t3, w_rm, w_t, dz, x, metapad, g_in)


def _repack_w_in(w_in_full):
    f = w_in_full[:, 1536:1544]
    return jnp.concatenate([w_in_full[:, :1536], w_in_full[:, 1544:], f,
                            jnp.zeros((D_MODEL, LANES - N_HEADS), w_in_full.dtype)], axis=1)


def _unpack_dw_in(dw_rm, dw_t):
    dga, du, dug, dgc, dfl = dw_rm
    dq, dk, dv = (t.T for t in dw_t)
    return jnp.concatenate([dq, dk, dv, dfl[:, :N_HEADS], dga, du, dug, dgc], axis=1)


def _local_step(x, target, meta_full, ln_in_g, ln_in_b, w_in_full, b_f, conv_w_full, conv_b, ln_conv_g,
                ln_conv_b, w_pw_full, w_out_full, ln_out_g, ln_out_b):
    seq = x.shape[0]
    nt = seq // TILE + 1
    row = lambda a: a.reshape(1, -1).astype(F32)
    metapad = jnp.concatenate([jnp.zeros((PAD, D_MODEL), F32), meta_full], axis=0)
    w_r = _repack_w_in(w_in_full)
    bf_pad = jnp.pad(row(b_f), ((0, 0), (0, LANES - N_HEADS)))
    cw = jnp.pad(conv_w_full, ((0, 32 - CONV_WIDTH), (0, 0)))
    g_in, b_in = row(ln_in_g), row(ln_in_b)
    g_cv, b_cv, c_b = row(ln_conv_g), row(ln_conv_b), row(conv_b)
    g_out, b_out = row(ln_out_g), row(ln_out_b)

    hb, qT3, kT3, vT3, k, v, ga, u, ug, gc, fl = _proj_fwd(x, metapad, g_in, b_in, w_r, nt)
    ck3, cq4 = _cumsum_fwd(fl, bf_pad, nt)
    oT3, o, lse4 = _attn_fwd(qT3, k, vT3, cq4, ck3, nt)
    co, hc, pw = _conv_fwd(u, ug, cw, c_b, g_cv, b_cv, w_pw_full, nt)
    y, dz, loss, dg_out, db_out = _out_fwd(o, ga, pw, gc, x, metapad, g_in, b_in, w_out_full, g_out, b_out,
                                            target, nt)
    doT3, dga, dpw, dgc, dw_out = _out_bwd(dz, y, o, ga, pw, gc, w_out_full, nt)
    dco, dw_pw, dg_cv, db_cv, dc_b = _conv_bwd_ln(dpw, hc, co, w_pw_full, g_cv, b_cv, nt)
    du, dug, dcw = _conv_bwd_taps(dco, u, ug, cw, nt)
    dqT3, dkT3, dvT3, dck, dcq4 = _attn_bwd(qT3, kT3, k, v, oT3, doT3, lse4, cq4, ck3, nt)
    dfl, dbf = _cumsum_bwd(dck, dcq4, fl, bf_pad, nt)
    secs = (dga, du, dug, dgc, dfl)
    secs_t3 = (dqT3, dkT3, dvT3)
    dw_rm = _dw_rowmajor(hb, secs, nt)
    dw_t = _dw_transposed(hb, secs_t3, nt)
    w_t = w_r[:, :1536].T
    grad_x, dmetapad, dg_in, db_in = _dh_bwd(secs, secs_t3, w_r[:, 1536:], w_t, dz, x, metapad, g_in, nt)
    grads = dict(
        meta=dmetapad[PAD:], ln_in_g=dg_in[0], ln_in_b=db_in[0], w_in=_unpack_dw_in(dw_rm, dw_t),
        b_f=dbf[:, :N_HEADS], conv_w=dcw[:CONV_WIDTH], conv_b=dc_b, ln_conv_g=dg_cv, ln_conv_b=db_cv,
        w_pw=dw_pw, w_out=dw_out, ln_out_g=dg_out, ln_out_b=db_out)
    return loss[0, 0], grad_x, grads


MESH = pl.DeviceIdType.MESH
ANY = pl.BlockSpec(memory_space=pl.ANY)


def _mesh_pos():
    return lax.axis_index("x"), lax.axis_index("y"), lax.axis_index("c")


def _all_gather(blk, name):
    rows, cols = blk.shape

    def body(x_ref, out_ref, send_sems, recv_sems, local_sem):
        x, y, c = _mesh_pos()
        me, sibling = (x, y, c), (x, y, 1 - c)
        chips = [(1 - x, y), (x, 1 - y), (1 - x, 1 - y)]

        def slot(px, py, pc):
            return out_ref.at[4 * px + 2 * py + pc]

        def copy(k, block, to, src=None):
            return pltpu.make_async_remote_copy(
                src_ref=slot(*block) if src is None else src, dst_ref=slot(*block),
                send_sem=send_sems.at[k], recv_sem=recv_sems.at[k], device_id=to, device_id_type=MESH)

        mine = pltpu.make_async_copy(x_ref, slot(*me), local_sem)
        mine.start()
        first = [copy(0, me, sibling, src=x_ref)]
        first += [copy(1 + j, me, (*chip, c), src=x_ref) for j, chip in enumerate(chips)]
        for cp in first:
            cp.start()
        passed = [copy(4 + j, (*chip, c), sibling) for j, chip in enumerate(chips)]
        for j, chip in enumerate(chips):
            copy(1 + j, (*chip, c), me).wait_recv()
            passed[j].start()
        copy(0, sibling, me).wait_recv()
        for j, chip in enumerate(chips):
            copy(4 + j, (*chip, 1 - c), me).wait_recv()
        for cp in first + passed:
            cp.wait_send()
        mine.wait()

    return pl.pallas_call(
        body, name=name, out_shape=jax.ShapeDtypeStruct((N_DEV, rows, cols), blk.dtype),
        in_specs=[ANY], out_specs=ANY,
        scratch_shapes=[pltpu.SemaphoreType.DMA((7,)), pltpu.SemaphoreType.DMA((7,)), pltpu.SemaphoreType.DMA],
    )(blk)


def _exchange_sibling(g8):
    _, r, cc = g8.shape

    def body(g_ref, out_ref, send_sems, recv_sems):
        x, y, c = _mesh_pos()
        cps = [pltpu.make_async_remote_copy(
            src_ref=g_ref.at[2 * q + (1 - c)], dst_ref=out_ref.at[q], send_sem=send_sems.at[q],
            recv_sem=recv_sems.at[q], device_id=(x, y, 1 - c), device_id_type=MESH) for q in range(4)]
        for cp in cps:
            cp.start()
        for cp in cps:
            cp.wait()

    return pl.pallas_call(
        body, name="rs_sibling", out_shape=jax.ShapeDtypeStruct((4, r, cc), g8.dtype),
        in_specs=[ANY], out_specs=ANY,
        scratch_shapes=[pltpu.SemaphoreType.DMA((4,)), pltpu.SemaphoreType.DMA((4,))],
    )(g8)


def _exchange_chips(p4):
    _, r, cc = p4.shape

    def body(p_ref, out_ref, send_sems, recv_sems):
        x, y, c = _mesh_pos()
        chips = [(1 - x, y), (x, 1 - y), (1 - x, 1 - y)]
        cps = [pltpu.make_async_remote_copy(
            src_ref=p_ref.at[2 * cx + cy], dst_ref=out_ref.at[k], send_sem=send_sems.at[k],
            recv_sem=recv_sems.at[k], device_id=(cx, cy, c), device_id_type=MESH)
            for k, (cx, cy) in enumerate(chips)]
        for cp in cps:
            cp.start()
        for cp in cps:
            cp.wait()

    return pl.pallas_call(
        body, name="rs_chips", out_shape=jax.ShapeDtypeStruct((3, r, cc), p4.dtype),
        in_specs=[ANY], out_specs=ANY,
        scratch_shapes=[pltpu.SemaphoreType.DMA((3,)), pltpu.SemaphoreType.DMA((3,))],
    )(p4)


def _rs_add_sibling(g8, recv, c_idx):
    _, r, cc = g8.shape

    def body(s_ref, g_ref, r_ref, p32_ref, pb_ref):
        p = g_ref[0] + r_ref[0].astype(F32)
        p32_ref[0] = p
        pb_ref[0] = p.astype(BF16)

    grid_spec = pltpu.PrefetchScalarGridSpec(
        num_scalar_prefetch=1, grid=(4,),
        in_specs=[pl.BlockSpec((1, r, cc), lambda q, s: (2 * q + s[0], 0, 0)),
                  pl.BlockSpec((1, r, cc), lambda q, s: (q, 0, 0))],
        out_specs=[pl.BlockSpec((1, r, cc), lambda q, s: (q, 0, 0)), pl.BlockSpec((1, r, cc), lambda q, s: (q, 0, 0))])
    return pl.pallas_call(
        body, name="rs_add_sibling", grid_spec=grid_spec,
        out_shape=[jax.ShapeDtypeStruct((4, r, cc), F32), jax.ShapeDtypeStruct((4, r, cc), BF16)],
        compiler_params=_cp(40, ("arbitrary",)),
    )(c_idx, g8, recv)


def _rs_add_chips(p32, recv, q_idx):
    _, r, cc = p32.shape

    def body(s_ref, p_ref, r_ref, out_ref):
        out_ref[...] = ((p_ref[0] + r_ref[0].astype(F32)) + r_ref[1].astype(F32)) + r_ref[2].astype(F32)

    grid_spec = pltpu.PrefetchScalarGridSpec(
        num_scalar_prefetch=1, grid=(1,),
        in_specs=[pl.BlockSpec((1, r, cc), lambda i, s: (s[0], 0, 0)), pl.BlockSpec((3, r, cc), lambda i, s: (0, 0, 0))],
        out_specs=pl.BlockSpec((r, cc), lambda i, s: (0, 0)))
    return pl.pallas_call(
        body, name="rs_add_chips", grid_spec=grid_spec, out_shape=jax.ShapeDtypeStruct((r, cc), F32),
        compiler_params=_cp(40, ("arbitrary",)),
    )(q_idx, p32, recv)


def _sum_devices(a):
    _, r, cc = a.shape

    def body(a_ref, out_ref):
        acc = a_ref[0]
        for d in range(1, N_DEV):
            acc = acc + a_ref[d]
        out_ref[...] = acc

    return pl.pallas_call(body, name="sum_devices", out_shape=jax.ShapeDtypeStruct((r, cc), F32))(a)


def _adamw(ws, gs, ms, vs):
    n = len(ws)
    c1 = 1.0 - ADAM_B1 ** ADAM_STEP
    c2 = 1.0 - ADAM_B2 ** ADAM_STEP

    def body(*refs):
        w_refs, g_refs, m_refs, v_refs = (refs[k * n:(k + 1) * n] for k in range(4))
        d_refs, nm_refs, nv_refs = (refs[(4 + k) * n:(5 + k) * n] for k in range(3))
        for w_ref, g_ref, m_ref, v_ref, d_ref, nm_ref, nv_ref in zip(w_refs, g_refs, m_refs, v_refs, d_refs,
                                                                     nm_refs, nv_refs):
            g = g_ref[...]
            m = ADAM_B1 * m_ref[...] + (1.0 - ADAM_B1) * g
            v = ADAM_B2 * v_ref[...] + (1.0 - ADAM_B2) * (g * g)
            nm_ref[...] = m
            nv_ref[...] = v
            d_ref[...] = -ADAM_LR * ((m / c1) / (jnp.sqrt(v / c2) + ADAM_EPS) + ADAM_WD * w_ref[...])

    shapes = [jax.ShapeDtypeStruct(w.shape, F32) for w in ws]
    outs = pl.pallas_call(body, name="adamw", out_shape=shapes * 3, compiler_params=_cp(48))(*ws, *gs, *ms, *vs)
    return outs[:n], outs[n:2 * n], outs[2 * n:]


PACK_COLS = 256
W_NAMES = ("meta", "ln_in_g", "ln_in_b", "w_in", "b_f", "conv_w", "conv_b", "ln_conv_g", "ln_conv_b", "w_pw",
           "w_out", "ln_out_g", "ln_out_b")


def _bf16_words(a):
    r, cc = a.shape
    return lax.bitcast_convert_type(a.astype(BF16).reshape(r, cc // 2, 2), F32)


def _words_bf16(a):
    b = lax.bitcast_convert_type(a, BF16)
    return b.reshape(*a.shape[:-1], a.shape[-1] * 2)


def kernel(x, meta, ln_in_g, ln_in_b, w_in, b_f, conv_w, conv_b, ln_conv_g, ln_conv_b, w_pw, w_out, ln_out_g, ln_out_b, loss_target, m_meta, m_ln_in_g, m_ln_in_b, m_w_in, m_b_f, m_conv_w, m_conv_b, m_ln_conv_g, m_ln_conv_b, m_w_pw, m_w_out, m_ln_out_g, m_ln_out_b, v_meta, v_ln_in_g, v_ln_in_b, v_w_in, v_b_f, v_conv_w, v_conv_b, v_ln_conv_g, v_ln_conv_b, v_w_pw, v_w_out, v_ln_out_g, v_ln_out_b):
    mx, my, mc = _mesh_pos()
    me = 4 * mx + 2 * my + mc
    n_meta_sh = D_MODEL // N_DEV
    n_cw_sh = D_CONV // N_DEV
    n_out_sh = D_MODEL // N_DEV
    n_pw_sh = D_CONV // N_DEV

    pk = jnp.concatenate([
        _bf16_words(jnp.pad(w_in[0], ((0, 0), (0, 512 - SHARD_IN)))),
        _bf16_words(w_out[0].reshape(2 * n_out_sh, 512)),
        _bf16_words(w_pw[0]),
        meta.reshape(N_META // 2, PACK_COLS),
        jnp.pad(conv_w[0], ((0, 1), (0, LANES - n_cw_sh))).reshape(16, PACK_COLS),
    ], axis=0)
    allw = _all_gather(pk, "gather_weights")
    w_in_full = _words_bf16(allw[:, :1024])[:, :, :SHARD_IN].transpose(1, 0, 2).reshape(D_MODEL, D_IN)
    w_out_full = _words_bf16(allw[:, 1024:1280]).reshape(D_MODEL, D_MODEL)
    w_pw_full = _words_bf16(allw[:, 1280:1344]).reshape(D_CONV, D_CONV)
    meta_full = allw[:, 1344:1352].reshape(N_DEV, N_META, n_meta_sh).transpose(1, 0, 2).reshape(N_META, D_MODEL)
    conv_w_full = allw[:, 1352:1368].reshape(N_DEV, 32, LANES)[:, :CONV_WIDTH, :n_cw_sh].transpose(1, 0, 2) \
        .reshape(CONV_WIDTH, D_CONV)

    loss, grad_x, g = _local_step(x[0], loss_target[0], meta_full, ln_in_g, ln_in_b, w_in_full, b_f[0], conv_w_full,
                                  conv_b[0], ln_conv_g[0], ln_conv_b[0], w_pw_full, w_out_full, ln_out_g[0],
                                  ln_out_b[0])

    g8 = jnp.concatenate([
        jnp.pad(g["w_in"].reshape(D_MODEL, N_DEV, SHARD_IN), ((0, 0), (0, 0), (0, 512 - SHARD_IN))).transpose(1, 0, 2),
        g["w_out"].reshape(N_DEV, 2 * n_out_sh, 512),
        g["w_pw"].reshape(N_DEV, n_pw_sh, 512),
    ], axis=1)
    from_sibling = _exchange_sibling(g8.astype(BF16))
    p32, pb = _rs_add_sibling(g8, from_sibling, jnp.reshape(mc, (1,)).astype(jnp.int32))
    from_chips = _exchange_chips(pb)
    g_big = _rs_add_chips(p32, from_chips, jnp.reshape(2 * mx + my, (1,)).astype(jnp.int32))
    g_w_in = g_big[:1024, :SHARD_IN]
    g_w_out = g_big[1024:1280].reshape(n_out_sh, D_MODEL)
    g_w_pw = g_big[1280:1344]

    small = [g["meta"].reshape(-1), jnp.pad(g["conv_w"], ((0, 1), (0, 0))).reshape(-1), g["ln_in_g"], g["ln_in_b"],
             jnp.pad(g["b_f"].reshape(-1), (0, LANES - N_HEADS)), g["conv_b"].reshape(-1), g["ln_conv_g"].reshape(-1),
             g["ln_conv_b"].reshape(-1), g["ln_out_g"].reshape(-1), g["ln_out_b"].reshape(-1),
             jnp.pad(loss.reshape(1), (0, LANES - 1))]
    sizes = [int(s.shape[0]) for s in small]
    total = sum(sizes)
    rows = -(-total // (8 * PACK_COLS)) * 8
    sv = jnp.pad(jnp.concatenate(small), (0, rows * PACK_COLS - total)).reshape(rows, PACK_COLS)
    summed = _sum_devices(_all_gather(sv, "gather_small_grads")).reshape(-1)
    offs = np.cumsum([0] + sizes)
    part = [summed[offs[k]:offs[k + 1]] for k in range(len(small))]
    g_meta_full = part[0].reshape(N_META, D_MODEL)
    g_cw_full = part[1].reshape(32, D_CONV)[:CONV_WIDTH]
    grads = {
        "meta": lax.dynamic_slice_in_dim(g_meta_full, me * n_meta_sh, n_meta_sh, axis=1),
        "ln_in_g": part[2], "ln_in_b": part[3], "w_in": g_w_in[None], "b_f": part[4][:N_HEADS].reshape(1, N_HEADS),
        "conv_w": lax.dynamic_slice_in_dim(g_cw_full, me * n_cw_sh, n_cw_sh, axis=1)[None],
        "conv_b": part[5].reshape(1, D_CONV), "ln_conv_g": part[6].reshape(1, D_CONV),
        "ln_conv_b": part[7].reshape(1, D_CONV), "w_pw": g_w_pw[None], "w_out": g_w_out[None],
        "ln_out_g": part[8].reshape(1, D_MODEL), "ln_out_b": part[9].reshape(1, D_MODEL)}
    loss_all = part[10][0]

    weights = dict(meta=meta, ln_in_g=ln_in_g, ln_in_b=ln_in_b, w_in=w_in, b_f=b_f, conv_w=conv_w, conv_b=conv_b,
                   ln_conv_g=ln_conv_g, ln_conv_b=ln_conv_b, w_pw=w_pw, w_out=w_out, ln_out_g=ln_out_g,
                   ln_out_b=ln_out_b)
    moms = dict(meta=m_meta, ln_in_g=m_ln_in_g, ln_in_b=m_ln_in_b, w_in=m_w_in, b_f=m_b_f, conv_w=m_conv_w,
                conv_b=m_conv_b, ln_conv_g=m_ln_conv_g, ln_conv_b=m_ln_conv_b, w_pw=m_w_pw, w_out=m_w_out,
                ln_out_g=m_ln_out_g, ln_out_b=m_ln_out_b)
    vels = dict(meta=v_meta, ln_in_g=v_ln_in_g, ln_in_b=v_ln_in_b, w_in=v_w_in, b_f=v_b_f, conv_w=v_conv_w,
                conv_b=v_conv_b, ln_conv_g=v_ln_conv_g, ln_conv_b=v_ln_conv_b, w_pw=v_w_pw, w_out=v_w_out,
                ln_out_g=v_ln_out_g, ln_out_b=v_ln_out_b)

    def two_d(a):
        return a.reshape(-1, a.shape[-1])

    deltas, new_m, new_v = _adamw([two_d(weights[n]) for n in W_NAMES], [two_d(grads[n]) for n in W_NAMES],
                                  [two_d(moms[n]) for n in W_NAMES], [two_d(vels[n]) for n in W_NAMES])
    shp = [weights[n].shape for n in W_NAMES]
    return (loss_all, grad_x[None], *[grads[n] for n in W_NAMES],
            *[d.reshape(s) for d, s in zip(deltas, shp)], *[a.reshape(s) for a, s in zip(new_m, shp)],
            *[a.reshape(s) for a, s in zip(new_v, shp)])
```

```python
import jax
import jax.numpy as jnp
import numpy as np
from jax import lax
from jax.experimental import pallas as pl
from jax.experimental.pallas import tpu as pltpu

F32 = jnp.float32
BF16 = jnp.bfloat16

D_MODEL = 1024
D_ATTN = 512
D_CONV = 512
N_HEADS = 8
HEAD_DIM = 64
N_META = 16
CONV_WIDTH = 31
LN_EPS = 1e-5
ALPHA = 2.0 ** 0.25
SCALE = HEAD_DIM ** -0.5
LOG2E = 1.4426950408889634
ADAM_LR, ADAM_B1, ADAM_B2, ADAM_EPS, ADAM_WD, ADAM_STEP = 0.001, 0.9, 0.999, 1e-08, 0.01, 10

N_DEV = 8
D_IN = 3592
SHARD_IN = D_IN // N_DEV
TILE = 256
PAD = TILE - N_META
HALO = 32
NEG = -1e30
LANES = 128
W_COLS = 7 * 512 + LANES
OFF_GA_R, OFF_F_R = 1536, 3584
MIB = 1024 * 1024


def _cp(vmem_mib, sem=None):
    kw = dict(vmem_limit_bytes=vmem_mib * MIB)
    if sem is not None:
        kw["dimension_semantics"] = sem
    return pltpu.CompilerParams(**kw)


def _sigmoid(x):
    return 1.0 / (1.0 + jnp.exp(-x))


def _silu_and_grad(x):
    s = _sigmoid(x)
    return x * s, s * (1.0 + x * (1.0 - s))


def _ln_stats(x):
    mu = jnp.mean(x, axis=-1, keepdims=True)
    xc = x - mu
    var = jnp.mean(xc * xc, axis=-1, keepdims=True)
    rstd = lax.rsqrt(var + LN_EPS)
    return xc * rstd, rstd


def _ln_bwd(dy, xhat, rstd, g):
    dxh = dy * g
    m1 = jnp.mean(dxh, axis=-1, keepdims=True)
    m2 = jnp.mean(dxh * xhat, axis=-1, keepdims=True)
    return rstd * (dxh - m1 - xhat * m2)


def _row_spec(cols, shift=False):
    if shift:
        return pl.BlockSpec((TILE, cols), lambda i: (jnp.maximum(i - 1, 0), 0))
    return pl.BlockSpec((TILE, cols), lambda i: (i, 0))


def _full_spec(shape):
    nd = len(shape)
    return pl.BlockSpec(shape, lambda i: (0,) * nd)


def _t3_spec(ch):
    return pl.BlockSpec((1, ch, TILE), lambda i: (i, 0, 0))


def _proj_fwd(x, metapad, g_in, b_in, w_r, nt):
    lp = nt * TILE

    def body(x_ref, mp_ref, g_ref, b_ref, w_ref, hb_ref, qT_ref, kT_ref, vT_ref, k_ref, v_ref,
             ga_ref, u_ref, ug_ref, gc_ref, fl_ref):
        i = pl.program_id(0)
        x0 = jnp.where(i == 0, mp_ref[...], x_ref[...])
        xhat, _ = _ln_stats(x0)
        hb = (xhat * g_ref[...] + b_ref[...]).astype(BF16)
        hb_ref[...] = hb

        def sec(off, n=512):
            return jnp.dot(hb, w_ref[:, off:off + n], preferred_element_type=F32)

        qT_ref[0] = (sec(0) * (SCALE * LOG2E)).T.astype(BF16)
        k = sec(512)
        kT_ref[0] = k.T.astype(BF16)
        k_ref[...] = k.astype(BF16)
        v = sec(1024)
        vT_ref[0] = v.T.astype(BF16)
        v_ref[...] = v.astype(BF16)
        ga_ref[...] = sec(OFF_GA_R)
        u_ref[...] = sec(OFF_GA_R + 512)
        ug_ref[...] = sec(OFF_GA_R + 1024)
        gc_ref[...] = sec(OFF_GA_R + 1536)
        fl_ref[...] = sec(OFF_F_R, LANES)

    t3 = jax.ShapeDtypeStruct((nt, 512, TILE), BF16)
    rm = lambda dt: jax.ShapeDtypeStruct((lp, 512), dt)
    return pl.pallas_call(
        body, name="proj_fwd", grid=(nt,),
        in_specs=[_row_spec(D_MODEL, shift=True), _full_spec((TILE, D_MODEL)), _full_spec((1, D_MODEL)),
                  _full_spec((1, D_MODEL)), _full_spec((D_MODEL, W_COLS))],
        out_specs=[_row_spec(D_MODEL), _t3_spec(512), _t3_spec(512), _t3_spec(512), _row_spec(512), _row_spec(512),
                   _row_spec(512), _row_spec(512), _row_spec(512), _row_spec(512), _row_spec(LANES)],
        out_shape=[jax.ShapeDtypeStruct((lp, D_MODEL), BF16), t3, t3, t3, rm(BF16), rm(BF16),
                   rm(F32), rm(F32), rm(F32), rm(F32), jax.ShapeDtypeStruct((lp, LANES), F32)],
        compiler_params=_cp(56, ("arbitrary",)),
    )(x, metapad, g_in, b_in, w_r)


def _row_mask(i, shape):
    r = lax.broadcasted_iota(jnp.int32, shape, 0)
    return (r >= PAD) | (i > 0)


def _cumsum_fwd(fl, bf_pad, nt):
    lp = nt * TILE

    def body(fl_ref, bf_ref, kx_ref, carry):
        i = pl.program_id(0)

        @pl.when(i == 0)
        def _():
            carry[...] = jnp.zeros_like(carry)

        z = fl_ref[...] + bf_ref[...]
        lf = jnp.minimum(z, 0.0) - jnp.log(1.0 + jnp.exp(-jnp.abs(z)))
        lane = lax.broadcasted_iota(jnp.int32, (TILE, LANES), 1)
        real = _row_mask(i, (TILE, LANES))
        lf = jnp.where(real & (lane < N_HEADS), lf, 0.0)
        r = lax.broadcasted_iota(jnp.int32, (TILE, TILE), 0)
        c = lax.broadcasted_iota(jnp.int32, (TILE, TILE), 1)
        tril = (c <= r).astype(F32)
        cs = jnp.dot(tril, lf, precision=lax.Precision.HIGHEST, preferred_element_type=F32) + carry[...]
        carry[...] = cs[TILE - 1:TILE, :]
        bias = jnp.where(real, cs * (-LOG2E), NEG)
        hi = bias.astype(BF16).astype(F32)
        mid = (bias - hi).astype(BF16).astype(F32)
        lo = (bias - hi - mid).astype(BF16).astype(F32)
        for p in range(N_HEADS // 2):
            out = jnp.zeros((TILE, LANES), F32)
            for hh in range(2):
                for part, piece in enumerate((hi, mid, lo)):
                    dst, src = 3 * hh + part, 2 * p + hh
                    moved = piece if dst == src else pltpu.roll(piece, (dst - src) % LANES, 1)
                    out = jnp.where(lane == dst, moved, out)
            kx_ref[p] = out.astype(BF16)

    return pl.pallas_call(
        body, name="cumsum_fwd", grid=(nt,),
        in_specs=[_row_spec(LANES), _full_spec((1, LANES))],
        out_specs=pl.BlockSpec((N_HEADS // 2, TILE, LANES), lambda i: (0, i, 0)),
        out_shape=jax.ShapeDtypeStruct((N_HEADS // 2, lp, LANES), BF16),
        scratch_shapes=[pltpu.VMEM((1, LANES), F32)],
        compiler_params=_cp(32, ("arbitrary",)),
    )(fl, bf_pad)


def _head_rows(blk, hh):
    r = lax.broadcasted_iota(jnp.int32, blk.shape, 0)
    return jnp.where((r >= hh * HEAD_DIM) & (r < (hh + 1) * HEAD_DIM), blk, jnp.zeros_like(blk))


def _two_heads(blk):
    return jnp.concatenate([_head_rows(blk, 0), _head_rows(blk, 1)], axis=1)


def _bias_rows():
    r = lax.broadcasted_iota(jnp.int32, (LANES, 2 * TILE), 0)
    c = lax.broadcasted_iota(jnp.int32, (LANES, 2 * TILE), 1)
    return jnp.where(((r < 3) & (c < TILE)) | ((r >= 3) & (r < 6) & (c >= TILE)), 1.0, 0.0).astype(BF16)


def _diag_mask(s):
    kpos = lax.broadcasted_iota(jnp.int32, (TILE, TILE), 0)
    qpos = lax.broadcasted_iota(jnp.int32, (TILE, TILE), 1)
    return jnp.where(kpos <= qpos, s, NEG)


def _pipelined_tiles(last, scores, update):
    scores(0, 0)

    def pair_body(t, carry):
        scores(2 * t + 1, 1)
        update(2 * t, 0, False)
        scores(2 * t + 2, 0)
        update(2 * t + 1, 1, False)
        return carry

    lax.fori_loop(0, last >> 1, pair_body, 0)

    @pl.when((last & 1) == 0)
    def _():
        update(last, 0, True)

    @pl.when((last & 1) == 1)
    def _():
        scores(last, 1)
        update(last - 1, 0, False)
        update(last, 1, True)


def _attn_fwd(qT3, k, kx3, vT3, nt):
    lp = nt * TILE
    npair = N_HEADS // 2

    def body(qT_ref, k_ref, kx_ref, vT_ref, oT_ref, o_ref, lse_ref, sbuf, m_s, l_s, acc_s):
        i = pl.program_id(1)
        qcat = jnp.concatenate([_two_heads(qT_ref[0]), _bias_rows()], axis=0)
        m_s[...] = jnp.full(m_s.shape, NEG, F32)
        l_s[...] = jnp.zeros(l_s.shape, F32)
        acc_s[...] = jnp.zeros(acc_s.shape, F32)

        def scores(j, slot):
            rows = pl.ds(pl.multiple_of(j * TILE, TILE), TILE)
            kext = jnp.concatenate([k_ref[rows, :], kx_ref[0, rows, :]], axis=1)
            sbuf[slot] = jnp.dot(kext, qcat, preferred_element_type=F32)

        def update(j, slot, diag):
            for hh in range(2):
                s = sbuf[slot, :, hh * TILE:(hh + 1) * TILE]
                if diag:
                    s = _diag_mask(s)
                m_prev = m_s[hh]
                m_new = jnp.maximum(m_prev, jnp.max(s, axis=0, keepdims=True))
                a = jnp.exp2(m_prev - m_new)
                p = jnp.exp2(s - m_new)
                l_s[hh] = a * l_s[hh] + jnp.sum(p, axis=0, keepdims=True)
                vj = vT_ref[j, hh * HEAD_DIM:(hh + 1) * HEAD_DIM, :]
                acc_s[hh] = a * acc_s[hh] + jnp.dot(vj, p.astype(BF16), preferred_element_type=F32)
                m_s[hh] = m_new

        _pipelined_tiles(i, scores, update)

        for hh in range(2):
            l = l_s[hh]
            oT_ref[0, hh * HEAD_DIM:(hh + 1) * HEAD_DIM, :] = acc_s[hh] / l
            lse_ref[0, 0, hh:hh + 1, :] = m_s[hh] + jnp.log(l) * LOG2E
        o_ref[...] = oT_ref[0].T

    return pl.pallas_call(
        body, name="attn_fwd", grid=(npair, nt),
        in_specs=[pl.BlockSpec((1, LANES, TILE), lambda p, i: (i, p, 0)),
                  pl.BlockSpec((lp, LANES), lambda p, i: (0, p)),
                  pl.BlockSpec((1, lp, LANES), lambda p, i: (p, 0, 0)),
                  pl.BlockSpec((nt, LANES, TILE), lambda p, i: (0, p, 0))],
        out_specs=[pl.BlockSpec((1, LANES, TILE), lambda p, i: (i, p, 0)),
                   pl.BlockSpec((TILE, LANES), lambda p, i: (i, p)),
                   pl.BlockSpec((1, 1, 8, TILE), lambda p, i: (p, i, 0, 0))],
        out_shape=[jax.ShapeDtypeStruct((nt, D_ATTN, TILE), F32),
                   jax.ShapeDtypeStruct((lp, D_ATTN), F32),
                   jax.ShapeDtypeStruct((npair, nt, 8, TILE), F32)],
        scratch_shapes=[pltpu.VMEM((2, TILE, 2 * TILE), F32), pltpu.VMEM((2, 1, TILE), F32),
                        pltpu.VMEM((2, 1, TILE), F32), pltpu.VMEM((2, HEAD_DIM, TILE), F32)],
        compiler_params=_cp(40, ("arbitrary", "arbitrary")),
    )(qT3, k, kx3, vT3)


def _attn_bwd(qT3, kT3, k, kx3, v, oT3, doT3, lse4, nt):
    lp = nt * TILE
    npair = N_HEADS // 2

    def body(qT_ref, kT_ref, k_ref, kx_ref, v_ref, oT_ref, doT_ref, lse_ref,
             dqT_ref, dkT_ref, dvT_ref, dck_ref, dcq_ref, sbuf, dpbuf, dq_s, dk_s, dv_s, dc_s, dcq_s):
        j = pl.program_id(1)

        @pl.when(j == 0)
        def _():
            dq_s[...] = jnp.zeros_like(dq_s)
            dcq_s[...] = jnp.zeros_like(dcq_s)

        kext = jnp.concatenate([k_ref[...], kx_ref[0]], axis=1)
        vblk = v_ref[...]
        ones = _bias_rows()
        dk_s[...] = jnp.zeros_like(dk_s)
        dv_s[...] = jnp.zeros_like(dv_s)
        dc_s[...] = jnp.zeros_like(dc_s)
        nt_dims = (((1,), (1,)), ((), ()))

        def scores(m, slot):
            i = nt - 1 - m
            qcat = jnp.concatenate([_two_heads(qT_ref[i]), ones], axis=0)
            sbuf[slot] = jnp.dot(kext, qcat, preferred_element_type=F32)
            dpbuf[slot] = jnp.dot(vblk, _two_heads(doT_ref[i]), preferred_element_type=F32)

        def update(m, slot, diag):
            i = nt - 1 - m
            for hh in range(2):
                hs = slice(hh * HEAD_DIM, (hh + 1) * HEAD_DIM)
                s = sbuf[slot, :, hh * TILE:(hh + 1) * TILE]
                if diag:
                    s = _diag_mask(s)
                p = jnp.exp2(s - lse_ref[0, i, hh:hh + 1, :])
                doh = doT_ref[i, hs, :]
                delta = jnp.sum(doh.astype(F32) * oT_ref[i, hs, :], axis=0, keepdims=True)
                ds = p * (dpbuf[slot, :, hh * TILE:(hh + 1) * TILE] - delta)
                dsb = ds.astype(BF16)
                dv_s[hh] += lax.dot_general(doh, p.astype(BF16), nt_dims, preferred_element_type=F32)
                dk_s[hh] += lax.dot_general(qT_ref[i, hs, :], dsb, nt_dims, preferred_element_type=F32)
                dq_s[i, hs, :] += jnp.dot(kT_ref[0, hs, :], dsb, preferred_element_type=F32)
                dc_s[hh] += ds[:, :LANES] + ds[:, LANES:]
                dcq_s[i, hh:hh + 1, :] += jnp.sum(ds, axis=0, keepdims=True)

        _pipelined_tiles(nt - 1 - j, scores, update)

        lane = lax.broadcasted_iota(jnp.int32, (TILE, LANES), 1)
        dck = jnp.zeros((TILE, LANES), F32)
        for hh in range(2):
            hs = slice(hh * HEAD_DIM, (hh + 1) * HEAD_DIM)
            dkT_ref[0, hs, :] = (dk_s[hh] * (1.0 / LOG2E)).astype(BF16)
            dvT_ref[0, hs, :] = dv_s[hh].astype(BF16)
            dck = jnp.where(lane == hh, -jnp.sum(dc_s[hh], axis=1, keepdims=True), dck)
        dck_ref[0] = dck

        @pl.when(j == nt - 1)
        def _():
            dqT_ref[...] = (dq_s[...] * SCALE).astype(BF16)
            dcq_ref[0] = dcq_s[...]

    blk_t = pl.BlockSpec((nt, LANES, TILE), lambda p, j: (0, p, 0))
    blk_1 = pl.BlockSpec((1, LANES, TILE), lambda p, j: (j, p, 0))
    blk_rm = pl.BlockSpec((TILE, LANES), lambda p, j: (j, p))
    blk_px = pl.BlockSpec((1, TILE, LANES), lambda p, j: (p, j, 0))
    blk_st = pl.BlockSpec((1, nt, 8, TILE), lambda p, j: (p, 0, 0, 0))
    t3 = jax.ShapeDtypeStruct((nt, D_ATTN, TILE), BF16)
    return pl.pallas_call(
        body, name="attn_bwd", grid=(npair, nt),
        in_specs=[blk_t, blk_1, blk_rm, blk_px, blk_rm, blk_t, blk_t, blk_st],
        out_specs=[blk_t, blk_1, blk_1, blk_px, blk_st],
        out_shape=[t3, t3, t3, jax.ShapeDtypeStruct((npair, lp, LANES), F32),
                   jax.ShapeDtypeStruct((npair, nt, 8, TILE), F32)],
        scratch_shapes=[pltpu.VMEM((2, TILE, 2 * TILE), F32), pltpu.VMEM((2, TILE, 2 * TILE), F32),
                        pltpu.VMEM((nt, LANES, TILE), F32), pltpu.VMEM((2, HEAD_DIM, TILE), F32),
                        pltpu.VMEM((2, HEAD_DIM, TILE), F32), pltpu.VMEM((2, TILE, LANES), F32),
                        pltpu.VMEM((nt, 8, TILE), F32)],
        compiler_params=_cp(48, ("arbitrary", "arbitrary")),
    )(qT3, kT3, k, kx3, v, oT3, doT3, lse4)


def _glu(u, ug, i):
    return jnp.where(_row_mask(i, u.shape), u * _sigmoid(ug), 0.0)


def _conv_fwd(u, ug, conv_w, conv_b, g, b, w_pw, nt):
    lp = nt * TILE

    def body(u_ref, ug_ref, up_ref, ugp_ref, w_ref, cb_ref, g_ref, b_ref, wpw_ref,
             co_ref, hc_ref, pw_ref, ext):
        i = pl.program_id(0)
        prev = _glu(up_ref[...], ugp_ref[...], i - 1)
        ext[0:HALO, :] = jnp.where(i > 0, prev[TILE - HALO:, :], 0.0)
        ext[HALO:, :] = _glu(u_ref[...], ug_ref[...], i)
        acc = jnp.zeros((TILE, D_CONV), F32) + cb_ref[...]
        for t in range(CONV_WIDTH):
            off = HALO - (CONV_WIDTH - 1) + t
            acc = acc + w_ref[t:t + 1, :] * ext[off:off + TILE, :]
        co_ref[...] = acc
        xhat, _ = _ln_stats(acc)
        a, _ = _silu_and_grad(xhat * g_ref[...] + b_ref[...])
        hc = a.astype(BF16)
        hc_ref[...] = hc
        pw_ref[...] = jnp.dot(hc, wpw_ref[...], preferred_element_type=F32)

    rm = lambda dt: jax.ShapeDtypeStruct((lp, D_CONV), dt)
    return pl.pallas_call(
        body, name="conv_fwd", grid=(nt,),
        in_specs=[_row_spec(512), _row_spec(512), _row_spec(512, shift=True), _row_spec(512, shift=True),
                  _full_spec((32, 512)), _full_spec((1, 512)), _full_spec((1, 512)), _full_spec((1, 512)),
                  _full_spec((512, 512))],
        out_specs=[_row_spec(512), _row_spec(512), _row_spec(512)],
        out_shape=[rm(F32), rm(BF16), rm(F32)],
        scratch_shapes=[pltpu.VMEM((TILE + HALO, D_CONV), F32)],
        compiler_params=_cp(32, ("arbitrary",)),
    )(u, ug, u, ug, conv_w, conv_b, g, b, w_pw)


def _out_fwd(o, ga, pw, gc, x, metapad, g_in, b_in, w_out, g_out, b_out, target, nt):
    lp = nt * TILE

    def body(o_ref, ga_ref, pw_ref, gc_ref, x_ref, mp_ref, gi_ref, bi_ref, wo_ref, go_ref, bo_ref, t_ref,
             y_ref, dz_ref, loss_ref, dgo_ref, dbo_ref):
        i = pl.program_id(0)

        @pl.when(i == 0)
        def _():
            loss_ref[...] = jnp.zeros_like(loss_ref)
            dgo_ref[...] = jnp.zeros_like(dgo_ref)
            dbo_ref[...] = jnp.zeros_like(dbo_ref)

        x0 = jnp.where(i == 0, mp_ref[...], x_ref[...])
        xhat, _ = _ln_stats(x0)
        h = xhat * gi_ref[...] + bi_ref[...]
        ya, _ = _silu_and_grad(ga_ref[...])
        yc, _ = _silu_and_grad(gc_ref[...])
        ya = (o_ref[...] * ya).astype(BF16)
        yc = (pw_ref[...] * yc).astype(BF16)
        y_ref[:, :D_ATTN] = ya
        y_ref[:, D_ATTN:] = yc
        z = ALPHA * h + jnp.dot(ya, wo_ref[:D_ATTN, :], preferred_element_type=F32) \
            + jnp.dot(yc, wo_ref[D_ATTN:, :], preferred_element_type=F32)
        zhat, rstd = _ln_stats(z)
        out = zhat * go_ref[...] + bo_ref[...]
        live = (i > 0).astype(F32)
        err = (out - t_ref[...]) * live
        dout = err * (1.0 / D_MODEL)
        loss_ref[...] += 0.5 * jnp.sum(jnp.sum(err * dout, axis=0, keepdims=True), axis=1, keepdims=True)
        dgo_ref[...] += jnp.sum(dout * zhat, axis=0, keepdims=True)
        dbo_ref[...] += jnp.sum(dout, axis=0, keepdims=True)
        dz_ref[...] = _ln_bwd(dout, zhat, rstd, go_ref[...])

    return pl.pallas_call(
        body, name="out_fwd", grid=(nt,),
        in_specs=[_row_spec(512), _row_spec(512), _row_spec(512), _row_spec(512),
                  _row_spec(D_MODEL, shift=True), _full_spec((TILE, D_MODEL)), _full_spec((1, D_MODEL)),
                  _full_spec((1, D_MODEL)), _full_spec((D_MODEL, D_MODEL)), _full_spec((1, D_MODEL)),
                  _full_spec((1, D_MODEL)), _row_spec(D_MODEL, shift=True)],
        out_specs=[_row_spec(D_MODEL), _row_spec(D_MODEL), _full_spec((1, LANES)), _full_spec((1, D_MODEL)),
                   _full_spec((1, D_MODEL))],
        out_shape=[jax.ShapeDtypeStruct((lp, D_MODEL), BF16), jax.ShapeDtypeStruct((lp, D_MODEL), F32),
                   jax.ShapeDtypeStruct((1, LANES), F32), jax.ShapeDtypeStruct((1, D_MODEL), F32),
                   jax.ShapeDtypeStruct((1, D_MODEL), F32)],
        compiler_params=_cp(40, ("arbitrary",)),
    )(o, ga, pw, gc, x, metapad, g_in, b_in, w_out, g_out, b_out, target)


def _out_bwd(dz, y, o, ga, pw, gc, w_out, nt):
    lp = nt * TILE

    def body(dz_ref, y_ref, o_ref, ga_ref, pw_ref, gc_ref, wo_ref,
             doT_ref, dga_ref, dpw_ref, dgc_ref, dwo_ref):
        i = pl.program_id(0)

        @pl.when(i == 0)
        def _():
            dwo_ref[...] = jnp.zeros_like(dwo_ref)

        dzb = dz_ref[...].astype(BF16)
        nt_dims = (((1,), (1,)), ((), ()))
        dya = lax.dot_general(dzb, wo_ref[:D_ATTN, :], nt_dims, preferred_element_type=F32)
        dyc = lax.dot_general(dzb, wo_ref[D_ATTN:, :], nt_dims, preferred_element_type=F32)
        sa, sga = _silu_and_grad(ga_ref[...])
        sc, sgc = _silu_and_grad(gc_ref[...])
        doT_ref[0] = (dya * sa).T.astype(BF16)
        dga_ref[...] = (dya * o_ref[...] * sga).astype(BF16)
        dpw_ref[...] = (dyc * sc).astype(BF16)
        dgc_ref[...] = (dyc * pw_ref[...] * sgc).astype(BF16)
        dwo_ref[...] += lax.dot_general(y_ref[...], dzb, (((0,), (0,)), ((), ())), preferred_element_type=F32)

    rm = jax.ShapeDtypeStruct((lp, 512), BF16)
    return pl.pallas_call(
        body, name="out_bwd", grid=(nt,),
        in_specs=[_row_spec(D_MODEL), _row_spec(D_MODEL), _row_spec(512), _row_spec(512), _row_spec(512),
                  _row_spec(512), _full_spec((D_MODEL, D_MODEL))],
        out_specs=[_t3_spec(512), _row_spec(512), _row_spec(512), _row_spec(512), _full_spec((D_MODEL, D_MODEL))],
        out_shape=[jax.ShapeDtypeStruct((nt, 512, TILE), BF16), rm, rm, rm,
                   jax.ShapeDtypeStruct((D_MODEL, D_MODEL), F32)],
        compiler_params=_cp(48, ("arbitrary",)),
    )(dz, y, o, ga, pw, gc, w_out)


def _conv_bwd_ln(dpw, hc, co, w_pw, g, b, nt):
    lp = nt * TILE

    def body(dpw_ref, hc_ref, co_ref, wpw_ref, g_ref, b_ref, dco_ref, dwpw_ref, dg_ref, db_ref, dcb_ref):
        i = pl.program_id(0)

        @pl.when(i == 0)
        def _():
            dwpw_ref[...] = jnp.zeros_like(dwpw_ref)
            dg_ref[...] = jnp.zeros_like(dg_ref)
            db_ref[...] = jnp.zeros_like(db_ref)
            dcb_ref[...] = jnp.zeros_like(dcb_ref)

        dpw_b = dpw_ref[...]
        dhc = lax.dot_general(dpw_b, wpw_ref[...], (((1,), (1,)), ((), ())), preferred_element_type=F32)
        xhat, rstd = _ln_stats(co_ref[...])
        _, sg = _silu_and_grad(xhat * g_ref[...] + b_ref[...])
        dln = dhc * sg
        dg_ref[...] += jnp.sum(dln * xhat, axis=0, keepdims=True)
        db_ref[...] += jnp.sum(dln, axis=0, keepdims=True)
        dco = _ln_bwd(dln, xhat, rstd, g_ref[...])
        dco_ref[...] = dco
        dcb_ref[...] += jnp.sum(dco, axis=0, keepdims=True)
        dwpw_ref[...] += lax.dot_general(hc_ref[...], dpw_b, (((0,), (0,)), ((), ())), preferred_element_type=F32)

    vec = jax.ShapeDtypeStruct((1, D_CONV), F32)
    return pl.pallas_call(
        body, name="conv_bwd_ln", grid=(nt,),
        in_specs=[_row_spec(512), _row_spec(512), _row_spec(512), _full_spec((512, 512)), _full_spec((1, 512)),
                  _full_spec((1, 512))],
        out_specs=[_row_spec(512), _full_spec((512, 512)), _full_spec((1, 512)), _full_spec((1, 512)),
                   _full_spec((1, 512))],
        out_shape=[jax.ShapeDtypeStruct((lp, D_CONV), F32), jax.ShapeDtypeStruct((512, 512), F32), vec, vec, vec],
        compiler_params=_cp(32, ("arbitrary",)),
    )(dpw, hc, co, w_pw, g, b)


def _conv_bwd_taps(dco, u, ug, conv_w, nt):
    lp = nt * TILE

    def body(dco_ref, dcon_ref, u_ref, ug_ref, up_ref, ugp_ref, w_ref, du_ref, dug_ref, dw_ref, ext, dext):
        i = pl.program_id(0)

        @pl.when(i == 0)
        def _():
            dw_ref[...] = jnp.zeros_like(dw_ref)

        prev = _glu(up_ref[...], ugp_ref[...], i - 1)
        ext[0:HALO, :] = jnp.where(i > 0, prev[TILE - HALO:, :], 0.0)
        ext[HALO:, :] = _glu(u_ref[...], ug_ref[...], i)
        dco = dco_ref[...]
        dext[0:TILE, :] = dco
        dext[TILE:, :] = jnp.where(i < nt - 1, dcon_ref[0:HALO, :], 0.0)
        dhg = jnp.zeros((TILE, D_CONV), F32)
        for t in range(CONV_WIDTH):
            off = HALO - (CONV_WIDTH - 1) + t
            dw_ref[t:t + 1, :] += jnp.sum(dco * ext[off:off + TILE, :], axis=0, keepdims=True)
            back = CONV_WIDTH - 1 - t
            dhg = dhg + w_ref[t:t + 1, :] * dext[back:back + TILE, :]
        dhg = jnp.where(_row_mask(i, dhg.shape), dhg, 0.0)
        sg = _sigmoid(ug_ref[...])
        du_ref[...] = (dhg * sg).astype(BF16)
        dug_ref[...] = (dhg * u_ref[...] * sg * (1.0 - sg)).astype(BF16)

    rm = jax.ShapeDtypeStruct((lp, D_CONV), BF16)
    nxt = pl.BlockSpec((TILE, 512), lambda i: (jnp.minimum(i + 1, nt - 1), 0))
    return pl.pallas_call(
        body, name="conv_bwd_taps", grid=(nt,),
        in_specs=[_row_spec(512), nxt, _row_spec(512), _row_spec(512), _row_spec(512, shift=True),
                  _row_spec(512, shift=True), _full_spec((32, 512))],
        out_specs=[_row_spec(512), _row_spec(512), _full_spec((32, 512))],
        out_shape=[rm, rm, jax.ShapeDtypeStruct((32, D_CONV), F32)],
        scratch_shapes=[pltpu.VMEM((TILE + HALO, D_CONV), F32), pltpu.VMEM((TILE + HALO, D_CONV), F32)],
        compiler_params=_cp(32, ("arbitrary",)),
    )(dco, dco, u, ug, u, ug, conv_w)


def _cumsum_bwd(dck, dcq4, fl, bf_pad, nt):
    lp = nt * TILE

    def body(dck_ref, dcq_ref, fl_ref, bf_ref, dfl_ref, dbf_ref, carry):
        i = pl.program_id(0)
        tile = nt - 1 - i

        @pl.when(i == 0)
        def _():
            carry[...] = jnp.zeros_like(carry)
            dbf_ref[...] = jnp.zeros_like(dbf_ref)

        dc = jnp.zeros((TILE, LANES), F32)
        for p in range(N_HEADS // 2):
            dq_rows = jnp.concatenate([dcq_ref[p, 0], jnp.zeros((LANES - 8, TILE), F32)], axis=0)
            both = dck_ref[p] + dq_rows.T
            dc = dc + (both if p == 0 else pltpu.roll(both, 2 * p, 1))
        r = lax.broadcasted_iota(jnp.int32, (TILE, TILE), 0)
        c = lax.broadcasted_iota(jnp.int32, (TILE, TILE), 1)
        triu = (c >= r).astype(F32)
        dlf = jnp.dot(triu, dc, precision=lax.Precision.HIGHEST, preferred_element_type=F32) + carry[...]
        carry[...] = dlf[0:1, :]
        z = fl_ref[...] + bf_ref[...]
        lane = lax.broadcasted_iota(jnp.int32, (TILE, LANES), 1)
        dfl = jnp.where(_row_mask(tile, (TILE, LANES)) & (lane < N_HEADS), dlf * _sigmoid(-z), 0.0)
        dfl_ref[...] = dfl.astype(BF16)
        dbf_ref[...] += jnp.sum(dfl, axis=0, keepdims=True)

    rev = lambda i: (nt - 1 - i, 0)
    return pl.pallas_call(
        body, name="cumsum_bwd", grid=(nt,),
        in_specs=[pl.BlockSpec((N_HEADS // 2, TILE, LANES), lambda i: (0, nt - 1 - i, 0)),
                  pl.BlockSpec((N_HEADS // 2, 1, 8, TILE), lambda i: (0, nt - 1 - i, 0, 0)),
                  pl.BlockSpec((TILE, LANES), rev), _full_spec((1, LANES))],
        out_specs=[pl.BlockSpec((TILE, LANES), rev), _full_spec((1, LANES))],
        out_shape=[jax.ShapeDtypeStruct((lp, LANES), BF16), jax.ShapeDtypeStruct((1, LANES), F32)],
        scratch_shapes=[pltpu.VMEM((1, LANES), F32)],
        compiler_params=_cp(32, ("arbitrary",)),
    )(dck, dcq4, fl, bf_pad)


def _dw_rowmajor(hb, secs, nt):
    n = len(secs)

    def body(*refs):
        hb_ref, sec_refs, out_refs = refs[0], refs[1:1 + n], refs[1 + n:]
        i = pl.program_id(0)

        @pl.when(i == 0)
        def _():
            for o_ref in out_refs:
                o_ref[...] = jnp.zeros_like(o_ref)

        hb_t = hb_ref[...]
        for s_ref, o_ref in zip(sec_refs, out_refs):
            o_ref[...] += lax.dot_general(hb_t, s_ref[...], (((0,), (0,)), ((), ())), preferred_element_type=F32)

    return pl.pallas_call(
        body, name="dw_rowmajor", grid=(nt,),
        in_specs=[_row_spec(D_MODEL)] + [_row_spec(s.shape[1]) for s in secs],
        out_specs=[_full_spec((D_MODEL, s.shape[1])) for s in secs],
        out_shape=[jax.ShapeDtypeStruct((D_MODEL, s.shape[1]), F32) for s in secs],
        compiler_params=_cp(48, ("arbitrary",)),
    )(hb, *secs)


def _dw_transposed(hb, secs_t3, nt):
    n = len(secs_t3)

    def body(*refs):
        hb_ref, sec_refs, out_refs = refs[0], refs[1:1 + n], refs[1 + n:]
        i = pl.program_id(0)

        @pl.when(i == 0)
        def _():
            for o_ref in out_refs:
                o_ref[...] = jnp.zeros_like(o_ref)

        hb_t = hb_ref[...]
        for s_ref, o_ref in zip(sec_refs, out_refs):
            o_ref[...] += jnp.dot(s_ref[0], hb_t, preferred_element_type=F32)

    return pl.pallas_call(
        body, name="dw_transposed", grid=(nt,),
        in_specs=[_row_spec(D_MODEL)] + [_t3_spec(512) for _ in secs_t3],
        out_specs=[_full_spec((512, D_MODEL)) for _ in secs_t3],
        out_shape=[jax.ShapeDtypeStruct((512, D_MODEL), F32) for _ in secs_t3],
        compiler_params=_cp(40, ("arbitrary",)),
    )(hb, *secs_t3)


def _dh_bwd(secs, secs_t3, w_rm, w_t, dz, x, metapad, g_in, nt):
    n, m = len(secs), len(secs_t3)
    offs = np.cumsum([0] + [s.shape[1] for s in secs])

    def body(*refs):
        sec_refs, t3_refs = refs[:n], refs[n:n + m]
        wrm_ref, wt_ref, dz_ref, x_ref, mp_ref, g_ref = refs[n + m:n + m + 6]
        dx_ref, dmeta_ref, dg_ref, db_ref = refs[n + m + 6:]
        i = pl.program_id(0)

        @pl.when(i == 0)
        def _():
            dg_ref[...] = jnp.zeros_like(dg_ref)
            db_ref[...] = jnp.zeros_like(db_ref)

        dh = ALPHA * dz_ref[...]
        for s_ref, lo, hi in zip(sec_refs, offs[:-1], offs[1:]):
            dh = dh + lax.dot_general(s_ref[...], wrm_ref[:, lo:hi], (((1,), (1,)), ((), ())),
                                      preferred_element_type=F32)
        for idx, t_ref in enumerate(t3_refs):
            dh = dh + lax.dot_general(t_ref[0], wt_ref[idx * 512:(idx + 1) * 512, :], (((0,), (0,)), ((), ())),
                                      preferred_element_type=F32)
        x0 = jnp.where(i == 0, mp_ref[...], x_ref[...])
        xhat, rstd = _ln_stats(x0)
        dg_ref[...] += jnp.sum(dh * xhat, axis=0, keepdims=True)
        db_ref[...] += jnp.sum(dh, axis=0, keepdims=True)
        dx = _ln_bwd(dh, xhat, rstd, g_ref[...])
        dx_ref[...] = dx

        @pl.when(i == 0)
        def _():
            dmeta_ref[...] = dx

    seq = (nt - 1) * TILE
    return pl.pallas_call(
        body, name="dh_bwd", grid=(nt,),
        in_specs=[_row_spec(s.shape[1]) for s in secs] + [_t3_spec(512) for _ in secs_t3]
        + [_full_spec(w_rm.shape), _full_spec(w_t.shape), _row_spec(D_MODEL), _row_spec(D_MODEL, shift=True),
           _full_spec((TILE, D_MODEL)), _full_spec((1, D_MODEL))],
        out_specs=[_row_spec(D_MODEL, shift=True), _full_spec((TILE, D_MODEL)), _full_spec((1, D_MODEL)),
                   _full_spec((1, D_MODEL))],
        out_shape=[jax.ShapeDtypeStruct((seq, D_MODEL), F32), jax.ShapeDtypeStruct((TILE, D_MODEL), F32),
                   jax.ShapeDtypeStruct((1, D_MODEL), F32), jax.ShapeDtypeStruct((1, D_MODEL), F32)],
        compiler_params=_cp(56, ("arbitrary",)),
    )(*secs, *secs_t3, w_rm, w_t, dz, x, metapad, g_in)


def _repack_w_in(w_in_full):
    f = w_in_full[:, 1536:1544]
    return jnp.concatenate([w_in_full[:, :1536], w_in_full[:, 1544:], f,
                            jnp.zeros((D_MODEL, LANES - N_HEADS), w_in_full.dtype)], axis=1)


def _unpack_dw_in(dw_rm, dw_t):
    dga, du, dug, dgc, dfl = dw_rm
    dq, dk, dv = (t.T for t in dw_t)
    return jnp.concatenate([dq, dk, dv, dfl[:, :N_HEADS], dga, du, dug, dgc], axis=1)


def _local_step(x, target, meta_full, ln_in_g, ln_in_b, w_in_full, b_f, conv_w_full, conv_b, ln_conv_g,
                ln_conv_b, w_pw_full, w_out_full, ln_out_g, ln_out_b):
    seq = x.shape[0]
    nt = seq // TILE + 1
    row = lambda a: a.reshape(1, -1).astype(F32)
    metapad = jnp.concatenate([jnp.zeros((PAD, D_MODEL), F32), meta_full], axis=0)
    w_r = _repack_w_in(w_in_full)
    bf_pad = jnp.pad(row(b_f), ((0, 0), (0, LANES - N_HEADS)))
    cw = jnp.pad(conv_w_full, ((0, 32 - CONV_WIDTH), (0, 0)))
    g_in, b_in = row(ln_in_g), row(ln_in_b)
    g_cv, b_cv, c_b = row(ln_conv_g), row(ln_conv_b), row(conv_b)
    g_out, b_out = row(ln_out_g), row(ln_out_b)

    hb, qT3, kT3, vT3, k, v, ga, u, ug, gc, fl = _proj_fwd(x, metapad, g_in, b_in, w_r, nt)
    kx3 = _cumsum_fwd(fl, bf_pad, nt)
    oT3, o, lse4 = _attn_fwd(qT3, k, kx3, vT3, nt)
    co, hc, pw = _conv_fwd(u, ug, cw, c_b, g_cv, b_cv, w_pw_full, nt)
    y, dz, loss, dg_out, db_out = _out_fwd(o, ga, pw, gc, x, metapad, g_in, b_in, w_out_full, g_out, b_out,
                                            target, nt)
    doT3, dga, dpw, dgc, dw_out = _out_bwd(dz, y, o, ga, pw, gc, w_out_full, nt)
    dco, dw_pw, dg_cv, db_cv, dc_b = _conv_bwd_ln(dpw, hc, co, w_pw_full, g_cv, b_cv, nt)
    du, dug, dcw = _conv_bwd_taps(dco, u, ug, cw, nt)
    dqT3, dkT3, dvT3, dck, dcq4 = _attn_bwd(qT3, kT3, k, kx3, v, oT3, doT3, lse4, nt)
    dfl, dbf = _cumsum_bwd(dck, dcq4, fl, bf_pad, nt)
    secs = (dga, du, dug, dgc, dfl)
    secs_t3 = (dqT3, dkT3, dvT3)
    dw_rm = _dw_rowmajor(hb, secs, nt)
    dw_t = _dw_transposed(hb, secs_t3, nt)
    w_t = w_r[:, :1536].T
    grad_x, dmetapad, dg_in, db_in = _dh_bwd(secs, secs_t3, w_r[:, 1536:], w_t, dz, x, metapad, g_in, nt)
    grads = dict(
        meta=dmetapad[PAD:], ln_in_g=dg_in[0], ln_in_b=db_in[0], w_in=_unpack_dw_in(dw_rm, dw_t),
        b_f=dbf[:, :N_HEADS], conv_w=dcw[:CONV_WIDTH], conv_b=dc_b, ln_conv_g=dg_cv, ln_conv_b=db_cv,
        w_pw=dw_pw, w_out=dw_out, ln_out_g=dg_out, ln_out_b=db_out)
    return loss[0, 0], grad_x, grads


MESH = pl.DeviceIdType.MESH
ANY = pl.BlockSpec(memory_space=pl.ANY)


def _mesh_pos():
    return lax.axis_index("x"), lax.axis_index("y"), lax.axis_index("c")


def _all_gather(blks, name):
    n = len(blks)

    def body(*refs):
        x_refs, out_refs = refs[:n], refs[n:2 * n]
        send_sems, recv_sems, local_sems = refs[2 * n:]
        x, y, c = _mesh_pos()
        me, sibling = (x, y, c), (x, y, 1 - c)
        chips = [(1 - x, y), (x, 1 - y), (1 - x, 1 - y)]

        def slot(a, px, py, pc):
            return out_refs[a].at[4 * px + 2 * py + pc]

        def copy(a, k, block, to, src=None):
            return pltpu.make_async_remote_copy(
                src_ref=slot(a, *block) if src is None else src, dst_ref=slot(a, *block),
                send_sem=send_sems.at[7 * a + k], recv_sem=recv_sems.at[7 * a + k], device_id=to,
                device_id_type=MESH)

        arrays = range(n)
        mine = [pltpu.make_async_copy(x_refs[a], slot(a, *me), local_sems.at[a]) for a in arrays]
        for cp in mine:
            cp.start()
        first = [copy(a, 0, me, sibling, src=x_refs[a]) for a in arrays]
        first += [copy(a, 1 + j, me, (*chip, c), src=x_refs[a]) for j, chip in enumerate(chips) for a in arrays]
        for cp in first:
            cp.start()
        passed = []
        for j, chip in enumerate(chips):
            for a in arrays:
                copy(a, 1 + j, (*chip, c), me).wait_recv()
                passed.append(copy(a, 4 + j, (*chip, c), sibling))
                passed[-1].start()
        for a in arrays:
            copy(a, 0, sibling, me).wait_recv()
            for j, chip in enumerate(chips):
                copy(a, 4 + j, (*chip, 1 - c), me).wait_recv()
        for cp in first + passed:
            cp.wait_send()
        for cp in mine:
            cp.wait()

    return pl.pallas_call(
        body, name=name, out_shape=[jax.ShapeDtypeStruct((N_DEV, *b.shape), b.dtype) for b in blks],
        in_specs=[ANY] * n, out_specs=[ANY] * n,
        scratch_shapes=[pltpu.SemaphoreType.DMA((7 * n,)), pltpu.SemaphoreType.DMA((7 * n,)),
                        pltpu.SemaphoreType.DMA((n,))],
    )(*blks)


def _exchange_sibling(g8):
    _, r, cc = g8.shape

    def body(g_ref, out_ref, send_sems, recv_sems):
        x, y, c = _mesh_pos()
        cps = [pltpu.make_async_remote_copy(
            src_ref=g_ref.at[2 * q + (1 - c)], dst_ref=out_ref.at[q], send_sem=send_sems.at[q],
            recv_sem=recv_sems.at[q], device_id=(x, y, 1 - c), device_id_type=MESH) for q in range(4)]
        for cp in cps:
            cp.start()
        for cp in cps:
            cp.wait()

    return pl.pallas_call(
        body, name="rs_sibling", out_shape=jax.ShapeDtypeStruct((4, r, cc), g8.dtype),
        in_specs=[ANY], out_specs=ANY,
        scratch_shapes=[pltpu.SemaphoreType.DMA((4,)), pltpu.SemaphoreType.DMA((4,))],
    )(g8)


def _exchange_chips(p4):
    _, r, cc = p4.shape

    def body(p_ref, out_ref, send_sems, recv_sems):
        x, y, c = _mesh_pos()
        chips = [(1 - x, y), (x, 1 - y), (1 - x, 1 - y)]
        cps = [pltpu.make_async_remote_copy(
            src_ref=p_ref.at[2 * cx + cy], dst_ref=out_ref.at[k], send_sem=send_sems.at[k],
            recv_sem=recv_sems.at[k], device_id=(cx, cy, c), device_id_type=MESH)
            for k, (cx, cy) in enumerate(chips)]
        for cp in cps:
            cp.start()
        for cp in cps:
            cp.wait()

    return pl.pallas_call(
        body, name="rs_chips", out_shape=jax.ShapeDtypeStruct((3, r, cc), p4.dtype),
        in_specs=[ANY], out_specs=ANY,
        scratch_shapes=[pltpu.SemaphoreType.DMA((3,)), pltpu.SemaphoreType.DMA((3,))],
    )(p4)


def _rs_add_sibling(g8, recv, c_idx):
    _, r, cc = g8.shape

    def body(s_ref, g_ref, r_ref, p32_ref, pb_ref):
        p = g_ref[0] + r_ref[0].astype(F32)
        p32_ref[0] = p
        pb_ref[0] = p.astype(BF16)

    grid_spec = pltpu.PrefetchScalarGridSpec(
        num_scalar_prefetch=1, grid=(4,),
        in_specs=[pl.BlockSpec((1, r, cc), lambda q, s: (2 * q + s[0], 0, 0)),
                  pl.BlockSpec((1, r, cc), lambda q, s: (q, 0, 0))],
        out_specs=[pl.BlockSpec((1, r, cc), lambda q, s: (q, 0, 0)), pl.BlockSpec((1, r, cc), lambda q, s: (q, 0, 0))])
    return pl.pallas_call(
        body, name="rs_add_sibling", grid_spec=grid_spec,
        out_shape=[jax.ShapeDtypeStruct((4, r, cc), F32), jax.ShapeDtypeStruct((4, r, cc), BF16)],
        compiler_params=_cp(40, ("arbitrary",)),
    )(c_idx, g8, recv)


def _rs_add_chips(p32, recv, q_idx):
    _, r, cc = p32.shape

    def body(s_ref, p_ref, r_ref, out_ref):
        out_ref[...] = ((p_ref[0] + r_ref[0].astype(F32)) + r_ref[1].astype(F32)) + r_ref[2].astype(F32)

    grid_spec = pltpu.PrefetchScalarGridSpec(
        num_scalar_prefetch=1, grid=(1,),
        in_specs=[pl.BlockSpec((1, r, cc), lambda i, s: (s[0], 0, 0)), pl.BlockSpec((3, r, cc), lambda i, s: (0, 0, 0))],
        out_specs=pl.BlockSpec((r, cc), lambda i, s: (0, 0)))
    return pl.pallas_call(
        body, name="rs_add_chips", grid_spec=grid_spec, out_shape=jax.ShapeDtypeStruct((r, cc), F32),
        compiler_params=_cp(40, ("arbitrary",)),
    )(q_idx, p32, recv)


def _sum_devices(a):
    _, r, cc = a.shape

    def body(a_ref, out_ref):
        acc = a_ref[0]
        for d in range(1, N_DEV):
            acc = acc + a_ref[d]
        out_ref[...] = acc

    return pl.pallas_call(body, name="sum_devices", out_shape=jax.ShapeDtypeStruct((r, cc), F32))(a)


def _adamw(ws, gs, ms, vs):
    n = len(ws)
    c1 = 1.0 - ADAM_B1 ** ADAM_STEP
    c2 = 1.0 - ADAM_B2 ** ADAM_STEP

    def body(*refs):
        w_refs, g_refs, m_refs, v_refs = (refs[k * n:(k + 1) * n] for k in range(4))
        d_refs, nm_refs, nv_refs = (refs[(4 + k) * n:(5 + k) * n] for k in range(3))
        for w_ref, g_ref, m_ref, v_ref, d_ref, nm_ref, nv_ref in zip(w_refs, g_refs, m_refs, v_refs, d_refs,
                                                                     nm_refs, nv_refs):
            g = g_ref[...]
            m = ADAM_B1 * m_ref[...] + (1.0 - ADAM_B1) * g
            v = ADAM_B2 * v_ref[...] + (1.0 - ADAM_B2) * (g * g)
            nm_ref[...] = m
            nv_ref[...] = v
            d_ref[...] = -ADAM_LR * ((m / c1) / (jnp.sqrt(v / c2) + ADAM_EPS) + ADAM_WD * w_ref[...])

    shapes = [jax.ShapeDtypeStruct(w.shape, F32) for w in ws]
    outs = pl.pallas_call(body, name="adamw", out_shape=shapes * 3, compiler_params=_cp(48))(*ws, *gs, *ms, *vs)
    return outs[:n], outs[n:2 * n], outs[2 * n:]


W_NAMES = ("meta", "ln_in_g", "ln_in_b", "w_in", "b_f", "conv_w", "conv_b", "ln_conv_g", "ln_conv_b", "w_pw",
           "w_out", "ln_out_g", "ln_out_b")
SMALL_COLS = 256


def kernel(x, meta, ln_in_g, ln_in_b, w_in, b_f, conv_w, conv_b, ln_conv_g, ln_conv_b, w_pw, w_out, ln_out_g, ln_out_b, loss_target, m_meta, m_ln_in_g, m_ln_in_b, m_w_in, m_b_f, m_conv_w, m_conv_b, m_ln_conv_g, m_ln_conv_b, m_w_pw, m_w_out, m_ln_out_g, m_ln_out_b, v_meta, v_ln_in_g, v_ln_in_b, v_w_in, v_b_f, v_conv_w, v_conv_b, v_ln_conv_g, v_ln_conv_b, v_w_pw, v_w_out, v_ln_out_g, v_ln_out_b):
    mx, my, mc = _mesh_pos()
    me = 4 * mx + 2 * my + mc
    n_meta_sh = D_MODEL // N_DEV
    n_cw_sh = D_CONV // N_DEV
    n_out_sh = D_MODEL // N_DEV
    n_pw_sh = D_CONV // N_DEV

    small_w = jnp.concatenate([meta, jnp.pad(conv_w[0], ((0, 1), (0, LANES - n_cw_sh)))], axis=0)
    all_in, all_out, all_pw, all_small = _all_gather(
        [jnp.pad(w_in[0], ((0, 0), (0, 512 - SHARD_IN))).astype(BF16), w_out[0].astype(BF16), w_pw[0].astype(BF16),
         small_w], "gather_weights")
    w_in_full = all_in[:, :, :SHARD_IN].transpose(1, 0, 2).reshape(D_MODEL, D_IN)
    w_out_full = all_out.reshape(D_MODEL, D_MODEL)
    w_pw_full = all_pw.reshape(D_CONV, D_CONV)
    meta_full = all_small[:, :N_META].transpose(1, 0, 2).reshape(N_META, D_MODEL)
    conv_w_full = all_small[:, N_META:N_META + CONV_WIDTH, :n_cw_sh].transpose(1, 0, 2).reshape(CONV_WIDTH, D_CONV)

    loss, grad_x, g = _local_step(x[0], loss_target[0], meta_full, ln_in_g, ln_in_b, w_in_full, b_f[0], conv_w_full,
                                  conv_b[0], ln_conv_g[0], ln_conv_b[0], w_pw_full, w_out_full, ln_out_g[0],
                                  ln_out_b[0])

    g8 = jnp.concatenate([
        jnp.pad(g["w_in"].reshape(D_MODEL, N_DEV, SHARD_IN), ((0, 0), (0, 0), (0, 512 - SHARD_IN))).transpose(1, 0, 2),
        g["w_out"].reshape(N_DEV, 2 * n_out_sh, 512),
        g["w_pw"].reshape(N_DEV, n_pw_sh, 512),
    ], axis=1)
    from_sibling = _exchange_sibling(g8.astype(BF16))
    p32, pb = _rs_add_sibling(g8, from_sibling, jnp.reshape(mc, (1,)).astype(jnp.int32))
    from_chips = _exchange_chips(pb)
    g_big = _rs_add_chips(p32, from_chips, jnp.reshape(2 * mx + my, (1,)).astype(jnp.int32))
    g_w_in = g_big[:1024, :SHARD_IN]
    g_w_out = g_big[1024:1280].reshape(n_out_sh, D_MODEL)
    g_w_pw = g_big[1280:1344]

    small = [g["meta"].reshape(-1), jnp.pad(g["conv_w"], ((0, 1), (0, 0))).reshape(-1), g["ln_in_g"], g["ln_in_b"],
             jnp.pad(g["b_f"].reshape(-1), (0, LANES - N_HEADS)), g["conv_b"].reshape(-1), g["ln_conv_g"].reshape(-1),
             g["ln_conv_b"].reshape(-1), g["ln_out_g"].reshape(-1), g["ln_out_b"].reshape(-1),
             jnp.pad(loss.reshape(1), (0, LANES - 1))]
    sizes = [int(s.shape[0]) for s in small]
    total = sum(sizes)
    rows = -(-total // (8 * SMALL_COLS)) * 8
    sv = jnp.pad(jnp.concatenate(small), (0, rows * SMALL_COLS - total)).reshape(rows, SMALL_COLS)
    summed = _sum_devices(_all_gather([sv], "gather_small_grads")[0]).reshape(-1)
    offs = np.cumsum([0] + sizes)
    part = [summed[offs[k]:offs[k + 1]] for k in range(len(small))]
    g_meta_full = part[0].reshape(N_META, D_MODEL)
    g_cw_full = part[1].reshape(32, D_CONV)[:CONV_WIDTH]
    grads = {
        "meta": lax.dynamic_slice_in_dim(g_meta_full, me * n_meta_sh, n_meta_sh, axis=1),
        "ln_in_g": part[2], "ln_in_b": part[3], "w_in": g_w_in[None], "b_f": part[4][:N_HEADS].reshape(1, N_HEADS),
        "conv_w": lax.dynamic_slice_in_dim(g_cw_full, me * n_cw_sh, n_cw_sh, axis=1)[None],
        "conv_b": part[5].reshape(1, D_CONV), "ln_conv_g": part[6].reshape(1, D_CONV),
        "ln_conv_b": part[7].reshape(1, D_CONV), "w_pw": g_w_pw[None], "w_out": g_w_out[None],
        "ln_out_g": part[8].reshape(1, D_MODEL), "ln_out_b": part[9].reshape(1, D_MODEL)}
    loss_all = part[10][0]

    weights = dict(meta=meta, ln_in_g=ln_in_g, ln_in_b=ln_in_b, w_in=w_in, b_f=b_f, conv_w=conv_w, conv_b=conv_b,
                   ln_conv_g=ln_conv_g, ln_conv_b=ln_conv_b, w_pw=w_pw, w_out=w_out, ln_out_g=ln_out_g,
                   ln_out_b=ln_out_b)
    moms = dict(meta=m_meta, ln_in_g=m_ln_in_g, ln_in_b=m_ln_in_b, w_in=m_w_in, b_f=m_b_f, conv_w=m_conv_w,
                conv_b=m_conv_b, ln_conv_g=m_ln_conv_g, ln_conv_b=m_ln_conv_b, w_pw=m_w_pw, w_out=m_w_out,
                ln_out_g=m_ln_out_g, ln_out_b=m_ln_out_b)
    vels = dict(meta=v_meta, ln_in_g=v_ln_in_g, ln_in_b=v_ln_in_b, w_in=v_w_in, b_f=v_b_f, conv_w=v_conv_w,
                conv_b=v_conv_b, ln_conv_g=v_ln_conv_g, ln_conv_b=v_ln_conv_b, w_pw=v_w_pw, w_out=v_w_out,
                ln_out_g=v_ln_out_g, ln_out_b=v_ln_out_b)

    def two_d(a):
        return a.reshape(-1, a.shape[-1])

    deltas, new_m, new_v = _adamw([two_d(weights[n]) for n in W_NAMES], [two_d(grads[n]) for n in W_NAMES],
                                  [two_d(moms[n]) for n in W_NAMES], [two_d(vels[n]) for n in W_NAMES])
    shp = [weights[n].shape for n in W_NAMES]
    return (loss_all, grad_x[None], *[grads[n] for n in W_NAMES],
            *[d.reshape(s) for d, s in zip(deltas, shp)], *[a.reshape(s) for a, s in zip(new_m, shp)],
            *[a.reshape(s) for a, s in zip(new_v, shp)])
```

```python
import jax
import jax.numpy as jnp
import numpy as np
from jax import lax
from jax.experimental import pallas as pl
from jax.experimental.pallas import tpu as pltpu

F32 = jnp.float32
BF16 = jnp.bfloat16

D_MODEL = 1024
D_ATTN = 512
D_CONV = 512
N_HEADS = 8
HEAD_DIM = 64
N_META = 16
CONV_WIDTH = 31
LN_EPS = 1e-5
ALPHA = 2.0 ** 0.25
SCALE = HEAD_DIM ** -0.5
LOG2E = 1.4426950408889634
ADAM_LR, ADAM_B1, ADAM_B2, ADAM_EPS, ADAM_WD, ADAM_STEP = 0.001, 0.9, 0.999, 1e-08, 0.01, 10

N_DEV = 8
D_IN = 3592
SHARD_IN = D_IN // N_DEV
TILE = 256
PAD = TILE - N_META
HALO = 32
SHIFT_ROWS = TILE + HALO
EXT_ROWS = SHIFT_ROWS + 8
NEG = -1e30
LANES = 128
W_COLS = 7 * 512 + LANES
OFF_GA_R, OFF_F_R = 1536, 3584
MIB = 1024 * 1024


def _cp(vmem_mib, sem=None):
    kw = dict(vmem_limit_bytes=vmem_mib * MIB)
    if sem is not None:
        kw["dimension_semantics"] = sem
    return pltpu.CompilerParams(**kw)


def _sigmoid(x):
    return 1.0 / (1.0 + jnp.exp(-x))


def _silu_and_grad(x):
    s = _sigmoid(x)
    return x * s, s * (1.0 + x * (1.0 - s))


def _ln_stats(x):
    mu = jnp.mean(x, axis=-1, keepdims=True)
    xc = x - mu
    var = jnp.mean(xc * xc, axis=-1, keepdims=True)
    rstd = lax.rsqrt(var + LN_EPS)
    return xc * rstd, rstd


def _ln_bwd(dy, xhat, rstd, g):
    dxh = dy * g
    m1 = jnp.mean(dxh, axis=-1, keepdims=True)
    m2 = jnp.mean(dxh * xhat, axis=-1, keepdims=True)
    return rstd * (dxh - m1 - xhat * m2)


def _row_spec(cols, shift=False):
    if shift:
        return pl.BlockSpec((TILE, cols), lambda i: (jnp.maximum(i - 1, 0), 0))
    return pl.BlockSpec((TILE, cols), lambda i: (i, 0))


def _full_spec(shape):
    nd = len(shape)
    return pl.BlockSpec(shape, lambda i: (0,) * nd)


def _t3_spec(ch):
    return pl.BlockSpec((1, ch, TILE), lambda i: (i, 0, 0))


def _proj_fwd(x, metapad, g_in, b_in, w_r, nt):
    lp = nt * TILE

    def body(x_ref, mp_ref, g_ref, b_ref, w_ref, hb_ref, qT_ref, kT_ref, vT_ref, k_ref, v_ref,
             ga_ref, u_ref, ug_ref, gc_ref, fl_ref):
        i = pl.program_id(0)
        x0 = jnp.where(i == 0, mp_ref[...], x_ref[...])
        xhat, _ = _ln_stats(x0)
        hb = (xhat * g_ref[...] + b_ref[...]).astype(BF16)
        hb_ref[...] = hb

        def sec(off, n=512):
            return jnp.dot(hb, w_ref[:, off:off + n], preferred_element_type=F32)

        qT_ref[0] = (sec(0) * (SCALE * LOG2E)).T.astype(BF16)
        k = sec(512)
        kT_ref[0] = k.T.astype(BF16)
        k_ref[...] = k.astype(BF16)
        v = sec(1024)
        vT_ref[0] = v.T.astype(BF16)
        v_ref[...] = v.astype(BF16)
        ga_ref[...] = sec(OFF_GA_R)
        u_ref[...] = sec(OFF_GA_R + 512)
        ug_ref[...] = sec(OFF_GA_R + 1024)
        gc_ref[...] = sec(OFF_GA_R + 1536)
        fl_ref[...] = sec(OFF_F_R, LANES)

    t3 = jax.ShapeDtypeStruct((nt, 512, TILE), BF16)
    rm = lambda dt: jax.ShapeDtypeStruct((lp, 512), dt)
    return pl.pallas_call(
        body, name="proj_fwd", grid=(nt,),
        in_specs=[_row_spec(D_MODEL, shift=True), _full_spec((TILE, D_MODEL)), _full_spec((1, D_MODEL)),
                  _full_spec((1, D_MODEL)), _full_spec((D_MODEL, W_COLS))],
        out_specs=[_row_spec(D_MODEL), _t3_spec(512), _t3_spec(512), _t3_spec(512), _row_spec(512), _row_spec(512),
                   _row_spec(512), _row_spec(512), _row_spec(512), _row_spec(512), _row_spec(LANES)],
        out_shape=[jax.ShapeDtypeStruct((lp, D_MODEL), BF16), t3, t3, t3, rm(BF16), rm(BF16),
                   rm(F32), rm(F32), rm(F32), rm(F32), jax.ShapeDtypeStruct((lp, LANES), F32)],
        compiler_params=_cp(56, ("arbitrary",)),
    )(x, metapad, g_in, b_in, w_r)


def _row_mask(i, shape):
    r = lax.broadcasted_iota(jnp.int32, shape, 0)
    return (r >= PAD) | (i > 0)


def _cumsum_fwd(fl, bf_pad, nt):
    lp = nt * TILE

    def body(fl_ref, bf_ref, kx_ref, carry):
        i = pl.program_id(0)

        @pl.when(i == 0)
        def _():
            carry[...] = jnp.zeros_like(carry)

        z = fl_ref[...] + bf_ref[...]
        lf = jnp.minimum(z, 0.0) - jnp.log(1.0 + jnp.exp(-jnp.abs(z)))
        lane = lax.broadcasted_iota(jnp.int32, (TILE, LANES), 1)
        real = _row_mask(i, (TILE, LANES))
        lf = jnp.where(real & (lane < N_HEADS), lf, 0.0)
        r = lax.broadcasted_iota(jnp.int32, (TILE, TILE), 0)
        c = lax.broadcasted_iota(jnp.int32, (TILE, TILE), 1)
        tril = (c <= r).astype(F32)
        cs = jnp.dot(tril, lf, precision=lax.Precision.HIGHEST, preferred_element_type=F32) + carry[...]
        carry[...] = cs[TILE - 1:TILE, :]
        bias = jnp.where(real, cs * (-LOG2E), NEG)
        hi = bias.astype(BF16).astype(F32)
        mid = (bias - hi).astype(BF16).astype(F32)
        lo = (bias - hi - mid).astype(BF16).astype(F32)
        for p in range(N_HEADS // 2):
            out = jnp.zeros((TILE, LANES), F32)
            for hh in range(2):
                for part, piece in enumerate((hi, mid, lo)):
                    dst, src = 3 * hh + part, 2 * p + hh
                    moved = piece if dst == src else pltpu.roll(piece, (dst - src) % LANES, 1)
                    out = jnp.where(lane == dst, moved, out)
            kx_ref[p] = out.astype(BF16)

    return pl.pallas_call(
        body, name="cumsum_fwd", grid=(nt,),
        in_specs=[_row_spec(LANES), _full_spec((1, LANES))],
        out_specs=pl.BlockSpec((N_HEADS // 2, TILE, LANES), lambda i: (0, i, 0)),
        out_shape=jax.ShapeDtypeStruct((N_HEADS // 2, lp, LANES), BF16),
        scratch_shapes=[pltpu.VMEM((1, LANES), F32)],
        compiler_params=_cp(32, ("arbitrary",)),
    )(fl, bf_pad)


def _head_rows(blk, hh):
    r = lax.broadcasted_iota(jnp.int32, blk.shape, 0)
    return jnp.where((r >= hh * HEAD_DIM) & (r < (hh + 1) * HEAD_DIM), blk, jnp.zeros_like(blk))


def _two_heads(blk):
    return jnp.concatenate([_head_rows(blk, 0), _head_rows(blk, 1)], axis=1)


def _bias_rows():
    r = lax.broadcasted_iota(jnp.int32, (LANES, 2 * TILE), 0)
    c = lax.broadcasted_iota(jnp.int32, (LANES, 2 * TILE), 1)
    return jnp.where(((r < 3) & (c < TILE)) | ((r >= 3) & (r < 6) & (c >= TILE)), 1.0, 0.0).astype(BF16)


def _diag_mask(s):
    kpos = lax.broadcasted_iota(jnp.int32, (TILE, TILE), 0)
    qpos = lax.broadcasted_iota(jnp.int32, (TILE, TILE), 1)
    return jnp.where(kpos <= qpos, s, NEG)


def _pipelined_tiles(last, scores, update):
    scores(0, 0)

    def pair_body(t, carry):
        scores(2 * t + 1, 1)
        update(2 * t, 0, False)
        scores(2 * t + 2, 0)
        update(2 * t + 1, 1, False)
        return carry

    lax.fori_loop(0, last >> 1, pair_body, 0)

    @pl.when((last & 1) == 0)
    def _():
        update(last, 0, True)

    @pl.when((last & 1) == 1)
    def _():
        scores(last, 1)
        update(last - 1, 0, False)
        update(last, 1, True)


def _attn_fwd(qT3, k, kx3, vT3, nt):
    lp = nt * TILE
    npair = N_HEADS // 2

    def body(qT_ref, k_ref, kx_ref, vT_ref, oT_ref, o_ref, lse_ref, sbuf, m_s, l_s, acc_s):
        i = pl.program_id(1)
        qcat = jnp.concatenate([_two_heads(qT_ref[0]), _bias_rows()], axis=0)
        m_s[...] = jnp.full(m_s.shape, NEG, F32)
        l_s[...] = jnp.zeros(l_s.shape, F32)
        acc_s[...] = jnp.zeros(acc_s.shape, F32)

        def scores(j, slot):
            rows = pl.ds(pl.multiple_of(j * TILE, TILE), TILE)
            kext = jnp.concatenate([k_ref[rows, :], kx_ref[0, rows, :]], axis=1)
            sbuf[slot] = jnp.dot(kext, qcat, preferred_element_type=F32)

        def update(j, slot, diag):
            for hh in range(2):
                s = sbuf[slot, :, hh * TILE:(hh + 1) * TILE]
                if diag:
                    s = _diag_mask(s)
                m_prev = m_s[hh]
                m_new = jnp.maximum(m_prev, jnp.max(s, axis=0, keepdims=True))
                a = jnp.exp2(m_prev - m_new)
                p = jnp.exp2(s - m_new)
                l_s[hh] = a * l_s[hh] + jnp.sum(p, axis=0, keepdims=True)
                vj = vT_ref[j, hh * HEAD_DIM:(hh + 1) * HEAD_DIM, :]
                acc_s[hh] = a * acc_s[hh] + jnp.dot(vj, p.astype(BF16), preferred_element_type=F32)
                m_s[hh] = m_new

        _pipelined_tiles(i, scores, update)

        for hh in range(2):
            l = l_s[hh]
            oT_ref[0, hh * HEAD_DIM:(hh + 1) * HEAD_DIM, :] = acc_s[hh] / l
            lse_ref[0, 0, hh:hh + 1, :] = m_s[hh] + jnp.log(l) * LOG2E
        o_ref[...] = oT_ref[0].T

    return pl.pallas_call(
        body, name="attn_fwd", grid=(npair, nt),
        in_specs=[pl.BlockSpec((1, LANES, TILE), lambda p, i: (i, p, 0)),
                  pl.BlockSpec((lp, LANES), lambda p, i: (0, p)),
                  pl.BlockSpec((1, lp, LANES), lambda p, i: (p, 0, 0)),
                  pl.BlockSpec((nt, LANES, TILE), lambda p, i: (0, p, 0))],
        out_specs=[pl.BlockSpec((1, LANES, TILE), lambda p, i: (i, p, 0)),
                   pl.BlockSpec((TILE, LANES), lambda p, i: (i, p)),
                   pl.BlockSpec((1, 1, 8, TILE), lambda p, i: (p, i, 0, 0))],
        out_shape=[jax.ShapeDtypeStruct((nt, D_ATTN, TILE), F32),
                   jax.ShapeDtypeStruct((lp, D_ATTN), F32),
                   jax.ShapeDtypeStruct((npair, nt, 8, TILE), F32)],
        scratch_shapes=[pltpu.VMEM((2, TILE, 2 * TILE), F32), pltpu.VMEM((2, 1, TILE), F32),
                        pltpu.VMEM((2, 1, TILE), F32), pltpu.VMEM((2, HEAD_DIM, TILE), F32)],
        compiler_params=_cp(40, ("arbitrary", "arbitrary")),
    )(qT3, k, kx3, vT3)


def _attn_bwd(qT3, kT3, k, kx3, v, oT3, doT3, lse4, nt):
    lp = nt * TILE
    npair = N_HEADS // 2

    def body(qT_ref, kT_ref, k_ref, kx_ref, v_ref, oT_ref, doT_ref, lse_ref,
             dqT_ref, dkT_ref, dvT_ref, dck_ref, dcq_ref, sbuf, dpbuf, dq_s, dk_s, dv_s, dc_s, dcq_s):
        j = pl.program_id(1)

        @pl.when(j == 0)
        def _():
            dq_s[...] = jnp.zeros_like(dq_s)
            dcq_s[...] = jnp.zeros_like(dcq_s)

        kext = jnp.concatenate([k_ref[...], kx_ref[0]], axis=1)
        vblk = v_ref[...]
        ones = _bias_rows()
        dk_s[...] = jnp.zeros_like(dk_s)
        dv_s[...] = jnp.zeros_like(dv_s)
        dc_s[...] = jnp.zeros_like(dc_s)
        nt_dims = (((1,), (1,)), ((), ()))

        def scores(m, slot):
            i = nt - 1 - m
            qcat = jnp.concatenate([_two_heads(qT_ref[i]), ones], axis=0)
            sbuf[slot] = jnp.dot(kext, qcat, preferred_element_type=F32)
            dpbuf[slot] = jnp.dot(vblk, _two_heads(doT_ref[i]), preferred_element_type=F32)

        def update(m, slot, diag):
            i = nt - 1 - m
            for hh in range(2):
                hs = slice(hh * HEAD_DIM, (hh + 1) * HEAD_DIM)
                s = sbuf[slot, :, hh * TILE:(hh + 1) * TILE]
                if diag:
                    s = _diag_mask(s)
                p = jnp.exp2(s - lse_ref[0, i, hh:hh + 1, :])
                doh = doT_ref[i, hs, :]
                delta = jnp.sum(doh.astype(F32) * oT_ref[i, hs, :], axis=0, keepdims=True)
                ds = p * (dpbuf[slot, :, hh * TILE:(hh + 1) * TILE] - delta)
                dsb = ds.astype(BF16)
                dv_s[hh] += lax.dot_general(doh, p.astype(BF16), nt_dims, preferred_element_type=F32)
                dk_s[hh] += lax.dot_general(qT_ref[i, hs, :], dsb, nt_dims, preferred_element_type=F32)
                dq_s[i, hs, :] += jnp.dot(kT_ref[0, hs, :], dsb, preferred_element_type=F32)
                dc_s[hh] += ds[:, :LANES] + ds[:, LANES:]
                dcq_s[i, hh:hh + 1, :] += jnp.sum(ds, axis=0, keepdims=True)

        _pipelined_tiles(nt - 1 - j, scores, update)

        lane = lax.broadcasted_iota(jnp.int32, (TILE, LANES), 1)
        dck = jnp.zeros((TILE, LANES), F32)
        for hh in range(2):
            hs = slice(hh * HEAD_DIM, (hh + 1) * HEAD_DIM)
            dkT_ref[0, hs, :] = (dk_s[hh] * (1.0 / LOG2E)).astype(BF16)
            dvT_ref[0, hs, :] = dv_s[hh].astype(BF16)
            dck = jnp.where(lane == hh, -jnp.sum(dc_s[hh], axis=1, keepdims=True), dck)
        dck_ref[0] = dck

        @pl.when(j == nt - 1)
        def _():
            dqT_ref[...] = (dq_s[...] * SCALE).astype(BF16)
            dcq_ref[0] = dcq_s[...]

    blk_t = pl.BlockSpec((nt, LANES, TILE), lambda p, j: (0, p, 0))
    blk_1 = pl.BlockSpec((1, LANES, TILE), lambda p, j: (j, p, 0))
    blk_rm = pl.BlockSpec((TILE, LANES), lambda p, j: (j, p))
    blk_px = pl.BlockSpec((1, TILE, LANES), lambda p, j: (p, j, 0))
    blk_st = pl.BlockSpec((1, nt, 8, TILE), lambda p, j: (p, 0, 0, 0))
    t3 = jax.ShapeDtypeStruct((nt, D_ATTN, TILE), BF16)
    return pl.pallas_call(
        body, name="attn_bwd", grid=(npair, nt),
        in_specs=[blk_t, blk_1, blk_rm, blk_px, blk_rm, blk_t, blk_t, blk_st],
        out_specs=[blk_t, blk_1, blk_1, blk_px, blk_st],
        out_shape=[t3, t3, t3, jax.ShapeDtypeStruct((npair, lp, LANES), F32),
                   jax.ShapeDtypeStruct((npair, nt, 8, TILE), F32)],
        scratch_shapes=[pltpu.VMEM((2, TILE, 2 * TILE), F32), pltpu.VMEM((2, TILE, 2 * TILE), F32),
                        pltpu.VMEM((nt, LANES, TILE), F32), pltpu.VMEM((2, HEAD_DIM, TILE), F32),
                        pltpu.VMEM((2, HEAD_DIM, TILE), F32), pltpu.VMEM((2, TILE, LANES), F32),
                        pltpu.VMEM((nt, 8, TILE), F32)],
        compiler_params=_cp(48, ("arbitrary", "arbitrary")),
    )(qT3, kT3, k, kx3, v, oT3, doT3, lse4)


def _glu(u, ug, i):
    return jnp.where(_row_mask(i, u.shape), u * _sigmoid(ug), 0.0)


def _shifted_copies(dst, src):
    for ph in range(8):
        dst[ph] = src[ph:ph + SHIFT_ROWS, :]


def _tap_window(sh, off, lanes, row0=0, rows=TILE):
    base = (off // 8) * 8 + row0
    return sh[off % 8, base:base + rows, lanes]


def _conv_fwd(u, ug, conv_w, conv_b, g, b, w_pw, nt):
    lp = nt * TILE

    def body(u_ref, ug_ref, up_ref, ugp_ref, w_ref, cb_ref, g_ref, b_ref, wpw_ref,
             co_ref, hc_ref, pw_ref, ext, sh):
        i = pl.program_id(0)
        prev = _glu(up_ref[...], ugp_ref[...], i - 1)
        ext[0:HALO, :] = jnp.where(i > 0, prev[TILE - HALO:, :], 0.0)
        ext[HALO:HALO + TILE, :] = _glu(u_ref[...], ug_ref[...], i)
        ext[HALO + TILE:, :] = jnp.zeros((8, D_CONV), F32)
        _shifted_copies(sh, ext)
        for lb in range(D_CONV // LANES):
            lanes = slice(lb * LANES, (lb + 1) * LANES)
            acc = jnp.zeros((TILE, LANES), F32) + cb_ref[:, lanes]
            for t in range(CONV_WIDTH):
                off = HALO - (CONV_WIDTH - 1) + t
                acc = acc + w_ref[t:t + 1, lanes] * _tap_window(sh, off, lanes)
            co_ref[:, lanes] = acc
        xhat, _ = _ln_stats(co_ref[...])
        a, _ = _silu_and_grad(xhat * g_ref[...] + b_ref[...])
        hc = a.astype(BF16)
        hc_ref[...] = hc
        pw_ref[...] = jnp.dot(hc, wpw_ref[...], preferred_element_type=F32)

    rm = lambda dt: jax.ShapeDtypeStruct((lp, D_CONV), dt)
    return pl.pallas_call(
        body, name="conv_fwd", grid=(nt,),
        in_specs=[_row_spec(512), _row_spec(512), _row_spec(512, shift=True), _row_spec(512, shift=True),
                  _full_spec((32, 512)), _full_spec((1, 512)), _full_spec((1, 512)), _full_spec((1, 512)),
                  _full_spec((512, 512))],
        out_specs=[_row_spec(512), _row_spec(512), _row_spec(512)],
        out_shape=[rm(F32), rm(BF16), rm(F32)],
        scratch_shapes=[pltpu.VMEM((EXT_ROWS, D_CONV), F32), pltpu.VMEM((8, SHIFT_ROWS, D_CONV), F32)],
        compiler_params=_cp(40, ("arbitrary",)),
    )(u, ug, u, ug, conv_w, conv_b, g, b, w_pw)


def _out_fwd(o, ga, pw, gc, x, metapad, g_in, b_in, w_out, g_out, b_out, target, nt):
    lp = nt * TILE

    def body(o_ref, ga_ref, pw_ref, gc_ref, x_ref, mp_ref, gi_ref, bi_ref, wo_ref, go_ref, bo_ref, t_ref,
             y_ref, dz_ref, loss_ref, dgo_ref, dbo_ref):
        i = pl.program_id(0)

        @pl.when(i == 0)
        def _():
            loss_ref[...] = jnp.zeros_like(loss_ref)
            dgo_ref[...] = jnp.zeros_like(dgo_ref)
            dbo_ref[...] = jnp.zeros_like(dbo_ref)

        x0 = jnp.where(i == 0, mp_ref[...], x_ref[...])
        xhat, _ = _ln_stats(x0)
        h = xhat * gi_ref[...] + bi_ref[...]
        ya, _ = _silu_and_grad(ga_ref[...])
        yc, _ = _silu_and_grad(gc_ref[...])
        ya = (o_ref[...] * ya).astype(BF16)
        yc = (pw_ref[...] * yc).astype(BF16)
        y_ref[:, :D_ATTN] = ya
        y_ref[:, D_ATTN:] = yc
        z = ALPHA * h + jnp.dot(ya, wo_ref[:D_ATTN, :], preferred_element_type=F32) \
            + jnp.dot(yc, wo_ref[D_ATTN:, :], preferred_element_type=F32)
        zhat, rstd = _ln_stats(z)
        out = zhat * go_ref[...] + bo_ref[...]
        live = (i > 0).astype(F32)
        err = (out - t_ref[...]) * live
        dout = err * (1.0 / D_MODEL)
        loss_ref[...] += 0.5 * jnp.sum(jnp.sum(err * dout, axis=0, keepdims=True), axis=1, keepdims=True)
        dgo_ref[...] += jnp.sum(dout * zhat, axis=0, keepdims=True)
        dbo_ref[...] += jnp.sum(dout, axis=0, keepdims=True)
        dz_ref[...] = _ln_bwd(dout, zhat, rstd, go_ref[...])

    return pl.pallas_call(
        body, name="out_fwd", grid=(nt,),
        in_specs=[_row_spec(512), _row_spec(512), _row_spec(512), _row_spec(512),
                  _row_spec(D_MODEL, shift=True), _full_spec((TILE, D_MODEL)), _full_spec((1, D_MODEL)),
                  _full_spec((1, D_MODEL)), _full_spec((D_MODEL, D_MODEL)), _full_spec((1, D_MODEL)),
                  _full_spec((1, D_MODEL)), _row_spec(D_MODEL, shift=True)],
        out_specs=[_row_spec(D_MODEL), _row_spec(D_MODEL), _full_spec((1, LANES)), _full_spec((1, D_MODEL)),
                   _full_spec((1, D_MODEL))],
        out_shape=[jax.ShapeDtypeStruct((lp, D_MODEL), BF16), jax.ShapeDtypeStruct((lp, D_MODEL), F32),
                   jax.ShapeDtypeStruct((1, LANES), F32), jax.ShapeDtypeStruct((1, D_MODEL), F32),
                   jax.ShapeDtypeStruct((1, D_MODEL), F32)],
        compiler_params=_cp(40, ("arbitrary",)),
    )(o, ga, pw, gc, x, metapad, g_in, b_in, w_out, g_out, b_out, target)


def _out_bwd(dz, y, o, ga, pw, gc, w_out, nt):
    lp = nt * TILE

    def body(dz_ref, y_ref, o_ref, ga_ref, pw_ref, gc_ref, wo_ref,
             doT_ref, dga_ref, dpw_ref, dgc_ref, dwo_ref):
        i = pl.program_id(0)

        @pl.when(i == 0)
        def _():
            dwo_ref[...] = jnp.zeros_like(dwo_ref)

        dzb = dz_ref[...].astype(BF16)
        nt_dims = (((1,), (1,)), ((), ()))
        dya = lax.dot_general(dzb, wo_ref[:D_ATTN, :], nt_dims, preferred_element_type=F32)
        dyc = lax.dot_general(dzb, wo_ref[D_ATTN:, :], nt_dims, preferred_element_type=F32)
        sa, sga = _silu_and_grad(ga_ref[...])
        sc, sgc = _silu_and_grad(gc_ref[...])
        doT_ref[0] = (dya * sa).T.astype(BF16)
        dga_ref[...] = (dya * o_ref[...] * sga).astype(BF16)
        dpw_ref[...] = (dyc * sc).astype(BF16)
        dgc_ref[...] = (dyc * pw_ref[...] * sgc).astype(BF16)
        dwo_ref[...] += lax.dot_general(y_ref[...], dzb, (((0,), (0,)), ((), ())), preferred_element_type=F32)

    rm = jax.ShapeDtypeStruct((lp, 512), BF16)
    return pl.pallas_call(
        body, name="out_bwd", grid=(nt,),
        in_specs=[_row_spec(D_MODEL), _row_spec(D_MODEL), _row_spec(512), _row_spec(512), _row_spec(512),
                  _row_spec(512), _full_spec((D_MODEL, D_MODEL))],
        out_specs=[_t3_spec(512), _row_spec(512), _row_spec(512), _row_spec(512), _full_spec((D_MODEL, D_MODEL))],
        out_shape=[jax.ShapeDtypeStruct((nt, 512, TILE), BF16), rm, rm, rm,
                   jax.ShapeDtypeStruct((D_MODEL, D_MODEL), F32)],
        compiler_params=_cp(48, ("arbitrary",)),
    )(dz, y, o, ga, pw, gc, w_out)


def _conv_bwd_ln(dpw, hc, co, w_pw, g, b, nt):
    lp = nt * TILE

    def body(dpw_ref, hc_ref, co_ref, wpw_ref, g_ref, b_ref, dco_ref, dwpw_ref, dg_ref, db_ref, dcb_ref):
        i = pl.program_id(0)

        @pl.when(i == 0)
        def _():
            dwpw_ref[...] = jnp.zeros_like(dwpw_ref)
            dg_ref[...] = jnp.zeros_like(dg_ref)
            db_ref[...] = jnp.zeros_like(db_ref)
            dcb_ref[...] = jnp.zeros_like(dcb_ref)

        dpw_b = dpw_ref[...]
        dhc = lax.dot_general(dpw_b, wpw_ref[...], (((1,), (1,)), ((), ())), preferred_element_type=F32)
        xhat, rstd = _ln_stats(co_ref[...])
        _, sg = _silu_and_grad(xhat * g_ref[...] + b_ref[...])
        dln = dhc * sg
        dg_ref[...] += jnp.sum(dln * xhat, axis=0, keepdims=True)
        db_ref[...] += jnp.sum(dln, axis=0, keepdims=True)
        dco = _ln_bwd(dln, xhat, rstd, g_ref[...])
        dco_ref[...] = dco
        dcb_ref[...] += jnp.sum(dco, axis=0, keepdims=True)
        dwpw_ref[...] += lax.dot_general(hc_ref[...], dpw_b, (((0,), (0,)), ((), ())), preferred_element_type=F32)

    vec = jax.ShapeDtypeStruct((1, D_CONV), F32)
    return pl.pallas_call(
        body, name="conv_bwd_ln", grid=(nt,),
        in_specs=[_row_spec(512), _row_spec(512), _row_spec(512), _full_spec((512, 512)), _full_spec((1, 512)),
                  _full_spec((1, 512))],
        out_specs=[_row_spec(512), _full_spec((512, 512)), _full_spec((1, 512)), _full_spec((1, 512)),
                   _full_spec((1, 512))],
        out_shape=[jax.ShapeDtypeStruct((lp, D_CONV), F32), jax.ShapeDtypeStruct((512, 512), F32), vec, vec, vec],
        compiler_params=_cp(32, ("arbitrary",)),
    )(dpw, hc, co, w_pw, g, b)


def _conv_bwd_taps(dco, u, ug, conv_w, nt):
    lp = nt * TILE

    def body(dco_ref, dcon_ref, u_ref, ug_ref, up_ref, ugp_ref, w3_ref, du_ref, dug_ref, dw_ref, ext, dext, sh, dsh,
             dhg_s, dw_s):
        i = pl.program_id(0)

        @pl.when(i == 0)
        def _():
            dw_s[...] = jnp.zeros_like(dw_s)

        prev = _glu(up_ref[...], ugp_ref[...], i - 1)
        ext[0:HALO, :] = jnp.where(i > 0, prev[TILE - HALO:, :], 0.0)
        ext[HALO:HALO + TILE, :] = _glu(u_ref[...], ug_ref[...], i)
        ext[HALO + TILE:, :] = jnp.zeros((8, D_CONV), F32)
        dext[0:TILE, :] = dco_ref[...]
        dext[TILE:TILE + HALO, :] = jnp.where(i < nt - 1, dcon_ref[0:HALO, :], 0.0)
        dext[TILE + HALO:, :] = jnp.zeros((8, D_CONV), F32)
        _shifted_copies(sh, ext)
        _shifted_copies(dsh, dext)
        stripe = 32

        def stripe_body(rb, carry):
            row0 = pl.multiple_of(rb * stripe, stripe)
            dco = dco_ref[pl.ds(row0, stripe), :]
            dhg = jnp.zeros((stripe, D_CONV), F32)
            for t in range(CONV_WIDTH):
                off = HALO - (CONV_WIDTH - 1) + t
                back = CONV_WIDTH - 1 - t
                prod = dco * sh[off % 8, pl.ds((off // 8) * 8 + row0, stripe), :]
                part = prod[0:8, :]
                for r8 in range(1, stripe // 8):
                    part = part + prod[8 * r8:8 * r8 + 8, :]
                dw_s[t] += part
                dhg = dhg + w3_ref[t] * dsh[back % 8, pl.ds((back // 8) * 8 + row0, stripe), :]
            dhg_s[pl.ds(row0, stripe), :] = dhg
            return carry

        lax.fori_loop(0, TILE // stripe, stripe_body, 0)

        @pl.when(i == nt - 1)
        def _():
            dw_ref[...] = jnp.sum(dw_s[...], axis=1)

        dhg = jnp.where(_row_mask(i, (TILE, D_CONV)), dhg_s[...], 0.0)
        sg = _sigmoid(ug_ref[...])
        du_ref[...] = (dhg * sg).astype(BF16)
        dug_ref[...] = (dhg * u_ref[...] * sg * (1.0 - sg)).astype(BF16)

    rm = jax.ShapeDtypeStruct((lp, D_CONV), BF16)
    nxt = pl.BlockSpec((TILE, 512), lambda i: (jnp.minimum(i + 1, nt - 1), 0))
    ext_t = pltpu.VMEM((EXT_ROWS, D_CONV), F32)
    sh_t = pltpu.VMEM((8, SHIFT_ROWS, D_CONV), F32)
    return pl.pallas_call(
        body, name="conv_bwd_taps", grid=(nt,),
        in_specs=[_row_spec(512), nxt, _row_spec(512), _row_spec(512), _row_spec(512, shift=True),
                  _row_spec(512, shift=True), _full_spec((32, 1, 512))],
        out_specs=[_row_spec(512), _row_spec(512), _full_spec((32, 512))],
        out_shape=[rm, rm, jax.ShapeDtypeStruct((32, D_CONV), F32)],
        scratch_shapes=[ext_t, ext_t, sh_t, sh_t, pltpu.VMEM((TILE, D_CONV), F32), pltpu.VMEM((32, 8, D_CONV), F32)],
        compiler_params=_cp(48, ("arbitrary",)),
    )(dco, dco, u, ug, u, ug, conv_w.reshape(32, 1, D_CONV))


def _cumsum_bwd(dck, dcq4, fl, bf_pad, nt):
    lp = nt * TILE

    def body(dck_ref, dcq_ref, fl_ref, bf_ref, dfl_ref, dbf_ref, carry):
        i = pl.program_id(0)
        tile = nt - 1 - i

        @pl.when(i == 0)
        def _():
            carry[...] = jnp.zeros_like(carry)
            dbf_ref[...] = jnp.zeros_like(dbf_ref)

        dc = jnp.zeros((TILE, LANES), F32)
        for p in range(N_HEADS // 2):
            dq_rows = jnp.concatenate([dcq_ref[p, 0], jnp.zeros((LANES - 8, TILE), F32)], axis=0)
            both = dck_ref[p] + dq_rows.T
            dc = dc + (both if p == 0 else pltpu.roll(both, 2 * p, 1))
        r = lax.broadcasted_iota(jnp.int32, (TILE, TILE), 0)
        c = lax.broadcasted_iota(jnp.int32, (TILE, TILE), 1)
        triu = (c >= r).astype(F32)
        dlf = jnp.dot(triu, dc, precision=lax.Precision.HIGHEST, preferred_element_type=F32) + carry[...]
        carry[...] = dlf[0:1, :]
        z = fl_ref[...] + bf_ref[...]
        lane = lax.broadcasted_iota(jnp.int32, (TILE, LANES), 1)
        dfl = jnp.where(_row_mask(tile, (TILE, LANES)) & (lane < N_HEADS), dlf * _sigmoid(-z), 0.0)
        dfl_ref[...] = dfl.astype(BF16)
        dbf_ref[...] += jnp.sum(dfl, axis=0, keepdims=True)

    rev = lambda i: (nt - 1 - i, 0)
    return pl.pallas_call(
        body, name="cumsum_bwd", grid=(nt,),
        in_specs=[pl.BlockSpec((N_HEADS // 2, TILE, LANES), lambda i: (0, nt - 1 - i, 0)),
                  pl.BlockSpec((N_HEADS // 2, 1, 8, TILE), lambda i: (0, nt - 1 - i, 0, 0)),
                  pl.BlockSpec((TILE, LANES), rev), _full_spec((1, LANES))],
        out_specs=[pl.BlockSpec((TILE, LANES), rev), _full_spec((1, LANES))],
        out_shape=[jax.ShapeDtypeStruct((lp, LANES), BF16), jax.ShapeDtypeStruct((1, LANES), F32)],
        scratch_shapes=[pltpu.VMEM((1, LANES), F32)],
        compiler_params=_cp(32, ("arbitrary",)),
    )(dck, dcq4, fl, bf_pad)


def _dw_rowmajor(hb, secs, nt):
    n = len(secs)

    def body(*refs):
        hb_ref, sec_refs, out_refs = refs[0], refs[1:1 + n], refs[1 + n:]
        i = pl.program_id(0)

        @pl.when(i == 0)
        def _():
            for o_ref in out_refs:
                o_ref[...] = jnp.zeros_like(o_ref)

        hb_t = hb_ref[...]
        for s_ref, o_ref in zip(sec_refs, out_refs):
            o_ref[...] += lax.dot_general(hb_t, s_ref[...], (((0,), (0,)), ((), ())), preferred_element_type=F32)

    return pl.pallas_call(
        body, name="dw_rowmajor", grid=(nt,),
        in_specs=[_row_spec(D_MODEL)] + [_row_spec(s.shape[1]) for s in secs],
        out_specs=[_full_spec((D_MODEL, s.shape[1])) for s in secs],
        out_shape=[jax.ShapeDtypeStruct((D_MODEL, s.shape[1]), F32) for s in secs],
        compiler_params=_cp(48, ("arbitrary",)),
    )(hb, *secs)


def _dw_transposed(hb, secs_t3, nt):
    n = len(secs_t3)

    def body(*refs):
        hb_ref, sec_refs, out_refs = refs[0], refs[1:1 + n], refs[1 + n:]
        i = pl.program_id(0)

        @pl.when(i == 0)
        def _():
            for o_ref in out_refs:
                o_ref[...] = jnp.zeros_like(o_ref)

        hb_t = hb_ref[...]
        for s_ref, o_ref in zip(sec_refs, out_refs):
            o_ref[...] += jnp.dot(s_ref[0], hb_t, preferred_element_type=F32)

    return pl.pallas_call(
        body, name="dw_transposed", grid=(nt,),
        in_specs=[_row_spec(D_MODEL)] + [_t3_spec(512) for _ in secs_t3],
        out_specs=[_full_spec((512, D_MODEL)) for _ in secs_t3],
        out_shape=[jax.ShapeDtypeStruct((512, D_MODEL), F32) for _ in secs_t3],
        compiler_params=_cp(40, ("arbitrary",)),
    )(hb, *secs_t3)


def _dh_bwd(secs, secs_t3, w_rm, w_t, dz, x, metapad, g_in, nt):
    n, m = len(secs), len(secs_t3)
    offs = OFF_GA_R + np.cumsum([0] + [s.shape[1] for s in secs])

    def body(*refs):
        sec_refs, t3_refs = refs[:n], refs[n:n + m]
        wrm_ref, wt_ref, dz_ref, x_ref, mp_ref, g_ref = refs[n + m:n + m + 6]
        dx_ref, dmeta_ref, dg_ref, db_ref = refs[n + m + 6:]
        i = pl.program_id(0)

        @pl.when(i == 0)
        def _():
            dg_ref[...] = jnp.zeros_like(dg_ref)
            db_ref[...] = jnp.zeros_like(db_ref)

        dh = ALPHA * dz_ref[...]
        for s_ref, lo, hi in zip(sec_refs, offs[:-1], offs[1:]):
            dh = dh + lax.dot_general(s_ref[...], wrm_ref[:, lo:hi], (((1,), (1,)), ((), ())),
                                      preferred_element_type=F32)
        for idx, t_ref in enumerate(t3_refs):
            dh = dh + lax.dot_general(t_ref[0], wt_ref[idx * 512:(idx + 1) * 512, :], (((0,), (0,)), ((), ())),
                                      preferred_element_type=F32)
        x0 = jnp.where(i == 0, mp_ref[...], x_ref[...])
        xhat, rstd = _ln_stats(x0)
        dg_ref[...] += jnp.sum(dh * xhat, axis=0, keepdims=True)
        db_ref[...] += jnp.sum(dh, axis=0, keepdims=True)
        dx = _ln_bwd(dh, xhat, rstd, g_ref[...])
        dx_ref[...] = dx

        @pl.when(i == 0)
        def _():
            dmeta_ref[...] = dx

    seq = (nt - 1) * TILE
    return pl.pallas_call(
        body, name="dh_bwd", grid=(nt,),
        in_specs=[_row_spec(s.shape[1]) for s in secs] + [_t3_spec(512) for _ in secs_t3]
        + [_full_spec(w_rm.shape), _full_spec(w_t.shape), _row_spec(D_MODEL), _row_spec(D_MODEL, shift=True),
           _full_spec((TILE, D_MODEL)), _full_spec((1, D_MODEL))],
        out_specs=[_row_spec(D_MODEL, shift=True), _full_spec((TILE, D_MODEL)), _full_spec((1, D_MODEL)),
                   _full_spec((1, D_MODEL))],
        out_shape=[jax.ShapeDtypeStruct((seq, D_MODEL), F32), jax.ShapeDtypeStruct((TILE, D_MODEL), F32),
                   jax.ShapeDtypeStruct((1, D_MODEL), F32), jax.ShapeDtypeStruct((1, D_MODEL), F32)],
        compiler_params=_cp(56, ("arbitrary",)),
    )(*secs, *secs_t3, w_rm, w_t, dz, x, metapad, g_in)


RB = 256
SMALL_ROWS = 48


def _repack_weights(all_in, all_small):
    n_cw = D_CONV // N_DEV

    def body(a_ref, s_ref, wr_ref, wt_ref, mp_ref, cw_ref):
        full = jnp.concatenate([a_ref[d][:, :SHARD_IN] for d in range(N_DEV)], axis=1)
        qkv = full[:, :1536]
        wr_ref[:, :1536] = qkv
        wr_ref[:, 1536:OFF_F_R] = full[:, 1544:]
        wr_ref[:, OFF_F_R:] = jnp.concatenate([full[:, 1536:1544], jnp.zeros((RB, LANES - N_HEADS), BF16)], axis=1)
        wt_ref[...] = qkv.T

        @pl.when(pl.program_id(0) == 0)
        def _():
            mp_ref[0:PAD, :] = jnp.zeros((PAD, D_MODEL), F32)
            mp_ref[PAD:, :] = jnp.concatenate([s_ref[d, 0:N_META, :] for d in range(N_DEV)], axis=1)
            cw_ref[...] = jnp.concatenate([s_ref[d, N_META:, 0:n_cw] for d in range(N_DEV)], axis=1)

    return pl.pallas_call(
        body, name="repack_weights", grid=(D_MODEL // RB,),
        in_specs=[pl.BlockSpec((N_DEV, RB, 512), lambda i: (0, i, 0)), _full_spec((N_DEV, SMALL_ROWS, LANES))],
        out_specs=[pl.BlockSpec((RB, W_COLS), lambda i: (i, 0)), pl.BlockSpec((1536, RB), lambda i: (0, i)),
                   _full_spec((TILE, D_MODEL)), _full_spec((32, D_CONV))],
        out_shape=[jax.ShapeDtypeStruct((D_MODEL, W_COLS), BF16), jax.ShapeDtypeStruct((1536, D_MODEL), BF16),
                   jax.ShapeDtypeStruct((TILE, D_MODEL), F32), jax.ShapeDtypeStruct((32, D_CONV), F32)],
        compiler_params=_cp(40, ("arbitrary",)),
    )(all_in, all_small)


def _unpack_dw_in(dw_rm, dw_t):
    def body(dga_ref, du_ref, dug_ref, dgc_ref, dfl_ref, dq_ref, dk_ref, dv_ref, out_ref):
        full = jnp.concatenate([dq_ref[...].T, dk_ref[...].T, dv_ref[...].T, dfl_ref[:, 0:N_HEADS], dga_ref[...],
                                du_ref[...], dug_ref[...], dgc_ref[...]], axis=1)
        pad = jnp.zeros((RB, 512 - SHARD_IN), F32)
        for d in range(N_DEV):
            out_ref[d] = jnp.concatenate([full[:, SHARD_IN * d:SHARD_IN * (d + 1)], pad], axis=1)

    rm = pl.BlockSpec((RB, 512), lambda i: (i, 0))
    tr = pl.BlockSpec((512, RB), lambda i: (0, i))
    return pl.pallas_call(
        body, name="unpack_dw_in", grid=(D_MODEL // RB,),
        in_specs=[rm, rm, rm, rm, pl.BlockSpec((RB, LANES), lambda i: (i, 0)), tr, tr, tr],
        out_specs=pl.BlockSpec((N_DEV, RB, 512), lambda i: (0, i, 0)),
        out_shape=jax.ShapeDtypeStruct((N_DEV, D_MODEL, 512), F32),
        compiler_params=_cp(48, ("arbitrary",)),
    )(*dw_rm, *dw_t)


def _local_step(x, target, metapad, cw, w_r, w_t, w_pw_full, w_out_full, ln_in_g, ln_in_b, b_f, conv_b, ln_conv_g,
                ln_conv_b, ln_out_g, ln_out_b):
    seq = x.shape[0]
    nt = seq // TILE + 1
    row = lambda a: a.reshape(1, -1).astype(F32)
    bf_pad = jnp.pad(row(b_f), ((0, 0), (0, LANES - N_HEADS)))
    g_in, b_in = row(ln_in_g), row(ln_in_b)
    g_cv, b_cv, c_b = row(ln_conv_g), row(ln_conv_b), row(conv_b)
    g_out, b_out = row(ln_out_g), row(ln_out_b)

    hb, qT3, kT3, vT3, k, v, ga, u, ug, gc, fl = _proj_fwd(x, metapad, g_in, b_in, w_r, nt)
    kx3 = _cumsum_fwd(fl, bf_pad, nt)
    oT3, o, lse4 = _attn_fwd(qT3, k, kx3, vT3, nt)
    co, hc, pw = _conv_fwd(u, ug, cw, c_b, g_cv, b_cv, w_pw_full, nt)
    y, dz, loss, dg_out, db_out = _out_fwd(o, ga, pw, gc, x, metapad, g_in, b_in, w_out_full, g_out, b_out,
                                            target, nt)
    doT3, dga, dpw, dgc, dw_out = _out_bwd(dz, y, o, ga, pw, gc, w_out_full, nt)
    dco, dw_pw, dg_cv, db_cv, dc_b = _conv_bwd_ln(dpw, hc, co, w_pw_full, g_cv, b_cv, nt)
    du, dug, dcw = _conv_bwd_taps(dco, u, ug, cw, nt)
    dqT3, dkT3, dvT3, dck, dcq4 = _attn_bwd(qT3, kT3, k, kx3, v, oT3, doT3, lse4, nt)
    dfl, dbf = _cumsum_bwd(dck, dcq4, fl, bf_pad, nt)
    secs = (dga, du, dug, dgc, dfl)
    secs_t3 = (dqT3, dkT3, dvT3)
    dw_rm = _dw_rowmajor(hb, secs, nt)
    dw_t = _dw_transposed(hb, secs_t3, nt)
    grad_x, dmetapad, dg_in, db_in = _dh_bwd(secs, secs_t3, w_r, w_t, dz, x, metapad, g_in, nt)
    pieces = dict(loss=loss, metapad=dmetapad, ln_in_g=dg_in, ln_in_b=db_in, w_in_rm=dw_rm, w_in_t=dw_t, b_f=dbf,
                  conv_w=dcw, conv_b=dc_b, ln_conv_g=dg_cv, ln_conv_b=db_cv, w_pw=dw_pw, w_out=dw_out,
                  ln_out_g=dg_out, ln_out_b=db_out)
    return grad_x, pieces


MESH = pl.DeviceIdType.MESH
ANY = pl.BlockSpec(memory_space=pl.ANY)


def _mesh_pos():
    return lax.axis_index("x"), lax.axis_index("y"), lax.axis_index("c")


def _all_gather(blks, name):
    n = len(blks)

    def body(*refs):
        x_refs, out_refs = refs[:n], refs[n:2 * n]
        send_sems, recv_sems, local_sems = refs[2 * n:]
        x, y, c = _mesh_pos()
        me, sibling = (x, y, c), (x, y, 1 - c)
        chips = [(1 - x, y), (x, 1 - y), (1 - x, 1 - y)]

        def slot(a, px, py, pc):
            return out_refs[a].at[4 * px + 2 * py + pc]

        def copy(a, k, block, to, src=None):
            return pltpu.make_async_remote_copy(
                src_ref=slot(a, *block) if src is None else src, dst_ref=slot(a, *block),
                send_sem=send_sems.at[7 * a + k], recv_sem=recv_sems.at[7 * a + k], device_id=to,
                device_id_type=MESH)

        arrays = range(n)
        mine = [pltpu.make_async_copy(x_refs[a], slot(a, *me), local_sems.at[a]) for a in arrays]
        for cp in mine:
            cp.start()
        first = [copy(a, 0, me, sibling, src=x_refs[a]) for a in arrays]
        first += [copy(a, 1 + j, me, (*chip, c), src=x_refs[a]) for j, chip in enumerate(chips) for a in arrays]
        for cp in first:
            cp.start()
        passed = []
        for j, chip in enumerate(chips):
            for a in arrays:
                copy(a, 1 + j, (*chip, c), me).wait_recv()
                passed.append(copy(a, 4 + j, (*chip, c), sibling))
                passed[-1].start()
        for a in arrays:
            copy(a, 0, sibling, me).wait_recv()
            for j, chip in enumerate(chips):
                copy(a, 4 + j, (*chip, 1 - c), me).wait_recv()
        for cp in first + passed:
            cp.wait_send()
        for cp in mine:
            cp.wait()

    return pl.pallas_call(
        body, name=name, out_shape=[jax.ShapeDtypeStruct((N_DEV, *b.shape), b.dtype) for b in blks],
        in_specs=[ANY] * n, out_specs=[ANY] * n,
        scratch_shapes=[pltpu.SemaphoreType.DMA((7 * n,)), pltpu.SemaphoreType.DMA((7 * n,)),
                        pltpu.SemaphoreType.DMA((n,))],
    )(*blks)


def _exchange_sibling(g8s):
    n = len(g8s)

    def body(*refs):
        g_refs, out_refs, send_sems, recv_sems = refs[:n], refs[n:2 * n], refs[2 * n], refs[2 * n + 1]
        x, y, c = _mesh_pos()
        cps = [pltpu.make_async_remote_copy(
            src_ref=g_refs[a].at[2 * q + (1 - c)], dst_ref=out_refs[a].at[q], send_sem=send_sems.at[4 * a + q],
            recv_sem=recv_sems.at[4 * a + q], device_id=(x, y, 1 - c), device_id_type=MESH)
            for a in range(n) for q in range(4)]
        for cp in cps:
            cp.start()
        for cp in cps:
            cp.wait()

    return pl.pallas_call(
        body, name="rs_sibling", out_shape=[jax.ShapeDtypeStruct((4, *g.shape[1:]), g.dtype) for g in g8s],
        in_specs=[ANY] * n, out_specs=[ANY] * n,
        scratch_shapes=[pltpu.SemaphoreType.DMA((4 * n,)), pltpu.SemaphoreType.DMA((4 * n,))],
    )(*g8s)


def _exchange_chips(p4s):
    n = len(p4s)

    def body(*refs):
        p_refs, out_refs, send_sems, recv_sems = refs[:n], refs[n:2 * n], refs[2 * n], refs[2 * n + 1]
        x, y, c = _mesh_pos()
        chips = [(1 - x, y), (x, 1 - y), (1 - x, 1 - y)]
        cps = [pltpu.make_async_remote_copy(
            src_ref=p_refs[a].at[2 * cx + cy], dst_ref=out_refs[a].at[k], send_sem=send_sems.at[3 * a + k],
            recv_sem=recv_sems.at[3 * a + k], device_id=(cx, cy, c), device_id_type=MESH)
            for k, (cx, cy) in enumerate(chips) for a in range(n)]
        for cp in cps:
            cp.start()
        for cp in cps:
            cp.wait()

    return pl.pallas_call(
        body, name="rs_chips", out_shape=[jax.ShapeDtypeStruct((3, *p.shape[1:]), p.dtype) for p in p4s],
        in_specs=[ANY] * n, out_specs=[ANY] * n,
        scratch_shapes=[pltpu.SemaphoreType.DMA((3 * n,)), pltpu.SemaphoreType.DMA((3 * n,))],
    )(*p4s)


def _rs_add_sibling(g8s, recvs, c_idx):
    n = len(g8s)

    def body(s_ref, *refs):
        g_refs, r_refs, p32_refs, pb_refs = (refs[k * n:(k + 1) * n] for k in range(4))
        for g_ref, r_ref, p32_ref, pb_ref in zip(g_refs, r_refs, p32_refs, pb_refs):
            p = g_ref[0] + r_ref[0]
            p32_ref[0] = p
            pb_ref[0] = p.astype(BF16)

    blk = lambda g: pl.BlockSpec((1, *g.shape[1:]), lambda q, s: (q, 0, 0))
    grid_spec = pltpu.PrefetchScalarGridSpec(
        num_scalar_prefetch=1, grid=(4,),
        in_specs=[pl.BlockSpec((1, *g.shape[1:]), lambda q, s: (2 * q + s[0], 0, 0)) for g in g8s]
        + [blk(g) for g in g8s],
        out_specs=[blk(g) for g in g8s] * 2)
    outs = pl.pallas_call(
        body, name="rs_add_sibling", grid_spec=grid_spec,
        out_shape=[jax.ShapeDtypeStruct((4, *g.shape[1:]), F32) for g in g8s]
        + [jax.ShapeDtypeStruct((4, *g.shape[1:]), BF16) for g in g8s],
        compiler_params=_cp(48, ("arbitrary",)),
    )(c_idx, *g8s, *recvs)
    return outs[:n], outs[n:]


def _rs_add_chips(p32s, recvs, q_idx):
    def body(s_ref, pin_ref, pout_ref, ppw_ref, rin_ref, rout_ref, rpw_ref, gin_ref, gout_ref, gpw_ref):
        def total(p_ref, r_ref):
            return ((p_ref[0] + r_ref[0].astype(F32)) + r_ref[1].astype(F32)) + r_ref[2].astype(F32)

        gin_ref[0] = total(pin_ref, rin_ref)[:, :SHARD_IN]
        gout_ref[0] = total(pout_ref, rout_ref)
        gpw_ref[0] = total(ppw_ref, rpw_ref)

    own = lambda p: pl.BlockSpec((1, *p.shape[1:]), lambda i, s: (s[0], 0, 0))
    whole = lambda shape: pl.BlockSpec(shape, lambda i, s: (0, 0, 0))
    out_shapes = [(1, D_MODEL, SHARD_IN), (1, *p32s[1].shape[1:]), (1, *p32s[2].shape[1:])]
    grid_spec = pltpu.PrefetchScalarGridSpec(
        num_scalar_prefetch=1, grid=(1,),
        in_specs=[own(p) for p in p32s] + [whole(r.shape) for r in recvs],
        out_specs=[whole(s) for s in out_shapes])
    return pl.pallas_call(
        body, name="rs_add_chips", grid_spec=grid_spec,
        out_shape=[jax.ShapeDtypeStruct(s, F32) for s in out_shapes],
        compiler_params=_cp(48, ("arbitrary",)),
    )(q_idx, *p32s, *recvs)


SMALL_ROWS_G = 64
SMALL_LAYOUT = {
    "metapad": (0, N_META, D_MODEL), "conv_w": (16, 32, D_CONV), "ln_in_g": (48, 1, D_MODEL),
    "ln_in_b": (49, 1, D_MODEL), "b_f": (50, 1, LANES), "conv_b": (51, 1, D_CONV), "ln_conv_g": (52, 1, D_CONV),
    "ln_conv_b": (53, 1, D_CONV), "ln_out_g": (54, 1, D_MODEL), "ln_out_b": (55, 1, D_MODEL), "loss": (56, 1, LANES)}


def _pack_small(pieces):
    names = list(SMALL_LAYOUT)

    def body(*refs):
        out_ref = refs[-1]
        out_ref[...] = jnp.zeros_like(out_ref)
        for name, ref in zip(names, refs[:-1]):
            r0, nr, nl = SMALL_LAYOUT[name]
            src = ref[PAD:, :] if name == "metapad" else ref[...]
            out_ref[r0:r0 + nr, 0:nl] = src

    return pl.pallas_call(body, name="pack_small", out_shape=jax.ShapeDtypeStruct((SMALL_ROWS_G, D_MODEL), F32),
                          compiler_params=_cp(16))(*[pieces[n] for n in names])


def _sum_small(gathered):
    names = list(SMALL_LAYOUT)

    def body(a_ref, *out_refs):
        acc = a_ref[0]
        for d in range(1, N_DEV):
            acc = acc + a_ref[d]
        for name, ref in zip(names, out_refs):
            r0, nr, nl = SMALL_LAYOUT[name]
            ref[...] = acc[r0:r0 + nr, 0:nl]

    outs = pl.pallas_call(
        body, name="sum_small",
        out_shape=[jax.ShapeDtypeStruct(SMALL_LAYOUT[n][1:], F32) for n in names], compiler_params=_cp(16))(gathered)
    return dict(zip(names, outs))


def _adamw(ws, gs, ms, vs):
    n = len(ws)
    c1 = 1.0 - ADAM_B1 ** ADAM_STEP
    c2 = 1.0 - ADAM_B2 ** ADAM_STEP

    def body(*refs):
        w_refs, g_refs, m_refs, v_refs = (refs[k * n:(k + 1) * n] for k in range(4))
        d_refs, nm_refs, nv_refs = (refs[(4 + k) * n:(5 + k) * n] for k in range(3))
        for w_ref, g_ref, m_ref, v_ref, d_ref, nm_ref, nv_ref in zip(w_refs, g_refs, m_refs, v_refs, d_refs,
                                                                     nm_refs, nv_refs):
            g = g_ref[...]
            m = ADAM_B1 * m_ref[...] + (1.0 - ADAM_B1) * g
            v = ADAM_B2 * v_ref[...] + (1.0 - ADAM_B2) * (g * g)
            nm_ref[...] = m
            nv_ref[...] = v
            d_ref[...] = -ADAM_LR * ((m / c1) / (jnp.sqrt(v / c2) + ADAM_EPS) + ADAM_WD * w_ref[...])

    shapes = [jax.ShapeDtypeStruct(w.shape, F32) for w in ws]
    outs = pl.pallas_call(body, name="adamw", out_shape=shapes * 3, compiler_params=_cp(48))(*ws, *gs, *ms, *vs)
    return outs[:n], outs[n:2 * n], outs[2 * n:]


W_NAMES = ("meta", "ln_in_g", "ln_in_b", "w_in", "b_f", "conv_w", "conv_b", "ln_conv_g", "ln_conv_b", "w_pw",
           "w_out", "ln_out_g", "ln_out_b")


def kernel(x, meta, ln_in_g, ln_in_b, w_in, b_f, conv_w, conv_b, ln_conv_g, ln_conv_b, w_pw, w_out, ln_out_g, ln_out_b, loss_target, m_meta, m_ln_in_g, m_ln_in_b, m_w_in, m_b_f, m_conv_w, m_conv_b, m_ln_conv_g, m_ln_conv_b, m_w_pw, m_w_out, m_ln_out_g, m_ln_out_b, v_meta, v_ln_in_g, v_ln_in_b, v_w_in, v_b_f, v_conv_w, v_conv_b, v_ln_conv_g, v_ln_conv_b, v_w_pw, v_w_out, v_ln_out_g, v_ln_out_b):
    mx, my, mc = _mesh_pos()
    me = 4 * mx + 2 * my + mc
    n_meta_sh = D_MODEL // N_DEV
    n_cw_sh = D_CONV // N_DEV
    n_out_sh = D_MODEL // N_DEV
    n_pw_sh = D_CONV // N_DEV

    small_w = jnp.concatenate([meta, jnp.pad(conv_w[0], ((0, 1), (0, LANES - n_cw_sh)))], axis=0)
    all_in, all_out, all_pw, all_small = _all_gather(
        [jnp.pad(w_in[0], ((0, 0), (0, 512 - SHARD_IN))).astype(BF16), w_out[0].astype(BF16), w_pw[0].astype(BF16),
         small_w], "gather_weights")
    w_r, w_t, metapad, cw = _repack_weights(all_in, all_small)
    w_out_full = all_out.reshape(D_MODEL, D_MODEL)
    w_pw_full = all_pw.reshape(D_CONV, D_CONV)

    grad_x, pc = _local_step(x[0], loss_target[0], metapad, cw, w_r, w_t, w_pw_full, w_out_full, ln_in_g, ln_in_b,
                             b_f[0], conv_b[0], ln_conv_g[0], ln_conv_b[0], ln_out_g[0], ln_out_b[0])

    g8s = [_unpack_dw_in(pc["w_in_rm"], pc["w_in_t"]), pc["w_out"].reshape(N_DEV, n_out_sh, D_MODEL),
           pc["w_pw"].reshape(N_DEV, n_pw_sh, D_CONV)]
    from_sibling = _exchange_sibling(g8s)
    p32s, pbs = _rs_add_sibling(g8s, from_sibling, jnp.reshape(mc, (1,)).astype(jnp.int32))
    from_chips = _exchange_chips(pbs)
    g_w_in, g_w_out, g_w_pw = _rs_add_chips(p32s, from_chips, jnp.reshape(2 * mx + my, (1,)).astype(jnp.int32))

    sm = _sum_small(_all_gather([_pack_small(pc)], "gather_small_grads")[0])
    grads = {
        "meta": lax.dynamic_slice_in_dim(sm["metapad"], me * n_meta_sh, n_meta_sh, axis=1),
        "ln_in_g": sm["ln_in_g"].reshape(D_MODEL), "ln_in_b": sm["ln_in_b"].reshape(D_MODEL), "w_in": g_w_in,
        "b_f": sm["b_f"][:, :N_HEADS],
        "conv_w": lax.dynamic_slice_in_dim(sm["conv_w"], me * n_cw_sh, n_cw_sh, axis=1)[None, :CONV_WIDTH],
        "conv_b": sm["conv_b"], "ln_conv_g": sm["ln_conv_g"], "ln_conv_b": sm["ln_conv_b"], "w_pw": g_w_pw,
        "w_out": g_w_out, "ln_out_g": sm["ln_out_g"], "ln_out_b": sm["ln_out_b"]}
    loss_all = sm["loss"][0, 0]

    weights = dict(meta=meta, ln_in_g=ln_in_g, ln_in_b=ln_in_b, w_in=w_in, b_f=b_f, conv_w=conv_w, conv_b=conv_b,
                   ln_conv_g=ln_conv_g, ln_conv_b=ln_conv_b, w_pw=w_pw, w_out=w_out, ln_out_g=ln_out_g,
                   ln_out_b=ln_out_b)
    moms = dict(meta=m_meta, ln_in_g=m_ln_in_g, ln_in_b=m_ln_in_b, w_in=m_w_in, b_f=m_b_f, conv_w=m_conv_w,
                conv_b=m_conv_b, ln_conv_g=m_ln_conv_g, ln_conv_b=m_ln_conv_b, w_pw=m_w_pw, w_out=m_w_out,
                ln_out_g=m_ln_out_g, ln_out_b=m_ln_out_b)
    vels = dict(meta=v_meta, ln_in_g=v_ln_in_g, ln_in_b=v_ln_in_b, w_in=v_w_in, b_f=v_b_f, conv_w=v_conv_w,
                conv_b=v_conv_b, ln_conv_g=v_ln_conv_g, ln_conv_b=v_ln_conv_b, w_pw=v_w_pw, w_out=v_w_out,
                ln_out_g=v_ln_out_g, ln_out_b=v_ln_out_b)

    def as_rows(a):
        return a.reshape(1, -1) if a.ndim == 1 else a

    deltas, new_m, new_v = _adamw([as_rows(weights[n]) for n in W_NAMES], [as_rows(grads[n]) for n in W_NAMES],
                                  [as_rows(moms[n]) for n in W_NAMES], [as_rows(vels[n]) for n in W_NAMES])
    shp = [weights[n].shape for n in W_NAMES]
    return (loss_all, grad_x[None], *[grads[n] for n in W_NAMES],
            *[d.reshape(s) for d, s in zip(deltas, shp)], *[a.reshape(s) for a, s in zip(new_m, shp)],
            *[a.reshape(s) for a, s in zip(new_v, shp)])
```

```python
import jax
import jax.numpy as jnp
import numpy as np
from jax import lax
from jax.experimental import pallas as pl
from jax.experimental.pallas import tpu as pltpu

F32 = jnp.float32
BF16 = jnp.bfloat16

D_MODEL = 1024
D_ATTN = 512
D_CONV = 512
N_HEADS = 8
HEAD_DIM = 64
N_META = 16
CONV_WIDTH = 31
LN_EPS = 1e-5
ALPHA = 2.0 ** 0.25
SCALE = HEAD_DIM ** -0.5
LOG2E = 1.4426950408889634
ADAM_LR, ADAM_B1, ADAM_B2, ADAM_EPS, ADAM_WD, ADAM_STEP = 0.001, 0.9, 0.999, 1e-08, 0.01, 10

N_DEV = 8
D_IN = 3592
SHARD_IN = D_IN // N_DEV
TILE = 256
PAD = TILE - N_META
HALO = 32
SHIFT_ROWS = TILE + HALO
EXT_ROWS = SHIFT_ROWS + 8
NEG = -1e30
LANES = 128
W_COLS = 7 * 512 + LANES
OFF_GA_R, OFF_F_R = 1536, 3584
MIB = 1024 * 1024


def _cp(vmem_mib, sem=None):
    kw = dict(vmem_limit_bytes=vmem_mib * MIB)
    if sem is not None:
        kw["dimension_semantics"] = sem
    return pltpu.CompilerParams(**kw)


def _sigmoid(x):
    return 1.0 / (1.0 + jnp.exp(-x))


def _silu_and_grad(x):
    s = _sigmoid(x)
    return x * s, s * (1.0 + x * (1.0 - s))


def _ln_stats(x):
    mu = jnp.mean(x, axis=-1, keepdims=True)
    xc = x - mu
    var = jnp.mean(xc * xc, axis=-1, keepdims=True)
    rstd = lax.rsqrt(var + LN_EPS)
    return xc * rstd, rstd


def _ln_bwd(dy, xhat, rstd, g):
    dxh = dy * g
    m1 = jnp.mean(dxh, axis=-1, keepdims=True)
    m2 = jnp.mean(dxh * xhat, axis=-1, keepdims=True)
    return rstd * (dxh - m1 - xhat * m2)


def _row_spec(cols, shift=False):
    if shift:
        return pl.BlockSpec((TILE, cols), lambda i: (jnp.maximum(i - 1, 0), 0))
    return pl.BlockSpec((TILE, cols), lambda i: (i, 0))


def _full_spec(shape):
    nd = len(shape)
    return pl.BlockSpec(shape, lambda i: (0,) * nd)


def _t3_spec(ch):
    return pl.BlockSpec((1, ch, TILE), lambda i: (i, 0, 0))


def _proj_fwd(x, metapad, g_in, b_in, w_r, nt):
    lp = nt * TILE

    def body(x_ref, mp_ref, g_ref, b_ref, w_ref, hb_ref, qT_ref, kT_ref, vT_ref, k_ref, v_ref,
             ga_ref, u_ref, ug_ref, gc_ref, fl_ref):
        i = pl.program_id(0)
        x0 = jnp.where(i == 0, mp_ref[...], x_ref[...])
        xhat, _ = _ln_stats(x0)
        hb = (xhat * g_ref[...] + b_ref[...]).astype(BF16)
        hb_ref[...] = hb

        def sec(off, n=512):
            return jnp.dot(hb, w_ref[:, off:off + n], preferred_element_type=F32)

        qT_ref[0] = (sec(0) * (SCALE * LOG2E)).T.astype(BF16)
        k = sec(512)
        kT_ref[0] = k.T.astype(BF16)
        k_ref[...] = k.astype(BF16)
        v = sec(1024)
        vT_ref[0] = v.T.astype(BF16)
        v_ref[...] = v.astype(BF16)
        ga_ref[...] = sec(OFF_GA_R)
        u_ref[...] = sec(OFF_GA_R + 512)
        ug_ref[...] = sec(OFF_GA_R + 1024)
        gc_ref[...] = sec(OFF_GA_R + 1536)
        fl_ref[...] = sec(OFF_F_R, LANES)

    t3 = jax.ShapeDtypeStruct((nt, 512, TILE), BF16)
    rm = lambda dt: jax.ShapeDtypeStruct((lp, 512), dt)
    return pl.pallas_call(
        body, name="proj_fwd", grid=(nt,),
        in_specs=[_row_spec(D_MODEL, shift=True), _full_spec((TILE, D_MODEL)), _full_spec((1, D_MODEL)),
                  _full_spec((1, D_MODEL)), _full_spec((D_MODEL, W_COLS))],
        out_specs=[_row_spec(D_MODEL), _t3_spec(512), _t3_spec(512), _t3_spec(512), _row_spec(512), _row_spec(512),
                   _row_spec(512), _row_spec(512), _row_spec(512), _row_spec(512), _row_spec(LANES)],
        out_shape=[jax.ShapeDtypeStruct((lp, D_MODEL), BF16), t3, t3, t3, rm(BF16), rm(BF16),
                   rm(F32), rm(F32), rm(F32), rm(F32), jax.ShapeDtypeStruct((lp, LANES), F32)],
        compiler_params=_cp(56, ("arbitrary",)),
    )(x, metapad, g_in, b_in, w_r)


def _row_mask(i, shape):
    r = lax.broadcasted_iota(jnp.int32, shape, 0)
    return (r >= PAD) | (i > 0)


def _cumsum_fwd(fl, bf_pad, nt):
    lp = nt * TILE

    def body(fl_ref, bf_ref, kx_ref, carry):
        i = pl.program_id(0)

        @pl.when(i == 0)
        def _():
            carry[...] = jnp.zeros_like(carry)

        z = fl_ref[...] + bf_ref[...]
        lf = jnp.minimum(z, 0.0) - jnp.log(1.0 + jnp.exp(-jnp.abs(z)))
        lane = lax.broadcasted_iota(jnp.int32, (TILE, LANES), 1)
        real = _row_mask(i, (TILE, LANES))
        lf = jnp.where(real & (lane < N_HEADS), lf, 0.0)
        r = lax.broadcasted_iota(jnp.int32, (TILE, TILE), 0)
        c = lax.broadcasted_iota(jnp.int32, (TILE, TILE), 1)
        tril = (c <= r).astype(F32)
        cs = jnp.dot(tril, lf, precision=lax.Precision.HIGHEST, preferred_element_type=F32) + carry[...]
        carry[...] = cs[TILE - 1:TILE, :]
        bias = jnp.where(real, cs * (-LOG2E), NEG)
        hi = bias.astype(BF16).astype(F32)
        mid = (bias - hi).astype(BF16).astype(F32)
        lo = (bias - hi - mid).astype(BF16).astype(F32)
        for p in range(N_HEADS // 2):
            out = jnp.zeros((TILE, LANES), F32)
            for hh in range(2):
                for part, piece in enumerate((hi, mid, lo)):
                    dst, src = 3 * hh + part, 2 * p + hh
                    moved = piece if dst == src else pltpu.roll(piece, (dst - src) % LANES, 1)
                    out = jnp.where(lane == dst, moved, out)
            kx_ref[p] = out.astype(BF16)

    return pl.pallas_call(
        body, name="cumsum_fwd", grid=(nt,),
        in_specs=[_row_spec(LANES), _full_spec((1, LANES))],
        out_specs=pl.BlockSpec((N_HEADS // 2, TILE, LANES), lambda i: (0, i, 0)),
        out_shape=jax.ShapeDtypeStruct((N_HEADS // 2, lp, LANES), BF16),
        scratch_shapes=[pltpu.VMEM((1, LANES), F32)],
        compiler_params=_cp(32, ("arbitrary",)),
    )(fl, bf_pad)


def _head_rows(blk, hh):
    r = lax.broadcasted_iota(jnp.int32, blk.shape, 0)
    return jnp.where((r >= hh * HEAD_DIM) & (r < (hh + 1) * HEAD_DIM), blk, jnp.zeros_like(blk))


def _two_heads(blk):
    return jnp.concatenate([_head_rows(blk, 0), _head_rows(blk, 1)], axis=1)


def _bias_rows():
    r = lax.broadcasted_iota(jnp.int32, (LANES, 2 * TILE), 0)
    c = lax.broadcasted_iota(jnp.int32, (LANES, 2 * TILE), 1)
    return jnp.where(((r < 3) & (c < TILE)) | ((r >= 3) & (r < 6) & (c >= TILE)), 1.0, 0.0).astype(BF16)


def _diag_mask(s):
    kpos = lax.broadcasted_iota(jnp.int32, (TILE, TILE), 0)
    qpos = lax.broadcasted_iota(jnp.int32, (TILE, TILE), 1)
    return jnp.where(kpos <= qpos, s, NEG)


def _stream(n, first, nxt, scores, update):
    if n == 0:
        return
    scores(first, 0)

    def pair_body(_, idx):
        idx_b = nxt(idx)
        scores(idx_b, 1)
        update(idx, 0)
        idx_c = nxt(idx_b)
        scores(idx_c, 0)
        update(idx_b, 1)
        return idx_c

    idx = lax.fori_loop(0, (n - 1) // 2, pair_body, first)
    if n % 2 == 1:
        update(idx, 0)
    else:
        idx_b = nxt(idx)
        scores(idx_b, 1)
        update(idx, 0)
        update(idx_b, 1)


def _next_below_diagonal(idx):
    i, j = idx
    wrap = j + 1 >= i
    return jnp.where(wrap, i + 1, i), jnp.where(wrap, 0, j + 1)


def _tile_rows(t):
    return pl.ds(pl.multiple_of(t * TILE, TILE), TILE)


def _attn_fwd(qT3, k, kx3, vT3, nt):
    lp = nt * TILE
    npair = N_HEADS // 2

    def body(qT_ref, k_ref, kx_ref, vT_ref, oT_ref, o_ref, lse_ref, sbuf, m_s, l_s, acc_s):
        ones = _bias_rows()

        def scores(idx, slot):
            i, j = idx
            qcat = jnp.concatenate([_two_heads(qT_ref[i]), ones], axis=0)
            kext = jnp.concatenate([k_ref[_tile_rows(j), :], kx_ref[0, _tile_rows(j), :]], axis=1)
            sbuf[slot] = jnp.dot(kext, qcat, preferred_element_type=F32)

        def update(idx, slot, diag):
            i, j = idx
            for hh in range(2):
                s = sbuf[slot, :, hh * TILE:(hh + 1) * TILE]
                vj = vT_ref[j, hh * HEAD_DIM:(hh + 1) * HEAD_DIM, :]
                if diag:
                    s = _diag_mask(s)
                    m_new = jnp.max(s, axis=0, keepdims=True)
                    p = jnp.exp2(s - m_new)
                    l_s[i, hh] = jnp.sum(p, axis=0, keepdims=True)
                    acc_s[i, hh] = jnp.dot(vj, p.astype(BF16), preferred_element_type=F32)
                else:
                    m_prev = m_s[i, hh]
                    m_new = jnp.maximum(m_prev, jnp.max(s, axis=0, keepdims=True))
                    a = jnp.exp2(m_prev - m_new)
                    p = jnp.exp2(s - m_new)
                    l_s[i, hh] = a * l_s[i, hh] + jnp.sum(p, axis=0, keepdims=True)
                    acc_s[i, hh] = a * acc_s[i, hh] + jnp.dot(vj, p.astype(BF16), preferred_element_type=F32)
                m_s[i, hh] = m_new

        zero = jnp.int32(0)
        _stream(nt, (zero, zero), lambda idx: (idx[0] + 1, idx[1] + 1), scores,
                lambda idx, slot: update(idx, slot, True))
        _stream(nt * (nt - 1) // 2, (zero + 1, zero), _next_below_diagonal, scores,
                lambda idx, slot: update(idx, slot, False))

        def finish(i, carry):
            for hh in range(2):
                l = l_s[i, hh]
                oT_ref[i, hh * HEAD_DIM:(hh + 1) * HEAD_DIM, :] = acc_s[i, hh] / l
                lse_ref[0, i, hh:hh + 1, :] = m_s[i, hh] + jnp.log(l) * LOG2E
            o_ref[_tile_rows(i), :] = oT_ref[i].T
            return carry

        lax.fori_loop(0, nt, finish, 0)

    blk_t = pl.BlockSpec((nt, LANES, TILE), lambda p: (0, p, 0))
    blk_rm = pl.BlockSpec((lp, LANES), lambda p: (0, p))
    blk_px = pl.BlockSpec((1, lp, LANES), lambda p: (p, 0, 0))
    blk_st = pl.BlockSpec((1, nt, 8, TILE), lambda p: (p, 0, 0, 0))
    return pl.pallas_call(
        body, name="attn_fwd", grid=(npair,),
        in_specs=[blk_t, blk_rm, blk_px, blk_t],
        out_specs=[blk_t, blk_rm, blk_st],
        out_shape=[jax.ShapeDtypeStruct((nt, D_ATTN, TILE), F32),
                   jax.ShapeDtypeStruct((lp, D_ATTN), F32),
                   jax.ShapeDtypeStruct((npair, nt, 8, TILE), F32)],
        scratch_shapes=[pltpu.VMEM((2, TILE, 2 * TILE), F32), pltpu.VMEM((nt, 2, 1, TILE), F32),
                        pltpu.VMEM((nt, 2, 1, TILE), F32), pltpu.VMEM((nt, 2, HEAD_DIM, TILE), F32)],
        compiler_params=_cp(56, ("arbitrary",)),
    )(qT3, k, kx3, vT3)


def _attn_bwd(qT3, kT3, k, kx3, v, oT3, doT3, lse4, nt):
    lp = nt * TILE
    npair = N_HEADS // 2

    def body(qT_ref, kT_ref, k_ref, kx_ref, v_ref, oT_ref, doT_ref, lse_ref,
             dqT_ref, dkT_ref, dvT_ref, dck_ref, dcq_ref, sbuf, dpbuf, dq_s, dk_s, dv_s, dc_s):
        ones = _bias_rows()
        nt_dims = (((1,), (1,)), ((), ()))

        def scores(idx, slot):
            i, j = idx
            qcat = jnp.concatenate([_two_heads(qT_ref[i]), ones], axis=0)
            kext = jnp.concatenate([k_ref[_tile_rows(j), :], kx_ref[0, _tile_rows(j), :]], axis=1)
            sbuf[slot] = jnp.dot(kext, qcat, preferred_element_type=F32)
            dpbuf[slot] = jnp.dot(v_ref[_tile_rows(j), :], _two_heads(doT_ref[i]), preferred_element_type=F32)

        def update(idx, slot, diag):
            i, j = idx
            for hh in range(2):
                hs = slice(hh * HEAD_DIM, (hh + 1) * HEAD_DIM)
                s = sbuf[slot, :, hh * TILE:(hh + 1) * TILE]
                if diag:
                    s = _diag_mask(s)
                p = jnp.exp2(s - lse_ref[0, i, hh:hh + 1, :])
                doh = doT_ref[i, hs, :]
                delta = jnp.sum(doh.astype(F32) * oT_ref[i, hs, :], axis=0, keepdims=True)
                ds = p * (dpbuf[slot, :, hh * TILE:(hh + 1) * TILE] - delta)
                dsb = ds.astype(BF16)
                dv = lax.dot_general(doh, p.astype(BF16), nt_dims, preferred_element_type=F32)
                dk = lax.dot_general(qT_ref[i, hs, :], dsb, nt_dims, preferred_element_type=F32)
                dq = jnp.dot(kT_ref[j, hs, :], dsb, preferred_element_type=F32)
                dc = ds[:, :LANES] + ds[:, LANES:]
                dcq = jnp.sum(ds, axis=0, keepdims=True)
                if diag:
                    dv_s[j, hh] = dv
                    dk_s[j, hh] = dk
                    dc_s[j, hh] = dc
                    dq_s[i, hs, :] = dq
                    dcq_ref[0, i, hh:hh + 1, :] = dcq
                else:
                    dv_s[j, hh] += dv
                    dk_s[j, hh] += dk
                    dc_s[j, hh] += dc
                    dq_s[i, hs, :] += dq
                    dcq_ref[0, i, hh:hh + 1, :] += dcq

        dcq_ref[...] = jnp.zeros_like(dcq_ref)
        zero = jnp.int32(0)
        _stream(nt, (zero, zero), lambda idx: (idx[0] + 1, idx[1] + 1), scores,
                lambda idx, slot: update(idx, slot, True))
        _stream(nt * (nt - 1) // 2, (zero + 1, zero), _next_below_diagonal, scores,
                lambda idx, slot: update(idx, slot, False))

        lane = lax.broadcasted_iota(jnp.int32, (TILE, LANES), 1)

        def finish(t, carry):
            dck = jnp.zeros((TILE, LANES), F32)
            for hh in range(2):
                hs = slice(hh * HEAD_DIM, (hh + 1) * HEAD_DIM)
                dkT_ref[t, hs, :] = (dk_s[t, hh] * (1.0 / LOG2E)).astype(BF16)
                dvT_ref[t, hs, :] = dv_s[t, hh].astype(BF16)
                dck = jnp.where(lane == hh, -jnp.sum(dc_s[t, hh], axis=1, keepdims=True), dck)
            dck_ref[0, _tile_rows(t), :] = dck
            dqT_ref[t] = (dq_s[t] * SCALE).astype(BF16)
            return carry

        lax.fori_loop(0, nt, finish, 0)

    blk_t = pl.BlockSpec((nt, LANES, TILE), lambda p: (0, p, 0))
    blk_rm = pl.BlockSpec((lp, LANES), lambda p: (0, p))
    blk_px = pl.BlockSpec((1, lp, LANES), lambda p: (p, 0, 0))
    blk_st = pl.BlockSpec((1, nt, 8, TILE), lambda p: (p, 0, 0, 0))
    t3 = jax.ShapeDtypeStruct((nt, D_ATTN, TILE), BF16)
    return pl.pallas_call(
        body, name="attn_bwd", grid=(npair,),
        in_specs=[blk_t, blk_t, blk_rm, blk_px, blk_rm, blk_t, blk_t, blk_st],
        out_specs=[blk_t, blk_t, blk_t, blk_px, blk_st],
        out_shape=[t3, t3, t3, jax.ShapeDtypeStruct((npair, lp, LANES), F32),
                   jax.ShapeDtypeStruct((npair, nt, 8, TILE), F32)],
        scratch_shapes=[pltpu.VMEM((2, TILE, 2 * TILE), F32), pltpu.VMEM((2, TILE, 2 * TILE), F32),
                        pltpu.VMEM((nt, LANES, TILE), F32), pltpu.VMEM((nt, 2, HEAD_DIM, TILE), F32),
                        pltpu.VMEM((nt, 2, HEAD_DIM, TILE), F32), pltpu.VMEM((nt, 2, TILE, LANES), F32)],
        compiler_params=_cp(60, ("arbitrary",)),
    )(qT3, kT3, k, kx3, v, oT3, doT3, lse4)


def _glu(u, ug, i):
    return jnp.where(_row_mask(i, u.shape), u * _sigmoid(ug), 0.0)


def _shifted_copies(dst, src):
    for ph in range(8):
        dst[ph] = src[ph:ph + SHIFT_ROWS, :]


def _tap_window(sh, off, lanes, row0=0, rows=TILE):
    base = (off // 8) * 8 + row0
    return sh[off % 8, base:base + rows, lanes]


def _conv_fwd(u, ug, conv_w, conv_b, g, b, w_pw, nt):
    lp = nt * TILE

    def body(u_ref, ug_ref, up_ref, ugp_ref, w_ref, cb_ref, g_ref, b_ref, wpw_ref,
             co_ref, hc_ref, pw_ref, ext, sh):
        i = pl.program_id(0)
        prev = _glu(up_ref[...], ugp_ref[...], i - 1)
        ext[0:HALO, :] = jnp.where(i > 0, prev[TILE - HALO:, :], 0.0)
        ext[HALO:HALO + TILE, :] = _glu(u_ref[...], ug_ref[...], i)
        ext[HALO + TILE:, :] = jnp.zeros((8, D_CONV), F32)
        _shifted_copies(sh, ext)
        for lb in range(D_CONV // LANES):
            lanes = slice(lb * LANES, (lb + 1) * LANES)
            acc = jnp.zeros((TILE, LANES), F32) + cb_ref[:, lanes]
            for t in range(CONV_WIDTH):
                off = HALO - (CONV_WIDTH - 1) + t
                acc = acc + w_ref[t:t + 1, lanes] * _tap_window(sh, off, lanes)
            co_ref[:, lanes] = acc
        xhat, _ = _ln_stats(co_ref[...])
        a, _ = _silu_and_grad(xhat * g_ref[...] + b_ref[...])
        hc = a.astype(BF16)
        hc_ref[...] = hc
        pw_ref[...] = jnp.dot(hc, wpw_ref[...], preferred_element_type=F32)

    rm = lambda dt: jax.ShapeDtypeStruct((lp, D_CONV), dt)
    return pl.pallas_call(
        body, name="conv_fwd", grid=(nt,),
        in_specs=[_row_spec(512), _row_spec(512), _row_spec(512, shift=True), _row_spec(512, shift=True),
                  _full_spec((32, 512)), _full_spec((1, 512)), _full_spec((1, 512)), _full_spec((1, 512)),
                  _full_spec((512, 512))],
        out_specs=[_row_spec(512), _row_spec(512), _row_spec(512)],
        out_shape=[rm(F32), rm(BF16), rm(F32)],
        scratch_shapes=[pltpu.VMEM((EXT_ROWS, D_CONV), F32), pltpu.VMEM((8, SHIFT_ROWS, D_CONV), F32)],
        compiler_params=_cp(40, ("arbitrary",)),
    )(u, ug, u, ug, conv_w, conv_b, g, b, w_pw)


def _out_fwd(o, ga, pw, gc, x, metapad, g_in, b_in, w_out, g_out, b_out, target, nt):
    lp = nt * TILE

    def body(o_ref, ga_ref, pw_ref, gc_ref, x_ref, mp_ref, gi_ref, bi_ref, wo_ref, go_ref, bo_ref, t_ref,
             y_ref, dz_ref, loss_ref, dgo_ref, dbo_ref):
        i = pl.program_id(0)

        @pl.when(i == 0)
        def _():
            loss_ref[...] = jnp.zeros_like(loss_ref)
            dgo_ref[...] = jnp.zeros_like(dgo_ref)
            dbo_ref[...] = jnp.zeros_like(dbo_ref)

        x0 = jnp.where(i == 0, mp_ref[...], x_ref[...])
        xhat, _ = _ln_stats(x0)
        h = xhat * gi_ref[...] + bi_ref[...]
        ya, _ = _silu_and_grad(ga_ref[...])
        yc, _ = _silu_and_grad(gc_ref[...])
        ya = (o_ref[...] * ya).astype(BF16)
        yc = (pw_ref[...] * yc).astype(BF16)
        y_ref[:, :D_ATTN] = ya
        y_ref[:, D_ATTN:] = yc
        z = ALPHA * h + jnp.dot(ya, wo_ref[:D_ATTN, :], preferred_element_type=F32) \
            + jnp.dot(yc, wo_ref[D_ATTN:, :], preferred_element_type=F32)
        zhat, rstd = _ln_stats(z)
        out = zhat * go_ref[...] + bo_ref[...]
        live = (i > 0).astype(F32)
        err = (out - t_ref[...]) * live
        dout = err * (1.0 / D_MODEL)
        loss_ref[...] += 0.5 * jnp.sum(jnp.sum(err * dout, axis=0, keepdims=True), axis=1, keepdims=True)
        dgo_ref[...] += jnp.sum(dout * zhat, axis=0, keepdims=True)
        dbo_ref[...] += jnp.sum(dout, axis=0, keepdims=True)
        dz_ref[...] = _ln_bwd(dout, zhat, rstd, go_ref[...])

    return pl.pallas_call(
        body, name="out_fwd", grid=(nt,),
        in_specs=[_row_spec(512), _row_spec(512), _row_spec(512), _row_spec(512),
                  _row_spec(D_MODEL, shift=True), _full_spec((TILE, D_MODEL)), _full_spec((1, D_MODEL)),
                  _full_spec((1, D_MODEL)), _full_spec((D_MODEL, D_MODEL)), _full_spec((1, D_MODEL)),
                  _full_spec((1, D_MODEL)), _row_spec(D_MODEL, shift=True)],
        out_specs=[_row_spec(D_MODEL), _row_spec(D_MODEL), _full_spec((1, LANES)), _full_spec((1, D_MODEL)),
                   _full_spec((1, D_MODEL))],
        out_shape=[jax.ShapeDtypeStruct((lp, D_MODEL), BF16), jax.ShapeDtypeStruct((lp, D_MODEL), F32),
                   jax.ShapeDtypeStruct((1, LANES), F32), jax.ShapeDtypeStruct((1, D_MODEL), F32),
                   jax.ShapeDtypeStruct((1, D_MODEL), F32)],
        compiler_params=_cp(40, ("arbitrary",)),
    )(o, ga, pw, gc, x, metapad, g_in, b_in, w_out, g_out, b_out, target)


def _out_bwd(dz, y, o, ga, pw, gc, w_out, nt):
    lp = nt * TILE

    def body(dz_ref, y_ref, o_ref, ga_ref, pw_ref, gc_ref, wo_ref,
             doT_ref, dga_ref, dpw_ref, dgc_ref, dwo_ref):
        i = pl.program_id(0)

        @pl.when(i == 0)
        def _():
            dwo_ref[...] = jnp.zeros_like(dwo_ref)

        dzb = dz_ref[...].astype(BF16)
        nt_dims = (((1,), (1,)), ((), ()))
        dya = lax.dot_general(dzb, wo_ref[:D_ATTN, :], nt_dims, preferred_element_type=F32)
        dyc = lax.dot_general(dzb, wo_ref[D_ATTN:, :], nt_dims, preferred_element_type=F32)
        sa, sga = _silu_and_grad(ga_ref[...])
        sc, sgc = _silu_and_grad(gc_ref[...])
        doT_ref[0] = (dya * sa).T.astype(BF16)
        dga_ref[...] = (dya * o_ref[...] * sga).astype(BF16)
        dpw_ref[...] = (dyc * sc).astype(BF16)
        dgc_ref[...] = (dyc * pw_ref[...] * sgc).astype(BF16)
        dwo_ref[...] += lax.dot_general(y_ref[...], dzb, (((0,), (0,)), ((), ())), preferred_element_type=F32)

    rm = jax.ShapeDtypeStruct((lp, 512), BF16)
    return pl.pallas_call(
        body, name="out_bwd", grid=(nt,),
        in_specs=[_row_spec(D_MODEL), _row_spec(D_MODEL), _row_spec(512), _row_spec(512), _row_spec(512),
                  _row_spec(512), _full_spec((D_MODEL, D_MODEL))],
        out_specs=[_t3_spec(512), _row_spec(512), _row_spec(512), _row_spec(512), _full_spec((D_MODEL, D_MODEL))],
        out_shape=[jax.ShapeDtypeStruct((nt, 512, TILE), BF16), rm, rm, rm,
                   jax.ShapeDtypeStruct((D_MODEL, D_MODEL), F32)],
        compiler_params=_cp(48, ("arbitrary",)),
    )(dz, y, o, ga, pw, gc, w_out)


def _conv_bwd_ln(dpw, hc, co, w_pw, g, b, nt):
    lp = nt * TILE

    def body(dpw_ref, hc_ref, co_ref, wpw_ref, g_ref, b_ref, dco_ref, dwpw_ref, dg_ref, db_ref, dcb_ref):
        i = pl.program_id(0)

        @pl.when(i == 0)
        def _():
            dwpw_ref[...] = jnp.zeros_like(dwpw_ref)
            dg_ref[...] = jnp.zeros_like(dg_ref)
            db_ref[...] = jnp.zeros_like(db_ref)
            dcb_ref[...] = jnp.zeros_like(dcb_ref)

        dpw_b = dpw_ref[...]
        dhc = lax.dot_general(dpw_b, wpw_ref[...], (((1,), (1,)), ((), ())), preferred_element_type=F32)
        xhat, rstd = _ln_stats(co_ref[...])
        _, sg = _silu_and_grad(xhat * g_ref[...] + b_ref[...])
        dln = dhc * sg
        dg_ref[...] += jnp.sum(dln * xhat, axis=0, keepdims=True)
        db_ref[...] += jnp.sum(dln, axis=0, keepdims=True)
        dco = _ln_bwd(dln, xhat, rstd, g_ref[...])
        dco_ref[...] = dco
        dcb_ref[...] += jnp.sum(dco, axis=0, keepdims=True)
        dwpw_ref[...] += lax.dot_general(hc_ref[...], dpw_b, (((0,), (0,)), ((), ())), preferred_element_type=F32)

    vec = jax.ShapeDtypeStruct((1, D_CONV), F32)
    return pl.pallas_call(
        body, name="conv_bwd_ln", grid=(nt,),
        in_specs=[_row_spec(512), _row_spec(512), _row_spec(512), _full_spec((512, 512)), _full_spec((1, 512)),
                  _full_spec((1, 512))],
        out_specs=[_row_spec(512), _full_spec((512, 512)), _full_spec((1, 512)), _full_spec((1, 512)),
                   _full_spec((1, 512))],
        out_shape=[jax.ShapeDtypeStruct((lp, D_CONV), F32), jax.ShapeDtypeStruct((512, 512), F32), vec, vec, vec],
        compiler_params=_cp(32, ("arbitrary",)),
    )(dpw, hc, co, w_pw, g, b)


def _conv_bwd_taps(dco, u, ug, conv_w, nt):
    lp = nt * TILE

    def body(dco_ref, dcon_ref, u_ref, ug_ref, up_ref, ugp_ref, w3_ref, du_ref, dug_ref, dw_ref, ext, dext, sh, dsh,
             dhg_s, dw_s):
        i = pl.program_id(0)

        @pl.when(i == 0)
        def _():
            dw_s[...] = jnp.zeros_like(dw_s)

        prev = _glu(up_ref[...], ugp_ref[...], i - 1)
        ext[0:HALO, :] = jnp.where(i > 0, prev[TILE - HALO:, :], 0.0)
        ext[HALO:HALO + TILE, :] = _glu(u_ref[...], ug_ref[...], i)
        ext[HALO + TILE:, :] = jnp.zeros((8, D_CONV), F32)
        dext[0:TILE, :] = dco_ref[...]
        dext[TILE:TILE + HALO, :] = jnp.where(i < nt - 1, dcon_ref[0:HALO, :], 0.0)
        dext[TILE + HALO:, :] = jnp.zeros((8, D_CONV), F32)
        _shifted_copies(sh, ext)
        _shifted_copies(dsh, dext)
        stripe = 32

        def stripe_body(rb, carry):
            row0 = pl.multiple_of(rb * stripe, stripe)
            dco = dco_ref[pl.ds(row0, stripe), :]
            dhg = jnp.zeros((stripe, D_CONV), F32)
            for t in range(CONV_WIDTH):
                off = HALO - (CONV_WIDTH - 1) + t
                back = CONV_WIDTH - 1 - t
                prod = dco * sh[off % 8, pl.ds((off // 8) * 8 + row0, stripe), :]
                part = prod[0:8, :]
                for r8 in range(1, stripe // 8):
                    part = part + prod[8 * r8:8 * r8 + 8, :]
                dw_s[t] += part
                dhg = dhg + w3_ref[t] * dsh[back % 8, pl.ds((back // 8) * 8 + row0, stripe), :]
            dhg_s[pl.ds(row0, stripe), :] = dhg
            return carry

        lax.fori_loop(0, TILE // stripe, stripe_body, 0)

        @pl.when(i == nt - 1)
        def _():
            dw_ref[...] = jnp.sum(dw_s[...], axis=1)

        dhg = jnp.where(_row_mask(i, (TILE, D_CONV)), dhg_s[...], 0.0)
        sg = _sigmoid(ug_ref[...])
        du_ref[...] = (dhg * sg).astype(BF16)
        dug_ref[...] = (dhg * u_ref[...] * sg * (1.0 - sg)).astype(BF16)

    rm = jax.ShapeDtypeStruct((lp, D_CONV), BF16)
    nxt = pl.BlockSpec((TILE, 512), lambda i: (jnp.minimum(i + 1, nt - 1), 0))
    ext_t = pltpu.VMEM((EXT_ROWS, D_CONV), F32)
    sh_t = pltpu.VMEM((8, SHIFT_ROWS, D_CONV), F32)
    return pl.pallas_call(
        body, name="conv_bwd_taps", grid=(nt,),
        in_specs=[_row_spec(512), nxt, _row_spec(512), _row_spec(512), _row_spec(512, shift=True),
                  _row_spec(512, shift=True), _full_spec((32, 1, 512))],
        out_specs=[_row_spec(512), _row_spec(512), _full_spec((32, 512))],
        out_shape=[rm, rm, jax.ShapeDtypeStruct((32, D_CONV), F32)],
        scratch_shapes=[ext_t, ext_t, sh_t, sh_t, pltpu.VMEM((TILE, D_CONV), F32), pltpu.VMEM((32, 8, D_CONV), F32)],
        compiler_params=_cp(48, ("arbitrary",)),
    )(dco, dco, u, ug, u, ug, conv_w.reshape(32, 1, D_CONV))


def _cumsum_bwd(dck, dcq4, fl, bf_pad, nt):
    lp = nt * TILE

    def body(dck_ref, dcq_ref, fl_ref, bf_ref, dfl_ref, dbf_ref, carry):
        i = pl.program_id(0)
        tile = nt - 1 - i

        @pl.when(i == 0)
        def _():
            carry[...] = jnp.zeros_like(carry)
            dbf_ref[...] = jnp.zeros_like(dbf_ref)

        dc = jnp.zeros((TILE, LANES), F32)
        for p in range(N_HEADS // 2):
            dq_rows = jnp.concatenate([dcq_ref[p, 0], jnp.zeros((LANES - 8, TILE), F32)], axis=0)
            both = dck_ref[p] + dq_rows.T
            dc = dc + (both if p == 0 else pltpu.roll(both, 2 * p, 1))
        r = lax.broadcasted_iota(jnp.int32, (TILE, TILE), 0)
        c = lax.broadcasted_iota(jnp.int32, (TILE, TILE), 1)
        triu = (c >= r).astype(F32)
        dlf = jnp.dot(triu, dc, precision=lax.Precision.HIGHEST, preferred_element_type=F32) + carry[...]
        carry[...] = dlf[0:1, :]
        z = fl_ref[...] + bf_ref[...]
        lane = lax.broadcasted_iota(jnp.int32, (TILE, LANES), 1)
        dfl = jnp.where(_row_mask(tile, (TILE, LANES)) & (lane < N_HEADS), dlf * _sigmoid(-z), 0.0)
        dfl_ref[...] = dfl.astype(BF16)
        dbf_ref[...] += jnp.sum(dfl, axis=0, keepdims=True)

    rev = lambda i: (nt - 1 - i, 0)
    return pl.pallas_call(
        body, name="cumsum_bwd", grid=(nt,),
        in_specs=[pl.BlockSpec((N_HEADS // 2, TILE, LANES), lambda i: (0, nt - 1 - i, 0)),
                  pl.BlockSpec((N_HEADS // 2, 1, 8, TILE), lambda i: (0, nt - 1 - i, 0, 0)),
                  pl.BlockSpec((TILE, LANES), rev), _full_spec((1, LANES))],
        out_specs=[pl.BlockSpec((TILE, LANES), rev), _full_spec((1, LANES))],
        out_shape=[jax.ShapeDtypeStruct((lp, LANES), BF16), jax.ShapeDtypeStruct((1, LANES), F32)],
        scratch_shapes=[pltpu.VMEM((1, LANES), F32)],
        compiler_params=_cp(32, ("arbitrary",)),
    )(dck, dcq4, fl, bf_pad)


def _dw_rowmajor(hb, secs, nt):
    n = len(secs)

    def body(*refs):
        hb_ref, sec_refs, out_refs = refs[0], refs[1:1 + n], refs[1 + n:]
        i = pl.program_id(0)

        @pl.when(i == 0)
        def _():
            for o_ref in out_refs:
                o_ref[...] = jnp.zeros_like(o_ref)

        hb_t = hb_ref[...]
        for s_ref, o_ref in zip(sec_refs, out_refs):
            o_ref[...] += lax.dot_general(hb_t, s_ref[...], (((0,), (0,)), ((), ())), preferred_element_type=F32)

    return pl.pallas_call(
        body, name="dw_rowmajor", grid=(nt,),
        in_specs=[_row_spec(D_MODEL)] + [_row_spec(s.shape[1]) for s in secs],
        out_specs=[_full_spec((D_MODEL, s.shape[1])) for s in secs],
        out_shape=[jax.ShapeDtypeStruct((D_MODEL, s.shape[1]), F32) for s in secs],
        compiler_params=_cp(48, ("arbitrary",)),
    )(hb, *secs)


def _dw_transposed(hb, secs_t3, nt):
    n = len(secs_t3)

    def body(*refs):
        hb_ref, sec_refs, out_refs = refs[0], refs[1:1 + n], refs[1 + n:]
        i = pl.program_id(0)

        @pl.when(i == 0)
        def _():
            for o_ref in out_refs:
                o_ref[...] = jnp.zeros_like(o_ref)

        hb_t = hb_ref[...]
        for s_ref, o_ref in zip(sec_refs, out_refs):
            o_ref[...] += jnp.dot(s_ref[0], hb_t, preferred_element_type=F32)

    return pl.pallas_call(
        body, name="dw_transposed", grid=(nt,),
        in_specs=[_row_spec(D_MODEL)] + [_t3_spec(512) for _ in secs_t3],
        out_specs=[_full_spec((512, D_MODEL)) for _ in secs_t3],
        out_shape=[jax.ShapeDtypeStruct((512, D_MODEL), F32) for _ in secs_t3],
        compiler_params=_cp(40, ("arbitrary",)),
    )(hb, *secs_t3)


def _dh_bwd(secs, secs_t3, w_rm, w_t, dz, x, metapad, g_in, nt):
    n, m = len(secs), len(secs_t3)
    offs = OFF_GA_R + np.cumsum([0] + [s.shape[1] for s in secs])

    def body(*refs):
        sec_refs, t3_refs = refs[:n], refs[n:n + m]
        wrm_ref, wt_ref, dz_ref, x_ref, mp_ref, g_ref = refs[n + m:n + m + 6]
        dx_ref, dmeta_ref, dg_ref, db_ref = refs[n + m + 6:]
        i = pl.program_id(0)

        @pl.when(i == 0)
        def _():
            dg_ref[...] = jnp.zeros_like(dg_ref)
            db_ref[...] = jnp.zeros_like(db_ref)

        dh = ALPHA * dz_ref[...]
        for s_ref, lo, hi in zip(sec_refs, offs[:-1], offs[1:]):
            dh = dh + lax.dot_general(s_ref[...], wrm_ref[:, lo:hi], (((1,), (1,)), ((), ())),
                                      preferred_element_type=F32)
        for idx, t_ref in enumerate(t3_refs):
            dh = dh + lax.dot_general(t_ref[0], wt_ref[idx * 512:(idx + 1) * 512, :], (((0,), (0,)), ((), ())),
                                      preferred_element_type=F32)
        x0 = jnp.where(i == 0, mp_ref[...], x_ref[...])
        xhat, rstd = _ln_stats(x0)
        dg_ref[...] += jnp.sum(dh * xhat, axis=0, keepdims=True)
        db_ref[...] += jnp.sum(dh, axis=0, keepdims=True)
        dx = _ln_bwd(dh, xhat, rstd, g_ref[...])
        dx_ref[...] = dx

        @pl.when(i == 0)
        def _():
            dmeta_ref[...] = dx

    seq = (nt - 1) * TILE
    return pl.pallas_call(
        body, name="dh_bwd", grid=(nt,),
        in_specs=[_row_spec(s.shape[1]) for s in secs] + [_t3_spec(512) for _ in secs_t3]
        + [_full_spec(w_rm.shape), _full_spec(w_t.shape), _row_spec(D_MODEL), _row_spec(D_MODEL, shift=True),
           _full_spec((TILE, D_MODEL)), _full_spec((1, D_MODEL))],
        out_specs=[_row_spec(D_MODEL, shift=True), _full_spec((TILE, D_MODEL)), _full_spec((1, D_MODEL)),
                   _full_spec((1, D_MODEL))],
        out_shape=[jax.ShapeDtypeStruct((seq, D_MODEL), F32), jax.ShapeDtypeStruct((TILE, D_MODEL), F32),
                   jax.ShapeDtypeStruct((1, D_MODEL), F32), jax.ShapeDtypeStruct((1, D_MODEL), F32)],
        compiler_params=_cp(56, ("arbitrary",)),
    )(*secs, *secs_t3, w_rm, w_t, dz, x, metapad, g_in)


RB = 256
SMALL_ROWS = 48


def _repack_weights(all_in, all_small):
    n_cw = D_CONV // N_DEV

    def body(a_ref, s_ref, wr_ref, wt_ref, mp_ref, cw_ref):
        full = jnp.concatenate([a_ref[d][:, :SHARD_IN] for d in range(N_DEV)], axis=1)
        qkv = full[:, :1536]
        wr_ref[:, :1536] = qkv
        wr_ref[:, 1536:OFF_F_R] = full[:, 1544:]
        wr_ref[:, OFF_F_R:] = jnp.concatenate([full[:, 1536:1544], jnp.zeros((RB, LANES - N_HEADS), BF16)], axis=1)
        wt_ref[...] = qkv.T

        @pl.when(pl.program_id(0) == 0)
        def _():
            mp_ref[0:PAD, :] = jnp.zeros((PAD, D_MODEL), F32)
            mp_ref[PAD:, :] = jnp.concatenate([s_ref[d, 0:N_META, :] for d in range(N_DEV)], axis=1)
            cw_ref[...] = jnp.concatenate([s_ref[d, N_META:, 0:n_cw] for d in range(N_DEV)], axis=1)

    return pl.pallas_call(
        body, name="repack_weights", grid=(D_MODEL // RB,),
        in_specs=[pl.BlockSpec((N_DEV, RB, 512), lambda i: (0, i, 0)), _full_spec((N_DEV, SMALL_ROWS, LANES))],
        out_specs=[pl.BlockSpec((RB, W_COLS), lambda i: (i, 0)), pl.BlockSpec((1536, RB), lambda i: (0, i)),
                   _full_spec((TILE, D_MODEL)), _full_spec((32, D_CONV))],
        out_shape=[jax.ShapeDtypeStruct((D_MODEL, W_COLS), BF16), jax.ShapeDtypeStruct((1536, D_MODEL), BF16),
                   jax.ShapeDtypeStruct((TILE, D_MODEL), F32), jax.ShapeDtypeStruct((32, D_CONV), F32)],
        compiler_params=_cp(40, ("arbitrary",)),
    )(all_in, all_small)


def _unpack_dw_in(dw_rm, dw_t):
    def body(dga_ref, du_ref, dug_ref, dgc_ref, dfl_ref, dq_ref, dk_ref, dv_ref, out_ref):
        full = jnp.concatenate([dq_ref[...].T, dk_ref[...].T, dv_ref[...].T, dfl_ref[:, 0:N_HEADS], dga_ref[...],
                                du_ref[...], dug_ref[...], dgc_ref[...]], axis=1)
        pad = jnp.zeros((RB, 512 - SHARD_IN), F32)
        for d in range(N_DEV):
            out_ref[d] = jnp.concatenate([full[:, SHARD_IN * d:SHARD_IN * (d + 1)], pad], axis=1)

    rm = pl.BlockSpec((RB, 512), lambda i: (i, 0))
    tr = pl.BlockSpec((512, RB), lambda i: (0, i))
    return pl.pallas_call(
        body, name="unpack_dw_in", grid=(D_MODEL // RB,),
        in_specs=[rm, rm, rm, rm, pl.BlockSpec((RB, LANES), lambda i: (i, 0)), tr, tr, tr],
        out_specs=pl.BlockSpec((N_DEV, RB, 512), lambda i: (0, i, 0)),
        out_shape=jax.ShapeDtypeStruct((N_DEV, D_MODEL, 512), F32),
        compiler_params=_cp(48, ("arbitrary",)),
    )(*dw_rm, *dw_t)


def _local_step(x, target, metapad, cw, w_r, w_t, w_pw_full, w_out_full, ln_in_g, ln_in_b, b_f, conv_b, ln_conv_g,
                ln_conv_b, ln_out_g, ln_out_b):
    seq = x.shape[0]
    nt = seq // TILE + 1
    row = lambda a: a.reshape(1, -1).astype(F32)
    bf_pad = jnp.pad(row(b_f), ((0, 0), (0, LANES - N_HEADS)))
    g_in, b_in = row(ln_in_g), row(ln_in_b)
    g_cv, b_cv, c_b = row(ln_conv_g), row(ln_conv_b), row(conv_b)
    g_out, b_out = row(ln_out_g), row(ln_out_b)

    hb, qT3, kT3, vT3, k, v, ga, u, ug, gc, fl = _proj_fwd(x, metapad, g_in, b_in, w_r, nt)
    kx3 = _cumsum_fwd(fl, bf_pad, nt)
    oT3, o, lse4 = _attn_fwd(qT3, k, kx3, vT3, nt)
    co, hc, pw = _conv_fwd(u, ug, cw, c_b, g_cv, b_cv, w_pw_full, nt)
    y, dz, loss, dg_out, db_out = _out_fwd(o, ga, pw, gc, x, metapad, g_in, b_in, w_out_full, g_out, b_out,
                                            target, nt)
    doT3, dga, dpw, dgc, dw_out = _out_bwd(dz, y, o, ga, pw, gc, w_out_full, nt)
    dco, dw_pw, dg_cv, db_cv, dc_b = _conv_bwd_ln(dpw, hc, co, w_pw_full, g_cv, b_cv, nt)
    du, dug, dcw = _conv_bwd_taps(dco, u, ug, cw, nt)
    dqT3, dkT3, dvT3, dck, dcq4 = _attn_bwd(qT3, kT3, k, kx3, v, oT3, doT3, lse4, nt)
    dfl, dbf = _cumsum_bwd(dck, dcq4, fl, bf_pad, nt)
    secs = (dga, du, dug, dgc, dfl)
    secs_t3 = (dqT3, dkT3, dvT3)
    dw_rm = _dw_rowmajor(hb, secs, nt)
    dw_t = _dw_transposed(hb, secs_t3, nt)
    grad_x, dmetapad, dg_in, db_in = _dh_bwd(secs, secs_t3, w_r, w_t, dz, x, metapad, g_in, nt)
    pieces = dict(loss=loss, metapad=dmetapad, ln_in_g=dg_in, ln_in_b=db_in, w_in_rm=dw_rm, w_in_t=dw_t, b_f=dbf,
                  conv_w=dcw, conv_b=dc_b, ln_conv_g=dg_cv, ln_conv_b=db_cv, w_pw=dw_pw, w_out=dw_out,
                  ln_out_g=dg_out, ln_out_b=db_out)
    return grad_x, pieces


MESH = pl.DeviceIdType.MESH
ANY = pl.BlockSpec(memory_space=pl.ANY)


def _mesh_pos():
    return lax.axis_index("x"), lax.axis_index("y"), lax.axis_index("c")


def _all_gather(blks, name):
    n = len(blks)

    def body(*refs):
        x_refs, out_refs = refs[:n], refs[n:2 * n]
        send_sems, recv_sems, local_sems = refs[2 * n:]
        x, y, c = _mesh_pos()
        me, sibling = (x, y, c), (x, y, 1 - c)
        chips = [(1 - x, y), (x, 1 - y), (1 - x, 1 - y)]

        def slot(a, px, py, pc):
            return out_refs[a].at[4 * px + 2 * py + pc]

        def copy(a, k, block, to, src=None):
            return pltpu.make_async_remote_copy(
                src_ref=slot(a, *block) if src is None else src, dst_ref=slot(a, *block),
                send_sem=send_sems.at[7 * a + k], recv_sem=recv_sems.at[7 * a + k], device_id=to,
                device_id_type=MESH)

        arrays = range(n)
        mine = [pltpu.make_async_copy(x_refs[a], slot(a, *me), local_sems.at[a]) for a in arrays]
        for cp in mine:
            cp.start()
        first = [copy(a, 0, me, sibling, src=x_refs[a]) for a in arrays]
        first += [copy(a, 1 + j, me, (*chip, c), src=x_refs[a]) for j, chip in enumerate(chips) for a in arrays]
        for cp in first:
            cp.start()
        passed = []
        for j, chip in enumerate(chips):
            for a in arrays:
                copy(a, 1 + j, (*chip, c), me).wait_recv()
                passed.append(copy(a, 4 + j, (*chip, c), sibling))
                passed[-1].start()
        for a in arrays:
            copy(a, 0, sibling, me).wait_recv()
            for j, chip in enumerate(chips):
                copy(a, 4 + j, (*chip, 1 - c), me).wait_recv()
        for cp in first + passed:
            cp.wait_send()
        for cp in mine:
            cp.wait()

    return pl.pallas_call(
        body, name=name, out_shape=[jax.ShapeDtypeStruct((N_DEV, *b.shape), b.dtype) for b in blks],
        in_specs=[ANY] * n, out_specs=[ANY] * n,
        scratch_shapes=[pltpu.SemaphoreType.DMA((7 * n,)), pltpu.SemaphoreType.DMA((7 * n,)),
                        pltpu.SemaphoreType.DMA((n,))],
    )(*blks)


def _exchange_sibling(g8s):
    n = len(g8s)

    def body(*refs):
        g_refs, out_refs, send_sems, recv_sems = refs[:n], refs[n:2 * n], refs[2 * n], refs[2 * n + 1]
        x, y, c = _mesh_pos()
        cps = [pltpu.make_async_remote_copy(
            src_ref=g_refs[a].at[2 * q + (1 - c)], dst_ref=out_refs[a].at[q], send_sem=send_sems.at[4 * a + q],
            recv_sem=recv_sems.at[4 * a + q], device_id=(x, y, 1 - c), device_id_type=MESH)
            for a in range(n) for q in range(4)]
        for cp in cps:
            cp.start()
        for cp in cps:
            cp.wait()

    return pl.pallas_call(
        body, name="rs_sibling", out_shape=[jax.ShapeDtypeStruct((4, *g.shape[1:]), g.dtype) for g in g8s],
        in_specs=[ANY] * n, out_specs=[ANY] * n,
        scratch_shapes=[pltpu.SemaphoreType.DMA((4 * n,)), pltpu.SemaphoreType.DMA((4 * n,))],
    )(*g8s)


def _exchange_chips(p4s):
    n = len(p4s)

    def body(*refs):
        p_refs, out_refs, send_sems, recv_sems = refs[:n], refs[n:2 * n], refs[2 * n], refs[2 * n + 1]
        x, y, c = _mesh_pos()
        chips = [(1 - x, y), (x, 1 - y), (1 - x, 1 - y)]
        cps = [pltpu.make_async_remote_copy(
            src_ref=p_refs[a].at[2 * cx + cy], dst_ref=out_refs[a].at[k], send_sem=send_sems.at[3 * a + k],
            recv_sem=recv_sems.at[3 * a + k], device_id=(cx, cy, c), device_id_type=MESH)
            for k, (cx, cy) in enumerate(chips) for a in range(n)]
        for cp in cps:
            cp.start()
        for cp in cps:
            cp.wait()

    return pl.pallas_call(
        body, name="rs_chips", out_shape=[jax.ShapeDtypeStruct((3, *p.shape[1:]), p.dtype) for p in p4s],
        in_specs=[ANY] * n, out_specs=[ANY] * n,
        scratch_shapes=[pltpu.SemaphoreType.DMA((3 * n,)), pltpu.SemaphoreType.DMA((3 * n,))],
    )(*p4s)


def _rs_add_sibling(g8s, recvs, c_idx):
    n = len(g8s)

    def body(s_ref, *refs):
        g_refs, r_refs, p32_refs, pb_refs = (refs[k * n:(k + 1) * n] for k in range(4))
        for g_ref, r_ref, p32_ref, pb_ref in zip(g_refs, r_refs, p32_refs, pb_refs):
            p = g_ref[0] + r_ref[0]
            p32_ref[0] = p
            pb_ref[0] = p.astype(BF16)

    blk = lambda g: pl.BlockSpec((1, *g.shape[1:]), lambda q, s: (q, 0, 0))
    grid_spec = pltpu.PrefetchScalarGridSpec(
        num_scalar_prefetch=1, grid=(4,),
        in_specs=[pl.BlockSpec((1, *g.shape[1:]), lambda q, s: (2 * q + s[0], 0, 0)) for g in g8s]
        + [blk(g) for g in g8s],
        out_specs=[blk(g) for g in g8s] * 2)
    outs = pl.pallas_call(
        body, name="rs_add_sibling", grid_spec=grid_spec,
        out_shape=[jax.ShapeDtypeStruct((4, *g.shape[1:]), F32) for g in g8s]
        + [jax.ShapeDtypeStruct((4, *g.shape[1:]), BF16) for g in g8s],
        compiler_params=_cp(48, ("arbitrary",)),
    )(c_idx, *g8s, *recvs)
    return outs[:n], outs[n:]


def _rs_add_chips(p32s, recvs, q_idx):
    def body(s_ref, pin_ref, pout_ref, ppw_ref, rin_ref, rout_ref, rpw_ref, gin_ref, gout_ref, gpw_ref):
        def total(p_ref, r_ref):
            return ((p_ref[0] + r_ref[0].astype(F32)) + r_ref[1].astype(F32)) + r_ref[2].astype(F32)

        gin_ref[0] = total(pin_ref, rin_ref)[:, :SHARD_IN]
        gout_ref[0] = total(pout_ref, rout_ref)
        gpw_ref[0] = total(ppw_ref, rpw_ref)

    own = lambda p: pl.BlockSpec((1, *p.shape[1:]), lambda i, s: (s[0], 0, 0))
    whole = lambda shape: pl.BlockSpec(shape, lambda i, s: (0, 0, 0))
    out_shapes = [(1, D_MODEL, SHARD_IN), (1, *p32s[1].shape[1:]), (1, *p32s[2].shape[1:])]
    grid_spec = pltpu.PrefetchScalarGridSpec(
        num_scalar_prefetch=1, grid=(1,),
        in_specs=[own(p) for p in p32s] + [whole(r.shape) for r in recvs],
        out_specs=[whole(s) for s in out_shapes])
    return pl.pallas_call(
        body, name="rs_add_chips", grid_spec=grid_spec,
        out_shape=[jax.ShapeDtypeStruct(s, F32) for s in out_shapes],
        compiler_params=_cp(48, ("arbitrary",)),
    )(q_idx, *p32s, *recvs)


SMALL_ROWS_G = 64
SMALL_LAYOUT = {
    "metapad": (0, N_META, D_MODEL), "conv_w": (16, 32, D_CONV), "ln_in_g": (48, 1, D_MODEL),
    "ln_in_b": (49, 1, D_MODEL), "b_f": (50, 1, LANES), "conv_b": (51, 1, D_CONV), "ln_conv_g": (52, 1, D_CONV),
    "ln_conv_b": (53, 1, D_CONV), "ln_out_g": (54, 1, D_MODEL), "ln_out_b": (55, 1, D_MODEL), "loss": (56, 1, LANES)}


def _pack_small(pieces):
    names = list(SMALL_LAYOUT)

    def body(*refs):
        out_ref = refs[-1]
        out_ref[...] = jnp.zeros_like(out_ref)
        for name, ref in zip(names, refs[:-1]):
            r0, nr, nl = SMALL_LAYOUT[name]
            src = ref[PAD:, :] if name == "metapad" else ref[...]
            out_ref[r0:r0 + nr, 0:nl] = src

    return pl.pallas_call(body, name="pack_small", out_shape=jax.ShapeDtypeStruct((SMALL_ROWS_G, D_MODEL), F32),
                          compiler_params=_cp(16))(*[pieces[n] for n in names])


def _sum_small(gathered):
    names = list(SMALL_LAYOUT)

    def body(a_ref, *out_refs):
        acc = a_ref[0]
        for d in range(1, N_DEV):
            acc = acc + a_ref[d]
        for name, ref in zip(names, out_refs):
            r0, nr, nl = SMALL_LAYOUT[name]
            ref[...] = acc[r0:r0 + nr, 0:nl]

    outs = pl.pallas_call(
        body, name="sum_small",
        out_shape=[jax.ShapeDtypeStruct(SMALL_LAYOUT[n][1:], F32) for n in names], compiler_params=_cp(16))(gathered)
    return dict(zip(names, outs))


def _adamw(ws, gs, ms, vs):
    n = len(ws)
    c1 = 1.0 - ADAM_B1 ** ADAM_STEP
    c2 = 1.0 - ADAM_B2 ** ADAM_STEP

    def body(*refs):
        w_refs, g_refs, m_refs, v_refs = (refs[k * n:(k + 1) * n] for k in range(4))
        d_refs, nm_refs, nv_refs = (refs[(4 + k) * n:(5 + k) * n] for k in range(3))
        for w_ref, g_ref, m_ref, v_ref, d_ref, nm_ref, nv_ref in zip(w_refs, g_refs, m_refs, v_refs, d_refs,
                                                                     nm_refs, nv_refs):
            g = g_ref[...]
            m = ADAM_B1 * m_ref[...] + (1.0 - ADAM_B1) * g
            v = ADAM_B2 * v_ref[...] + (1.0 - ADAM_B2) * (g * g)
            nm_ref[...] = m
            nv_ref[...] = v
            d_ref[...] = -ADAM_LR * ((m / c1) / (jnp.sqrt(v / c2) + ADAM_EPS) + ADAM_WD * w_ref[...])

    shapes = [jax.ShapeDtypeStruct(w.shape, F32) for w in ws]
    outs = pl.pallas_call(body, name="adamw", out_shape=shapes * 3, compiler_params=_cp(48))(*ws, *gs, *ms, *vs)
    return outs[:n], outs[n:2 * n], outs[2 * n:]


W_NAMES = ("meta", "ln_in_g", "ln_in_b", "w_in", "b_f", "conv_w", "conv_b", "ln_conv_g", "ln_conv_b", "w_pw",
           "w_out", "ln_out_g", "ln_out_b")


def kernel(x, meta, ln_in_g, ln_in_b, w_in, b_f, conv_w, conv_b, ln_conv_g, ln_conv_b, w_pw, w_out, ln_out_g, ln_out_b, loss_target, m_meta, m_ln_in_g, m_ln_in_b, m_w_in, m_b_f, m_conv_w, m_conv_b, m_ln_conv_g, m_ln_conv_b, m_w_pw, m_w_out, m_ln_out_g, m_ln_out_b, v_meta, v_ln_in_g, v_ln_in_b, v_w_in, v_b_f, v_conv_w, v_conv_b, v_ln_conv_g, v_ln_conv_b, v_w_pw, v_w_out, v_ln_out_g, v_ln_out_b):
    mx, my, mc = _mesh_pos()
    me = 4 * mx + 2 * my + mc
    n_meta_sh = D_MODEL // N_DEV
    n_cw_sh = D_CONV // N_DEV
    n_out_sh = D_MODEL // N_DEV
    n_pw_sh = D_CONV // N_DEV

    small_w = jnp.concatenate([meta, jnp.pad(conv_w[0], ((0, 1), (0, LANES - n_cw_sh)))], axis=0)
    all_in, all_out, all_pw, all_small = _all_gather(
        [jnp.pad(w_in[0], ((0, 0), (0, 512 - SHARD_IN))).astype(BF16), w_out[0].astype(BF16), w_pw[0].astype(BF16),
         small_w], "gather_weights")
    w_r, w_t, metapad, cw = _repack_weights(all_in, all_small)
    w_out_full = all_out.reshape(D_MODEL, D_MODEL)
    w_pw_full = all_pw.reshape(D_CONV, D_CONV)

    grad_x, pc = _local_step(x[0], loss_target[0], metapad, cw, w_r, w_t, w_pw_full, w_out_full, ln_in_g, ln_in_b,
                             b_f[0], conv_b[0], ln_conv_g[0], ln_conv_b[0], ln_out_g[0], ln_out_b[0])

    g8s = [_unpack_dw_in(pc["w_in_rm"], pc["w_in_t"]), pc["w_out"].reshape(N_DEV, n_out_sh, D_MODEL),
           pc["w_pw"].reshape(N_DEV, n_pw_sh, D_CONV)]
    from_sibling = _exchange_sibling(g8s)
    p32s, pbs = _rs_add_sibling(g8s, from_sibling, jnp.reshape(mc, (1,)).astype(jnp.int32))
    from_chips = _exchange_chips(pbs)
    g_w_in, g_w_out, g_w_pw = _rs_add_chips(p32s, from_chips, jnp.reshape(2 * mx + my, (1,)).astype(jnp.int32))

    sm = _sum_small(_all_gather([_pack_small(pc)], "gather_small_grads")[0])
    grads = {
        "meta": lax.dynamic_slice_in_dim(sm["metapad"], me * n_meta_sh, n_meta_sh, axis=1),
        "ln_in_g": sm["ln_in_g"].reshape(D_MODEL), "ln_in_b": sm["ln_in_b"].reshape(D_MODEL), "w_in": g_w_in,
        "b_f": sm["b_f"][:, :N_HEADS],
        "conv_w": lax.dynamic_slice_in_dim(sm["conv_w"], me * n_cw_sh, n_cw_sh, axis=1)[None, :CONV_WIDTH],
        "conv_b": sm["conv_b"], "ln_conv_g": sm["ln_conv_g"], "ln_conv_b": sm["ln_conv_b"], "w_pw": g_w_pw,
        "w_out": g_w_out, "ln_out_g": sm["ln_out_g"], "ln_out_b": sm["ln_out_b"]}
    loss_all = sm["loss"][0, 0]

    weights = dict(meta=meta, ln_in_g=ln_in_g, ln_in_b=ln_in_b, w_in=w_in, b_f=b_f, conv_w=conv_w, conv_b=conv_b,
                   ln_conv_g=ln_conv_g, ln_conv_b=ln_conv_b, w_pw=w_pw, w_out=w_out, ln_out_g=ln_out_g,
                   ln_out_b=ln_out_b)
    moms = dict(meta=m_meta, ln_in_g=m_ln_in_g, ln_in_b=m_ln_in_b, w_in=m_w_in, b_f=m_b_f, conv_w=m_conv_w,
                conv_b=m_conv_b, ln_conv_g=m_ln_conv_g, ln_conv_b=m_ln_conv_b, w_pw=m_w_pw, w_out=m_w_out,
                ln_out_g=m_ln_out_g, ln_out_b=m_ln_out_b)
    vels = dict(meta=v_meta, ln_in_g=v_ln_in_g, ln_in_b=v_ln_in_b, w_in=v_w_in, b_f=v_b_f, conv_w=v_conv_w,
                conv_b=v_conv_b, ln_conv_g=v_ln_conv_g, ln_conv_b=v_ln_conv_b, w_pw=v_w_pw, w_out=v_w_out,
                ln_out_g=v_ln_out_g, ln_out_b=v_ln_out_b)

    def as_rows(a):
        return a.reshape(1, -1) if a.ndim == 1 else a

    deltas, new_m, new_v = _adamw([as_rows(weights[n]) for n in W_NAMES], [as_rows(grads[n]) for n in W_NAMES],
                                  [as_rows(moms[n]) for n in W_NAMES], [as_rows(vels[n]) for n in W_NAMES])
    shp = [weights[n].shape for n in W_NAMES]
    return (loss_all, grad_x[None], *[grads[n] for n in W_NAMES],
            *[d.reshape(s) for d, s in zip(deltas, shp)], *[a.reshape(s) for a, s in zip(new_m, shp)],
            *[a.reshape(s) for a, s in zip(new_v, shp)])
```

```python
import jax
import jax.numpy as jnp
import numpy as np
from jax import lax
from jax.experimental import pallas as pl
from jax.experimental.pallas import tpu as pltpu

F32 = jnp.float32
BF16 = jnp.bfloat16

D_MODEL = 1024
D_ATTN = 512
D_CONV = 512
N_HEADS = 8
HEAD_DIM = 64
N_META = 16
CONV_WIDTH = 31
LN_EPS = 1e-5
ALPHA = 2.0 ** 0.25
SCALE = HEAD_DIM ** -0.5
LOG2E = 1.4426950408889634
ADAM_LR, ADAM_B1, ADAM_B2, ADAM_EPS, ADAM_WD, ADAM_STEP = 0.001, 0.9, 0.999, 1e-08, 0.01, 10

N_DEV = 8
D_IN = 3592
SHARD_IN = D_IN // N_DEV
TILE = 256
PAD = TILE - N_META
HALO = 32
SHIFT_ROWS = TILE + HALO
EXT_ROWS = SHIFT_ROWS + 8
NEG = -1e30
LANES = 128
W_COLS = 7 * 512 + LANES
OFF_GA_R, OFF_F_R = 1536, 3584
MIB = 1024 * 1024


def _cp(vmem_mib, sem=None):
    kw = dict(vmem_limit_bytes=vmem_mib * MIB)
    if sem is not None:
        kw["dimension_semantics"] = sem
    return pltpu.CompilerParams(**kw)


def _sigmoid(x):
    return 1.0 / (1.0 + jnp.exp(-x))


def _silu_and_grad(x):
    s = _sigmoid(x)
    return x * s, s * (1.0 + x * (1.0 - s))


def _ln_stats(x):
    mu = jnp.mean(x, axis=-1, keepdims=True)
    xc = x - mu
    var = jnp.mean(xc * xc, axis=-1, keepdims=True)
    rstd = lax.rsqrt(var + LN_EPS)
    return xc * rstd, rstd


def _ln_bwd(dy, xhat, rstd, g):
    dxh = dy * g
    m1 = jnp.mean(dxh, axis=-1, keepdims=True)
    m2 = jnp.mean(dxh * xhat, axis=-1, keepdims=True)
    return rstd * (dxh - m1 - xhat * m2)


def _row_spec(cols, shift=False):
    if shift:
        return pl.BlockSpec((TILE, cols), lambda i: (jnp.maximum(i - 1, 0), 0))
    return pl.BlockSpec((TILE, cols), lambda i: (i, 0))


def _full_spec(shape):
    nd = len(shape)
    return pl.BlockSpec(shape, lambda i: (0,) * nd)


def _t3_spec(ch):
    return pl.BlockSpec((1, ch, TILE), lambda i: (i, 0, 0))


def _proj_fwd(x, metapad, g_in, b_in, w_r, nt):
    lp = nt * TILE

    def body(x_ref, mp_ref, g_ref, b_ref, w_ref, hb_ref, qT_ref, kT_ref, vT_ref, k_ref, v_ref,
             ga_ref, u_ref, ug_ref, gc_ref, fl_ref):
        i = pl.program_id(0)
        x0 = jnp.where(i == 0, mp_ref[...], x_ref[...])
        xhat, _ = _ln_stats(x0)
        hb = (xhat * g_ref[...] + b_ref[...]).astype(BF16)
        hb_ref[...] = hb

        def sec(off, n=512):
            return jnp.dot(hb, w_ref[:, off:off + n], preferred_element_type=F32)

        qT_ref[0] = (sec(0) * (SCALE * LOG2E)).T.astype(BF16)
        k = sec(512)
        kT_ref[0] = k.T.astype(BF16)
        k_ref[...] = k.astype(BF16)
        v = sec(1024)
        vT_ref[0] = v.T.astype(BF16)
        v_ref[...] = v.astype(BF16)
        ga_ref[...] = sec(OFF_GA_R).astype(BF16)
        u_ref[...] = sec(OFF_GA_R + 512).astype(BF16)
        ug_ref[...] = sec(OFF_GA_R + 1024).astype(BF16)
        gc_ref[...] = sec(OFF_GA_R + 1536).astype(BF16)
        fl_ref[...] = sec(OFF_F_R, LANES)

    t3 = jax.ShapeDtypeStruct((nt, 512, TILE), BF16)
    rm = lambda dt: jax.ShapeDtypeStruct((lp, 512), dt)
    return pl.pallas_call(
        body, name="proj_fwd", grid=(nt,),
        in_specs=[_row_spec(D_MODEL, shift=True), _full_spec((TILE, D_MODEL)), _full_spec((1, D_MODEL)),
                  _full_spec((1, D_MODEL)), _full_spec((D_MODEL, W_COLS))],
        out_specs=[_row_spec(D_MODEL), _t3_spec(512), _t3_spec(512), _t3_spec(512), _row_spec(512), _row_spec(512),
                   _row_spec(512), _row_spec(512), _row_spec(512), _row_spec(512), _row_spec(LANES)],
        out_shape=[jax.ShapeDtypeStruct((lp, D_MODEL), BF16), t3, t3, t3, rm(BF16), rm(BF16),
                   rm(BF16), rm(BF16), rm(BF16), rm(BF16), jax.ShapeDtypeStruct((lp, LANES), F32)],
        compiler_params=_cp(56, ("arbitrary",)),
    )(x, metapad, g_in, b_in, w_r)


def _row_mask(i, shape):
    r = lax.broadcasted_iota(jnp.int32, shape, 0)
    return (r >= PAD) | (i > 0)


def _cumsum_fwd(fl, bf_pad, nt):
    lp = nt * TILE

    def body(fl_ref, bf_ref, kx_ref, carry):
        i = pl.program_id(0)

        @pl.when(i == 0)
        def _():
            carry[...] = jnp.zeros_like(carry)

        z = fl_ref[...] + bf_ref[...]
        lf = jnp.minimum(z, 0.0) - jnp.log(1.0 + jnp.exp(-jnp.abs(z)))
        lane = lax.broadcasted_iota(jnp.int32, (TILE, LANES), 1)
        real = _row_mask(i, (TILE, LANES))
        lf = jnp.where(real & (lane < N_HEADS), lf, 0.0)
        r = lax.broadcasted_iota(jnp.int32, (TILE, TILE), 0)
        c = lax.broadcasted_iota(jnp.int32, (TILE, TILE), 1)
        tril = (c <= r).astype(F32)
        cs = jnp.dot(tril, lf, precision=lax.Precision.HIGHEST, preferred_element_type=F32) + carry[...]
        carry[...] = cs[TILE - 1:TILE, :]
        bias = jnp.where(real, cs * (-LOG2E), NEG)
        hi = bias.astype(BF16).astype(F32)
        mid = (bias - hi).astype(BF16).astype(F32)
        lo = (bias - hi - mid).astype(BF16).astype(F32)
        for p in range(N_HEADS // 2):
            out = jnp.zeros((TILE, LANES), F32)
            for hh in range(2):
                for part, piece in enumerate((hi, mid, lo)):
                    dst, src = 3 * hh + part, 2 * p + hh
                    moved = piece if dst == src else pltpu.roll(piece, (dst - src) % LANES, 1)
                    out = jnp.where(lane == dst, moved, out)
            kx_ref[p] = out.astype(BF16)

    return pl.pallas_call(
        body, name="cumsum_fwd", grid=(nt,),
        in_specs=[_row_spec(LANES), _full_spec((1, LANES))],
        out_specs=pl.BlockSpec((N_HEADS // 2, TILE, LANES), lambda i: (0, i, 0)),
        out_shape=jax.ShapeDtypeStruct((N_HEADS // 2, lp, LANES), BF16),
        scratch_shapes=[pltpu.VMEM((1, LANES), F32)],
        compiler_params=_cp(32, ("arbitrary",)),
    )(fl, bf_pad)


def _head_rows(blk, hh):
    r = lax.broadcasted_iota(jnp.int32, blk.shape, 0)
    return jnp.where((r >= hh * HEAD_DIM) & (r < (hh + 1) * HEAD_DIM), blk, jnp.zeros_like(blk))


def _two_heads(blk):
    return jnp.concatenate([_head_rows(blk, 0), _head_rows(blk, 1)], axis=1)


def _bias_rows():
    r = lax.broadcasted_iota(jnp.int32, (LANES, 2 * TILE), 0)
    c = lax.broadcasted_iota(jnp.int32, (LANES, 2 * TILE), 1)
    return jnp.where(((r < 3) & (c < TILE)) | ((r >= 3) & (r < 6) & (c >= TILE)), 1.0, 0.0).astype(BF16)


def _diag_mask(s):
    kpos = lax.broadcasted_iota(jnp.int32, (TILE, TILE), 0)
    qpos = lax.broadcasted_iota(jnp.int32, (TILE, TILE), 1)
    return jnp.where(kpos <= qpos, s, NEG)


def _stream(n, first, nxt, scores, update):
    if n == 0:
        return
    scores(first, 0)

    def pair_body(_, idx):
        idx_b = nxt(idx)
        scores(idx_b, 1)
        update(idx, 0)
        idx_c = nxt(idx_b)
        scores(idx_c, 0)
        update(idx_b, 1)
        return idx_c

    idx = lax.fori_loop(0, (n - 1) // 2, pair_body, first)
    if n % 2 == 1:
        update(idx, 0)
    else:
        idx_b = nxt(idx)
        scores(idx_b, 1)
        update(idx, 0)
        update(idx_b, 1)


def _next_below_diagonal(idx):
    i, j = idx
    wrap = j + 1 >= i
    return jnp.where(wrap, i + 1, i), jnp.where(wrap, 0, j + 1)


def _tile_rows(t):
    return pl.ds(pl.multiple_of(t * TILE, TILE), TILE)


def _attn_fwd(qT3, k, kx3, vT3, nt):
    lp = nt * TILE
    npair = N_HEADS // 2

    def body(qT_ref, k_ref, kx_ref, vT_ref, oT_ref, o_ref, lse_ref, sbuf, m_s, l_s, acc_s):
        ones = _bias_rows()

        def scores(idx, slot):
            i, j = idx
            qcat = jnp.concatenate([_two_heads(qT_ref[i]), ones], axis=0)
            kext = jnp.concatenate([k_ref[_tile_rows(j), :], kx_ref[0, _tile_rows(j), :]], axis=1)
            sbuf[slot] = jnp.dot(kext, qcat, preferred_element_type=F32)

        def update(idx, slot, diag):
            i, j = idx
            for hh in range(2):
                s = sbuf[slot, :, hh * TILE:(hh + 1) * TILE]
                vj = vT_ref[j, hh * HEAD_DIM:(hh + 1) * HEAD_DIM, :]
                if diag:
                    s = _diag_mask(s)
                    m_new = jnp.max(s, axis=0, keepdims=True)
                    p = jnp.exp2(s - m_new)
                    l_s[i, hh] = jnp.sum(p, axis=0, keepdims=True)
                    acc_s[i, hh] = jnp.dot(vj, p.astype(BF16), preferred_element_type=F32)
                else:
                    m_prev = m_s[i, hh]
                    m_new = jnp.maximum(m_prev, jnp.max(s, axis=0, keepdims=True))
                    a = jnp.exp2(m_prev - m_new)
                    p = jnp.exp2(s - m_new)
                    l_s[i, hh] = a * l_s[i, hh] + jnp.sum(p, axis=0, keepdims=True)
                    acc_s[i, hh] = a * acc_s[i, hh] + jnp.dot(vj, p.astype(BF16), preferred_element_type=F32)
                m_s[i, hh] = m_new

        zero = jnp.int32(0)
        _stream(nt, (zero, zero), lambda idx: (idx[0] + 1, idx[1] + 1), scores,
                lambda idx, slot: update(idx, slot, True))
        _stream(nt * (nt - 1) // 2, (zero + 1, zero), _next_below_diagonal, scores,
                lambda idx, slot: update(idx, slot, False))

        def finish(i, carry):
            for hh in range(2):
                l = l_s[i, hh]
                oT_ref[i, hh * HEAD_DIM:(hh + 1) * HEAD_DIM, :] = acc_s[i, hh] / l
                lse_ref[0, i, hh:hh + 1, :] = m_s[i, hh] + jnp.log(l) * LOG2E
            o_ref[_tile_rows(i), :] = oT_ref[i].T.astype(BF16)
            return carry

        lax.fori_loop(0, nt, finish, 0)

    blk_t = pl.BlockSpec((nt, LANES, TILE), lambda p: (0, p, 0))
    blk_rm = pl.BlockSpec((lp, LANES), lambda p: (0, p))
    blk_px = pl.BlockSpec((1, lp, LANES), lambda p: (p, 0, 0))
    blk_st = pl.BlockSpec((1, nt, 8, TILE), lambda p: (p, 0, 0, 0))
    return pl.pallas_call(
        body, name="attn_fwd", grid=(npair,),
        in_specs=[blk_t, blk_rm, blk_px, blk_t],
        out_specs=[blk_t, blk_rm, blk_st],
        out_shape=[jax.ShapeDtypeStruct((nt, D_ATTN, TILE), F32),
                   jax.ShapeDtypeStruct((lp, D_ATTN), BF16),
                   jax.ShapeDtypeStruct((npair, nt, 8, TILE), F32)],
        scratch_shapes=[pltpu.VMEM((2, TILE, 2 * TILE), F32), pltpu.VMEM((nt, 2, 1, TILE), F32),
                        pltpu.VMEM((nt, 2, 1, TILE), F32), pltpu.VMEM((nt, 2, HEAD_DIM, TILE), F32)],
        compiler_params=_cp(56, ("arbitrary",)),
    )(qT3, k, kx3, vT3)


def _attn_bwd(qT3, kT3, k, kx3, v, oT3, doT3, lse4, nt):
    lp = nt * TILE
    npair = N_HEADS // 2

    def body(qT_ref, kT_ref, k_ref, kx_ref, v_ref, oT_ref, doT_ref, lse_ref,
             dqT_ref, dkT_ref, dvT_ref, dck_ref, dcq_ref, sbuf, dpbuf, dq_s, dk_s, dv_s, dc_s):
        ones = _bias_rows()
        nt_dims = (((1,), (1,)), ((), ()))

        def scores(idx, slot):
            i, j = idx
            qcat = jnp.concatenate([_two_heads(qT_ref[i]), ones], axis=0)
            kext = jnp.concatenate([k_ref[_tile_rows(j), :], kx_ref[0, _tile_rows(j), :]], axis=1)
            sbuf[slot] = jnp.dot(kext, qcat, preferred_element_type=F32)
            dpbuf[slot] = jnp.dot(v_ref[_tile_rows(j), :], _two_heads(doT_ref[i]), preferred_element_type=F32)

        def update(idx, slot, diag):
            i, j = idx
            for hh in range(2):
                hs = slice(hh * HEAD_DIM, (hh + 1) * HEAD_DIM)
                s = sbuf[slot, :, hh * TILE:(hh + 1) * TILE]
                if diag:
                    s = _diag_mask(s)
                p = jnp.exp2(s - lse_ref[0, i, hh:hh + 1, :])
                doh = doT_ref[i, hs, :]
                delta = jnp.sum(doh.astype(F32) * oT_ref[i, hs, :], axis=0, keepdims=True)
                ds = p * (dpbuf[slot, :, hh * TILE:(hh + 1) * TILE] - delta)
                dsb = ds.astype(BF16)
                dv = lax.dot_general(doh, p.astype(BF16), nt_dims, preferred_element_type=F32)
                dk = lax.dot_general(qT_ref[i, hs, :], dsb, nt_dims, preferred_element_type=F32)
                dq = jnp.dot(kT_ref[j, hs, :], dsb, preferred_element_type=F32)
                dc = ds[:, :LANES] + ds[:, LANES:]
                dcq = jnp.sum(ds, axis=0, keepdims=True)
                if diag:
                    dv_s[j, hh] = dv
                    dk_s[j, hh] = dk
                    dc_s[j, hh] = dc
                    dq_s[i, hs, :] = dq
                    dcq_ref[0, i, hh:hh + 1, :] = dcq
                else:
                    dv_s[j, hh] += dv
                    dk_s[j, hh] += dk
                    dc_s[j, hh] += dc
                    dq_s[i, hs, :] += dq
                    dcq_ref[0, i, hh:hh + 1, :] += dcq

        dcq_ref[...] = jnp.zeros_like(dcq_ref)
        zero = jnp.int32(0)
        _stream(nt, (zero, zero), lambda idx: (idx[0] + 1, idx[1] + 1), scores,
                lambda idx, slot: update(idx, slot, True))
        _stream(nt * (nt - 1) // 2, (zero + 1, zero), _next_below_diagonal, scores,
                lambda idx, slot: update(idx, slot, False))

        lane = lax.broadcasted_iota(jnp.int32, (TILE, LANES), 1)

        def finish(t, carry):
            dck = jnp.zeros((TILE, LANES), F32)
            for hh in range(2):
                hs = slice(hh * HEAD_DIM, (hh + 1) * HEAD_DIM)
                dkT_ref[t, hs, :] = (dk_s[t, hh] * (1.0 / LOG2E)).astype(BF16)
                dvT_ref[t, hs, :] = dv_s[t, hh].astype(BF16)
                dck = jnp.where(lane == hh, -jnp.sum(dc_s[t, hh], axis=1, keepdims=True), dck)
            dck_ref[0, _tile_rows(t), :] = dck
            dqT_ref[t] = (dq_s[t] * SCALE).astype(BF16)
            return carry

        lax.fori_loop(0, nt, finish, 0)

    blk_t = pl.BlockSpec((nt, LANES, TILE), lambda p: (0, p, 0))
    blk_rm = pl.BlockSpec((lp, LANES), lambda p: (0, p))
    blk_px = pl.BlockSpec((1, lp, LANES), lambda p: (p, 0, 0))
    blk_st = pl.BlockSpec((1, nt, 8, TILE), lambda p: (p, 0, 0, 0))
    t3 = jax.ShapeDtypeStruct((nt, D_ATTN, TILE), BF16)
    return pl.pallas_call(
        body, name="attn_bwd", grid=(npair,),
        in_specs=[blk_t, blk_t, blk_rm, blk_px, blk_rm, blk_t, blk_t, blk_st],
        out_specs=[blk_t, blk_t, blk_t, blk_px, blk_st],
        out_shape=[t3, t3, t3, jax.ShapeDtypeStruct((npair, lp, LANES), F32),
                   jax.ShapeDtypeStruct((npair, nt, 8, TILE), F32)],
        scratch_shapes=[pltpu.VMEM((2, TILE, 2 * TILE), F32), pltpu.VMEM((2, TILE, 2 * TILE), F32),
                        pltpu.VMEM((nt, LANES, TILE), F32), pltpu.VMEM((nt, 2, HEAD_DIM, TILE), F32),
                        pltpu.VMEM((nt, 2, HEAD_DIM, TILE), F32), pltpu.VMEM((nt, 2, TILE, LANES), F32)],
        compiler_params=_cp(60, ("arbitrary",)),
    )(qT3, kT3, k, kx3, v, oT3, doT3, lse4)


def _glu(u, ug, i):
    return jnp.where(_row_mask(i, u.shape), u.astype(F32) * _sigmoid(ug.astype(F32)), 0.0)


def _shifted_copies(dst, src):
    for ph in range(8):
        dst[ph] = src[ph:ph + SHIFT_ROWS, :]


def _tap_window(sh, off, lanes, row0=0, rows=TILE):
    base = (off // 8) * 8 + row0
    return sh[off % 8, base:base + rows, lanes]


def _conv_fwd(u, ug, conv_w, conv_b, g, b, w_pw, nt):
    lp = nt * TILE

    def body(u_ref, ug_ref, up_ref, ugp_ref, w_ref, cb_ref, g_ref, b_ref, wpw_ref,
             co_ref, hc_ref, pw_ref, ext, sh):
        i = pl.program_id(0)
        prev = _glu(up_ref[...], ugp_ref[...], i - 1)
        ext[0:HALO, :] = jnp.where(i > 0, prev[TILE - HALO:, :], 0.0)
        ext[HALO:HALO + TILE, :] = _glu(u_ref[...], ug_ref[...], i)
        ext[HALO + TILE:, :] = jnp.zeros((8, D_CONV), F32)
        _shifted_copies(sh, ext)
        for lb in range(D_CONV // LANES):
            lanes = slice(lb * LANES, (lb + 1) * LANES)
            acc = jnp.zeros((TILE, LANES), F32) + cb_ref[:, lanes]
            for t in range(CONV_WIDTH):
                off = HALO - (CONV_WIDTH - 1) + t
                acc = acc + w_ref[t:t + 1, lanes] * _tap_window(sh, off, lanes)
            co_ref[:, lanes] = acc
        xhat, _ = _ln_stats(co_ref[...])
        a, _ = _silu_and_grad(xhat * g_ref[...] + b_ref[...])
        hc = a.astype(BF16)
        hc_ref[...] = hc
        pw_ref[...] = jnp.dot(hc, wpw_ref[...], preferred_element_type=F32).astype(BF16)

    rm = lambda dt: jax.ShapeDtypeStruct((lp, D_CONV), dt)
    return pl.pallas_call(
        body, name="conv_fwd", grid=(nt,),
        in_specs=[_row_spec(512), _row_spec(512), _row_spec(512, shift=True), _row_spec(512, shift=True),
                  _full_spec((32, 512)), _full_spec((1, 512)), _full_spec((1, 512)), _full_spec((1, 512)),
                  _full_spec((512, 512))],
        out_specs=[_row_spec(512), _row_spec(512), _row_spec(512)],
        out_shape=[rm(F32), rm(BF16), rm(BF16)],
        scratch_shapes=[pltpu.VMEM((EXT_ROWS, D_CONV), F32), pltpu.VMEM((8, SHIFT_ROWS, D_CONV), F32)],
        compiler_params=_cp(40, ("arbitrary",)),
    )(u, ug, u, ug, conv_w, conv_b, g, b, w_pw)


def _out_fwd(o, ga, pw, gc, x, metapad, g_in, b_in, w_out, g_out, b_out, target, nt):
    lp = nt * TILE

    def body(o_ref, ga_ref, pw_ref, gc_ref, x_ref, mp_ref, gi_ref, bi_ref, wo_ref, go_ref, bo_ref, t_ref,
             y_ref, dz_ref, loss_ref, dgo_ref, dbo_ref):
        i = pl.program_id(0)

        @pl.when(i == 0)
        def _():
            loss_ref[...] = jnp.zeros_like(loss_ref)
            dgo_ref[...] = jnp.zeros_like(dgo_ref)
            dbo_ref[...] = jnp.zeros_like(dbo_ref)

        x0 = jnp.where(i == 0, mp_ref[...], x_ref[...])
        xhat, _ = _ln_stats(x0)
        h = xhat * gi_ref[...] + bi_ref[...]
        ya, _ = _silu_and_grad(ga_ref[...].astype(F32))
        yc, _ = _silu_and_grad(gc_ref[...].astype(F32))
        ya = (o_ref[...].astype(F32) * ya).astype(BF16)
        yc = (pw_ref[...].astype(F32) * yc).astype(BF16)
        y_ref[:, :D_ATTN] = ya
        y_ref[:, D_ATTN:] = yc
        z = ALPHA * h + jnp.dot(ya, wo_ref[:D_ATTN, :], preferred_element_type=F32) \
            + jnp.dot(yc, wo_ref[D_ATTN:, :], preferred_element_type=F32)
        zhat, rstd = _ln_stats(z)
        out = zhat * go_ref[...] + bo_ref[...]
        live = (i > 0).astype(F32)
        err = (out - t_ref[...]) * live
        dout = err * (1.0 / D_MODEL)
        loss_ref[...] += 0.5 * jnp.sum(jnp.sum(err * dout, axis=0, keepdims=True), axis=1, keepdims=True)
        dgo_ref[...] += jnp.sum(dout * zhat, axis=0, keepdims=True)
        dbo_ref[...] += jnp.sum(dout, axis=0, keepdims=True)
        dz_ref[...] = _ln_bwd(dout, zhat, rstd, go_ref[...])

    return pl.pallas_call(
        body, name="out_fwd", grid=(nt,),
        in_specs=[_row_spec(512), _row_spec(512), _row_spec(512), _row_spec(512),
                  _row_spec(D_MODEL, shift=True), _full_spec((TILE, D_MODEL)), _full_spec((1, D_MODEL)),
                  _full_spec((1, D_MODEL)), _full_spec((D_MODEL, D_MODEL)), _full_spec((1, D_MODEL)),
                  _full_spec((1, D_MODEL)), _row_spec(D_MODEL, shift=True)],
        out_specs=[_row_spec(D_MODEL), _row_spec(D_MODEL), _full_spec((1, LANES)), _full_spec((1, D_MODEL)),
                   _full_spec((1, D_MODEL))],
        out_shape=[jax.ShapeDtypeStruct((lp, D_MODEL), BF16), jax.ShapeDtypeStruct((lp, D_MODEL), F32),
                   jax.ShapeDtypeStruct((1, LANES), F32), jax.ShapeDtypeStruct((1, D_MODEL), F32),
                   jax.ShapeDtypeStruct((1, D_MODEL), F32)],
        compiler_params=_cp(40, ("arbitrary",)),
    )(o, ga, pw, gc, x, metapad, g_in, b_in, w_out, g_out, b_out, target)


def _out_bwd(dz, y, o, ga, pw, gc, w_out, nt):
    lp = nt * TILE

    def body(dz_ref, y_ref, o_ref, ga_ref, pw_ref, gc_ref, wo_ref,
             doT_ref, dga_ref, dpw_ref, dgc_ref, dwo_ref):
        i = pl.program_id(0)

        @pl.when(i == 0)
        def _():
            dwo_ref[...] = jnp.zeros_like(dwo_ref)

        dzb = dz_ref[...].astype(BF16)
        nt_dims = (((1,), (1,)), ((), ()))
        dya = lax.dot_general(dzb, wo_ref[:D_ATTN, :], nt_dims, preferred_element_type=F32)
        dyc = lax.dot_general(dzb, wo_ref[D_ATTN:, :], nt_dims, preferred_element_type=F32)
        sa, sga = _silu_and_grad(ga_ref[...].astype(F32))
        sc, sgc = _silu_and_grad(gc_ref[...].astype(F32))
        doT_ref[0] = (dya * sa).T.astype(BF16)
        dga_ref[...] = (dya * o_ref[...].astype(F32) * sga).astype(BF16)
        dpw_ref[...] = (dyc * sc).astype(BF16)
        dgc_ref[...] = (dyc * pw_ref[...].astype(F32) * sgc).astype(BF16)
        dwo_ref[...] += lax.dot_general(y_ref[...], dzb, (((0,), (0,)), ((), ())), preferred_element_type=F32)

    rm = jax.ShapeDtypeStruct((lp, 512), BF16)
    return pl.pallas_call(
        body, name="out_bwd", grid=(nt,),
        in_specs=[_row_spec(D_MODEL), _row_spec(D_MODEL), _row_spec(512), _row_spec(512), _row_spec(512),
                  _row_spec(512), _full_spec((D_MODEL, D_MODEL))],
        out_specs=[_t3_spec(512), _row_spec(512), _row_spec(512), _row_spec(512), _full_spec((D_MODEL, D_MODEL))],
        out_shape=[jax.ShapeDtypeStruct((nt, 512, TILE), BF16), rm, rm, rm,
                   jax.ShapeDtypeStruct((D_MODEL, D_MODEL), F32)],
        compiler_params=_cp(48, ("arbitrary",)),
    )(dz, y, o, ga, pw, gc, w_out)


def _conv_bwd_ln(dpw, hc, co, w_pw, g, b, nt):
    lp = nt * TILE

    def body(dpw_ref, hc_ref, co_ref, wpw_ref, g_ref, b_ref, dco_ref, dwpw_ref, dg_ref, db_ref, dcb_ref):
        i = pl.program_id(0)

        @pl.when(i == 0)
        def _():
            dwpw_ref[...] = jnp.zeros_like(dwpw_ref)
            dg_ref[...] = jnp.zeros_like(dg_ref)
            db_ref[...] = jnp.zeros_like(db_ref)
            dcb_ref[...] = jnp.zeros_like(dcb_ref)

        dpw_b = dpw_ref[...]
        dhc = lax.dot_general(dpw_b, wpw_ref[...], (((1,), (1,)), ((), ())), preferred_element_type=F32)
        xhat, rstd = _ln_stats(co_ref[...])
        _, sg = _silu_and_grad(xhat * g_ref[...] + b_ref[...])
        dln = dhc * sg
        dg_ref[...] += jnp.sum(dln * xhat, axis=0, keepdims=True)
        db_ref[...] += jnp.sum(dln, axis=0, keepdims=True)
        dco = _ln_bwd(dln, xhat, rstd, g_ref[...])
        dco_ref[...] = dco
        dcb_ref[...] += jnp.sum(dco, axis=0, keepdims=True)
        dwpw_ref[...] += lax.dot_general(hc_ref[...], dpw_b, (((0,), (0,)), ((), ())), preferred_element_type=F32)

    vec = jax.ShapeDtypeStruct((1, D_CONV), F32)
    return pl.pallas_call(
        body, name="conv_bwd_ln", grid=(nt,),
        in_specs=[_row_spec(512), _row_spec(512), _row_spec(512), _full_spec((512, 512)), _full_spec((1, 512)),
                  _full_spec((1, 512))],
        out_specs=[_row_spec(512), _full_spec((512, 512)), _full_spec((1, 512)), _full_spec((1, 512)),
                   _full_spec((1, 512))],
        out_shape=[jax.ShapeDtypeStruct((lp, D_CONV), F32), jax.ShapeDtypeStruct((512, 512), F32), vec, vec, vec],
        compiler_params=_cp(32, ("arbitrary",)),
    )(dpw, hc, co, w_pw, g, b)


def _conv_bwd_taps(dco, u, ug, conv_w, nt):
    lp = nt * TILE

    def body(dco_ref, dcon_ref, u_ref, ug_ref, up_ref, ugp_ref, w3_ref, du_ref, dug_ref, dw_ref, ext, dext, sh, dsh,
             dhg_s, dw_s):
        i = pl.program_id(0)

        @pl.when(i == 0)
        def _():
            dw_s[...] = jnp.zeros_like(dw_s)

        prev = _glu(up_ref[...], ugp_ref[...], i - 1)
        ext[0:HALO, :] = jnp.where(i > 0, prev[TILE - HALO:, :], 0.0)
        ext[HALO:HALO + TILE, :] = _glu(u_ref[...], ug_ref[...], i)
        ext[HALO + TILE:, :] = jnp.zeros((8, D_CONV), F32)
        dext[0:TILE, :] = dco_ref[...]
        dext[TILE:TILE + HALO, :] = jnp.where(i < nt - 1, dcon_ref[0:HALO, :], 0.0)
        dext[TILE + HALO:, :] = jnp.zeros((8, D_CONV), F32)
        _shifted_copies(sh, ext)
        _shifted_copies(dsh, dext)
        stripe = 32

        def stripe_body(rb, carry):
            row0 = pl.multiple_of(rb * stripe, stripe)
            dco = dco_ref[pl.ds(row0, stripe), :]
            dhg = jnp.zeros((stripe, D_CONV), F32)
            for t in range(CONV_WIDTH):
                off = HALO - (CONV_WIDTH - 1) + t
                back = CONV_WIDTH - 1 - t
                prod = dco * sh[off % 8, pl.ds((off // 8) * 8 + row0, stripe), :]
                part = prod[0:8, :]
                for r8 in range(1, stripe // 8):
                    part = part + prod[8 * r8:8 * r8 + 8, :]
                dw_s[t] += part
                dhg = dhg + w3_ref[t] * dsh[back % 8, pl.ds((back // 8) * 8 + row0, stripe), :]
            dhg_s[pl.ds(row0, stripe), :] = dhg
            return carry

        lax.fori_loop(0, TILE // stripe, stripe_body, 0)

        @pl.when(i == nt - 1)
        def _():
            dw_ref[...] = jnp.sum(dw_s[...], axis=1)

        dhg = jnp.where(_row_mask(i, (TILE, D_CONV)), dhg_s[...], 0.0)
        sg = _sigmoid(ug_ref[...].astype(F32))
        du_ref[...] = (dhg * sg).astype(BF16)
        dug_ref[...] = (dhg * u_ref[...].astype(F32) * sg * (1.0 - sg)).astype(BF16)

    rm = jax.ShapeDtypeStruct((lp, D_CONV), BF16)
    nxt = pl.BlockSpec((TILE, 512), lambda i: (jnp.minimum(i + 1, nt - 1), 0))
    ext_t = pltpu.VMEM((EXT_ROWS, D_CONV), F32)
    sh_t = pltpu.VMEM((8, SHIFT_ROWS, D_CONV), F32)
    return pl.pallas_call(
        body, name="conv_bwd_taps", grid=(nt,),
        in_specs=[_row_spec(512), nxt, _row_spec(512), _row_spec(512), _row_spec(512, shift=True),
                  _row_spec(512, shift=True), _full_spec((32, 1, 512))],
        out_specs=[_row_spec(512), _row_spec(512), _full_spec((32, 512))],
        out_shape=[rm, rm, jax.ShapeDtypeStruct((32, D_CONV), F32)],
        scratch_shapes=[ext_t, ext_t, sh_t, sh_t, pltpu.VMEM((TILE, D_CONV), F32), pltpu.VMEM((32, 8, D_CONV), F32)],
        compiler_params=_cp(48, ("arbitrary",)),
    )(dco, dco, u, ug, u, ug, conv_w.reshape(32, 1, D_CONV))


def _cumsum_bwd(dck, dcq4, fl, bf_pad, nt):
    lp = nt * TILE

    def body(dck_ref, dcq_ref, fl_ref, bf_ref, dfl_ref, dbf_ref, carry):
        i = pl.program_id(0)
        tile = nt - 1 - i

        @pl.when(i == 0)
        def _():
            carry[...] = jnp.zeros_like(carry)
            dbf_ref[...] = jnp.zeros_like(dbf_ref)

        dc = jnp.zeros((TILE, LANES), F32)
        for p in range(N_HEADS // 2):
            dq_rows = jnp.concatenate([dcq_ref[p, 0], jnp.zeros((LANES - 8, TILE), F32)], axis=0)
            both = dck_ref[p] + dq_rows.T
            dc = dc + (both if p == 0 else pltpu.roll(both, 2 * p, 1))
        r = lax.broadcasted_iota(jnp.int32, (TILE, TILE), 0)
        c = lax.broadcasted_iota(jnp.int32, (TILE, TILE), 1)
        triu = (c >= r).astype(F32)
        dlf = jnp.dot(triu, dc, precision=lax.Precision.HIGHEST, preferred_element_type=F32) + carry[...]
        carry[...] = dlf[0:1, :]
        z = fl_ref[...] + bf_ref[...]
        lane = lax.broadcasted_iota(jnp.int32, (TILE, LANES), 1)
        dfl = jnp.where(_row_mask(tile, (TILE, LANES)) & (lane < N_HEADS), dlf * _sigmoid(-z), 0.0)
        dfl_ref[...] = dfl.astype(BF16)
        dbf_ref[...] += jnp.sum(dfl, axis=0, keepdims=True)

    rev = lambda i: (nt - 1 - i, 0)
    return pl.pallas_call(
        body, name="cumsum_bwd", grid=(nt,),
        in_specs=[pl.BlockSpec((N_HEADS // 2, TILE, LANES), lambda i: (0, nt - 1 - i, 0)),
                  pl.BlockSpec((N_HEADS // 2, 1, 8, TILE), lambda i: (0, nt - 1 - i, 0, 0)),
                  pl.BlockSpec((TILE, LANES), rev), _full_spec((1, LANES))],
        out_specs=[pl.BlockSpec((TILE, LANES), rev), _full_spec((1, LANES))],
        out_shape=[jax.ShapeDtypeStruct((lp, LANES), BF16), jax.ShapeDtypeStruct((1, LANES), F32)],
        scratch_shapes=[pltpu.VMEM((1, LANES), F32)],
        compiler_params=_cp(32, ("arbitrary",)),
    )(dck, dcq4, fl, bf_pad)


def _dw_rowmajor(hb, secs, nt):
    n = len(secs)

    def body(*refs):
        hb_ref, sec_refs, out_refs = refs[0], refs[1:1 + n], refs[1 + n:]
        i = pl.program_id(0)

        @pl.when(i == 0)
        def _():
            for o_ref in out_refs:
                o_ref[...] = jnp.zeros_like(o_ref)

        hb_t = hb_ref[...]
        for s_ref, o_ref in zip(sec_refs, out_refs):
            o_ref[...] += lax.dot_general(hb_t, s_ref[...], (((0,), (0,)), ((), ())), preferred_element_type=F32)

    return pl.pallas_call(
        body, name="dw_rowmajor", grid=(nt,),
        in_specs=[_row_spec(D_MODEL)] + [_row_spec(s.shape[1]) for s in secs],
        out_specs=[_full_spec((D_MODEL, s.shape[1])) for s in secs],
        out_shape=[jax.ShapeDtypeStruct((D_MODEL, s.shape[1]), F32) for s in secs],
        compiler_params=_cp(48, ("arbitrary",)),
    )(hb, *secs)


def _dw_transposed(hb, secs_t3, nt):
    n = len(secs_t3)

    def body(*refs):
        hb_ref, sec_refs, out_refs = refs[0], refs[1:1 + n], refs[1 + n:]
        i = pl.program_id(0)

        @pl.when(i == 0)
        def _():
            for o_ref in out_refs:
                o_ref[...] = jnp.zeros_like(o_ref)

        hb_t = hb_ref[...]
        for s_ref, o_ref in zip(sec_refs, out_refs):
            o_ref[...] += jnp.dot(s_ref[0], hb_t, preferred_element_type=F32)

    return pl.pallas_call(
        body, name="dw_transposed", grid=(nt,),
        in_specs=[_row_spec(D_MODEL)] + [_t3_spec(512) for _ in secs_t3],
        out_specs=[_full_spec((512, D_MODEL)) for _ in secs_t3],
        out_shape=[jax.ShapeDtypeStruct((512, D_MODEL), F32) for _ in secs_t3],
        compiler_params=_cp(40, ("arbitrary",)),
    )(hb, *secs_t3)


def _dh_bwd(secs, secs_t3, w_rm, w_t, dz, x, metapad, g_in, nt):
    n, m = len(secs), len(secs_t3)
    offs = OFF_GA_R + np.cumsum([0] + [s.shape[1] for s in secs])

    def body(*refs):
        sec_refs, t3_refs = refs[:n], refs[n:n + m]
        wrm_ref, wt_ref, dz_ref, x_ref, mp_ref, g_ref = refs[n + m:n + m + 6]
        dx_ref, dmeta_ref, dg_ref, db_ref = refs[n + m + 6:]
        i = pl.program_id(0)

        @pl.when(i == 0)
        def _():
            dg_ref[...] = jnp.zeros_like(dg_ref)
            db_ref[...] = jnp.zeros_like(db_ref)

        dh = ALPHA * dz_ref[...]
        for s_ref, lo, hi in zip(sec_refs, offs[:-1], offs[1:]):
            dh = dh + lax.dot_general(s_ref[...], wrm_ref[:, lo:hi], (((1,), (1,)), ((), ())),
                                      preferred_element_type=F32)
        for idx, t_ref in enumerate(t3_refs):
            dh = dh + lax.dot_general(t_ref[0], wt_ref[idx * 512:(idx + 1) * 512, :], (((0,), (0,)), ((), ())),
                                      preferred_element_type=F32)
        x0 = jnp.where(i == 0, mp_ref[...], x_ref[...])
        xhat, rstd = _ln_stats(x0)
        dg_ref[...] += jnp.sum(dh * xhat, axis=0, keepdims=True)
        db_ref[...] += jnp.sum(dh, axis=0, keepdims=True)
        dx = _ln_bwd(dh, xhat, rstd, g_ref[...])
        dx_ref[...] = dx

        @pl.when(i == 0)
        def _():
            dmeta_ref[...] = dx

    seq = (nt - 1) * TILE
    return pl.pallas_call(
        body, name="dh_bwd", grid=(nt,),
        in_specs=[_row_spec(s.shape[1]) for s in secs] + [_t3_spec(512) for _ in secs_t3]
        + [_full_spec(w_rm.shape), _full_spec(w_t.shape), _row_spec(D_MODEL), _row_spec(D_MODEL, shift=True),
           _full_spec((TILE, D_MODEL)), _full_spec((1, D_MODEL))],
        out_specs=[_row_spec(D_MODEL, shift=True), _full_spec((TILE, D_MODEL)), _full_spec((1, D_MODEL)),
                   _full_spec((1, D_MODEL))],
        out_shape=[jax.ShapeDtypeStruct((seq, D_MODEL), F32), jax.ShapeDtypeStruct((TILE, D_MODEL), F32),
                   jax.ShapeDtypeStruct((1, D_MODEL), F32), jax.ShapeDtypeStruct((1, D_MODEL), F32)],
        compiler_params=_cp(56, ("arbitrary",)),
    )(*secs, *secs_t3, w_rm, w_t, dz, x, metapad, g_in)


RB = 256
SMALL_ROWS = 48


def _repack_weights(all_in, all_small):
    n_cw = D_CONV // N_DEV

    def body(a_ref, s_ref, wr_ref, wt_ref, mp_ref, cw_ref):
        full = jnp.concatenate([a_ref[d].T[:, :SHARD_IN] for d in range(N_DEV)], axis=1)
        qkv = full[:, :1536]
        wr_ref[:, :1536] = qkv
        wr_ref[:, 1536:OFF_F_R] = full[:, 1544:]
        wr_ref[:, OFF_F_R:] = jnp.concatenate([full[:, 1536:1544], jnp.zeros((RB, LANES - N_HEADS), BF16)], axis=1)
        wt_ref[...] = qkv.T

        @pl.when(pl.program_id(0) == 0)
        def _():
            mp_ref[0:PAD, :] = jnp.zeros((PAD, D_MODEL), F32)
            mp_ref[PAD:, :] = jnp.concatenate([s_ref[d, 0:N_META, :] for d in range(N_DEV)], axis=1)
            cw_ref[...] = jnp.concatenate([s_ref[d, N_META:, 0:n_cw] for d in range(N_DEV)], axis=1)

    return pl.pallas_call(
        body, name="repack_weights", grid=(D_MODEL // RB,),
        in_specs=[pl.BlockSpec((N_DEV, 512, RB), lambda i: (0, 0, i)), _full_spec((N_DEV, SMALL_ROWS, LANES))],
        out_specs=[pl.BlockSpec((RB, W_COLS), lambda i: (i, 0)), pl.BlockSpec((1536, RB), lambda i: (0, i)),
                   _full_spec((TILE, D_MODEL)), _full_spec((32, D_CONV))],
        out_shape=[jax.ShapeDtypeStruct((D_MODEL, W_COLS), BF16), jax.ShapeDtypeStruct((1536, D_MODEL), BF16),
                   jax.ShapeDtypeStruct((TILE, D_MODEL), F32), jax.ShapeDtypeStruct((32, D_CONV), F32)],
        compiler_params=_cp(40, ("arbitrary",)),
    )(all_in, all_small)


def _unpack_dw_in(dw_rm, dw_t):
    def body(dga_ref, du_ref, dug_ref, dgc_ref, dfl_ref, dq_ref, dk_ref, dv_ref, out_ref):
        full = jnp.concatenate([dq_ref[...].T, dk_ref[...].T, dv_ref[...].T, dfl_ref[:, 0:N_HEADS], dga_ref[...],
                                du_ref[...], dug_ref[...], dgc_ref[...]], axis=1)
        pad = jnp.zeros((RB, 512 - SHARD_IN), F32)
        for d in range(N_DEV):
            out_ref[d] = jnp.concatenate([full[:, SHARD_IN * d:SHARD_IN * (d + 1)], pad], axis=1).T

    rm = pl.BlockSpec((RB, 512), lambda i: (i, 0))
    tr = pl.BlockSpec((512, RB), lambda i: (0, i))
    return pl.pallas_call(
        body, name="unpack_dw_in", grid=(D_MODEL // RB,),
        in_specs=[rm, rm, rm, rm, pl.BlockSpec((RB, LANES), lambda i: (i, 0)), tr, tr, tr],
        out_specs=pl.BlockSpec((N_DEV, 512, RB), lambda i: (0, 0, i)),
        out_shape=jax.ShapeDtypeStruct((N_DEV, 512, D_MODEL), F32),
        compiler_params=_cp(48, ("arbitrary",)),
    )(*dw_rm, *dw_t)


def _local_step(x, target, metapad, cw, w_r, w_t, w_pw_full, w_out_full, ln_in_g, ln_in_b, b_f, conv_b, ln_conv_g,
                ln_conv_b, ln_out_g, ln_out_b):
    seq = x.shape[0]
    nt = seq // TILE + 1
    row = lambda a: a.reshape(1, -1).astype(F32)
    bf_pad = jnp.pad(row(b_f), ((0, 0), (0, LANES - N_HEADS)))
    g_in, b_in = row(ln_in_g), row(ln_in_b)
    g_cv, b_cv, c_b = row(ln_conv_g), row(ln_conv_b), row(conv_b)
    g_out, b_out = row(ln_out_g), row(ln_out_b)

    hb, qT3, kT3, vT3, k, v, ga, u, ug, gc, fl = _proj_fwd(x, metapad, g_in, b_in, w_r, nt)
    kx3 = _cumsum_fwd(fl, bf_pad, nt)
    oT3, o, lse4 = _attn_fwd(qT3, k, kx3, vT3, nt)
    co, hc, pw = _conv_fwd(u, ug, cw, c_b, g_cv, b_cv, w_pw_full, nt)
    y, dz, loss, dg_out, db_out = _out_fwd(o, ga, pw, gc, x, metapad, g_in, b_in, w_out_full, g_out, b_out,
                                            target, nt)
    doT3, dga, dpw, dgc, dw_out = _out_bwd(dz, y, o, ga, pw, gc, w_out_full, nt)
    dco, dw_pw, dg_cv, db_cv, dc_b = _conv_bwd_ln(dpw, hc, co, w_pw_full, g_cv, b_cv, nt)
    du, dug, dcw = _conv_bwd_taps(dco, u, ug, cw, nt)
    dqT3, dkT3, dvT3, dck, dcq4 = _attn_bwd(qT3, kT3, k, kx3, v, oT3, doT3, lse4, nt)
    dfl, dbf = _cumsum_bwd(dck, dcq4, fl, bf_pad, nt)
    secs = (dga, du, dug, dgc, dfl)
    secs_t3 = (dqT3, dkT3, dvT3)
    dw_rm = _dw_rowmajor(hb, secs, nt)
    dw_t = _dw_transposed(hb, secs_t3, nt)
    grad_x, dmetapad, dg_in, db_in = _dh_bwd(secs, secs_t3, w_r, w_t, dz, x, metapad, g_in, nt)
    pieces = dict(loss=loss, metapad=dmetapad, ln_in_g=dg_in, ln_in_b=db_in, w_in_rm=dw_rm, w_in_t=dw_t, b_f=dbf,
                  conv_w=dcw, conv_b=dc_b, ln_conv_g=dg_cv, ln_conv_b=db_cv, w_pw=dw_pw, w_out=dw_out,
                  ln_out_g=dg_out, ln_out_b=db_out)
    return grad_x, pieces


MESH = pl.DeviceIdType.MESH
ANY = pl.BlockSpec(memory_space=pl.ANY)


def _mesh_pos():
    return lax.axis_index("x"), lax.axis_index("y"), lax.axis_index("c")


def _all_gather(blks, name):
    n = len(blks)

    def body(*refs):
        x_refs, out_refs = refs[:n], refs[n:2 * n]
        send_sems, recv_sems, local_sems = refs[2 * n:]
        x, y, c = _mesh_pos()
        me, sibling = (x, y, c), (x, y, 1 - c)
        chips = [(1 - x, y), (x, 1 - y), (1 - x, 1 - y)]

        def slot(a, px, py, pc):
            return out_refs[a].at[4 * px + 2 * py + pc]

        def copy(a, k, block, to, src=None):
            return pltpu.make_async_remote_copy(
                src_ref=slot(a, *block) if src is None else src, dst_ref=slot(a, *block),
                send_sem=send_sems.at[7 * a + k], recv_sem=recv_sems.at[7 * a + k], device_id=to,
                device_id_type=MESH)

        arrays = range(n)
        mine = [pltpu.make_async_copy(x_refs[a], slot(a, *me), local_sems.at[a]) for a in arrays]
        for cp in mine:
            cp.start()
        first = [copy(a, 0, me, sibling, src=x_refs[a]) for a in arrays]
        first += [copy(a, 1 + j, me, (*chip, c), src=x_refs[a]) for j, chip in enumerate(chips) for a in arrays]
        for cp in first:
            cp.start()
        passed = []
        for j, chip in enumerate(chips):
            for a in arrays:
                copy(a, 1 + j, (*chip, c), me).wait_recv()
                passed.append(copy(a, 4 + j, (*chip, c), sibling))
                passed[-1].start()
        for a in arrays:
            copy(a, 0, sibling, me).wait_recv()
            for j, chip in enumerate(chips):
                copy(a, 4 + j, (*chip, 1 - c), me).wait_recv()
        for cp in first + passed:
            cp.wait_send()
        for cp in mine:
            cp.wait()

    return pl.pallas_call(
        body, name=name, out_shape=[jax.ShapeDtypeStruct((N_DEV, *b.shape), b.dtype) for b in blks],
        in_specs=[ANY] * n, out_specs=[ANY] * n,
        scratch_shapes=[pltpu.SemaphoreType.DMA((7 * n,)), pltpu.SemaphoreType.DMA((7 * n,)),
                        pltpu.SemaphoreType.DMA((n,))],
    )(*blks)


def _exchange_sibling(g8s):
    n = len(g8s)

    def body(*refs):
        g_refs, out_refs, send_sems, recv_sems = refs[:n], refs[n:2 * n], refs[2 * n], refs[2 * n + 1]
        x, y, c = _mesh_pos()
        cps = [pltpu.make_async_remote_copy(
            src_ref=g_refs[a].at[2 * q + (1 - c)], dst_ref=out_refs[a].at[q], send_sem=send_sems.at[4 * a + q],
            recv_sem=recv_sems.at[4 * a + q], device_id=(x, y, 1 - c), device_id_type=MESH)
            for a in range(n) for q in range(4)]
        for cp in cps:
            cp.start()
        for cp in cps:
            cp.wait()

    return pl.pallas_call(
        body, name="rs_sibling", out_shape=[jax.ShapeDtypeStruct((4, *g.shape[1:]), g.dtype) for g in g8s],
        in_specs=[ANY] * n, out_specs=[ANY] * n,
        scratch_shapes=[pltpu.SemaphoreType.DMA((4 * n,)), pltpu.SemaphoreType.DMA((4 * n,))],
    )(*g8s)


def _exchange_chips(p4s):
    n = len(p4s)

    def body(*refs):
        p_refs, out_refs, send_sems, recv_sems = refs[:n], refs[n:2 * n], refs[2 * n], refs[2 * n + 1]
        x, y, c = _mesh_pos()
        chips = [(1 - x, y), (x, 1 - y), (1 - x, 1 - y)]
        cps = [pltpu.make_async_remote_copy(
            src_ref=p_refs[a].at[2 * cx + cy], dst_ref=out_refs[a].at[k], send_sem=send_sems.at[3 * a + k],
            recv_sem=recv_sems.at[3 * a + k], device_id=(cx, cy, c), device_id_type=MESH)
            for k, (cx, cy) in enumerate(chips) for a in range(n)]
        for cp in cps:
            cp.start()
        for cp in cps:
            cp.wait()

    return pl.pallas_call(
        body, name="rs_chips", out_shape=[jax.ShapeDtypeStruct((3, *p.shape[1:]), p.dtype) for p in p4s],
        in_specs=[ANY] * n, out_specs=[ANY] * n,
        scratch_shapes=[pltpu.SemaphoreType.DMA((3 * n,)), pltpu.SemaphoreType.DMA((3 * n,))],
    )(*p4s)


def _rs_add_sibling(g8s, recvs, c_idx):
    n = len(g8s)

    def body(s_ref, *refs):
        g_refs, r_refs, p32_refs, pb_refs = (refs[k * n:(k + 1) * n] for k in range(4))
        for g_ref, r_ref, p32_ref, pb_ref in zip(g_refs, r_refs, p32_refs, pb_refs):
            p = g_ref[0] + r_ref[0]
            p32_ref[0] = p
            pb_ref[0] = p.astype(BF16)

    blk = lambda g: pl.BlockSpec((1, *g.shape[1:]), lambda q, s: (q, 0, 0))
    grid_spec = pltpu.PrefetchScalarGridSpec(
        num_scalar_prefetch=1, grid=(4,),
        in_specs=[pl.BlockSpec((1, *g.shape[1:]), lambda q, s: (2 * q + s[0], 0, 0)) for g in g8s]
        + [blk(g) for g in g8s],
        out_specs=[blk(g) for g in g8s] * 2)
    outs = pl.pallas_call(
        body, name="rs_add_sibling", grid_spec=grid_spec,
        out_shape=[jax.ShapeDtypeStruct((4, *g.shape[1:]), F32) for g in g8s]
        + [jax.ShapeDtypeStruct((4, *g.shape[1:]), BF16) for g in g8s],
        compiler_params=_cp(48, ("arbitrary",)),
    )(c_idx, *g8s, *recvs)
    return outs[:n], outs[n:]


def _rs_add_chips(p32s, recvs, q_idx):
    def body(s_ref, pin_ref, pout_ref, ppw_ref, rin_ref, rout_ref, rpw_ref, gin_ref, gout_ref, gpw_ref):
        def total(p_ref, r_ref):
            return ((p_ref[0] + r_ref[0].astype(F32)) + r_ref[1].astype(F32)) + r_ref[2].astype(F32)

        gin_ref[...] = total(pin_ref, rin_ref)[:SHARD_IN, :]
        gout_ref[0] = total(pout_ref, rout_ref)
        gpw_ref[0] = total(ppw_ref, rpw_ref)

    own = lambda p: pl.BlockSpec((1, *p.shape[1:]), lambda i, s: (s[0], 0, 0))
    whole = lambda shape: pl.BlockSpec(shape, lambda i, s: (0,) * len(shape))
    out_shapes = [(SHARD_IN, D_MODEL), (1, *p32s[1].shape[1:]), (1, *p32s[2].shape[1:])]
    grid_spec = pltpu.PrefetchScalarGridSpec(
        num_scalar_prefetch=1, grid=(1,),
        in_specs=[own(p) for p in p32s] + [whole(r.shape) for r in recvs],
        out_specs=[whole(s) for s in out_shapes])
    return pl.pallas_call(
        body, name="rs_add_chips", grid_spec=grid_spec,
        out_shape=[jax.ShapeDtypeStruct(s, F32) for s in out_shapes],
        compiler_params=_cp(48, ("arbitrary",)),
    )(q_idx, *p32s, *recvs)


SMALL_ROWS_G = 64
SMALL_LAYOUT = {
    "metapad": (0, N_META, D_MODEL), "conv_w": (16, 32, D_CONV), "ln_in_g": (48, 1, D_MODEL),
    "ln_in_b": (49, 1, D_MODEL), "b_f": (50, 1, LANES), "conv_b": (51, 1, D_CONV), "ln_conv_g": (52, 1, D_CONV),
    "ln_conv_b": (53, 1, D_CONV), "ln_out_g": (54, 1, D_MODEL), "ln_out_b": (55, 1, D_MODEL), "loss": (56, 1, LANES)}


def _pack_small(pieces):
    names = list(SMALL_LAYOUT)

    def body(*refs):
        out_ref = refs[-1]
        out_ref[...] = jnp.zeros_like(out_ref)
        for name, ref in zip(names, refs[:-1]):
            r0, nr, nl = SMALL_LAYOUT[name]
            src = ref[PAD:, :] if name == "metapad" else ref[...]
            out_ref[r0:r0 + nr, 0:nl] = src

    return pl.pallas_call(body, name="pack_small", out_shape=jax.ShapeDtypeStruct((SMALL_ROWS_G, D_MODEL), F32),
                          compiler_params=_cp(16))(*[pieces[n] for n in names])


def _sum_small(gathered):
    names = list(SMALL_LAYOUT)

    def body(a_ref, *out_refs):
        acc = a_ref[0]
        for d in range(1, N_DEV):
            acc = acc + a_ref[d]
        for name, ref in zip(names, out_refs):
            r0, nr, nl = SMALL_LAYOUT[name]
            ref[...] = acc[r0:r0 + nr, 0:nl]

    outs = pl.pallas_call(
        body, name="sum_small",
        out_shape=[jax.ShapeDtypeStruct(SMALL_LAYOUT[n][1:], F32) for n in names], compiler_params=_cp(16))(gathered)
    return dict(zip(names, outs))


def _adamw(ws, gs, ms, vs):
    n = len(ws)
    c1 = 1.0 - ADAM_B1 ** ADAM_STEP
    c2 = 1.0 - ADAM_B2 ** ADAM_STEP

    def body(*refs):
        w_refs, g_refs, m_refs, v_refs = (refs[k * n:(k + 1) * n] for k in range(4))
        d_refs, nm_refs, nv_refs = (refs[(4 + k) * n:(5 + k) * n] for k in range(3))
        for w_ref, g_ref, m_ref, v_ref, d_ref, nm_ref, nv_ref in zip(w_refs, g_refs, m_refs, v_refs, d_refs,
                                                                     nm_refs, nv_refs):
            g = g_ref[...]
            m = ADAM_B1 * m_ref[...] + (1.0 - ADAM_B1) * g
            v = ADAM_B2 * v_ref[...] + (1.0 - ADAM_B2) * (g * g)
            nm_ref[...] = m
            nv_ref[...] = v
            d_ref[...] = -ADAM_LR * ((m / c1) / (jnp.sqrt(v / c2) + ADAM_EPS) + ADAM_WD * w_ref[...])

    shapes = [jax.ShapeDtypeStruct(w.shape, F32) for w in ws]
    outs = pl.pallas_call(body, name="adamw", out_shape=shapes * 3, compiler_params=_cp(48))(*ws, *gs, *ms, *vs)
    return outs[:n], outs[n:2 * n], outs[2 * n:]


W_NAMES = ("meta", "ln_in_g", "ln_in_b", "w_in", "b_f", "conv_w", "conv_b", "ln_conv_g", "ln_conv_b", "w_pw",
           "w_out", "ln_out_g", "ln_out_b")


def kernel(x, meta, ln_in_g, ln_in_b, w_in, b_f, conv_w, conv_b, ln_conv_g, ln_conv_b, w_pw, w_out, ln_out_g, ln_out_b, loss_target, m_meta, m_ln_in_g, m_ln_in_b, m_w_in, m_b_f, m_conv_w, m_conv_b, m_ln_conv_g, m_ln_conv_b, m_w_pw, m_w_out, m_ln_out_g, m_ln_out_b, v_meta, v_ln_in_g, v_ln_in_b, v_w_in, v_b_f, v_conv_w, v_conv_b, v_ln_conv_g, v_ln_conv_b, v_w_pw, v_w_out, v_ln_out_g, v_ln_out_b):
    mx, my, mc = _mesh_pos()
    me = 4 * mx + 2 * my + mc
    n_meta_sh = D_MODEL // N_DEV
    n_cw_sh = D_CONV // N_DEV
    n_out_sh = D_MODEL // N_DEV
    n_pw_sh = D_CONV // N_DEV

    small_w = jnp.concatenate([meta, jnp.pad(conv_w[0], ((0, 1), (0, LANES - n_cw_sh)))], axis=0)
    all_in, all_out, all_pw, all_small = _all_gather(
        [jnp.pad(w_in[0].T, ((0, 512 - SHARD_IN), (0, 0))).astype(BF16), w_out[0].astype(BF16), w_pw[0].astype(BF16),
         small_w], "gather_weights")
    w_r, w_t, metapad, cw = _repack_weights(all_in, all_small)
    w_out_full = all_out.reshape(D_MODEL, D_MODEL)
    w_pw_full = all_pw.reshape(D_CONV, D_CONV)

    grad_x, pc = _local_step(x[0], loss_target[0], metapad, cw, w_r, w_t, w_pw_full, w_out_full, ln_in_g, ln_in_b,
                             b_f[0], conv_b[0], ln_conv_g[0], ln_conv_b[0], ln_out_g[0], ln_out_b[0])

    g8s = [_unpack_dw_in(pc["w_in_rm"], pc["w_in_t"]), pc["w_out"].reshape(N_DEV, n_out_sh, D_MODEL),
           pc["w_pw"].reshape(N_DEV, n_pw_sh, D_CONV)]
    from_sibling = _exchange_sibling(g8s)
    p32s, pbs = _rs_add_sibling(g8s, from_sibling, jnp.reshape(mc, (1,)).astype(jnp.int32))
    from_chips = _exchange_chips(pbs)
    g_w_in, g_w_out, g_w_pw = _rs_add_chips(p32s, from_chips, jnp.reshape(2 * mx + my, (1,)).astype(jnp.int32))

    sm = _sum_small(_all_gather([_pack_small(pc)], "gather_small_grads")[0])
    grads = {
        "meta": lax.dynamic_slice_in_dim(sm["metapad"], me * n_meta_sh, n_meta_sh, axis=1),
        "ln_in_g": sm["ln_in_g"].reshape(D_MODEL), "ln_in_b": sm["ln_in_b"].reshape(D_MODEL), "w_in": g_w_in.T[None],
        "b_f": sm["b_f"][:, :N_HEADS],
        "conv_w": lax.dynamic_slice_in_dim(sm["conv_w"], me * n_cw_sh, n_cw_sh, axis=1)[None, :CONV_WIDTH],
        "conv_b": sm["conv_b"], "ln_conv_g": sm["ln_conv_g"], "ln_conv_b": sm["ln_conv_b"], "w_pw": g_w_pw,
        "w_out": g_w_out, "ln_out_g": sm["ln_out_g"], "ln_out_b": sm["ln_out_b"]}
    loss_all = sm["loss"][0, 0]

    weights = dict(meta=meta, ln_in_g=ln_in_g, ln_in_b=ln_in_b, w_in=w_in, b_f=b_f, conv_w=conv_w, conv_b=conv_b,
                   ln_conv_g=ln_conv_g, ln_conv_b=ln_conv_b, w_pw=w_pw, w_out=w_out, ln_out_g=ln_out_g,
                   ln_out_b=ln_out_b)
    moms = dict(meta=m_meta, ln_in_g=m_ln_in_g, ln_in_b=m_ln_in_b, w_in=m_w_in, b_f=m_b_f, conv_w=m_conv_w,
                conv_b=m_conv_b, ln_conv_g=m_ln_conv_g, ln_conv_b=m_ln_conv_b, w_pw=m_w_pw, w_out=m_w_out,
                ln_out_g=m_ln_out_g, ln_out_b=m_ln_out_b)
    vels = dict(meta=v_meta, ln_in_g=v_ln_in_g, ln_in_b=v_ln_in_b, w_in=v_w_in, b_f=v_b_f, conv_w=v_conv_w,
                conv_b=v_conv_b, ln_conv_g=v_ln_conv_g, ln_conv_b=v_ln_conv_b, w_pw=v_w_pw, w_out=v_w_out,
                ln_out_g=v_ln_out_g, ln_out_b=v_ln_out_b)

    def to_kernel(name, a):
        if name == "w_in":
            return a[0].T
        return a.reshape(1, -1) if a.ndim == 1 else a

    def from_kernel(name, a):
        return a.T[None] if name == "w_in" else a.reshape(weights[name].shape)

    upd = _adamw(*[[to_kernel(n, d[n]) for n in W_NAMES] for d in (weights, grads, moms, vels)])
    deltas, new_m, new_v = ([from_kernel(n, a) for n, a in zip(W_NAMES, part)] for part in upd)
    return (loss_all, grad_x[None], *[grads[n] for n in W_NAMES], *deltas, *new_m, *new_v)
```

```python
import jax
import jax.numpy as jnp
import numpy as np
from jax import lax
from jax.experimental import pallas as pl
from jax.experimental.pallas import tpu as pltpu

F32 = jnp.float32
BF16 = jnp.bfloat16

D_MODEL = 1024
D_ATTN = 512
D_CONV = 512
N_HEADS = 8
HEAD_DIM = 64
N_META = 16
CONV_WIDTH = 31
LN_EPS = 1e-5
ALPHA = 2.0 ** 0.25
SCALE = HEAD_DIM ** -0.5
LOG2E = 1.4426950408889634
ADAM_LR, ADAM_B1, ADAM_B2, ADAM_EPS, ADAM_WD, ADAM_STEP = 0.001, 0.9, 0.999, 1e-08, 0.01, 10

N_DEV = 8
D_IN = 3592
SHARD_IN = D_IN // N_DEV
TILE = 256
PAD = TILE - N_META
HALO = 32
SHIFT_ROWS = TILE + HALO
EXT_ROWS = SHIFT_ROWS + 8
NEG = -1e30
LANES = 128
W_COLS = 7 * 512 + LANES
OFF_GA_R, OFF_F_R = 1536, 3584
MIB = 1024 * 1024


def _cp(vmem_mib, sem=None):
    kw = dict(vmem_limit_bytes=vmem_mib * MIB)
    if sem is not None:
        kw["dimension_semantics"] = sem
    return pltpu.CompilerParams(**kw)


def _sigmoid(x):
    return 1.0 / (1.0 + jnp.exp(-x))


def _silu_and_grad(x):
    s = _sigmoid(x)
    return x * s, s * (1.0 + x * (1.0 - s))


def _ln_stats(x):
    mu = jnp.mean(x, axis=-1, keepdims=True)
    xc = x - mu
    var = jnp.mean(xc * xc, axis=-1, keepdims=True)
    rstd = lax.rsqrt(var + LN_EPS)
    return xc * rstd, rstd


def _ln_bwd(dy, xhat, rstd, g):
    dxh = dy * g
    m1 = jnp.mean(dxh, axis=-1, keepdims=True)
    m2 = jnp.mean(dxh * xhat, axis=-1, keepdims=True)
    return rstd * (dxh - m1 - xhat * m2)


def _row_spec(cols, shift=False):
    if shift:
        return pl.BlockSpec((TILE, cols), lambda i: (jnp.maximum(i - 1, 0), 0))
    return pl.BlockSpec((TILE, cols), lambda i: (i, 0))


def _full_spec(shape):
    nd = len(shape)
    return pl.BlockSpec(shape, lambda i: (0,) * nd)


def _t3_spec(ch):
    return pl.BlockSpec((1, ch, TILE), lambda i: (i, 0, 0))


def _proj_fwd(x, metapad, g_in, b_in, w_r, nt):
    lp = nt * TILE

    def body(x_ref, mp_ref, g_ref, b_ref, w_ref, hb_ref, qT_ref, kT_ref, vT_ref, k_ref, v_ref,
             ga_ref, u_ref, ug_ref, gc_ref, fl_ref):
        i = pl.program_id(0)
        x0 = jnp.where(i == 0, mp_ref[...], x_ref[...])
        xhat, _ = _ln_stats(x0)
        hb = (xhat * g_ref[...] + b_ref[...]).astype(BF16)
        hb_ref[...] = hb

        def sec(off, n=512):
            return jnp.dot(hb, w_ref[:, off:off + n], preferred_element_type=F32)

        qT_ref[0] = (sec(0) * (SCALE * LOG2E)).T.astype(BF16)
        k = sec(512)
        kT_ref[0] = k.T.astype(BF16)
        k_ref[...] = k.astype(BF16)
        v = sec(1024)
        vT_ref[0] = v.T.astype(BF16)
        v_ref[...] = v.astype(BF16)
        ga_ref[...] = sec(OFF_GA_R).astype(BF16)
        u_ref[...] = sec(OFF_GA_R + 512).astype(BF16)
        ug_ref[...] = sec(OFF_GA_R + 1024).astype(BF16)
        gc_ref[...] = sec(OFF_GA_R + 1536).astype(BF16)
        fl_ref[...] = sec(OFF_F_R, LANES)

    t3 = jax.ShapeDtypeStruct((nt, 512, TILE), BF16)
    rm = lambda dt: jax.ShapeDtypeStruct((lp, 512), dt)
    return pl.pallas_call(
        body, name="proj_fwd", grid=(nt,),
        in_specs=[_row_spec(D_MODEL, shift=True), _full_spec((TILE, D_MODEL)), _full_spec((1, D_MODEL)),
                  _full_spec((1, D_MODEL)), _full_spec((D_MODEL, W_COLS))],
        out_specs=[_row_spec(D_MODEL), _t3_spec(512), _t3_spec(512), _t3_spec(512), _row_spec(512), _row_spec(512),
                   _row_spec(512), _row_spec(512), _row_spec(512), _row_spec(512), _row_spec(LANES)],
        out_shape=[jax.ShapeDtypeStruct((lp, D_MODEL), BF16), t3, t3, t3, rm(BF16), rm(BF16),
                   rm(BF16), rm(BF16), rm(BF16), rm(BF16), jax.ShapeDtypeStruct((lp, LANES), F32)],
        compiler_params=_cp(56, ("arbitrary",)),
    )(x, metapad, g_in, b_in, w_r)


def _row_mask(i, shape):
    r = lax.broadcasted_iota(jnp.int32, shape, 0)
    return (r >= PAD) | (i > 0)


def _cumsum_fwd(fl, bf_pad, nt):
    lp = nt * TILE

    def body(fl_ref, bf_ref, kx_ref, carry):
        i = pl.program_id(0)

        @pl.when(i == 0)
        def _():
            carry[...] = jnp.zeros_like(carry)

        z = fl_ref[...] + bf_ref[...]
        lf = jnp.minimum(z, 0.0) - jnp.log(1.0 + jnp.exp(-jnp.abs(z)))
        lane = lax.broadcasted_iota(jnp.int32, (TILE, LANES), 1)
        real = _row_mask(i, (TILE, LANES))
        lf = jnp.where(real & (lane < N_HEADS), lf, 0.0)
        r = lax.broadcasted_iota(jnp.int32, (TILE, TILE), 0)
        c = lax.broadcasted_iota(jnp.int32, (TILE, TILE), 1)
        tril = (c <= r).astype(F32)
        cs = jnp.dot(tril, lf, precision=lax.Precision.HIGHEST, preferred_element_type=F32) + carry[...]
        carry[...] = cs[TILE - 1:TILE, :]
        bias = jnp.where(real, cs * (-LOG2E), NEG)
        hi = bias.astype(BF16).astype(F32)
        mid = (bias - hi).astype(BF16).astype(F32)
        lo = (bias - hi - mid).astype(BF16).astype(F32)
        for p in range(N_HEADS // 2):
            out = jnp.zeros((TILE, LANES), F32)
            for hh in range(2):
                for part, piece in enumerate((hi, mid, lo)):
                    dst, src = 3 * hh + part, 2 * p + hh
                    moved = piece if dst == src else pltpu.roll(piece, (dst - src) % LANES, 1)
                    out = jnp.where(lane == dst, moved, out)
            kx_ref[p] = out.astype(BF16)

    return pl.pallas_call(
        body, name="cumsum_fwd", grid=(nt,),
        in_specs=[_row_spec(LANES), _full_spec((1, LANES))],
        out_specs=pl.BlockSpec((N_HEADS // 2, TILE, LANES), lambda i: (0, i, 0)),
        out_shape=jax.ShapeDtypeStruct((N_HEADS // 2, lp, LANES), BF16),
        scratch_shapes=[pltpu.VMEM((1, LANES), F32)],
        compiler_params=_cp(32, ("arbitrary",)),
    )(fl, bf_pad)


def _head_rows(blk, hh):
    r = lax.broadcasted_iota(jnp.int32, blk.shape, 0)
    return jnp.where((r >= hh * HEAD_DIM) & (r < (hh + 1) * HEAD_DIM), blk, jnp.zeros_like(blk))


def _two_heads(blk):
    return jnp.concatenate([_head_rows(blk, 0), _head_rows(blk, 1)], axis=1)


def _bias_rows():
    r = lax.broadcasted_iota(jnp.int32, (LANES, 2 * TILE), 0)
    c = lax.broadcasted_iota(jnp.int32, (LANES, 2 * TILE), 1)
    return jnp.where(((r < 3) & (c < TILE)) | ((r >= 3) & (r < 6) & (c >= TILE)), 1.0, 0.0).astype(BF16)


def _diag_mask(s):
    kpos = lax.broadcasted_iota(jnp.int32, (TILE, TILE), 0)
    qpos = lax.broadcasted_iota(jnp.int32, (TILE, TILE), 1)
    return jnp.where(kpos <= qpos, s, NEG)


def _stream(n, first, nxt, scores, update):
    if n == 0:
        return
    scores(first, 0)

    def pair_body(_, idx):
        idx_b = nxt(idx)
        scores(idx_b, 1)
        update(idx, 0)
        idx_c = nxt(idx_b)
        scores(idx_c, 0)
        update(idx_b, 1)
        return idx_c

    idx = lax.fori_loop(0, (n - 1) // 2, pair_body, first)
    if n % 2 == 1:
        update(idx, 0)
    else:
        idx_b = nxt(idx)
        scores(idx_b, 1)
        update(idx, 0)
        update(idx_b, 1)


def _next_below_diagonal(idx):
    i, j = idx
    wrap = j + 1 >= i
    return jnp.where(wrap, i + 1, i), jnp.where(wrap, 0, j + 1)


def _tile_rows(t):
    return pl.ds(pl.multiple_of(t * TILE, TILE), TILE)


def _two_streams(nt):
    load, group = [0, 0], {}
    for i in sorted(range(1, nt), reverse=True):
        g = 0 if load[0] <= load[1] else 1
        group[i] = g
        load[g] += i
    rows = [[(i, i, j) for i in range(1, nt) if group[i] == g for j in range(i)] for g in range(2)]
    length = max(len(r) for r in rows)
    rows = [r + [(nt, 0, 0)] * (length - len(r)) for r in rows]
    return group, np.asarray(rows, np.int32).reshape(2, -1), length


def _attn_fwd(qT3, k, kx3, vT3, nt):
    lp = nt * TILE
    npair = N_HEADS // 2
    group, table, n_stream = _two_streams(nt)

    def body(tab_ref, qT_ref, k_ref, kx_ref, vT_ref, oT_ref, o_ref, lse_ref, sbuf, m_0, l_0, acc_0, m_1, l_1, acc_1):
        ones = _bias_rows()
        states = ((m_0, l_0, acc_0), (m_1, l_1, acc_1))

        def scores(i, j, slot):
            qcat = jnp.concatenate([_two_heads(qT_ref[i]), ones], axis=0)
            kext = jnp.concatenate([k_ref[_tile_rows(j), :], kx_ref[0, _tile_rows(j), :]], axis=1)
            sbuf[slot] = jnp.dot(kext, qcat, preferred_element_type=F32)

        def update(st, j, slot, state, diag):
            m_s, l_s, acc_s = state
            for hh in range(2):
                s = sbuf[slot, :, hh * TILE:(hh + 1) * TILE]
                vj = vT_ref[j, hh * HEAD_DIM:(hh + 1) * HEAD_DIM, :]
                if diag:
                    s = _diag_mask(s)
                    m_new = jnp.max(s, axis=0, keepdims=True)
                    p = jnp.exp2(s - m_new)
                    l_s[st, hh] = jnp.sum(p, axis=0, keepdims=True)
                    acc_s[st, hh] = jnp.dot(vj, p.astype(BF16), preferred_element_type=F32)
                else:
                    m_prev = m_s[st, hh]
                    m_new = jnp.maximum(m_prev, jnp.max(s, axis=0, keepdims=True))
                    a = jnp.exp2(m_prev - m_new)
                    p = jnp.exp2(s - m_new)
                    l_s[st, hh] = a * l_s[st, hh] + jnp.sum(p, axis=0, keepdims=True)
                    acc_s[st, hh] = a * acc_s[st, hh] + jnp.dot(vj, p.astype(BF16), preferred_element_type=F32)
                m_s[st, hh] = m_new

        _stream(nt, jnp.int32(0), lambda t: t + 1, lambda t, slot: scores(t, t, slot),
                lambda t, slot: update(t, t, slot, states[0], True))
        for dst, src in zip(states[1], states[0]):
            dst[0:nt] = src[0:nt]
        for m_s, l_s, acc_s in states:
            m_s[nt] = jnp.full(m_s.shape[1:], NEG, F32)
            l_s[nt] = jnp.zeros(l_s.shape[1:], F32)
            acc_s[nt] = jnp.zeros(acc_s.shape[1:], F32)

        def entry(g, t):
            return tab_ref[g, 3 * t], tab_ref[g, 3 * t + 1], tab_ref[g, 3 * t + 2]

        def scores2(t, slot):
            for g in range(2):
                _, qi, kj = entry(g, t)
                scores(qi, kj, 2 * g + slot)

        def update2(t, slot):
            for g in range(2):
                st, _, kj = entry(g, t)
                update(st, kj, 2 * g + slot, states[g], False)

        _stream(n_stream, jnp.int32(0), lambda t: t + 1, scores2, update2)

        for i in range(nt):
            m_s, l_s, acc_s = states[group.get(i, 0)]
            for hh in range(2):
                l = l_s[i, hh]
                oT_ref[i, hh * HEAD_DIM:(hh + 1) * HEAD_DIM, :] = acc_s[i, hh] / l
                lse_ref[0, i, hh:hh + 1, :] = m_s[i, hh] + jnp.log(l) * LOG2E
            o_ref[i * TILE:(i + 1) * TILE, :] = oT_ref[i].T.astype(BF16)

    blk_t = pl.BlockSpec((nt, LANES, TILE), lambda p, tab: (0, p, 0))
    blk_rm = pl.BlockSpec((lp, LANES), lambda p, tab: (0, p))
    blk_px = pl.BlockSpec((1, lp, LANES), lambda p, tab: (p, 0, 0))
    blk_st = pl.BlockSpec((1, nt, 8, TILE), lambda p, tab: (p, 0, 0, 0))
    state = [pltpu.VMEM((nt + 1, 2, 1, TILE), F32), pltpu.VMEM((nt + 1, 2, 1, TILE), F32),
             pltpu.VMEM((nt + 1, 2, HEAD_DIM, TILE), F32)]
    grid_spec = pltpu.PrefetchScalarGridSpec(
        num_scalar_prefetch=1, grid=(npair,), in_specs=[blk_t, blk_rm, blk_px, blk_t],
        out_specs=[blk_t, blk_rm, blk_st], scratch_shapes=[pltpu.VMEM((4, TILE, 2 * TILE), F32)] + state + state)
    return pl.pallas_call(
        body, name="attn_fwd", grid_spec=grid_spec,
        out_shape=[jax.ShapeDtypeStruct((nt, D_ATTN, TILE), F32),
                   jax.ShapeDtypeStruct((lp, D_ATTN), BF16),
                   jax.ShapeDtypeStruct((npair, nt, 8, TILE), F32)],
        compiler_params=_cp(60, ("arbitrary",)),
    )(jnp.asarray(table), qT3, k, kx3, vT3)


def _attn_bwd(qT3, kT3, k, kx3, v, oT3, doT3, lse4, nt):
    lp = nt * TILE
    npair = N_HEADS // 2

    def body(qT_ref, kT_ref, k_ref, kx_ref, v_ref, oT_ref, doT_ref, lse_ref,
             dqT_ref, dkT_ref, dvT_ref, dck_ref, dcq_ref, sbuf, dpbuf, dq_s, dk_s, dv_s, dc_s):
        ones = _bias_rows()
        nt_dims = (((1,), (1,)), ((), ()))

        def scores(idx, slot):
            i, j = idx
            qcat = jnp.concatenate([_two_heads(qT_ref[i]), ones], axis=0)
            kext = jnp.concatenate([k_ref[_tile_rows(j), :], kx_ref[0, _tile_rows(j), :]], axis=1)
            sbuf[slot] = jnp.dot(kext, qcat, preferred_element_type=F32)
            dpbuf[slot] = jnp.dot(v_ref[_tile_rows(j), :], _two_heads(doT_ref[i]), preferred_element_type=F32)

        def update(idx, slot, diag):
            i, j = idx
            for hh in range(2):
                hs = slice(hh * HEAD_DIM, (hh + 1) * HEAD_DIM)
                s = sbuf[slot, :, hh * TILE:(hh + 1) * TILE]
                if diag:
                    s = _diag_mask(s)
                p = jnp.exp2(s - lse_ref[0, i, hh:hh + 1, :])
                doh = doT_ref[i, hs, :]
                delta = jnp.sum(doh.astype(F32) * oT_ref[i, hs, :], axis=0, keepdims=True)
                ds = p * (dpbuf[slot, :, hh * TILE:(hh + 1) * TILE] - delta)
                dsb = ds.astype(BF16)
                dv = lax.dot_general(doh, p.astype(BF16), nt_dims, preferred_element_type=F32)
                dk = lax.dot_general(qT_ref[i, hs, :], dsb, nt_dims, preferred_element_type=F32)
                dq = jnp.dot(kT_ref[j, hs, :], dsb, preferred_element_type=F32)
                dc = ds[:, :LANES] + ds[:, LANES:]
                dcq = jnp.sum(ds, axis=0, keepdims=True)
                if diag:
                    dv_s[j, hh] = dv
                    dk_s[j, hh] = dk
                    dc_s[j, hh] = dc
                    dq_s[i, hs, :] = dq
                    dcq_ref[0, i, hh:hh + 1, :] = dcq
                else:
                    dv_s[j, hh] += dv
                    dk_s[j, hh] += dk
                    dc_s[j, hh] += dc
                    dq_s[i, hs, :] += dq
                    dcq_ref[0, i, hh:hh + 1, :] += dcq

        dcq_ref[...] = jnp.zeros_like(dcq_ref)
        zero = jnp.int32(0)
        _stream(nt, (zero, zero), lambda idx: (idx[0] + 1, idx[1] + 1), scores,
                lambda idx, slot: update(idx, slot, True))
        _stream(nt * (nt - 1) // 2, (zero + 1, zero), _next_below_diagonal, scores,
                lambda idx, slot: update(idx, slot, False))

        lane = lax.broadcasted_iota(jnp.int32, (TILE, LANES), 1)

        def finish(t, carry):
            dck = jnp.zeros((TILE, LANES), F32)
            for hh in range(2):
                hs = slice(hh * HEAD_DIM, (hh + 1) * HEAD_DIM)
                dkT_ref[t, hs, :] = (dk_s[t, hh] * (1.0 / LOG2E)).astype(BF16)
                dvT_ref[t, hs, :] = dv_s[t, hh].astype(BF16)
                dck = jnp.where(lane == hh, -jnp.sum(dc_s[t, hh], axis=1, keepdims=True), dck)
            dck_ref[0, _tile_rows(t), :] = dck
            dqT_ref[t] = (dq_s[t] * SCALE).astype(BF16)
            return carry

        lax.fori_loop(0, nt, finish, 0)

    blk_t = pl.BlockSpec((nt, LANES, TILE), lambda p: (0, p, 0))
    blk_rm = pl.BlockSpec((lp, LANES), lambda p: (0, p))
    blk_px = pl.BlockSpec((1, lp, LANES), lambda p: (p, 0, 0))
    blk_st = pl.BlockSpec((1, nt, 8, TILE), lambda p: (p, 0, 0, 0))
    t3 = jax.ShapeDtypeStruct((nt, D_ATTN, TILE), BF16)
    return pl.pallas_call(
        body, name="attn_bwd", grid=(npair,),
        in_specs=[blk_t, blk_t, blk_rm, blk_px, blk_rm, blk_t, blk_t, blk_st],
        out_specs=[blk_t, blk_t, blk_t, blk_px, blk_st],
        out_shape=[t3, t3, t3, jax.ShapeDtypeStruct((npair, lp, LANES), F32),
                   jax.ShapeDtypeStruct((npair, nt, 8, TILE), F32)],
        scratch_shapes=[pltpu.VMEM((2, TILE, 2 * TILE), F32), pltpu.VMEM((2, TILE, 2 * TILE), F32),
                        pltpu.VMEM((nt, LANES, TILE), F32), pltpu.VMEM((nt, 2, HEAD_DIM, TILE), F32),
                        pltpu.VMEM((nt, 2, HEAD_DIM, TILE), F32), pltpu.VMEM((nt, 2, TILE, LANES), F32)],
        compiler_params=_cp(60, ("arbitrary",)),
    )(qT3, kT3, k, kx3, v, oT3, doT3, lse4)


def _glu(u, ug, i):
    return jnp.where(_row_mask(i, u.shape), u.astype(F32) * _sigmoid(ug.astype(F32)), 0.0)


def _shifted_copies(dst, src):
    for ph in range(8):
        dst[ph] = src[ph:ph + SHIFT_ROWS, :]


def _tap_window(sh, off, lanes, row0=0, rows=TILE):
    base = (off // 8) * 8 + row0
    return sh[off % 8, base:base + rows, lanes]


def _conv_fwd(u, ug, conv_w, conv_b, g, b, w_pw, nt):
    lp = nt * TILE

    def body(u_ref, ug_ref, up_ref, ugp_ref, w_ref, cb_ref, g_ref, b_ref, wpw_ref,
             co_ref, hc_ref, pw_ref, ext, sh):
        i = pl.program_id(0)
        prev = _glu(up_ref[...], ugp_ref[...], i - 1)
        ext[0:HALO, :] = jnp.where(i > 0, prev[TILE - HALO:, :], 0.0)
        ext[HALO:HALO + TILE, :] = _glu(u_ref[...], ug_ref[...], i)
        ext[HALO + TILE:, :] = jnp.zeros((8, D_CONV), F32)
        _shifted_copies(sh, ext)
        for lb in range(D_CONV // LANES):
            lanes = slice(lb * LANES, (lb + 1) * LANES)
            acc = jnp.zeros((TILE, LANES), F32) + cb_ref[:, lanes]
            for t in range(CONV_WIDTH):
                off = HALO - (CONV_WIDTH - 1) + t
                acc = acc + w_ref[t:t + 1, lanes] * _tap_window(sh, off, lanes)
            co_ref[:, lanes] = acc
        xhat, _ = _ln_stats(co_ref[...])
        a, _ = _silu_and_grad(xhat * g_ref[...] + b_ref[...])
        hc = a.astype(BF16)
        hc_ref[...] = hc
        pw_ref[...] = jnp.dot(hc, wpw_ref[...], preferred_element_type=F32).astype(BF16)

    rm = lambda dt: jax.ShapeDtypeStruct((lp, D_CONV), dt)
    return pl.pallas_call(
        body, name="conv_fwd", grid=(nt,),
        in_specs=[_row_spec(512), _row_spec(512), _row_spec(512, shift=True), _row_spec(512, shift=True),
                  _full_spec((32, 512)), _full_spec((1, 512)), _full_spec((1, 512)), _full_spec((1, 512)),
                  _full_spec((512, 512))],
        out_specs=[_row_spec(512), _row_spec(512), _row_spec(512)],
        out_shape=[rm(F32), rm(BF16), rm(BF16)],
        scratch_shapes=[pltpu.VMEM((EXT_ROWS, D_CONV), F32), pltpu.VMEM((8, SHIFT_ROWS, D_CONV), F32)],
        compiler_params=_cp(40, ("arbitrary",)),
    )(u, ug, u, ug, conv_w, conv_b, g, b, w_pw)


def _out_fwd(o, ga, pw, gc, x, metapad, g_in, b_in, w_out, g_out, b_out, target, nt):
    lp = nt * TILE

    def body(o_ref, ga_ref, pw_ref, gc_ref, x_ref, mp_ref, gi_ref, bi_ref, wo_ref, go_ref, bo_ref, t_ref,
             y_ref, dz_ref, loss_ref, dgo_ref, dbo_ref):
        i = pl.program_id(0)

        @pl.when(i == 0)
        def _():
            loss_ref[...] = jnp.zeros_like(loss_ref)
            dgo_ref[...] = jnp.zeros_like(dgo_ref)
            dbo_ref[...] = jnp.zeros_like(dbo_ref)

        x0 = jnp.where(i == 0, mp_ref[...], x_ref[...])
        xhat, _ = _ln_stats(x0)
        h = xhat * gi_ref[...] + bi_ref[...]
        ya, _ = _silu_and_grad(ga_ref[...].astype(F32))
        yc, _ = _silu_and_grad(gc_ref[...].astype(F32))
        ya = (o_ref[...].astype(F32) * ya).astype(BF16)
        yc = (pw_ref[...].astype(F32) * yc).astype(BF16)
        y_ref[:, :D_ATTN] = ya
        y_ref[:, D_ATTN:] = yc
        z = ALPHA * h + jnp.dot(ya, wo_ref[:D_ATTN, :], preferred_element_type=F32) \
            + jnp.dot(yc, wo_ref[D_ATTN:, :], preferred_element_type=F32)
        zhat, rstd = _ln_stats(z)
        out = zhat * go_ref[...] + bo_ref[...]
        live = (i > 0).astype(F32)
        err = (out - t_ref[...]) * live
        dout = err * (1.0 / D_MODEL)
        loss_ref[...] += 0.5 * jnp.sum(jnp.sum(err * dout, axis=0, keepdims=True), axis=1, keepdims=True)
        dgo_ref[...] += jnp.sum(dout * zhat, axis=0, keepdims=True)
        dbo_ref[...] += jnp.sum(dout, axis=0, keepdims=True)
        dz_ref[...] = _ln_bwd(dout, zhat, rstd, go_ref[...])

    return pl.pallas_call(
        body, name="out_fwd", grid=(nt,),
        in_specs=[_row_spec(512), _row_spec(512), _row_spec(512), _row_spec(512),
                  _row_spec(D_MODEL, shift=True), _full_spec((TILE, D_MODEL)), _full_spec((1, D_MODEL)),
                  _full_spec((1, D_MODEL)), _full_spec((D_MODEL, D_MODEL)), _full_spec((1, D_MODEL)),
                  _full_spec((1, D_MODEL)), _row_spec(D_MODEL, shift=True)],
        out_specs=[_row_spec(D_MODEL), _row_spec(D_MODEL), _full_spec((1, LANES)), _full_spec((1, D_MODEL)),
                   _full_spec((1, D_MODEL))],
        out_shape=[jax.ShapeDtypeStruct((lp, D_MODEL), BF16), jax.ShapeDtypeStruct((lp, D_MODEL), F32),
                   jax.ShapeDtypeStruct((1, LANES), F32), jax.ShapeDtypeStruct((1, D_MODEL), F32),
                   jax.ShapeDtypeStruct((1, D_MODEL), F32)],
        compiler_params=_cp(40, ("arbitrary",)),
    )(o, ga, pw, gc, x, metapad, g_in, b_in, w_out, g_out, b_out, target)


def _out_bwd(dz, y, o, ga, pw, gc, w_out, hc, co, w_pw, g_cv, b_cv, nt):
    lp = nt * TILE

    def body(dz_ref, y_ref, o_ref, ga_ref, pw_ref, gc_ref, wo_ref, hc_ref, co_ref, wpw_ref, g_ref, b_ref,
             doT_ref, dga_ref, dgc_ref, dwo_ref, dco_ref, dwpw_ref, dg_ref, db_ref, dcb_ref):
        i = pl.program_id(0)

        @pl.when(i == 0)
        def _():
            dwo_ref[...] = jnp.zeros_like(dwo_ref)
            dwpw_ref[...] = jnp.zeros_like(dwpw_ref)
            dg_ref[...] = jnp.zeros_like(dg_ref)
            db_ref[...] = jnp.zeros_like(db_ref)
            dcb_ref[...] = jnp.zeros_like(dcb_ref)

        dzb = dz_ref[...].astype(BF16)
        nt_dims = (((1,), (1,)), ((), ()))
        tn_dims = (((0,), (0,)), ((), ()))
        dya = lax.dot_general(dzb, wo_ref[:D_ATTN, :], nt_dims, preferred_element_type=F32)
        dyc = lax.dot_general(dzb, wo_ref[D_ATTN:, :], nt_dims, preferred_element_type=F32)
        sa, sga = _silu_and_grad(ga_ref[...].astype(F32))
        sc, sgc = _silu_and_grad(gc_ref[...].astype(F32))
        doT_ref[0] = (dya * sa).T.astype(BF16)
        dga_ref[...] = (dya * o_ref[...].astype(F32) * sga).astype(BF16)
        dpw_b = (dyc * sc).astype(BF16)
        dgc_ref[...] = (dyc * pw_ref[...].astype(F32) * sgc).astype(BF16)
        dwo_ref[...] += lax.dot_general(y_ref[...], dzb, tn_dims, preferred_element_type=F32)

        dhc = lax.dot_general(dpw_b, wpw_ref[...], nt_dims, preferred_element_type=F32)
        xhat, rstd = _ln_stats(co_ref[...])
        _, sg = _silu_and_grad(xhat * g_ref[...] + b_ref[...])
        dln = dhc * sg
        dg_ref[...] += jnp.sum(dln * xhat, axis=0, keepdims=True)
        db_ref[...] += jnp.sum(dln, axis=0, keepdims=True)
        dco = _ln_bwd(dln, xhat, rstd, g_ref[...])
        dco_ref[...] = dco
        dcb_ref[...] += jnp.sum(dco, axis=0, keepdims=True)
        dwpw_ref[...] += lax.dot_general(hc_ref[...], dpw_b, tn_dims, preferred_element_type=F32)

    rm = jax.ShapeDtypeStruct((lp, 512), BF16)
    vec = jax.ShapeDtypeStruct((1, D_CONV), F32)
    return pl.pallas_call(
        body, name="out_bwd", grid=(nt,),
        in_specs=[_row_spec(D_MODEL), _row_spec(D_MODEL), _row_spec(512), _row_spec(512), _row_spec(512),
                  _row_spec(512), _full_spec((D_MODEL, D_MODEL)), _row_spec(512), _row_spec(512),
                  _full_spec((512, 512)), _full_spec((1, 512)), _full_spec((1, 512))],
        out_specs=[_t3_spec(512), _row_spec(512), _row_spec(512), _full_spec((D_MODEL, D_MODEL)), _row_spec(512),
                   _full_spec((512, 512)), _full_spec((1, 512)), _full_spec((1, 512)), _full_spec((1, 512))],
        out_shape=[jax.ShapeDtypeStruct((nt, 512, TILE), BF16), rm, rm, jax.ShapeDtypeStruct((D_MODEL, D_MODEL), F32),
                   jax.ShapeDtypeStruct((lp, D_CONV), F32), jax.ShapeDtypeStruct((512, 512), F32), vec, vec, vec],
        compiler_params=_cp(56, ("arbitrary",)),
    )(dz, y, o, ga, pw, gc, w_out, hc, co, w_pw, g_cv, b_cv)


def _conv_bwd_taps(dco, u, ug, conv_w, nt):
    lp = nt * TILE

    def body(dco_ref, dcon_ref, u_ref, ug_ref, up_ref, ugp_ref, w3_ref, du_ref, dug_ref, dw_ref, ext, dext, sh, dsh,
             dhg_s, dw_s):
        i = pl.program_id(0)

        @pl.when(i == 0)
        def _():
            dw_s[...] = jnp.zeros_like(dw_s)

        prev = _glu(up_ref[...], ugp_ref[...], i - 1)
        ext[0:HALO, :] = jnp.where(i > 0, prev[TILE - HALO:, :], 0.0)
        ext[HALO:HALO + TILE, :] = _glu(u_ref[...], ug_ref[...], i)
        ext[HALO + TILE:, :] = jnp.zeros((8, D_CONV), F32)
        dext[0:TILE, :] = dco_ref[...]
        dext[TILE:TILE + HALO, :] = jnp.where(i < nt - 1, dcon_ref[0:HALO, :], 0.0)
        dext[TILE + HALO:, :] = jnp.zeros((8, D_CONV), F32)
        _shifted_copies(sh, ext)
        _shifted_copies(dsh, dext)
        stripe = 32

        def stripe_body(rb, carry):
            row0 = pl.multiple_of(rb * stripe, stripe)
            dco = dco_ref[pl.ds(row0, stripe), :]
            dhg = jnp.zeros((stripe, D_CONV), F32)
            for t in range(CONV_WIDTH):
                off = HALO - (CONV_WIDTH - 1) + t
                back = CONV_WIDTH - 1 - t
                prod = dco * sh[off % 8, pl.ds((off // 8) * 8 + row0, stripe), :]
                part = prod[0:8, :]
                for r8 in range(1, stripe // 8):
                    part = part + prod[8 * r8:8 * r8 + 8, :]
                dw_s[t] += part
                dhg = dhg + w3_ref[t] * dsh[back % 8, pl.ds((back // 8) * 8 + row0, stripe), :]
            dhg_s[pl.ds(row0, stripe), :] = dhg
            return carry

        lax.fori_loop(0, TILE // stripe, stripe_body, 0)

        @pl.when(i == nt - 1)
        def _():
            dw_ref[...] = jnp.sum(dw_s[...], axis=1)

        dhg = jnp.where(_row_mask(i, (TILE, D_CONV)), dhg_s[...], 0.0)
        sg = _sigmoid(ug_ref[...].astype(F32))
        du_ref[...] = (dhg * sg).astype(BF16)
        dug_ref[...] = (dhg * u_ref[...].astype(F32) * sg * (1.0 - sg)).astype(BF16)

    rm = jax.ShapeDtypeStruct((lp, D_CONV), BF16)
    nxt = pl.BlockSpec((TILE, 512), lambda i: (jnp.minimum(i + 1, nt - 1), 0))
    ext_t = pltpu.VMEM((EXT_ROWS, D_CONV), F32)
    sh_t = pltpu.VMEM((8, SHIFT_ROWS, D_CONV), F32)
    return pl.pallas_call(
        body, name="conv_bwd_taps", grid=(nt,),
        in_specs=[_row_spec(512), nxt, _row_spec(512), _row_spec(512), _row_spec(512, shift=True),
                  _row_spec(512, shift=True), _full_spec((32, 1, 512))],
        out_specs=[_row_spec(512), _row_spec(512), _full_spec((32, 512))],
        out_shape=[rm, rm, jax.ShapeDtypeStruct((32, D_CONV), F32)],
        scratch_shapes=[ext_t, ext_t, sh_t, sh_t, pltpu.VMEM((TILE, D_CONV), F32), pltpu.VMEM((32, 8, D_CONV), F32)],
        compiler_params=_cp(48, ("arbitrary",)),
    )(dco, dco, u, ug, u, ug, conv_w.reshape(32, 1, D_CONV))


def _cumsum_bwd(dck, dcq4, fl, bf_pad, nt):
    lp = nt * TILE

    def body(dck_ref, dcq_ref, fl_ref, bf_ref, dfl_ref, dbf_ref, carry):
        i = pl.program_id(0)
        tile = nt - 1 - i

        @pl.when(i == 0)
        def _():
            carry[...] = jnp.zeros_like(carry)
            dbf_ref[...] = jnp.zeros_like(dbf_ref)

        dc = jnp.zeros((TILE, LANES), F32)
        for p in range(N_HEADS // 2):
            dq_rows = jnp.concatenate([dcq_ref[p, 0], jnp.zeros((LANES - 8, TILE), F32)], axis=0)
            both = dck_ref[p] + dq_rows.T
            dc = dc + (both if p == 0 else pltpu.roll(both, 2 * p, 1))
        r = lax.broadcasted_iota(jnp.int32, (TILE, TILE), 0)
        c = lax.broadcasted_iota(jnp.int32, (TILE, TILE), 1)
        triu = (c >= r).astype(F32)
        dlf = jnp.dot(triu, dc, precision=lax.Precision.HIGHEST, preferred_element_type=F32) + carry[...]
        carry[...] = dlf[0:1, :]
        z = fl_ref[...] + bf_ref[...]
        lane = lax.broadcasted_iota(jnp.int32, (TILE, LANES), 1)
        dfl = jnp.where(_row_mask(tile, (TILE, LANES)) & (lane < N_HEADS), dlf * _sigmoid(-z), 0.0)
        dfl_ref[...] = dfl.astype(BF16)
        dbf_ref[...] += jnp.sum(dfl, axis=0, keepdims=True)

    rev = lambda i: (nt - 1 - i, 0)
    return pl.pallas_call(
        body, name="cumsum_bwd", grid=(nt,),
        in_specs=[pl.BlockSpec((N_HEADS // 2, TILE, LANES), lambda i: (0, nt - 1 - i, 0)),
                  pl.BlockSpec((N_HEADS // 2, 1, 8, TILE), lambda i: (0, nt - 1 - i, 0, 0)),
                  pl.BlockSpec((TILE, LANES), rev), _full_spec((1, LANES))],
        out_specs=[pl.BlockSpec((TILE, LANES), rev), _full_spec((1, LANES))],
        out_shape=[jax.ShapeDtypeStruct((lp, LANES), BF16), jax.ShapeDtypeStruct((1, LANES), F32)],
        scratch_shapes=[pltpu.VMEM((1, LANES), F32)],
        compiler_params=_cp(32, ("arbitrary",)),
    )(dck, dcq4, fl, bf_pad)


def _dw_rowmajor(hb, secs, nt):
    n = len(secs)

    def body(*refs):
        hb_ref, sec_refs, out_refs = refs[0], refs[1:1 + n], refs[1 + n:]
        i = pl.program_id(0)

        @pl.when(i == 0)
        def _():
            for o_ref in out_refs:
                o_ref[...] = jnp.zeros_like(o_ref)

        hb_t = hb_ref[...]
        for s_ref, o_ref in zip(sec_refs, out_refs):
            o_ref[...] += lax.dot_general(hb_t, s_ref[...], (((0,), (0,)), ((), ())), preferred_element_type=F32)

    return pl.pallas_call(
        body, name="dw_rowmajor", grid=(nt,),
        in_specs=[_row_spec(D_MODEL)] + [_row_spec(s.shape[1]) for s in secs],
        out_specs=[_full_spec((D_MODEL, s.shape[1])) for s in secs],
        out_shape=[jax.ShapeDtypeStruct((D_MODEL, s.shape[1]), F32) for s in secs],
        compiler_params=_cp(48, ("arbitrary",)),
    )(hb, *secs)


def _dw_transposed(hb, secs_t3, nt):
    n = len(secs_t3)

    def body(*refs):
        hb_ref, sec_refs, out_refs = refs[0], refs[1:1 + n], refs[1 + n:]
        i = pl.program_id(0)

        @pl.when(i == 0)
        def _():
            for o_ref in out_refs:
                o_ref[...] = jnp.zeros_like(o_ref)

        hb_t = hb_ref[...]
        for s_ref, o_ref in zip(sec_refs, out_refs):
            o_ref[...] += jnp.dot(s_ref[0], hb_t, preferred_element_type=F32)

    return pl.pallas_call(
        body, name="dw_transposed", grid=(nt,),
        in_specs=[_row_spec(D_MODEL)] + [_t3_spec(512) for _ in secs_t3],
        out_specs=[_full_spec((512, D_MODEL)) for _ in secs_t3],
        out_shape=[jax.ShapeDtypeStruct((512, D_MODEL), F32) for _ in secs_t3],
        compiler_params=_cp(40, ("arbitrary",)),
    )(hb, *secs_t3)


def _dh_bwd(secs, secs_t3, w_rm, w_t, dz, x, metapad, g_in, nt):
    n, m = len(secs), len(secs_t3)
    offs = OFF_GA_R + np.cumsum([0] + [s.shape[1] for s in secs])

    def body(*refs):
        sec_refs, t3_refs = refs[:n], refs[n:n + m]
        wrm_ref, wt_ref, dz_ref, x_ref, mp_ref, g_ref = refs[n + m:n + m + 6]
        dx_ref, dmeta_ref, dg_ref, db_ref = refs[n + m + 6:]
        i = pl.program_id(0)

        @pl.when(i == 0)
        def _():
            dg_ref[...] = jnp.zeros_like(dg_ref)
            db_ref[...] = jnp.zeros_like(db_ref)

        dh = ALPHA * dz_ref[...]
        for s_ref, lo, hi in zip(sec_refs, offs[:-1], offs[1:]):
            dh = dh + lax.dot_general(s_ref[...], wrm_ref[:, lo:hi], (((1,), (1,)), ((), ())),
                                      preferred_element_type=F32)
        for idx, t_ref in enumerate(t3_refs):
            dh = dh + lax.dot_general(t_ref[0], wt_ref[idx * 512:(idx + 1) * 512, :], (((0,), (0,)), ((), ())),
                                      preferred_element_type=F32)
        x0 = jnp.where(i == 0, mp_ref[...], x_ref[...])
        xhat, rstd = _ln_stats(x0)
        dg_ref[...] += jnp.sum(dh * xhat, axis=0, keepdims=True)
        db_ref[...] += jnp.sum(dh, axis=0, keepdims=True)
        dx = _ln_bwd(dh, xhat, rstd, g_ref[...])
        dx_ref[...] = dx

        @pl.when(i == 0)
        def _():
            dmeta_ref[...] = dx

    seq = (nt - 1) * TILE
    return pl.pallas_call(
        body, name="dh_bwd", grid=(nt,),
        in_specs=[_row_spec(s.shape[1]) for s in secs] + [_t3_spec(512) for _ in secs_t3]
        + [_full_spec(w_rm.shape), _full_spec(w_t.shape), _row_spec(D_MODEL), _row_spec(D_MODEL, shift=True),
           _full_spec((TILE, D_MODEL)), _full_spec((1, D_MODEL))],
        out_specs=[_row_spec(D_MODEL, shift=True), _full_spec((TILE, D_MODEL)), _full_spec((1, D_MODEL)),
                   _full_spec((1, D_MODEL))],
        out_shape=[jax.ShapeDtypeStruct((seq, D_MODEL), F32), jax.ShapeDtypeStruct((TILE, D_MODEL), F32),
                   jax.ShapeDtypeStruct((1, D_MODEL), F32), jax.ShapeDtypeStruct((1, D_MODEL), F32)],
        compiler_params=_cp(56, ("arbitrary",)),
    )(*secs, *secs_t3, w_rm, w_t, dz, x, metapad, g_in)


RB = 256
SMALL_ROWS = 48


def _repack_weights(all_in, all_small):
    n_cw = D_CONV // N_DEV

    def body(a_ref, s_ref, wr_ref, wt_ref, mp_ref, cw_ref):
        full = jnp.concatenate([a_ref[d].T[:, :SHARD_IN] for d in range(N_DEV)], axis=1)
        qkv = full[:, :1536]
        wr_ref[:, :1536] = qkv
        wr_ref[:, 1536:OFF_F_R] = full[:, 1544:]
        wr_ref[:, OFF_F_R:] = jnp.concatenate([full[:, 1536:1544], jnp.zeros((RB, LANES - N_HEADS), BF16)], axis=1)
        wt_ref[...] = qkv.T

        @pl.when(pl.program_id(0) == 0)
        def _():
            mp_ref[0:PAD, :] = jnp.zeros((PAD, D_MODEL), F32)
            mp_ref[PAD:, :] = jnp.concatenate([s_ref[d, 0:N_META, :] for d in range(N_DEV)], axis=1)
            cw_ref[...] = jnp.concatenate([s_ref[d, N_META:, 0:n_cw] for d in range(N_DEV)], axis=1)

    return pl.pallas_call(
        body, name="repack_weights", grid=(D_MODEL // RB,),
        in_specs=[pl.BlockSpec((N_DEV, 512, RB), lambda i: (0, 0, i)), _full_spec((N_DEV, SMALL_ROWS, LANES))],
        out_specs=[pl.BlockSpec((RB, W_COLS), lambda i: (i, 0)), pl.BlockSpec((1536, RB), lambda i: (0, i)),
                   _full_spec((TILE, D_MODEL)), _full_spec((32, D_CONV))],
        out_shape=[jax.ShapeDtypeStruct((D_MODEL, W_COLS), BF16), jax.ShapeDtypeStruct((1536, D_MODEL), BF16),
                   jax.ShapeDtypeStruct((TILE, D_MODEL), F32), jax.ShapeDtypeStruct((32, D_CONV), F32)],
        compiler_params=_cp(40, ("arbitrary",)),
    )(all_in, all_small)


def _unpack_dw_in(dw_rm, dw_t):
    def body(dga_ref, du_ref, dug_ref, dgc_ref, dfl_ref, dq_ref, dk_ref, dv_ref, out_ref):
        full = jnp.concatenate([dq_ref[...].T, dk_ref[...].T, dv_ref[...].T, dfl_ref[:, 0:N_HEADS], dga_ref[...],
                                du_ref[...], dug_ref[...], dgc_ref[...]], axis=1)
        pad = jnp.zeros((RB, 512 - SHARD_IN), F32)
        for d in range(N_DEV):
            out_ref[d] = jnp.concatenate([full[:, SHARD_IN * d:SHARD_IN * (d + 1)], pad], axis=1).T

    rm = pl.BlockSpec((RB, 512), lambda i: (i, 0))
    tr = pl.BlockSpec((512, RB), lambda i: (0, i))
    return pl.pallas_call(
        body, name="unpack_dw_in", grid=(D_MODEL // RB,),
        in_specs=[rm, rm, rm, rm, pl.BlockSpec((RB, LANES), lambda i: (i, 0)), tr, tr, tr],
        out_specs=pl.BlockSpec((N_DEV, 512, RB), lambda i: (0, 0, i)),
        out_shape=jax.ShapeDtypeStruct((N_DEV, 512, D_MODEL), F32),
        compiler_params=_cp(48, ("arbitrary",)),
    )(*dw_rm, *dw_t)


def _local_step(x, target, metapad, cw, w_r, w_t, w_pw_full, w_out_full, ln_in_g, ln_in_b, b_f, conv_b, ln_conv_g,
                ln_conv_b, ln_out_g, ln_out_b):
    seq = x.shape[0]
    nt = seq // TILE + 1
    row = lambda a: a.reshape(1, -1).astype(F32)
    bf_pad = jnp.pad(row(b_f), ((0, 0), (0, LANES - N_HEADS)))
    g_in, b_in = row(ln_in_g), row(ln_in_b)
    g_cv, b_cv, c_b = row(ln_conv_g), row(ln_conv_b), row(conv_b)
    g_out, b_out = row(ln_out_g), row(ln_out_b)

    hb, qT3, kT3, vT3, k, v, ga, u, ug, gc, fl = _proj_fwd(x, metapad, g_in, b_in, w_r, nt)
    kx3 = _cumsum_fwd(fl, bf_pad, nt)
    oT3, o, lse4 = _attn_fwd(qT3, k, kx3, vT3, nt)
    co, hc, pw = _conv_fwd(u, ug, cw, c_b, g_cv, b_cv, w_pw_full, nt)
    y, dz, loss, dg_out, db_out = _out_fwd(o, ga, pw, gc, x, metapad, g_in, b_in, w_out_full, g_out, b_out,
                                            target, nt)
    doT3, dga, dgc, dw_out, dco, dw_pw, dg_cv, db_cv, dc_b = _out_bwd(dz, y, o, ga, pw, gc, w_out_full, hc, co,
                                                                      w_pw_full, g_cv, b_cv, nt)
    du, dug, dcw = _conv_bwd_taps(dco, u, ug, cw, nt)
    dqT3, dkT3, dvT3, dck, dcq4 = _attn_bwd(qT3, kT3, k, kx3, v, oT3, doT3, lse4, nt)
    dfl, dbf = _cumsum_bwd(dck, dcq4, fl, bf_pad, nt)
    secs = (dga, du, dug, dgc, dfl)
    secs_t3 = (dqT3, dkT3, dvT3)
    dw_rm = _dw_rowmajor(hb, secs, nt)
    dw_t = _dw_transposed(hb, secs_t3, nt)
    grad_x, dmetapad, dg_in, db_in = _dh_bwd(secs, secs_t3, w_r, w_t, dz, x, metapad, g_in, nt)
    pieces = dict(loss=loss, metapad=dmetapad, ln_in_g=dg_in, ln_in_b=db_in, w_in_rm=dw_rm, w_in_t=dw_t, b_f=dbf,
                  conv_w=dcw, conv_b=dc_b, ln_conv_g=dg_cv, ln_conv_b=db_cv, w_pw=dw_pw, w_out=dw_out,
                  ln_out_g=dg_out, ln_out_b=db_out)
    return grad_x, pieces


MESH = pl.DeviceIdType.MESH
ANY = pl.BlockSpec(memory_space=pl.ANY)


def _mesh_pos():
    return lax.axis_index("x"), lax.axis_index("y"), lax.axis_index("c")


def _gather_body(x_refs, out_refs, send_sems, recv_sems, local_sems):
    n = len(x_refs)
    x, y, c = _mesh_pos()
    me, sibling = (x, y, c), (x, y, 1 - c)
    chips = [(1 - x, y), (x, 1 - y), (1 - x, 1 - y)]

    def slot(a, px, py, pc):
        return out_refs[a].at[4 * px + 2 * py + pc]

    def copy(a, k, block, to, src=None):
        return pltpu.make_async_remote_copy(
            src_ref=slot(a, *block) if src is None else src, dst_ref=slot(a, *block),
            send_sem=send_sems.at[7 * a + k], recv_sem=recv_sems.at[7 * a + k], device_id=to,
            device_id_type=MESH)

    arrays = range(n)
    mine = [pltpu.make_async_copy(x_refs[a], slot(a, *me), local_sems.at[a]) for a in arrays]
    for cp in mine:
        cp.start()
    first = [copy(a, 0, me, sibling, src=x_refs[a]) for a in arrays]
    first += [copy(a, 1 + j, me, (*chip, c), src=x_refs[a]) for j, chip in enumerate(chips) for a in arrays]
    for cp in first:
        cp.start()
    passed = []
    for j, chip in enumerate(chips):
        for a in arrays:
            copy(a, 1 + j, (*chip, c), me).wait_recv()
            passed.append(copy(a, 4 + j, (*chip, c), sibling))
            passed[-1].start()
    for a in arrays:
        copy(a, 0, sibling, me).wait_recv()
        for j, chip in enumerate(chips):
            copy(a, 4 + j, (*chip, 1 - c), me).wait_recv()
    for cp in first + passed:
        cp.wait_send()
    for cp in mine:
        cp.wait()


def _all_gather(blks, name):
    n = len(blks)

    def body(*refs):
        _gather_body(refs[:n], refs[n:2 * n], *refs[2 * n:])

    return pl.pallas_call(
        body, name=name, out_shape=[jax.ShapeDtypeStruct((N_DEV, *b.shape), b.dtype) for b in blks],
        in_specs=[ANY] * n, out_specs=[ANY] * n,
        scratch_shapes=[pltpu.SemaphoreType.DMA((7 * n,)), pltpu.SemaphoreType.DMA((7 * n,)),
                        pltpu.SemaphoreType.DMA((n,))],
    )(*blks)


def _exchange_sibling(g8s):
    n = len(g8s)

    def body(*refs):
        g_refs, out_refs, send_sems, recv_sems = refs[:n], refs[n:2 * n], refs[2 * n], refs[2 * n + 1]
        x, y, c = _mesh_pos()
        cps = [pltpu.make_async_remote_copy(
            src_ref=g_refs[a].at[2 * q + (1 - c)], dst_ref=out_refs[a].at[q], send_sem=send_sems.at[4 * a + q],
            recv_sem=recv_sems.at[4 * a + q], device_id=(x, y, 1 - c), device_id_type=MESH)
            for a in range(n) for q in range(4)]
        for cp in cps:
            cp.start()
        for cp in cps:
            cp.wait()

    return pl.pallas_call(
        body, name="rs_sibling", out_shape=[jax.ShapeDtypeStruct((4, *g.shape[1:]), g.dtype) for g in g8s],
        in_specs=[ANY] * n, out_specs=[ANY] * n,
        scratch_shapes=[pltpu.SemaphoreType.DMA((4 * n,)), pltpu.SemaphoreType.DMA((4 * n,))],
    )(*g8s)


def _exchange_chips(p4s, small):
    n = len(p4s)

    def body(*refs):
        p_refs, s_ref, out_refs, g_ref = refs[:n], refs[n], refs[n + 1:2 * n + 1], refs[2 * n + 1]
        send_sems, recv_sems, g_send, g_recv, g_local = refs[2 * n + 2:]
        x, y, c = _mesh_pos()
        chips = [(1 - x, y), (x, 1 - y), (1 - x, 1 - y)]
        cps = [pltpu.make_async_remote_copy(
            src_ref=p_refs[a].at[2 * cx + cy], dst_ref=out_refs[a].at[k], send_sem=send_sems.at[3 * a + k],
            recv_sem=recv_sems.at[3 * a + k], device_id=(cx, cy, c), device_id_type=MESH)
            for k, (cx, cy) in enumerate(chips) for a in range(n)]
        for cp in cps:
            cp.start()
        _gather_body([s_ref], [g_ref], g_send, g_recv, g_local)
        for cp in cps:
            cp.wait()

    outs = pl.pallas_call(
        body, name="rs_chips",
        out_shape=[jax.ShapeDtypeStruct((3, *p.shape[1:]), p.dtype) for p in p4s]
        + [jax.ShapeDtypeStruct((N_DEV, *small.shape), small.dtype)],
        in_specs=[ANY] * (n + 1), out_specs=[ANY] * (n + 1),
        scratch_shapes=[pltpu.SemaphoreType.DMA((3 * n,)), pltpu.SemaphoreType.DMA((3 * n,)),
                        pltpu.SemaphoreType.DMA((7,)), pltpu.SemaphoreType.DMA((7,)), pltpu.SemaphoreType.DMA((1,))],
    )(*p4s, small)
    return outs[:n], outs[n]


def _rs_add_sibling(g8s, recvs, c_idx):
    n = len(g8s)

    def body(s_ref, *refs):
        g_refs, r_refs, p32_refs, pb_refs = (refs[k * n:(k + 1) * n] for k in range(4))
        for g_ref, r_ref, p32_ref, pb_ref in zip(g_refs, r_refs, p32_refs, pb_refs):
            p = g_ref[0] + r_ref[0]
            p32_ref[0] = p
            pb_ref[0] = p.astype(BF16)

    blk = lambda g: pl.BlockSpec((1, *g.shape[1:]), lambda q, s: (q, 0, 0))
    grid_spec = pltpu.PrefetchScalarGridSpec(
        num_scalar_prefetch=1, grid=(4,),
        in_specs=[pl.BlockSpec((1, *g.shape[1:]), lambda q, s: (2 * q + s[0], 0, 0)) for g in g8s]
        + [blk(g) for g in g8s],
        out_specs=[blk(g) for g in g8s] * 2)
    outs = pl.pallas_call(
        body, name="rs_add_sibling", grid_spec=grid_spec,
        out_shape=[jax.ShapeDtypeStruct((4, *g.shape[1:]), F32) for g in g8s]
        + [jax.ShapeDtypeStruct((4, *g.shape[1:]), BF16) for g in g8s],
        compiler_params=_cp(48, ("arbitrary",)),
    )(c_idx, *g8s, *recvs)
    return outs[:n], outs[n:]


def _rs_add_chips(p32s, recvs, q_idx):
    def body(s_ref, pin_ref, pout_ref, ppw_ref, rin_ref, rout_ref, rpw_ref, gin_ref, gout_ref, gpw_ref):
        def total(p_ref, r_ref):
            return ((p_ref[0] + r_ref[0].astype(F32)) + r_ref[1].astype(F32)) + r_ref[2].astype(F32)

        gin_ref[...] = total(pin_ref, rin_ref)[:SHARD_IN, :]
        gout_ref[0] = total(pout_ref, rout_ref)
        gpw_ref[0] = total(ppw_ref, rpw_ref)

    own = lambda p: pl.BlockSpec((1, *p.shape[1:]), lambda i, s: (s[0], 0, 0))
    whole = lambda shape: pl.BlockSpec(shape, lambda i, s: (0,) * len(shape))
    out_shapes = [(SHARD_IN, D_MODEL), (1, *p32s[1].shape[1:]), (1, *p32s[2].shape[1:])]
    grid_spec = pltpu.PrefetchScalarGridSpec(
        num_scalar_prefetch=1, grid=(1,),
        in_specs=[own(p) for p in p32s] + [whole(r.shape) for r in recvs],
        out_specs=[whole(s) for s in out_shapes])
    return pl.pallas_call(
        body, name="rs_add_chips", grid_spec=grid_spec,
        out_shape=[jax.ShapeDtypeStruct(s, F32) for s in out_shapes],
        compiler_params=_cp(48, ("arbitrary",)),
    )(q_idx, *p32s, *recvs)


SMALL_ROWS_G = 64
SMALL_LAYOUT = {
    "metapad": (0, N_META, D_MODEL), "conv_w": (16, 32, D_CONV), "ln_in_g": (48, 1, D_MODEL),
    "ln_in_b": (49, 1, D_MODEL), "b_f": (50, 1, LANES), "conv_b": (51, 1, D_CONV), "ln_conv_g": (52, 1, D_CONV),
    "ln_conv_b": (53, 1, D_CONV), "ln_out_g": (54, 1, D_MODEL), "ln_out_b": (55, 1, D_MODEL), "loss": (56, 1, LANES)}


def _pack_small(pieces):
    names = list(SMALL_LAYOUT)

    def body(*refs):
        out_ref = refs[-1]
        out_ref[...] = jnp.zeros_like(out_ref)
        for name, ref in zip(names, refs[:-1]):
            r0, nr, nl = SMALL_LAYOUT[name]
            src = ref[PAD:, :] if name == "metapad" else ref[...]
            out_ref[r0:r0 + nr, 0:nl] = src

    return pl.pallas_call(body, name="pack_small", out_shape=jax.ShapeDtypeStruct((SMALL_ROWS_G, D_MODEL), F32),
                          compiler_params=_cp(16))(*[pieces[n] for n in names])


def _sum_small(gathered):
    names = list(SMALL_LAYOUT)

    def body(a_ref, *out_refs):
        acc = a_ref[0]
        for d in range(1, N_DEV):
            acc = acc + a_ref[d]
        for name, ref in zip(names, out_refs):
            r0, nr, nl = SMALL_LAYOUT[name]
            ref[...] = acc[r0:r0 + nr, 0:nl]

    outs = pl.pallas_call(
        body, name="sum_small",
        out_shape=[jax.ShapeDtypeStruct(SMALL_LAYOUT[n][1:], F32) for n in names], compiler_params=_cp(16))(gathered)
    return dict(zip(names, outs))


def _adamw(ws, gs, ms, vs):
    n = len(ws)
    c1 = 1.0 - ADAM_B1 ** ADAM_STEP
    c2 = 1.0 - ADAM_B2 ** ADAM_STEP

    def body(*refs):
        w_refs, g_refs, m_refs, v_refs = (refs[k * n:(k + 1) * n] for k in range(4))
        d_refs, nm_refs, nv_refs = (refs[(4 + k) * n:(5 + k) * n] for k in range(3))
        for w_ref, g_ref, m_ref, v_ref, d_ref, nm_ref, nv_ref in zip(w_refs, g_refs, m_refs, v_refs, d_refs,
                                                                     nm_refs, nv_refs):
            g = g_ref[...]
            m = ADAM_B1 * m_ref[...] + (1.0 - ADAM_B1) * g
            v = ADAM_B2 * v_ref[...] + (1.0 - ADAM_B2) * (g * g)
            nm_ref[...] = m
            nv_ref[...] = v
            d_ref[...] = -ADAM_LR * ((m / c1) / (jnp.sqrt(v / c2) + ADAM_EPS) + ADAM_WD * w_ref[...])

    shapes = [jax.ShapeDtypeStruct(w.shape, F32) for w in ws]
    outs = pl.pallas_call(body, name="adamw", out_shape=shapes * 3, compiler_params=_cp(48))(*ws, *gs, *ms, *vs)
    return outs[:n], outs[n:2 * n], outs[2 * n:]


W_NAMES = ("meta", "ln_in_g", "ln_in_b", "w_in", "b_f", "conv_w", "conv_b", "ln_conv_g", "ln_conv_b", "w_pw",
           "w_out", "ln_out_g", "ln_out_b")


def kernel(x, meta, ln_in_g, ln_in_b, w_in, b_f, conv_w, conv_b, ln_conv_g, ln_conv_b, w_pw, w_out, ln_out_g, ln_out_b, loss_target, m_meta, m_ln_in_g, m_ln_in_b, m_w_in, m_b_f, m_conv_w, m_conv_b, m_ln_conv_g, m_ln_conv_b, m_w_pw, m_w_out, m_ln_out_g, m_ln_out_b, v_meta, v_ln_in_g, v_ln_in_b, v_w_in, v_b_f, v_conv_w, v_conv_b, v_ln_conv_g, v_ln_conv_b, v_w_pw, v_w_out, v_ln_out_g, v_ln_out_b):
    mx, my, mc = _mesh_pos()
    me = 4 * mx + 2 * my + mc
    n_meta_sh = D_MODEL // N_DEV
    n_cw_sh = D_CONV // N_DEV
    n_out_sh = D_MODEL // N_DEV
    n_pw_sh = D_CONV // N_DEV

    small_w = jnp.concatenate([meta, jnp.pad(conv_w[0], ((0, 1), (0, LANES - n_cw_sh)))], axis=0)
    all_in, all_out, all_pw, all_small = _all_gather(
        [jnp.pad(w_in[0].T, ((0, 512 - SHARD_IN), (0, 0))).astype(BF16), w_out[0].astype(BF16), w_pw[0].astype(BF16),
         small_w], "gather_weights")
    w_r, w_t, metapad, cw = _repack_weights(all_in, all_small)
    w_out_full = all_out.reshape(D_MODEL, D_MODEL)
    w_pw_full = all_pw.reshape(D_CONV, D_CONV)

    grad_x, pc = _local_step(x[0], loss_target[0], metapad, cw, w_r, w_t, w_pw_full, w_out_full, ln_in_g, ln_in_b,
                             b_f[0], conv_b[0], ln_conv_g[0], ln_conv_b[0], ln_out_g[0], ln_out_b[0])

    g8s = [_unpack_dw_in(pc["w_in_rm"], pc["w_in_t"]), pc["w_out"].reshape(N_DEV, n_out_sh, D_MODEL),
           pc["w_pw"].reshape(N_DEV, n_pw_sh, D_CONV)]
    from_sibling = _exchange_sibling(g8s)
    p32s, pbs = _rs_add_sibling(g8s, from_sibling, jnp.reshape(mc, (1,)).astype(jnp.int32))
    from_chips, all_small_g = _exchange_chips(pbs, _pack_small(pc))
    g_w_in, g_w_out, g_w_pw = _rs_add_chips(p32s, from_chips, jnp.reshape(2 * mx + my, (1,)).astype(jnp.int32))

    sm = _sum_small(all_small_g)
    grads = {
        "meta": lax.dynamic_slice_in_dim(sm["metapad"], me * n_meta_sh, n_meta_sh, axis=1),
        "ln_in_g": sm["ln_in_g"].reshape(D_MODEL), "ln_in_b": sm["ln_in_b"].reshape(D_MODEL), "w_in": g_w_in.T[None],
        "b_f": sm["b_f"][:, :N_HEADS],
        "conv_w": lax.dynamic_slice_in_dim(sm["conv_w"], me * n_cw_sh, n_cw_sh, axis=1)[None, :CONV_WIDTH],
        "conv_b": sm["conv_b"], "ln_conv_g": sm["ln_conv_g"], "ln_conv_b": sm["ln_conv_b"], "w_pw": g_w_pw,
        "w_out": g_w_out, "ln_out_g": sm["ln_out_g"], "ln_out_b": sm["ln_out_b"]}
    loss_all = sm["loss"][0, 0]

    weights = dict(meta=meta, ln_in_g=ln_in_g, ln_in_b=ln_in_b, w_in=w_in, b_f=b_f, conv_w=conv_w, conv_b=conv_b,
                   ln_conv_g=ln_conv_g, ln_conv_b=ln_conv_b, w_pw=w_pw, w_out=w_out, ln_out_g=ln_out_g,
                   ln_out_b=ln_out_b)
    moms = dict(meta=m_meta, ln_in_g=m_ln_in_g, ln_in_b=m_ln_in_b, w_in=m_w_in, b_f=m_b_f, conv_w=m_conv_w,
                conv_b=m_conv_b, ln_conv_g=m_ln_conv_g, ln_conv_b=m_ln_conv_b, w_pw=m_w_pw, w_out=m_w_out,
                ln_out_g=m_ln_out_g, ln_out_b=m_ln_out_b)
    vels = dict(meta=v_meta, ln_in_g=v_ln_in_g, ln_in_b=v_ln_in_b, w_in=v_w_in, b_f=v_b_f, conv_w=v_conv_w,
                conv_b=v_conv_b, ln_conv_g=v_ln_conv_g, ln_conv_b=v_ln_conv_b, w_pw=v_w_pw, w_out=v_w_out,
                ln_out_g=v_ln_out_g, ln_out_b=v_ln_out_b)

    def to_kernel(name, a):
        if name == "w_in":
            return a[0].T
        return a.reshape(1, -1) if a.ndim == 1 else a

    def from_kernel(name, a):
        return a.T[None] if name == "w_in" else a.reshape(weights[name].shape)

    upd = _adamw(*[[to_kernel(n, d[n]) for n in W_NAMES] for d in (weights, grads, moms, vels)])
    deltas, new_m, new_v = ([from_kernel(n, a) for n, a in zip(W_NAMES, part)] for part in upd)
    return (loss_all, grad_x[None], *[grads[n] for n in W_NAMES], *deltas, *new_m, *new_v)
```

```python
import jax
import jax.numpy as jnp
import numpy as np
from jax import lax
from jax.experimental import pallas as pl
from jax.experimental.pallas import tpu as pltpu

F32 = jnp.float32
BF16 = jnp.bfloat16

D_MODEL = 1024
D_ATTN = 512
D_CONV = 512
N_HEADS = 8
HEAD_DIM = 64
N_META = 16
CONV_WIDTH = 31
LN_EPS = 1e-5
ALPHA = 2.0 ** 0.25
SCALE = HEAD_DIM ** -0.5
LOG2E = 1.4426950408889634
ADAM_LR, ADAM_B1, ADAM_B2, ADAM_EPS, ADAM_WD, ADAM_STEP = 0.001, 0.9, 0.999, 1e-08, 0.01, 10

N_DEV = 8
D_IN = 3592
SHARD_IN = D_IN // N_DEV
TILE = 256
PAD = TILE - N_META
HALO = 32
SHIFT_ROWS = TILE + HALO
EXT_ROWS = SHIFT_ROWS + 8
NEG = -1e30
LANES = 128
W_COLS = 7 * 512 + LANES
OFF_GA_R, OFF_F_R = 1536, 3584
MIB = 1024 * 1024


def _cp(vmem_mib, sem=None):
    kw = dict(vmem_limit_bytes=vmem_mib * MIB)
    if sem is not None:
        kw["dimension_semantics"] = sem
    return pltpu.CompilerParams(**kw)


def _sigmoid(x):
    return 1.0 / (1.0 + jnp.exp(-x))


def _silu_and_grad(x):
    s = _sigmoid(x)
    return x * s, s * (1.0 + x * (1.0 - s))


def _ln_stats(x):
    mu = jnp.mean(x, axis=-1, keepdims=True)
    xc = x - mu
    var = jnp.mean(xc * xc, axis=-1, keepdims=True)
    rstd = lax.rsqrt(var + LN_EPS)
    return xc * rstd, rstd


def _ln_bwd(dy, xhat, rstd, g):
    dxh = dy * g
    m1 = jnp.mean(dxh, axis=-1, keepdims=True)
    m2 = jnp.mean(dxh * xhat, axis=-1, keepdims=True)
    return rstd * (dxh - m1 - xhat * m2)


def _row_spec(cols, shift=False):
    if shift:
        return pl.BlockSpec((TILE, cols), lambda i: (jnp.maximum(i - 1, 0), 0))
    return pl.BlockSpec((TILE, cols), lambda i: (i, 0))


def _full_spec(shape):
    nd = len(shape)
    return pl.BlockSpec(shape, lambda i: (0,) * nd)


def _t3_spec(ch):
    return pl.BlockSpec((1, ch, TILE), lambda i: (i, 0, 0))


def _proj_fwd(x, metapad, g_in, b_in, w_r, nt):
    lp = nt * TILE

    def body(x_ref, mp_ref, g_ref, b_ref, w_ref, hb_ref, qT_ref, kT_ref, vT_ref, k_ref, v_ref,
             ga_ref, u_ref, ug_ref, gc_ref, fl_ref):
        i = pl.program_id(0)
        x0 = jnp.where(i == 0, mp_ref[...], x_ref[...])
        xhat, _ = _ln_stats(x0)
        hb = (xhat * g_ref[...] + b_ref[...]).astype(BF16)
        hb_ref[...] = hb

        def sec(off, n=512):
            return jnp.dot(hb, w_ref[:, off:off + n], preferred_element_type=F32)

        qT_ref[0] = (sec(0) * (SCALE * LOG2E)).T.astype(BF16)
        k = sec(512)
        kT_ref[0] = k.T.astype(BF16)
        k_ref[...] = k.astype(BF16)
        v = sec(1024)
        vT_ref[0] = v.T.astype(BF16)
        v_ref[...] = v.astype(BF16)
        ga_ref[...] = sec(OFF_GA_R).astype(BF16)
        u_ref[...] = sec(OFF_GA_R + 512).astype(BF16)
        ug_ref[...] = sec(OFF_GA_R + 1024).astype(BF16)
        gc_ref[...] = sec(OFF_GA_R + 1536).astype(BF16)
        fl_ref[...] = sec(OFF_F_R, LANES)

    t3 = jax.ShapeDtypeStruct((nt, 512, TILE), BF16)
    rm = lambda dt: jax.ShapeDtypeStruct((lp, 512), dt)
    return pl.pallas_call(
        body, name="proj_fwd", grid=(nt,),
        in_specs=[_row_spec(D_MODEL, shift=True), _full_spec((TILE, D_MODEL)), _full_spec((1, D_MODEL)),
                  _full_spec((1, D_MODEL)), _full_spec((D_MODEL, W_COLS))],
        out_specs=[_row_spec(D_MODEL), _t3_spec(512), _t3_spec(512), _t3_spec(512), _row_spec(512), _row_spec(512),
                   _row_spec(512), _row_spec(512), _row_spec(512), _row_spec(512), _row_spec(LANES)],
        out_shape=[jax.ShapeDtypeStruct((lp, D_MODEL), BF16), t3, t3, t3, rm(BF16), rm(BF16),
                   rm(BF16), rm(BF16), rm(BF16), rm(BF16), jax.ShapeDtypeStruct((lp, LANES), F32)],
        compiler_params=_cp(56, ("arbitrary",)),
    )(x, metapad, g_in, b_in, w_r)


def _row_mask(i, shape):
    r = lax.broadcasted_iota(jnp.int32, shape, 0)
    return (r >= PAD) | (i > 0)


def _cumsum_fwd(fl, bf_pad, nt):
    lp = nt * TILE

    def body(fl_ref, bf_ref, kx_ref, carry):
        i = pl.program_id(0)

        @pl.when(i == 0)
        def _():
            carry[...] = jnp.zeros_like(carry)

        z = fl_ref[...] + bf_ref[...]
        lf = jnp.minimum(z, 0.0) - jnp.log(1.0 + jnp.exp(-jnp.abs(z)))
        lane = lax.broadcasted_iota(jnp.int32, (TILE, LANES), 1)
        real = _row_mask(i, (TILE, LANES))
        lf = jnp.where(real & (lane < N_HEADS), lf, 0.0)
        r = lax.broadcasted_iota(jnp.int32, (TILE, TILE), 0)
        c = lax.broadcasted_iota(jnp.int32, (TILE, TILE), 1)
        tril = (c <= r).astype(F32)
        cs = jnp.dot(tril, lf, precision=lax.Precision.HIGHEST, preferred_element_type=F32) + carry[...]
        carry[...] = cs[TILE - 1:TILE, :]
        bias = jnp.where(real, cs * (-LOG2E), NEG)
        hi = bias.astype(BF16).astype(F32)
        mid = (bias - hi).astype(BF16).astype(F32)
        lo = (bias - hi - mid).astype(BF16).astype(F32)
        for p in range(N_HEADS // 2):
            out = jnp.zeros((TILE, LANES), F32)
            for hh in range(2):
                for part, piece in enumerate((hi, mid, lo)):
                    dst, src = 3 * hh + part, 2 * p + hh
                    moved = piece if dst == src else pltpu.roll(piece, (dst - src) % LANES, 1)
                    out = jnp.where(lane == dst, moved, out)
            kx_ref[p] = out.astype(BF16)

    return pl.pallas_call(
        body, name="cumsum_fwd", grid=(nt,),
        in_specs=[_row_spec(LANES), _full_spec((1, LANES))],
        out_specs=pl.BlockSpec((N_HEADS // 2, TILE, LANES), lambda i: (0, i, 0)),
        out_shape=jax.ShapeDtypeStruct((N_HEADS // 2, lp, LANES), BF16),
        scratch_shapes=[pltpu.VMEM((1, LANES), F32)],
        compiler_params=_cp(32, ("arbitrary",)),
    )(fl, bf_pad)


def _head_rows(blk, hh):
    r = lax.broadcasted_iota(jnp.int32, blk.shape, 0)
    return jnp.where((r >= hh * HEAD_DIM) & (r < (hh + 1) * HEAD_DIM), blk, jnp.zeros_like(blk))


def _two_heads(blk):
    return jnp.concatenate([_head_rows(blk, 0), _head_rows(blk, 1)], axis=1)


def _bias_rows():
    r = lax.broadcasted_iota(jnp.int32, (LANES, 2 * TILE), 0)
    c = lax.broadcasted_iota(jnp.int32, (LANES, 2 * TILE), 1)
    return jnp.where(((r < 3) & (c < TILE)) | ((r >= 3) & (r < 6) & (c >= TILE)), 1.0, 0.0).astype(BF16)


def _diag_mask(s):
    kpos = lax.broadcasted_iota(jnp.int32, (TILE, TILE), 0)
    qpos = lax.broadcasted_iota(jnp.int32, (TILE, TILE), 1)
    return jnp.where(kpos <= qpos, s, NEG)


def _stream(n, first, nxt, scores, update):
    if n == 0:
        return
    scores(first, 0)

    def pair_body(_, idx):
        idx_b = nxt(idx)
        scores(idx_b, 1)
        update(idx, 0)
        idx_c = nxt(idx_b)
        scores(idx_c, 0)
        update(idx_b, 1)
        return idx_c

    idx = lax.fori_loop(0, (n - 1) // 2, pair_body, first)
    if n % 2 == 1:
        update(idx, 0)
    else:
        idx_b = nxt(idx)
        scores(idx_b, 1)
        update(idx, 0)
        update(idx_b, 1)


def _next_below_diagonal(idx):
    i, j = idx
    wrap = j + 1 >= i
    return jnp.where(wrap, i + 1, i), jnp.where(wrap, 0, j + 1)


def _tile_rows(t):
    return pl.ds(pl.multiple_of(t * TILE, TILE), TILE)


def _two_streams(nt):
    load, group = [0, 0], {}
    for i in sorted(range(1, nt), reverse=True):
        g = 0 if load[0] <= load[1] else 1
        group[i] = g
        load[g] += i
    rows = [[(i, i, j) for i in range(1, nt) if group[i] == g for j in range(i)] for g in range(2)]
    length = max(len(r) for r in rows)
    rows = [r + [(nt, 0, 0)] * (length - len(r)) for r in rows]
    return group, np.asarray(rows, np.int32).reshape(2, -1), length


def _attn_fwd(qT3, k, kx3, vT3, nt):
    lp = nt * TILE
    npair = N_HEADS // 2
    group, table, n_stream = _two_streams(nt)

    def body(tab_ref, qT_ref, k_ref, kx_ref, vT_ref, oT_ref, o_ref, lse_ref, sbuf, m_0, l_0, acc_0, m_1, l_1, acc_1):
        ones = _bias_rows()
        states = ((m_0, l_0, acc_0), (m_1, l_1, acc_1))

        def scores(i, j, slot):
            qcat = jnp.concatenate([_two_heads(qT_ref[i]), ones], axis=0)
            kext = jnp.concatenate([k_ref[_tile_rows(j), :], kx_ref[0, _tile_rows(j), :]], axis=1)
            sbuf[slot] = jnp.dot(kext, qcat, preferred_element_type=F32)

        def update(st, j, slot, state, diag):
            m_s, l_s, acc_s = state
            for hh in range(2):
                s = sbuf[slot, :, hh * TILE:(hh + 1) * TILE]
                vj = vT_ref[j, hh * HEAD_DIM:(hh + 1) * HEAD_DIM, :]
                if diag:
                    s = _diag_mask(s)
                    m_new = jnp.max(s, axis=0, keepdims=True)
                    p = jnp.exp2(s - m_new)
                    l_s[st, hh] = jnp.sum(p, axis=0, keepdims=True)
                    acc_s[st, hh] = jnp.dot(vj, p.astype(BF16), preferred_element_type=F32)
                else:
                    m_prev = m_s[st, hh]
                    m_new = jnp.maximum(m_prev, jnp.max(s, axis=0, keepdims=True))
                    a = jnp.exp2(m_prev - m_new)
                    p = jnp.exp2(s - m_new)
                    l_s[st, hh] = a * l_s[st, hh] + jnp.sum(p, axis=0, keepdims=True)
                    acc_s[st, hh] = a * acc_s[st, hh] + jnp.dot(vj, p.astype(BF16), preferred_element_type=F32)
                m_s[st, hh] = m_new

        _stream(nt, jnp.int32(0), lambda t: t + 1, lambda t, slot: scores(t, t, slot),
                lambda t, slot: update(t, t, slot, states[0], True))
        for dst, src in zip(states[1], states[0]):
            dst[0:nt] = src[0:nt]
        for m_s, l_s, acc_s in states:
            m_s[nt] = jnp.full(m_s.shape[1:], NEG, F32)
            l_s[nt] = jnp.zeros(l_s.shape[1:], F32)
            acc_s[nt] = jnp.zeros(acc_s.shape[1:], F32)

        def entry(g, t):
            return tab_ref[g, 3 * t], tab_ref[g, 3 * t + 1], tab_ref[g, 3 * t + 2]

        def scores2(t, slot):
            for g in range(2):
                _, qi, kj = entry(g, t)
                scores(qi, kj, 2 * g + slot)

        def update2(t, slot):
            for g in range(2):
                st, _, kj = entry(g, t)
                update(st, kj, 2 * g + slot, states[g], False)

        _stream(n_stream, jnp.int32(0), lambda t: t + 1, scores2, update2)

        for i in range(nt):
            m_s, l_s, acc_s = states[group.get(i, 0)]
            for hh in range(2):
                l = l_s[i, hh]
                oT_ref[i, hh * HEAD_DIM:(hh + 1) * HEAD_DIM, :] = acc_s[i, hh] / l
                lse_ref[0, i, hh:hh + 1, :] = m_s[i, hh] + jnp.log(l) * LOG2E
            o_ref[i * TILE:(i + 1) * TILE, :] = oT_ref[i].T.astype(BF16)

    blk_t = pl.BlockSpec((nt, LANES, TILE), lambda p, tab: (0, p, 0))
    blk_rm = pl.BlockSpec((lp, LANES), lambda p, tab: (0, p))
    blk_px = pl.BlockSpec((1, lp, LANES), lambda p, tab: (p, 0, 0))
    blk_st = pl.BlockSpec((1, nt, 8, TILE), lambda p, tab: (p, 0, 0, 0))
    state = [pltpu.VMEM((nt + 1, 2, 1, TILE), F32), pltpu.VMEM((nt + 1, 2, 1, TILE), F32),
             pltpu.VMEM((nt + 1, 2, HEAD_DIM, TILE), F32)]
    grid_spec = pltpu.PrefetchScalarGridSpec(
        num_scalar_prefetch=1, grid=(npair,), in_specs=[blk_t, blk_rm, blk_px, blk_t],
        out_specs=[blk_t, blk_rm, blk_st], scratch_shapes=[pltpu.VMEM((4, TILE, 2 * TILE), F32)] + state + state)
    return pl.pallas_call(
        body, name="attn_fwd", grid_spec=grid_spec,
        out_shape=[jax.ShapeDtypeStruct((nt, D_ATTN, TILE), F32),
                   jax.ShapeDtypeStruct((lp, D_ATTN), BF16),
                   jax.ShapeDtypeStruct((npair, nt, 8, TILE), F32)],
        compiler_params=_cp(60, ("arbitrary",)),
    )(jnp.asarray(table), qT3, k, kx3, vT3)


def _attn_bwd(qT3, kT3, k, kx3, v, oT3, doT3, lse4, nt):
    lp = nt * TILE
    npair = N_HEADS // 2

    def body(qT_ref, kT_ref, k_ref, kx_ref, v_ref, oT_ref, doT_ref, lse_ref,
             dqT_ref, dkT_ref, dvT_ref, dck_ref, dcq_ref, sbuf, dpbuf, dq_s, dk_s, dv_s, dc_s):
        ones = _bias_rows()
        nt_dims = (((1,), (1,)), ((), ()))

        def scores(idx, slot):
            i, j = idx
            qcat = jnp.concatenate([_two_heads(qT_ref[i]), ones], axis=0)
            kext = jnp.concatenate([k_ref[_tile_rows(j), :], kx_ref[0, _tile_rows(j), :]], axis=1)
            sbuf[slot] = jnp.dot(kext, qcat, preferred_element_type=F32)
            dpbuf[slot] = jnp.dot(v_ref[_tile_rows(j), :], _two_heads(doT_ref[i]), preferred_element_type=F32)

        def update(idx, slot, diag):
            i, j = idx
            for hh in range(2):
                hs = slice(hh * HEAD_DIM, (hh + 1) * HEAD_DIM)
                s = sbuf[slot, :, hh * TILE:(hh + 1) * TILE]
                if diag:
                    s = _diag_mask(s)
                p = jnp.exp2(s - lse_ref[0, i, hh:hh + 1, :])
                doh = doT_ref[i, hs, :]
                delta = jnp.sum(doh.astype(F32) * oT_ref[i, hs, :], axis=0, keepdims=True)
                ds = p * (dpbuf[slot, :, hh * TILE:(hh + 1) * TILE] - delta)
                dsb = ds.astype(BF16)
                dv = lax.dot_general(doh, p.astype(BF16), nt_dims, preferred_element_type=F32)
                dk = lax.dot_general(qT_ref[i, hs, :], dsb, nt_dims, preferred_element_type=F32)
                dq = jnp.dot(kT_ref[j, hs, :], dsb, preferred_element_type=F32)
                dc = ds[:, :LANES] + ds[:, LANES:]
                dcq = jnp.sum(ds, axis=0, keepdims=True)
                if diag:
                    dv_s[j, hh] = dv
                    dk_s[j, hh] = dk
                    dc_s[j, hh] = dc
                    dq_s[i, hs, :] = dq
                    dcq_ref[0, i, hh:hh + 1, :] = dcq
                else:
                    dv_s[j, hh] += dv
                    dk_s[j, hh] += dk
                    dc_s[j, hh] += dc
                    dq_s[i, hs, :] += dq
                    dcq_ref[0, i, hh:hh + 1, :] += dcq

        dcq_ref[...] = jnp.zeros_like(dcq_ref)
        zero = jnp.int32(0)
        _stream(nt, (zero, zero), lambda idx: (idx[0] + 1, idx[1] + 1), scores,
                lambda idx, slot: update(idx, slot, True))
        _stream(nt * (nt - 1) // 2, (zero + 1, zero), _next_below_diagonal, scores,
                lambda idx, slot: update(idx, slot, False))

        lane = lax.broadcasted_iota(jnp.int32, (TILE, LANES), 1)

        def finish(t, carry):
            dck = jnp.zeros((TILE, LANES), F32)
            for hh in range(2):
                hs = slice(hh * HEAD_DIM, (hh + 1) * HEAD_DIM)
                dkT_ref[t, hs, :] = (dk_s[t, hh] * (1.0 / LOG2E)).astype(BF16)
                dvT_ref[t, hs, :] = dv_s[t, hh].astype(BF16)
                dck = jnp.where(lane == hh, -jnp.sum(dc_s[t, hh], axis=1, keepdims=True), dck)
            dck_ref[0, _tile_rows(t), :] = dck
            dqT_ref[t] = (dq_s[t] * SCALE).astype(BF16)
            return carry

        lax.fori_loop(0, nt, finish, 0)

    blk_t = pl.BlockSpec((nt, LANES, TILE), lambda p: (0, p, 0))
    blk_rm = pl.BlockSpec((lp, LANES), lambda p: (0, p))
    blk_px = pl.BlockSpec((1, lp, LANES), lambda p: (p, 0, 0))
    blk_st = pl.BlockSpec((1, nt, 8, TILE), lambda p: (p, 0, 0, 0))
    t3 = jax.ShapeDtypeStruct((nt, D_ATTN, TILE), BF16)
    return pl.pallas_call(
        body, name="attn_bwd", grid=(npair,),
        in_specs=[blk_t, blk_t, blk_rm, blk_px, blk_rm, blk_t, blk_t, blk_st],
        out_specs=[blk_t, blk_t, blk_t, blk_px, blk_st],
        out_shape=[t3, t3, t3, jax.ShapeDtypeStruct((npair, lp, LANES), F32),
                   jax.ShapeDtypeStruct((npair, nt, 8, TILE), F32)],
        scratch_shapes=[pltpu.VMEM((2, TILE, 2 * TILE), F32), pltpu.VMEM((2, TILE, 2 * TILE), F32),
                        pltpu.VMEM((nt, LANES, TILE), F32), pltpu.VMEM((nt, 2, HEAD_DIM, TILE), F32),
                        pltpu.VMEM((nt, 2, HEAD_DIM, TILE), F32), pltpu.VMEM((nt, 2, TILE, LANES), F32)],
        compiler_params=_cp(60, ("arbitrary",)),
    )(qT3, kT3, k, kx3, v, oT3, doT3, lse4)


def _glu(u, ug, i):
    return jnp.where(_row_mask(i, u.shape), u.astype(F32) * _sigmoid(ug.astype(F32)), 0.0)


def _shifted_copies(dst, src):
    for ph in range(8):
        dst[ph] = src[ph:ph + SHIFT_ROWS, :]


def _tap_window(sh, off, lanes, row0=0, rows=TILE):
    base = (off // 8) * 8 + row0
    return sh[off % 8, base:base + rows, lanes]


def _conv_fwd(u, ug, conv_w, conv_b, g, b, w_pw, nt):
    lp = nt * TILE

    def body(u_ref, ug_ref, up_ref, ugp_ref, w_ref, cb_ref, g_ref, b_ref, wpw_ref,
             co_ref, hc_ref, pw_ref, ext, sh):
        i = pl.program_id(0)
        prev = _glu(up_ref[...], ugp_ref[...], i - 1)
        ext[0:HALO, :] = jnp.where(i > 0, prev[TILE - HALO:, :], 0.0)
        ext[HALO:HALO + TILE, :] = _glu(u_ref[...], ug_ref[...], i)
        ext[HALO + TILE:, :] = jnp.zeros((8, D_CONV), F32)
        _shifted_copies(sh, ext)
        for lb in range(D_CONV // LANES):
            lanes = slice(lb * LANES, (lb + 1) * LANES)
            acc = jnp.zeros((TILE, LANES), F32) + cb_ref[:, lanes]
            for t in range(CONV_WIDTH):
                off = HALO - (CONV_WIDTH - 1) + t
                acc = acc + w_ref[t:t + 1, lanes] * _tap_window(sh, off, lanes)
            co_ref[:, lanes] = acc
        xhat, _ = _ln_stats(co_ref[...])
        a, _ = _silu_and_grad(xhat * g_ref[...] + b_ref[...])
        hc = a.astype(BF16)
        hc_ref[...] = hc
        pw_ref[...] = jnp.dot(hc, wpw_ref[...], preferred_element_type=F32).astype(BF16)

    rm = lambda dt: jax.ShapeDtypeStruct((lp, D_CONV), dt)
    return pl.pallas_call(
        body, name="conv_fwd", grid=(nt,),
        in_specs=[_row_spec(512), _row_spec(512), _row_spec(512, shift=True), _row_spec(512, shift=True),
                  _full_spec((32, 512)), _full_spec((1, 512)), _full_spec((1, 512)), _full_spec((1, 512)),
                  _full_spec((512, 512))],
        out_specs=[_row_spec(512), _row_spec(512), _row_spec(512)],
        out_shape=[rm(F32), rm(BF16), rm(BF16)],
        scratch_shapes=[pltpu.VMEM((EXT_ROWS, D_CONV), F32), pltpu.VMEM((8, SHIFT_ROWS, D_CONV), F32)],
        compiler_params=_cp(40, ("arbitrary",)),
    )(u, ug, u, ug, conv_w, conv_b, g, b, w_pw)


def _out_fwd(o, ga, pw, gc, x, metapad, g_in, b_in, w_out, g_out, b_out, target, nt):
    lp = nt * TILE

    def body(o_ref, ga_ref, pw_ref, gc_ref, x_ref, mp_ref, gi_ref, bi_ref, wo_ref, go_ref, bo_ref, t_ref,
             y_ref, dz_ref, loss_ref, dgo_ref, dbo_ref):
        i = pl.program_id(0)

        @pl.when(i == 0)
        def _():
            loss_ref[...] = jnp.zeros_like(loss_ref)
            dgo_ref[...] = jnp.zeros_like(dgo_ref)
            dbo_ref[...] = jnp.zeros_like(dbo_ref)

        x0 = jnp.where(i == 0, mp_ref[...], x_ref[...])
        xhat, _ = _ln_stats(x0)
        h = xhat * gi_ref[...] + bi_ref[...]
        ya, _ = _silu_and_grad(ga_ref[...].astype(F32))
        yc, _ = _silu_and_grad(gc_ref[...].astype(F32))
        ya = (o_ref[...].astype(F32) * ya).astype(BF16)
        yc = (pw_ref[...].astype(F32) * yc).astype(BF16)
        y_ref[:, :D_ATTN] = ya
        y_ref[:, D_ATTN:] = yc
        z = ALPHA * h + jnp.dot(ya, wo_ref[:D_ATTN, :], preferred_element_type=F32) \
            + jnp.dot(yc, wo_ref[D_ATTN:, :], preferred_element_type=F32)
        zhat, rstd = _ln_stats(z)
        out = zhat * go_ref[...] + bo_ref[...]
        live = (i > 0).astype(F32)
        err = (out - t_ref[...]) * live
        dout = err * (1.0 / D_MODEL)
        loss_ref[...] += 0.5 * jnp.sum(jnp.sum(err * dout, axis=0, keepdims=True), axis=1, keepdims=True)
        dgo_ref[...] += jnp.sum(dout * zhat, axis=0, keepdims=True)
        dbo_ref[...] += jnp.sum(dout, axis=0, keepdims=True)
        dz_ref[...] = _ln_bwd(dout, zhat, rstd, go_ref[...])

    return pl.pallas_call(
        body, name="out_fwd", grid=(nt,),
        in_specs=[_row_spec(512), _row_spec(512), _row_spec(512), _row_spec(512),
                  _row_spec(D_MODEL, shift=True), _full_spec((TILE, D_MODEL)), _full_spec((1, D_MODEL)),
                  _full_spec((1, D_MODEL)), _full_spec((D_MODEL, D_MODEL)), _full_spec((1, D_MODEL)),
                  _full_spec((1, D_MODEL)), _row_spec(D_MODEL, shift=True)],
        out_specs=[_row_spec(D_MODEL), _row_spec(D_MODEL), _full_spec((1, LANES)), _full_spec((1, D_MODEL)),
                   _full_spec((1, D_MODEL))],
        out_shape=[jax.ShapeDtypeStruct((lp, D_MODEL), BF16), jax.ShapeDtypeStruct((lp, D_MODEL), F32),
                   jax.ShapeDtypeStruct((1, LANES), F32), jax.ShapeDtypeStruct((1, D_MODEL), F32),
                   jax.ShapeDtypeStruct((1, D_MODEL), F32)],
        compiler_params=_cp(40, ("arbitrary",)),
    )(o, ga, pw, gc, x, metapad, g_in, b_in, w_out, g_out, b_out, target)


def _out_bwd(dz, y, o, ga, pw, gc, w_out, hc, co, w_pw, g_cv, b_cv, nt):
    lp = nt * TILE

    def body(dz_ref, y_ref, o_ref, ga_ref, pw_ref, gc_ref, wo_ref, hc_ref, co_ref, wpw_ref, g_ref, b_ref,
             doT_ref, dga_ref, dgc_ref, dwo_ref, dco_ref, dwpw_ref, dg_ref, db_ref, dcb_ref):
        i = pl.program_id(0)

        @pl.when(i == 0)
        def _():
            dwo_ref[...] = jnp.zeros_like(dwo_ref)
            dwpw_ref[...] = jnp.zeros_like(dwpw_ref)
            dg_ref[...] = jnp.zeros_like(dg_ref)
            db_ref[...] = jnp.zeros_like(db_ref)
            dcb_ref[...] = jnp.zeros_like(dcb_ref)

        dzb = dz_ref[...].astype(BF16)
        nt_dims = (((1,), (1,)), ((), ()))
        tn_dims = (((0,), (0,)), ((), ()))
        dya = lax.dot_general(dzb, wo_ref[:D_ATTN, :], nt_dims, preferred_element_type=F32)
        dyc = lax.dot_general(dzb, wo_ref[D_ATTN:, :], nt_dims, preferred_element_type=F32)
        sa, sga = _silu_and_grad(ga_ref[...].astype(F32))
        sc, sgc = _silu_and_grad(gc_ref[...].astype(F32))
        doT_ref[0] = (dya * sa).T.astype(BF16)
        dga_ref[...] = (dya * o_ref[...].astype(F32) * sga).astype(BF16)
        dpw_b = (dyc * sc).astype(BF16)
        dgc_ref[...] = (dyc * pw_ref[...].astype(F32) * sgc).astype(BF16)
        dwo_ref[...] += lax.dot_general(y_ref[...], dzb, tn_dims, preferred_element_type=F32)

        dhc = lax.dot_general(dpw_b, wpw_ref[...], nt_dims, preferred_element_type=F32)
        xhat, rstd = _ln_stats(co_ref[...])
        _, sg = _silu_and_grad(xhat * g_ref[...] + b_ref[...])
        dln = dhc * sg
        dg_ref[...] += jnp.sum(dln * xhat, axis=0, keepdims=True)
        db_ref[...] += jnp.sum(dln, axis=0, keepdims=True)
        dco = _ln_bwd(dln, xhat, rstd, g_ref[...])
        dco_ref[...] = dco
        dcb_ref[...] += jnp.sum(dco, axis=0, keepdims=True)
        dwpw_ref[...] += lax.dot_general(hc_ref[...], dpw_b, tn_dims, preferred_element_type=F32)

    rm = jax.ShapeDtypeStruct((lp, 512), BF16)
    vec = jax.ShapeDtypeStruct((1, D_CONV), F32)
    return pl.pallas_call(
        body, name="out_bwd", grid=(nt,),
        in_specs=[_row_spec(D_MODEL), _row_spec(D_MODEL), _row_spec(512), _row_spec(512), _row_spec(512),
                  _row_spec(512), _full_spec((D_MODEL, D_MODEL)), _row_spec(512), _row_spec(512),
                  _full_spec((512, 512)), _full_spec((1, 512)), _full_spec((1, 512))],
        out_specs=[_t3_spec(512), _row_spec(512), _row_spec(512), _full_spec((D_MODEL, D_MODEL)), _row_spec(512),
                   _full_spec((512, 512)), _full_spec((1, 512)), _full_spec((1, 512)), _full_spec((1, 512))],
        out_shape=[jax.ShapeDtypeStruct((nt, 512, TILE), BF16), rm, rm, jax.ShapeDtypeStruct((D_MODEL, D_MODEL), F32),
                   jax.ShapeDtypeStruct((lp, D_CONV), F32), jax.ShapeDtypeStruct((512, 512), F32), vec, vec, vec],
        compiler_params=_cp(56, ("arbitrary",)),
    )(dz, y, o, ga, pw, gc, w_out, hc, co, w_pw, g_cv, b_cv)


def _conv_bwd_taps(dco, u, ug, conv_w, nt):
    lp = nt * TILE

    def body(dco_ref, dcon_ref, u_ref, ug_ref, up_ref, ugp_ref, w3_ref, du_ref, dug_ref, dw_ref, ext, dext, sh, dsh,
             dhg_s, dw_s):
        i = pl.program_id(0)

        @pl.when(i == 0)
        def _():
            dw_s[...] = jnp.zeros_like(dw_s)

        prev = _glu(up_ref[...], ugp_ref[...], i - 1)
        ext[0:HALO, :] = jnp.where(i > 0, prev[TILE - HALO:, :], 0.0)
        ext[HALO:HALO + TILE, :] = _glu(u_ref[...], ug_ref[...], i)
        ext[HALO + TILE:, :] = jnp.zeros((8, D_CONV), F32)
        dext[0:TILE, :] = dco_ref[...]
        dext[TILE:TILE + HALO, :] = jnp.where(i < nt - 1, dcon_ref[0:HALO, :], 0.0)
        dext[TILE + HALO:, :] = jnp.zeros((8, D_CONV), F32)
        _shifted_copies(sh, ext)
        _shifted_copies(dsh, dext)
        stripe = 32

        def stripe_body(rb, carry):
            row0 = pl.multiple_of(rb * stripe, stripe)
            dco = dco_ref[pl.ds(row0, stripe), :]
            dhg = jnp.zeros((stripe, D_CONV), F32)
            for t in range(CONV_WIDTH):
                off = HALO - (CONV_WIDTH - 1) + t
                back = CONV_WIDTH - 1 - t
                prod = dco * sh[off % 8, pl.ds((off // 8) * 8 + row0, stripe), :]
                part = prod[0:8, :]
                for r8 in range(1, stripe // 8):
                    part = part + prod[8 * r8:8 * r8 + 8, :]
                dw_s[t] += part
                dhg = dhg + w3_ref[t] * dsh[back % 8, pl.ds((back // 8) * 8 + row0, stripe), :]
            dhg_s[pl.ds(row0, stripe), :] = dhg
            return carry

        lax.fori_loop(0, TILE // stripe, stripe_body, 0)

        @pl.when(i == nt - 1)
        def _():
            dw_ref[...] = jnp.sum(dw_s[...], axis=1)

        dhg = jnp.where(_row_mask(i, (TILE, D_CONV)), dhg_s[...], 0.0)
        sg = _sigmoid(ug_ref[...].astype(F32))
        du_ref[...] = (dhg * sg).astype(BF16)
        dug_ref[...] = (dhg * u_ref[...].astype(F32) * sg * (1.0 - sg)).astype(BF16)

    rm = jax.ShapeDtypeStruct((lp, D_CONV), BF16)
    nxt = pl.BlockSpec((TILE, 512), lambda i: (jnp.minimum(i + 1, nt - 1), 0))
    ext_t = pltpu.VMEM((EXT_ROWS, D_CONV), F32)
    sh_t = pltpu.VMEM((8, SHIFT_ROWS, D_CONV), F32)
    return pl.pallas_call(
        body, name="conv_bwd_taps", grid=(nt,),
        in_specs=[_row_spec(512), nxt, _row_spec(512), _row_spec(512), _row_spec(512, shift=True),
                  _row_spec(512, shift=True), _full_spec((32, 1, 512))],
        out_specs=[_row_spec(512), _row_spec(512), _full_spec((32, 512))],
        out_shape=[rm, rm, jax.ShapeDtypeStruct((32, D_CONV), F32)],
        scratch_shapes=[ext_t, ext_t, sh_t, sh_t, pltpu.VMEM((TILE, D_CONV), F32), pltpu.VMEM((32, 8, D_CONV), F32)],
        compiler_params=_cp(48, ("arbitrary",)),
    )(dco, dco, u, ug, u, ug, conv_w.reshape(32, 1, D_CONV))


def _cumsum_bwd(dck, dcq4, fl, bf_pad, nt):
    lp = nt * TILE

    def body(dck_ref, dcq_ref, fl_ref, bf_ref, dfl_ref, dbf_ref, carry):
        i = pl.program_id(0)
        tile = nt - 1 - i

        @pl.when(i == 0)
        def _():
            carry[...] = jnp.zeros_like(carry)
            dbf_ref[...] = jnp.zeros_like(dbf_ref)

        dc = jnp.zeros((TILE, LANES), F32)
        for p in range(N_HEADS // 2):
            dq_rows = jnp.concatenate([dcq_ref[p, 0], jnp.zeros((LANES - 8, TILE), F32)], axis=0)
            both = dck_ref[p] + dq_rows.T
            dc = dc + (both if p == 0 else pltpu.roll(both, 2 * p, 1))
        r = lax.broadcasted_iota(jnp.int32, (TILE, TILE), 0)
        c = lax.broadcasted_iota(jnp.int32, (TILE, TILE), 1)
        triu = (c >= r).astype(F32)
        dlf = jnp.dot(triu, dc, precision=lax.Precision.HIGHEST, preferred_element_type=F32) + carry[...]
        carry[...] = dlf[0:1, :]
        z = fl_ref[...] + bf_ref[...]
        lane = lax.broadcasted_iota(jnp.int32, (TILE, LANES), 1)
        dfl = jnp.where(_row_mask(tile, (TILE, LANES)) & (lane < N_HEADS), dlf * _sigmoid(-z), 0.0)
        dfl_ref[...] = dfl.astype(BF16)
        dbf_ref[...] += jnp.sum(dfl, axis=0, keepdims=True)

    rev = lambda i: (nt - 1 - i, 0)
    return pl.pallas_call(
        body, name="cumsum_bwd", grid=(nt,),
        in_specs=[pl.BlockSpec((N_HEADS // 2, TILE, LANES), lambda i: (0, nt - 1 - i, 0)),
                  pl.BlockSpec((N_HEADS // 2, 1, 8, TILE), lambda i: (0, nt - 1 - i, 0, 0)),
                  pl.BlockSpec((TILE, LANES), rev), _full_spec((1, LANES))],
        out_specs=[pl.BlockSpec((TILE, LANES), rev), _full_spec((1, LANES))],
        out_shape=[jax.ShapeDtypeStruct((lp, LANES), BF16), jax.ShapeDtypeStruct((1, LANES), F32)],
        scratch_shapes=[pltpu.VMEM((1, LANES), F32)],
        compiler_params=_cp(32, ("arbitrary",)),
    )(dck, dcq4, fl, bf_pad)


def _dw_rowmajor(hb, secs, nt):
    n = len(secs)

    def body(*refs):
        hb_ref, sec_refs, out_refs = refs[0], refs[1:1 + n], refs[1 + n:]
        i = pl.program_id(0)

        @pl.when(i == 0)
        def _():
            for o_ref in out_refs:
                o_ref[...] = jnp.zeros_like(o_ref)

        hb_t = hb_ref[...]
        for s_ref, o_ref in zip(sec_refs, out_refs):
            o_ref[...] += lax.dot_general(hb_t, s_ref[...], (((0,), (0,)), ((), ())), preferred_element_type=F32)

    return pl.pallas_call(
        body, name="dw_rowmajor", grid=(nt,),
        in_specs=[_row_spec(D_MODEL)] + [_row_spec(s.shape[1]) for s in secs],
        out_specs=[_full_spec((D_MODEL, s.shape[1])) for s in secs],
        out_shape=[jax.ShapeDtypeStruct((D_MODEL, s.shape[1]), F32) for s in secs],
        compiler_params=_cp(48, ("arbitrary",)),
    )(hb, *secs)


def _dw_transposed(hb, secs_t3, nt):
    n = len(secs_t3)

    def body(*refs):
        hb_ref, sec_refs, out_refs = refs[0], refs[1:1 + n], refs[1 + n:]
        i = pl.program_id(0)

        @pl.when(i == 0)
        def _():
            for o_ref in out_refs:
                o_ref[...] = jnp.zeros_like(o_ref)

        hb_t = hb_ref[...]
        for s_ref, o_ref in zip(sec_refs, out_refs):
            o_ref[...] += jnp.dot(s_ref[0], hb_t, preferred_element_type=F32)

    return pl.pallas_call(
        body, name="dw_transposed", grid=(nt,),
        in_specs=[_row_spec(D_MODEL)] + [_t3_spec(512) for _ in secs_t3],
        out_specs=[_full_spec((512, D_MODEL)) for _ in secs_t3],
        out_shape=[jax.ShapeDtypeStruct((512, D_MODEL), F32) for _ in secs_t3],
        compiler_params=_cp(40, ("arbitrary",)),
    )(hb, *secs_t3)


def _dh_bwd(secs, secs_t3, w_rm, w_t, dz, x, metapad, g_in, nt):
    n, m = len(secs), len(secs_t3)
    offs = OFF_GA_R + np.cumsum([0] + [s.shape[1] for s in secs])

    def body(*refs):
        sec_refs, t3_refs = refs[:n], refs[n:n + m]
        wrm_ref, wt_ref, dz_ref, x_ref, mp_ref, g_ref = refs[n + m:n + m + 6]
        dx_ref, dmeta_ref, dg_ref, db_ref = refs[n + m + 6:]
        i = pl.program_id(0)

        @pl.when(i == 0)
        def _():
            dg_ref[...] = jnp.zeros_like(dg_ref)
            db_ref[...] = jnp.zeros_like(db_ref)

        dh = ALPHA * dz_ref[...]
        for s_ref, lo, hi in zip(sec_refs, offs[:-1], offs[1:]):
            dh = dh + lax.dot_general(s_ref[...], wrm_ref[:, lo:hi], (((1,), (1,)), ((), ())),
                                      preferred_element_type=F32)
        for idx, t_ref in enumerate(t3_refs):
            dh = dh + lax.dot_general(t_ref[0], wt_ref[idx * 512:(idx + 1) * 512, :], (((0,), (0,)), ((), ())),
                                      preferred_element_type=F32)
        x0 = jnp.where(i == 0, mp_ref[...], x_ref[...])
        xhat, rstd = _ln_stats(x0)
        dg_ref[...] += jnp.sum(dh * xhat, axis=0, keepdims=True)
        db_ref[...] += jnp.sum(dh, axis=0, keepdims=True)
        dx = _ln_bwd(dh, xhat, rstd, g_ref[...])
        dx_ref[...] = dx

        @pl.when(i == 0)
        def _():
            dmeta_ref[...] = dx

    seq = (nt - 1) * TILE
    return pl.pallas_call(
        body, name="dh_bwd", grid=(nt,),
        in_specs=[_row_spec(s.shape[1]) for s in secs] + [_t3_spec(512) for _ in secs_t3]
        + [_full_spec(w_rm.shape), _full_spec(w_t.shape), _row_spec(D_MODEL), _row_spec(D_MODEL, shift=True),
           _full_spec((TILE, D_MODEL)), _full_spec((1, D_MODEL))],
        out_specs=[_row_spec(D_MODEL, shift=True), _full_spec((TILE, D_MODEL)), _full_spec((1, D_MODEL)),
                   _full_spec((1, D_MODEL))],
        out_shape=[jax.ShapeDtypeStruct((seq, D_MODEL), F32), jax.ShapeDtypeStruct((TILE, D_MODEL), F32),
                   jax.ShapeDtypeStruct((1, D_MODEL), F32), jax.ShapeDtypeStruct((1, D_MODEL), F32)],
        compiler_params=_cp(56, ("arbitrary",)),
    )(*secs, *secs_t3, w_rm, w_t, dz, x, metapad, g_in)


RB = 256
SMALL_ROWS = 48


def _repack_weights(all_in, all_small):
    n_cw = D_CONV // N_DEV

    def body(a_ref, s_ref, wr_ref, wt_ref, mp_ref, cw_ref):
        full = jnp.concatenate([a_ref[d].T[:, :SHARD_IN] for d in range(N_DEV)], axis=1)
        qkv = full[:, :1536]
        wr_ref[:, :1536] = qkv
        wr_ref[:, 1536:OFF_F_R] = full[:, 1544:]
        wr_ref[:, OFF_F_R:] = jnp.concatenate([full[:, 1536:1544], jnp.zeros((RB, LANES - N_HEADS), BF16)], axis=1)
        wt_ref[...] = qkv.T

        @pl.when(pl.program_id(0) == 0)
        def _():
            mp_ref[0:PAD, :] = jnp.zeros((PAD, D_MODEL), F32)
            mp_ref[PAD:, :] = jnp.concatenate([s_ref[d, 0:N_META, :] for d in range(N_DEV)], axis=1)
            cw_ref[...] = jnp.concatenate([s_ref[d, N_META:, 0:n_cw] for d in range(N_DEV)], axis=1)

    return pl.pallas_call(
        body, name="repack_weights", grid=(D_MODEL // RB,),
        in_specs=[pl.BlockSpec((N_DEV, 512, RB), lambda i: (0, 0, i)), _full_spec((N_DEV, SMALL_ROWS, LANES))],
        out_specs=[pl.BlockSpec((RB, W_COLS), lambda i: (i, 0)), pl.BlockSpec((1536, RB), lambda i: (0, i)),
                   _full_spec((TILE, D_MODEL)), _full_spec((32, D_CONV))],
        out_shape=[jax.ShapeDtypeStruct((D_MODEL, W_COLS), BF16), jax.ShapeDtypeStruct((1536, D_MODEL), BF16),
                   jax.ShapeDtypeStruct((TILE, D_MODEL), F32), jax.ShapeDtypeStruct((32, D_CONV), F32)],
        compiler_params=_cp(40, ("arbitrary",)),
    )(all_in, all_small)


def _unpack_dw_in(dw_rm, dw_t):
    def body(dga_ref, du_ref, dug_ref, dgc_ref, dfl_ref, dq_ref, dk_ref, dv_ref, out_ref):
        full = jnp.concatenate([dq_ref[...].T, dk_ref[...].T, dv_ref[...].T, dfl_ref[:, 0:N_HEADS], dga_ref[...],
                                du_ref[...], dug_ref[...], dgc_ref[...]], axis=1)
        pad = jnp.zeros((RB, 512 - SHARD_IN), F32)
        for d in range(N_DEV):
            out_ref[d] = jnp.concatenate([full[:, SHARD_IN * d:SHARD_IN * (d + 1)], pad], axis=1).T

    rm = pl.BlockSpec((RB, 512), lambda i: (i, 0))
    tr = pl.BlockSpec((512, RB), lambda i: (0, i))
    return pl.pallas_call(
        body, name="unpack_dw_in", grid=(D_MODEL // RB,),
        in_specs=[rm, rm, rm, rm, pl.BlockSpec((RB, LANES), lambda i: (i, 0)), tr, tr, tr],
        out_specs=pl.BlockSpec((N_DEV, 512, RB), lambda i: (0, 0, i)),
        out_shape=jax.ShapeDtypeStruct((N_DEV, 512, D_MODEL), F32),
        compiler_params=_cp(48, ("arbitrary",)),
    )(*dw_rm, *dw_t)


def _local_step(x, target, metapad, cw, w_r, w_t, w_pw_full, w_out_full, ln_in_g, ln_in_b, b_f, conv_b, ln_conv_g,
                ln_conv_b, ln_out_g, ln_out_b):
    seq = x.shape[0]
    nt = seq // TILE + 1
    row = lambda a: a.reshape(1, -1).astype(F32)
    bf_pad = jnp.pad(row(b_f), ((0, 0), (0, LANES - N_HEADS)))
    g_in, b_in = row(ln_in_g), row(ln_in_b)
    g_cv, b_cv, c_b = row(ln_conv_g), row(ln_conv_b), row(conv_b)
    g_out, b_out = row(ln_out_g), row(ln_out_b)

    hb, qT3, kT3, vT3, k, v, ga, u, ug, gc, fl = _proj_fwd(x, metapad, g_in, b_in, w_r, nt)
    kx3 = _cumsum_fwd(fl, bf_pad, nt)
    oT3, o, lse4 = _attn_fwd(qT3, k, kx3, vT3, nt)
    co, hc, pw = _conv_fwd(u, ug, cw, c_b, g_cv, b_cv, w_pw_full, nt)
    y, dz, loss, dg_out, db_out = _out_fwd(o, ga, pw, gc, x, metapad, g_in, b_in, w_out_full, g_out, b_out,
                                            target, nt)
    doT3, dga, dgc, dw_out, dco, dw_pw, dg_cv, db_cv, dc_b = _out_bwd(dz, y, o, ga, pw, gc, w_out_full, hc, co,
                                                                      w_pw_full, g_cv, b_cv, nt)
    du, dug, dcw = _conv_bwd_taps(dco, u, ug, cw, nt)
    dqT3, dkT3, dvT3, dck, dcq4 = _attn_bwd(qT3, kT3, k, kx3, v, oT3, doT3, lse4, nt)
    dfl, dbf = _cumsum_bwd(dck, dcq4, fl, bf_pad, nt)
    secs = (dga, du, dug, dgc, dfl)
    secs_t3 = (dqT3, dkT3, dvT3)
    dw_rm = _dw_rowmajor(hb, secs, nt)
    dw_t = _dw_transposed(hb, secs_t3, nt)
    grad_x, dmetapad, dg_in, db_in = _dh_bwd(secs, secs_t3, w_r, w_t, dz, x, metapad, g_in, nt)
    pieces = dict(loss=loss, metapad=dmetapad, ln_in_g=dg_in, ln_in_b=db_in, w_in_rm=dw_rm, w_in_t=dw_t, b_f=dbf,
                  conv_w=dcw, conv_b=dc_b, ln_conv_g=dg_cv, ln_conv_b=db_cv, w_pw=dw_pw, w_out=dw_out,
                  ln_out_g=dg_out, ln_out_b=db_out)
    return grad_x, pieces


MESH = pl.DeviceIdType.MESH
ANY = pl.BlockSpec(memory_space=pl.ANY)


def _mesh_pos():
    return lax.axis_index("x"), lax.axis_index("y"), lax.axis_index("c")


GATHER_SEMS = 8


def _gather_body(x_refs, out_refs, send_sems, recv_sems, local_sems):
    n = len(x_refs)
    x, y, c = _mesh_pos()
    me, sibling = (x, y, c), (x, y, 1 - c)
    xn, yn, dg = (1 - x, y), (x, 1 - y), (1 - x, 1 - y)

    def slot(a, px, py, pc, half=None):
        blk = out_refs[a].at[4 * px + 2 * py + pc]
        if half is None:
            return blk
        rows = blk.shape[0] // 2
        return blk.at[pl.ds(half * rows, rows)]

    def copy(a, k, block, to, src=None, half=None):
        return pltpu.make_async_remote_copy(
            src_ref=slot(a, *block, half) if src is None else src, dst_ref=slot(a, *block, half),
            send_sem=send_sems.at[GATHER_SEMS * a + k], recv_sem=recv_sems.at[GATHER_SEMS * a + k], device_id=to,
            device_id_type=MESH)

    arrays = range(n)
    mine = [pltpu.make_async_copy(x_refs[a], slot(a, *me), local_sems.at[a]) for a in arrays]
    for cp in mine:
        cp.start()
    sent = []
    for a in arrays:
        sent += [copy(a, 0, me, sibling, src=x_refs[a]), copy(a, 1, me, (*xn, c), src=x_refs[a]),
                 copy(a, 2, me, (*yn, c), src=x_refs[a])]
    for cp in sent:
        cp.start()

    def also(cp):
        cp.start()
        sent.append(cp)

    for a in arrays:
        copy(a, 2, (*yn, c), me).wait_recv()
        also(copy(a, 3, (*yn, c), (*xn, c), half=0))
        also(copy(a, 6, (*yn, c), sibling))
        copy(a, 1, (*xn, c), me).wait_recv()
        also(copy(a, 4, (*xn, c), (*yn, c), half=1))
        also(copy(a, 5, (*xn, c), sibling))
    for a in arrays:
        copy(a, 3, (*dg, c), me, half=0).wait_recv()
        copy(a, 4, (*dg, c), me, half=1).wait_recv()
        also(copy(a, 7, (*dg, c), sibling))
    for a in arrays:
        copy(a, 0, sibling, me).wait_recv()
        copy(a, 5, (*xn, 1 - c), me).wait_recv()
        copy(a, 6, (*yn, 1 - c), me).wait_recv()
        copy(a, 7, (*dg, 1 - c), me).wait_recv()
    for cp in sent:
        cp.wait_send()
    for cp in mine:
        cp.wait()


def _all_gather(blks, name):
    n = len(blks)

    def body(*refs):
        _gather_body(refs[:n], refs[n:2 * n], *refs[2 * n:])

    return pl.pallas_call(
        body, name=name, out_shape=[jax.ShapeDtypeStruct((N_DEV, *b.shape), b.dtype) for b in blks],
        in_specs=[ANY] * n, out_specs=[ANY] * n,
        scratch_shapes=[pltpu.SemaphoreType.DMA((GATHER_SEMS * n,)), pltpu.SemaphoreType.DMA((GATHER_SEMS * n,)),
                        pltpu.SemaphoreType.DMA((n,))],
    )(*blks)


def _exchange_sibling(g8s):
    n = len(g8s)

    def body(*refs):
        g_refs, out_refs, send_sems, recv_sems = refs[:n], refs[n:2 * n], refs[2 * n], refs[2 * n + 1]
        x, y, c = _mesh_pos()
        cps = [pltpu.make_async_remote_copy(
            src_ref=g_refs[a].at[2 * q + (1 - c)], dst_ref=out_refs[a].at[q], send_sem=send_sems.at[4 * a + q],
            recv_sem=recv_sems.at[4 * a + q], device_id=(x, y, 1 - c), device_id_type=MESH)
            for a in range(n) for q in range(4)]
        for cp in cps:
            cp.start()
        for cp in cps:
            cp.wait()

    return pl.pallas_call(
        body, name="rs_sibling", out_shape=[jax.ShapeDtypeStruct((4, *g.shape[1:]), g.dtype) for g in g8s],
        in_specs=[ANY] * n, out_specs=[ANY] * n,
        scratch_shapes=[pltpu.SemaphoreType.DMA((4 * n,)), pltpu.SemaphoreType.DMA((4 * n,))],
    )(*g8s)


def _exchange_chips(p4s, small):
    n = len(p4s)

    def body(*refs):
        p_refs, s_ref, out_refs, g_ref = refs[:n], refs[n], refs[n + 1:2 * n + 1], refs[2 * n + 1]
        send_sems, recv_sems, g_send, g_recv, g_local = refs[2 * n + 2:]
        x, y, c = _mesh_pos()
        chips = [(1 - x, y), (x, 1 - y), (1 - x, 1 - y)]
        cps = [pltpu.make_async_remote_copy(
            src_ref=p_refs[a].at[2 * cx + cy], dst_ref=out_refs[a].at[k], send_sem=send_sems.at[3 * a + k],
            recv_sem=recv_sems.at[3 * a + k], device_id=(cx, cy, c), device_id_type=MESH)
            for k, (cx, cy) in enumerate(chips) for a in range(n)]
        for cp in cps:
            cp.start()
        _gather_body([s_ref], [g_ref], g_send, g_recv, g_local)
        for cp in cps:
            cp.wait()

    outs = pl.pallas_call(
        body, name="rs_chips",
        out_shape=[jax.ShapeDtypeStruct((3, *p.shape[1:]), p.dtype) for p in p4s]
        + [jax.ShapeDtypeStruct((N_DEV, *small.shape), small.dtype)],
        in_specs=[ANY] * (n + 1), out_specs=[ANY] * (n + 1),
        scratch_shapes=[pltpu.SemaphoreType.DMA((3 * n,)), pltpu.SemaphoreType.DMA((3 * n,)),
                        pltpu.SemaphoreType.DMA((GATHER_SEMS,)), pltpu.SemaphoreType.DMA((GATHER_SEMS,)),
                        pltpu.SemaphoreType.DMA((1,))],
    )(*p4s, small)
    return outs[:n], outs[n]


def _rs_add_sibling(g8s, recvs, c_idx):
    n = len(g8s)

    def body(s_ref, *refs):
        g_refs, r_refs, p32_refs, pb_refs = (refs[k * n:(k + 1) * n] for k in range(4))
        for g_ref, r_ref, p32_ref, pb_ref in zip(g_refs, r_refs, p32_refs, pb_refs):
            p = g_ref[0] + r_ref[0]
            p32_ref[0] = p
            pb_ref[0] = p.astype(BF16)

    blk = lambda g: pl.BlockSpec((1, *g.shape[1:]), lambda q, s: (q, 0, 0))
    grid_spec = pltpu.PrefetchScalarGridSpec(
        num_scalar_prefetch=1, grid=(4,),
        in_specs=[pl.BlockSpec((1, *g.shape[1:]), lambda q, s: (2 * q + s[0], 0, 0)) for g in g8s]
        + [blk(g) for g in g8s],
        out_specs=[blk(g) for g in g8s] * 2)
    outs = pl.pallas_call(
        body, name="rs_add_sibling", grid_spec=grid_spec,
        out_shape=[jax.ShapeDtypeStruct((4, *g.shape[1:]), F32) for g in g8s]
        + [jax.ShapeDtypeStruct((4, *g.shape[1:]), BF16) for g in g8s],
        compiler_params=_cp(48, ("arbitrary",)),
    )(c_idx, *g8s, *recvs)
    return outs[:n], outs[n:]


def _rs_add_chips(p32s, recvs, q_idx):
    def body(s_ref, pin_ref, pout_ref, ppw_ref, rin_ref, rout_ref, rpw_ref, gin_ref, gout_ref, gpw_ref):
        def total(p_ref, r_ref):
            return ((p_ref[0] + r_ref[0].astype(F32)) + r_ref[1].astype(F32)) + r_ref[2].astype(F32)

        gin_ref[...] = total(pin_ref, rin_ref)[:SHARD_IN, :]
        gout_ref[0] = total(pout_ref, rout_ref)
        gpw_ref[0] = total(ppw_ref, rpw_ref)

    own = lambda p: pl.BlockSpec((1, *p.shape[1:]), lambda i, s: (s[0], 0, 0))
    whole = lambda shape: pl.BlockSpec(shape, lambda i, s: (0,) * len(shape))
    out_shapes = [(SHARD_IN, D_MODEL), (1, *p32s[1].shape[1:]), (1, *p32s[2].shape[1:])]
    grid_spec = pltpu.PrefetchScalarGridSpec(
        num_scalar_prefetch=1, grid=(1,),
        in_specs=[own(p) for p in p32s] + [whole(r.shape) for r in recvs],
        out_specs=[whole(s) for s in out_shapes])
    return pl.pallas_call(
        body, name="rs_add_chips", grid_spec=grid_spec,
        out_shape=[jax.ShapeDtypeStruct(s, F32) for s in out_shapes],
        compiler_params=_cp(48, ("arbitrary",)),
    )(q_idx, *p32s, *recvs)


SMALL_ROWS_G = 64
SMALL_LAYOUT = {
    "metapad": (0, N_META, D_MODEL), "conv_w": (16, 32, D_CONV), "ln_in_g": (48, 1, D_MODEL),
    "ln_in_b": (49, 1, D_MODEL), "b_f": (50, 1, LANES), "conv_b": (51, 1, D_CONV), "ln_conv_g": (52, 1, D_CONV),
    "ln_conv_b": (53, 1, D_CONV), "ln_out_g": (54, 1, D_MODEL), "ln_out_b": (55, 1, D_MODEL), "loss": (56, 1, LANES)}


def _pack_small(pieces):
    names = list(SMALL_LAYOUT)

    def body(*refs):
        out_ref = refs[-1]
        out_ref[...] = jnp.zeros_like(out_ref)
        for name, ref in zip(names, refs[:-1]):
            r0, nr, nl = SMALL_LAYOUT[name]
            src = ref[PAD:, :] if name == "metapad" else ref[...]
            out_ref[r0:r0 + nr, 0:nl] = src

    return pl.pallas_call(body, name="pack_small", out_shape=jax.ShapeDtypeStruct((SMALL_ROWS_G, D_MODEL), F32),
                          compiler_params=_cp(16))(*[pieces[n] for n in names])


def _sum_small(gathered):
    names = list(SMALL_LAYOUT)

    def body(a_ref, *out_refs):
        acc = a_ref[0]
        for d in range(1, N_DEV):
            acc = acc + a_ref[d]
        for name, ref in zip(names, out_refs):
            r0, nr, nl = SMALL_LAYOUT[name]
            ref[...] = acc[r0:r0 + nr, 0:nl]

    outs = pl.pallas_call(
        body, name="sum_small",
        out_shape=[jax.ShapeDtypeStruct(SMALL_LAYOUT[n][1:], F32) for n in names], compiler_params=_cp(16))(gathered)
    return dict(zip(names, outs))


def _adamw(ws, gs, ms, vs):
    n = len(ws)
    c1 = 1.0 - ADAM_B1 ** ADAM_STEP
    c2 = 1.0 - ADAM_B2 ** ADAM_STEP

    def body(*refs):
        w_refs, g_refs, m_refs, v_refs = (refs[k * n:(k + 1) * n] for k in range(4))
        d_refs, nm_refs, nv_refs = (refs[(4 + k) * n:(5 + k) * n] for k in range(3))
        for w_ref, g_ref, m_ref, v_ref, d_ref, nm_ref, nv_ref in zip(w_refs, g_refs, m_refs, v_refs, d_refs,
                                                                     nm_refs, nv_refs):
            g = g_ref[...]
            m = ADAM_B1 * m_ref[...] + (1.0 - ADAM_B1) * g
            v = ADAM_B2 * v_ref[...] + (1.0 - ADAM_B2) * (g * g)
            nm_ref[...] = m
            nv_ref[...] = v
            d_ref[...] = -ADAM_LR * ((m / c1) / (jnp.sqrt(v / c2) + ADAM_EPS) + ADAM_WD * w_ref[...])

    shapes = [jax.ShapeDtypeStruct(w.shape, F32) for w in ws]
    outs = pl.pallas_call(body, name="adamw", out_shape=shapes * 3, compiler_params=_cp(48))(*ws, *gs, *ms, *vs)
    return outs[:n], outs[n:2 * n], outs[2 * n:]


W_NAMES = ("meta", "ln_in_g", "ln_in_b", "w_in", "b_f", "conv_w", "conv_b", "ln_conv_g", "ln_conv_b", "w_pw",
           "w_out", "ln_out_g", "ln_out_b")


def kernel(x, meta, ln_in_g, ln_in_b, w_in, b_f, conv_w, conv_b, ln_conv_g, ln_conv_b, w_pw, w_out, ln_out_g, ln_out_b, loss_target, m_meta, m_ln_in_g, m_ln_in_b, m_w_in, m_b_f, m_conv_w, m_conv_b, m_ln_conv_g, m_ln_conv_b, m_w_pw, m_w_out, m_ln_out_g, m_ln_out_b, v_meta, v_ln_in_g, v_ln_in_b, v_w_in, v_b_f, v_conv_w, v_conv_b, v_ln_conv_g, v_ln_conv_b, v_w_pw, v_w_out, v_ln_out_g, v_ln_out_b):
    mx, my, mc = _mesh_pos()
    me = 4 * mx + 2 * my + mc
    n_meta_sh = D_MODEL // N_DEV
    n_cw_sh = D_CONV // N_DEV
    n_out_sh = D_MODEL // N_DEV
    n_pw_sh = D_CONV // N_DEV

    small_w = jnp.concatenate([meta, jnp.pad(conv_w[0], ((0, 1), (0, LANES - n_cw_sh)))], axis=0)
    all_in, all_out, all_pw, all_small = _all_gather(
        [jnp.pad(w_in[0].T, ((0, 512 - SHARD_IN), (0, 0))).astype(BF16), w_out[0].astype(BF16), w_pw[0].astype(BF16),
         small_w], "gather_weights")
    w_r, w_t, metapad, cw = _repack_weights(all_in, all_small)
    w_out_full = all_out.reshape(D_MODEL, D_MODEL)
    w_pw_full = all_pw.reshape(D_CONV, D_CONV)

    grad_x, pc = _local_step(x[0], loss_target[0], metapad, cw, w_r, w_t, w_pw_full, w_out_full, ln_in_g, ln_in_b,
                             b_f[0], conv_b[0], ln_conv_g[0], ln_conv_b[0], ln_out_g[0], ln_out_b[0])

    g8s = [_unpack_dw_in(pc["w_in_rm"], pc["w_in_t"]), pc["w_out"].reshape(N_DEV, n_out_sh, D_MODEL),
           pc["w_pw"].reshape(N_DEV, n_pw_sh, D_CONV)]
    from_sibling = _exchange_sibling(g8s)
    p32s, pbs = _rs_add_sibling(g8s, from_sibling, jnp.reshape(mc, (1,)).astype(jnp.int32))
    from_chips, all_small_g = _exchange_chips(pbs, _pack_small(pc))
    g_w_in, g_w_out, g_w_pw = _rs_add_chips(p32s, from_chips, jnp.reshape(2 * mx + my, (1,)).astype(jnp.int32))

    sm = _sum_small(all_small_g)
    grads = {
        "meta": lax.dynamic_slice_in_dim(sm["metapad"], me * n_meta_sh, n_meta_sh, axis=1),
        "ln_in_g": sm["ln_in_g"].reshape(D_MODEL), "ln_in_b": sm["ln_in_b"].reshape(D_MODEL), "w_in": g_w_in.T[None],
        "b_f": sm["b_f"][:, :N_HEADS],
        "conv_w": lax.dynamic_slice_in_dim(sm["conv_w"], me * n_cw_sh, n_cw_sh, axis=1)[None, :CONV_WIDTH],
        "conv_b": sm["conv_b"], "ln_conv_g": sm["ln_conv_g"], "ln_conv_b": sm["ln_conv_b"], "w_pw": g_w_pw,
        "w_out": g_w_out, "ln_out_g": sm["ln_out_g"], "ln_out_b": sm["ln_out_b"]}
    loss_all = sm["loss"][0, 0]

    weights = dict(meta=meta, ln_in_g=ln_in_g, ln_in_b=ln_in_b, w_in=w_in, b_f=b_f, conv_w=conv_w, conv_b=conv_b,
                   ln_conv_g=ln_conv_g, ln_conv_b=ln_conv_b, w_pw=w_pw, w_out=w_out, ln_out_g=ln_out_g,
                   ln_out_b=ln_out_b)
    moms = dict(meta=m_meta, ln_in_g=m_ln_in_g, ln_in_b=m_ln_in_b, w_in=m_w_in, b_f=m_b_f, conv_w=m_conv_w,
                conv_b=m_conv_b, ln_conv_g=m_ln_conv_g, ln_conv_b=m_ln_conv_b, w_pw=m_w_pw, w_out=m_w_out,
                ln_out_g=m_ln_out_g, ln_out_b=m_ln_out_b)
    vels = dict(meta=v_meta, ln_in_g=v_ln_in_g, ln_in_b=v_ln_in_b, w_in=v_w_in, b_f=v_b_f, conv_w=v_conv_w,
                conv_b=v_conv_b, ln_conv_g=v_ln_conv_g, ln_conv_b=v_ln_conv_b, w_pw=v_w_pw, w_out=v_w_out,
                ln_out_g=v_ln_out_g, ln_out_b=v_ln_out_b)

    def to_kernel(name, a):
        if name == "w_in":
            return a[0].T
        return a.reshape(1, -1) if a.ndim == 1 else a

    def from_kernel(name, a):
        return a.T[None] if name == "w_in" else a.reshape(weights[name].shape)

    upd = _adamw(*[[to_kernel(n, d[n]) for n in W_NAMES] for d in (weights, grads, moms, vels)])
    deltas, new_m, new_v = ([from_kernel(n, a) for n, a in zip(W_NAMES, part)] for part in upd)
    return (loss_all, grad_x[None], *[grads[n] for n in W_NAMES], *deltas, *new_m, *new_v)
```

```python
import jax
import jax.numpy as jnp
import numpy as np
from jax import lax
from jax.experimental import pallas as pl
from jax.experimental.pallas import tpu as pltpu

F32 = jnp.float32
BF16 = jnp.bfloat16

D_MODEL = 1024
D_ATTN = 512
D_CONV = 512
N_HEADS = 8
HEAD_DIM = 64
N_META = 16
CONV_WIDTH = 31
LN_EPS = 1e-5
ALPHA = 2.0 ** 0.25
SCALE = HEAD_DIM ** -0.5
LOG2E = 1.4426950408889634
ADAM_LR, ADAM_B1, ADAM_B2, ADAM_EPS, ADAM_WD, ADAM_STEP = 0.001, 0.9, 0.999, 1e-08, 0.01, 10

N_DEV = 8
D_IN = 3592
SHARD_IN = D_IN // N_DEV
TILE = 256
PAD = TILE - N_META
HALO = 32
SHIFT_ROWS = TILE + HALO
EXT_ROWS = SHIFT_ROWS + 8
NEG = -1e30
LANES = 128
W_COLS = 7 * 512 + LANES
OFF_GA_R, OFF_F_R = 1536, 3584
MIB = 1024 * 1024


def _cp(vmem_mib, sem=None):
    kw = dict(vmem_limit_bytes=vmem_mib * MIB)
    if sem is not None:
        kw["dimension_semantics"] = sem
    return pltpu.CompilerParams(**kw)


def _sigmoid(x):
    return 1.0 / (1.0 + jnp.exp(-x))


def _silu_and_grad(x):
    s = _sigmoid(x)
    return x * s, s * (1.0 + x * (1.0 - s))


def _ln_stats(x):
    mu = jnp.mean(x, axis=-1, keepdims=True)
    xc = x - mu
    var = jnp.mean(xc * xc, axis=-1, keepdims=True)
    rstd = lax.rsqrt(var + LN_EPS)
    return xc * rstd, rstd


def _ln_bwd(dy, xhat, rstd, g):
    dxh = dy * g
    m1 = jnp.mean(dxh, axis=-1, keepdims=True)
    m2 = jnp.mean(dxh * xhat, axis=-1, keepdims=True)
    return rstd * (dxh - m1 - xhat * m2)


def _row_spec(cols, shift=False):
    if shift:
        return pl.BlockSpec((TILE, cols), lambda i: (jnp.maximum(i - 1, 0), 0))
    return pl.BlockSpec((TILE, cols), lambda i: (i, 0))


def _full_spec(shape):
    nd = len(shape)
    return pl.BlockSpec(shape, lambda i: (0,) * nd)


def _t3_spec(ch):
    return pl.BlockSpec((1, ch, TILE), lambda i: (i, 0, 0))


def _proj_fwd(x, metapad, g_in, b_in, w_r, nt):
    lp = nt * TILE

    def body(x_ref, mp_ref, g_ref, b_ref, w_ref, hb_ref, qT_ref, kT_ref, vT_ref, k_ref, v_ref,
             ga_ref, u_ref, ug_ref, gc_ref, fl_ref):
        i = pl.program_id(0)
        x0 = jnp.where(i == 0, mp_ref[...], x_ref[...])
        xhat, _ = _ln_stats(x0)
        hb = (xhat * g_ref[...] + b_ref[...]).astype(BF16)
        hb_ref[...] = hb

        def sec(off, n=512):
            return jnp.dot(hb, w_ref[:, off:off + n], preferred_element_type=F32)

        qT_ref[0] = (sec(0) * (SCALE * LOG2E)).T.astype(BF16)
        k = sec(512)
        kT_ref[0] = k.T.astype(BF16)
        k_ref[...] = k.astype(BF16)
        v = sec(1024)
        vT_ref[0] = v.T.astype(BF16)
        v_ref[...] = v.astype(BF16)
        ga_ref[...] = sec(OFF_GA_R).astype(BF16)
        u_ref[...] = sec(OFF_GA_R + 512).astype(BF16)
        ug_ref[...] = sec(OFF_GA_R + 1024).astype(BF16)
        gc_ref[...] = sec(OFF_GA_R + 1536).astype(BF16)
        fl_ref[...] = sec(OFF_F_R, LANES)

    t3 = jax.ShapeDtypeStruct((nt, 512, TILE), BF16)
    rm = lambda dt: jax.ShapeDtypeStruct((lp, 512), dt)
    return pl.pallas_call(
        body, name="proj_fwd", grid=(nt,),
        in_specs=[_row_spec(D_MODEL, shift=True), _full_spec((TILE, D_MODEL)), _full_spec((1, D_MODEL)),
                  _full_spec((1, D_MODEL)), _full_spec((D_MODEL, W_COLS))],
        out_specs=[_row_spec(D_MODEL), _t3_spec(512), _t3_spec(512), _t3_spec(512), _row_spec(512), _row_spec(512),
                   _row_spec(512), _row_spec(512), _row_spec(512), _row_spec(512), _row_spec(LANES)],
        out_shape=[jax.ShapeDtypeStruct((lp, D_MODEL), BF16), t3, t3, t3, rm(BF16), rm(BF16),
                   rm(BF16), rm(BF16), rm(BF16), rm(BF16), jax.ShapeDtypeStruct((lp, LANES), F32)],
        compiler_params=_cp(56, ("arbitrary",)),
    )(x, metapad, g_in, b_in, w_r)


def _row_mask(i, shape):
    r = lax.broadcasted_iota(jnp.int32, shape, 0)
    return (r >= PAD) | (i > 0)


def _cumsum_fwd(fl, bf_pad, nt):
    lp = nt * TILE

    def body(fl_ref, bf_ref, kx_ref, carry):
        i = pl.program_id(0)

        @pl.when(i == 0)
        def _():
            carry[...] = jnp.zeros_like(carry)

        z = fl_ref[...] + bf_ref[...]
        lf = jnp.minimum(z, 0.0) - jnp.log(1.0 + jnp.exp(-jnp.abs(z)))
        lane = lax.broadcasted_iota(jnp.int32, (TILE, LANES), 1)
        real = _row_mask(i, (TILE, LANES))
        lf = jnp.where(real & (lane < N_HEADS), lf, 0.0)
        r = lax.broadcasted_iota(jnp.int32, (TILE, TILE), 0)
        c = lax.broadcasted_iota(jnp.int32, (TILE, TILE), 1)
        tril = (c <= r).astype(F32)
        cs = jnp.dot(tril, lf, precision=lax.Precision.HIGHEST, preferred_element_type=F32) + carry[...]
        carry[...] = cs[TILE - 1:TILE, :]
        bias = jnp.where(real, cs * (-LOG2E), NEG)
        hi = bias.astype(BF16).astype(F32)
        mid = (bias - hi).astype(BF16).astype(F32)
        lo = (bias - hi - mid).astype(BF16).astype(F32)
        for p in range(N_HEADS // 2):
            out = jnp.zeros((TILE, LANES), F32)
            for hh in range(2):
                for part, piece in enumerate((hi, mid, lo)):
                    dst, src = 3 * hh + part, 2 * p + hh
                    moved = piece if dst == src else pltpu.roll(piece, (dst - src) % LANES, 1)
                    out = jnp.where(lane == dst, moved, out)
            kx_ref[p] = out.astype(BF16)

    return pl.pallas_call(
        body, name="cumsum_fwd", grid=(nt,),
        in_specs=[_row_spec(LANES), _full_spec((1, LANES))],
        out_specs=pl.BlockSpec((N_HEADS // 2, TILE, LANES), lambda i: (0, i, 0)),
        out_shape=jax.ShapeDtypeStruct((N_HEADS // 2, lp, LANES), BF16),
        scratch_shapes=[pltpu.VMEM((1, LANES), F32)],
        compiler_params=_cp(32, ("arbitrary",)),
    )(fl, bf_pad)


def _head_rows(blk, hh):
    r = lax.broadcasted_iota(jnp.int32, blk.shape, 0)
    return jnp.where((r >= hh * HEAD_DIM) & (r < (hh + 1) * HEAD_DIM), blk, jnp.zeros_like(blk))


def _two_heads(blk):
    return jnp.concatenate([_head_rows(blk, 0), _head_rows(blk, 1)], axis=1)


def _bias_rows():
    r = lax.broadcasted_iota(jnp.int32, (LANES, 2 * TILE), 0)
    c = lax.broadcasted_iota(jnp.int32, (LANES, 2 * TILE), 1)
    return jnp.where(((r < 3) & (c < TILE)) | ((r >= 3) & (r < 6) & (c >= TILE)), 1.0, 0.0).astype(BF16)


def _diag_mask(s):
    kpos = lax.broadcasted_iota(jnp.int32, (TILE, TILE), 0)
    qpos = lax.broadcasted_iota(jnp.int32, (TILE, TILE), 1)
    return jnp.where(kpos <= qpos, s, NEG)


def _stream(n, first, nxt, scores, update):
    if n == 0:
        return
    scores(first, 0)

    def pair_body(_, idx):
        idx_b = nxt(idx)
        scores(idx_b, 1)
        update(idx, 0)
        idx_c = nxt(idx_b)
        scores(idx_c, 0)
        update(idx_b, 1)
        return idx_c

    idx = lax.fori_loop(0, (n - 1) // 2, pair_body, first)
    if n % 2 == 1:
        update(idx, 0)
    else:
        idx_b = nxt(idx)
        scores(idx_b, 1)
        update(idx, 0)
        update(idx_b, 1)


def _next_below_diagonal(idx):
    i, j = idx
    wrap = j + 1 >= i
    return jnp.where(wrap, i + 1, i), jnp.where(wrap, 0, j + 1)


def _tile_rows(t):
    return pl.ds(pl.multiple_of(t * TILE, TILE), TILE)


def _two_streams(nt):
    load, group = [0, 0], {}
    for i in sorted(range(1, nt), reverse=True):
        g = 0 if load[0] <= load[1] else 1
        group[i] = g
        load[g] += i
    rows = [[(i, i, j) for i in range(1, nt) if group[i] == g for j in range(i)] for g in range(2)]
    length = max(len(r) for r in rows)
    rows = [r + [(nt, 0, 0)] * (length - len(r)) for r in rows]
    return group, np.asarray(rows, np.int32).reshape(2, -1), length


def _attn_fwd(qT3, k, kx3, vT3, nt):
    lp = nt * TILE
    npair = N_HEADS // 2
    group, table, n_stream = _two_streams(nt)

    def body(tab_ref, qT_ref, k_ref, kx_ref, vT_ref, oT_ref, o_ref, lse_ref, sbuf, m_0, l_0, acc_0, m_1, l_1, acc_1):
        ones = _bias_rows()
        states = ((m_0, l_0, acc_0), (m_1, l_1, acc_1))

        def scores(i, j, slot):
            qcat = jnp.concatenate([_two_heads(qT_ref[i]), ones], axis=0)
            kext = jnp.concatenate([k_ref[_tile_rows(j), :], kx_ref[0, _tile_rows(j), :]], axis=1)
            sbuf[slot] = jnp.dot(kext, qcat, preferred_element_type=F32)

        def update(st, j, slot, state, diag):
            m_s, l_s, acc_s = state
            for hh in range(2):
                s = sbuf[slot, :, hh * TILE:(hh + 1) * TILE]
                vj = vT_ref[j, hh * HEAD_DIM:(hh + 1) * HEAD_DIM, :]
                if diag:
                    s = _diag_mask(s)
                    m_new = jnp.max(s, axis=0, keepdims=True)
                    p = jnp.exp2(s - m_new)
                    l_s[st, hh] = jnp.sum(p, axis=0, keepdims=True)
                    acc_s[st, hh] = jnp.dot(vj, p.astype(BF16), preferred_element_type=F32)
                else:
                    m_prev = m_s[st, hh]
                    m_new = jnp.maximum(m_prev, jnp.max(s, axis=0, keepdims=True))
                    a = jnp.exp2(m_prev - m_new)
                    p = jnp.exp2(s - m_new)
                    l_s[st, hh] = a * l_s[st, hh] + jnp.sum(p, axis=0, keepdims=True)
                    acc_s[st, hh] = a * acc_s[st, hh] + jnp.dot(vj, p.astype(BF16), preferred_element_type=F32)
                m_s[st, hh] = m_new

        _stream(nt, jnp.int32(0), lambda t: t + 1, lambda t, slot: scores(t, t, slot),
                lambda t, slot: update(t, t, slot, states[0], True))
        for dst, src in zip(states[1], states[0]):
            dst[0:nt] = src[0:nt]
        for m_s, l_s, acc_s in states:
            m_s[nt] = jnp.full(m_s.shape[1:], NEG, F32)
            l_s[nt] = jnp.zeros(l_s.shape[1:], F32)
            acc_s[nt] = jnp.zeros(acc_s.shape[1:], F32)

        def entry(g, t):
            return tab_ref[g, 3 * t], tab_ref[g, 3 * t + 1], tab_ref[g, 3 * t + 2]

        def scores2(t, slot):
            for g in range(2):
                _, qi, kj = entry(g, t)
                scores(qi, kj, 2 * g + slot)

        def update2(t, slot):
            for g in range(2):
                st, _, kj = entry(g, t)
                update(st, kj, 2 * g + slot, states[g], False)

        _stream(n_stream, jnp.int32(0), lambda t: t + 1, scores2, update2)

        for i in range(nt):
            m_s, l_s, acc_s = states[group.get(i, 0)]
            for hh in range(2):
                l = l_s[i, hh]
                oT_ref[i, hh * HEAD_DIM:(hh + 1) * HEAD_DIM, :] = acc_s[i, hh] / l
                lse_ref[0, i, hh:hh + 1, :] = m_s[i, hh] + jnp.log(l) * LOG2E
            o_ref[i * TILE:(i + 1) * TILE, :] = oT_ref[i].T.astype(BF16)

    blk_t = pl.BlockSpec((nt, LANES, TILE), lambda p, tab: (0, p, 0))
    blk_rm = pl.BlockSpec((lp, LANES), lambda p, tab: (0, p))
    blk_px = pl.BlockSpec((1, lp, LANES), lambda p, tab: (p, 0, 0))
    blk_st = pl.BlockSpec((1, nt, 8, TILE), lambda p, tab: (p, 0, 0, 0))
    state = [pltpu.VMEM((nt + 1, 2, 1, TILE), F32), pltpu.VMEM((nt + 1, 2, 1, TILE), F32),
             pltpu.VMEM((nt + 1, 2, HEAD_DIM, TILE), F32)]
    grid_spec = pltpu.PrefetchScalarGridSpec(
        num_scalar_prefetch=1, grid=(npair,), in_specs=[blk_t, blk_rm, blk_px, blk_t],
        out_specs=[blk_t, blk_rm, blk_st], scratch_shapes=[pltpu.VMEM((4, TILE, 2 * TILE), F32)] + state + state)
    return pl.pallas_call(
        body, name="attn_fwd", grid_spec=grid_spec,
        out_shape=[jax.ShapeDtypeStruct((nt, D_ATTN, TILE), F32),
                   jax.ShapeDtypeStruct((lp, D_ATTN), BF16),
                   jax.ShapeDtypeStruct((npair, nt, 8, TILE), F32)],
        compiler_params=_cp(60, ("arbitrary",)),
    )(jnp.asarray(table), qT3, k, kx3, vT3)


def _attn_bwd(qT3, kT3, k, kx3, v, oT3, doT3, lse4, nt):
    lp = nt * TILE
    npair = N_HEADS // 2

    def body(qT_ref, kT_ref, k_ref, kx_ref, v_ref, oT_ref, doT_ref, lse_ref,
             dqT_ref, dkT_ref, dvT_ref, dck_ref, dcq_ref, sbuf, dpbuf, dq_s, dk_s, dv_s, dc_s):
        ones = _bias_rows()
        nt_dims = (((1,), (1,)), ((), ()))

        def scores(idx, slot):
            i, j = idx
            qcat = jnp.concatenate([_two_heads(qT_ref[i]), ones], axis=0)
            kext = jnp.concatenate([k_ref[_tile_rows(j), :], kx_ref[0, _tile_rows(j), :]], axis=1)
            sbuf[slot] = jnp.dot(kext, qcat, preferred_element_type=F32)
            dpbuf[slot] = jnp.dot(v_ref[_tile_rows(j), :], _two_heads(doT_ref[i]), preferred_element_type=F32)

        def update(idx, slot, diag):
            i, j = idx
            for hh in range(2):
                hs = slice(hh * HEAD_DIM, (hh + 1) * HEAD_DIM)
                s = sbuf[slot, :, hh * TILE:(hh + 1) * TILE]
                if diag:
                    s = _diag_mask(s)
                p = jnp.exp2(s - lse_ref[0, i, hh:hh + 1, :])
                doh = doT_ref[i, hs, :]
                delta = jnp.sum(doh.astype(F32) * oT_ref[i, hs, :], axis=0, keepdims=True)
                ds = p * (dpbuf[slot, :, hh * TILE:(hh + 1) * TILE] - delta)
                dsb = ds.astype(BF16)
                dv = lax.dot_general(doh, p.astype(BF16), nt_dims, preferred_element_type=F32)
                dk = lax.dot_general(qT_ref[i, hs, :], dsb, nt_dims, preferred_element_type=F32)
                dq = jnp.dot(kT_ref[j, hs, :], dsb, preferred_element_type=F32)
                dc = ds[:, :LANES] + ds[:, LANES:]
                dcq = jnp.sum(ds, axis=0, keepdims=True)
                if diag:
                    dv_s[j, hh] = dv
                    dk_s[j, hh] = dk
                    dc_s[j, hh] = dc
                    dq_s[i, hs, :] = dq
                    dcq_ref[0, i, hh:hh + 1, :] = dcq
                else:
                    dv_s[j, hh] += dv
                    dk_s[j, hh] += dk
                    dc_s[j, hh] += dc
                    dq_s[i, hs, :] += dq
                    dcq_ref[0, i, hh:hh + 1, :] += dcq

        dcq_ref[...] = jnp.zeros_like(dcq_ref)
        zero = jnp.int32(0)
        _stream(nt, (zero, zero), lambda idx: (idx[0] + 1, idx[1] + 1), scores,
                lambda idx, slot: update(idx, slot, True))
        _stream(nt * (nt - 1) // 2, (zero + 1, zero), _next_below_diagonal, scores,
                lambda idx, slot: update(idx, slot, False))

        lane = lax.broadcasted_iota(jnp.int32, (TILE, LANES), 1)

        def finish(t, carry):
            dck = jnp.zeros((TILE, LANES), F32)
            for hh in range(2):
                hs = slice(hh * HEAD_DIM, (hh + 1) * HEAD_DIM)
                dkT_ref[t, hs, :] = (dk_s[t, hh] * (1.0 / LOG2E)).astype(BF16)
                dvT_ref[t, hs, :] = dv_s[t, hh].astype(BF16)
                dck = jnp.where(lane == hh, -jnp.sum(dc_s[t, hh], axis=1, keepdims=True), dck)
            dck_ref[0, _tile_rows(t), :] = dck
            dqT_ref[t] = (dq_s[t] * SCALE).astype(BF16)
            return carry

        lax.fori_loop(0, nt, finish, 0)

    blk_t = pl.BlockSpec((nt, LANES, TILE), lambda p: (0, p, 0))
    blk_rm = pl.BlockSpec((lp, LANES), lambda p: (0, p))
    blk_px = pl.BlockSpec((1, lp, LANES), lambda p: (p, 0, 0))
    blk_st = pl.BlockSpec((1, nt, 8, TILE), lambda p: (p, 0, 0, 0))
    t3 = jax.ShapeDtypeStruct((nt, D_ATTN, TILE), BF16)
    return pl.pallas_call(
        body, name="attn_bwd", grid=(npair,),
        in_specs=[blk_t, blk_t, blk_rm, blk_px, blk_rm, blk_t, blk_t, blk_st],
        out_specs=[blk_t, blk_t, blk_t, blk_px, blk_st],
        out_shape=[t3, t3, t3, jax.ShapeDtypeStruct((npair, lp, LANES), F32),
                   jax.ShapeDtypeStruct((npair, nt, 8, TILE), F32)],
        scratch_shapes=[pltpu.VMEM((2, TILE, 2 * TILE), F32), pltpu.VMEM((2, TILE, 2 * TILE), F32),
                        pltpu.VMEM((nt, LANES, TILE), F32), pltpu.VMEM((nt, 2, HEAD_DIM, TILE), F32),
                        pltpu.VMEM((nt, 2, HEAD_DIM, TILE), F32), pltpu.VMEM((nt, 2, TILE, LANES), F32)],
        compiler_params=_cp(60, ("arbitrary",)),
    )(qT3, kT3, k, kx3, v, oT3, doT3, lse4)


def _glu(u, ug, i):
    return jnp.where(_row_mask(i, u.shape), u.astype(F32) * _sigmoid(ug.astype(F32)), 0.0)


def _shifted_copies(dst, src):
    for ph in range(8):
        dst[ph] = src[ph:ph + SHIFT_ROWS, :]


def _tap_window(sh, off, lanes, row0=0, rows=TILE):
    base = (off // 8) * 8 + row0
    return sh[off % 8, base:base + rows, lanes]


def _conv_fwd(u, ug, conv_w, conv_b, g, b, w_pw, nt):
    lp = nt * TILE

    def body(u_ref, ug_ref, up_ref, ugp_ref, w_ref, cb_ref, g_ref, b_ref, wpw_ref,
             co_ref, hc_ref, pw_ref, ext, sh):
        i = pl.program_id(0)
        prev = _glu(up_ref[...], ugp_ref[...], i - 1)
        ext[0:HALO, :] = jnp.where(i > 0, prev[TILE - HALO:, :], 0.0)
        ext[HALO:HALO + TILE, :] = _glu(u_ref[...], ug_ref[...], i)
        ext[HALO + TILE:, :] = jnp.zeros((8, D_CONV), F32)
        _shifted_copies(sh, ext)
        for lb in range(D_CONV // LANES):
            lanes = slice(lb * LANES, (lb + 1) * LANES)
            acc = jnp.zeros((TILE, LANES), F32) + cb_ref[:, lanes]
            for t in range(CONV_WIDTH):
                off = HALO - (CONV_WIDTH - 1) + t
                acc = acc + w_ref[t:t + 1, lanes] * _tap_window(sh, off, lanes)
            co_ref[:, lanes] = acc
        xhat, _ = _ln_stats(co_ref[...])
        a, _ = _silu_and_grad(xhat * g_ref[...] + b_ref[...])
        hc = a.astype(BF16)
        hc_ref[...] = hc
        pw_ref[...] = jnp.dot(hc, wpw_ref[...], preferred_element_type=F32).astype(BF16)

    rm = lambda dt: jax.ShapeDtypeStruct((lp, D_CONV), dt)
    return pl.pallas_call(
        body, name="conv_fwd", grid=(nt,),
        in_specs=[_row_spec(512), _row_spec(512), _row_spec(512, shift=True), _row_spec(512, shift=True),
                  _full_spec((32, 512)), _full_spec((1, 512)), _full_spec((1, 512)), _full_spec((1, 512)),
                  _full_spec((512, 512))],
        out_specs=[_row_spec(512), _row_spec(512), _row_spec(512)],
        out_shape=[rm(F32), rm(BF16), rm(BF16)],
        scratch_shapes=[pltpu.VMEM((EXT_ROWS, D_CONV), F32), pltpu.VMEM((8, SHIFT_ROWS, D_CONV), F32)],
        compiler_params=_cp(40, ("arbitrary",)),
    )(u, ug, u, ug, conv_w, conv_b, g, b, w_pw)


def _out_fwd(o, ga, pw, gc, x, metapad, g_in, b_in, w_out, g_out, b_out, target, nt):
    lp = nt * TILE

    def body(o_ref, ga_ref, pw_ref, gc_ref, x_ref, mp_ref, gi_ref, bi_ref, wo_ref, go_ref, bo_ref, t_ref,
             y_ref, dz_ref, loss_ref, dgo_ref, dbo_ref):
        i = pl.program_id(0)

        @pl.when(i == 0)
        def _():
            loss_ref[...] = jnp.zeros_like(loss_ref)
            dgo_ref[...] = jnp.zeros_like(dgo_ref)
            dbo_ref[...] = jnp.zeros_like(dbo_ref)

        x0 = jnp.where(i == 0, mp_ref[...], x_ref[...])
        xhat, _ = _ln_stats(x0)
        h = xhat * gi_ref[...] + bi_ref[...]
        ya, _ = _silu_and_grad(ga_ref[...].astype(F32))
        yc, _ = _silu_and_grad(gc_ref[...].astype(F32))
        ya = (o_ref[...].astype(F32) * ya).astype(BF16)
        yc = (pw_ref[...].astype(F32) * yc).astype(BF16)
        y_ref[:, :D_ATTN] = ya
        y_ref[:, D_ATTN:] = yc
        z = ALPHA * h + jnp.dot(ya, wo_ref[:D_ATTN, :], preferred_element_type=F32) \
            + jnp.dot(yc, wo_ref[D_ATTN:, :], preferred_element_type=F32)
        zhat, rstd = _ln_stats(z)
        out = zhat * go_ref[...] + bo_ref[...]
        live = (i > 0).astype(F32)
        err = (out - t_ref[...]) * live
        dout = err * (1.0 / D_MODEL)
        loss_ref[...] += 0.5 * jnp.sum(jnp.sum(err * dout, axis=0, keepdims=True), axis=1, keepdims=True)
        dgo_ref[...] += jnp.sum(dout * zhat, axis=0, keepdims=True)
        dbo_ref[...] += jnp.sum(dout, axis=0, keepdims=True)
        dz_ref[...] = _ln_bwd(dout, zhat, rstd, go_ref[...])

    return pl.pallas_call(
        body, name="out_fwd", grid=(nt,),
        in_specs=[_row_spec(512), _row_spec(512), _row_spec(512), _row_spec(512),
                  _row_spec(D_MODEL, shift=True), _full_spec((TILE, D_MODEL)), _full_spec((1, D_MODEL)),
                  _full_spec((1, D_MODEL)), _full_spec((D_MODEL, D_MODEL)), _full_spec((1, D_MODEL)),
                  _full_spec((1, D_MODEL)), _row_spec(D_MODEL, shift=True)],
        out_specs=[_row_spec(D_MODEL), _row_spec(D_MODEL), _full_spec((1, LANES)), _full_spec((1, D_MODEL)),
                   _full_spec((1, D_MODEL))],
        out_shape=[jax.ShapeDtypeStruct((lp, D_MODEL), BF16), jax.ShapeDtypeStruct((lp, D_MODEL), F32),
                   jax.ShapeDtypeStruct((1, LANES), F32), jax.ShapeDtypeStruct((1, D_MODEL), F32),
                   jax.ShapeDtypeStruct((1, D_MODEL), F32)],
        compiler_params=_cp(40, ("arbitrary",)),
    )(o, ga, pw, gc, x, metapad, g_in, b_in, w_out, g_out, b_out, target)


def _out_bwd(dz, y, o, ga, pw, gc, w_out, hc, co, w_pw, g_cv, b_cv, nt):
    lp = nt * TILE

    def body(dz_ref, y_ref, o_ref, ga_ref, pw_ref, gc_ref, wo_ref, hc_ref, co_ref, wpw_ref, g_ref, b_ref,
             doT_ref, dga_ref, dgc_ref, dwo_ref, dco_ref, dwpw_ref, dg_ref, db_ref, dcb_ref):
        i = pl.program_id(0)

        @pl.when(i == 0)
        def _():
            dwo_ref[...] = jnp.zeros_like(dwo_ref)
            dwpw_ref[...] = jnp.zeros_like(dwpw_ref)
            dg_ref[...] = jnp.zeros_like(dg_ref)
            db_ref[...] = jnp.zeros_like(db_ref)
            dcb_ref[...] = jnp.zeros_like(dcb_ref)

        dzb = dz_ref[...].astype(BF16)
        nt_dims = (((1,), (1,)), ((), ()))
        tn_dims = (((0,), (0,)), ((), ()))
        dya = lax.dot_general(dzb, wo_ref[:D_ATTN, :], nt_dims, preferred_element_type=F32)
        dyc = lax.dot_general(dzb, wo_ref[D_ATTN:, :], nt_dims, preferred_element_type=F32)
        sa, sga = _silu_and_grad(ga_ref[...].astype(F32))
        sc, sgc = _silu_and_grad(gc_ref[...].astype(F32))
        doT_ref[0] = (dya * sa).T.astype(BF16)
        dga_ref[...] = (dya * o_ref[...].astype(F32) * sga).astype(BF16)
        dpw_b = (dyc * sc).astype(BF16)
        dgc_ref[...] = (dyc * pw_ref[...].astype(F32) * sgc).astype(BF16)
        dwo_ref[...] += lax.dot_general(y_ref[...], dzb, tn_dims, preferred_element_type=F32)

        dhc = lax.dot_general(dpw_b, wpw_ref[...], nt_dims, preferred_element_type=F32)
        xhat, rstd = _ln_stats(co_ref[...])
        _, sg = _silu_and_grad(xhat * g_ref[...] + b_ref[...])
        dln = dhc * sg
        dg_ref[...] += jnp.sum(dln * xhat, axis=0, keepdims=True)
        db_ref[...] += jnp.sum(dln, axis=0, keepdims=True)
        dco = _ln_bwd(dln, xhat, rstd, g_ref[...])
        dco_ref[...] = dco
        dcb_ref[...] += jnp.sum(dco, axis=0, keepdims=True)
        dwpw_ref[...] += lax.dot_general(hc_ref[...], dpw_b, tn_dims, preferred_element_type=F32)

    rm = jax.ShapeDtypeStruct((lp, 512), BF16)
    vec = jax.ShapeDtypeStruct((1, D_CONV), F32)
    return pl.pallas_call(
        body, name="out_bwd", grid=(nt,),
        in_specs=[_row_spec(D_MODEL), _row_spec(D_MODEL), _row_spec(512), _row_spec(512), _row_spec(512),
                  _row_spec(512), _full_spec((D_MODEL, D_MODEL)), _row_spec(512), _row_spec(512),
                  _full_spec((512, 512)), _full_spec((1, 512)), _full_spec((1, 512))],
        out_specs=[_t3_spec(512), _row_spec(512), _row_spec(512), _full_spec((D_MODEL, D_MODEL)), _row_spec(512),
                   _full_spec((512, 512)), _full_spec((1, 512)), _full_spec((1, 512)), _full_spec((1, 512))],
        out_shape=[jax.ShapeDtypeStruct((nt, 512, TILE), BF16), rm, rm, jax.ShapeDtypeStruct((D_MODEL, D_MODEL), F32),
                   jax.ShapeDtypeStruct((lp, D_CONV), F32), jax.ShapeDtypeStruct((512, 512), F32), vec, vec, vec],
        compiler_params=_cp(56, ("arbitrary",)),
    )(dz, y, o, ga, pw, gc, w_out, hc, co, w_pw, g_cv, b_cv)


def _conv_bwd_taps(dco, u, ug, conv_w, nt):
    lp = nt * TILE

    def body(dco_ref, dcon_ref, u_ref, ug_ref, up_ref, ugp_ref, w3_ref, du_ref, dug_ref, dw_ref, ext, dext, sh, dsh,
             dhg_s, dw_s):
        i = pl.program_id(0)

        @pl.when(i == 0)
        def _():
            dw_s[...] = jnp.zeros_like(dw_s)

        prev = _glu(up_ref[...], ugp_ref[...], i - 1)
        ext[0:HALO, :] = jnp.where(i > 0, prev[TILE - HALO:, :], 0.0)
        ext[HALO:HALO + TILE, :] = _glu(u_ref[...], ug_ref[...], i)
        ext[HALO + TILE:, :] = jnp.zeros((8, D_CONV), F32)
        dext[0:TILE, :] = dco_ref[...]
        dext[TILE:TILE + HALO, :] = jnp.where(i < nt - 1, dcon_ref[0:HALO, :], 0.0)
        dext[TILE + HALO:, :] = jnp.zeros((8, D_CONV), F32)
        _shifted_copies(sh, ext)
        _shifted_copies(dsh, dext)
        stripe = 32

        def stripe_body(rb, carry):
            row0 = pl.multiple_of(rb * stripe, stripe)
            dco = dco_ref[pl.ds(row0, stripe), :]
            dhg = jnp.zeros((stripe, D_CONV), F32)
            for t in range(CONV_WIDTH):
                off = HALO - (CONV_WIDTH - 1) + t
                back = CONV_WIDTH - 1 - t
                prod = dco * sh[off % 8, pl.ds((off // 8) * 8 + row0, stripe), :]
                part = prod[0:8, :]
                for r8 in range(1, stripe // 8):
                    part = part + prod[8 * r8:8 * r8 + 8, :]
                dw_s[t] += part
                dhg = dhg + w3_ref[t] * dsh[back % 8, pl.ds((back // 8) * 8 + row0, stripe), :]
            dhg_s[pl.ds(row0, stripe), :] = dhg
            return carry

        lax.fori_loop(0, TILE // stripe, stripe_body, 0)

        @pl.when(i == nt - 1)
        def _():
            dw_ref[...] = jnp.sum(dw_s[...], axis=1)

        dhg = jnp.where(_row_mask(i, (TILE, D_CONV)), dhg_s[...], 0.0)
        sg = _sigmoid(ug_ref[...].astype(F32))
        du_ref[...] = (dhg * sg).astype(BF16)
        dug_ref[...] = (dhg * u_ref[...].astype(F32) * sg * (1.0 - sg)).astype(BF16)

    rm = jax.ShapeDtypeStruct((lp, D_CONV), BF16)
    nxt = pl.BlockSpec((TILE, 512), lambda i: (jnp.minimum(i + 1, nt - 1), 0))
    ext_t = pltpu.VMEM((EXT_ROWS, D_CONV), F32)
    sh_t = pltpu.VMEM((8, SHIFT_ROWS, D_CONV), F32)
    return pl.pallas_call(
        body, name="conv_bwd_taps", grid=(nt,),
        in_specs=[_row_spec(512), nxt, _row_spec(512), _row_spec(512), _row_spec(512, shift=True),
                  _row_spec(512, shift=True), _full_spec((32, 1, 512))],
        out_specs=[_row_spec(512), _row_spec(512), _full_spec((32, 512))],
        out_shape=[rm, rm, jax.ShapeDtypeStruct((32, D_CONV), F32)],
        scratch_shapes=[ext_t, ext_t, sh_t, sh_t, pltpu.VMEM((TILE, D_CONV), F32), pltpu.VMEM((32, 8, D_CONV), F32)],
        compiler_params=_cp(48, ("arbitrary",)),
    )(dco, dco, u, ug, u, ug, conv_w.reshape(32, 1, D_CONV))


def _cumsum_bwd(dck, dcq4, fl, bf_pad, nt):
    lp = nt * TILE

    def body(dck_ref, dcq_ref, fl_ref, bf_ref, dfl_ref, dbf_ref, carry):
        i = pl.program_id(0)
        tile = nt - 1 - i

        @pl.when(i == 0)
        def _():
            carry[...] = jnp.zeros_like(carry)
            dbf_ref[...] = jnp.zeros_like(dbf_ref)

        dc = jnp.zeros((TILE, LANES), F32)
        for p in range(N_HEADS // 2):
            dq_rows = jnp.concatenate([dcq_ref[p, 0], jnp.zeros((LANES - 8, TILE), F32)], axis=0)
            both = dck_ref[p] + dq_rows.T
            dc = dc + (both if p == 0 else pltpu.roll(both, 2 * p, 1))
        r = lax.broadcasted_iota(jnp.int32, (TILE, TILE), 0)
        c = lax.broadcasted_iota(jnp.int32, (TILE, TILE), 1)
        triu = (c >= r).astype(F32)
        dlf = jnp.dot(triu, dc, precision=lax.Precision.HIGHEST, preferred_element_type=F32) + carry[...]
        carry[...] = dlf[0:1, :]
        z = fl_ref[...] + bf_ref[...]
        lane = lax.broadcasted_iota(jnp.int32, (TILE, LANES), 1)
        dfl = jnp.where(_row_mask(tile, (TILE, LANES)) & (lane < N_HEADS), dlf * _sigmoid(-z), 0.0)
        dfl_ref[...] = dfl.astype(BF16)
        dbf_ref[...] += jnp.sum(dfl, axis=0, keepdims=True)

    rev = lambda i: (nt - 1 - i, 0)
    return pl.pallas_call(
        body, name="cumsum_bwd", grid=(nt,),
        in_specs=[pl.BlockSpec((N_HEADS // 2, TILE, LANES), lambda i: (0, nt - 1 - i, 0)),
                  pl.BlockSpec((N_HEADS // 2, 1, 8, TILE), lambda i: (0, nt - 1 - i, 0, 0)),
                  pl.BlockSpec((TILE, LANES), rev), _full_spec((1, LANES))],
        out_specs=[pl.BlockSpec((TILE, LANES), rev), _full_spec((1, LANES))],
        out_shape=[jax.ShapeDtypeStruct((lp, LANES), BF16), jax.ShapeDtypeStruct((1, LANES), F32)],
        scratch_shapes=[pltpu.VMEM((1, LANES), F32)],
        compiler_params=_cp(32, ("arbitrary",)),
    )(dck, dcq4, fl, bf_pad)


def _dw_rowmajor(hb, secs, nt):
    n = len(secs)

    def body(*refs):
        hb_ref, sec_refs, out_refs = refs[0], refs[1:1 + n], refs[1 + n:]
        i = pl.program_id(0)

        @pl.when(i == 0)
        def _():
            for o_ref in out_refs:
                o_ref[...] = jnp.zeros_like(o_ref)

        hb_t = hb_ref[...]
        for s_ref, o_ref in zip(sec_refs, out_refs):
            o_ref[...] += lax.dot_general(hb_t, s_ref[...], (((0,), (0,)), ((), ())), preferred_element_type=F32)

    return pl.pallas_call(
        body, name="dw_rowmajor", grid=(nt,),
        in_specs=[_row_spec(D_MODEL)] + [_row_spec(s.shape[1]) for s in secs],
        out_specs=[_full_spec((D_MODEL, s.shape[1])) for s in secs],
        out_shape=[jax.ShapeDtypeStruct((D_MODEL, s.shape[1]), F32) for s in secs],
        compiler_params=_cp(48, ("arbitrary",)),
    )(hb, *secs)


def _dw_transposed(hb, secs_t3, nt):
    n = len(secs_t3)

    def body(*refs):
        hb_ref, sec_refs, out_refs = refs[0], refs[1:1 + n], refs[1 + n:]
        i = pl.program_id(0)

        @pl.when(i == 0)
        def _():
            for o_ref in out_refs:
                o_ref[...] = jnp.zeros_like(o_ref)

        hb_t = hb_ref[...]
        for s_ref, o_ref in zip(sec_refs, out_refs):
            o_ref[...] += jnp.dot(s_ref[0], hb_t, preferred_element_type=F32)

    return pl.pallas_call(
        body, name="dw_transposed", grid=(nt,),
        in_specs=[_row_spec(D_MODEL)] + [_t3_spec(512) for _ in secs_t3],
        out_specs=[_full_spec((512, D_MODEL)) for _ in secs_t3],
        out_shape=[jax.ShapeDtypeStruct((512, D_MODEL), F32) for _ in secs_t3],
        compiler_params=_cp(40, ("arbitrary",)),
    )(hb, *secs_t3)


def _dh_bwd(secs, secs_t3, w_rm, w_t, dz, x, metapad, g_in, nt):
    n, m = len(secs), len(secs_t3)
    offs = OFF_GA_R + np.cumsum([0] + [s.shape[1] for s in secs])

    def body(*refs):
        sec_refs, t3_refs = refs[:n], refs[n:n + m]
        wrm_ref, wt_ref, dz_ref, x_ref, mp_ref, g_ref = refs[n + m:n + m + 6]
        dx_ref, dmeta_ref, dg_ref, db_ref = refs[n + m + 6:]
        i = pl.program_id(0)

        @pl.when(i == 0)
        def _():
            dg_ref[...] = jnp.zeros_like(dg_ref)
            db_ref[...] = jnp.zeros_like(db_ref)

        dh = ALPHA * dz_ref[...]
        for s_ref, lo, hi in zip(sec_refs, offs[:-1], offs[1:]):
            dh = dh + lax.dot_general(s_ref[...], wrm_ref[:, lo:hi], (((1,), (1,)), ((), ())),
                                      preferred_element_type=F32)
        for idx, t_ref in enumerate(t3_refs):
            dh = dh + lax.dot_general(t_ref[0], wt_ref[idx * 512:(idx + 1) * 512, :], (((0,), (0,)), ((), ())),
                                      preferred_element_type=F32)
        x0 = jnp.where(i == 0, mp_ref[...], x_ref[...])
        xhat, rstd = _ln_stats(x0)
        dg_ref[...] += jnp.sum(dh * xhat, axis=0, keepdims=True)
        db_ref[...] += jnp.sum(dh, axis=0, keepdims=True)
        dx = _ln_bwd(dh, xhat, rstd, g_ref[...])
        dx_ref[...] = dx

        @pl.when(i == 0)
        def _():
            dmeta_ref[...] = dx

    seq = (nt - 1) * TILE
    return pl.pallas_call(
        body, name="dh_bwd", grid=(nt,),
        in_specs=[_row_spec(s.shape[1]) for s in secs] + [_t3_spec(512) for _ in secs_t3]
        + [_full_spec(w_rm.shape), _full_spec(w_t.shape), _row_spec(D_MODEL), _row_spec(D_MODEL, shift=True),
           _full_spec((TILE, D_MODEL)), _full_spec((1, D_MODEL))],
        out_specs=[_row_spec(D_MODEL, shift=True), _full_spec((TILE, D_MODEL)), _full_spec((1, D_MODEL)),
                   _full_spec((1, D_MODEL))],
        out_shape=[jax.ShapeDtypeStruct((seq, D_MODEL), F32), jax.ShapeDtypeStruct((TILE, D_MODEL), F32),
                   jax.ShapeDtypeStruct((1, D_MODEL), F32), jax.ShapeDtypeStruct((1, D_MODEL), F32)],
        compiler_params=_cp(56, ("arbitrary",)),
    )(*secs, *secs_t3, w_rm, w_t, dz, x, metapad, g_in)


RB = 256
SMALL_ROWS = 48


def _repack_weights(all_in, all_small):
    n_cw = D_CONV // N_DEV

    def body(a_ref, s_ref, wr_ref, wt_ref, mp_ref, cw_ref):
        full = jnp.concatenate([a_ref[d].T[:, :SHARD_IN] for d in range(N_DEV)], axis=1)
        qkv = full[:, :1536]
        wr_ref[:, :1536] = qkv
        wr_ref[:, 1536:OFF_F_R] = full[:, 1544:]
        wr_ref[:, OFF_F_R:] = jnp.concatenate([full[:, 1536:1544], jnp.zeros((RB, LANES - N_HEADS), BF16)], axis=1)
        wt_ref[...] = qkv.T

        @pl.when(pl.program_id(0) == 0)
        def _():
            mp_ref[0:PAD, :] = jnp.zeros((PAD, D_MODEL), F32)
            mp_ref[PAD:, :] = jnp.concatenate([s_ref[d, 0:N_META, :] for d in range(N_DEV)], axis=1)
            cw_ref[...] = jnp.concatenate([s_ref[d, N_META:, 0:n_cw] for d in range(N_DEV)], axis=1)

    return pl.pallas_call(
        body, name="repack_weights", grid=(D_MODEL // RB,),
        in_specs=[pl.BlockSpec((N_DEV, 512, RB), lambda i: (0, 0, i)), _full_spec((N_DEV, SMALL_ROWS, LANES))],
        out_specs=[pl.BlockSpec((RB, W_COLS), lambda i: (i, 0)), pl.BlockSpec((1536, RB), lambda i: (0, i)),
                   _full_spec((TILE, D_MODEL)), _full_spec((32, D_CONV))],
        out_shape=[jax.ShapeDtypeStruct((D_MODEL, W_COLS), BF16), jax.ShapeDtypeStruct((1536, D_MODEL), BF16),
                   jax.ShapeDtypeStruct((TILE, D_MODEL), F32), jax.ShapeDtypeStruct((32, D_CONV), F32)],
        compiler_params=_cp(40, ("arbitrary",)),
    )(all_in, all_small)


def _unpack_dw_in(dw_rm, dw_t):
    def body(dga_ref, du_ref, dug_ref, dgc_ref, dfl_ref, dq_ref, dk_ref, dv_ref, out_ref):
        full = jnp.concatenate([dq_ref[...].T, dk_ref[...].T, dv_ref[...].T, dfl_ref[:, 0:N_HEADS], dga_ref[...],
                                du_ref[...], dug_ref[...], dgc_ref[...]], axis=1)
        pad = jnp.zeros((RB, 512 - SHARD_IN), F32)
        for d in range(N_DEV):
            out_ref[d] = jnp.concatenate([full[:, SHARD_IN * d:SHARD_IN * (d + 1)], pad], axis=1).T

    rm = pl.BlockSpec((RB, 512), lambda i: (i, 0))
    tr = pl.BlockSpec((512, RB), lambda i: (0, i))
    return pl.pallas_call(
        body, name="unpack_dw_in", grid=(D_MODEL // RB,),
        in_specs=[rm, rm, rm, rm, pl.BlockSpec((RB, LANES), lambda i: (i, 0)), tr, tr, tr],
        out_specs=pl.BlockSpec((N_DEV, 512, RB), lambda i: (0, 0, i)),
        out_shape=jax.ShapeDtypeStruct((N_DEV, 512, D_MODEL), F32),
        compiler_params=_cp(48, ("arbitrary",)),
    )(*dw_rm, *dw_t)


def _local_step(x, target, metapad, cw, w_r, w_t, w_pw_full, w_out_full, ln_in_g, ln_in_b, b_f, conv_b, ln_conv_g,
                ln_conv_b, ln_out_g, ln_out_b):
    seq = x.shape[0]
    nt = seq // TILE + 1
    row = lambda a: a.reshape(1, -1).astype(F32)
    bf_pad = jnp.pad(row(b_f), ((0, 0), (0, LANES - N_HEADS)))
    g_in, b_in = row(ln_in_g), row(ln_in_b)
    g_cv, b_cv, c_b = row(ln_conv_g), row(ln_conv_b), row(conv_b)
    g_out, b_out = row(ln_out_g), row(ln_out_b)

    hb, qT3, kT3, vT3, k, v, ga, u, ug, gc, fl = _proj_fwd(x, metapad, g_in, b_in, w_r, nt)
    kx3 = _cumsum_fwd(fl, bf_pad, nt)
    oT3, o, lse4 = _attn_fwd(qT3, k, kx3, vT3, nt)
    co, hc, pw = _conv_fwd(u, ug, cw, c_b, g_cv, b_cv, w_pw_full, nt)
    y, dz, loss, dg_out, db_out = _out_fwd(o, ga, pw, gc, x, metapad, g_in, b_in, w_out_full, g_out, b_out,
                                            target, nt)
    doT3, dga, dgc, dw_out, dco, dw_pw, dg_cv, db_cv, dc_b = _out_bwd(dz, y, o, ga, pw, gc, w_out_full, hc, co,
                                                                      w_pw_full, g_cv, b_cv, nt)
    du, dug, dcw = _conv_bwd_taps(dco, u, ug, cw, nt)
    dqT3, dkT3, dvT3, dck, dcq4 = _attn_bwd(qT3, kT3, k, kx3, v, oT3, doT3, lse4, nt)
    dfl, dbf = _cumsum_bwd(dck, dcq4, fl, bf_pad, nt)
    secs = (dga, du, dug, dgc, dfl)
    secs_t3 = (dqT3, dkT3, dvT3)
    dw_rm = _dw_rowmajor(hb, secs, nt)
    dw_t = _dw_transposed(hb, secs_t3, nt)
    grad_x, dmetapad, dg_in, db_in = _dh_bwd(secs, secs_t3, w_r, w_t, dz, x, metapad, g_in, nt)
    pieces = dict(loss=loss, metapad=dmetapad, ln_in_g=dg_in, ln_in_b=db_in, w_in_rm=dw_rm, w_in_t=dw_t, b_f=dbf,
                  conv_w=dcw, conv_b=dc_b, ln_conv_g=dg_cv, ln_conv_b=db_cv, w_pw=dw_pw, w_out=dw_out,
                  ln_out_g=dg_out, ln_out_b=db_out)
    return grad_x, pieces


MESH = pl.DeviceIdType.MESH
ANY = pl.BlockSpec(memory_space=pl.ANY)


def _mesh_pos():
    return lax.axis_index("x"), lax.axis_index("y"), lax.axis_index("c")


GATHER_SEMS = 8


def _gather_body(x_refs, out_refs, send_sems, recv_sems, local_sems):
    n = len(x_refs)
    x, y, c = _mesh_pos()
    me, sibling = (x, y, c), (x, y, 1 - c)
    xn, yn, dg = (1 - x, y), (x, 1 - y), (1 - x, 1 - y)

    def slot(a, px, py, pc, half=None):
        blk = out_refs[a].at[4 * px + 2 * py + pc]
        if half is None:
            return blk
        rows = blk.shape[0] // 2
        return blk.at[pl.ds(half * rows, rows)]

    def copy(a, k, block, to, src=None, half=None):
        return pltpu.make_async_remote_copy(
            src_ref=slot(a, *block, half) if src is None else src, dst_ref=slot(a, *block, half),
            send_sem=send_sems.at[GATHER_SEMS * a + k], recv_sem=recv_sems.at[GATHER_SEMS * a + k], device_id=to,
            device_id_type=MESH)

    arrays = range(n)
    mine = [pltpu.make_async_copy(x_refs[a], slot(a, *me), local_sems.at[a]) for a in arrays]
    for cp in mine:
        cp.start()
    sent = []
    for a in arrays:
        sent += [copy(a, 0, me, sibling, src=x_refs[a]), copy(a, 1, me, (*xn, c), src=x_refs[a]),
                 copy(a, 2, me, (*yn, c), src=x_refs[a])]
    for cp in sent:
        cp.start()

    def also(cp):
        cp.start()
        sent.append(cp)

    for a in arrays:
        copy(a, 2, (*yn, c), me).wait_recv()
        also(copy(a, 3, (*yn, c), (*xn, c), half=0))
        also(copy(a, 6, (*yn, c), sibling))
        copy(a, 1, (*xn, c), me).wait_recv()
        also(copy(a, 4, (*xn, c), (*yn, c), half=1))
        also(copy(a, 5, (*xn, c), sibling))
    for a in arrays:
        copy(a, 3, (*dg, c), me, half=0).wait_recv()
        copy(a, 4, (*dg, c), me, half=1).wait_recv()
        also(copy(a, 7, (*dg, c), sibling))
    for a in arrays:
        copy(a, 0, sibling, me).wait_recv()
        copy(a, 5, (*xn, 1 - c), me).wait_recv()
        copy(a, 6, (*yn, 1 - c), me).wait_recv()
        copy(a, 7, (*dg, 1 - c), me).wait_recv()
    for cp in sent:
        cp.wait_send()
    for cp in mine:
        cp.wait()


def _all_gather(blks, name):
    n = len(blks)

    def body(*refs):
        _gather_body(refs[:n], refs[n:2 * n], *refs[2 * n:])

    return pl.pallas_call(
        body, name=name, out_shape=[jax.ShapeDtypeStruct((N_DEV, *b.shape), b.dtype) for b in blks],
        in_specs=[ANY] * n, out_specs=[ANY] * n,
        scratch_shapes=[pltpu.SemaphoreType.DMA((GATHER_SEMS * n,)), pltpu.SemaphoreType.DMA((GATHER_SEMS * n,)),
                        pltpu.SemaphoreType.DMA((n,))],
    )(*blks)


def _exchange_sibling(g8s, small):
    n = len(g8s)

    def body(*refs):
        g_refs, s_ref, out_refs, a_ref = refs[:n], refs[n], refs[n + 1:2 * n + 1], refs[2 * n + 1]
        send_sems, recv_sems, a_send, a_recv, a_local = refs[2 * n + 2:]
        x, y, c = _mesh_pos()
        cps = [pltpu.make_async_remote_copy(
            src_ref=g_refs[a].at[2 * q + (1 - c)], dst_ref=out_refs[a].at[q], send_sem=send_sems.at[4 * a + q],
            recv_sem=recv_sems.at[4 * a + q], device_id=(x, y, 1 - c), device_id_type=MESH)
            for a in range(n) for q in range(4)]
        for cp in cps:
            cp.start()
        _gather_body([s_ref], [a_ref], a_send, a_recv, a_local)
        for cp in cps:
            cp.wait()

    outs = pl.pallas_call(
        body, name="rs_sibling",
        out_shape=[jax.ShapeDtypeStruct((4, *g.shape[1:]), g.dtype) for g in g8s]
        + [jax.ShapeDtypeStruct((N_DEV, *small.shape), small.dtype)],
        in_specs=[ANY] * (n + 1), out_specs=[ANY] * (n + 1),
        scratch_shapes=[pltpu.SemaphoreType.DMA((4 * n,)), pltpu.SemaphoreType.DMA((4 * n,)),
                        pltpu.SemaphoreType.DMA((GATHER_SEMS,)), pltpu.SemaphoreType.DMA((GATHER_SEMS,)),
                        pltpu.SemaphoreType.DMA((1,))],
    )(*g8s, small)
    return outs[:n], outs[n]


def _exchange_chips(p4s):
    n = len(p4s)

    def body(*refs):
        p_refs, out_refs, send_sems, recv_sems = refs[:n], refs[n:2 * n], refs[2 * n], refs[2 * n + 1]
        x, y, c = _mesh_pos()
        chips = [(1 - x, y), (x, 1 - y), (1 - x, 1 - y)]
        cps = [pltpu.make_async_remote_copy(
            src_ref=p_refs[a].at[2 * cx + cy], dst_ref=out_refs[a].at[k], send_sem=send_sems.at[3 * a + k],
            recv_sem=recv_sems.at[3 * a + k], device_id=(cx, cy, c), device_id_type=MESH)
            for k, (cx, cy) in enumerate(chips) for a in range(n)]
        for cp in cps:
            cp.start()
        for cp in cps:
            cp.wait()

    return pl.pallas_call(
        body, name="rs_chips", out_shape=[jax.ShapeDtypeStruct((3, *p.shape[1:]), p.dtype) for p in p4s],
        in_specs=[ANY] * n, out_specs=[ANY] * n,
        scratch_shapes=[pltpu.SemaphoreType.DMA((3 * n,)), pltpu.SemaphoreType.DMA((3 * n,))],
    )(*p4s)


def _rs_add_sibling(g8s, recvs, c_idx):
    n = len(g8s)

    def body(s_ref, *refs):
        g_refs, r_refs, p32_refs, pb_refs = (refs[k * n:(k + 1) * n] for k in range(4))
        for g_ref, r_ref, p32_ref, pb_ref in zip(g_refs, r_refs, p32_refs, pb_refs):
            p = g_ref[0] + r_ref[0]
            p32_ref[0] = p
            pb_ref[0] = p.astype(BF16)

    blk = lambda g: pl.BlockSpec((1, *g.shape[1:]), lambda q, s: (q, 0, 0))
    grid_spec = pltpu.PrefetchScalarGridSpec(
        num_scalar_prefetch=1, grid=(4,),
        in_specs=[pl.BlockSpec((1, *g.shape[1:]), lambda q, s: (2 * q + s[0], 0, 0)) for g in g8s]
        + [blk(g) for g in g8s],
        out_specs=[blk(g) for g in g8s] * 2)
    outs = pl.pallas_call(
        body, name="rs_add_sibling", grid_spec=grid_spec,
        out_shape=[jax.ShapeDtypeStruct((4, *g.shape[1:]), F32) for g in g8s]
        + [jax.ShapeDtypeStruct((4, *g.shape[1:]), BF16) for g in g8s],
        compiler_params=_cp(48, ("arbitrary",)),
    )(c_idx, *g8s, *recvs)
    return outs[:n], outs[n:]


def _rs_add_chips(p32s, recvs, q_idx):
    def body(s_ref, pin_ref, pout_ref, ppw_ref, rin_ref, rout_ref, rpw_ref, gin_ref, gout_ref, gpw_ref):
        def total(p_ref, r_ref):
            return ((p_ref[0] + r_ref[0].astype(F32)) + r_ref[1].astype(F32)) + r_ref[2].astype(F32)

        gin_ref[...] = total(pin_ref, rin_ref)[:SHARD_IN, :]
        gout_ref[0] = total(pout_ref, rout_ref)
        gpw_ref[0] = total(ppw_ref, rpw_ref)

    own = lambda p: pl.BlockSpec((1, *p.shape[1:]), lambda i, s: (s[0], 0, 0))
    whole = lambda shape: pl.BlockSpec(shape, lambda i, s: (0,) * len(shape))
    out_shapes = [(SHARD_IN, D_MODEL), (1, *p32s[1].shape[1:]), (1, *p32s[2].shape[1:])]
    grid_spec = pltpu.PrefetchScalarGridSpec(
        num_scalar_prefetch=1, grid=(1,),
        in_specs=[own(p) for p in p32s] + [whole(r.shape) for r in recvs],
        out_specs=[whole(s) for s in out_shapes])
    return pl.pallas_call(
        body, name="rs_add_chips", grid_spec=grid_spec,
        out_shape=[jax.ShapeDtypeStruct(s, F32) for s in out_shapes],
        compiler_params=_cp(48, ("arbitrary",)),
    )(q_idx, *p32s, *recvs)


SMALL_ROWS_G = 64
SMALL_LAYOUT = {
    "metapad": (0, N_META, D_MODEL), "conv_w": (16, 32, D_CONV), "ln_in_g": (48, 1, D_MODEL),
    "ln_in_b": (49, 1, D_MODEL), "b_f": (50, 1, LANES), "conv_b": (51, 1, D_CONV), "ln_conv_g": (52, 1, D_CONV),
    "ln_conv_b": (53, 1, D_CONV), "ln_out_g": (54, 1, D_MODEL), "ln_out_b": (55, 1, D_MODEL), "loss": (56, 1, LANES)}


def _pack_small(pieces):
    names = list(SMALL_LAYOUT)

    def body(*refs):
        out_ref = refs[-1]
        out_ref[...] = jnp.zeros_like(out_ref)
        for name, ref in zip(names, refs[:-1]):
            r0, nr, nl = SMALL_LAYOUT[name]
            src = ref[PAD:, :] if name == "metapad" else ref[...]
            out_ref[r0:r0 + nr, 0:nl] = src

    return pl.pallas_call(body, name="pack_small", out_shape=jax.ShapeDtypeStruct((SMALL_ROWS_G, D_MODEL), F32),
                          compiler_params=_cp(16))(*[pieces[n] for n in names])


def _sum_small(gathered):
    names = list(SMALL_LAYOUT)

    def body(a_ref, *out_refs):
        acc = a_ref[0]
        for d in range(1, N_DEV):
            acc = acc + a_ref[d]
        for name, ref in zip(names, out_refs):
            r0, nr, nl = SMALL_LAYOUT[name]
            ref[...] = acc[r0:r0 + nr, 0:nl]

    outs = pl.pallas_call(
        body, name="sum_small",
        out_shape=[jax.ShapeDtypeStruct(SMALL_LAYOUT[n][1:], F32) for n in names], compiler_params=_cp(16))(gathered)
    return dict(zip(names, outs))


def _adamw(ws, gs, ms, vs):
    n = len(ws)
    c1 = 1.0 - ADAM_B1 ** ADAM_STEP
    c2 = 1.0 - ADAM_B2 ** ADAM_STEP

    def body(*refs):
        w_refs, g_refs, m_refs, v_refs = (refs[k * n:(k + 1) * n] for k in range(4))
        d_refs, nm_refs, nv_refs = (refs[(4 + k) * n:(5 + k) * n] for k in range(3))
        for w_ref, g_ref, m_ref, v_ref, d_ref, nm_ref, nv_ref in zip(w_refs, g_refs, m_refs, v_refs, d_refs,
                                                                     nm_refs, nv_refs):
            g = g_ref[...]
            m = ADAM_B1 * m_ref[...] + (1.0 - ADAM_B1) * g
            v = ADAM_B2 * v_ref[...] + (1.0 - ADAM_B2) * (g * g)
            nm_ref[...] = m
            nv_ref[...] = v
            d_ref[...] = -ADAM_LR * ((m / c1) / (jnp.sqrt(v / c2) + ADAM_EPS) + ADAM_WD * w_ref[...])

    shapes = [jax.ShapeDtypeStruct(w.shape, F32) for w in ws]
    outs = pl.pallas_call(body, name="adamw", out_shape=shapes * 3, compiler_params=_cp(48))(*ws, *gs, *ms, *vs)
    return outs[:n], outs[n:2 * n], outs[2 * n:]


W_NAMES = ("meta", "ln_in_g", "ln_in_b", "w_in", "b_f", "conv_w", "conv_b", "ln_conv_g", "ln_conv_b", "w_pw",
           "w_out", "ln_out_g", "ln_out_b")


def kernel(x, meta, ln_in_g, ln_in_b, w_in, b_f, conv_w, conv_b, ln_conv_g, ln_conv_b, w_pw, w_out, ln_out_g, ln_out_b, loss_target, m_meta, m_ln_in_g, m_ln_in_b, m_w_in, m_b_f, m_conv_w, m_conv_b, m_ln_conv_g, m_ln_conv_b, m_w_pw, m_w_out, m_ln_out_g, m_ln_out_b, v_meta, v_ln_in_g, v_ln_in_b, v_w_in, v_b_f, v_conv_w, v_conv_b, v_ln_conv_g, v_ln_conv_b, v_w_pw, v_w_out, v_ln_out_g, v_ln_out_b):
    mx, my, mc = _mesh_pos()
    me = 4 * mx + 2 * my + mc
    n_meta_sh = D_MODEL // N_DEV
    n_cw_sh = D_CONV // N_DEV
    n_out_sh = D_MODEL // N_DEV
    n_pw_sh = D_CONV // N_DEV

    small_w = jnp.concatenate([meta, jnp.pad(conv_w[0], ((0, 1), (0, LANES - n_cw_sh)))], axis=0)
    all_in, all_out, all_pw, all_small = _all_gather(
        [jnp.pad(w_in[0].T, ((0, 512 - SHARD_IN), (0, 0))).astype(BF16), w_out[0].astype(BF16), w_pw[0].astype(BF16),
         small_w], "gather_weights")
    w_r, w_t, metapad, cw = _repack_weights(all_in, all_small)
    w_out_full = all_out.reshape(D_MODEL, D_MODEL)
    w_pw_full = all_pw.reshape(D_CONV, D_CONV)

    grad_x, pc = _local_step(x[0], loss_target[0], metapad, cw, w_r, w_t, w_pw_full, w_out_full, ln_in_g, ln_in_b,
                             b_f[0], conv_b[0], ln_conv_g[0], ln_conv_b[0], ln_out_g[0], ln_out_b[0])

    g8s = [_unpack_dw_in(pc["w_in_rm"], pc["w_in_t"]), pc["w_out"].reshape(N_DEV, n_out_sh, D_MODEL),
           pc["w_pw"].reshape(N_DEV, n_pw_sh, D_CONV)]
    from_sibling, all_small_g = _exchange_sibling(g8s, _pack_small(pc))
    p32s, pbs = _rs_add_sibling(g8s, from_sibling, jnp.reshape(mc, (1,)).astype(jnp.int32))
    from_chips = _exchange_chips(pbs)
    g_w_in, g_w_out, g_w_pw = _rs_add_chips(p32s, from_chips, jnp.reshape(2 * mx + my, (1,)).astype(jnp.int32))

    sm = _sum_small(all_small_g)
    grads = {
        "meta": lax.dynamic_slice_in_dim(sm["metapad"], me * n_meta_sh, n_meta_sh, axis=1),
        "ln_in_g": sm["ln_in_g"].reshape(D_MODEL), "ln_in_b": sm["ln_in_b"].reshape(D_MODEL), "w_in": g_w_in.T[None],
        "b_f": sm["b_f"][:, :N_HEADS],
        "conv_w": lax.dynamic_slice_in_dim(sm["conv_w"], me * n_cw_sh, n_cw_sh, axis=1)[None, :CONV_WIDTH],
        "conv_b": sm["conv_b"], "ln_conv_g": sm["ln_conv_g"], "ln_conv_b": sm["ln_conv_b"], "w_pw": g_w_pw,
        "w_out": g_w_out, "ln_out_g": sm["ln_out_g"], "ln_out_b": sm["ln_out_b"]}
    loss_all = sm["loss"][0, 0]

    weights = dict(meta=meta, ln_in_g=ln_in_g, ln_in_b=ln_in_b, w_in=w_in, b_f=b_f, conv_w=conv_w, conv_b=conv_b,
                   ln_conv_g=ln_conv_g, ln_conv_b=ln_conv_b, w_pw=w_pw, w_out=w_out, ln_out_g=ln_out_g,
                   ln_out_b=ln_out_b)
    moms = dict(meta=m_meta, ln_in_g=m_ln_in_g, ln_in_b=m_ln_in_b, w_in=m_w_in, b_f=m_b_f, conv_w=m_conv_w,
                conv_b=m_conv_b, ln_conv_g=m_ln_conv_g, ln_conv_b=m_ln_conv_b, w_pw=m_w_pw, w_out=m_w_out,
                ln_out_g=m_ln_out_g, ln_out_b=m_ln_out_b)
    vels = dict(meta=v_meta, ln_in_g=v_ln_in_g, ln_in_b=v_ln_in_b, w_in=v_w_in, b_f=v_b_f, conv_w=v_conv_w,
                conv_b=v_conv_b, ln_conv_g=v_ln_conv_g, ln_conv_b=v_ln_conv_b, w_pw=v_w_pw, w_out=v_w_out,
                ln_out_g=v_ln_out_g, ln_out_b=v_ln_out_b)

    def to_kernel(name, a):
        if name == "w_in":
            return a[0].T
        return a.reshape(1, -1) if a.ndim == 1 else a

    def from_kernel(name, a):
        return a.T[None] if name == "w_in" else a.reshape(weights[name].shape)

    upd = _adamw(*[[to_kernel(n, d[n]) for n in W_NAMES] for d in (weights, grads, moms, vels)])
    deltas, new_m, new_v = ([from_kernel(n, a) for n, a in zip(W_NAMES, part)] for part in upd)
    return (loss_all, grad_x[None], *[grads[n] for n in W_NAMES], *deltas, *new_m, *new_v)
```

```python
import jax
import jax.numpy as jnp
import numpy as np
from jax import lax
from jax.experimental import pallas as pl
from jax.experimental.pallas import tpu as pltpu

F32 = jnp.float32
BF16 = jnp.bfloat16

D_MODEL = 1024
D_ATTN = 512
D_CONV = 512
N_HEADS = 8
HEAD_DIM = 64
N_META = 16
CONV_WIDTH = 31
LN_EPS = 1e-5
ALPHA = 2.0 ** 0.25
SCALE = HEAD_DIM ** -0.5
LOG2E = 1.4426950408889634
ADAM_LR, ADAM_B1, ADAM_B2, ADAM_EPS, ADAM_WD, ADAM_STEP = 0.001, 0.9, 0.999, 1e-08, 0.01, 10

N_DEV = 8
D_IN = 3592
SHARD_IN = D_IN // N_DEV
TILE = 256
PAD = TILE - N_META
HALO = 32
SHIFT_ROWS = TILE + HALO
EXT_ROWS = SHIFT_ROWS + 8
NEG = -1e30
LANES = 128
W_COLS = 7 * 512 + LANES
OFF_GA_R, OFF_F_R = 1536, 3584
MIB = 1024 * 1024


def _cp(vmem_mib, sem=None):
    kw = dict(vmem_limit_bytes=vmem_mib * MIB)
    if sem is not None:
        kw["dimension_semantics"] = sem
    return pltpu.CompilerParams(**kw)


def _sigmoid(x):
    return 1.0 / (1.0 + jnp.exp(-x))


def _silu_and_grad(x):
    s = _sigmoid(x)
    return x * s, s * (1.0 + x * (1.0 - s))


def _ln_stats(x):
    mu = jnp.mean(x, axis=-1, keepdims=True)
    xc = x - mu
    var = jnp.mean(xc * xc, axis=-1, keepdims=True)
    rstd = lax.rsqrt(var + LN_EPS)
    return xc * rstd, rstd


def _ln_bwd(dy, xhat, rstd, g):
    dxh = dy * g
    m1 = jnp.mean(dxh, axis=-1, keepdims=True)
    m2 = jnp.mean(dxh * xhat, axis=-1, keepdims=True)
    return rstd * (dxh - m1 - xhat * m2)


def _row_spec(cols, shift=False):
    if shift:
        return pl.BlockSpec((TILE, cols), lambda i: (jnp.maximum(i - 1, 0), 0))
    return pl.BlockSpec((TILE, cols), lambda i: (i, 0))


def _full_spec(shape):
    nd = len(shape)
    return pl.BlockSpec(shape, lambda i: (0,) * nd)


def _t3_spec(ch):
    return pl.BlockSpec((1, ch, TILE), lambda i: (i, 0, 0))


def _proj_fwd(x, metapad, g_in, b_in, w_r, nt):
    lp = nt * TILE

    def body(x_ref, mp_ref, g_ref, b_ref, w_ref, hb_ref, qT_ref, kT_ref, vT_ref, k_ref, v_ref,
             ga_ref, u_ref, ug_ref, gc_ref, fl_ref):
        i = pl.program_id(0)
        x0 = jnp.where(i == 0, mp_ref[...], x_ref[...])
        xhat, _ = _ln_stats(x0)
        hb = (xhat * g_ref[...] + b_ref[...]).astype(BF16)
        hb_ref[...] = hb

        def sec(off, n=512):
            return jnp.dot(hb, w_ref[:, off:off + n], preferred_element_type=F32)

        qT_ref[0] = (sec(0) * (SCALE * LOG2E)).T.astype(BF16)
        k = sec(512)
        kT_ref[0] = k.T.astype(BF16)
        k_ref[...] = k.astype(BF16)
        v = sec(1024)
        vT_ref[0] = v.T.astype(BF16)
        v_ref[...] = v.astype(BF16)
        ga_ref[...] = sec(OFF_GA_R).astype(BF16)
        u_ref[...] = sec(OFF_GA_R + 512).astype(BF16)
        ug_ref[...] = sec(OFF_GA_R + 1024).astype(BF16)
        gc_ref[...] = sec(OFF_GA_R + 1536).astype(BF16)
        fl_ref[...] = sec(OFF_F_R, LANES)

    t3 = jax.ShapeDtypeStruct((nt, 512, TILE), BF16)
    rm = lambda dt: jax.ShapeDtypeStruct((lp, 512), dt)
    return pl.pallas_call(
        body, name="proj_fwd", grid=(nt,),
        in_specs=[_row_spec(D_MODEL, shift=True), _full_spec((TILE, D_MODEL)), _full_spec((1, D_MODEL)),
                  _full_spec((1, D_MODEL)), _full_spec((D_MODEL, W_COLS))],
        out_specs=[_row_spec(D_MODEL), _t3_spec(512), _t3_spec(512), _t3_spec(512), _row_spec(512), _row_spec(512),
                   _row_spec(512), _row_spec(512), _row_spec(512), _row_spec(512), _row_spec(LANES)],
        out_shape=[jax.ShapeDtypeStruct((lp, D_MODEL), BF16), t3, t3, t3, rm(BF16), rm(BF16),
                   rm(BF16), rm(BF16), rm(BF16), rm(BF16), jax.ShapeDtypeStruct((lp, LANES), F32)],
        compiler_params=_cp(56, ("arbitrary",)),
    )(x, metapad, g_in, b_in, w_r)


def _row_mask(i, shape):
    r = lax.broadcasted_iota(jnp.int32, shape, 0)
    return (r >= PAD) | (i > 0)


def _cumsum_fwd(fl, bf_pad, nt):
    lp = nt * TILE

    def body(fl_ref, bf_ref, kx_ref, carry):
        i = pl.program_id(0)

        @pl.when(i == 0)
        def _():
            carry[...] = jnp.zeros_like(carry)

        z = fl_ref[...] + bf_ref[...]
        lf = jnp.minimum(z, 0.0) - jnp.log(1.0 + jnp.exp(-jnp.abs(z)))
        lane = lax.broadcasted_iota(jnp.int32, (TILE, LANES), 1)
        real = _row_mask(i, (TILE, LANES))
        lf = jnp.where(real & (lane < N_HEADS), lf, 0.0)
        r = lax.broadcasted_iota(jnp.int32, (TILE, TILE), 0)
        c = lax.broadcasted_iota(jnp.int32, (TILE, TILE), 1)
        tril = (c <= r).astype(F32)
        cs = jnp.dot(tril, lf, precision=lax.Precision.HIGHEST, preferred_element_type=F32) + carry[...]
        carry[...] = cs[TILE - 1:TILE, :]
        bias = jnp.where(real, cs * (-LOG2E), NEG)
        hi = bias.astype(BF16).astype(F32)
        mid = (bias - hi).astype(BF16).astype(F32)
        lo = (bias - hi - mid).astype(BF16).astype(F32)
        for p in range(N_HEADS // 2):
            out = jnp.zeros((TILE, LANES), F32)
            for hh in range(2):
                for part, piece in enumerate((hi, mid, lo)):
                    dst, src = 3 * hh + part, 2 * p + hh
                    moved = piece if dst == src else pltpu.roll(piece, (dst - src) % LANES, 1)
                    out = jnp.where(lane == dst, moved, out)
            kx_ref[p] = out.astype(BF16)

    return pl.pallas_call(
        body, name="cumsum_fwd", grid=(nt,),
        in_specs=[_row_spec(LANES), _full_spec((1, LANES))],
        out_specs=pl.BlockSpec((N_HEADS // 2, TILE, LANES), lambda i: (0, i, 0)),
        out_shape=jax.ShapeDtypeStruct((N_HEADS // 2, lp, LANES), BF16),
        scratch_shapes=[pltpu.VMEM((1, LANES), F32)],
        compiler_params=_cp(32, ("arbitrary",)),
    )(fl, bf_pad)


def _head_rows(blk, hh):
    r = lax.broadcasted_iota(jnp.int32, blk.shape, 0)
    return jnp.where((r >= hh * HEAD_DIM) & (r < (hh + 1) * HEAD_DIM), blk, jnp.zeros_like(blk))


def _two_heads(blk):
    return jnp.concatenate([_head_rows(blk, 0), _head_rows(blk, 1)], axis=1)


def _bias_rows():
    r = lax.broadcasted_iota(jnp.int32, (LANES, 2 * TILE), 0)
    c = lax.broadcasted_iota(jnp.int32, (LANES, 2 * TILE), 1)
    return jnp.where(((r < 3) & (c < TILE)) | ((r >= 3) & (r < 6) & (c >= TILE)), 1.0, 0.0).astype(BF16)


def _diag_mask(s):
    kpos = lax.broadcasted_iota(jnp.int32, (TILE, TILE), 0)
    qpos = lax.broadcasted_iota(jnp.int32, (TILE, TILE), 1)
    return jnp.where(kpos <= qpos, s, NEG)


def _stream(n, first, nxt, scores, update):
    if n == 0:
        return
    scores(first, 0)

    def pair_body(_, idx):
        idx_b = nxt(idx)
        scores(idx_b, 1)
        update(idx, 0)
        idx_c = nxt(idx_b)
        scores(idx_c, 0)
        update(idx_b, 1)
        return idx_c

    idx = lax.fori_loop(0, (n - 1) // 2, pair_body, first)
    if n % 2 == 1:
        update(idx, 0)
    else:
        idx_b = nxt(idx)
        scores(idx_b, 1)
        update(idx, 0)
        update(idx_b, 1)


def _next_below_diagonal(idx):
    i, j = idx
    wrap = j + 1 >= i
    return jnp.where(wrap, i + 1, i), jnp.where(wrap, 0, j + 1)


def _tile_rows(t):
    return pl.ds(pl.multiple_of(t * TILE, TILE), TILE)


def _two_streams(nt):
    load, group = [0, 0], {}
    for i in sorted(range(1, nt), reverse=True):
        g = 0 if load[0] <= load[1] else 1
        group[i] = g
        load[g] += i
    rows = [[(i, i, j) for i in range(1, nt) if group[i] == g for j in range(i)] for g in range(2)]
    length = max(len(r) for r in rows)
    rows = [r + [(nt, 0, 0)] * (length - len(r)) for r in rows]
    return group, np.asarray(rows, np.int32).reshape(2, -1), length


def _attn_fwd(qT3, k, kx3, vT3, nt):
    lp = nt * TILE
    npair = N_HEADS // 2
    group, table, n_stream = _two_streams(nt)

    def body(tab_ref, qT_ref, k_ref, kx_ref, vT_ref, oT_ref, o_ref, lse_ref, sbuf, m_0, l_0, acc_0, m_1, l_1, acc_1):
        ones = _bias_rows()
        states = ((m_0, l_0, acc_0), (m_1, l_1, acc_1))

        def scores(i, j, slot):
            qcat = jnp.concatenate([_two_heads(qT_ref[i]), ones], axis=0)
            kext = jnp.concatenate([k_ref[_tile_rows(j), :], kx_ref[0, _tile_rows(j), :]], axis=1)
            sbuf[slot] = jnp.dot(kext, qcat, preferred_element_type=F32)

        def update(st, j, slot, state, diag):
            m_s, l_s, acc_s = state
            for hh in range(2):
                s = sbuf[slot, :, hh * TILE:(hh + 1) * TILE]
                vj = vT_ref[j, hh * HEAD_DIM:(hh + 1) * HEAD_DIM, :]
                if diag:
                    s = _diag_mask(s)
                    m_new = jnp.max(s, axis=0, keepdims=True)
                    p = jnp.exp2(s - m_new)
                    l_s[st, hh] = jnp.sum(p, axis=0, keepdims=True)
                    acc_s[st, hh] = jnp.dot(vj, p.astype(BF16), preferred_element_type=F32)
                else:
                    m_prev = m_s[st, hh]
                    m_new = jnp.maximum(m_prev, jnp.max(s, axis=0, keepdims=True))
                    a = jnp.exp2(m_prev - m_new)
                    p = jnp.exp2(s - m_new)
                    l_s[st, hh] = a * l_s[st, hh] + jnp.sum(p, axis=0, keepdims=True)
                    acc_s[st, hh] = a * acc_s[st, hh] + jnp.dot(vj, p.astype(BF16), preferred_element_type=F32)
                m_s[st, hh] = m_new

        _stream(nt, jnp.int32(0), lambda t: t + 1, lambda t, slot: scores(t, t, slot),
                lambda t, slot: update(t, t, slot, states[0], True))
        for dst, src in zip(states[1], states[0]):
            dst[0:nt] = src[0:nt]
        for m_s, l_s, acc_s in states:
            m_s[nt] = jnp.full(m_s.shape[1:], NEG, F32)
            l_s[nt] = jnp.zeros(l_s.shape[1:], F32)
            acc_s[nt] = jnp.zeros(acc_s.shape[1:], F32)

        def entry(g, t):
            return tab_ref[g, 3 * t], tab_ref[g, 3 * t + 1], tab_ref[g, 3 * t + 2]

        def scores2(t, slot):
            for g in range(2):
                _, qi, kj = entry(g, t)
                scores(qi, kj, 2 * g + slot)

        def update2(t, slot):
            for g in range(2):
                st, _, kj = entry(g, t)
                update(st, kj, 2 * g + slot, states[g], False)

        _stream(n_stream, jnp.int32(0), lambda t: t + 1, scores2, update2)

        for i in range(nt):
            m_s, l_s, acc_s = states[group.get(i, 0)]
            for hh in range(2):
                l = l_s[i, hh]
                oT_ref[i, hh * HEAD_DIM:(hh + 1) * HEAD_DIM, :] = acc_s[i, hh] / l
                lse_ref[0, i, hh:hh + 1, :] = m_s[i, hh] + jnp.log(l) * LOG2E
            o_ref[i * TILE:(i + 1) * TILE, :] = oT_ref[i].T.astype(BF16)

    blk_t = pl.BlockSpec((nt, LANES, TILE), lambda p, tab: (0, p, 0))
    blk_rm = pl.BlockSpec((lp, LANES), lambda p, tab: (0, p))
    blk_px = pl.BlockSpec((1, lp, LANES), lambda p, tab: (p, 0, 0))
    blk_st = pl.BlockSpec((1, nt, 8, TILE), lambda p, tab: (p, 0, 0, 0))
    state = [pltpu.VMEM((nt + 1, 2, 1, TILE), F32), pltpu.VMEM((nt + 1, 2, 1, TILE), F32),
             pltpu.VMEM((nt + 1, 2, HEAD_DIM, TILE), F32)]
    grid_spec = pltpu.PrefetchScalarGridSpec(
        num_scalar_prefetch=1, grid=(npair,), in_specs=[blk_t, blk_rm, blk_px, blk_t],
        out_specs=[blk_t, blk_rm, blk_st], scratch_shapes=[pltpu.VMEM((4, TILE, 2 * TILE), F32)] + state + state)
    return pl.pallas_call(
        body, name="attn_fwd", grid_spec=grid_spec,
        out_shape=[jax.ShapeDtypeStruct((nt, D_ATTN, TILE), F32),
                   jax.ShapeDtypeStruct((lp, D_ATTN), BF16),
                   jax.ShapeDtypeStruct((npair, nt, 8, TILE), F32)],
        compiler_params=_cp(60, ("arbitrary",)),
    )(jnp.asarray(table), qT3, k, kx3, vT3)


def _attn_bwd(qT3, kT3, k, kx3, v, oT3, doT3, lse4, nt):
    lp = nt * TILE
    npair = N_HEADS // 2

    def body(qT_ref, kT_ref, k_ref, kx_ref, v_ref, oT_ref, doT_ref, lse_ref,
             dqT_ref, dkT_ref, dvT_ref, dck_ref, dcq_ref, sbuf, dpbuf, dq_s, dk_s, dv_s, dc_s, tp_s, tds_s):
        ones = _bias_rows()

        def scores(idx, slot):
            i, j = idx
            qcat = jnp.concatenate([_two_heads(qT_ref[i]), ones], axis=0)
            kext = jnp.concatenate([k_ref[_tile_rows(j), :], kx_ref[0, _tile_rows(j), :]], axis=1)
            sbuf[slot] = jnp.dot(kext, qcat, preferred_element_type=F32)
            dpbuf[slot] = jnp.dot(v_ref[_tile_rows(j), :], _two_heads(doT_ref[i]), preferred_element_type=F32)

        def update(idx, slot, diag):
            i, j = idx
            for hh in range(2):
                hs = slice(hh * HEAD_DIM, (hh + 1) * HEAD_DIM)
                s = sbuf[slot, :, hh * TILE:(hh + 1) * TILE]
                if diag:
                    s = _diag_mask(s)
                p = jnp.exp2(s - lse_ref[0, i, hh:hh + 1, :])
                doh = doT_ref[i, hs, :]
                delta = jnp.sum(doh.astype(F32) * oT_ref[i, hs, :], axis=0, keepdims=True)
                ds = p * (dpbuf[slot, :, hh * TILE:(hh + 1) * TILE] - delta)
                dsb = ds.astype(BF16)
                tp_s[hh] = p.astype(BF16).T
                tds_s[hh] = dsb.T
                dv = jnp.dot(doh, tp_s[hh], preferred_element_type=F32)
                dk = jnp.dot(qT_ref[i, hs, :], tds_s[hh], preferred_element_type=F32)
                dq = jnp.dot(kT_ref[j, hs, :], dsb, preferred_element_type=F32)
                dc = ds[:, :LANES] + ds[:, LANES:]
                dcq = jnp.sum(ds, axis=0, keepdims=True)
                if diag:
                    dv_s[j, hh] = dv
                    dk_s[j, hh] = dk
                    dc_s[j, hh] = dc
                    dq_s[i, hs, :] = dq
                    dcq_ref[0, i, hh:hh + 1, :] = dcq
                else:
                    dv_s[j, hh] += dv
                    dk_s[j, hh] += dk
                    dc_s[j, hh] += dc
                    dq_s[i, hs, :] += dq
                    dcq_ref[0, i, hh:hh + 1, :] += dcq

        dcq_ref[...] = jnp.zeros_like(dcq_ref)
        zero = jnp.int32(0)
        _stream(nt, (zero, zero), lambda idx: (idx[0] + 1, idx[1] + 1), scores,
                lambda idx, slot: update(idx, slot, True))
        _stream(nt * (nt - 1) // 2, (zero + 1, zero), _next_below_diagonal, scores,
                lambda idx, slot: update(idx, slot, False))

        lane = lax.broadcasted_iota(jnp.int32, (TILE, LANES), 1)

        def finish(t, carry):
            dck = jnp.zeros((TILE, LANES), F32)
            for hh in range(2):
                hs = slice(hh * HEAD_DIM, (hh + 1) * HEAD_DIM)
                dkT_ref[t, hs, :] = (dk_s[t, hh] * (1.0 / LOG2E)).astype(BF16)
                dvT_ref[t, hs, :] = dv_s[t, hh].astype(BF16)
                dck = jnp.where(lane == hh, -jnp.sum(dc_s[t, hh], axis=1, keepdims=True), dck)
            dck_ref[0, _tile_rows(t), :] = dck
            dqT_ref[t] = (dq_s[t] * SCALE).astype(BF16)
            return carry

        lax.fori_loop(0, nt, finish, 0)

    blk_t = pl.BlockSpec((nt, LANES, TILE), lambda p: (0, p, 0))
    blk_rm = pl.BlockSpec((lp, LANES), lambda p: (0, p))
    blk_px = pl.BlockSpec((1, lp, LANES), lambda p: (p, 0, 0))
    blk_st = pl.BlockSpec((1, nt, 8, TILE), lambda p: (p, 0, 0, 0))
    t3 = jax.ShapeDtypeStruct((nt, D_ATTN, TILE), BF16)
    return pl.pallas_call(
        body, name="attn_bwd", grid=(npair,),
        in_specs=[blk_t, blk_t, blk_rm, blk_px, blk_rm, blk_t, blk_t, blk_st],
        out_specs=[blk_t, blk_t, blk_t, blk_px, blk_st],
        out_shape=[t3, t3, t3, jax.ShapeDtypeStruct((npair, lp, LANES), F32),
                   jax.ShapeDtypeStruct((npair, nt, 8, TILE), F32)],
        scratch_shapes=[pltpu.VMEM((2, TILE, 2 * TILE), F32), pltpu.VMEM((2, TILE, 2 * TILE), F32),
                        pltpu.VMEM((nt, LANES, TILE), F32), pltpu.VMEM((nt, 2, HEAD_DIM, TILE), F32),
                        pltpu.VMEM((nt, 2, HEAD_DIM, TILE), F32), pltpu.VMEM((nt, 2, TILE, LANES), F32),
                        pltpu.VMEM((2, TILE, TILE), BF16), pltpu.VMEM((2, TILE, TILE), BF16)],
        compiler_params=_cp(60, ("arbitrary",)),
    )(qT3, kT3, k, kx3, v, oT3, doT3, lse4)


def _glu(u, ug, i):
    return jnp.where(_row_mask(i, u.shape), u.astype(F32) * _sigmoid(ug.astype(F32)), 0.0)


def _shifted_copies(dst, src):
    for ph in range(8):
        dst[ph] = src[ph:ph + SHIFT_ROWS, :]


def _tap_window(sh, off, lanes, row0=0, rows=TILE):
    base = (off // 8) * 8 + row0
    return sh[off % 8, base:base + rows, lanes]


def _conv_fwd(u, ug, conv_w, conv_b, g, b, w_pw, nt):
    lp = nt * TILE

    def body(u_ref, ug_ref, up_ref, ugp_ref, w_ref, cb_ref, g_ref, b_ref, wpw_ref,
             co_ref, hc_ref, pw_ref, ext, sh):
        i = pl.program_id(0)
        prev = _glu(up_ref[...], ugp_ref[...], i - 1)
        ext[0:HALO, :] = jnp.where(i > 0, prev[TILE - HALO:, :], 0.0)
        ext[HALO:HALO + TILE, :] = _glu(u_ref[...], ug_ref[...], i)
        ext[HALO + TILE:, :] = jnp.zeros((8, D_CONV), F32)
        _shifted_copies(sh, ext)
        for lb in range(D_CONV // LANES):
            lanes = slice(lb * LANES, (lb + 1) * LANES)
            acc = jnp.zeros((TILE, LANES), F32) + cb_ref[:, lanes]
            for t in range(CONV_WIDTH):
                off = HALO - (CONV_WIDTH - 1) + t
                acc = acc + w_ref[t:t + 1, lanes] * _tap_window(sh, off, lanes)
            co_ref[:, lanes] = acc
        xhat, _ = _ln_stats(co_ref[...])
        a, _ = _silu_and_grad(xhat * g_ref[...] + b_ref[...])
        hc = a.astype(BF16)
        hc_ref[...] = hc
        pw_ref[...] = jnp.dot(hc, wpw_ref[...], preferred_element_type=F32).astype(BF16)

    rm = lambda dt: jax.ShapeDtypeStruct((lp, D_CONV), dt)
    return pl.pallas_call(
        body, name="conv_fwd", grid=(nt,),
        in_specs=[_row_spec(512), _row_spec(512), _row_spec(512, shift=True), _row_spec(512, shift=True),
                  _full_spec((32, 512)), _full_spec((1, 512)), _full_spec((1, 512)), _full_spec((1, 512)),
                  _full_spec((512, 512))],
        out_specs=[_row_spec(512), _row_spec(512), _row_spec(512)],
        out_shape=[rm(F32), rm(BF16), rm(BF16)],
        scratch_shapes=[pltpu.VMEM((EXT_ROWS, D_CONV), F32), pltpu.VMEM((8, SHIFT_ROWS, D_CONV), F32)],
        compiler_params=_cp(40, ("arbitrary",)),
    )(u, ug, u, ug, conv_w, conv_b, g, b, w_pw)


def _out_fwd(o, ga, pw, gc, x, metapad, g_in, b_in, w_out, g_out, b_out, target, nt):
    lp = nt * TILE

    def body(o_ref, ga_ref, pw_ref, gc_ref, x_ref, mp_ref, gi_ref, bi_ref, wo_ref, go_ref, bo_ref, t_ref,
             y_ref, dz_ref, loss_ref, dgo_ref, dbo_ref):
        i = pl.program_id(0)

        @pl.when(i == 0)
        def _():
            loss_ref[...] = jnp.zeros_like(loss_ref)
            dgo_ref[...] = jnp.zeros_like(dgo_ref)
            dbo_ref[...] = jnp.zeros_like(dbo_ref)

        x0 = jnp.where(i == 0, mp_ref[...], x_ref[...])
        xhat, _ = _ln_stats(x0)
        h = xhat * gi_ref[...] + bi_ref[...]
        ya, _ = _silu_and_grad(ga_ref[...].astype(F32))
        yc, _ = _silu_and_grad(gc_ref[...].astype(F32))
        ya = (o_ref[...].astype(F32) * ya).astype(BF16)
        yc = (pw_ref[...].astype(F32) * yc).astype(BF16)
        y_ref[:, :D_ATTN] = ya
        y_ref[:, D_ATTN:] = yc
        z = ALPHA * h + jnp.dot(ya, wo_ref[:D_ATTN, :], preferred_element_type=F32) \
            + jnp.dot(yc, wo_ref[D_ATTN:, :], preferred_element_type=F32)
        zhat, rstd = _ln_stats(z)
        out = zhat * go_ref[...] + bo_ref[...]
        live = (i > 0).astype(F32)
        err = (out - t_ref[...]) * live
        dout = err * (1.0 / D_MODEL)
        loss_ref[...] += 0.5 * jnp.sum(jnp.sum(err * dout, axis=0, keepdims=True), axis=1, keepdims=True)
        dgo_ref[...] += jnp.sum(dout * zhat, axis=0, keepdims=True)
        dbo_ref[...] += jnp.sum(dout, axis=0, keepdims=True)
        dz_ref[...] = _ln_bwd(dout, zhat, rstd, go_ref[...])

    return pl.pallas_call(
        body, name="out_fwd", grid=(nt,),
        in_specs=[_row_spec(512), _row_spec(512), _row_spec(512), _row_spec(512),
                  _row_spec(D_MODEL, shift=True), _full_spec((TILE, D_MODEL)), _full_spec((1, D_MODEL)),
                  _full_spec((1, D_MODEL)), _full_spec((D_MODEL, D_MODEL)), _full_spec((1, D_MODEL)),
                  _full_spec((1, D_MODEL)), _row_spec(D_MODEL, shift=True)],
        out_specs=[_row_spec(D_MODEL), _row_spec(D_MODEL), _full_spec((1, LANES)), _full_spec((1, D_MODEL)),
                   _full_spec((1, D_MODEL))],
        out_shape=[jax.ShapeDtypeStruct((lp, D_MODEL), BF16), jax.ShapeDtypeStruct((lp, D_MODEL), F32),
                   jax.ShapeDtypeStruct((1, LANES), F32), jax.ShapeDtypeStruct((1, D_MODEL), F32),
                   jax.ShapeDtypeStruct((1, D_MODEL), F32)],
        compiler_params=_cp(40, ("arbitrary",)),
    )(o, ga, pw, gc, x, metapad, g_in, b_in, w_out, g_out, b_out, target)


def _out_bwd(dz, y, o, ga, pw, gc, w_out, hc, co, w_pw, g_cv, b_cv, nt):
    lp = nt * TILE

    def body(dz_ref, y_ref, o_ref, ga_ref, pw_ref, gc_ref, wo_ref, hc_ref, co_ref, wpw_ref, g_ref, b_ref,
             doT_ref, dga_ref, dgc_ref, dwo_ref, dco_ref, dwpw_ref, dg_ref, db_ref, dcb_ref):
        i = pl.program_id(0)

        @pl.when(i == 0)
        def _():
            dwo_ref[...] = jnp.zeros_like(dwo_ref)
            dwpw_ref[...] = jnp.zeros_like(dwpw_ref)
            dg_ref[...] = jnp.zeros_like(dg_ref)
            db_ref[...] = jnp.zeros_like(db_ref)
            dcb_ref[...] = jnp.zeros_like(dcb_ref)

        dzb = dz_ref[...].astype(BF16)
        nt_dims = (((1,), (1,)), ((), ()))
        tn_dims = (((0,), (0,)), ((), ()))
        dya = lax.dot_general(dzb, wo_ref[:D_ATTN, :], nt_dims, preferred_element_type=F32)
        dyc = lax.dot_general(dzb, wo_ref[D_ATTN:, :], nt_dims, preferred_element_type=F32)
        sa, sga = _silu_and_grad(ga_ref[...].astype(F32))
        sc, sgc = _silu_and_grad(gc_ref[...].astype(F32))
        doT_ref[0] = (dya * sa).T.astype(BF16)
        dga_ref[...] = (dya * o_ref[...].astype(F32) * sga).astype(BF16)
        dpw_b = (dyc * sc).astype(BF16)
        dgc_ref[...] = (dyc * pw_ref[...].astype(F32) * sgc).astype(BF16)
        dwo_ref[...] += lax.dot_general(y_ref[...], dzb, tn_dims, preferred_element_type=F32)

        dhc = lax.dot_general(dpw_b, wpw_ref[...], nt_dims, preferred_element_type=F32)
        xhat, rstd = _ln_stats(co_ref[...])
        _, sg = _silu_and_grad(xhat * g_ref[...] + b_ref[...])
        dln = dhc * sg
        dg_ref[...] += jnp.sum(dln * xhat, axis=0, keepdims=True)
        db_ref[...] += jnp.sum(dln, axis=0, keepdims=True)
        dco = _ln_bwd(dln, xhat, rstd, g_ref[...])
        dco_ref[...] = dco
        dcb_ref[...] += jnp.sum(dco, axis=0, keepdims=True)
        dwpw_ref[...] += lax.dot_general(hc_ref[...], dpw_b, tn_dims, preferred_element_type=F32)

    rm = jax.ShapeDtypeStruct((lp, 512), BF16)
    vec = jax.ShapeDtypeStruct((1, D_CONV), F32)
    return pl.pallas_call(
        body, name="out_bwd", grid=(nt,),
        in_specs=[_row_spec(D_MODEL), _row_spec(D_MODEL), _row_spec(512), _row_spec(512), _row_spec(512),
                  _row_spec(512), _full_spec((D_MODEL, D_MODEL)), _row_spec(512), _row_spec(512),
                  _full_spec((512, 512)), _full_spec((1, 512)), _full_spec((1, 512))],
        out_specs=[_t3_spec(512), _row_spec(512), _row_spec(512), _full_spec((D_MODEL, D_MODEL)), _row_spec(512),
                   _full_spec((512, 512)), _full_spec((1, 512)), _full_spec((1, 512)), _full_spec((1, 512))],
        out_shape=[jax.ShapeDtypeStruct((nt, 512, TILE), BF16), rm, rm, jax.ShapeDtypeStruct((D_MODEL, D_MODEL), F32),
                   jax.ShapeDtypeStruct((lp, D_CONV), F32), jax.ShapeDtypeStruct((512, 512), F32), vec, vec, vec],
        compiler_params=_cp(56, ("arbitrary",)),
    )(dz, y, o, ga, pw, gc, w_out, hc, co, w_pw, g_cv, b_cv)


def _conv_bwd_taps(dco, u, ug, conv_w, nt):
    lp = nt * TILE

    def body(dco_ref, dcon_ref, u_ref, ug_ref, up_ref, ugp_ref, w3_ref, du_ref, dug_ref, dw_ref, ext, dext, sh, dsh,
             dhg_s, dw_s):
        i = pl.program_id(0)

        @pl.when(i == 0)
        def _():
            dw_s[...] = jnp.zeros_like(dw_s)

        prev = _glu(up_ref[...], ugp_ref[...], i - 1)
        ext[0:HALO, :] = jnp.where(i > 0, prev[TILE - HALO:, :], 0.0)
        ext[HALO:HALO + TILE, :] = _glu(u_ref[...], ug_ref[...], i)
        ext[HALO + TILE:, :] = jnp.zeros((8, D_CONV), F32)
        dext[0:TILE, :] = dco_ref[...]
        dext[TILE:TILE + HALO, :] = jnp.where(i < nt - 1, dcon_ref[0:HALO, :], 0.0)
        dext[TILE + HALO:, :] = jnp.zeros((8, D_CONV), F32)
        _shifted_copies(sh, ext)
        _shifted_copies(dsh, dext)
        stripe = 32

        def stripe_body(rb, carry):
            row0 = pl.multiple_of(rb * stripe, stripe)
            dco = dco_ref[pl.ds(row0, stripe), :]
            dhg = jnp.zeros((stripe, D_CONV), F32)
            for t in range(CONV_WIDTH):
                off = HALO - (CONV_WIDTH - 1) + t
                back = CONV_WIDTH - 1 - t
                prod = dco * sh[off % 8, pl.ds((off // 8) * 8 + row0, stripe), :]
                part = prod[0:8, :]
                for r8 in range(1, stripe // 8):
                    part = part + prod[8 * r8:8 * r8 + 8, :]
                dw_s[t] += part
                dhg = dhg + w3_ref[t] * dsh[back % 8, pl.ds((back // 8) * 8 + row0, stripe), :]
            dhg_s[pl.ds(row0, stripe), :] = dhg
            return carry

        lax.fori_loop(0, TILE // stripe, stripe_body, 0)

        @pl.when(i == nt - 1)
        def _():
            dw_ref[...] = jnp.sum(dw_s[...], axis=1)

        dhg = jnp.where(_row_mask(i, (TILE, D_CONV)), dhg_s[...], 0.0)
        sg = _sigmoid(ug_ref[...].astype(F32))
        du_ref[...] = (dhg * sg).astype(BF16)
        dug_ref[...] = (dhg * u_ref[...].astype(F32) * sg * (1.0 - sg)).astype(BF16)

    rm = jax.ShapeDtypeStruct((lp, D_CONV), BF16)
    nxt = pl.BlockSpec((TILE, 512), lambda i: (jnp.minimum(i + 1, nt - 1), 0))
    ext_t = pltpu.VMEM((EXT_ROWS, D_CONV), F32)
    sh_t = pltpu.VMEM((8, SHIFT_ROWS, D_CONV), F32)
    return pl.pallas_call(
        body, name="conv_bwd_taps", grid=(nt,),
        in_specs=[_row_spec(512), nxt, _row_spec(512), _row_spec(512), _row_spec(512, shift=True),
                  _row_spec(512, shift=True), _full_spec((32, 1, 512))],
        out_specs=[_row_spec(512), _row_spec(512), _full_spec((32, 512))],
        out_shape=[rm, rm, jax.ShapeDtypeStruct((32, D_CONV), F32)],
        scratch_shapes=[ext_t, ext_t, sh_t, sh_t, pltpu.VMEM((TILE, D_CONV), F32), pltpu.VMEM((32, 8, D_CONV), F32)],
        compiler_params=_cp(48, ("arbitrary",)),
    )(dco, dco, u, ug, u, ug, conv_w.reshape(32, 1, D_CONV))


def _cumsum_bwd(dck, dcq4, fl, bf_pad, nt):
    lp = nt * TILE

    def body(dck_ref, dcq_ref, fl_ref, bf_ref, dfl_ref, dbf_ref, carry):
        i = pl.program_id(0)
        tile = nt - 1 - i

        @pl.when(i == 0)
        def _():
            carry[...] = jnp.zeros_like(carry)
            dbf_ref[...] = jnp.zeros_like(dbf_ref)

        dc = jnp.zeros((TILE, LANES), F32)
        for p in range(N_HEADS // 2):
            dq_rows = jnp.concatenate([dcq_ref[p, 0], jnp.zeros((LANES - 8, TILE), F32)], axis=0)
            both = dck_ref[p] + dq_rows.T
            dc = dc + (both if p == 0 else pltpu.roll(both, 2 * p, 1))
        r = lax.broadcasted_iota(jnp.int32, (TILE, TILE), 0)
        c = lax.broadcasted_iota(jnp.int32, (TILE, TILE), 1)
        triu = (c >= r).astype(F32)
        dlf = jnp.dot(triu, dc, precision=lax.Precision.HIGHEST, preferred_element_type=F32) + carry[...]
        carry[...] = dlf[0:1, :]
        z = fl_ref[...] + bf_ref[...]
        lane = lax.broadcasted_iota(jnp.int32, (TILE, LANES), 1)
        dfl = jnp.where(_row_mask(tile, (TILE, LANES)) & (lane < N_HEADS), dlf * _sigmoid(-z), 0.0)
        dfl_ref[...] = dfl.astype(BF16)
        dbf_ref[...] += jnp.sum(dfl, axis=0, keepdims=True)

    rev = lambda i: (nt - 1 - i, 0)
    return pl.pallas_call(
        body, name="cumsum_bwd", grid=(nt,),
        in_specs=[pl.BlockSpec((N_HEADS // 2, TILE, LANES), lambda i: (0, nt - 1 - i, 0)),
                  pl.BlockSpec((N_HEADS // 2, 1, 8, TILE), lambda i: (0, nt - 1 - i, 0, 0)),
                  pl.BlockSpec((TILE, LANES), rev), _full_spec((1, LANES))],
        out_specs=[pl.BlockSpec((TILE, LANES), rev), _full_spec((1, LANES))],
        out_shape=[jax.ShapeDtypeStruct((lp, LANES), BF16), jax.ShapeDtypeStruct((1, LANES), F32)],
        scratch_shapes=[pltpu.VMEM((1, LANES), F32)],
        compiler_params=_cp(32, ("arbitrary",)),
    )(dck, dcq4, fl, bf_pad)


def _dw_rowmajor(hb, secs, nt):
    n = len(secs)

    def body(*refs):
        hb_ref, sec_refs, out_refs = refs[0], refs[1:1 + n], refs[1 + n:]
        i = pl.program_id(0)

        @pl.when(i == 0)
        def _():
            for o_ref in out_refs:
                o_ref[...] = jnp.zeros_like(o_ref)

        hb_t = hb_ref[...]
        for s_ref, o_ref in zip(sec_refs, out_refs):
            o_ref[...] += lax.dot_general(hb_t, s_ref[...], (((0,), (0,)), ((), ())), preferred_element_type=F32)

    return pl.pallas_call(
        body, name="dw_rowmajor", grid=(nt,),
        in_specs=[_row_spec(D_MODEL)] + [_row_spec(s.shape[1]) for s in secs],
        out_specs=[_full_spec((D_MODEL, s.shape[1])) for s in secs],
        out_shape=[jax.ShapeDtypeStruct((D_MODEL, s.shape[1]), F32) for s in secs],
        compiler_params=_cp(48, ("arbitrary",)),
    )(hb, *secs)


def _dw_transposed(hb, secs_t3, nt):
    n = len(secs_t3)

    def body(*refs):
        hb_ref, sec_refs, out_refs = refs[0], refs[1:1 + n], refs[1 + n:]
        i = pl.program_id(0)

        @pl.when(i == 0)
        def _():
            for o_ref in out_refs:
                o_ref[...] = jnp.zeros_like(o_ref)

        hb_t = hb_ref[...]
        for s_ref, o_ref in zip(sec_refs, out_refs):
            o_ref[...] += jnp.dot(s_ref[0], hb_t, preferred_element_type=F32)

    return pl.pallas_call(
        body, name="dw_transposed", grid=(nt,),
        in_specs=[_row_spec(D_MODEL)] + [_t3_spec(512) for _ in secs_t3],
        out_specs=[_full_spec((512, D_MODEL)) for _ in secs_t3],
        out_shape=[jax.ShapeDtypeStruct((512, D_MODEL), F32) for _ in secs_t3],
        compiler_params=_cp(40, ("arbitrary",)),
    )(hb, *secs_t3)


def _dh_bwd(secs, secs_t3, w_rm, w_t, dz, x, metapad, g_in, nt):
    n, m = len(secs), len(secs_t3)
    offs = OFF_GA_R + np.cumsum([0] + [s.shape[1] for s in secs])

    def body(*refs):
        sec_refs, t3_refs = refs[:n], refs[n:n + m]
        wrm_ref, wt_ref, dz_ref, x_ref, mp_ref, g_ref = refs[n + m:n + m + 6]
        dx_ref, dmeta_ref, dg_ref, db_ref = refs[n + m + 6:]
        i = pl.program_id(0)

        @pl.when(i == 0)
        def _():
            dg_ref[...] = jnp.zeros_like(dg_ref)
            db_ref[...] = jnp.zeros_like(db_ref)

        dh = ALPHA * dz_ref[...]
        for s_ref, lo, hi in zip(sec_refs, offs[:-1], offs[1:]):
            dh = dh + lax.dot_general(s_ref[...], wrm_ref[:, lo:hi], (((1,), (1,)), ((), ())),
                                      preferred_element_type=F32)
        for idx, t_ref in enumerate(t3_refs):
            dh = dh + lax.dot_general(t_ref[0], wt_ref[idx * 512:(idx + 1) * 512, :], (((0,), (0,)), ((), ())),
                                      preferred_element_type=F32)
        x0 = jnp.where(i == 0, mp_ref[...], x_ref[...])
        xhat, rstd = _ln_stats(x0)
        dg_ref[...] += jnp.sum(dh * xhat, axis=0, keepdims=True)
        db_ref[...] += jnp.sum(dh, axis=0, keepdims=True)
        dx = _ln_bwd(dh, xhat, rstd, g_ref[...])
        dx_ref[...] = dx

        @pl.when(i == 0)
        def _():
            dmeta_ref[...] = dx

    seq = (nt - 1) * TILE
    return pl.pallas_call(
        body, name="dh_bwd", grid=(nt,),
        in_specs=[_row_spec(s.shape[1]) for s in secs] + [_t3_spec(512) for _ in secs_t3]
        + [_full_spec(w_rm.shape), _full_spec(w_t.shape), _row_spec(D_MODEL), _row_spec(D_MODEL, shift=True),
           _full_spec((TILE, D_MODEL)), _full_spec((1, D_MODEL))],
        out_specs=[_row_spec(D_MODEL, shift=True), _full_spec((TILE, D_MODEL)), _full_spec((1, D_MODEL)),
                   _full_spec((1, D_MODEL))],
        out_shape=[jax.ShapeDtypeStruct((seq, D_MODEL), F32), jax.ShapeDtypeStruct((TILE, D_MODEL), F32),
                   jax.ShapeDtypeStruct((1, D_MODEL), F32), jax.ShapeDtypeStruct((1, D_MODEL), F32)],
        compiler_params=_cp(56, ("arbitrary",)),
    )(*secs, *secs_t3, w_rm, w_t, dz, x, metapad, g_in)


RB = 256
SMALL_ROWS = 48


def _repack_weights(all_in, all_small):
    n_cw = D_CONV // N_DEV

    def body(a_ref, s_ref, wr_ref, wt_ref, mp_ref, cw_ref):
        full = jnp.concatenate([a_ref[d].T[:, :SHARD_IN] for d in range(N_DEV)], axis=1)
        qkv = full[:, :1536]
        wr_ref[:, :1536] = qkv
        wr_ref[:, 1536:OFF_F_R] = full[:, 1544:]
        wr_ref[:, OFF_F_R:] = jnp.concatenate([full[:, 1536:1544], jnp.zeros((RB, LANES - N_HEADS), BF16)], axis=1)
        wt_ref[...] = qkv.T

        @pl.when(pl.program_id(0) == 0)
        def _():
            mp_ref[0:PAD, :] = jnp.zeros((PAD, D_MODEL), F32)
            mp_ref[PAD:, :] = jnp.concatenate([s_ref[d, 0:N_META, :] for d in range(N_DEV)], axis=1)
            cw_ref[...] = jnp.concatenate([s_ref[d, N_META:, 0:n_cw] for d in range(N_DEV)], axis=1)

    return pl.pallas_call(
        body, name="repack_weights", grid=(D_MODEL // RB,),
        in_specs=[pl.BlockSpec((N_DEV, 512, RB), lambda i: (0, 0, i)), _full_spec((N_DEV, SMALL_ROWS, LANES))],
        out_specs=[pl.BlockSpec((RB, W_COLS), lambda i: (i, 0)), pl.BlockSpec((1536, RB), lambda i: (0, i)),
                   _full_spec((TILE, D_MODEL)), _full_spec((32, D_CONV))],
        out_shape=[jax.ShapeDtypeStruct((D_MODEL, W_COLS), BF16), jax.ShapeDtypeStruct((1536, D_MODEL), BF16),
                   jax.ShapeDtypeStruct((TILE, D_MODEL), F32), jax.ShapeDtypeStruct((32, D_CONV), F32)],
        compiler_params=_cp(40, ("arbitrary",)),
    )(all_in, all_small)


def _unpack_dw_in(dw_rm, dw_t):
    def body(dga_ref, du_ref, dug_ref, dgc_ref, dfl_ref, dq_ref, dk_ref, dv_ref, out_ref):
        full = jnp.concatenate([dq_ref[...].T, dk_ref[...].T, dv_ref[...].T, dfl_ref[:, 0:N_HEADS], dga_ref[...],
                                du_ref[...], dug_ref[...], dgc_ref[...]], axis=1)
        pad = jnp.zeros((RB, 512 - SHARD_IN), F32)
        for d in range(N_DEV):
            out_ref[d] = jnp.concatenate([full[:, SHARD_IN * d:SHARD_IN * (d + 1)], pad], axis=1).T

    rm = pl.BlockSpec((RB, 512), lambda i: (i, 0))
    tr = pl.BlockSpec((512, RB), lambda i: (0, i))
    return pl.pallas_call(
        body, name="unpack_dw_in", grid=(D_MODEL // RB,),
        in_specs=[rm, rm, rm, rm, pl.BlockSpec((RB, LANES), lambda i: (i, 0)), tr, tr, tr],
        out_specs=pl.BlockSpec((N_DEV, 512, RB), lambda i: (0, 0, i)),
        out_shape=jax.ShapeDtypeStruct((N_DEV, 512, D_MODEL), F32),
        compiler_params=_cp(48, ("arbitrary",)),
    )(*dw_rm, *dw_t)


def _local_step(x, target, metapad, cw, w_r, w_t, w_pw_full, w_out_full, ln_in_g, ln_in_b, b_f, conv_b, ln_conv_g,
                ln_conv_b, ln_out_g, ln_out_b):
    seq = x.shape[0]
    nt = seq // TILE + 1
    row = lambda a: a.reshape(1, -1).astype(F32)
    bf_pad = jnp.pad(row(b_f), ((0, 0), (0, LANES - N_HEADS)))
    g_in, b_in = row(ln_in_g), row(ln_in_b)
    g_cv, b_cv, c_b = row(ln_conv_g), row(ln_conv_b), row(conv_b)
    g_out, b_out = row(ln_out_g), row(ln_out_b)

    hb, qT3, kT3, vT3, k, v, ga, u, ug, gc, fl = _proj_fwd(x, metapad, g_in, b_in, w_r, nt)
    kx3 = _cumsum_fwd(fl, bf_pad, nt)
    oT3, o, lse4 = _attn_fwd(qT3, k, kx3, vT3, nt)
    co, hc, pw = _conv_fwd(u, ug, cw, c_b, g_cv, b_cv, w_pw_full, nt)
    y, dz, loss, dg_out, db_out = _out_fwd(o, ga, pw, gc, x, metapad, g_in, b_in, w_out_full, g_out, b_out,
                                            target, nt)
    doT3, dga, dgc, dw_out, dco, dw_pw, dg_cv, db_cv, dc_b = _out_bwd(dz, y, o, ga, pw, gc, w_out_full, hc, co,
                                                                      w_pw_full, g_cv, b_cv, nt)
    du, dug, dcw = _conv_bwd_taps(dco, u, ug, cw, nt)
    dqT3, dkT3, dvT3, dck, dcq4 = _attn_bwd(qT3, kT3, k, kx3, v, oT3, doT3, lse4, nt)
    dfl, dbf = _cumsum_bwd(dck, dcq4, fl, bf_pad, nt)
    secs = (dga, du, dug, dgc, dfl)
    secs_t3 = (dqT3, dkT3, dvT3)
    dw_rm = _dw_rowmajor(hb, secs, nt)
    dw_t = _dw_transposed(hb, secs_t3, nt)
    grad_x, dmetapad, dg_in, db_in = _dh_bwd(secs, secs_t3, w_r, w_t, dz, x, metapad, g_in, nt)
    pieces = dict(loss=loss, metapad=dmetapad, ln_in_g=dg_in, ln_in_b=db_in, w_in_rm=dw_rm, w_in_t=dw_t, b_f=dbf,
                  conv_w=dcw, conv_b=dc_b, ln_conv_g=dg_cv, ln_conv_b=db_cv, w_pw=dw_pw, w_out=dw_out,
                  ln_out_g=dg_out, ln_out_b=db_out)
    return grad_x, pieces


MESH = pl.DeviceIdType.MESH
ANY = pl.BlockSpec(memory_space=pl.ANY)


def _mesh_pos():
    return lax.axis_index("x"), lax.axis_index("y"), lax.axis_index("c")


GATHER_SEMS = 8


def _gather_body(x_refs, out_refs, send_sems, recv_sems, local_sems):
    n = len(x_refs)
    x, y, c = _mesh_pos()
    me, sibling = (x, y, c), (x, y, 1 - c)
    xn, yn, dg = (1 - x, y), (x, 1 - y), (1 - x, 1 - y)

    def slot(a, px, py, pc, half=None):
        blk = out_refs[a].at[4 * px + 2 * py + pc]
        if half is None:
            return blk
        rows = blk.shape[0] // 2
        return blk.at[pl.ds(half * rows, rows)]

    def copy(a, k, block, to, src=None, half=None):
        return pltpu.make_async_remote_copy(
            src_ref=slot(a, *block, half) if src is None else src, dst_ref=slot(a, *block, half),
            send_sem=send_sems.at[GATHER_SEMS * a + k], recv_sem=recv_sems.at[GATHER_SEMS * a + k], device_id=to,
            device_id_type=MESH)

    arrays = range(n)
    mine = [pltpu.make_async_copy(x_refs[a], slot(a, *me), local_sems.at[a]) for a in arrays]
    for cp in mine:
        cp.start()
    sent = []
    for a in arrays:
        sent += [copy(a, 0, me, sibling, src=x_refs[a]), copy(a, 1, me, (*xn, c), src=x_refs[a]),
                 copy(a, 2, me, (*yn, c), src=x_refs[a])]
    for cp in sent:
        cp.start()

    def also(cp):
        cp.start()
        sent.append(cp)

    for a in arrays:
        copy(a, 2, (*yn, c), me).wait_recv()
        also(copy(a, 3, (*yn, c), (*xn, c), half=0))
        also(copy(a, 6, (*yn, c), sibling))
        copy(a, 1, (*xn, c), me).wait_recv()
        also(copy(a, 4, (*xn, c), (*yn, c), half=1))
        also(copy(a, 5, (*xn, c), sibling))
    for a in arrays:
        copy(a, 3, (*dg, c), me, half=0).wait_recv()
        copy(a, 4, (*dg, c), me, half=1).wait_recv()
        also(copy(a, 7, (*dg, c), sibling))
    for a in arrays:
        copy(a, 0, sibling, me).wait_recv()
        copy(a, 5, (*xn, 1 - c), me).wait_recv()
        copy(a, 6, (*yn, 1 - c), me).wait_recv()
        copy(a, 7, (*dg, 1 - c), me).wait_recv()
    for cp in sent:
        cp.wait_send()
    for cp in mine:
        cp.wait()


def _all_gather(blks, name):
    n = len(blks)

    def body(*refs):
        _gather_body(refs[:n], refs[n:2 * n], *refs[2 * n:])

    return pl.pallas_call(
        body, name=name, out_shape=[jax.ShapeDtypeStruct((N_DEV, *b.shape), b.dtype) for b in blks],
        in_specs=[ANY] * n, out_specs=[ANY] * n,
        scratch_shapes=[pltpu.SemaphoreType.DMA((GATHER_SEMS * n,)), pltpu.SemaphoreType.DMA((GATHER_SEMS * n,)),
                        pltpu.SemaphoreType.DMA((n,))],
    )(*blks)


def _exchange_sibling(g8s, small):
    n = len(g8s)

    def body(*refs):
        g_refs, s_ref, out_refs, a_ref = refs[:n], refs[n], refs[n + 1:2 * n + 1], refs[2 * n + 1]
        send_sems, recv_sems, a_send, a_recv, a_local = refs[2 * n + 2:]
        x, y, c = _mesh_pos()
        cps = [pltpu.make_async_remote_copy(
            src_ref=g_refs[a].at[2 * q + (1 - c)], dst_ref=out_refs[a].at[q], send_sem=send_sems.at[4 * a + q],
            recv_sem=recv_sems.at[4 * a + q], device_id=(x, y, 1 - c), device_id_type=MESH)
            for a in range(n) for q in range(4)]
        for cp in cps:
            cp.start()
        _gather_body([s_ref], [a_ref], a_send, a_recv, a_local)
        for cp in cps:
            cp.wait()

    outs = pl.pallas_call(
        body, name="rs_sibling",
        out_shape=[jax.ShapeDtypeStruct((4, *g.shape[1:]), g.dtype) for g in g8s]
        + [jax.ShapeDtypeStruct((N_DEV, *small.shape), small.dtype)],
        in_specs=[ANY] * (n + 1), out_specs=[ANY] * (n + 1),
        scratch_shapes=[pltpu.SemaphoreType.DMA((4 * n,)), pltpu.SemaphoreType.DMA((4 * n,)),
                        pltpu.SemaphoreType.DMA((GATHER_SEMS,)), pltpu.SemaphoreType.DMA((GATHER_SEMS,)),
                        pltpu.SemaphoreType.DMA((1,))],
    )(*g8s, small)
    return outs[:n], outs[n]


def _exchange_chips(p4s):
    n = len(p4s)

    def body(*refs):
        p_refs, out_refs, send_sems, recv_sems = refs[:n], refs[n:2 * n], refs[2 * n], refs[2 * n + 1]
        x, y, c = _mesh_pos()
        chips = [(1 - x, y), (x, 1 - y), (1 - x, 1 - y)]
        cps = [pltpu.make_async_remote_copy(
            src_ref=p_refs[a].at[2 * cx + cy], dst_ref=out_refs[a].at[k], send_sem=send_sems.at[3 * a + k],
            recv_sem=recv_sems.at[3 * a + k], device_id=(cx, cy, c), device_id_type=MESH)
            for k, (cx, cy) in enumerate(chips) for a in range(n)]
        for cp in cps:
            cp.start()
        for cp in cps:
            cp.wait()

    return pl.pallas_call(
        body, name="rs_chips", out_shape=[jax.ShapeDtypeStruct((3, *p.shape[1:]), p.dtype) for p in p4s],
        in_specs=[ANY] * n, out_specs=[ANY] * n,
        scratch_shapes=[pltpu.SemaphoreType.DMA((3 * n,)), pltpu.SemaphoreType.DMA((3 * n,))],
    )(*p4s)


def _rs_add_sibling(g8s, recvs, c_idx):
    n = len(g8s)

    def body(s_ref, *refs):
        g_refs, r_refs, p32_refs, pb_refs = (refs[k * n:(k + 1) * n] for k in range(4))
        for g_ref, r_ref, p32_ref, pb_ref in zip(g_refs, r_refs, p32_refs, pb_refs):
            p = g_ref[0] + r_ref[0]
            p32_ref[0] = p
            pb_ref[0] = p.astype(BF16)

    blk = lambda g: pl.BlockSpec((1, *g.shape[1:]), lambda q, s: (q, 0, 0))
    grid_spec = pltpu.PrefetchScalarGridSpec(
        num_scalar_prefetch=1, grid=(4,),
        in_specs=[pl.BlockSpec((1, *g.shape[1:]), lambda q, s: (2 * q + s[0], 0, 0)) for g in g8s]
        + [blk(g) for g in g8s],
        out_specs=[blk(g) for g in g8s] * 2)
    outs = pl.pallas_call(
        body, name="rs_add_sibling", grid_spec=grid_spec,
        out_shape=[jax.ShapeDtypeStruct((4, *g.shape[1:]), F32) for g in g8s]
        + [jax.ShapeDtypeStruct((4, *g.shape[1:]), BF16) for g in g8s],
        compiler_params=_cp(48, ("arbitrary",)),
    )(c_idx, *g8s, *recvs)
    return outs[:n], outs[n:]


def _rs_add_chips(p32s, recvs, q_idx):
    def body(s_ref, pin_ref, pout_ref, ppw_ref, rin_ref, rout_ref, rpw_ref, gin_ref, gout_ref, gpw_ref):
        def total(p_ref, r_ref):
            return ((p_ref[0] + r_ref[0].astype(F32)) + r_ref[1].astype(F32)) + r_ref[2].astype(F32)

        gin_ref[...] = total(pin_ref, rin_ref)[:SHARD_IN, :]
        gout_ref[0] = total(pout_ref, rout_ref)
        gpw_ref[0] = total(ppw_ref, rpw_ref)

    own = lambda p: pl.BlockSpec((1, *p.shape[1:]), lambda i, s: (s[0], 0, 0))
    whole = lambda shape: pl.BlockSpec(shape, lambda i, s: (0,) * len(shape))
    out_shapes = [(SHARD_IN, D_MODEL), (1, *p32s[1].shape[1:]), (1, *p32s[2].shape[1:])]
    grid_spec = pltpu.PrefetchScalarGridSpec(
        num_scalar_prefetch=1, grid=(1,),
        in_specs=[own(p) for p in p32s] + [whole(r.shape) for r in recvs],
        out_specs=[whole(s) for s in out_shapes])
    return pl.pallas_call(
        body, name="rs_add_chips", grid_spec=grid_spec,
        out_shape=[jax.ShapeDtypeStruct(s, F32) for s in out_shapes],
        compiler_params=_cp(48, ("arbitrary",)),
    )(q_idx, *p32s, *recvs)


SMALL_ROWS_G = 64
SMALL_LAYOUT = {
    "metapad": (0, N_META, D_MODEL), "conv_w": (16, 32, D_CONV), "ln_in_g": (48, 1, D_MODEL),
    "ln_in_b": (49, 1, D_MODEL), "b_f": (50, 1, LANES), "conv_b": (51, 1, D_CONV), "ln_conv_g": (52, 1, D_CONV),
    "ln_conv_b": (53, 1, D_CONV), "ln_out_g": (54, 1, D_MODEL), "ln_out_b": (55, 1, D_MODEL), "loss": (56, 1, LANES)}


def _pack_small(pieces):
    names = list(SMALL_LAYOUT)

    def body(*refs):
        out_ref = refs[-1]
        out_ref[...] = jnp.zeros_like(out_ref)
        for name, ref in zip(names, refs[:-1]):
            r0, nr, nl = SMALL_LAYOUT[name]
            src = ref[PAD:, :] if name == "metapad" else ref[...]
            out_ref[r0:r0 + nr, 0:nl] = src

    return pl.pallas_call(body, name="pack_small", out_shape=jax.ShapeDtypeStruct((SMALL_ROWS_G, D_MODEL), F32),
                          compiler_params=_cp(16))(*[pieces[n] for n in names])


def _sum_small(gathered):
    names = list(SMALL_LAYOUT)

    def body(a_ref, *out_refs):
        acc = a_ref[0]
        for d in range(1, N_DEV):
            acc = acc + a_ref[d]
        for name, ref in zip(names, out_refs):
            r0, nr, nl = SMALL_LAYOUT[name]
            ref[...] = acc[r0:r0 + nr, 0:nl]

    outs = pl.pallas_call(
        body, name="sum_small",
        out_shape=[jax.ShapeDtypeStruct(SMALL_LAYOUT[n][1:], F32) for n in names], compiler_params=_cp(16))(gathered)
    return dict(zip(names, outs))


def _adamw(ws, gs, ms, vs):
    n = len(ws)
    c1 = 1.0 - ADAM_B1 ** ADAM_STEP
    c2 = 1.0 - ADAM_B2 ** ADAM_STEP

    def body(*refs):
        w_refs, g_refs, m_refs, v_refs = (refs[k * n:(k + 1) * n] for k in range(4))
        d_refs, nm_refs, nv_refs = (refs[(4 + k) * n:(5 + k) * n] for k in range(3))
        for w_ref, g_ref, m_ref, v_ref, d_ref, nm_ref, nv_ref in zip(w_refs, g_refs, m_refs, v_refs, d_refs,
                                                                     nm_refs, nv_refs):
            g = g_ref[...]
            m = ADAM_B1 * m_ref[...] + (1.0 - ADAM_B1) * g
            v = ADAM_B2 * v_ref[...] + (1.0 - ADAM_B2) * (g * g)
            nm_ref[...] = m
            nv_ref[...] = v
            d_ref[...] = -ADAM_LR * ((m / c1) / (jnp.sqrt(v / c2) + ADAM_EPS) + ADAM_WD * w_ref[...])

    shapes = [jax.ShapeDtypeStruct(w.shape, F32) for w in ws]
    outs = pl.pallas_call(body, name="adamw", out_shape=shapes * 3, compiler_params=_cp(48))(*ws, *gs, *ms, *vs)
    return outs[:n], outs[n:2 * n], outs[2 * n:]


W_NAMES = ("meta", "ln_in_g", "ln_in_b", "w_in", "b_f", "conv_w", "conv_b", "ln_conv_g", "ln_conv_b", "w_pw",
           "w_out", "ln_out_g", "ln_out_b")


def kernel(x, meta, ln_in_g, ln_in_b, w_in, b_f, conv_w, conv_b, ln_conv_g, ln_conv_b, w_pw, w_out, ln_out_g, ln_out_b, loss_target, m_meta, m_ln_in_g, m_ln_in_b, m_w_in, m_b_f, m_conv_w, m_conv_b, m_ln_conv_g, m_ln_conv_b, m_w_pw, m_w_out, m_ln_out_g, m_ln_out_b, v_meta, v_ln_in_g, v_ln_in_b, v_w_in, v_b_f, v_conv_w, v_conv_b, v_ln_conv_g, v_ln_conv_b, v_w_pw, v_w_out, v_ln_out_g, v_ln_out_b):
    mx, my, mc = _mesh_pos()
    me = 4 * mx + 2 * my + mc
    n_meta_sh = D_MODEL // N_DEV
    n_cw_sh = D_CONV // N_DEV
    n_out_sh = D_MODEL // N_DEV
    n_pw_sh = D_CONV // N_DEV

    small_w = jnp.concatenate([meta, jnp.pad(conv_w[0], ((0, 1), (0, LANES - n_cw_sh)))], axis=0)
    all_in, all_out, all_pw, all_small = _all_gather(
        [jnp.pad(w_in[0].T, ((0, 512 - SHARD_IN), (0, 0))).astype(BF16), w_out[0].astype(BF16), w_pw[0].astype(BF16),
         small_w], "gather_weights")
    w_r, w_t, metapad, cw = _repack_weights(all_in, all_small)
    w_out_full = all_out.reshape(D_MODEL, D_MODEL)
    w_pw_full = all_pw.reshape(D_CONV, D_CONV)

    grad_x, pc = _local_step(x[0], loss_target[0], metapad, cw, w_r, w_t, w_pw_full, w_out_full, ln_in_g, ln_in_b,
                             b_f[0], conv_b[0], ln_conv_g[0], ln_conv_b[0], ln_out_g[0], ln_out_b[0])

    g8s = [_unpack_dw_in(pc["w_in_rm"], pc["w_in_t"]), pc["w_out"].reshape(N_DEV, n_out_sh, D_MODEL),
           pc["w_pw"].reshape(N_DEV, n_pw_sh, D_CONV)]
    from_sibling, all_small_g = _exchange_sibling(g8s, _pack_small(pc))
    p32s, pbs = _rs_add_sibling(g8s, from_sibling, jnp.reshape(mc, (1,)).astype(jnp.int32))
    from_chips = _exchange_chips(pbs)
    g_w_in, g_w_out, g_w_pw = _rs_add_chips(p32s, from_chips, jnp.reshape(2 * mx + my, (1,)).astype(jnp.int32))

    sm = _sum_small(all_small_g)
    grads = {
        "meta": lax.dynamic_slice_in_dim(sm["metapad"], me * n_meta_sh, n_meta_sh, axis=1),
        "ln_in_g": sm["ln_in_g"].reshape(D_MODEL), "ln_in_b": sm["ln_in_b"].reshape(D_MODEL), "w_in": g_w_in.T[None],
        "b_f": sm["b_f"][:, :N_HEADS],
        "conv_w": lax.dynamic_slice_in_dim(sm["conv_w"], me * n_cw_sh, n_cw_sh, axis=1)[None, :CONV_WIDTH],
        "conv_b": sm["conv_b"], "ln_conv_g": sm["ln_conv_g"], "ln_conv_b": sm["ln_conv_b"], "w_pw": g_w_pw,
        "w_out": g_w_out, "ln_out_g": sm["ln_out_g"], "ln_out_b": sm["ln_out_b"]}
    loss_all = sm["loss"][0, 0]

    weights = dict(meta=meta, ln_in_g=ln_in_g, ln_in_b=ln_in_b, w_in=w_in, b_f=b_f, conv_w=conv_w, conv_b=conv_b,
                   ln_conv_g=ln_conv_g, ln_conv_b=ln_conv_b, w_pw=w_pw, w_out=w_out, ln_out_g=ln_out_g,
                   ln_out_b=ln_out_b)
    moms = dict(meta=m_meta, ln_in_g=m_ln_in_g, ln_in_b=m_ln_in_b, w_in=m_w_in, b_f=m_b_f, conv_w=m_conv_w,
                conv_b=m_conv_b, ln_conv_g=m_ln_conv_g, ln_conv_b=m_ln_conv_b, w_pw=m_w_pw, w_out=m_w_out,
                ln_out_g=m_ln_out_g, ln_out_b=m_ln_out_b)
    vels = dict(meta=v_meta, ln_in_g=v_ln_in_g, ln_in_b=v_ln_in_b, w_in=v_w_in, b_f=v_b_f, conv_w=v_conv_w,
                conv_b=v_conv_b, ln_conv_g=v_ln_conv_g, ln_conv_b=v_ln_conv_b, w_pw=v_w_pw, w_out=v_w_out,
                ln_out_g=v_ln_out_g, ln_out_b=v_ln_out_b)

    def to_kernel(name, a):
        if name == "w_in":
            return a[0].T
        return a.reshape(1, -1) if a.ndim == 1 else a

    def from_kernel(name, a):
        return a.T[None] if name == "w_in" else a.reshape(weights[name].shape)

    upd = _adamw(*[[to_kernel(n, d[n]) for n in W_NAMES] for d in (weights, grads, moms, vels)])
    deltas, new_m, new_v = ([from_kernel(n, a) for n, a in zip(W_NAMES, part)] for part in upd)
    return (loss_all, grad_x[None], *[grads[n] for n in W_NAMES], *deltas, *new_m, *new_v)
```

```python
import jax
import jax.numpy as jnp
import numpy as np
from jax import lax
from jax.experimental import pallas as pl
from jax.experimental.pallas import tpu as pltpu

F32 = jnp.float32
BF16 = jnp.bfloat16

D_MODEL = 1024
D_ATTN = 512
D_CONV = 512
N_HEADS = 8
HEAD_DIM = 64
N_META = 16
CONV_WIDTH = 31
LN_EPS = 1e-5
ALPHA = 2.0 ** 0.25
SCALE = HEAD_DIM ** -0.5
LOG2E = 1.4426950408889634
ADAM_LR, ADAM_B1, ADAM_B2, ADAM_EPS, ADAM_WD, ADAM_STEP = 0.001, 0.9, 0.999, 1e-08, 0.01, 10

N_DEV = 8
D_IN = 3592
SHARD_IN = D_IN // N_DEV
TILE = 256
PAD = TILE - N_META
HALO = 32
SHIFT_ROWS = TILE + HALO
EXT_ROWS = SHIFT_ROWS + 8
NEG = -1e30
LANES = 128
W_COLS = 7 * 512 + LANES
OFF_GA_R, OFF_F_R = 1536, 3584
MIB = 1024 * 1024


def _cp(vmem_mib, sem=None):
    kw = dict(vmem_limit_bytes=vmem_mib * MIB)
    if sem is not None:
        kw["dimension_semantics"] = sem
    return pltpu.CompilerParams(**kw)


def _sigmoid(x):
    return 1.0 / (1.0 + jnp.exp(-x))


def _silu_and_grad(x):
    s = _sigmoid(x)
    return x * s, s * (1.0 + x * (1.0 - s))


def _ln_stats(x):
    mu = jnp.mean(x, axis=-1, keepdims=True)
    xc = x - mu
    var = jnp.mean(xc * xc, axis=-1, keepdims=True)
    rstd = lax.rsqrt(var + LN_EPS)
    return xc * rstd, rstd


def _ln_bwd(dy, xhat, rstd, g):
    dxh = dy * g
    m1 = jnp.mean(dxh, axis=-1, keepdims=True)
    m2 = jnp.mean(dxh * xhat, axis=-1, keepdims=True)
    return rstd * (dxh - m1 - xhat * m2)


def _row_spec(cols, shift=False):
    if shift:
        return pl.BlockSpec((TILE, cols), lambda i: (jnp.maximum(i - 1, 0), 0))
    return pl.BlockSpec((TILE, cols), lambda i: (i, 0))


def _full_spec(shape):
    nd = len(shape)
    return pl.BlockSpec(shape, lambda i: (0,) * nd)


def _t3_spec(ch):
    return pl.BlockSpec((1, ch, TILE), lambda i: (i, 0, 0))


def _proj_fwd(x, metapad, g_in, b_in, w_r, nt):
    lp = nt * TILE

    def body(x_ref, mp_ref, g_ref, b_ref, w_ref, hb_ref, qT_ref, kT_ref, vT_ref, k_ref, v_ref,
             ga_ref, u_ref, ug_ref, gc_ref, fl_ref):
        i = pl.program_id(0)
        x0 = jnp.where(i == 0, mp_ref[...], x_ref[...])
        xhat, _ = _ln_stats(x0)
        hb = (xhat * g_ref[...] + b_ref[...]).astype(BF16)
        hb_ref[...] = hb

        def sec(off, n=512):
            return jnp.dot(hb, w_ref[:, off:off + n], preferred_element_type=F32)

        qT_ref[0] = (sec(0) * (SCALE * LOG2E)).T.astype(BF16)
        k = sec(512)
        kT_ref[0] = k.T.astype(BF16)
        k_ref[...] = k.astype(BF16)
        v = sec(1024)
        vT_ref[0] = v.T.astype(BF16)
        v_ref[...] = v.astype(BF16)
        ga_ref[...] = sec(OFF_GA_R).astype(BF16)
        u_ref[...] = sec(OFF_GA_R + 512).astype(BF16)
        ug_ref[...] = sec(OFF_GA_R + 1024).astype(BF16)
        gc_ref[...] = sec(OFF_GA_R + 1536).astype(BF16)
        fl_ref[...] = sec(OFF_F_R, LANES)

    t3 = jax.ShapeDtypeStruct((nt, 512, TILE), BF16)
    rm = lambda dt: jax.ShapeDtypeStruct((lp, 512), dt)
    return pl.pallas_call(
        body, name="proj_fwd", grid=(nt,),
        in_specs=[_row_spec(D_MODEL, shift=True), _full_spec((TILE, D_MODEL)), _full_spec((1, D_MODEL)),
                  _full_spec((1, D_MODEL)), _full_spec((D_MODEL, W_COLS))],
        out_specs=[_row_spec(D_MODEL), _t3_spec(512), _t3_spec(512), _t3_spec(512), _row_spec(512), _row_spec(512),
                   _row_spec(512), _row_spec(512), _row_spec(512), _row_spec(512), _row_spec(LANES)],
        out_shape=[jax.ShapeDtypeStruct((lp, D_MODEL), BF16), t3, t3, t3, rm(BF16), rm(BF16),
                   rm(BF16), rm(BF16), rm(BF16), rm(BF16), jax.ShapeDtypeStruct((lp, LANES), F32)],
        compiler_params=_cp(56, ("arbitrary",)),
    )(x, metapad, g_in, b_in, w_r)


def _row_mask(i, shape):
    r = lax.broadcasted_iota(jnp.int32, shape, 0)
    return (r >= PAD) | (i > 0)


def _cumsum_fwd(fl, bf_pad, nt):
    lp = nt * TILE

    def body(fl_ref, bf_ref, kx_ref, carry):
        i = pl.program_id(0)

        @pl.when(i == 0)
        def _():
            carry[...] = jnp.zeros_like(carry)

        z = fl_ref[...] + bf_ref[...]
        lf = jnp.minimum(z, 0.0) - jnp.log(1.0 + jnp.exp(-jnp.abs(z)))
        lane = lax.broadcasted_iota(jnp.int32, (TILE, LANES), 1)
        real = _row_mask(i, (TILE, LANES))
        lf = jnp.where(real & (lane < N_HEADS), lf, 0.0)
        r = lax.broadcasted_iota(jnp.int32, (TILE, TILE), 0)
        c = lax.broadcasted_iota(jnp.int32, (TILE, TILE), 1)
        tril = (c <= r).astype(F32)
        cs = jnp.dot(tril, lf, precision=lax.Precision.HIGHEST, preferred_element_type=F32) + carry[...]
        carry[...] = cs[TILE - 1:TILE, :]
        bias = jnp.where(real, cs * (-LOG2E), NEG)
        hi = bias.astype(BF16).astype(F32)
        mid = (bias - hi).astype(BF16).astype(F32)
        lo = (bias - hi - mid).astype(BF16).astype(F32)
        for p in range(N_HEADS // 2):
            out = jnp.zeros((TILE, LANES), F32)
            for hh in range(2):
                for part, piece in enumerate((hi, mid, lo)):
                    dst, src = 3 * hh + part, 2 * p + hh
                    moved = piece if dst == src else pltpu.roll(piece, (dst - src) % LANES, 1)
                    out = jnp.where(lane == dst, moved, out)
            kx_ref[p] = out.astype(BF16)

    return pl.pallas_call(
        body, name="cumsum_fwd", grid=(nt,),
        in_specs=[_row_spec(LANES), _full_spec((1, LANES))],
        out_specs=pl.BlockSpec((N_HEADS // 2, TILE, LANES), lambda i: (0, i, 0)),
        out_shape=jax.ShapeDtypeStruct((N_HEADS // 2, lp, LANES), BF16),
        scratch_shapes=[pltpu.VMEM((1, LANES), F32)],
        compiler_params=_cp(32, ("arbitrary",)),
    )(fl, bf_pad)


def _head_rows(blk, hh):
    r = lax.broadcasted_iota(jnp.int32, blk.shape, 0)
    return jnp.where((r >= hh * HEAD_DIM) & (r < (hh + 1) * HEAD_DIM), blk, jnp.zeros_like(blk))


def _two_heads(blk):
    return jnp.concatenate([_head_rows(blk, 0), _head_rows(blk, 1)], axis=1)


def _bias_rows():
    r = lax.broadcasted_iota(jnp.int32, (LANES, 2 * TILE), 0)
    c = lax.broadcasted_iota(jnp.int32, (LANES, 2 * TILE), 1)
    return jnp.where(((r < 3) & (c < TILE)) | ((r >= 3) & (r < 6) & (c >= TILE)), 1.0, 0.0).astype(BF16)


def _diag_mask(s):
    kpos = lax.broadcasted_iota(jnp.int32, (TILE, TILE), 0)
    qpos = lax.broadcasted_iota(jnp.int32, (TILE, TILE), 1)
    return jnp.where(kpos <= qpos, s, NEG)


def _stream(n, first, nxt, scores, update):
    if n == 0:
        return
    scores(first, 0)

    def pair_body(_, idx):
        idx_b = nxt(idx)
        scores(idx_b, 1)
        update(idx, 0)
        idx_c = nxt(idx_b)
        scores(idx_c, 0)
        update(idx_b, 1)
        return idx_c

    idx = lax.fori_loop(0, (n - 1) // 2, pair_body, first)
    if n % 2 == 1:
        update(idx, 0)
    else:
        idx_b = nxt(idx)
        scores(idx_b, 1)
        update(idx, 0)
        update(idx_b, 1)


def _next_below_diagonal(idx):
    i, j = idx
    wrap = j + 1 >= i
    return jnp.where(wrap, i + 1, i), jnp.where(wrap, 0, j + 1)


def _tile_rows(t):
    return pl.ds(pl.multiple_of(t * TILE, TILE), TILE)


def _two_streams(nt):
    load, group = [0, 0], {}
    for i in sorted(range(1, nt), reverse=True):
        g = 0 if load[0] <= load[1] else 1
        group[i] = g
        load[g] += i
    rows = [[(i, i, j) for i in range(1, nt) if group[i] == g for j in range(i)] for g in range(2)]
    length = max(len(r) for r in rows)
    rows = [r + [(nt, 0, 0)] * (length - len(r)) for r in rows]
    return group, np.asarray(rows, np.int32).reshape(2, -1), length


def _attn_fwd(qT3, k, kx3, vT3, nt):
    lp = nt * TILE
    npair = N_HEADS // 2
    group, table, n_stream = _two_streams(nt)

    def body(tab_ref, qT_ref, k_ref, kx_ref, vT_ref, oT_ref, o_ref, lse_ref, sbuf, m_0, l_0, acc_0, m_1, l_1, acc_1):
        ones = _bias_rows()
        states = ((m_0, l_0, acc_0), (m_1, l_1, acc_1))

        def scores(i, j, slot):
            qcat = jnp.concatenate([_two_heads(qT_ref[i]), ones], axis=0)
            kext = jnp.concatenate([k_ref[_tile_rows(j), :], kx_ref[0, _tile_rows(j), :]], axis=1)
            sbuf[slot] = jnp.dot(kext, qcat, preferred_element_type=F32)

        def update(st, j, slot, state, diag):
            m_s, l_s, acc_s = state
            for hh in range(2):
                s = sbuf[slot, :, hh * TILE:(hh + 1) * TILE]
                vj = vT_ref[j, hh * HEAD_DIM:(hh + 1) * HEAD_DIM, :]
                if diag:
                    s = _diag_mask(s)
                    m_new = jnp.max(s, axis=0, keepdims=True)
                    p = jnp.exp2(s - m_new)
                    l_s[st, hh] = jnp.sum(p, axis=0, keepdims=True)
                    acc_s[st, hh] = jnp.dot(vj, p.astype(BF16), preferred_element_type=F32)
                else:
                    m_prev = m_s[st, hh]
                    m_new = jnp.maximum(m_prev, jnp.max(s, axis=0, keepdims=True))
                    a = jnp.exp2(m_prev - m_new)
                    p = jnp.exp2(s - m_new)
                    l_s[st, hh] = a * l_s[st, hh] + jnp.sum(p, axis=0, keepdims=True)
                    acc_s[st, hh] = a * acc_s[st, hh] + jnp.dot(vj, p.astype(BF16), preferred_element_type=F32)
                m_s[st, hh] = m_new

        _stream(nt, jnp.int32(0), lambda t: t + 1, lambda t, slot: scores(t, t, slot),
                lambda t, slot: update(t, t, slot, states[0], True))
        for dst, src in zip(states[1], states[0]):
            dst[0:nt] = src[0:nt]
        for m_s, l_s, acc_s in states:
            m_s[nt] = jnp.full(m_s.shape[1:], NEG, F32)
            l_s[nt] = jnp.zeros(l_s.shape[1:], F32)
            acc_s[nt] = jnp.zeros(acc_s.shape[1:], F32)

        def entry(g, t):
            return tab_ref[g, 3 * t], tab_ref[g, 3 * t + 1], tab_ref[g, 3 * t + 2]

        def scores2(t, slot):
            for g in range(2):
                _, qi, kj = entry(g, t)
                scores(qi, kj, 2 * g + slot)

        def update2(t, slot):
            for g in range(2):
                st, _, kj = entry(g, t)
                update(st, kj, 2 * g + slot, states[g], False)

        _stream(n_stream, jnp.int32(0), lambda t: t + 1, scores2, update2)

        for i in range(nt):
            m_s, l_s, acc_s = states[group.get(i, 0)]
            for hh in range(2):
                l = l_s[i, hh]
                oT_ref[i, hh * HEAD_DIM:(hh + 1) * HEAD_DIM, :] = acc_s[i, hh] / l
                lse_ref[0, i, hh:hh + 1, :] = m_s[i, hh] + jnp.log(l) * LOG2E
            o_ref[i * TILE:(i + 1) * TILE, :] = oT_ref[i].T.astype(BF16)

    blk_t = pl.BlockSpec((nt, LANES, TILE), lambda p, tab: (0, p, 0))
    blk_rm = pl.BlockSpec((lp, LANES), lambda p, tab: (0, p))
    blk_px = pl.BlockSpec((1, lp, LANES), lambda p, tab: (p, 0, 0))
    blk_st = pl.BlockSpec((1, nt, 8, TILE), lambda p, tab: (p, 0, 0, 0))
    state = [pltpu.VMEM((nt + 1, 2, 1, TILE), F32), pltpu.VMEM((nt + 1, 2, 1, TILE), F32),
             pltpu.VMEM((nt + 1, 2, HEAD_DIM, TILE), F32)]
    grid_spec = pltpu.PrefetchScalarGridSpec(
        num_scalar_prefetch=1, grid=(npair,), in_specs=[blk_t, blk_rm, blk_px, blk_t],
        out_specs=[blk_t, blk_rm, blk_st], scratch_shapes=[pltpu.VMEM((4, TILE, 2 * TILE), F32)] + state + state)
    return pl.pallas_call(
        body, name="attn_fwd", grid_spec=grid_spec,
        out_shape=[jax.ShapeDtypeStruct((nt, D_ATTN, TILE), F32),
                   jax.ShapeDtypeStruct((lp, D_ATTN), BF16),
                   jax.ShapeDtypeStruct((npair, nt, 8, TILE), F32)],
        compiler_params=_cp(60, ("arbitrary",)),
    )(jnp.asarray(table), qT3, k, kx3, vT3)


def _attn_bwd(qT3, kT3, k, kx3, v, oT3, doT3, lse4, nt):
    lp = nt * TILE
    npair = N_HEADS // 2

    def body(qT_ref, kT_ref, k_ref, kx_ref, v_ref, oT_ref, doT_ref, lse_ref,
             dqT_ref, dkT_ref, dvT_ref, dck_ref, dcq_ref, sbuf, dpbuf, dq_s, dk_s, dv_s, dc_s, tp_s, tds_s):
        ones = _bias_rows()

        def scores(idx, slot):
            i, j = idx
            qcat = jnp.concatenate([_two_heads(qT_ref[i]), ones], axis=0)
            kext = jnp.concatenate([k_ref[_tile_rows(j), :], kx_ref[0, _tile_rows(j), :]], axis=1)
            sbuf[slot] = jnp.dot(kext, qcat, preferred_element_type=F32)
            dpbuf[slot] = jnp.dot(v_ref[_tile_rows(j), :], _two_heads(doT_ref[i]), preferred_element_type=F32)

        def update(idx, slot, diag):
            i, j = idx
            for hh in range(2):
                hs = slice(hh * HEAD_DIM, (hh + 1) * HEAD_DIM)
                s = sbuf[slot, :, hh * TILE:(hh + 1) * TILE]
                if diag:
                    s = _diag_mask(s)
                p = jnp.exp2(s - lse_ref[0, i, hh:hh + 1, :])
                doh = doT_ref[i, hs, :]
                delta = jnp.sum(doh.astype(F32) * oT_ref[i, hs, :], axis=0, keepdims=True)
                ds = p * (dpbuf[slot, :, hh * TILE:(hh + 1) * TILE] - delta)
                dsb = ds.astype(BF16)
                tp_s[hh] = p.astype(BF16).T
                tds_s[hh] = dsb.T
                dv = jnp.dot(doh, tp_s[hh], preferred_element_type=F32)
                dk = jnp.dot(qT_ref[i, hs, :], tds_s[hh], preferred_element_type=F32)
                dq = jnp.dot(kT_ref[j, hs, :], dsb, preferred_element_type=F32)
                dc = ds[:, :LANES] + ds[:, LANES:]
                dcq = jnp.sum(ds, axis=0, keepdims=True)
                if diag:
                    dv_s[j, hh] = dv
                    dk_s[j, hh] = dk
                    dc_s[j, hh] = dc
                    dq_s[i, hs, :] = dq
                    dcq_ref[0, i, hh:hh + 1, :] = dcq
                else:
                    dv_s[j, hh] += dv
                    dk_s[j, hh] += dk
                    dc_s[j, hh] += dc
                    dq_s[i, hs, :] += dq
                    dcq_ref[0, i, hh:hh + 1, :] += dcq

        dcq_ref[...] = jnp.zeros_like(dcq_ref)
        zero = jnp.int32(0)
        _stream(nt, (zero, zero), lambda idx: (idx[0] + 1, idx[1] + 1), scores,
                lambda idx, slot: update(idx, slot, True))
        _stream(nt * (nt - 1) // 2, (zero + 1, zero), _next_below_diagonal, scores,
                lambda idx, slot: update(idx, slot, False))

        lane = lax.broadcasted_iota(jnp.int32, (TILE, LANES), 1)

        def finish(t, carry):
            dck = jnp.zeros((TILE, LANES), F32)
            for hh in range(2):
                hs = slice(hh * HEAD_DIM, (hh + 1) * HEAD_DIM)
                dkT_ref[t, hs, :] = (dk_s[t, hh] * (1.0 / LOG2E)).astype(BF16)
                dvT_ref[t, hs, :] = dv_s[t, hh].astype(BF16)
                dck = jnp.where(lane == hh, -jnp.sum(dc_s[t, hh], axis=1, keepdims=True), dck)
            dck_ref[0, _tile_rows(t), :] = dck
            dqT_ref[t] = (dq_s[t] * SCALE).astype(BF16)
            return carry

        lax.fori_loop(0, nt, finish, 0)

    blk_t = pl.BlockSpec((nt, LANES, TILE), lambda p: (0, p, 0))
    blk_rm = pl.BlockSpec((lp, LANES), lambda p: (0, p))
    blk_px = pl.BlockSpec((1, lp, LANES), lambda p: (p, 0, 0))
    blk_st = pl.BlockSpec((1, nt, 8, TILE), lambda p: (p, 0, 0, 0))
    t3 = jax.ShapeDtypeStruct((nt, D_ATTN, TILE), BF16)
    return pl.pallas_call(
        body, name="attn_bwd", grid=(npair,),
        in_specs=[blk_t, blk_t, blk_rm, blk_px, blk_rm, blk_t, blk_t, blk_st],
        out_specs=[blk_t, blk_t, blk_t, blk_px, blk_st],
        out_shape=[t3, t3, t3, jax.ShapeDtypeStruct((npair, lp, LANES), F32),
                   jax.ShapeDtypeStruct((npair, nt, 8, TILE), F32)],
        scratch_shapes=[pltpu.VMEM((2, TILE, 2 * TILE), F32), pltpu.VMEM((2, TILE, 2 * TILE), F32),
                        pltpu.VMEM((nt, LANES, TILE), F32), pltpu.VMEM((nt, 2, HEAD_DIM, TILE), F32),
                        pltpu.VMEM((nt, 2, HEAD_DIM, TILE), F32), pltpu.VMEM((nt, 2, TILE, LANES), F32),
                        pltpu.VMEM((2, TILE, TILE), BF16), pltpu.VMEM((2, TILE, TILE), BF16)],
        compiler_params=_cp(60, ("arbitrary",)),
    )(qT3, kT3, k, kx3, v, oT3, doT3, lse4)


def _glu(u, ug, i):
    return jnp.where(_row_mask(i, u.shape), u.astype(F32) * _sigmoid(ug.astype(F32)), 0.0)


def _shifted_copies(dst, src):
    for ph in range(8):
        dst[ph] = src[ph:ph + SHIFT_ROWS, :]


def _tap_window(sh, off, lanes, row0=0, rows=TILE):
    base = (off // 8) * 8 + row0
    return sh[off % 8, base:base + rows, lanes]


def _conv_fwd(u, ug, conv_w, conv_b, g, b, w_pw, nt):
    lp = nt * TILE

    def body(u_ref, ug_ref, up_ref, ugp_ref, w_ref, cb_ref, g_ref, b_ref, wpw_ref,
             co_ref, hc_ref, pw_ref, ext, sh):
        i = pl.program_id(0)
        prev = _glu(up_ref[...], ugp_ref[...], i - 1)
        ext[0:HALO, :] = jnp.where(i > 0, prev[TILE - HALO:, :], 0.0)
        ext[HALO:HALO + TILE, :] = _glu(u_ref[...], ug_ref[...], i)
        ext[HALO + TILE:, :] = jnp.zeros((8, D_CONV), F32)
        _shifted_copies(sh, ext)
        for lb in range(D_CONV // LANES):
            lanes = slice(lb * LANES, (lb + 1) * LANES)
            acc = jnp.zeros((TILE, LANES), F32) + cb_ref[:, lanes]
            for t in range(CONV_WIDTH):
                off = HALO - (CONV_WIDTH - 1) + t
                acc = acc + w_ref[t:t + 1, lanes] * _tap_window(sh, off, lanes)
            co_ref[:, lanes] = acc
        xhat, _ = _ln_stats(co_ref[...])
        a, _ = _silu_and_grad(xhat * g_ref[...] + b_ref[...])
        hc = a.astype(BF16)
        hc_ref[...] = hc
        pw_ref[...] = jnp.dot(hc, wpw_ref[...], preferred_element_type=F32).astype(BF16)

    rm = lambda dt: jax.ShapeDtypeStruct((lp, D_CONV), dt)
    return pl.pallas_call(
        body, name="conv_fwd", grid=(nt,),
        in_specs=[_row_spec(512), _row_spec(512), _row_spec(512, shift=True), _row_spec(512, shift=True),
                  _full_spec((32, 512)), _full_spec((1, 512)), _full_spec((1, 512)), _full_spec((1, 512)),
                  _full_spec((512, 512))],
        out_specs=[_row_spec(512), _row_spec(512), _row_spec(512)],
        out_shape=[rm(F32), rm(BF16), rm(BF16)],
        scratch_shapes=[pltpu.VMEM((EXT_ROWS, D_CONV), F32), pltpu.VMEM((8, SHIFT_ROWS, D_CONV), F32)],
        compiler_params=_cp(40, ("arbitrary",)),
    )(u, ug, u, ug, conv_w, conv_b, g, b, w_pw)


def _out_fwd(o, ga, pw, gc, x, metapad, g_in, b_in, w_out, g_out, b_out, target, nt):
    lp = nt * TILE

    def body(o_ref, ga_ref, pw_ref, gc_ref, x_ref, mp_ref, gi_ref, bi_ref, wo_ref, go_ref, bo_ref, t_ref,
             y_ref, dz_ref, loss_ref, dgo_ref, dbo_ref):
        i = pl.program_id(0)

        @pl.when(i == 0)
        def _():
            loss_ref[...] = jnp.zeros_like(loss_ref)
            dgo_ref[...] = jnp.zeros_like(dgo_ref)
            dbo_ref[...] = jnp.zeros_like(dbo_ref)

        x0 = jnp.where(i == 0, mp_ref[...], x_ref[...])
        xhat, _ = _ln_stats(x0)
        h = xhat * gi_ref[...] + bi_ref[...]
        ya, _ = _silu_and_grad(ga_ref[...].astype(F32))
        yc, _ = _silu_and_grad(gc_ref[...].astype(F32))
        ya = (o_ref[...].astype(F32) * ya).astype(BF16)
        yc = (pw_ref[...].astype(F32) * yc).astype(BF16)
        y_ref[:, :D_ATTN] = ya
        y_ref[:, D_ATTN:] = yc
        z = ALPHA * h + jnp.dot(ya, wo_ref[:D_ATTN, :], preferred_element_type=F32) \
            + jnp.dot(yc, wo_ref[D_ATTN:, :], preferred_element_type=F32)
        zhat, rstd = _ln_stats(z)
        out = zhat * go_ref[...] + bo_ref[...]
        live = (i > 0).astype(F32)
        err = (out - t_ref[...]) * live
        dout = err * (1.0 / D_MODEL)
        loss_ref[...] += 0.5 * jnp.sum(jnp.sum(err * dout, axis=0, keepdims=True), axis=1, keepdims=True)
        dgo_ref[...] += jnp.sum(dout * zhat, axis=0, keepdims=True)
        dbo_ref[...] += jnp.sum(dout, axis=0, keepdims=True)
        dz_ref[...] = _ln_bwd(dout, zhat, rstd, go_ref[...])

    return pl.pallas_call(
        body, name="out_fwd", grid=(nt,),
        in_specs=[_row_spec(512), _row_spec(512), _row_spec(512), _row_spec(512),
                  _row_spec(D_MODEL, shift=True), _full_spec((TILE, D_MODEL)), _full_spec((1, D_MODEL)),
                  _full_spec((1, D_MODEL)), _full_spec((D_MODEL, D_MODEL)), _full_spec((1, D_MODEL)),
                  _full_spec((1, D_MODEL)), _row_spec(D_MODEL, shift=True)],
        out_specs=[_row_spec(D_MODEL), _row_spec(D_MODEL), _full_spec((1, LANES)), _full_spec((1, D_MODEL)),
                   _full_spec((1, D_MODEL))],
        out_shape=[jax.ShapeDtypeStruct((lp, D_MODEL), BF16), jax.ShapeDtypeStruct((lp, D_MODEL), F32),
                   jax.ShapeDtypeStruct((1, LANES), F32), jax.ShapeDtypeStruct((1, D_MODEL), F32),
                   jax.ShapeDtypeStruct((1, D_MODEL), F32)],
        compiler_params=_cp(40, ("arbitrary",)),
    )(o, ga, pw, gc, x, metapad, g_in, b_in, w_out, g_out, b_out, target)


def _out_bwd(dz, y, o, ga, pw, gc, w_out, hc, co, w_pw, g_cv, b_cv, nt):
    lp = nt * TILE

    def body(dz_ref, y_ref, o_ref, ga_ref, pw_ref, gc_ref, wo_ref, hc_ref, co_ref, wpw_ref, g_ref, b_ref,
             doT_ref, dga_ref, dgc_ref, dwo_ref, dco_ref, dwpw_ref, dg_ref, db_ref, dcb_ref):
        i = pl.program_id(0)

        @pl.when(i == 0)
        def _():
            dwo_ref[...] = jnp.zeros_like(dwo_ref)
            dwpw_ref[...] = jnp.zeros_like(dwpw_ref)
            dg_ref[...] = jnp.zeros_like(dg_ref)
            db_ref[...] = jnp.zeros_like(db_ref)
            dcb_ref[...] = jnp.zeros_like(dcb_ref)

        dzb = dz_ref[...].astype(BF16)
        nt_dims = (((1,), (1,)), ((), ()))
        tn_dims = (((0,), (0,)), ((), ()))
        dya = lax.dot_general(dzb, wo_ref[:D_ATTN, :], nt_dims, preferred_element_type=F32)
        dyc = lax.dot_general(dzb, wo_ref[D_ATTN:, :], nt_dims, preferred_element_type=F32)
        sa, sga = _silu_and_grad(ga_ref[...].astype(F32))
        sc, sgc = _silu_and_grad(gc_ref[...].astype(F32))
        doT_ref[0] = (dya * sa).T.astype(BF16)
        dga_ref[...] = (dya * o_ref[...].astype(F32) * sga).astype(BF16)
        dpw_b = (dyc * sc).astype(BF16)
        dgc_ref[...] = (dyc * pw_ref[...].astype(F32) * sgc).astype(BF16)
        dwo_ref[...] += lax.dot_general(y_ref[...], dzb, tn_dims, preferred_element_type=F32)

        dhc = lax.dot_general(dpw_b, wpw_ref[...], nt_dims, preferred_element_type=F32)
        xhat, rstd = _ln_stats(co_ref[...])
        _, sg = _silu_and_grad(xhat * g_ref[...] + b_ref[...])
        dln = dhc * sg
        dg_ref[...] += jnp.sum(dln * xhat, axis=0, keepdims=True)
        db_ref[...] += jnp.sum(dln, axis=0, keepdims=True)
        dco = _ln_bwd(dln, xhat, rstd, g_ref[...])
        dco_ref[...] = dco
        dcb_ref[...] += jnp.sum(dco, axis=0, keepdims=True)
        dwpw_ref[...] += lax.dot_general(hc_ref[...], dpw_b, tn_dims, preferred_element_type=F32)

    rm = jax.ShapeDtypeStruct((lp, 512), BF16)
    vec = jax.ShapeDtypeStruct((1, D_CONV), F32)
    return pl.pallas_call(
        body, name="out_bwd", grid=(nt,),
        in_specs=[_row_spec(D_MODEL), _row_spec(D_MODEL), _row_spec(512), _row_spec(512), _row_spec(512),
                  _row_spec(512), _full_spec((D_MODEL, D_MODEL)), _row_spec(512), _row_spec(512),
                  _full_spec((512, 512)), _full_spec((1, 512)), _full_spec((1, 512))],
        out_specs=[_t3_spec(512), _row_spec(512), _row_spec(512), _full_spec((D_MODEL, D_MODEL)), _row_spec(512),
                   _full_spec((512, 512)), _full_spec((1, 512)), _full_spec((1, 512)), _full_spec((1, 512))],
        out_shape=[jax.ShapeDtypeStruct((nt, 512, TILE), BF16), rm, rm, jax.ShapeDtypeStruct((D_MODEL, D_MODEL), F32),
                   jax.ShapeDtypeStruct((lp, D_CONV), F32), jax.ShapeDtypeStruct((512, 512), F32), vec, vec, vec],
        compiler_params=_cp(56, ("arbitrary",)),
    )(dz, y, o, ga, pw, gc, w_out, hc, co, w_pw, g_cv, b_cv)


def _conv_bwd_taps(dco, u, ug, conv_w, nt):
    lp = nt * TILE

    def body(dco_ref, dcon_ref, u_ref, ug_ref, up_ref, ugp_ref, w3_ref, du_ref, dug_ref, dw_ref, ext, dext, sh, dsh,
             dhg_s, dw_s):
        i = pl.program_id(0)

        @pl.when(i == 0)
        def _():
            dw_s[...] = jnp.zeros_like(dw_s)

        prev = _glu(up_ref[...], ugp_ref[...], i - 1)
        ext[0:HALO, :] = jnp.where(i > 0, prev[TILE - HALO:, :], 0.0)
        ext[HALO:HALO + TILE, :] = _glu(u_ref[...], ug_ref[...], i)
        ext[HALO + TILE:, :] = jnp.zeros((8, D_CONV), F32)
        dext[0:TILE, :] = dco_ref[...]
        dext[TILE:TILE + HALO, :] = jnp.where(i < nt - 1, dcon_ref[0:HALO, :], 0.0)
        dext[TILE + HALO:, :] = jnp.zeros((8, D_CONV), F32)
        _shifted_copies(sh, ext)
        _shifted_copies(dsh, dext)
        stripe = 32

        def stripe_body(rb, carry):
            row0 = pl.multiple_of(rb * stripe, stripe)
            dco = dco_ref[pl.ds(row0, stripe), :]
            dhg = jnp.zeros((stripe, D_CONV), F32)
            for t in range(CONV_WIDTH):
                off = HALO - (CONV_WIDTH - 1) + t
                back = CONV_WIDTH - 1 - t
                prod = dco * sh[off % 8, pl.ds((off // 8) * 8 + row0, stripe), :]
                part = prod[0:8, :]
                for r8 in range(1, stripe // 8):
                    part = part + prod[8 * r8:8 * r8 + 8, :]
                dw_s[t] += part
                dhg = dhg + w3_ref[t] * dsh[back % 8, pl.ds((back // 8) * 8 + row0, stripe), :]
            dhg_s[pl.ds(row0, stripe), :] = dhg
            return carry

        lax.fori_loop(0, TILE // stripe, stripe_body, 0)

        @pl.when(i == nt - 1)
        def _():
            dw_ref[...] = jnp.sum(dw_s[...], axis=1)

        dhg = jnp.where(_row_mask(i, (TILE, D_CONV)), dhg_s[...], 0.0)
        sg = _sigmoid(ug_ref[...].astype(F32))
        du_ref[...] = (dhg * sg).astype(BF16)
        dug_ref[...] = (dhg * u_ref[...].astype(F32) * sg * (1.0 - sg)).astype(BF16)

    rm = jax.ShapeDtypeStruct((lp, D_CONV), BF16)
    nxt = pl.BlockSpec((TILE, 512), lambda i: (jnp.minimum(i + 1, nt - 1), 0))
    ext_t = pltpu.VMEM((EXT_ROWS, D_CONV), F32)
    sh_t = pltpu.VMEM((8, SHIFT_ROWS, D_CONV), F32)
    return pl.pallas_call(
        body, name="conv_bwd_taps", grid=(nt,),
        in_specs=[_row_spec(512), nxt, _row_spec(512), _row_spec(512), _row_spec(512, shift=True),
                  _row_spec(512, shift=True), _full_spec((32, 1, 512))],
        out_specs=[_row_spec(512), _row_spec(512), _full_spec((32, 512))],
        out_shape=[rm, rm, jax.ShapeDtypeStruct((32, D_CONV), F32)],
        scratch_shapes=[ext_t, ext_t, sh_t, sh_t, pltpu.VMEM((TILE, D_CONV), F32), pltpu.VMEM((32, 8, D_CONV), F32)],
        compiler_params=_cp(48, ("arbitrary",)),
    )(dco, dco, u, ug, u, ug, conv_w.reshape(32, 1, D_CONV))


def _cumsum_bwd(dck, dcq4, fl, bf_pad, nt):
    lp = nt * TILE

    def body(dck_ref, dcq_ref, fl_ref, bf_ref, dfl_ref, dbf_ref, carry):
        i = pl.program_id(0)
        tile = nt - 1 - i

        @pl.when(i == 0)
        def _():
            carry[...] = jnp.zeros_like(carry)
            dbf_ref[...] = jnp.zeros_like(dbf_ref)

        dc = jnp.zeros((TILE, LANES), F32)
        for p in range(N_HEADS // 2):
            dq_rows = jnp.concatenate([dcq_ref[p, 0], jnp.zeros((LANES - 8, TILE), F32)], axis=0)
            both = dck_ref[p] + dq_rows.T
            dc = dc + (both if p == 0 else pltpu.roll(both, 2 * p, 1))
        r = lax.broadcasted_iota(jnp.int32, (TILE, TILE), 0)
        c = lax.broadcasted_iota(jnp.int32, (TILE, TILE), 1)
        triu = (c >= r).astype(F32)
        dlf = jnp.dot(triu, dc, precision=lax.Precision.HIGHEST, preferred_element_type=F32) + carry[...]
        carry[...] = dlf[0:1, :]
        z = fl_ref[...] + bf_ref[...]
        lane = lax.broadcasted_iota(jnp.int32, (TILE, LANES), 1)
        dfl = jnp.where(_row_mask(tile, (TILE, LANES)) & (lane < N_HEADS), dlf * _sigmoid(-z), 0.0)
        dfl_ref[...] = dfl.astype(BF16)
        dbf_ref[...] += jnp.sum(dfl, axis=0, keepdims=True)

    rev = lambda i: (nt - 1 - i, 0)
    return pl.pallas_call(
        body, name="cumsum_bwd", grid=(nt,),
        in_specs=[pl.BlockSpec((N_HEADS // 2, TILE, LANES), lambda i: (0, nt - 1 - i, 0)),
                  pl.BlockSpec((N_HEADS // 2, 1, 8, TILE), lambda i: (0, nt - 1 - i, 0, 0)),
                  pl.BlockSpec((TILE, LANES), rev), _full_spec((1, LANES))],
        out_specs=[pl.BlockSpec((TILE, LANES), rev), _full_spec((1, LANES))],
        out_shape=[jax.ShapeDtypeStruct((lp, LANES), BF16), jax.ShapeDtypeStruct((1, LANES), F32)],
        scratch_shapes=[pltpu.VMEM((1, LANES), F32)],
        compiler_params=_cp(32, ("arbitrary",)),
    )(dck, dcq4, fl, bf_pad)


def _dw_rowmajor(hb, secs, nt):
    n = len(secs)

    def body(*refs):
        hb_ref, sec_refs, out_refs = refs[0], refs[1:1 + n], refs[1 + n:]
        i = pl.program_id(0)

        @pl.when(i == 0)
        def _():
            for o_ref in out_refs:
                o_ref[...] = jnp.zeros_like(o_ref)

        hb_t = hb_ref[...]
        for s_ref, o_ref in zip(sec_refs, out_refs):
            o_ref[...] += lax.dot_general(hb_t, s_ref[...], (((0,), (0,)), ((), ())), preferred_element_type=F32)

    return pl.pallas_call(
        body, name="dw_rowmajor", grid=(nt,),
        in_specs=[_row_spec(D_MODEL)] + [_row_spec(s.shape[1]) for s in secs],
        out_specs=[_full_spec((D_MODEL, s.shape[1])) for s in secs],
        out_shape=[jax.ShapeDtypeStruct((D_MODEL, s.shape[1]), F32) for s in secs],
        compiler_params=_cp(48, ("arbitrary",)),
    )(hb, *secs)


def _dw_transposed(hb, secs_t3, nt):
    n = len(secs_t3)

    def body(*refs):
        hb_ref, sec_refs, out_refs = refs[0], refs[1:1 + n], refs[1 + n:]
        i = pl.program_id(0)

        @pl.when(i == 0)
        def _():
            for o_ref in out_refs:
                o_ref[...] = jnp.zeros_like(o_ref)

        hb_t = hb_ref[...]
        for s_ref, o_ref in zip(sec_refs, out_refs):
            o_ref[...] += jnp.dot(s_ref[0], hb_t, preferred_element_type=F32)

    return pl.pallas_call(
        body, name="dw_transposed", grid=(nt,),
        in_specs=[_row_spec(D_MODEL)] + [_t3_spec(512) for _ in secs_t3],
        out_specs=[_full_spec((512, D_MODEL)) for _ in secs_t3],
        out_shape=[jax.ShapeDtypeStruct((512, D_MODEL), F32) for _ in secs_t3],
        compiler_params=_cp(40, ("arbitrary",)),
    )(hb, *secs_t3)


def _dh_bwd(secs, secs_t3, w_rm, w_t, dz, x, metapad, g_in, nt):
    n, m = len(secs), len(secs_t3)
    offs = OFF_GA_R + np.cumsum([0] + [s.shape[1] for s in secs])

    def body(*refs):
        sec_refs, t3_refs = refs[:n], refs[n:n + m]
        wrm_ref, wt_ref, dz_ref, x_ref, mp_ref, g_ref = refs[n + m:n + m + 6]
        dx_ref, dmeta_ref, dg_ref, db_ref = refs[n + m + 6:]
        i = pl.program_id(0)

        @pl.when(i == 0)
        def _():
            dg_ref[...] = jnp.zeros_like(dg_ref)
            db_ref[...] = jnp.zeros_like(db_ref)

        dh = ALPHA * dz_ref[...]
        for s_ref, lo, hi in zip(sec_refs, offs[:-1], offs[1:]):
            dh = dh + lax.dot_general(s_ref[...], wrm_ref[:, lo:hi], (((1,), (1,)), ((), ())),
                                      preferred_element_type=F32)
        for idx, t_ref in enumerate(t3_refs):
            dh = dh + lax.dot_general(t_ref[0], wt_ref[idx * 512:(idx + 1) * 512, :], (((0,), (0,)), ((), ())),
                                      preferred_element_type=F32)
        x0 = jnp.where(i == 0, mp_ref[...], x_ref[...])
        xhat, rstd = _ln_stats(x0)
        dg_ref[...] += jnp.sum(dh * xhat, axis=0, keepdims=True)
        db_ref[...] += jnp.sum(dh, axis=0, keepdims=True)
        dx = _ln_bwd(dh, xhat, rstd, g_ref[...])
        dx_ref[...] = dx

        @pl.when(i == 0)
        def _():
            dmeta_ref[...] = dx

    seq = (nt - 1) * TILE
    return pl.pallas_call(
        body, name="dh_bwd", grid=(nt,),
        in_specs=[_row_spec(s.shape[1]) for s in secs] + [_t3_spec(512) for _ in secs_t3]
        + [_full_spec(w_rm.shape), _full_spec(w_t.shape), _row_spec(D_MODEL), _row_spec(D_MODEL, shift=True),
           _full_spec((TILE, D_MODEL)), _full_spec((1, D_MODEL))],
        out_specs=[_row_spec(D_MODEL, shift=True), _full_spec((TILE, D_MODEL)), _full_spec((1, D_MODEL)),
                   _full_spec((1, D_MODEL))],
        out_shape=[jax.ShapeDtypeStruct((seq, D_MODEL), F32), jax.ShapeDtypeStruct((TILE, D_MODEL), F32),
                   jax.ShapeDtypeStruct((1, D_MODEL), F32), jax.ShapeDtypeStruct((1, D_MODEL), F32)],
        compiler_params=_cp(56, ("arbitrary",)),
    )(*secs, *secs_t3, w_rm, w_t, dz, x, metapad, g_in)


RB = 256
SMALL_ROWS = 48


def _repack_weights(all_in, all_small):
    n_cw = D_CONV // N_DEV

    def body(a_ref, s_ref, wr_ref, wt_ref, mp_ref, cw_ref):
        full = jnp.concatenate([a_ref[d].T[:, :SHARD_IN] for d in range(N_DEV)], axis=1)
        qkv = full[:, :1536]
        wr_ref[:, :1536] = qkv
        wr_ref[:, 1536:OFF_F_R] = full[:, 1544:]
        wr_ref[:, OFF_F_R:] = jnp.concatenate([full[:, 1536:1544], jnp.zeros((RB, LANES - N_HEADS), BF16)], axis=1)
        wt_ref[...] = qkv.T

        @pl.when(pl.program_id(0) == 0)
        def _():
            mp_ref[0:PAD, :] = jnp.zeros((PAD, D_MODEL), F32)
            mp_ref[PAD:, :] = jnp.concatenate([s_ref[d, 0:N_META, :] for d in range(N_DEV)], axis=1)
            cw_ref[...] = jnp.concatenate([s_ref[d, N_META:, 0:n_cw] for d in range(N_DEV)], axis=1)

    return pl.pallas_call(
        body, name="repack_weights", grid=(D_MODEL // RB,),
        in_specs=[pl.BlockSpec((N_DEV, 512, RB), lambda i: (0, 0, i)), _full_spec((N_DEV, SMALL_ROWS, LANES))],
        out_specs=[pl.BlockSpec((RB, W_COLS), lambda i: (i, 0)), pl.BlockSpec((1536, RB), lambda i: (0, i)),
                   _full_spec((TILE, D_MODEL)), _full_spec((32, D_CONV))],
        out_shape=[jax.ShapeDtypeStruct((D_MODEL, W_COLS), BF16), jax.ShapeDtypeStruct((1536, D_MODEL), BF16),
                   jax.ShapeDtypeStruct((TILE, D_MODEL), F32), jax.ShapeDtypeStruct((32, D_CONV), F32)],
        compiler_params=_cp(40, ("arbitrary",)),
    )(all_in, all_small)


def _unpack_dw_in(dw_rm, dw_t):
    def body(dga_ref, du_ref, dug_ref, dgc_ref, dfl_ref, dq_ref, dk_ref, dv_ref, out_ref, outb_ref):
        full = jnp.concatenate([dq_ref[...].T, dk_ref[...].T, dv_ref[...].T, dfl_ref[:, 0:N_HEADS], dga_ref[...],
                                du_ref[...], dug_ref[...], dgc_ref[...]], axis=1)
        pad = jnp.zeros((RB, 512 - SHARD_IN), F32)
        for d in range(N_DEV):
            blk = jnp.concatenate([full[:, SHARD_IN * d:SHARD_IN * (d + 1)], pad], axis=1).T
            out_ref[d] = blk
            outb_ref[d] = blk.astype(BF16)

    rm = pl.BlockSpec((RB, 512), lambda i: (i, 0))
    tr = pl.BlockSpec((512, RB), lambda i: (0, i))
    blocks = pl.BlockSpec((N_DEV, 512, RB), lambda i: (0, 0, i))
    return pl.pallas_call(
        body, name="unpack_dw_in", grid=(D_MODEL // RB,),
        in_specs=[rm, rm, rm, rm, pl.BlockSpec((RB, LANES), lambda i: (i, 0)), tr, tr, tr],
        out_specs=[blocks, blocks],
        out_shape=[jax.ShapeDtypeStruct((N_DEV, 512, D_MODEL), F32), jax.ShapeDtypeStruct((N_DEV, 512, D_MODEL), BF16)],
        compiler_params=_cp(48, ("arbitrary",)),
    )(*dw_rm, *dw_t)


def _local_step(x, target, metapad, cw, w_r, w_t, w_pw_full, w_out_full, ln_in_g, ln_in_b, b_f, conv_b, ln_conv_g,
                ln_conv_b, ln_out_g, ln_out_b):
    seq = x.shape[0]
    nt = seq // TILE + 1
    row = lambda a: a.reshape(1, -1).astype(F32)
    bf_pad = jnp.pad(row(b_f), ((0, 0), (0, LANES - N_HEADS)))
    g_in, b_in = row(ln_in_g), row(ln_in_b)
    g_cv, b_cv, c_b = row(ln_conv_g), row(ln_conv_b), row(conv_b)
    g_out, b_out = row(ln_out_g), row(ln_out_b)

    hb, qT3, kT3, vT3, k, v, ga, u, ug, gc, fl = _proj_fwd(x, metapad, g_in, b_in, w_r, nt)
    kx3 = _cumsum_fwd(fl, bf_pad, nt)
    oT3, o, lse4 = _attn_fwd(qT3, k, kx3, vT3, nt)
    co, hc, pw = _conv_fwd(u, ug, cw, c_b, g_cv, b_cv, w_pw_full, nt)
    y, dz, loss, dg_out, db_out = _out_fwd(o, ga, pw, gc, x, metapad, g_in, b_in, w_out_full, g_out, b_out,
                                            target, nt)
    doT3, dga, dgc, dw_out, dco, dw_pw, dg_cv, db_cv, dc_b = _out_bwd(dz, y, o, ga, pw, gc, w_out_full, hc, co,
                                                                      w_pw_full, g_cv, b_cv, nt)
    du, dug, dcw = _conv_bwd_taps(dco, u, ug, cw, nt)
    dqT3, dkT3, dvT3, dck, dcq4 = _attn_bwd(qT3, kT3, k, kx3, v, oT3, doT3, lse4, nt)
    dfl, dbf = _cumsum_bwd(dck, dcq4, fl, bf_pad, nt)
    secs = (dga, du, dug, dgc, dfl)
    secs_t3 = (dqT3, dkT3, dvT3)
    dw_rm = _dw_rowmajor(hb, secs, nt)
    dw_t = _dw_transposed(hb, secs_t3, nt)
    grad_x, dmetapad, dg_in, db_in = _dh_bwd(secs, secs_t3, w_r, w_t, dz, x, metapad, g_in, nt)
    pieces = dict(loss=loss, metapad=dmetapad, ln_in_g=dg_in, ln_in_b=db_in, w_in_rm=dw_rm, w_in_t=dw_t, b_f=dbf,
                  conv_w=dcw, conv_b=dc_b, ln_conv_g=dg_cv, ln_conv_b=db_cv, w_pw=dw_pw, w_out=dw_out,
                  ln_out_g=dg_out, ln_out_b=db_out)
    return grad_x, pieces


MESH = pl.DeviceIdType.MESH
ANY = pl.BlockSpec(memory_space=pl.ANY)


def _mesh_pos():
    return lax.axis_index("x"), lax.axis_index("y"), lax.axis_index("c")


GATHER_SEMS = 8


def _gather_body(x_refs, out_refs, send_sems, recv_sems, local_sems):
    n = len(x_refs)
    x, y, c = _mesh_pos()
    me, sibling = (x, y, c), (x, y, 1 - c)
    xn, yn, dg = (1 - x, y), (x, 1 - y), (1 - x, 1 - y)

    def slot(a, px, py, pc, half=None):
        blk = out_refs[a].at[4 * px + 2 * py + pc]
        if half is None:
            return blk
        rows = blk.shape[0] // 2
        return blk.at[pl.ds(half * rows, rows)]

    def copy(a, k, block, to, src=None, half=None):
        return pltpu.make_async_remote_copy(
            src_ref=slot(a, *block, half) if src is None else src, dst_ref=slot(a, *block, half),
            send_sem=send_sems.at[GATHER_SEMS * a + k], recv_sem=recv_sems.at[GATHER_SEMS * a + k], device_id=to,
            device_id_type=MESH)

    arrays = range(n)
    mine = [pltpu.make_async_copy(x_refs[a], slot(a, *me), local_sems.at[a]) for a in arrays]
    for cp in mine:
        cp.start()
    sent = []
    for a in arrays:
        sent += [copy(a, 0, me, sibling, src=x_refs[a]), copy(a, 1, me, (*xn, c), src=x_refs[a]),
                 copy(a, 2, me, (*yn, c), src=x_refs[a])]
    for cp in sent:
        cp.start()

    def also(cp):
        cp.start()
        sent.append(cp)

    for a in arrays:
        copy(a, 2, (*yn, c), me).wait_recv()
        also(copy(a, 3, (*yn, c), (*xn, c), half=0))
        also(copy(a, 6, (*yn, c), sibling))
        copy(a, 1, (*xn, c), me).wait_recv()
        also(copy(a, 4, (*xn, c), (*yn, c), half=1))
        also(copy(a, 5, (*xn, c), sibling))
    for a in arrays:
        copy(a, 3, (*dg, c), me, half=0).wait_recv()
        copy(a, 4, (*dg, c), me, half=1).wait_recv()
        also(copy(a, 7, (*dg, c), sibling))
    for a in arrays:
        copy(a, 0, sibling, me).wait_recv()
        copy(a, 5, (*xn, 1 - c), me).wait_recv()
        copy(a, 6, (*yn, 1 - c), me).wait_recv()
        copy(a, 7, (*dg, 1 - c), me).wait_recv()
    for cp in sent:
        cp.wait_send()
    for cp in mine:
        cp.wait()


def _all_gather(blks, name):
    n = len(blks)

    def body(*refs):
        _gather_body(refs[:n], refs[n:2 * n], *refs[2 * n:])

    return pl.pallas_call(
        body, name=name, out_shape=[jax.ShapeDtypeStruct((N_DEV, *b.shape), b.dtype) for b in blks],
        in_specs=[ANY] * n, out_specs=[ANY] * n,
        scratch_shapes=[pltpu.SemaphoreType.DMA((GATHER_SEMS * n,)), pltpu.SemaphoreType.DMA((GATHER_SEMS * n,)),
                        pltpu.SemaphoreType.DMA((n,))],
    )(*blks)


def _exchange_sibling(g8s, small):
    n = len(g8s)

    def body(*refs):
        g_refs, s_ref, out_refs, a_ref = refs[:n], refs[n], refs[n + 1:2 * n + 1], refs[2 * n + 1]
        send_sems, recv_sems, a_send, a_recv, a_local = refs[2 * n + 2:]
        x, y, c = _mesh_pos()
        cps = [pltpu.make_async_remote_copy(
            src_ref=g_refs[a].at[2 * q + (1 - c)], dst_ref=out_refs[a].at[q], send_sem=send_sems.at[4 * a + q],
            recv_sem=recv_sems.at[4 * a + q], device_id=(x, y, 1 - c), device_id_type=MESH)
            for a in range(n) for q in range(4)]
        for cp in cps:
            cp.start()
        _gather_body([s_ref], [a_ref], a_send, a_recv, a_local)
        for cp in cps:
            cp.wait()

    outs = pl.pallas_call(
        body, name="rs_sibling",
        out_shape=[jax.ShapeDtypeStruct((4, *g.shape[1:]), g.dtype) for g in g8s]
        + [jax.ShapeDtypeStruct((N_DEV, *small.shape), small.dtype)],
        in_specs=[ANY] * (n + 1), out_specs=[ANY] * (n + 1),
        scratch_shapes=[pltpu.SemaphoreType.DMA((4 * n,)), pltpu.SemaphoreType.DMA((4 * n,)),
                        pltpu.SemaphoreType.DMA((GATHER_SEMS,)), pltpu.SemaphoreType.DMA((GATHER_SEMS,)),
                        pltpu.SemaphoreType.DMA((1,))],
    )(*g8s, small)
    return outs[:n], outs[n]


def _exchange_chips(p4s):
    n = len(p4s)

    def body(*refs):
        p_refs, out_refs, send_sems, recv_sems = refs[:n], refs[n:2 * n], refs[2 * n], refs[2 * n + 1]
        x, y, c = _mesh_pos()
        chips = [(1 - x, y), (x, 1 - y), (1 - x, 1 - y)]
        cps = [pltpu.make_async_remote_copy(
            src_ref=p_refs[a].at[2 * cx + cy], dst_ref=out_refs[a].at[k], send_sem=send_sems.at[3 * a + k],
            recv_sem=recv_sems.at[3 * a + k], device_id=(cx, cy, c), device_id_type=MESH)
            for k, (cx, cy) in enumerate(chips) for a in range(n)]
        for cp in cps:
            cp.start()
        for cp in cps:
            cp.wait()

    return pl.pallas_call(
        body, name="rs_chips", out_shape=[jax.ShapeDtypeStruct((3, *p.shape[1:]), p.dtype) for p in p4s],
        in_specs=[ANY] * n, out_specs=[ANY] * n,
        scratch_shapes=[pltpu.SemaphoreType.DMA((3 * n,)), pltpu.SemaphoreType.DMA((3 * n,))],
    )(*p4s)


def _rs_add_sibling(g8s, recvs, c_idx):
    n = len(g8s)

    def body(s_ref, *refs):
        g_refs, r_refs, p32_refs, pb_refs = (refs[k * n:(k + 1) * n] for k in range(4))
        for g_ref, r_ref, p32_ref, pb_ref in zip(g_refs, r_refs, p32_refs, pb_refs):
            p = g_ref[0] + r_ref[0].astype(F32)
            p32_ref[0] = p
            pb_ref[0] = p.astype(BF16)

    blk = lambda g: pl.BlockSpec((1, *g.shape[1:]), lambda q, s: (q, 0, 0))
    grid_spec = pltpu.PrefetchScalarGridSpec(
        num_scalar_prefetch=1, grid=(4,),
        in_specs=[pl.BlockSpec((1, *g.shape[1:]), lambda q, s: (2 * q + s[0], 0, 0)) for g in g8s]
        + [blk(g) for g in g8s],
        out_specs=[blk(g) for g in g8s] * 2)
    outs = pl.pallas_call(
        body, name="rs_add_sibling", grid_spec=grid_spec,
        out_shape=[jax.ShapeDtypeStruct((4, *g.shape[1:]), F32) for g in g8s]
        + [jax.ShapeDtypeStruct((4, *g.shape[1:]), BF16) for g in g8s],
        compiler_params=_cp(48, ("arbitrary",)),
    )(c_idx, *g8s, *recvs)
    return outs[:n], outs[n:]


def _rs_add_chips(p32s, recvs, q_idx):
    def body(s_ref, pin_ref, pout_ref, ppw_ref, rin_ref, rout_ref, rpw_ref, gin_ref, gout_ref, gpw_ref):
        def total(p_ref, r_ref):
            return ((p_ref[0] + r_ref[0].astype(F32)) + r_ref[1].astype(F32)) + r_ref[2].astype(F32)

        gin_ref[...] = total(pin_ref, rin_ref)[:SHARD_IN, :]
        gout_ref[0] = total(pout_ref, rout_ref)
        gpw_ref[0] = total(ppw_ref, rpw_ref)

    own = lambda p: pl.BlockSpec((1, *p.shape[1:]), lambda i, s: (s[0], 0, 0))
    whole = lambda shape: pl.BlockSpec(shape, lambda i, s: (0,) * len(shape))
    out_shapes = [(SHARD_IN, D_MODEL), (1, *p32s[1].shape[1:]), (1, *p32s[2].shape[1:])]
    grid_spec = pltpu.PrefetchScalarGridSpec(
        num_scalar_prefetch=1, grid=(1,),
        in_specs=[own(p) for p in p32s] + [whole(r.shape) for r in recvs],
        out_specs=[whole(s) for s in out_shapes])
    return pl.pallas_call(
        body, name="rs_add_chips", grid_spec=grid_spec,
        out_shape=[jax.ShapeDtypeStruct(s, F32) for s in out_shapes],
        compiler_params=_cp(48, ("arbitrary",)),
    )(q_idx, *p32s, *recvs)


SMALL_ROWS_G = 64
SMALL_LAYOUT = {
    "metapad": (0, N_META, D_MODEL), "conv_w": (16, 32, D_CONV), "ln_in_g": (48, 1, D_MODEL),
    "ln_in_b": (49, 1, D_MODEL), "b_f": (50, 1, LANES), "conv_b": (51, 1, D_CONV), "ln_conv_g": (52, 1, D_CONV),
    "ln_conv_b": (53, 1, D_CONV), "ln_out_g": (54, 1, D_MODEL), "ln_out_b": (55, 1, D_MODEL), "loss": (56, 1, LANES)}


def _pack_small(pieces):
    names = list(SMALL_LAYOUT)

    def body(*refs):
        out_ref = refs[-1]
        out_ref[...] = jnp.zeros_like(out_ref)
        for name, ref in zip(names, refs[:-1]):
            r0, nr, nl = SMALL_LAYOUT[name]
            src = ref[PAD:, :] if name == "metapad" else ref[...]
            out_ref[r0:r0 + nr, 0:nl] = src

    return pl.pallas_call(body, name="pack_small", out_shape=jax.ShapeDtypeStruct((SMALL_ROWS_G, D_MODEL), F32),
                          compiler_params=_cp(16))(*[pieces[n] for n in names])


def _sum_small(gathered):
    names = list(SMALL_LAYOUT)

    def body(a_ref, *out_refs):
        acc = a_ref[0]
        for d in range(1, N_DEV):
            acc = acc + a_ref[d]
        for name, ref in zip(names, out_refs):
            r0, nr, nl = SMALL_LAYOUT[name]
            ref[...] = acc[r0:r0 + nr, 0:nl]

    outs = pl.pallas_call(
        body, name="sum_small",
        out_shape=[jax.ShapeDtypeStruct(SMALL_LAYOUT[n][1:], F32) for n in names], compiler_params=_cp(16))(gathered)
    return dict(zip(names, outs))


def _adamw(ws, gs, ms, vs):
    n = len(ws)
    c1 = 1.0 - ADAM_B1 ** ADAM_STEP
    c2 = 1.0 - ADAM_B2 ** ADAM_STEP

    def body(*refs):
        w_refs, g_refs, m_refs, v_refs = (refs[k * n:(k + 1) * n] for k in range(4))
        d_refs, nm_refs, nv_refs = (refs[(4 + k) * n:(5 + k) * n] for k in range(3))
        for w_ref, g_ref, m_ref, v_ref, d_ref, nm_ref, nv_ref in zip(w_refs, g_refs, m_refs, v_refs, d_refs,
                                                                     nm_refs, nv_refs):
            g = g_ref[...]
            m = ADAM_B1 * m_ref[...] + (1.0 - ADAM_B1) * g
            v = ADAM_B2 * v_ref[...] + (1.0 - ADAM_B2) * (g * g)
            nm_ref[...] = m
            nv_ref[...] = v
            d_ref[...] = -ADAM_LR * ((m / c1) / (jnp.sqrt(v / c2) + ADAM_EPS) + ADAM_WD * w_ref[...])

    shapes = [jax.ShapeDtypeStruct(w.shape, F32) for w in ws]
    outs = pl.pallas_call(body, name="adamw", out_shape=shapes * 3, compiler_params=_cp(48))(*ws, *gs, *ms, *vs)
    return outs[:n], outs[n:2 * n], outs[2 * n:]


W_NAMES = ("meta", "ln_in_g", "ln_in_b", "w_in", "b_f", "conv_w", "conv_b", "ln_conv_g", "ln_conv_b", "w_pw",
           "w_out", "ln_out_g", "ln_out_b")


def kernel(x, meta, ln_in_g, ln_in_b, w_in, b_f, conv_w, conv_b, ln_conv_g, ln_conv_b, w_pw, w_out, ln_out_g, ln_out_b, loss_target, m_meta, m_ln_in_g, m_ln_in_b, m_w_in, m_b_f, m_conv_w, m_conv_b, m_ln_conv_g, m_ln_conv_b, m_w_pw, m_w_out, m_ln_out_g, m_ln_out_b, v_meta, v_ln_in_g, v_ln_in_b, v_w_in, v_b_f, v_conv_w, v_conv_b, v_ln_conv_g, v_ln_conv_b, v_w_pw, v_w_out, v_ln_out_g, v_ln_out_b):
    mx, my, mc = _mesh_pos()
    me = 4 * mx + 2 * my + mc
    n_meta_sh = D_MODEL // N_DEV
    n_cw_sh = D_CONV // N_DEV
    n_out_sh = D_MODEL // N_DEV
    n_pw_sh = D_CONV // N_DEV

    small_w = jnp.concatenate([meta, jnp.pad(conv_w[0], ((0, 1), (0, LANES - n_cw_sh)))], axis=0)
    all_in, all_out, all_pw, all_small = _all_gather(
        [jnp.pad(w_in[0].T, ((0, 512 - SHARD_IN), (0, 0))).astype(BF16), w_out[0].astype(BF16), w_pw[0].astype(BF16),
         small_w], "gather_weights")
    w_r, w_t, metapad, cw = _repack_weights(all_in, all_small)
    w_out_full = all_out.reshape(D_MODEL, D_MODEL)
    w_pw_full = all_pw.reshape(D_CONV, D_CONV)

    grad_x, pc = _local_step(x[0], loss_target[0], metapad, cw, w_r, w_t, w_pw_full, w_out_full, ln_in_g, ln_in_b,
                             b_f[0], conv_b[0], ln_conv_g[0], ln_conv_b[0], ln_out_g[0], ln_out_b[0])

    g_in8, g_in8_b = _unpack_dw_in(pc["w_in_rm"], pc["w_in_t"])
    g8s = [g_in8, pc["w_out"].reshape(N_DEV, n_out_sh, D_MODEL), pc["w_pw"].reshape(N_DEV, n_pw_sh, D_CONV)]
    from_sibling, all_small_g = _exchange_sibling([g_in8_b] + g8s[1:], _pack_small(pc))
    p32s, pbs = _rs_add_sibling(g8s, from_sibling, jnp.reshape(mc, (1,)).astype(jnp.int32))
    from_chips = _exchange_chips(pbs)
    g_w_in, g_w_out, g_w_pw = _rs_add_chips(p32s, from_chips, jnp.reshape(2 * mx + my, (1,)).astype(jnp.int32))

    sm = _sum_small(all_small_g)
    grads = {
        "meta": lax.dynamic_slice_in_dim(sm["metapad"], me * n_meta_sh, n_meta_sh, axis=1),
        "ln_in_g": sm["ln_in_g"].reshape(D_MODEL), "ln_in_b": sm["ln_in_b"].reshape(D_MODEL), "w_in": g_w_in.T[None],
        "b_f": sm["b_f"][:, :N_HEADS],
        "conv_w": lax.dynamic_slice_in_dim(sm["conv_w"], me * n_cw_sh, n_cw_sh, axis=1)[None, :CONV_WIDTH],
        "conv_b": sm["conv_b"], "ln_conv_g": sm["ln_conv_g"], "ln_conv_b": sm["ln_conv_b"], "w_pw": g_w_pw,
        "w_out": g_w_out, "ln_out_g": sm["ln_out_g"], "ln_out_b": sm["ln_out_b"]}
    loss_all = sm["loss"][0, 0]

    weights = dict(meta=meta, ln_in_g=ln_in_g, ln_in_b=ln_in_b, w_in=w_in, b_f=b_f, conv_w=conv_w, conv_b=conv_b,
                   ln_conv_g=ln_conv_g, ln_conv_b=ln_conv_b, w_pw=w_pw, w_out=w_out, ln_out_g=ln_out_g,
                   ln_out_b=ln_out_b)
    moms = dict(meta=m_meta, ln_in_g=m_ln_in_g, ln_in_b=m_ln_in_b, w_in=m_w_in, b_f=m_b_f, conv_w=m_conv_w,
                conv_b=m_conv_b, ln_conv_g=m_ln_conv_g, ln_conv_b=m_ln_conv_b, w_pw=m_w_pw, w_out=m_w_out,
                ln_out_g=m_ln_out_g, ln_out_b=m_ln_out_b)
    vels = dict(meta=v_meta, ln_in_g=v_ln_in_g, ln_in_b=v_ln_in_b, w_in=v_w_in, b_f=v_b_f, conv_w=v_conv_w,
                conv_b=v_conv_b, ln_conv_g=v_ln_conv_g, ln_conv_b=v_ln_conv_b, w_pw=v_w_pw, w_out=v_w_out,
                ln_out_g=v_ln_out_g, ln_out_b=v_ln_out_b)

    def to_kernel(name, a):
        if name == "w_in":
            return a[0].T
        return a.reshape(1, -1) if a.ndim == 1 else a

    def from_kernel(name, a):
        return a.T[None] if name == "w_in" else a.reshape(weights[name].shape)

    upd = _adamw(*[[to_kernel(n, d[n]) for n in W_NAMES] for d in (weights, grads, moms, vels)])
    deltas, new_m, new_v = ([from_kernel(n, a) for n, a in zip(W_NAMES, part)] for part in upd)
    return (loss_all, grad_x[None], *[grads[n] for n in W_NAMES], *deltas, *new_m, *new_v)
```

```python
import jax
import jax.numpy as jnp
import numpy as np
from jax import lax
from jax.experimental import pallas as pl
from jax.experimental.pallas import tpu as pltpu

F32 = jnp.float32
BF16 = jnp.bfloat16

D_MODEL = 1024
D_ATTN = 512
D_CONV = 512
N_HEADS = 8
HEAD_DIM = 64
N_META = 16
CONV_WIDTH = 31
LN_EPS = 1e-5
ALPHA = 2.0 ** 0.25
SCALE = HEAD_DIM ** -0.5
LOG2E = 1.4426950408889634
ADAM_LR, ADAM_B1, ADAM_B2, ADAM_EPS, ADAM_WD, ADAM_STEP = 0.001, 0.9, 0.999, 1e-08, 0.01, 10

N_DEV = 8
D_IN = 3592
SHARD_IN = D_IN // N_DEV
TILE = 256
PAD = TILE - N_META
HALO = 32
SHIFT_ROWS = TILE + HALO
EXT_ROWS = SHIFT_ROWS + 8
NEG = -1e30
LANES = 128
W_COLS = 7 * 512 + LANES
OFF_GA_R, OFF_F_R = 1536, 3584
MIB = 1024 * 1024


def _cp(vmem_mib, sem=None):
    kw = dict(vmem_limit_bytes=vmem_mib * MIB)
    if sem is not None:
        kw["dimension_semantics"] = sem
    return pltpu.CompilerParams(**kw)


def _sigmoid(x):
    return 1.0 / (1.0 + jnp.exp(-x))


def _silu_and_grad(x):
    s = _sigmoid(x)
    return x * s, s * (1.0 + x * (1.0 - s))


def _ln_stats(x):
    mu = jnp.mean(x, axis=-1, keepdims=True)
    xc = x - mu
    var = jnp.mean(xc * xc, axis=-1, keepdims=True)
    rstd = lax.rsqrt(var + LN_EPS)
    return xc * rstd, rstd


def _ln_bwd(dy, xhat, rstd, g):
    dxh = dy * g
    m1 = jnp.mean(dxh, axis=-1, keepdims=True)
    m2 = jnp.mean(dxh * xhat, axis=-1, keepdims=True)
    return rstd * (dxh - m1 - xhat * m2)


def _row_spec(cols, shift=False):
    if shift:
        return pl.BlockSpec((TILE, cols), lambda i: (jnp.maximum(i - 1, 0), 0))
    return pl.BlockSpec((TILE, cols), lambda i: (i, 0))


def _full_spec(shape):
    nd = len(shape)
    return pl.BlockSpec(shape, lambda i: (0,) * nd)


def _t3_spec(ch):
    return pl.BlockSpec((1, ch, TILE), lambda i: (i, 0, 0))


def _proj_fwd(x, metapad, g_in, b_in, w_r, nt):
    lp = nt * TILE

    def body(x_ref, mp_ref, g_ref, b_ref, w_ref, hb_ref, qT_ref, kT_ref, vT_ref, k_ref, v_ref,
             ga_ref, u_ref, ug_ref, gc_ref, fl_ref):
        i = pl.program_id(0)
        x0 = jnp.where(i == 0, mp_ref[...], x_ref[...])
        xhat, _ = _ln_stats(x0)
        hb = (xhat * g_ref[...] + b_ref[...]).astype(BF16)
        hb_ref[...] = hb

        def sec(off, n=512):
            return jnp.dot(hb, w_ref[:, off:off + n], preferred_element_type=F32)

        qT_ref[0] = (sec(0) * (SCALE * LOG2E)).T.astype(BF16)
        k = sec(512)
        kT_ref[0] = k.T.astype(BF16)
        k_ref[...] = k.astype(BF16)
        v = sec(1024)
        vT_ref[0] = v.T.astype(BF16)
        v_ref[...] = v.astype(BF16)
        ga_ref[...] = sec(OFF_GA_R).astype(BF16)
        u_ref[...] = sec(OFF_GA_R + 512).astype(BF16)
        ug_ref[...] = sec(OFF_GA_R + 1024).astype(BF16)
        gc_ref[...] = sec(OFF_GA_R + 1536).astype(BF16)
        fl_ref[...] = sec(OFF_F_R, LANES)

    t3 = jax.ShapeDtypeStruct((nt, 512, TILE), BF16)
    rm = lambda dt: jax.ShapeDtypeStruct((lp, 512), dt)
    return pl.pallas_call(
        body, name="proj_fwd", grid=(nt,),
        in_specs=[_row_spec(D_MODEL, shift=True), _full_spec((TILE, D_MODEL)), _full_spec((1, D_MODEL)),
                  _full_spec((1, D_MODEL)), _full_spec((D_MODEL, W_COLS))],
        out_specs=[_row_spec(D_MODEL), _t3_spec(512), _t3_spec(512), _t3_spec(512), _row_spec(512), _row_spec(512),
                   _row_spec(512), _row_spec(512), _row_spec(512), _row_spec(512), _row_spec(LANES)],
        out_shape=[jax.ShapeDtypeStruct((lp, D_MODEL), BF16), t3, t3, t3, rm(BF16), rm(BF16),
                   rm(BF16), rm(BF16), rm(BF16), rm(BF16), jax.ShapeDtypeStruct((lp, LANES), F32)],
        compiler_params=_cp(56, ("arbitrary",)),
    )(x, metapad, g_in, b_in, w_r)


def _row_mask(i, shape):
    r = lax.broadcasted_iota(jnp.int32, shape, 0)
    return (r >= PAD) | (i > 0)


def _cumsum_fwd(fl, bf_pad, nt):
    lp = nt * TILE

    def body(fl_ref, bf_ref, kx_ref, carry):
        i = pl.program_id(0)

        @pl.when(i == 0)
        def _():
            carry[...] = jnp.zeros_like(carry)

        z = fl_ref[...] + bf_ref[...]
        lf = jnp.minimum(z, 0.0) - jnp.log(1.0 + jnp.exp(-jnp.abs(z)))
        lane = lax.broadcasted_iota(jnp.int32, (TILE, LANES), 1)
        real = _row_mask(i, (TILE, LANES))
        lf = jnp.where(real & (lane < N_HEADS), lf, 0.0)
        r = lax.broadcasted_iota(jnp.int32, (TILE, TILE), 0)
        c = lax.broadcasted_iota(jnp.int32, (TILE, TILE), 1)
        tril = (c <= r).astype(F32)
        cs = jnp.dot(tril, lf, precision=lax.Precision.HIGHEST, preferred_element_type=F32) + carry[...]
        carry[...] = cs[TILE - 1:TILE, :]
        bias = jnp.where(real, cs * (-LOG2E), NEG)
        hi = bias.astype(BF16).astype(F32)
        mid = (bias - hi).astype(BF16).astype(F32)
        lo = (bias - hi - mid).astype(BF16).astype(F32)
        for p in range(N_HEADS // 2):
            out = jnp.zeros((TILE, LANES), F32)
            for hh in range(2):
                for part, piece in enumerate((hi, mid, lo)):
                    dst, src = 3 * hh + part, 2 * p + hh
                    moved = piece if dst == src else pltpu.roll(piece, (dst - src) % LANES, 1)
                    out = jnp.where(lane == dst, moved, out)
            kx_ref[p] = out.astype(BF16)

    return pl.pallas_call(
        body, name="cumsum_fwd", grid=(nt,),
        in_specs=[_row_spec(LANES), _full_spec((1, LANES))],
        out_specs=pl.BlockSpec((N_HEADS // 2, TILE, LANES), lambda i: (0, i, 0)),
        out_shape=jax.ShapeDtypeStruct((N_HEADS // 2, lp, LANES), BF16),
        scratch_shapes=[pltpu.VMEM((1, LANES), F32)],
        compiler_params=_cp(32, ("arbitrary",)),
    )(fl, bf_pad)


def _head_rows(blk, hh):
    r = lax.broadcasted_iota(jnp.int32, blk.shape, 0)
    return jnp.where((r >= hh * HEAD_DIM) & (r < (hh + 1) * HEAD_DIM), blk, jnp.zeros_like(blk))


def _two_heads(blk):
    return jnp.concatenate([_head_rows(blk, 0), _head_rows(blk, 1)], axis=1)


def _bias_rows():
    r = lax.broadcasted_iota(jnp.int32, (LANES, 2 * TILE), 0)
    c = lax.broadcasted_iota(jnp.int32, (LANES, 2 * TILE), 1)
    return jnp.where(((r < 3) & (c < TILE)) | ((r >= 3) & (r < 6) & (c >= TILE)), 1.0, 0.0).astype(BF16)


def _diag_mask(s):
    kpos = lax.broadcasted_iota(jnp.int32, (TILE, TILE), 0)
    qpos = lax.broadcasted_iota(jnp.int32, (TILE, TILE), 1)
    return jnp.where(kpos <= qpos, s, NEG)


def _stream(n, first, nxt, scores, update, unroll=4):
    if n == 0:
        return
    scores(first, 0)

    def step(_, idx):
        for _u in range(unroll):
            idx_b = nxt(idx)
            scores(idx_b, 1)
            update(idx, 0)
            idx = nxt(idx_b)
            scores(idx, 0)
            update(idx_b, 1)
        return idx

    steps = (n - 1) // (2 * unroll)
    idx = lax.fori_loop(0, steps, step, first)
    left = n - 2 * unroll * steps
    for r in range(left - 1):
        idx_b = nxt(idx)
        scores(idx_b, (r + 1) % 2)
        update(idx, r % 2)
        idx = idx_b
    update(idx, (left - 1) % 2)


def _next_below_diagonal(idx):
    i, j = idx
    wrap = j + 1 >= i
    return jnp.where(wrap, i + 1, i), jnp.where(wrap, 0, j + 1)


def _tile_rows(t):
    return pl.ds(pl.multiple_of(t * TILE, TILE), TILE)


def _two_streams(nt):
    load, group = [0, 0], {}
    for i in sorted(range(1, nt), reverse=True):
        g = 0 if load[0] <= load[1] else 1
        group[i] = g
        load[g] += i
    rows = [[(i, i, j) for i in range(1, nt) if group[i] == g for j in range(i)] for g in range(2)]
    length = max(len(r) for r in rows)
    rows = [r + [(nt, 0, 0)] * (length - len(r)) for r in rows]
    return group, np.asarray(rows, np.int32).reshape(2, -1), length


def _attn_fwd(qT3, k, kx3, vT3, nt):
    lp = nt * TILE
    npair = N_HEADS // 2
    group, table, n_stream = _two_streams(nt)

    def body(tab_ref, qT_ref, k_ref, kx_ref, vT_ref, oT_ref, o_ref, lse_ref, sbuf, m_0, l_0, acc_0, m_1, l_1, acc_1):
        ones = _bias_rows()
        states = ((m_0, l_0, acc_0), (m_1, l_1, acc_1))

        def scores(i, j, slot):
            qcat = jnp.concatenate([_two_heads(qT_ref[i]), ones], axis=0)
            kext = jnp.concatenate([k_ref[_tile_rows(j), :], kx_ref[0, _tile_rows(j), :]], axis=1)
            sbuf[slot] = jnp.dot(kext, qcat, preferred_element_type=F32)

        def update(st, j, slot, state, diag):
            m_s, l_s, acc_s = state
            for hh in range(2):
                s = sbuf[slot, :, hh * TILE:(hh + 1) * TILE]
                vj = vT_ref[j, hh * HEAD_DIM:(hh + 1) * HEAD_DIM, :]
                if diag:
                    s = _diag_mask(s)
                    m_new = jnp.max(s, axis=0, keepdims=True)
                    p = jnp.exp2(s - m_new)
                    l_s[st, hh] = jnp.sum(p, axis=0, keepdims=True)
                    acc_s[st, hh] = jnp.dot(vj, p.astype(BF16), preferred_element_type=F32)
                else:
                    m_prev = m_s[st, hh]
                    m_new = jnp.maximum(m_prev, jnp.max(s, axis=0, keepdims=True))
                    a = jnp.exp2(m_prev - m_new)
                    p = jnp.exp2(s - m_new)
                    l_s[st, hh] = a * l_s[st, hh] + jnp.sum(p, axis=0, keepdims=True)
                    acc_s[st, hh] = a * acc_s[st, hh] + jnp.dot(vj, p.astype(BF16), preferred_element_type=F32)
                m_s[st, hh] = m_new

        _stream(nt, jnp.int32(0), lambda t: t + 1, lambda t, slot: scores(t, t, slot),
                lambda t, slot: update(t, t, slot, states[0], True))
        for dst, src in zip(states[1], states[0]):
            dst[0:nt] = src[0:nt]
        for m_s, l_s, acc_s in states:
            m_s[nt] = jnp.full(m_s.shape[1:], NEG, F32)
            l_s[nt] = jnp.zeros(l_s.shape[1:], F32)
            acc_s[nt] = jnp.zeros(acc_s.shape[1:], F32)

        def entry(g, t):
            return tab_ref[g, 3 * t], tab_ref[g, 3 * t + 1], tab_ref[g, 3 * t + 2]

        def scores2(t, slot):
            for g in range(2):
                _, qi, kj = entry(g, t)
                scores(qi, kj, 2 * g + slot)

        def update2(t, slot):
            for g in range(2):
                st, _, kj = entry(g, t)
                update(st, kj, 2 * g + slot, states[g], False)

        _stream(n_stream, jnp.int32(0), lambda t: t + 1, scores2, update2)

        for i in range(nt):
            m_s, l_s, acc_s = states[group.get(i, 0)]
            for hh in range(2):
                l = l_s[i, hh]
                oT_ref[i, hh * HEAD_DIM:(hh + 1) * HEAD_DIM, :] = acc_s[i, hh] / l
                lse_ref[0, i, hh:hh + 1, :] = m_s[i, hh] + jnp.log(l) * LOG2E
            o_ref[i * TILE:(i + 1) * TILE, :] = oT_ref[i].T.astype(BF16)

    blk_t = pl.BlockSpec((nt, LANES, TILE), lambda p, tab: (0, p, 0))
    blk_rm = pl.BlockSpec((lp, LANES), lambda p, tab: (0, p))
    blk_px = pl.BlockSpec((1, lp, LANES), lambda p, tab: (p, 0, 0))
    blk_st = pl.BlockSpec((1, nt, 8, TILE), lambda p, tab: (p, 0, 0, 0))
    state = [pltpu.VMEM((nt + 1, 2, 1, TILE), F32), pltpu.VMEM((nt + 1, 2, 1, TILE), F32),
             pltpu.VMEM((nt + 1, 2, HEAD_DIM, TILE), F32)]
    grid_spec = pltpu.PrefetchScalarGridSpec(
        num_scalar_prefetch=1, grid=(npair,), in_specs=[blk_t, blk_rm, blk_px, blk_t],
        out_specs=[blk_t, blk_rm, blk_st], scratch_shapes=[pltpu.VMEM((4, TILE, 2 * TILE), F32)] + state + state)
    return pl.pallas_call(
        body, name="attn_fwd", grid_spec=grid_spec,
        out_shape=[jax.ShapeDtypeStruct((nt, D_ATTN, TILE), F32),
                   jax.ShapeDtypeStruct((lp, D_ATTN), BF16),
                   jax.ShapeDtypeStruct((npair, nt, 8, TILE), F32)],
        compiler_params=_cp(60, ("arbitrary",)),
    )(jnp.asarray(table), qT3, k, kx3, vT3)


def _attn_bwd(qT3, kT3, k, kx3, v, oT3, doT3, lse4, nt):
    lp = nt * TILE
    npair = N_HEADS // 2

    def body(qT_ref, kT_ref, k_ref, kx_ref, v_ref, oT_ref, doT_ref, lse_ref,
             dqT_ref, dkT_ref, dvT_ref, dck_ref, dcq_ref, sbuf, dpbuf, dq_s, dk_s, dv_s, dc_s, tp_s, tds_s):
        ones = _bias_rows()

        def scores(idx, slot):
            i, j = idx
            qcat = jnp.concatenate([_two_heads(qT_ref[i]), ones], axis=0)
            kext = jnp.concatenate([k_ref[_tile_rows(j), :], kx_ref[0, _tile_rows(j), :]], axis=1)
            sbuf[slot] = jnp.dot(kext, qcat, preferred_element_type=F32)
            dpbuf[slot] = jnp.dot(v_ref[_tile_rows(j), :], _two_heads(doT_ref[i]), preferred_element_type=F32)

        def update(idx, slot, diag):
            i, j = idx
            for hh in range(2):
                hs = slice(hh * HEAD_DIM, (hh + 1) * HEAD_DIM)
                s = sbuf[slot, :, hh * TILE:(hh + 1) * TILE]
                if diag:
                    s = _diag_mask(s)
                p = jnp.exp2(s - lse_ref[0, i, hh:hh + 1, :])
                doh = doT_ref[i, hs, :]
                delta = jnp.sum(doh.astype(F32) * oT_ref[i, hs, :], axis=0, keepdims=True)
                ds = p * (dpbuf[slot, :, hh * TILE:(hh + 1) * TILE] - delta)
                dsb = ds.astype(BF16)
                tp_s[hh] = p.astype(BF16).T
                tds_s[hh] = dsb.T
                dv = jnp.dot(doh, tp_s[hh], preferred_element_type=F32)
                dk = jnp.dot(qT_ref[i, hs, :], tds_s[hh], preferred_element_type=F32)
                dq = jnp.dot(kT_ref[j, hs, :], dsb, preferred_element_type=F32)
                dc = ds[:, :LANES] + ds[:, LANES:]
                dcq = jnp.sum(ds, axis=0, keepdims=True)
                if diag:
                    dv_s[j, hh] = dv
                    dk_s[j, hh] = dk
                    dc_s[j, hh] = dc
                    dq_s[i, hs, :] = dq
                    dcq_ref[0, i, hh:hh + 1, :] = dcq
                else:
                    dv_s[j, hh] += dv
                    dk_s[j, hh] += dk
                    dc_s[j, hh] += dc
                    dq_s[i, hs, :] += dq
                    dcq_ref[0, i, hh:hh + 1, :] += dcq

        dcq_ref[...] = jnp.zeros_like(dcq_ref)
        zero = jnp.int32(0)
        _stream(nt, (zero, zero), lambda idx: (idx[0] + 1, idx[1] + 1), scores,
                lambda idx, slot: update(idx, slot, True))
        _stream(nt * (nt - 1) // 2, (zero + 1, zero), _next_below_diagonal, scores,
                lambda idx, slot: update(idx, slot, False))

        lane = lax.broadcasted_iota(jnp.int32, (TILE, LANES), 1)

        def finish(t, carry):
            dck = jnp.zeros((TILE, LANES), F32)
            for hh in range(2):
                hs = slice(hh * HEAD_DIM, (hh + 1) * HEAD_DIM)
                dkT_ref[t, hs, :] = (dk_s[t, hh] * (1.0 / LOG2E)).astype(BF16)
                dvT_ref[t, hs, :] = dv_s[t, hh].astype(BF16)
                dck = jnp.where(lane == hh, -jnp.sum(dc_s[t, hh], axis=1, keepdims=True), dck)
            dck_ref[0, _tile_rows(t), :] = dck
            dqT_ref[t] = (dq_s[t] * SCALE).astype(BF16)
            return carry

        lax.fori_loop(0, nt, finish, 0)

    blk_t = pl.BlockSpec((nt, LANES, TILE), lambda p: (0, p, 0))
    blk_rm = pl.BlockSpec((lp, LANES), lambda p: (0, p))
    blk_px = pl.BlockSpec((1, lp, LANES), lambda p: (p, 0, 0))
    blk_st = pl.BlockSpec((1, nt, 8, TILE), lambda p: (p, 0, 0, 0))
    t3 = jax.ShapeDtypeStruct((nt, D_ATTN, TILE), BF16)
    return pl.pallas_call(
        body, name="attn_bwd", grid=(npair,),
        in_specs=[blk_t, blk_t, blk_rm, blk_px, blk_rm, blk_t, blk_t, blk_st],
        out_specs=[blk_t, blk_t, blk_t, blk_px, blk_st],
        out_shape=[t3, t3, t3, jax.ShapeDtypeStruct((npair, lp, LANES), F32),
                   jax.ShapeDtypeStruct((npair, nt, 8, TILE), F32)],
        scratch_shapes=[pltpu.VMEM((2, TILE, 2 * TILE), F32), pltpu.VMEM((2, TILE, 2 * TILE), F32),
                        pltpu.VMEM((nt, LANES, TILE), F32), pltpu.VMEM((nt, 2, HEAD_DIM, TILE), F32),
                        pltpu.VMEM((nt, 2, HEAD_DIM, TILE), F32), pltpu.VMEM((nt, 2, TILE, LANES), F32),
                        pltpu.VMEM((2, TILE, TILE), BF16), pltpu.VMEM((2, TILE, TILE), BF16)],
        compiler_params=_cp(60, ("arbitrary",)),
    )(qT3, kT3, k, kx3, v, oT3, doT3, lse4)


def _glu(u, ug, i):
    return jnp.where(_row_mask(i, u.shape), u.astype(F32) * _sigmoid(ug.astype(F32)), 0.0)


def _shifted_copies(dst, src):
    for ph in range(8):
        dst[ph] = src[ph:ph + SHIFT_ROWS, :]


def _tap_window(sh, off, lanes, row0=0, rows=TILE):
    base = (off // 8) * 8 + row0
    return sh[off % 8, base:base + rows, lanes]


def _conv_fwd(u, ug, conv_w, conv_b, g, b, w_pw, nt):
    lp = nt * TILE

    def body(u_ref, ug_ref, up_ref, ugp_ref, w_ref, cb_ref, g_ref, b_ref, wpw_ref,
             co_ref, hc_ref, pw_ref, ext, sh):
        i = pl.program_id(0)
        prev = _glu(up_ref[...], ugp_ref[...], i - 1)
        ext[0:HALO, :] = jnp.where(i > 0, prev[TILE - HALO:, :], 0.0)
        ext[HALO:HALO + TILE, :] = _glu(u_ref[...], ug_ref[...], i)
        ext[HALO + TILE:, :] = jnp.zeros((8, D_CONV), F32)
        _shifted_copies(sh, ext)
        for lb in range(D_CONV // LANES):
            lanes = slice(lb * LANES, (lb + 1) * LANES)
            acc = jnp.zeros((TILE, LANES), F32) + cb_ref[:, lanes]
            for t in range(CONV_WIDTH):
                off = HALO - (CONV_WIDTH - 1) + t
                acc = acc + w_ref[t:t + 1, lanes] * _tap_window(sh, off, lanes)
            co_ref[:, lanes] = acc
        xhat, _ = _ln_stats(co_ref[...])
        a, _ = _silu_and_grad(xhat * g_ref[...] + b_ref[...])
        hc = a.astype(BF16)
        hc_ref[...] = hc
        pw_ref[...] = jnp.dot(hc, wpw_ref[...], preferred_element_type=F32).astype(BF16)

    rm = lambda dt: jax.ShapeDtypeStruct((lp, D_CONV), dt)
    return pl.pallas_call(
        body, name="conv_fwd", grid=(nt,),
        in_specs=[_row_spec(512), _row_spec(512), _row_spec(512, shift=True), _row_spec(512, shift=True),
                  _full_spec((32, 512)), _full_spec((1, 512)), _full_spec((1, 512)), _full_spec((1, 512)),
                  _full_spec((512, 512))],
        out_specs=[_row_spec(512), _row_spec(512), _row_spec(512)],
        out_shape=[rm(F32), rm(BF16), rm(BF16)],
        scratch_shapes=[pltpu.VMEM((EXT_ROWS, D_CONV), F32), pltpu.VMEM((8, SHIFT_ROWS, D_CONV), F32)],
        compiler_params=_cp(40, ("arbitrary",)),
    )(u, ug, u, ug, conv_w, conv_b, g, b, w_pw)


def _out_fwd(o, ga, pw, gc, x, metapad, g_in, b_in, w_out, g_out, b_out, target, nt):
    lp = nt * TILE

    def body(o_ref, ga_ref, pw_ref, gc_ref, x_ref, mp_ref, gi_ref, bi_ref, wo_ref, go_ref, bo_ref, t_ref,
             y_ref, dz_ref, loss_ref, dgo_ref, dbo_ref):
        i = pl.program_id(0)

        @pl.when(i == 0)
        def _():
            loss_ref[...] = jnp.zeros_like(loss_ref)
            dgo_ref[...] = jnp.zeros_like(dgo_ref)
            dbo_ref[...] = jnp.zeros_like(dbo_ref)

        x0 = jnp.where(i == 0, mp_ref[...], x_ref[...])
        xhat, _ = _ln_stats(x0)
        h = xhat * gi_ref[...] + bi_ref[...]
        ya, _ = _silu_and_grad(ga_ref[...].astype(F32))
        yc, _ = _silu_and_grad(gc_ref[...].astype(F32))
        ya = (o_ref[...].astype(F32) * ya).astype(BF16)
        yc = (pw_ref[...].astype(F32) * yc).astype(BF16)
        y_ref[:, :D_ATTN] = ya
        y_ref[:, D_ATTN:] = yc
        z = ALPHA * h + jnp.dot(ya, wo_ref[:D_ATTN, :], preferred_element_type=F32) \
            + jnp.dot(yc, wo_ref[D_ATTN:, :], preferred_element_type=F32)
        zhat, rstd = _ln_stats(z)
        out = zhat * go_ref[...] + bo_ref[...]
        live = (i > 0).astype(F32)
        err = (out - t_ref[...]) * live
        dout = err * (1.0 / D_MODEL)
        loss_ref[...] += 0.5 * jnp.sum(jnp.sum(err * dout, axis=0, keepdims=True), axis=1, keepdims=True)
        dgo_ref[...] += jnp.sum(dout * zhat, axis=0, keepdims=True)
        dbo_ref[...] += jnp.sum(dout, axis=0, keepdims=True)
        dz_ref[...] = _ln_bwd(dout, zhat, rstd, go_ref[...])

    return pl.pallas_call(
        body, name="out_fwd", grid=(nt,),
        in_specs=[_row_spec(512), _row_spec(512), _row_spec(512), _row_spec(512),
                  _row_spec(D_MODEL, shift=True), _full_spec((TILE, D_MODEL)), _full_spec((1, D_MODEL)),
                  _full_spec((1, D_MODEL)), _full_spec((D_MODEL, D_MODEL)), _full_spec((1, D_MODEL)),
                  _full_spec((1, D_MODEL)), _row_spec(D_MODEL, shift=True)],
        out_specs=[_row_spec(D_MODEL), _row_spec(D_MODEL), _full_spec((1, LANES)), _full_spec((1, D_MODEL)),
                   _full_spec((1, D_MODEL))],
        out_shape=[jax.ShapeDtypeStruct((lp, D_MODEL), BF16), jax.ShapeDtypeStruct((lp, D_MODEL), F32),
                   jax.ShapeDtypeStruct((1, LANES), F32), jax.ShapeDtypeStruct((1, D_MODEL), F32),
                   jax.ShapeDtypeStruct((1, D_MODEL), F32)],
        compiler_params=_cp(40, ("arbitrary",)),
    )(o, ga, pw, gc, x, metapad, g_in, b_in, w_out, g_out, b_out, target)


def _out_bwd(dz, y, o, ga, pw, gc, w_out, hc, co, w_pw, g_cv, b_cv, nt):
    lp = nt * TILE

    def body(dz_ref, y_ref, o_ref, ga_ref, pw_ref, gc_ref, wo_ref, hc_ref, co_ref, wpw_ref, g_ref, b_ref,
             doT_ref, dga_ref, dgc_ref, dwo_ref, dco_ref, dwpw_ref, dg_ref, db_ref, dcb_ref):
        i = pl.program_id(0)

        @pl.when(i == 0)
        def _():
            dwo_ref[...] = jnp.zeros_like(dwo_ref)
            dwpw_ref[...] = jnp.zeros_like(dwpw_ref)
            dg_ref[...] = jnp.zeros_like(dg_ref)
            db_ref[...] = jnp.zeros_like(db_ref)
            dcb_ref[...] = jnp.zeros_like(dcb_ref)

        dzb = dz_ref[...].astype(BF16)
        nt_dims = (((1,), (1,)), ((), ()))
        tn_dims = (((0,), (0,)), ((), ()))
        dya = lax.dot_general(dzb, wo_ref[:D_ATTN, :], nt_dims, preferred_element_type=F32)
        dyc = lax.dot_general(dzb, wo_ref[D_ATTN:, :], nt_dims, preferred_element_type=F32)
        sa, sga = _silu_and_grad(ga_ref[...].astype(F32))
        sc, sgc = _silu_and_grad(gc_ref[...].astype(F32))
        doT_ref[0] = (dya * sa).T.astype(BF16)
        dga_ref[...] = (dya * o_ref[...].astype(F32) * sga).astype(BF16)
        dpw_b = (dyc * sc).astype(BF16)
        dgc_ref[...] = (dyc * pw_ref[...].astype(F32) * sgc).astype(BF16)
        dwo_ref[...] += lax.dot_general(y_ref[...], dzb, tn_dims, preferred_element_type=F32)

        dhc = lax.dot_general(dpw_b, wpw_ref[...], nt_dims, preferred_element_type=F32)
        xhat, rstd = _ln_stats(co_ref[...])
        _, sg = _silu_and_grad(xhat * g_ref[...] + b_ref[...])
        dln = dhc * sg
        dg_ref[...] += jnp.sum(dln * xhat, axis=0, keepdims=True)
        db_ref[...] += jnp.sum(dln, axis=0, keepdims=True)
        dco = _ln_bwd(dln, xhat, rstd, g_ref[...])
        dco_ref[...] = dco
        dcb_ref[...] += jnp.sum(dco, axis=0, keepdims=True)
        dwpw_ref[...] += lax.dot_general(hc_ref[...], dpw_b, tn_dims, preferred_element_type=F32)

    rm = jax.ShapeDtypeStruct((lp, 512), BF16)
    vec = jax.ShapeDtypeStruct((1, D_CONV), F32)
    return pl.pallas_call(
        body, name="out_bwd", grid=(nt,),
        in_specs=[_row_spec(D_MODEL), _row_spec(D_MODEL), _row_spec(512), _row_spec(512), _row_spec(512),
                  _row_spec(512), _full_spec((D_MODEL, D_MODEL)), _row_spec(512), _row_spec(512),
                  _full_spec((512, 512)), _full_spec((1, 512)), _full_spec((1, 512))],
        out_specs=[_t3_spec(512), _row_spec(512), _row_spec(512), _full_spec((D_MODEL, D_MODEL)), _row_spec(512),
                   _full_spec((512, 512)), _full_spec((1, 512)), _full_spec((1, 512)), _full_spec((1, 512))],
        out_shape=[jax.ShapeDtypeStruct((nt, 512, TILE), BF16), rm, rm, jax.ShapeDtypeStruct((D_MODEL, D_MODEL), F32),
                   jax.ShapeDtypeStruct((lp, D_CONV), F32), jax.ShapeDtypeStruct((512, 512), F32), vec, vec, vec],
        compiler_params=_cp(56, ("arbitrary",)),
    )(dz, y, o, ga, pw, gc, w_out, hc, co, w_pw, g_cv, b_cv)


def _conv_bwd_taps(dco, u, ug, conv_w, nt):
    lp = nt * TILE

    def body(dco_ref, dcon_ref, u_ref, ug_ref, up_ref, ugp_ref, w3_ref, du_ref, dug_ref, dw_ref, ext, dext, sh, dsh,
             dhg_s, dw_s):
        i = pl.program_id(0)

        @pl.when(i == 0)
        def _():
            dw_s[...] = jnp.zeros_like(dw_s)

        prev = _glu(up_ref[...], ugp_ref[...], i - 1)
        ext[0:HALO, :] = jnp.where(i > 0, prev[TILE - HALO:, :], 0.0)
        ext[HALO:HALO + TILE, :] = _glu(u_ref[...], ug_ref[...], i)
        ext[HALO + TILE:, :] = jnp.zeros((8, D_CONV), F32)
        dext[0:TILE, :] = dco_ref[...]
        dext[TILE:TILE + HALO, :] = jnp.where(i < nt - 1, dcon_ref[0:HALO, :], 0.0)
        dext[TILE + HALO:, :] = jnp.zeros((8, D_CONV), F32)
        _shifted_copies(sh, ext)
        _shifted_copies(dsh, dext)
        stripe = 32

        def stripe_body(rb, carry):
            row0 = pl.multiple_of(rb * stripe, stripe)
            dco = dco_ref[pl.ds(row0, stripe), :]
            dhg = jnp.zeros((stripe, D_CONV), F32)
            for t in range(CONV_WIDTH):
                off = HALO - (CONV_WIDTH - 1) + t
                back = CONV_WIDTH - 1 - t
                prod = dco * sh[off % 8, pl.ds((off // 8) * 8 + row0, stripe), :]
                part = prod[0:8, :]
                for r8 in range(1, stripe // 8):
                    part = part + prod[8 * r8:8 * r8 + 8, :]
                dw_s[t] += part
                dhg = dhg + w3_ref[t] * dsh[back % 8, pl.ds((back // 8) * 8 + row0, stripe), :]
            dhg_s[pl.ds(row0, stripe), :] = dhg
            return carry

        lax.fori_loop(0, TILE // stripe, stripe_body, 0)

        @pl.when(i == nt - 1)
        def _():
            dw_ref[...] = jnp.sum(dw_s[...], axis=1)

        dhg = jnp.where(_row_mask(i, (TILE, D_CONV)), dhg_s[...], 0.0)
        sg = _sigmoid(ug_ref[...].astype(F32))
        du_ref[...] = (dhg * sg).astype(BF16)
        dug_ref[...] = (dhg * u_ref[...].astype(F32) * sg * (1.0 - sg)).astype(BF16)

    rm = jax.ShapeDtypeStruct((lp, D_CONV), BF16)
    nxt = pl.BlockSpec((TILE, 512), lambda i: (jnp.minimum(i + 1, nt - 1), 0))
    ext_t = pltpu.VMEM((EXT_ROWS, D_CONV), F32)
    sh_t = pltpu.VMEM((8, SHIFT_ROWS, D_CONV), F32)
    return pl.pallas_call(
        body, name="conv_bwd_taps", grid=(nt,),
        in_specs=[_row_spec(512), nxt, _row_spec(512), _row_spec(512), _row_spec(512, shift=True),
                  _row_spec(512, shift=True), _full_spec((32, 1, 512))],
        out_specs=[_row_spec(512), _row_spec(512), _full_spec((32, 512))],
        out_shape=[rm, rm, jax.ShapeDtypeStruct((32, D_CONV), F32)],
        scratch_shapes=[ext_t, ext_t, sh_t, sh_t, pltpu.VMEM((TILE, D_CONV), F32), pltpu.VMEM((32, 8, D_CONV), F32)],
        compiler_params=_cp(48, ("arbitrary",)),
    )(dco, dco, u, ug, u, ug, conv_w.reshape(32, 1, D_CONV))


def _cumsum_bwd(dck, dcq4, fl, bf_pad, nt):
    lp = nt * TILE

    def body(dck_ref, dcq_ref, fl_ref, bf_ref, dfl_ref, dbf_ref, carry):
        i = pl.program_id(0)
        tile = nt - 1 - i

        @pl.when(i == 0)
        def _():
            carry[...] = jnp.zeros_like(carry)
            dbf_ref[...] = jnp.zeros_like(dbf_ref)

        dc = jnp.zeros((TILE, LANES), F32)
        for p in range(N_HEADS // 2):
            dq_rows = jnp.concatenate([dcq_ref[p, 0], jnp.zeros((LANES - 8, TILE), F32)], axis=0)
            both = dck_ref[p] + dq_rows.T
            dc = dc + (both if p == 0 else pltpu.roll(both, 2 * p, 1))
        r = lax.broadcasted_iota(jnp.int32, (TILE, TILE), 0)
        c = lax.broadcasted_iota(jnp.int32, (TILE, TILE), 1)
        triu = (c >= r).astype(F32)
        dlf = jnp.dot(triu, dc, precision=lax.Precision.HIGHEST, preferred_element_type=F32) + carry[...]
        carry[...] = dlf[0:1, :]
        z = fl_ref[...] + bf_ref[...]
        lane = lax.broadcasted_iota(jnp.int32, (TILE, LANES), 1)
        dfl = jnp.where(_row_mask(tile, (TILE, LANES)) & (lane < N_HEADS), dlf * _sigmoid(-z), 0.0)
        dfl_ref[...] = dfl.astype(BF16)
        dbf_ref[...] += jnp.sum(dfl, axis=0, keepdims=True)

    rev = lambda i: (nt - 1 - i, 0)
    return pl.pallas_call(
        body, name="cumsum_bwd", grid=(nt,),
        in_specs=[pl.BlockSpec((N_HEADS // 2, TILE, LANES), lambda i: (0, nt - 1 - i, 0)),
                  pl.BlockSpec((N_HEADS // 2, 1, 8, TILE), lambda i: (0, nt - 1 - i, 0, 0)),
                  pl.BlockSpec((TILE, LANES), rev), _full_spec((1, LANES))],
        out_specs=[pl.BlockSpec((TILE, LANES), rev), _full_spec((1, LANES))],
        out_shape=[jax.ShapeDtypeStruct((lp, LANES), BF16), jax.ShapeDtypeStruct((1, LANES), F32)],
        scratch_shapes=[pltpu.VMEM((1, LANES), F32)],
        compiler_params=_cp(32, ("arbitrary",)),
    )(dck, dcq4, fl, bf_pad)


def _dw_rowmajor(hb, secs, nt):
    n = len(secs)

    def body(*refs):
        hb_ref, sec_refs, out_refs = refs[0], refs[1:1 + n], refs[1 + n:]
        i = pl.program_id(0)

        @pl.when(i == 0)
        def _():
            for o_ref in out_refs:
                o_ref[...] = jnp.zeros_like(o_ref)

        hb_t = hb_ref[...]
        for s_ref, o_ref in zip(sec_refs, out_refs):
            o_ref[...] += lax.dot_general(hb_t, s_ref[...], (((0,), (0,)), ((), ())), preferred_element_type=F32)

    return pl.pallas_call(
        body, name="dw_rowmajor", grid=(nt,),
        in_specs=[_row_spec(D_MODEL)] + [_row_spec(s.shape[1]) for s in secs],
        out_specs=[_full_spec((D_MODEL, s.shape[1])) for s in secs],
        out_shape=[jax.ShapeDtypeStruct((D_MODEL, s.shape[1]), F32) for s in secs],
        compiler_params=_cp(48, ("arbitrary",)),
    )(hb, *secs)


def _dw_transposed(hb, secs_t3, nt):
    n = len(secs_t3)

    def body(*refs):
        hb_ref, sec_refs, out_refs = refs[0], refs[1:1 + n], refs[1 + n:]
        i = pl.program_id(0)

        @pl.when(i == 0)
        def _():
            for o_ref in out_refs:
                o_ref[...] = jnp.zeros_like(o_ref)

        hb_t = hb_ref[...]
        for s_ref, o_ref in zip(sec_refs, out_refs):
            o_ref[...] += jnp.dot(s_ref[0], hb_t, preferred_element_type=F32)

    return pl.pallas_call(
        body, name="dw_transposed", grid=(nt,),
        in_specs=[_row_spec(D_MODEL)] + [_t3_spec(512) for _ in secs_t3],
        out_specs=[_full_spec((512, D_MODEL)) for _ in secs_t3],
        out_shape=[jax.ShapeDtypeStruct((512, D_MODEL), F32) for _ in secs_t3],
        compiler_params=_cp(40, ("arbitrary",)),
    )(hb, *secs_t3)


def _dh_bwd(secs, secs_t3, w_rm, w_t, dz, x, metapad, g_in, nt):
    n, m = len(secs), len(secs_t3)
    offs = OFF_GA_R + np.cumsum([0] + [s.shape[1] for s in secs])

    def body(*refs):
        sec_refs, t3_refs = refs[:n], refs[n:n + m]
        wrm_ref, wt_ref, dz_ref, x_ref, mp_ref, g_ref = refs[n + m:n + m + 6]
        dx_ref, dmeta_ref, dg_ref, db_ref = refs[n + m + 6:]
        i = pl.program_id(0)

        @pl.when(i == 0)
        def _():
            dg_ref[...] = jnp.zeros_like(dg_ref)
            db_ref[...] = jnp.zeros_like(db_ref)

        dh = ALPHA * dz_ref[...]
        for s_ref, lo, hi in zip(sec_refs, offs[:-1], offs[1:]):
            dh = dh + lax.dot_general(s_ref[...], wrm_ref[:, lo:hi], (((1,), (1,)), ((), ())),
                                      preferred_element_type=F32)
        for idx, t_ref in enumerate(t3_refs):
            dh = dh + lax.dot_general(t_ref[0], wt_ref[idx * 512:(idx + 1) * 512, :], (((0,), (0,)), ((), ())),
                                      preferred_element_type=F32)
        x0 = jnp.where(i == 0, mp_ref[...], x_ref[...])
        xhat, rstd = _ln_stats(x0)
        dg_ref[...] += jnp.sum(dh * xhat, axis=0, keepdims=True)
        db_ref[...] += jnp.sum(dh, axis=0, keepdims=True)
        dx = _ln_bwd(dh, xhat, rstd, g_ref[...])
        dx_ref[...] = dx

        @pl.when(i == 0)
        def _():
            dmeta_ref[...] = dx

    seq = (nt - 1) * TILE
    return pl.pallas_call(
        body, name="dh_bwd", grid=(nt,),
        in_specs=[_row_spec(s.shape[1]) for s in secs] + [_t3_spec(512) for _ in secs_t3]
        + [_full_spec(w_rm.shape), _full_spec(w_t.shape), _row_spec(D_MODEL), _row_spec(D_MODEL, shift=True),
           _full_spec((TILE, D_MODEL)), _full_spec((1, D_MODEL))],
        out_specs=[_row_spec(D_MODEL, shift=True), _full_spec((TILE, D_MODEL)), _full_spec((1, D_MODEL)),
                   _full_spec((1, D_MODEL))],
        out_shape=[jax.ShapeDtypeStruct((seq, D_MODEL), F32), jax.ShapeDtypeStruct((TILE, D_MODEL), F32),
                   jax.ShapeDtypeStruct((1, D_MODEL), F32), jax.ShapeDtypeStruct((1, D_MODEL), F32)],
        compiler_params=_cp(56, ("arbitrary",)),
    )(*secs, *secs_t3, w_rm, w_t, dz, x, metapad, g_in)


RB = 256
SMALL_ROWS = 48


def _repack_weights(all_in, all_small):
    n_cw = D_CONV // N_DEV

    def body(a_ref, s_ref, wr_ref, wt_ref, mp_ref, cw_ref):
        full = jnp.concatenate([a_ref[d].T[:, :SHARD_IN] for d in range(N_DEV)], axis=1)
        qkv = full[:, :1536]
        wr_ref[:, :1536] = qkv
        wr_ref[:, 1536:OFF_F_R] = full[:, 1544:]
        wr_ref[:, OFF_F_R:] = jnp.concatenate([full[:, 1536:1544], jnp.zeros((RB, LANES - N_HEADS), BF16)], axis=1)
        wt_ref[...] = qkv.T

        @pl.when(pl.program_id(0) == 0)
        def _():
            mp_ref[0:PAD, :] = jnp.zeros((PAD, D_MODEL), F32)
            mp_ref[PAD:, :] = jnp.concatenate([s_ref[d, 0:N_META, :] for d in range(N_DEV)], axis=1)
            cw_ref[...] = jnp.concatenate([s_ref[d, N_META:, 0:n_cw] for d in range(N_DEV)], axis=1)

    return pl.pallas_call(
        body, name="repack_weights", grid=(D_MODEL // RB,),
        in_specs=[pl.BlockSpec((N_DEV, 512, RB), lambda i: (0, 0, i)), _full_spec((N_DEV, SMALL_ROWS, LANES))],
        out_specs=[pl.BlockSpec((RB, W_COLS), lambda i: (i, 0)), pl.BlockSpec((1536, RB), lambda i: (0, i)),
                   _full_spec((TILE, D_MODEL)), _full_spec((32, D_CONV))],
        out_shape=[jax.ShapeDtypeStruct((D_MODEL, W_COLS), BF16), jax.ShapeDtypeStruct((1536, D_MODEL), BF16),
                   jax.ShapeDtypeStruct((TILE, D_MODEL), F32), jax.ShapeDtypeStruct((32, D_CONV), F32)],
        compiler_params=_cp(40, ("arbitrary",)),
    )(all_in, all_small)


def _unpack_dw_in(dw_rm, dw_t):
    def body(dga_ref, du_ref, dug_ref, dgc_ref, dfl_ref, dq_ref, dk_ref, dv_ref, out_ref, outb_ref):
        full = jnp.concatenate([dq_ref[...].T, dk_ref[...].T, dv_ref[...].T, dfl_ref[:, 0:N_HEADS], dga_ref[...],
                                du_ref[...], dug_ref[...], dgc_ref[...]], axis=1)
        pad = jnp.zeros((RB, 512 - SHARD_IN), F32)
        for d in range(N_DEV):
            blk = jnp.concatenate([full[:, SHARD_IN * d:SHARD_IN * (d + 1)], pad], axis=1).T
            out_ref[d] = blk
            outb_ref[d] = blk.astype(BF16)

    rm = pl.BlockSpec((RB, 512), lambda i: (i, 0))
    tr = pl.BlockSpec((512, RB), lambda i: (0, i))
    blocks = pl.BlockSpec((N_DEV, 512, RB), lambda i: (0, 0, i))
    return pl.pallas_call(
        body, name="unpack_dw_in", grid=(D_MODEL // RB,),
        in_specs=[rm, rm, rm, rm, pl.BlockSpec((RB, LANES), lambda i: (i, 0)), tr, tr, tr],
        out_specs=[blocks, blocks],
        out_shape=[jax.ShapeDtypeStruct((N_DEV, 512, D_MODEL), F32), jax.ShapeDtypeStruct((N_DEV, 512, D_MODEL), BF16)],
        compiler_params=_cp(48, ("arbitrary",)),
    )(*dw_rm, *dw_t)


def _local_step(x, target, metapad, cw, w_r, w_t, w_pw_full, w_out_full, ln_in_g, ln_in_b, b_f, conv_b, ln_conv_g,
                ln_conv_b, ln_out_g, ln_out_b):
    seq = x.shape[0]
    nt = seq // TILE + 1
    row = lambda a: a.reshape(1, -1).astype(F32)
    bf_pad = jnp.pad(row(b_f), ((0, 0), (0, LANES - N_HEADS)))
    g_in, b_in = row(ln_in_g), row(ln_in_b)
    g_cv, b_cv, c_b = row(ln_conv_g), row(ln_conv_b), row(conv_b)
    g_out, b_out = row(ln_out_g), row(ln_out_b)

    hb, qT3, kT3, vT3, k, v, ga, u, ug, gc, fl = _proj_fwd(x, metapad, g_in, b_in, w_r, nt)
    kx3 = _cumsum_fwd(fl, bf_pad, nt)
    oT3, o, lse4 = _attn_fwd(qT3, k, kx3, vT3, nt)
    co, hc, pw = _conv_fwd(u, ug, cw, c_b, g_cv, b_cv, w_pw_full, nt)
    y, dz, loss, dg_out, db_out = _out_fwd(o, ga, pw, gc, x, metapad, g_in, b_in, w_out_full, g_out, b_out,
                                            target, nt)
    doT3, dga, dgc, dw_out, dco, dw_pw, dg_cv, db_cv, dc_b = _out_bwd(dz, y, o, ga, pw, gc, w_out_full, hc, co,
                                                                      w_pw_full, g_cv, b_cv, nt)
    du, dug, dcw = _conv_bwd_taps(dco, u, ug, cw, nt)
    dqT3, dkT3, dvT3, dck, dcq4 = _attn_bwd(qT3, kT3, k, kx3, v, oT3, doT3, lse4, nt)
    dfl, dbf = _cumsum_bwd(dck, dcq4, fl, bf_pad, nt)
    secs = (dga, du, dug, dgc, dfl)
    secs_t3 = (dqT3, dkT3, dvT3)
    dw_rm = _dw_rowmajor(hb, secs, nt)
    dw_t = _dw_transposed(hb, secs_t3, nt)
    grad_x, dmetapad, dg_in, db_in = _dh_bwd(secs, secs_t3, w_r, w_t, dz, x, metapad, g_in, nt)
    pieces = dict(loss=loss, metapad=dmetapad, ln_in_g=dg_in, ln_in_b=db_in, w_in_rm=dw_rm, w_in_t=dw_t, b_f=dbf,
                  conv_w=dcw, conv_b=dc_b, ln_conv_g=dg_cv, ln_conv_b=db_cv, w_pw=dw_pw, w_out=dw_out,
                  ln_out_g=dg_out, ln_out_b=db_out)
    return grad_x, pieces


MESH = pl.DeviceIdType.MESH
ANY = pl.BlockSpec(memory_space=pl.ANY)


def _mesh_pos():
    return lax.axis_index("x"), lax.axis_index("y"), lax.axis_index("c")


GATHER_SEMS = 8


def _gather_body(x_refs, out_refs, send_sems, recv_sems, local_sems):
    n = len(x_refs)
    x, y, c = _mesh_pos()
    me, sibling = (x, y, c), (x, y, 1 - c)
    xn, yn, dg = (1 - x, y), (x, 1 - y), (1 - x, 1 - y)

    def slot(a, px, py, pc, half=None):
        blk = out_refs[a].at[4 * px + 2 * py + pc]
        if half is None:
            return blk
        rows = blk.shape[0] // 2
        return blk.at[pl.ds(half * rows, rows)]

    def copy(a, k, block, to, src=None, half=None):
        return pltpu.make_async_remote_copy(
            src_ref=slot(a, *block, half) if src is None else src, dst_ref=slot(a, *block, half),
            send_sem=send_sems.at[GATHER_SEMS * a + k], recv_sem=recv_sems.at[GATHER_SEMS * a + k], device_id=to,
            device_id_type=MESH)

    arrays = range(n)
    mine = [pltpu.make_async_copy(x_refs[a], slot(a, *me), local_sems.at[a]) for a in arrays]
    for cp in mine:
        cp.start()
    sent = []
    for a in arrays:
        sent += [copy(a, 0, me, sibling, src=x_refs[a]), copy(a, 1, me, (*xn, c), src=x_refs[a]),
                 copy(a, 2, me, (*yn, c), src=x_refs[a])]
    for cp in sent:
        cp.start()

    def also(cp):
        cp.start()
        sent.append(cp)

    for a in arrays:
        copy(a, 2, (*yn, c), me).wait_recv()
        also(copy(a, 3, (*yn, c), (*xn, c), half=0))
        also(copy(a, 6, (*yn, c), sibling))
        copy(a, 1, (*xn, c), me).wait_recv()
        also(copy(a, 4, (*xn, c), (*yn, c), half=1))
        also(copy(a, 5, (*xn, c), sibling))
    for a in arrays:
        copy(a, 3, (*dg, c), me, half=0).wait_recv()
        copy(a, 4, (*dg, c), me, half=1).wait_recv()
        also(copy(a, 7, (*dg, c), sibling))
    for a in arrays:
        copy(a, 0, sibling, me).wait_recv()
        copy(a, 5, (*xn, 1 - c), me).wait_recv()
        copy(a, 6, (*yn, 1 - c), me).wait_recv()
        copy(a, 7, (*dg, 1 - c), me).wait_recv()
    for cp in sent:
        cp.wait_send()
    for cp in mine:
        cp.wait()


def _all_gather(blks, name):
    n = len(blks)

    def body(*refs):
        _gather_body(refs[:n], refs[n:2 * n], *refs[2 * n:])

    return pl.pallas_call(
        body, name=name, out_shape=[jax.ShapeDtypeStruct((N_DEV, *b.shape), b.dtype) for b in blks],
        in_specs=[ANY] * n, out_specs=[ANY] * n,
        scratch_shapes=[pltpu.SemaphoreType.DMA((GATHER_SEMS * n,)), pltpu.SemaphoreType.DMA((GATHER_SEMS * n,)),
                        pltpu.SemaphoreType.DMA((n,))],
    )(*blks)


def _exchange_sibling(g8s, small):
    n = len(g8s)

    def body(*refs):
        g_refs, s_ref, out_refs, a_ref = refs[:n], refs[n], refs[n + 1:2 * n + 1], refs[2 * n + 1]
        send_sems, recv_sems, a_send, a_recv, a_local = refs[2 * n + 2:]
        x, y, c = _mesh_pos()
        cps = [pltpu.make_async_remote_copy(
            src_ref=g_refs[a].at[2 * q + (1 - c)], dst_ref=out_refs[a].at[q], send_sem=send_sems.at[4 * a + q],
            recv_sem=recv_sems.at[4 * a + q], device_id=(x, y, 1 - c), device_id_type=MESH)
            for a in range(n) for q in range(4)]
        for cp in cps:
            cp.start()
        _gather_body([s_ref], [a_ref], a_send, a_recv, a_local)
        for cp in cps:
            cp.wait()

    outs = pl.pallas_call(
        body, name="rs_sibling",
        out_shape=[jax.ShapeDtypeStruct((4, *g.shape[1:]), g.dtype) for g in g8s]
        + [jax.ShapeDtypeStruct((N_DEV, *small.shape), small.dtype)],
        in_specs=[ANY] * (n + 1), out_specs=[ANY] * (n + 1),
        scratch_shapes=[pltpu.SemaphoreType.DMA((4 * n,)), pltpu.SemaphoreType.DMA((4 * n,)),
                        pltpu.SemaphoreType.DMA((GATHER_SEMS,)), pltpu.SemaphoreType.DMA((GATHER_SEMS,)),
                        pltpu.SemaphoreType.DMA((1,))],
    )(*g8s, small)
    return outs[:n], outs[n]


def _exchange_chips(p4s):
    n = len(p4s)

    def body(*refs):
        p_refs, out_refs, send_sems, recv_sems = refs[:n], refs[n:2 * n], refs[2 * n], refs[2 * n + 1]
        x, y, c = _mesh_pos()
        chips = [(1 - x, y), (x, 1 - y), (1 - x, 1 - y)]
        cps = [pltpu.make_async_remote_copy(
            src_ref=p_refs[a].at[2 * cx + cy], dst_ref=out_refs[a].at[k], send_sem=send_sems.at[3 * a + k],
            recv_sem=recv_sems.at[3 * a + k], device_id=(cx, cy, c), device_id_type=MESH)
            for k, (cx, cy) in enumerate(chips) for a in range(n)]
        for cp in cps:
            cp.start()
        for cp in cps:
            cp.wait()

    return pl.pallas_call(
        body, name="rs_chips", out_shape=[jax.ShapeDtypeStruct((3, *p.shape[1:]), p.dtype) for p in p4s],
        in_specs=[ANY] * n, out_specs=[ANY] * n,
        scratch_shapes=[pltpu.SemaphoreType.DMA((3 * n,)), pltpu.SemaphoreType.DMA((3 * n,))],
    )(*p4s)


def _rs_add_sibling(g8s, recvs, c_idx):
    n = len(g8s)

    def body(s_ref, *refs):
        g_refs, r_refs, p32_refs, pb_refs = (refs[k * n:(k + 1) * n] for k in range(4))
        for g_ref, r_ref, p32_ref, pb_ref in zip(g_refs, r_refs, p32_refs, pb_refs):
            p = g_ref[0] + r_ref[0].astype(F32)
            p32_ref[0] = p
            pb_ref[0] = p.astype(BF16)

    blk = lambda g: pl.BlockSpec((1, *g.shape[1:]), lambda q, s: (q, 0, 0))
    grid_spec = pltpu.PrefetchScalarGridSpec(
        num_scalar_prefetch=1, grid=(4,),
        in_specs=[pl.BlockSpec((1, *g.shape[1:]), lambda q, s: (2 * q + s[0], 0, 0)) for g in g8s]
        + [blk(g) for g in g8s],
        out_specs=[blk(g) for g in g8s] * 2)
    outs = pl.pallas_call(
        body, name="rs_add_sibling", grid_spec=grid_spec,
        out_shape=[jax.ShapeDtypeStruct((4, *g.shape[1:]), F32) for g in g8s]
        + [jax.ShapeDtypeStruct((4, *g.shape[1:]), BF16) for g in g8s],
        compiler_params=_cp(48, ("arbitrary",)),
    )(c_idx, *g8s, *recvs)
    return outs[:n], outs[n:]


def _rs_add_chips(p32s, recvs, q_idx):
    def body(s_ref, pin_ref, pout_ref, ppw_ref, rin_ref, rout_ref, rpw_ref, gin_ref, gout_ref, gpw_ref):
        def total(p_ref, r_ref):
            return ((p_ref[0] + r_ref[0].astype(F32)) + r_ref[1].astype(F32)) + r_ref[2].astype(F32)

        gin_ref[...] = total(pin_ref, rin_ref)[:SHARD_IN, :]
        gout_ref[0] = total(pout_ref, rout_ref)
        gpw_ref[0] = total(ppw_ref, rpw_ref)

    own = lambda p: pl.BlockSpec((1, *p.shape[1:]), lambda i, s: (s[0], 0, 0))
    whole = lambda shape: pl.BlockSpec(shape, lambda i, s: (0,) * len(shape))
    out_shapes = [(SHARD_IN, D_MODEL), (1, *p32s[1].shape[1:]), (1, *p32s[2].shape[1:])]
    grid_spec = pltpu.PrefetchScalarGridSpec(
        num_scalar_prefetch=1, grid=(1,),
        in_specs=[own(p) for p in p32s] + [whole(r.shape) for r in recvs],
        out_specs=[whole(s) for s in out_shapes])
    return pl.pallas_call(
        body, name="rs_add_chips", grid_spec=grid_spec,
        out_shape=[jax.ShapeDtypeStruct(s, F32) for s in out_shapes],
        compiler_params=_cp(48, ("arbitrary",)),
    )(q_idx, *p32s, *recvs)


SMALL_ROWS_G = 64
SMALL_LAYOUT = {
    "metapad": (0, N_META, D_MODEL), "conv_w": (16, 32, D_CONV), "ln_in_g": (48, 1, D_MODEL),
    "ln_in_b": (49, 1, D_MODEL), "b_f": (50, 1, LANES), "conv_b": (51, 1, D_CONV), "ln_conv_g": (52, 1, D_CONV),
    "ln_conv_b": (53, 1, D_CONV), "ln_out_g": (54, 1, D_MODEL), "ln_out_b": (55, 1, D_MODEL), "loss": (56, 1, LANES)}


def _pack_small(pieces):
    names = list(SMALL_LAYOUT)

    def body(*refs):
        out_ref = refs[-1]
        out_ref[...] = jnp.zeros_like(out_ref)
        for name, ref in zip(names, refs[:-1]):
            r0, nr, nl = SMALL_LAYOUT[name]
            src = ref[PAD:, :] if name == "metapad" else ref[...]
            out_ref[r0:r0 + nr, 0:nl] = src

    return pl.pallas_call(body, name="pack_small", out_shape=jax.ShapeDtypeStruct((SMALL_ROWS_G, D_MODEL), F32),
                          compiler_params=_cp(16))(*[pieces[n] for n in names])


def _sum_small(gathered):
    names = list(SMALL_LAYOUT)

    def body(a_ref, *out_refs):
        acc = a_ref[0]
        for d in range(1, N_DEV):
            acc = acc + a_ref[d]
        for name, ref in zip(names, out_refs):
            r0, nr, nl = SMALL_LAYOUT[name]
            ref[...] = acc[r0:r0 + nr, 0:nl]

    outs = pl.pallas_call(
        body, name="sum_small",
        out_shape=[jax.ShapeDtypeStruct(SMALL_LAYOUT[n][1:], F32) for n in names], compiler_params=_cp(16))(gathered)
    return dict(zip(names, outs))


def _adamw(ws, gs, ms, vs):
    n = len(ws)
    c1 = 1.0 - ADAM_B1 ** ADAM_STEP
    c2 = 1.0 - ADAM_B2 ** ADAM_STEP

    def body(*refs):
        w_refs, g_refs, m_refs, v_refs = (refs[k * n:(k + 1) * n] for k in range(4))
        d_refs, nm_refs, nv_refs = (refs[(4 + k) * n:(5 + k) * n] for k in range(3))
        for w_ref, g_ref, m_ref, v_ref, d_ref, nm_ref, nv_ref in zip(w_refs, g_refs, m_refs, v_refs, d_refs,
                                                                     nm_refs, nv_refs):
            g = g_ref[...]
            m = ADAM_B1 * m_ref[...] + (1.0 - ADAM_B1) * g
            v = ADAM_B2 * v_ref[...] + (1.0 - ADAM_B2) * (g * g)
            nm_ref[...] = m
            nv_ref[...] = v
            d_ref[...] = -ADAM_LR * ((m / c1) / (jnp.sqrt(v / c2) + ADAM_EPS) + ADAM_WD * w_ref[...])

    shapes = [jax.ShapeDtypeStruct(w.shape, F32) for w in ws]
    outs = pl.pallas_call(body, name="adamw", out_shape=shapes * 3, compiler_params=_cp(48))(*ws, *gs, *ms, *vs)
    return outs[:n], outs[n:2 * n], outs[2 * n:]


W_NAMES = ("meta", "ln_in_g", "ln_in_b", "w_in", "b_f", "conv_w", "conv_b", "ln_conv_g", "ln_conv_b", "w_pw",
           "w_out", "ln_out_g", "ln_out_b")


def kernel(x, meta, ln_in_g, ln_in_b, w_in, b_f, conv_w, conv_b, ln_conv_g, ln_conv_b, w_pw, w_out, ln_out_g, ln_out_b, loss_target, m_meta, m_ln_in_g, m_ln_in_b, m_w_in, m_b_f, m_conv_w, m_conv_b, m_ln_conv_g, m_ln_conv_b, m_w_pw, m_w_out, m_ln_out_g, m_ln_out_b, v_meta, v_ln_in_g, v_ln_in_b, v_w_in, v_b_f, v_conv_w, v_conv_b, v_ln_conv_g, v_ln_conv_b, v_w_pw, v_w_out, v_ln_out_g, v_ln_out_b):
    mx, my, mc = _mesh_pos()
    me = 4 * mx + 2 * my + mc
    n_meta_sh = D_MODEL // N_DEV
    n_cw_sh = D_CONV // N_DEV
    n_out_sh = D_MODEL // N_DEV
    n_pw_sh = D_CONV // N_DEV

    small_w = jnp.concatenate([meta, jnp.pad(conv_w[0], ((0, 1), (0, LANES - n_cw_sh)))], axis=0)
    all_in, all_out, all_pw, all_small = _all_gather(
        [jnp.pad(w_in[0].T, ((0, 512 - SHARD_IN), (0, 0))).astype(BF16), w_out[0].astype(BF16), w_pw[0].astype(BF16),
         small_w], "gather_weights")
    w_r, w_t, metapad, cw = _repack_weights(all_in, all_small)
    w_out_full = all_out.reshape(D_MODEL, D_MODEL)
    w_pw_full = all_pw.reshape(D_CONV, D_CONV)

    grad_x, pc = _local_step(x[0], loss_target[0], metapad, cw, w_r, w_t, w_pw_full, w_out_full, ln_in_g, ln_in_b,
                             b_f[0], conv_b[0], ln_conv_g[0], ln_conv_b[0], ln_out_g[0], ln_out_b[0])

    g_in8, g_in8_b = _unpack_dw_in(pc["w_in_rm"], pc["w_in_t"])
    g8s = [g_in8, pc["w_out"].reshape(N_DEV, n_out_sh, D_MODEL), pc["w_pw"].reshape(N_DEV, n_pw_sh, D_CONV)]
    from_sibling, all_small_g = _exchange_sibling([g_in8_b] + g8s[1:], _pack_small(pc))
    p32s, pbs = _rs_add_sibling(g8s, from_sibling, jnp.reshape(mc, (1,)).astype(jnp.int32))
    from_chips = _exchange_chips(pbs)
    g_w_in, g_w_out, g_w_pw = _rs_add_chips(p32s, from_chips, jnp.reshape(2 * mx + my, (1,)).astype(jnp.int32))

    sm = _sum_small(all_small_g)
    grads = {
        "meta": lax.dynamic_slice_in_dim(sm["metapad"], me * n_meta_sh, n_meta_sh, axis=1),
        "ln_in_g": sm["ln_in_g"].reshape(D_MODEL), "ln_in_b": sm["ln_in_b"].reshape(D_MODEL), "w_in": g_w_in.T[None],
        "b_f": sm["b_f"][:, :N_HEADS],
        "conv_w": lax.dynamic_slice_in_dim(sm["conv_w"], me * n_cw_sh, n_cw_sh, axis=1)[None, :CONV_WIDTH],
        "conv_b": sm["conv_b"], "ln_conv_g": sm["ln_conv_g"], "ln_conv_b": sm["ln_conv_b"], "w_pw": g_w_pw,
        "w_out": g_w_out, "ln_out_g": sm["ln_out_g"], "ln_out_b": sm["ln_out_b"]}
    loss_all = sm["loss"][0, 0]

    weights = dict(meta=meta, ln_in_g=ln_in_g, ln_in_b=ln_in_b, w_in=w_in, b_f=b_f, conv_w=conv_w, conv_b=conv_b,
                   ln_conv_g=ln_conv_g, ln_conv_b=ln_conv_b, w_pw=w_pw, w_out=w_out, ln_out_g=ln_out_g,
                   ln_out_b=ln_out_b)
    moms = dict(meta=m_meta, ln_in_g=m_ln_in_g, ln_in_b=m_ln_in_b, w_in=m_w_in, b_f=m_b_f, conv_w=m_conv_w,
                conv_b=m_conv_b, ln_conv_g=m_ln_conv_g, ln_conv_b=m_ln_conv_b, w_pw=m_w_pw, w_out=m_w_out,
                ln_out_g=m_ln_out_g, ln_out_b=m_ln_out_b)
    vels = dict(meta=v_meta, ln_in_g=v_ln_in_g, ln_in_b=v_ln_in_b, w_in=v_w_in, b_f=v_b_f, conv_w=v_conv_w,
                conv_b=v_conv_b, ln_conv_g=v_ln_conv_g, ln_conv_b=v_ln_conv_b, w_pw=v_w_pw, w_out=v_w_out,
                ln_out_g=v_ln_out_g, ln_out_b=v_ln_out_b)

    def to_kernel(name, a):
        if name == "w_in":
            return a[0].T
        return a.reshape(1, -1) if a.ndim == 1 else a

    def from_kernel(name, a):
        return a.T[None] if name == "w_in" else a.reshape(weights[name].shape)

    upd = _adamw(*[[to_kernel(n, d[n]) for n in W_NAMES] for d in (weights, grads, moms, vels)])
    deltas, new_m, new_v = ([from_kernel(n, a) for n, a in zip(W_NAMES, part)] for part in upd)
    return (loss_all, grad_x[None], *[grads[n] for n in W_NAMES], *deltas, *new_m, *new_v)
```

```python
import jax
import jax.numpy as jnp
import numpy as np
from jax import lax
from jax.experimental import pallas as pl
from jax.experimental.pallas import tpu as pltpu

F32 = jnp.float32
BF16 = jnp.bfloat16

D_MODEL = 1024
D_ATTN = 512
D_CONV = 512
N_HEADS = 8
HEAD_DIM = 64
N_META = 16
CONV_WIDTH = 31
LN_EPS = 1e-5
ALPHA = 2.0 ** 0.25
SCALE = HEAD_DIM ** -0.5
LOG2E = 1.4426950408889634
ADAM_LR, ADAM_B1, ADAM_B2, ADAM_EPS, ADAM_WD, ADAM_STEP = 0.001, 0.9, 0.999, 1e-08, 0.01, 10

N_DEV = 8
D_IN = 3592
SHARD_IN = D_IN // N_DEV
TILE = 256
PAD = TILE - N_META
HALO = 32
SHIFT_ROWS = TILE + HALO
EXT_ROWS = SHIFT_ROWS + 8
NEG = -1e30
LANES = 128
W_COLS = 7 * 512 + LANES
OFF_GA_R, OFF_F_R = 1536, 3584
MIB = 1024 * 1024


def _cp(vmem_mib, sem=None):
    kw = dict(vmem_limit_bytes=vmem_mib * MIB)
    if sem is not None:
        kw["dimension_semantics"] = sem
    return pltpu.CompilerParams(**kw)


def _sigmoid(x):
    return 1.0 / (1.0 + jnp.exp(-x))


def _silu_and_grad(x):
    s = _sigmoid(x)
    return x * s, s * (1.0 + x * (1.0 - s))


def _ln_stats(x):
    mu = jnp.mean(x, axis=-1, keepdims=True)
    xc = x - mu
    var = jnp.mean(xc * xc, axis=-1, keepdims=True)
    rstd = lax.rsqrt(var + LN_EPS)
    return xc * rstd, rstd


def _ln_bwd(dy, xhat, rstd, g):
    dxh = dy * g
    m1 = jnp.mean(dxh, axis=-1, keepdims=True)
    m2 = jnp.mean(dxh * xhat, axis=-1, keepdims=True)
    return rstd * (dxh - m1 - xhat * m2)


def _row_spec(cols, shift=False):
    if shift:
        return pl.BlockSpec((TILE, cols), lambda i: (jnp.maximum(i - 1, 0), 0))
    return pl.BlockSpec((TILE, cols), lambda i: (i, 0))


def _full_spec(shape):
    nd = len(shape)
    return pl.BlockSpec(shape, lambda i: (0,) * nd)


def _t3_spec(ch):
    return pl.BlockSpec((1, ch, TILE), lambda i: (i, 0, 0))


def _proj_fwd(x, metapad, g_in, b_in, w_r, nt):
    lp = nt * TILE

    def body(x_ref, mp_ref, g_ref, b_ref, w_ref, hb_ref, qT_ref, kT_ref, vT_ref, k_ref, v_ref,
             ga_ref, u_ref, ug_ref, gc_ref, fl_ref):
        i = pl.program_id(0)
        x0 = jnp.where(i == 0, mp_ref[...], x_ref[...])
        xhat, _ = _ln_stats(x0)
        hb = (xhat * g_ref[...] + b_ref[...]).astype(BF16)
        hb_ref[...] = hb

        def sec(off, n=512):
            return jnp.dot(hb, w_ref[:, off:off + n], preferred_element_type=F32)

        qT_ref[0] = (sec(0) * (SCALE * LOG2E)).T.astype(BF16)
        k = sec(512)
        kT_ref[0] = k.T.astype(BF16)
        k_ref[...] = k.astype(BF16)
        v = sec(1024)
        vT_ref[0] = v.T.astype(BF16)
        v_ref[...] = v.astype(BF16)
        ga_ref[...] = sec(OFF_GA_R).astype(BF16)
        u_ref[...] = sec(OFF_GA_R + 512).astype(BF16)
        ug_ref[...] = sec(OFF_GA_R + 1024).astype(BF16)
        gc_ref[...] = sec(OFF_GA_R + 1536).astype(BF16)
        fl_ref[...] = sec(OFF_F_R, LANES)

    t3 = jax.ShapeDtypeStruct((nt, 512, TILE), BF16)
    rm = lambda dt: jax.ShapeDtypeStruct((lp, 512), dt)
    return pl.pallas_call(
        body, name="proj_fwd", grid=(nt,),
        in_specs=[_row_spec(D_MODEL, shift=True), _full_spec((TILE, D_MODEL)), _full_spec((1, D_MODEL)),
                  _full_spec((1, D_MODEL)), _full_spec((D_MODEL, W_COLS))],
        out_specs=[_row_spec(D_MODEL), _t3_spec(512), _t3_spec(512), _t3_spec(512), _row_spec(512), _row_spec(512),
                   _row_spec(512), _row_spec(512), _row_spec(512), _row_spec(512), _row_spec(LANES)],
        out_shape=[jax.ShapeDtypeStruct((lp, D_MODEL), BF16), t3, t3, t3, rm(BF16), rm(BF16),
                   rm(BF16), rm(BF16), rm(BF16), rm(BF16), jax.ShapeDtypeStruct((lp, LANES), F32)],
        compiler_params=_cp(56, ("arbitrary",)),
    )(x, metapad, g_in, b_in, w_r)


def _row_mask(i, shape):
    r = lax.broadcasted_iota(jnp.int32, shape, 0)
    return (r >= PAD) | (i > 0)


def _cumsum_fwd(fl, bf_pad, nt):
    lp = nt * TILE

    def body(fl_ref, bf_ref, kx_ref, carry):
        i = pl.program_id(0)

        @pl.when(i == 0)
        def _():
            carry[...] = jnp.zeros_like(carry)

        z = fl_ref[...] + bf_ref[...]
        lf = jnp.minimum(z, 0.0) - jnp.log(1.0 + jnp.exp(-jnp.abs(z)))
        lane = lax.broadcasted_iota(jnp.int32, (TILE, LANES), 1)
        real = _row_mask(i, (TILE, LANES))
        lf = jnp.where(real & (lane < N_HEADS), lf, 0.0)
        r = lax.broadcasted_iota(jnp.int32, (TILE, TILE), 0)
        c = lax.broadcasted_iota(jnp.int32, (TILE, TILE), 1)
        tril = (c <= r).astype(F32)
        cs = jnp.dot(tril, lf, precision=lax.Precision.HIGHEST, preferred_element_type=F32) + carry[...]
        carry[...] = cs[TILE - 1:TILE, :]
        bias = jnp.where(real, cs * (-LOG2E), NEG)
        hi = bias.astype(BF16).astype(F32)
        mid = (bias - hi).astype(BF16).astype(F32)
        lo = (bias - hi - mid).astype(BF16).astype(F32)
        for p in range(N_HEADS // 2):
            out = jnp.zeros((TILE, LANES), F32)
            for hh in range(2):
                for part, piece in enumerate((hi, mid, lo)):
                    dst, src = 3 * hh + part, 2 * p + hh
                    moved = piece if dst == src else pltpu.roll(piece, (dst - src) % LANES, 1)
                    out = jnp.where(lane == dst, moved, out)
            kx_ref[p] = out.astype(BF16)

    return pl.pallas_call(
        body, name="cumsum_fwd", grid=(nt,),
        in_specs=[_row_spec(LANES), _full_spec((1, LANES))],
        out_specs=pl.BlockSpec((N_HEADS // 2, TILE, LANES), lambda i: (0, i, 0)),
        out_shape=jax.ShapeDtypeStruct((N_HEADS // 2, lp, LANES), BF16),
        scratch_shapes=[pltpu.VMEM((1, LANES), F32)],
        compiler_params=_cp(32, ("arbitrary",)),
    )(fl, bf_pad)


def _head_rows(blk, hh):
    r = lax.broadcasted_iota(jnp.int32, blk.shape, 0)
    return jnp.where((r >= hh * HEAD_DIM) & (r < (hh + 1) * HEAD_DIM), blk, jnp.zeros_like(blk))


def _two_heads(blk):
    return jnp.concatenate([_head_rows(blk, 0), _head_rows(blk, 1)], axis=1)


def _bias_rows():
    r = lax.broadcasted_iota(jnp.int32, (LANES, 2 * TILE), 0)
    c = lax.broadcasted_iota(jnp.int32, (LANES, 2 * TILE), 1)
    return jnp.where(((r < 3) & (c < TILE)) | ((r >= 3) & (r < 6) & (c >= TILE)), 1.0, 0.0).astype(BF16)


def _diag_mask(s):
    kpos = lax.broadcasted_iota(jnp.int32, (TILE, TILE), 0)
    qpos = lax.broadcasted_iota(jnp.int32, (TILE, TILE), 1)
    return jnp.where(kpos <= qpos, s, NEG)


def _stream(n, first, nxt, scores, update, unroll=8):
    if n == 0:
        return
    scores(first, 0)

    def step(_, idx):
        for _u in range(unroll):
            idx_b = nxt(idx)
            scores(idx_b, 1)
            update(idx, 0)
            idx = nxt(idx_b)
            scores(idx, 0)
            update(idx_b, 1)
        return idx

    steps = (n - 1) // (2 * unroll)
    idx = lax.fori_loop(0, steps, step, first)
    left = n - 2 * unroll * steps
    for r in range(left - 1):
        idx_b = nxt(idx)
        scores(idx_b, (r + 1) % 2)
        update(idx, r % 2)
        idx = idx_b
    update(idx, (left - 1) % 2)


def _next_below_diagonal(idx):
    i, j = idx
    wrap = j + 1 >= i
    return jnp.where(wrap, i + 1, i), jnp.where(wrap, 0, j + 1)


def _tile_rows(t):
    return pl.ds(pl.multiple_of(t * TILE, TILE), TILE)


def _two_streams(nt):
    load, group = [0, 0], {}
    for i in sorted(range(1, nt), reverse=True):
        g = 0 if load[0] <= load[1] else 1
        group[i] = g
        load[g] += i
    rows = [[(i, i, j) for i in range(1, nt) if group[i] == g for j in range(i)] for g in range(2)]
    length = max(len(r) for r in rows)
    rows = [r + [(nt, 0, 0)] * (length - len(r)) for r in rows]
    return group, np.asarray(rows, np.int32).reshape(2, -1), length


def _attn_fwd(qT3, k, kx3, vT3, nt):
    lp = nt * TILE
    npair = N_HEADS // 2
    group, table, n_stream = _two_streams(nt)

    def body(tab_ref, qT_ref, k_ref, kx_ref, vT_ref, oT_ref, o_ref, lse_ref, sbuf, m_0, l_0, acc_0, m_1, l_1, acc_1):
        ones = _bias_rows()
        states = ((m_0, l_0, acc_0), (m_1, l_1, acc_1))

        def scores(i, j, slot):
            qcat = jnp.concatenate([_two_heads(qT_ref[i]), ones], axis=0)
            kext = jnp.concatenate([k_ref[_tile_rows(j), :], kx_ref[0, _tile_rows(j), :]], axis=1)
            sbuf[slot] = jnp.dot(kext, qcat, preferred_element_type=F32)

        def update(st, j, slot, state, diag):
            m_s, l_s, acc_s = state
            for hh in range(2):
                s = sbuf[slot, :, hh * TILE:(hh + 1) * TILE]
                vj = vT_ref[j, hh * HEAD_DIM:(hh + 1) * HEAD_DIM, :]
                if diag:
                    s = _diag_mask(s)
                    m_new = jnp.max(s, axis=0, keepdims=True)
                    p = jnp.exp2(s - m_new)
                    l_s[st, hh] = jnp.sum(p, axis=0, keepdims=True)
                    acc_s[st, hh] = jnp.dot(vj, p.astype(BF16), preferred_element_type=F32)
                else:
                    m_prev = m_s[st, hh]
                    m_new = jnp.maximum(m_prev, jnp.max(s, axis=0, keepdims=True))
                    a = jnp.exp2(m_prev - m_new)
                    p = jnp.exp2(s - m_new)
                    l_s[st, hh] = a * l_s[st, hh] + jnp.sum(p, axis=0, keepdims=True)
                    acc_s[st, hh] = a * acc_s[st, hh] + jnp.dot(vj, p.astype(BF16), preferred_element_type=F32)
                m_s[st, hh] = m_new

        _stream(nt, jnp.int32(0), lambda t: t + 1, lambda t, slot: scores(t, t, slot),
                lambda t, slot: update(t, t, slot, states[0], True))
        for dst, src in zip(states[1], states[0]):
            dst[0:nt] = src[0:nt]
        for m_s, l_s, acc_s in states:
            m_s[nt] = jnp.full(m_s.shape[1:], NEG, F32)
            l_s[nt] = jnp.zeros(l_s.shape[1:], F32)
            acc_s[nt] = jnp.zeros(acc_s.shape[1:], F32)

        def entry(g, t):
            return tab_ref[g, 3 * t], tab_ref[g, 3 * t + 1], tab_ref[g, 3 * t + 2]

        def scores2(t, slot):
            for g in range(2):
                _, qi, kj = entry(g, t)
                scores(qi, kj, 2 * g + slot)

        def update2(t, slot):
            for g in range(2):
                st, _, kj = entry(g, t)
                update(st, kj, 2 * g + slot, states[g], False)

        _stream(n_stream, jnp.int32(0), lambda t: t + 1, scores2, update2)

        for i in range(nt):
            m_s, l_s, acc_s = states[group.get(i, 0)]
            for hh in range(2):
                l = l_s[i, hh]
                oT_ref[i, hh * HEAD_DIM:(hh + 1) * HEAD_DIM, :] = acc_s[i, hh] / l
                lse_ref[0, i, hh:hh + 1, :] = m_s[i, hh] + jnp.log(l) * LOG2E
            o_ref[i * TILE:(i + 1) * TILE, :] = oT_ref[i].T.astype(BF16)

    blk_t = pl.BlockSpec((nt, LANES, TILE), lambda p, tab: (0, p, 0))
    blk_rm = pl.BlockSpec((lp, LANES), lambda p, tab: (0, p))
    blk_px = pl.BlockSpec((1, lp, LANES), lambda p, tab: (p, 0, 0))
    blk_st = pl.BlockSpec((1, nt, 8, TILE), lambda p, tab: (p, 0, 0, 0))
    state = [pltpu.VMEM((nt + 1, 2, 1, TILE), F32), pltpu.VMEM((nt + 1, 2, 1, TILE), F32),
             pltpu.VMEM((nt + 1, 2, HEAD_DIM, TILE), F32)]
    grid_spec = pltpu.PrefetchScalarGridSpec(
        num_scalar_prefetch=1, grid=(npair,), in_specs=[blk_t, blk_rm, blk_px, blk_t],
        out_specs=[blk_t, blk_rm, blk_st], scratch_shapes=[pltpu.VMEM((4, TILE, 2 * TILE), F32)] + state + state)
    return pl.pallas_call(
        body, name="attn_fwd", grid_spec=grid_spec,
        out_shape=[jax.ShapeDtypeStruct((nt, D_ATTN, TILE), F32),
                   jax.ShapeDtypeStruct((lp, D_ATTN), BF16),
                   jax.ShapeDtypeStruct((npair, nt, 8, TILE), F32)],
        compiler_params=_cp(60, ("arbitrary",)),
    )(jnp.asarray(table), qT3, k, kx3, vT3)


def _attn_bwd(qT3, kT3, k, kx3, v, oT3, doT3, lse4, nt):
    lp = nt * TILE
    npair = N_HEADS // 2

    def body(qT_ref, kT_ref, k_ref, kx_ref, v_ref, oT_ref, doT_ref, lse_ref,
             dqT_ref, dkT_ref, dvT_ref, dck_ref, dcq_ref, sbuf, dpbuf, dq_s, dk_s, dv_s, dc_s, tp_s, tds_s):
        ones = _bias_rows()

        def scores(idx, slot):
            i, j = idx
            qcat = jnp.concatenate([_two_heads(qT_ref[i]), ones], axis=0)
            kext = jnp.concatenate([k_ref[_tile_rows(j), :], kx_ref[0, _tile_rows(j), :]], axis=1)
            sbuf[slot] = jnp.dot(kext, qcat, preferred_element_type=F32)
            dpbuf[slot] = jnp.dot(v_ref[_tile_rows(j), :], _two_heads(doT_ref[i]), preferred_element_type=F32)

        def update(idx, slot, diag):
            i, j = idx
            for hh in range(2):
                hs = slice(hh * HEAD_DIM, (hh + 1) * HEAD_DIM)
                s = sbuf[slot, :, hh * TILE:(hh + 1) * TILE]
                if diag:
                    s = _diag_mask(s)
                p = jnp.exp2(s - lse_ref[0, i, hh:hh + 1, :])
                doh = doT_ref[i, hs, :]
                delta = jnp.sum(doh.astype(F32) * oT_ref[i, hs, :], axis=0, keepdims=True)
                ds = p * (dpbuf[slot, :, hh * TILE:(hh + 1) * TILE] - delta)
                dsb = ds.astype(BF16)
                tp_s[hh] = p.astype(BF16).T
                tds_s[hh] = dsb.T
                dv = jnp.dot(doh, tp_s[hh], preferred_element_type=F32)
                dk = jnp.dot(qT_ref[i, hs, :], tds_s[hh], preferred_element_type=F32)
                dq = jnp.dot(kT_ref[j, hs, :], dsb, preferred_element_type=F32)
                dc = ds[:, :LANES] + ds[:, LANES:]
                dcq = jnp.sum(ds, axis=0, keepdims=True)
                if diag:
                    dv_s[j, hh] = dv
                    dk_s[j, hh] = dk
                    dc_s[j, hh] = dc
                    dq_s[i, hs, :] = dq
                    dcq_ref[0, i, hh:hh + 1, :] = dcq
                else:
                    dv_s[j, hh] += dv
                    dk_s[j, hh] += dk
                    dc_s[j, hh] += dc
                    dq_s[i, hs, :] += dq
                    dcq_ref[0, i, hh:hh + 1, :] += dcq

        dcq_ref[...] = jnp.zeros_like(dcq_ref)
        zero = jnp.int32(0)
        _stream(nt, (zero, zero), lambda idx: (idx[0] + 1, idx[1] + 1), scores,
                lambda idx, slot: update(idx, slot, True))
        _stream(nt * (nt - 1) // 2, (zero + 1, zero), _next_below_diagonal, scores,
                lambda idx, slot: update(idx, slot, False))

        lane = lax.broadcasted_iota(jnp.int32, (TILE, LANES), 1)

        def finish(t, carry):
            dck = jnp.zeros((TILE, LANES), F32)
            for hh in range(2):
                hs = slice(hh * HEAD_DIM, (hh + 1) * HEAD_DIM)
                dkT_ref[t, hs, :] = (dk_s[t, hh] * (1.0 / LOG2E)).astype(BF16)
                dvT_ref[t, hs, :] = dv_s[t, hh].astype(BF16)
                dck = jnp.where(lane == hh, -jnp.sum(dc_s[t, hh], axis=1, keepdims=True), dck)
            dck_ref[0, _tile_rows(t), :] = dck
            dqT_ref[t] = (dq_s[t] * SCALE).astype(BF16)
            return carry

        lax.fori_loop(0, nt, finish, 0)

    blk_t = pl.BlockSpec((nt, LANES, TILE), lambda p: (0, p, 0))
    blk_rm = pl.BlockSpec((lp, LANES), lambda p: (0, p))
    blk_px = pl.BlockSpec((1, lp, LANES), lambda p: (p, 0, 0))
    blk_st = pl.BlockSpec((1, nt, 8, TILE), lambda p: (p, 0, 0, 0))
    t3 = jax.ShapeDtypeStruct((nt, D_ATTN, TILE), BF16)
    return pl.pallas_call(
        body, name="attn_bwd", grid=(npair,),
        in_specs=[blk_t, blk_t, blk_rm, blk_px, blk_rm, blk_t, blk_t, blk_st],
        out_specs=[blk_t, blk_t, blk_t, blk_px, blk_st],
        out_shape=[t3, t3, t3, jax.ShapeDtypeStruct((npair, lp, LANES), F32),
                   jax.ShapeDtypeStruct((npair, nt, 8, TILE), F32)],
        scratch_shapes=[pltpu.VMEM((2, TILE, 2 * TILE), F32), pltpu.VMEM((2, TILE, 2 * TILE), F32),
                        pltpu.VMEM((nt, LANES, TILE), F32), pltpu.VMEM((nt, 2, HEAD_DIM, TILE), F32),
                        pltpu.VMEM((nt, 2, HEAD_DIM, TILE), F32), pltpu.VMEM((nt, 2, TILE, LANES), F32),
                        pltpu.VMEM((2, TILE, TILE), BF16), pltpu.VMEM((2, TILE, TILE), BF16)],
        compiler_params=_cp(60, ("arbitrary",)),
    )(qT3, kT3, k, kx3, v, oT3, doT3, lse4)


def _glu(u, ug, i):
    return jnp.where(_row_mask(i, u.shape), u.astype(F32) * _sigmoid(ug.astype(F32)), 0.0)


def _shifted_copies(dst, src):
    for ph in range(8):
        dst[ph] = src[ph:ph + SHIFT_ROWS, :]


def _tap_window(sh, off, lanes, row0=0, rows=TILE):
    base = (off // 8) * 8 + row0
    return sh[off % 8, base:base + rows, lanes]


def _conv_fwd(u, ug, conv_w, conv_b, g, b, w_pw, nt):
    lp = nt * TILE

    def body(u_ref, ug_ref, up_ref, ugp_ref, w_ref, cb_ref, g_ref, b_ref, wpw_ref,
             co_ref, hc_ref, pw_ref, ext, sh):
        i = pl.program_id(0)
        prev = _glu(up_ref[...], ugp_ref[...], i - 1)
        ext[0:HALO, :] = jnp.where(i > 0, prev[TILE - HALO:, :], 0.0)
        ext[HALO:HALO + TILE, :] = _glu(u_ref[...], ug_ref[...], i)
        ext[HALO + TILE:, :] = jnp.zeros((8, D_CONV), F32)
        _shifted_copies(sh, ext)
        for lb in range(D_CONV // LANES):
            lanes = slice(lb * LANES, (lb + 1) * LANES)
            acc = jnp.zeros((TILE, LANES), F32) + cb_ref[:, lanes]
            for t in range(CONV_WIDTH):
                off = HALO - (CONV_WIDTH - 1) + t
                acc = acc + w_ref[t:t + 1, lanes] * _tap_window(sh, off, lanes)
            co_ref[:, lanes] = acc
        xhat, _ = _ln_stats(co_ref[...])
        a, _ = _silu_and_grad(xhat * g_ref[...] + b_ref[...])
        hc = a.astype(BF16)
        hc_ref[...] = hc
        pw_ref[...] = jnp.dot(hc, wpw_ref[...], preferred_element_type=F32).astype(BF16)

    rm = lambda dt: jax.ShapeDtypeStruct((lp, D_CONV), dt)
    return pl.pallas_call(
        body, name="conv_fwd", grid=(nt,),
        in_specs=[_row_spec(512), _row_spec(512), _row_spec(512, shift=True), _row_spec(512, shift=True),
                  _full_spec((32, 512)), _full_spec((1, 512)), _full_spec((1, 512)), _full_spec((1, 512)),
                  _full_spec((512, 512))],
        out_specs=[_row_spec(512), _row_spec(512), _row_spec(512)],
        out_shape=[rm(F32), rm(BF16), rm(BF16)],
        scratch_shapes=[pltpu.VMEM((EXT_ROWS, D_CONV), F32), pltpu.VMEM((8, SHIFT_ROWS, D_CONV), F32)],
        compiler_params=_cp(40, ("arbitrary",)),
    )(u, ug, u, ug, conv_w, conv_b, g, b, w_pw)


def _out_fwd(o, ga, pw, gc, x, metapad, g_in, b_in, w_out, g_out, b_out, target, nt):
    lp = nt * TILE

    def body(o_ref, ga_ref, pw_ref, gc_ref, x_ref, mp_ref, gi_ref, bi_ref, wo_ref, go_ref, bo_ref, t_ref,
             y_ref, dz_ref, loss_ref, dgo_ref, dbo_ref):
        i = pl.program_id(0)

        @pl.when(i == 0)
        def _():
            loss_ref[...] = jnp.zeros_like(loss_ref)
            dgo_ref[...] = jnp.zeros_like(dgo_ref)
            dbo_ref[...] = jnp.zeros_like(dbo_ref)

        x0 = jnp.where(i == 0, mp_ref[...], x_ref[...])
        xhat, _ = _ln_stats(x0)
        h = xhat * gi_ref[...] + bi_ref[...]
        ya, _ = _silu_and_grad(ga_ref[...].astype(F32))
        yc, _ = _silu_and_grad(gc_ref[...].astype(F32))
        ya = (o_ref[...].astype(F32) * ya).astype(BF16)
        yc = (pw_ref[...].astype(F32) * yc).astype(BF16)
        y_ref[:, :D_ATTN] = ya
        y_ref[:, D_ATTN:] = yc
        z = ALPHA * h + jnp.dot(ya, wo_ref[:D_ATTN, :], preferred_element_type=F32) \
            + jnp.dot(yc, wo_ref[D_ATTN:, :], preferred_element_type=F32)
        zhat, rstd = _ln_stats(z)
        out = zhat * go_ref[...] + bo_ref[...]
        live = (i > 0).astype(F32)
        err = (out - t_ref[...]) * live
        dout = err * (1.0 / D_MODEL)
        loss_ref[...] += 0.5 * jnp.sum(jnp.sum(err * dout, axis=0, keepdims=True), axis=1, keepdims=True)
        dgo_ref[...] += jnp.sum(dout * zhat, axis=0, keepdims=True)
        dbo_ref[...] += jnp.sum(dout, axis=0, keepdims=True)
        dz_ref[...] = _ln_bwd(dout, zhat, rstd, go_ref[...])

    return pl.pallas_call(
        body, name="out_fwd", grid=(nt,),
        in_specs=[_row_spec(512), _row_spec(512), _row_spec(512), _row_spec(512),
                  _row_spec(D_MODEL, shift=True), _full_spec((TILE, D_MODEL)), _full_spec((1, D_MODEL)),
                  _full_spec((1, D_MODEL)), _full_spec((D_MODEL, D_MODEL)), _full_spec((1, D_MODEL)),
                  _full_spec((1, D_MODEL)), _row_spec(D_MODEL, shift=True)],
        out_specs=[_row_spec(D_MODEL), _row_spec(D_MODEL), _full_spec((1, LANES)), _full_spec((1, D_MODEL)),
                   _full_spec((1, D_MODEL))],
        out_shape=[jax.ShapeDtypeStruct((lp, D_MODEL), BF16), jax.ShapeDtypeStruct((lp, D_MODEL), F32),
                   jax.ShapeDtypeStruct((1, LANES), F32), jax.ShapeDtypeStruct((1, D_MODEL), F32),
                   jax.ShapeDtypeStruct((1, D_MODEL), F32)],
        compiler_params=_cp(40, ("arbitrary",)),
    )(o, ga, pw, gc, x, metapad, g_in, b_in, w_out, g_out, b_out, target)


def _out_bwd(dz, y, o, ga, pw, gc, w_out, hc, co, w_pw, g_cv, b_cv, nt):
    lp = nt * TILE

    def body(dz_ref, y_ref, o_ref, ga_ref, pw_ref, gc_ref, wo_ref, hc_ref, co_ref, wpw_ref, g_ref, b_ref,
             doT_ref, dga_ref, dgc_ref, dwo_ref, dco_ref, dwpw_ref, dg_ref, db_ref, dcb_ref):
        i = pl.program_id(0)

        @pl.when(i == 0)
        def _():
            dwo_ref[...] = jnp.zeros_like(dwo_ref)
            dwpw_ref[...] = jnp.zeros_like(dwpw_ref)
            dg_ref[...] = jnp.zeros_like(dg_ref)
            db_ref[...] = jnp.zeros_like(db_ref)
            dcb_ref[...] = jnp.zeros_like(dcb_ref)

        dzb = dz_ref[...].astype(BF16)
        nt_dims = (((1,), (1,)), ((), ()))
        tn_dims = (((0,), (0,)), ((), ()))
        dya = lax.dot_general(dzb, wo_ref[:D_ATTN, :], nt_dims, preferred_element_type=F32)
        dyc = lax.dot_general(dzb, wo_ref[D_ATTN:, :], nt_dims, preferred_element_type=F32)
        sa, sga = _silu_and_grad(ga_ref[...].astype(F32))
        sc, sgc = _silu_and_grad(gc_ref[...].astype(F32))
        doT_ref[0] = (dya * sa).T.astype(BF16)
        dga_ref[...] = (dya * o_ref[...].astype(F32) * sga).astype(BF16)
        dpw_b = (dyc * sc).astype(BF16)
        dgc_ref[...] = (dyc * pw_ref[...].astype(F32) * sgc).astype(BF16)
        dwo_ref[...] += lax.dot_general(y_ref[...], dzb, tn_dims, preferred_element_type=F32)

        dhc = lax.dot_general(dpw_b, wpw_ref[...], nt_dims, preferred_element_type=F32)
        xhat, rstd = _ln_stats(co_ref[...])
        _, sg = _silu_and_grad(xhat * g_ref[...] + b_ref[...])
        dln = dhc * sg
        dg_ref[...] += jnp.sum(dln * xhat, axis=0, keepdims=True)
        db_ref[...] += jnp.sum(dln, axis=0, keepdims=True)
        dco = _ln_bwd(dln, xhat, rstd, g_ref[...])
        dco_ref[...] = dco
        dcb_ref[...] += jnp.sum(dco, axis=0, keepdims=True)
        dwpw_ref[...] += lax.dot_general(hc_ref[...], dpw_b, tn_dims, preferred_element_type=F32)

    rm = jax.ShapeDtypeStruct((lp, 512), BF16)
    vec = jax.ShapeDtypeStruct((1, D_CONV), F32)
    return pl.pallas_call(
        body, name="out_bwd", grid=(nt,),
        in_specs=[_row_spec(D_MODEL), _row_spec(D_MODEL), _row_spec(512), _row_spec(512), _row_spec(512),
                  _row_spec(512), _full_spec((D_MODEL, D_MODEL)), _row_spec(512), _row_spec(512),
                  _full_spec((512, 512)), _full_spec((1, 512)), _full_spec((1, 512))],
        out_specs=[_t3_spec(512), _row_spec(512), _row_spec(512), _full_spec((D_MODEL, D_MODEL)), _row_spec(512),
                   _full_spec((512, 512)), _full_spec((1, 512)), _full_spec((1, 512)), _full_spec((1, 512))],
        out_shape=[jax.ShapeDtypeStruct((nt, 512, TILE), BF16), rm, rm, jax.ShapeDtypeStruct((D_MODEL, D_MODEL), F32),
                   jax.ShapeDtypeStruct((lp, D_CONV), F32), jax.ShapeDtypeStruct((512, 512), F32), vec, vec, vec],
        compiler_params=_cp(56, ("arbitrary",)),
    )(dz, y, o, ga, pw, gc, w_out, hc, co, w_pw, g_cv, b_cv)


def _conv_bwd_taps(dco, u, ug, conv_w, nt):
    lp = nt * TILE

    def body(dco_ref, dcon_ref, u_ref, ug_ref, up_ref, ugp_ref, w3_ref, du_ref, dug_ref, dw_ref, ext, dext, sh, dsh,
             dhg_s, dw_s):
        i = pl.program_id(0)

        @pl.when(i == 0)
        def _():
            dw_s[...] = jnp.zeros_like(dw_s)

        prev = _glu(up_ref[...], ugp_ref[...], i - 1)
        ext[0:HALO, :] = jnp.where(i > 0, prev[TILE - HALO:, :], 0.0)
        ext[HALO:HALO + TILE, :] = _glu(u_ref[...], ug_ref[...], i)
        ext[HALO + TILE:, :] = jnp.zeros((8, D_CONV), F32)
        dext[0:TILE, :] = dco_ref[...]
        dext[TILE:TILE + HALO, :] = jnp.where(i < nt - 1, dcon_ref[0:HALO, :], 0.0)
        dext[TILE + HALO:, :] = jnp.zeros((8, D_CONV), F32)
        _shifted_copies(sh, ext)
        _shifted_copies(dsh, dext)
        stripe = 32

        def stripe_body(rb, carry):
            row0 = pl.multiple_of(rb * stripe, stripe)
            dco = dco_ref[pl.ds(row0, stripe), :]
            dhg = jnp.zeros((stripe, D_CONV), F32)
            for t in range(CONV_WIDTH):
                off = HALO - (CONV_WIDTH - 1) + t
                back = CONV_WIDTH - 1 - t
                prod = dco * sh[off % 8, pl.ds((off // 8) * 8 + row0, stripe), :]
                part = prod[0:8, :]
                for r8 in range(1, stripe // 8):
                    part = part + prod[8 * r8:8 * r8 + 8, :]
                dw_s[t] += part
                dhg = dhg + w3_ref[t] * dsh[back % 8, pl.ds((back // 8) * 8 + row0, stripe), :]
            dhg_s[pl.ds(row0, stripe), :] = dhg
            return carry

        lax.fori_loop(0, TILE // stripe, stripe_body, 0)

        @pl.when(i == nt - 1)
        def _():
            dw_ref[...] = jnp.sum(dw_s[...], axis=1)

        dhg = jnp.where(_row_mask(i, (TILE, D_CONV)), dhg_s[...], 0.0)
        sg = _sigmoid(ug_ref[...].astype(F32))
        du_ref[...] = (dhg * sg).astype(BF16)
        dug_ref[...] = (dhg * u_ref[...].astype(F32) * sg * (1.0 - sg)).astype(BF16)

    rm = jax.ShapeDtypeStruct((lp, D_CONV), BF16)
    nxt = pl.BlockSpec((TILE, 512), lambda i: (jnp.minimum(i + 1, nt - 1), 0))
    ext_t = pltpu.VMEM((EXT_ROWS, D_CONV), F32)
    sh_t = pltpu.VMEM((8, SHIFT_ROWS, D_CONV), F32)
    return pl.pallas_call(
        body, name="conv_bwd_taps", grid=(nt,),
        in_specs=[_row_spec(512), nxt, _row_spec(512), _row_spec(512), _row_spec(512, shift=True),
                  _row_spec(512, shift=True), _full_spec((32, 1, 512))],
        out_specs=[_row_spec(512), _row_spec(512), _full_spec((32, 512))],
        out_shape=[rm, rm, jax.ShapeDtypeStruct((32, D_CONV), F32)],
        scratch_shapes=[ext_t, ext_t, sh_t, sh_t, pltpu.VMEM((TILE, D_CONV), F32), pltpu.VMEM((32, 8, D_CONV), F32)],
        compiler_params=_cp(48, ("arbitrary",)),
    )(dco, dco, u, ug, u, ug, conv_w.reshape(32, 1, D_CONV))


def _cumsum_bwd(dck, dcq4, fl, bf_pad, nt):
    lp = nt * TILE

    def body(dck_ref, dcq_ref, fl_ref, bf_ref, dfl_ref, dbf_ref, carry):
        i = pl.program_id(0)
        tile = nt - 1 - i

        @pl.when(i == 0)
        def _():
            carry[...] = jnp.zeros_like(carry)
            dbf_ref[...] = jnp.zeros_like(dbf_ref)

        dc = jnp.zeros((TILE, LANES), F32)
        for p in range(N_HEADS // 2):
            dq_rows = jnp.concatenate([dcq_ref[p, 0], jnp.zeros((LANES - 8, TILE), F32)], axis=0)
            both = dck_ref[p] + dq_rows.T
            dc = dc + (both if p == 0 else pltpu.roll(both, 2 * p, 1))
        r = lax.broadcasted_iota(jnp.int32, (TILE, TILE), 0)
        c = lax.broadcasted_iota(jnp.int32, (TILE, TILE), 1)
        triu = (c >= r).astype(F32)
        dlf = jnp.dot(triu, dc, precision=lax.Precision.HIGHEST, preferred_element_type=F32) + carry[...]
        carry[...] = dlf[0:1, :]
        z = fl_ref[...] + bf_ref[...]
        lane = lax.broadcasted_iota(jnp.int32, (TILE, LANES), 1)
        dfl = jnp.where(_row_mask(tile, (TILE, LANES)) & (lane < N_HEADS), dlf * _sigmoid(-z), 0.0)
        dfl_ref[...] = dfl.astype(BF16)
        dbf_ref[...] += jnp.sum(dfl, axis=0, keepdims=True)

    rev = lambda i: (nt - 1 - i, 0)
    return pl.pallas_call(
        body, name="cumsum_bwd", grid=(nt,),
        in_specs=[pl.BlockSpec((N_HEADS // 2, TILE, LANES), lambda i: (0, nt - 1 - i, 0)),
                  pl.BlockSpec((N_HEADS // 2, 1, 8, TILE), lambda i: (0, nt - 1 - i, 0, 0)),
                  pl.BlockSpec((TILE, LANES), rev), _full_spec((1, LANES))],
        out_specs=[pl.BlockSpec((TILE, LANES), rev), _full_spec((1, LANES))],
        out_shape=[jax.ShapeDtypeStruct((lp, LANES), BF16), jax.ShapeDtypeStruct((1, LANES), F32)],
        scratch_shapes=[pltpu.VMEM((1, LANES), F32)],
        compiler_params=_cp(32, ("arbitrary",)),
    )(dck, dcq4, fl, bf_pad)


def _dw_rowmajor(hb, secs, nt):
    n = len(secs)

    def body(*refs):
        hb_ref, sec_refs, out_refs = refs[0], refs[1:1 + n], refs[1 + n:]
        i = pl.program_id(0)

        @pl.when(i == 0)
        def _():
            for o_ref in out_refs:
                o_ref[...] = jnp.zeros_like(o_ref)

        hb_t = hb_ref[...]
        for s_ref, o_ref in zip(sec_refs, out_refs):
            o_ref[...] += lax.dot_general(hb_t, s_ref[...], (((0,), (0,)), ((), ())), preferred_element_type=F32)

    return pl.pallas_call(
        body, name="dw_rowmajor", grid=(nt,),
        in_specs=[_row_spec(D_MODEL)] + [_row_spec(s.shape[1]) for s in secs],
        out_specs=[_full_spec((D_MODEL, s.shape[1])) for s in secs],
        out_shape=[jax.ShapeDtypeStruct((D_MODEL, s.shape[1]), F32) for s in secs],
        compiler_params=_cp(48, ("arbitrary",)),
    )(hb, *secs)


def _dw_transposed(hb, secs_t3, nt):
    n = len(secs_t3)

    def body(*refs):
        hb_ref, sec_refs, out_refs = refs[0], refs[1:1 + n], refs[1 + n:]
        i = pl.program_id(0)

        @pl.when(i == 0)
        def _():
            for o_ref in out_refs:
                o_ref[...] = jnp.zeros_like(o_ref)

        hb_t = hb_ref[...]
        for s_ref, o_ref in zip(sec_refs, out_refs):
            o_ref[...] += jnp.dot(s_ref[0], hb_t, preferred_element_type=F32)

    return pl.pallas_call(
        body, name="dw_transposed", grid=(nt,),
        in_specs=[_row_spec(D_MODEL)] + [_t3_spec(512) for _ in secs_t3],
        out_specs=[_full_spec((512, D_MODEL)) for _ in secs_t3],
        out_shape=[jax.ShapeDtypeStruct((512, D_MODEL), F32) for _ in secs_t3],
        compiler_params=_cp(40, ("arbitrary",)),
    )(hb, *secs_t3)


def _dh_bwd(secs, secs_t3, w_rm, w_t, dz, x, metapad, g_in, nt):
    n, m = len(secs), len(secs_t3)
    offs = OFF_GA_R + np.cumsum([0] + [s.shape[1] for s in secs])

    def body(*refs):
        sec_refs, t3_refs = refs[:n], refs[n:n + m]
        wrm_ref, wt_ref, dz_ref, x_ref, mp_ref, g_ref = refs[n + m:n + m + 6]
        dx_ref, dmeta_ref, dg_ref, db_ref = refs[n + m + 6:]
        i = pl.program_id(0)

        @pl.when(i == 0)
        def _():
            dg_ref[...] = jnp.zeros_like(dg_ref)
            db_ref[...] = jnp.zeros_like(db_ref)

        dh = ALPHA * dz_ref[...]
        for s_ref, lo, hi in zip(sec_refs, offs[:-1], offs[1:]):
            dh = dh + lax.dot_general(s_ref[...], wrm_ref[:, lo:hi], (((1,), (1,)), ((), ())),
                                      preferred_element_type=F32)
        for idx, t_ref in enumerate(t3_refs):
            dh = dh + lax.dot_general(t_ref[0], wt_ref[idx * 512:(idx + 1) * 512, :], (((0,), (0,)), ((), ())),
                                      preferred_element_type=F32)
        x0 = jnp.where(i == 0, mp_ref[...], x_ref[...])
        xhat, rstd = _ln_stats(x0)
        dg_ref[...] += jnp.sum(dh * xhat, axis=0, keepdims=True)
        db_ref[...] += jnp.sum(dh, axis=0, keepdims=True)
        dx = _ln_bwd(dh, xhat, rstd, g_ref[...])
        dx_ref[...] = dx

        @pl.when(i == 0)
        def _():
            dmeta_ref[...] = dx

    seq = (nt - 1) * TILE
    return pl.pallas_call(
        body, name="dh_bwd", grid=(nt,),
        in_specs=[_row_spec(s.shape[1]) for s in secs] + [_t3_spec(512) for _ in secs_t3]
        + [_full_spec(w_rm.shape), _full_spec(w_t.shape), _row_spec(D_MODEL), _row_spec(D_MODEL, shift=True),
           _full_spec((TILE, D_MODEL)), _full_spec((1, D_MODEL))],
        out_specs=[_row_spec(D_MODEL, shift=True), _full_spec((TILE, D_MODEL)), _full_spec((1, D_MODEL)),
                   _full_spec((1, D_MODEL))],
        out_shape=[jax.ShapeDtypeStruct((seq, D_MODEL), F32), jax.ShapeDtypeStruct((TILE, D_MODEL), F32),
                   jax.ShapeDtypeStruct((1, D_MODEL), F32), jax.ShapeDtypeStruct((1, D_MODEL), F32)],
        compiler_params=_cp(56, ("arbitrary",)),
    )(*secs, *secs_t3, w_rm, w_t, dz, x, metapad, g_in)


RB = 256
SMALL_ROWS = 48


def _repack_weights(all_in, all_small):
    n_cw = D_CONV // N_DEV

    def body(a_ref, s_ref, wr_ref, wt_ref, mp_ref, cw_ref):
        full = jnp.concatenate([a_ref[d].T[:, :SHARD_IN] for d in range(N_DEV)], axis=1)
        qkv = full[:, :1536]
        wr_ref[:, :1536] = qkv
        wr_ref[:, 1536:OFF_F_R] = full[:, 1544:]
        wr_ref[:, OFF_F_R:] = jnp.concatenate([full[:, 1536:1544], jnp.zeros((RB, LANES - N_HEADS), BF16)], axis=1)
        wt_ref[...] = qkv.T

        @pl.when(pl.program_id(0) == 0)
        def _():
            mp_ref[0:PAD, :] = jnp.zeros((PAD, D_MODEL), F32)
            mp_ref[PAD:, :] = jnp.concatenate([s_ref[d, 0:N_META, :] for d in range(N_DEV)], axis=1)
            cw_ref[...] = jnp.concatenate([s_ref[d, N_META:, 0:n_cw] for d in range(N_DEV)], axis=1)

    return pl.pallas_call(
        body, name="repack_weights", grid=(D_MODEL // RB,),
        in_specs=[pl.BlockSpec((N_DEV, 512, RB), lambda i: (0, 0, i)), _full_spec((N_DEV, SMALL_ROWS, LANES))],
        out_specs=[pl.BlockSpec((RB, W_COLS), lambda i: (i, 0)), pl.BlockSpec((1536, RB), lambda i: (0, i)),
                   _full_spec((TILE, D_MODEL)), _full_spec((32, D_CONV))],
        out_shape=[jax.ShapeDtypeStruct((D_MODEL, W_COLS), BF16), jax.ShapeDtypeStruct((1536, D_MODEL), BF16),
                   jax.ShapeDtypeStruct((TILE, D_MODEL), F32), jax.ShapeDtypeStruct((32, D_CONV), F32)],
        compiler_params=_cp(40, ("arbitrary",)),
    )(all_in, all_small)


def _unpack_dw_in(dw_rm, dw_t):
    def body(dga_ref, du_ref, dug_ref, dgc_ref, dfl_ref, dq_ref, dk_ref, dv_ref, out_ref, outb_ref):
        full = jnp.concatenate([dq_ref[...].T, dk_ref[...].T, dv_ref[...].T, dfl_ref[:, 0:N_HEADS], dga_ref[...],
                                du_ref[...], dug_ref[...], dgc_ref[...]], axis=1)
        pad = jnp.zeros((RB, 512 - SHARD_IN), F32)
        for d in range(N_DEV):
            blk = jnp.concatenate([full[:, SHARD_IN * d:SHARD_IN * (d + 1)], pad], axis=1).T
            out_ref[d] = blk
            outb_ref[d] = blk.astype(BF16)

    rm = pl.BlockSpec((RB, 512), lambda i: (i, 0))
    tr = pl.BlockSpec((512, RB), lambda i: (0, i))
    blocks = pl.BlockSpec((N_DEV, 512, RB), lambda i: (0, 0, i))
    return pl.pallas_call(
        body, name="unpack_dw_in", grid=(D_MODEL // RB,),
        in_specs=[rm, rm, rm, rm, pl.BlockSpec((RB, LANES), lambda i: (i, 0)), tr, tr, tr],
        out_specs=[blocks, blocks],
        out_shape=[jax.ShapeDtypeStruct((N_DEV, 512, D_MODEL), F32), jax.ShapeDtypeStruct((N_DEV, 512, D_MODEL), BF16)],
        compiler_params=_cp(48, ("arbitrary",)),
    )(*dw_rm, *dw_t)


def _local_step(x, target, metapad, cw, w_r, w_t, w_pw_full, w_out_full, ln_in_g, ln_in_b, b_f, conv_b, ln_conv_g,
                ln_conv_b, ln_out_g, ln_out_b):
    seq = x.shape[0]
    nt = seq // TILE + 1
    row = lambda a: a.reshape(1, -1).astype(F32)
    bf_pad = jnp.pad(row(b_f), ((0, 0), (0, LANES - N_HEADS)))
    g_in, b_in = row(ln_in_g), row(ln_in_b)
    g_cv, b_cv, c_b = row(ln_conv_g), row(ln_conv_b), row(conv_b)
    g_out, b_out = row(ln_out_g), row(ln_out_b)

    hb, qT3, kT3, vT3, k, v, ga, u, ug, gc, fl = _proj_fwd(x, metapad, g_in, b_in, w_r, nt)
    kx3 = _cumsum_fwd(fl, bf_pad, nt)
    oT3, o, lse4 = _attn_fwd(qT3, k, kx3, vT3, nt)
    co, hc, pw = _conv_fwd(u, ug, cw, c_b, g_cv, b_cv, w_pw_full, nt)
    y, dz, loss, dg_out, db_out = _out_fwd(o, ga, pw, gc, x, metapad, g_in, b_in, w_out_full, g_out, b_out,
                                            target, nt)
    doT3, dga, dgc, dw_out, dco, dw_pw, dg_cv, db_cv, dc_b = _out_bwd(dz, y, o, ga, pw, gc, w_out_full, hc, co,
                                                                      w_pw_full, g_cv, b_cv, nt)
    du, dug, dcw = _conv_bwd_taps(dco, u, ug, cw, nt)
    dqT3, dkT3, dvT3, dck, dcq4 = _attn_bwd(qT3, kT3, k, kx3, v, oT3, doT3, lse4, nt)
    dfl, dbf = _cumsum_bwd(dck, dcq4, fl, bf_pad, nt)
    secs = (dga, du, dug, dgc, dfl)
    secs_t3 = (dqT3, dkT3, dvT3)
    dw_rm = _dw_rowmajor(hb, secs, nt)
    dw_t = _dw_transposed(hb, secs_t3, nt)
    grad_x, dmetapad, dg_in, db_in = _dh_bwd(secs, secs_t3, w_r, w_t, dz, x, metapad, g_in, nt)
    pieces = dict(loss=loss, metapad=dmetapad, ln_in_g=dg_in, ln_in_b=db_in, w_in_rm=dw_rm, w_in_t=dw_t, b_f=dbf,
                  conv_w=dcw, conv_b=dc_b, ln_conv_g=dg_cv, ln_conv_b=db_cv, w_pw=dw_pw, w_out=dw_out,
                  ln_out_g=dg_out, ln_out_b=db_out)
    return grad_x, pieces


MESH = pl.DeviceIdType.MESH
ANY = pl.BlockSpec(memory_space=pl.ANY)


def _mesh_pos():
    return lax.axis_index("x"), lax.axis_index("y"), lax.axis_index("c")


GATHER_SEMS = 8


def _gather_body(x_refs, out_refs, send_sems, recv_sems, local_sems):
    n = len(x_refs)
    x, y, c = _mesh_pos()
    me, sibling = (x, y, c), (x, y, 1 - c)
    xn, yn, dg = (1 - x, y), (x, 1 - y), (1 - x, 1 - y)

    def slot(a, px, py, pc, half=None):
        blk = out_refs[a].at[4 * px + 2 * py + pc]
        if half is None:
            return blk
        rows = blk.shape[0] // 2
        return blk.at[pl.ds(half * rows, rows)]

    def copy(a, k, block, to, src=None, half=None):
        return pltpu.make_async_remote_copy(
            src_ref=slot(a, *block, half) if src is None else src, dst_ref=slot(a, *block, half),
            send_sem=send_sems.at[GATHER_SEMS * a + k], recv_sem=recv_sems.at[GATHER_SEMS * a + k], device_id=to,
            device_id_type=MESH)

    arrays = range(n)
    mine = [pltpu.make_async_copy(x_refs[a], slot(a, *me), local_sems.at[a]) for a in arrays]
    for cp in mine:
        cp.start()
    sent = []
    for a in arrays:
        sent += [copy(a, 0, me, sibling, src=x_refs[a]), copy(a, 1, me, (*xn, c), src=x_refs[a]),
                 copy(a, 2, me, (*yn, c), src=x_refs[a])]
    for cp in sent:
        cp.start()

    def also(cp):
        cp.start()
        sent.append(cp)

    for a in arrays:
        copy(a, 2, (*yn, c), me).wait_recv()
        also(copy(a, 3, (*yn, c), (*xn, c), half=0))
        also(copy(a, 6, (*yn, c), sibling))
        copy(a, 1, (*xn, c), me).wait_recv()
        also(copy(a, 4, (*xn, c), (*yn, c), half=1))
        also(copy(a, 5, (*xn, c), sibling))
    for a in arrays:
        copy(a, 3, (*dg, c), me, half=0).wait_recv()
        copy(a, 4, (*dg, c), me, half=1).wait_recv()
        also(copy(a, 7, (*dg, c), sibling))
    for a in arrays:
        copy(a, 0, sibling, me).wait_recv()
        copy(a, 5, (*xn, 1 - c), me).wait_recv()
        copy(a, 6, (*yn, 1 - c), me).wait_recv()
        copy(a, 7, (*dg, 1 - c), me).wait_recv()
    for cp in sent:
        cp.wait_send()
    for cp in mine:
        cp.wait()


def _all_gather(blks, name):
    n = len(blks)

    def body(*refs):
        _gather_body(refs[:n], refs[n:2 * n], *refs[2 * n:])

    return pl.pallas_call(
        body, name=name, out_shape=[jax.ShapeDtypeStruct((N_DEV, *b.shape), b.dtype) for b in blks],
        in_specs=[ANY] * n, out_specs=[ANY] * n,
        scratch_shapes=[pltpu.SemaphoreType.DMA((GATHER_SEMS * n,)), pltpu.SemaphoreType.DMA((GATHER_SEMS * n,)),
                        pltpu.SemaphoreType.DMA((n,))],
    )(*blks)


def _exchange_sibling(g8s, small):
    n = len(g8s)

    def body(*refs):
        g_refs, s_ref, out_refs, a_ref = refs[:n], refs[n], refs[n + 1:2 * n + 1], refs[2 * n + 1]
        send_sems, recv_sems, a_send, a_recv, a_local = refs[2 * n + 2:]
        x, y, c = _mesh_pos()
        cps = [pltpu.make_async_remote_copy(
            src_ref=g_refs[a].at[2 * q + (1 - c)], dst_ref=out_refs[a].at[q], send_sem=send_sems.at[4 * a + q],
            recv_sem=recv_sems.at[4 * a + q], device_id=(x, y, 1 - c), device_id_type=MESH)
            for a in range(n) for q in range(4)]
        for cp in cps:
            cp.start()
        _gather_body([s_ref], [a_ref], a_send, a_recv, a_local)
        for cp in cps:
            cp.wait()

    outs = pl.pallas_call(
        body, name="rs_sibling",
        out_shape=[jax.ShapeDtypeStruct((4, *g.shape[1:]), g.dtype) for g in g8s]
        + [jax.ShapeDtypeStruct((N_DEV, *small.shape), small.dtype)],
        in_specs=[ANY] * (n + 1), out_specs=[ANY] * (n + 1),
        scratch_shapes=[pltpu.SemaphoreType.DMA((4 * n,)), pltpu.SemaphoreType.DMA((4 * n,)),
                        pltpu.SemaphoreType.DMA((GATHER_SEMS,)), pltpu.SemaphoreType.DMA((GATHER_SEMS,)),
                        pltpu.SemaphoreType.DMA((1,))],
    )(*g8s, small)
    return outs[:n], outs[n]


def _exchange_chips(p4s):
    n = len(p4s)

    def body(*refs):
        p_refs, out_refs, send_sems, recv_sems = refs[:n], refs[n:2 * n], refs[2 * n], refs[2 * n + 1]
        x, y, c = _mesh_pos()
        chips = [(1 - x, y), (x, 1 - y), (1 - x, 1 - y)]
        cps = [pltpu.make_async_remote_copy(
            src_ref=p_refs[a].at[2 * cx + cy], dst_ref=out_refs[a].at[k], send_sem=send_sems.at[3 * a + k],
            recv_sem=recv_sems.at[3 * a + k], device_id=(cx, cy, c), device_id_type=MESH)
            for k, (cx, cy) in enumerate(chips) for a in range(n)]
        for cp in cps:
            cp.start()
        for cp in cps:
            cp.wait()

    return pl.pallas_call(
        body, name="rs_chips", out_shape=[jax.ShapeDtypeStruct((3, *p.shape[1:]), p.dtype) for p in p4s],
        in_specs=[ANY] * n, out_specs=[ANY] * n,
        scratch_shapes=[pltpu.SemaphoreType.DMA((3 * n,)), pltpu.SemaphoreType.DMA((3 * n,))],
    )(*p4s)


def _rs_add_sibling(g8s, recvs, c_idx):
    n = len(g8s)

    def body(s_ref, *refs):
        g_refs, r_refs, p32_refs, pb_refs = (refs[k * n:(k + 1) * n] for k in range(4))
        for g_ref, r_ref, p32_ref, pb_ref in zip(g_refs, r_refs, p32_refs, pb_refs):
            p = g_ref[0] + r_ref[0].astype(F32)
            p32_ref[0] = p
            pb_ref[0] = p.astype(BF16)

    blk = lambda g: pl.BlockSpec((1, *g.shape[1:]), lambda q, s: (q, 0, 0))
    grid_spec = pltpu.PrefetchScalarGridSpec(
        num_scalar_prefetch=1, grid=(4,),
        in_specs=[pl.BlockSpec((1, *g.shape[1:]), lambda q, s: (2 * q + s[0], 0, 0)) for g in g8s]
        + [blk(g) for g in g8s],
        out_specs=[blk(g) for g in g8s] * 2)
    outs = pl.pallas_call(
        body, name="rs_add_sibling", grid_spec=grid_spec,
        out_shape=[jax.ShapeDtypeStruct((4, *g.shape[1:]), F32) for g in g8s]
        + [jax.ShapeDtypeStruct((4, *g.shape[1:]), BF16) for g in g8s],
        compiler_params=_cp(48, ("arbitrary",)),
    )(c_idx, *g8s, *recvs)
    return outs[:n], outs[n:]


def _rs_add_chips(p32s, recvs, q_idx):
    def body(s_ref, pin_ref, pout_ref, ppw_ref, rin_ref, rout_ref, rpw_ref, gin_ref, gout_ref, gpw_ref):
        def total(p_ref, r_ref):
            return ((p_ref[0] + r_ref[0].astype(F32)) + r_ref[1].astype(F32)) + r_ref[2].astype(F32)

        gin_ref[...] = total(pin_ref, rin_ref)[:SHARD_IN, :]
        gout_ref[0] = total(pout_ref, rout_ref)
        gpw_ref[0] = total(ppw_ref, rpw_ref)

    own = lambda p: pl.BlockSpec((1, *p.shape[1:]), lambda i, s: (s[0], 0, 0))
    whole = lambda shape: pl.BlockSpec(shape, lambda i, s: (0,) * len(shape))
    out_shapes = [(SHARD_IN, D_MODEL), (1, *p32s[1].shape[1:]), (1, *p32s[2].shape[1:])]
    grid_spec = pltpu.PrefetchScalarGridSpec(
        num_scalar_prefetch=1, grid=(1,),
        in_specs=[own(p) for p in p32s] + [whole(r.shape) for r in recvs],
        out_specs=[whole(s) for s in out_shapes])
    return pl.pallas_call(
        body, name="rs_add_chips", grid_spec=grid_spec,
        out_shape=[jax.ShapeDtypeStruct(s, F32) for s in out_shapes],
        compiler_params=_cp(48, ("arbitrary",)),
    )(q_idx, *p32s, *recvs)


SMALL_ROWS_G = 64
SMALL_LAYOUT = {
    "metapad": (0, N_META, D_MODEL), "conv_w": (16, 32, D_CONV), "ln_in_g": (48, 1, D_MODEL),
    "ln_in_b": (49, 1, D_MODEL), "b_f": (50, 1, LANES), "conv_b": (51, 1, D_CONV), "ln_conv_g": (52, 1, D_CONV),
    "ln_conv_b": (53, 1, D_CONV), "ln_out_g": (54, 1, D_MODEL), "ln_out_b": (55, 1, D_MODEL), "loss": (56, 1, LANES)}


def _pack_small(pieces):
    names = list(SMALL_LAYOUT)

    def body(*refs):
        out_ref = refs[-1]
        out_ref[...] = jnp.zeros_like(out_ref)
        for name, ref in zip(names, refs[:-1]):
            r0, nr, nl = SMALL_LAYOUT[name]
            src = ref[PAD:, :] if name == "metapad" else ref[...]
            out_ref[r0:r0 + nr, 0:nl] = src

    return pl.pallas_call(body, name="pack_small", out_shape=jax.ShapeDtypeStruct((SMALL_ROWS_G, D_MODEL), F32),
                          compiler_params=_cp(16))(*[pieces[n] for n in names])


def _sum_small(gathered):
    names = list(SMALL_LAYOUT)

    def body(a_ref, *out_refs):
        acc = a_ref[0]
        for d in range(1, N_DEV):
            acc = acc + a_ref[d]
        for name, ref in zip(names, out_refs):
            r0, nr, nl = SMALL_LAYOUT[name]
            ref[...] = acc[r0:r0 + nr, 0:nl]

    outs = pl.pallas_call(
        body, name="sum_small",
        out_shape=[jax.ShapeDtypeStruct(SMALL_LAYOUT[n][1:], F32) for n in names], compiler_params=_cp(16))(gathered)
    return dict(zip(names, outs))


def _adamw(ws, gs, ms, vs):
    n = len(ws)
    c1 = 1.0 - ADAM_B1 ** ADAM_STEP
    c2 = 1.0 - ADAM_B2 ** ADAM_STEP

    def body(*refs):
        w_refs, g_refs, m_refs, v_refs = (refs[k * n:(k + 1) * n] for k in range(4))
        d_refs, nm_refs, nv_refs = (refs[(4 + k) * n:(5 + k) * n] for k in range(3))
        for w_ref, g_ref, m_ref, v_ref, d_ref, nm_ref, nv_ref in zip(w_refs, g_refs, m_refs, v_refs, d_refs,
                                                                     nm_refs, nv_refs):
            g = g_ref[...]
            m = ADAM_B1 * m_ref[...] + (1.0 - ADAM_B1) * g
            v = ADAM_B2 * v_ref[...] + (1.0 - ADAM_B2) * (g * g)
            nm_ref[...] = m
            nv_ref[...] = v
            d_ref[...] = -ADAM_LR * ((m / c1) / (jnp.sqrt(v / c2) + ADAM_EPS) + ADAM_WD * w_ref[...])

    shapes = [jax.ShapeDtypeStruct(w.shape, F32) for w in ws]
    outs = pl.pallas_call(body, name="adamw", out_shape=shapes * 3, compiler_params=_cp(48))(*ws, *gs, *ms, *vs)
    return outs[:n], outs[n:2 * n], outs[2 * n:]


W_NAMES = ("meta", "ln_in_g", "ln_in_b", "w_in", "b_f", "conv_w", "conv_b", "ln_conv_g", "ln_conv_b", "w_pw",
           "w_out", "ln_out_g", "ln_out_b")


def kernel(x, meta, ln_in_g, ln_in_b, w_in, b_f, conv_w, conv_b, ln_conv_g, ln_conv_b, w_pw, w_out, ln_out_g, ln_out_b, loss_target, m_meta, m_ln_in_g, m_ln_in_b, m_w_in, m_b_f, m_conv_w, m_conv_b, m_ln_conv_g, m_ln_conv_b, m_w_pw, m_w_out, m_ln_out_g, m_ln_out_b, v_meta, v_ln_in_g, v_ln_in_b, v_w_in, v_b_f, v_conv_w, v_conv_b, v_ln_conv_g, v_ln_conv_b, v_w_pw, v_w_out, v_ln_out_g, v_ln_out_b):
    mx, my, mc = _mesh_pos()
    me = 4 * mx + 2 * my + mc
    n_meta_sh = D_MODEL // N_DEV
    n_cw_sh = D_CONV // N_DEV
    n_out_sh = D_MODEL // N_DEV
    n_pw_sh = D_CONV // N_DEV

    small_w = jnp.concatenate([meta, jnp.pad(conv_w[0], ((0, 1), (0, LANES - n_cw_sh)))], axis=0)
    all_in, all_out, all_pw, all_small = _all_gather(
        [jnp.pad(w_in[0].T, ((0, 512 - SHARD_IN), (0, 0))).astype(BF16), w_out[0].astype(BF16), w_pw[0].astype(BF16),
         small_w], "gather_weights")
    w_r, w_t, metapad, cw = _repack_weights(all_in, all_small)
    w_out_full = all_out.reshape(D_MODEL, D_MODEL)
    w_pw_full = all_pw.reshape(D_CONV, D_CONV)

    grad_x, pc = _local_step(x[0], loss_target[0], metapad, cw, w_r, w_t, w_pw_full, w_out_full, ln_in_g, ln_in_b,
                             b_f[0], conv_b[0], ln_conv_g[0], ln_conv_b[0], ln_out_g[0], ln_out_b[0])

    g_in8, g_in8_b = _unpack_dw_in(pc["w_in_rm"], pc["w_in_t"])
    g8s = [g_in8, pc["w_out"].reshape(N_DEV, n_out_sh, D_MODEL), pc["w_pw"].reshape(N_DEV, n_pw_sh, D_CONV)]
    from_sibling, all_small_g = _exchange_sibling([g_in8_b] + g8s[1:], _pack_small(pc))
    p32s, pbs = _rs_add_sibling(g8s, from_sibling, jnp.reshape(mc, (1,)).astype(jnp.int32))
    from_chips = _exchange_chips(pbs)
    g_w_in, g_w_out, g_w_pw = _rs_add_chips(p32s, from_chips, jnp.reshape(2 * mx + my, (1,)).astype(jnp.int32))

    sm = _sum_small(all_small_g)
    grads = {
        "meta": lax.dynamic_slice_in_dim(sm["metapad"], me * n_meta_sh, n_meta_sh, axis=1),
        "ln_in_g": sm["ln_in_g"].reshape(D_MODEL), "ln_in_b": sm["ln_in_b"].reshape(D_MODEL), "w_in": g_w_in.T[None],
        "b_f": sm["b_f"][:, :N_HEADS],
        "conv_w": lax.dynamic_slice_in_dim(sm["conv_w"], me * n_cw_sh, n_cw_sh, axis=1)[None, :CONV_WIDTH],
        "conv_b": sm["conv_b"], "ln_conv_g": sm["ln_conv_g"], "ln_conv_b": sm["ln_conv_b"], "w_pw": g_w_pw,
        "w_out": g_w_out, "ln_out_g": sm["ln_out_g"], "ln_out_b": sm["ln_out_b"]}
    loss_all = sm["loss"][0, 0]

    weights = dict(meta=meta, ln_in_g=ln_in_g, ln_in_b=ln_in_b, w_in=w_in, b_f=b_f, conv_w=conv_w, conv_b=conv_b,
                   ln_conv_g=ln_conv_g, ln_conv_b=ln_conv_b, w_pw=w_pw, w_out=w_out, ln_out_g=ln_out_g,
                   ln_out_b=ln_out_b)
    moms = dict(meta=m_meta, ln_in_g=m_ln_in_g, ln_in_b=m_ln_in_b, w_in=m_w_in, b_f=m_b_f, conv_w=m_conv_w,
                conv_b=m_conv_b, ln_conv_g=m_ln_conv_g, ln_conv_b=m_ln_conv_b, w_pw=m_w_pw, w_out=m_w_out,
                ln_out_g=m_ln_out_g, ln_out_b=m_ln_out_b)
    vels = dict(meta=v_meta, ln_in_g=v_ln_in_g, ln_in_b=v_ln_in_b, w_in=v_w_in, b_f=v_b_f, conv_w=v_conv_w,
                conv_b=v_conv_b, ln_conv_g=v_ln_conv_g, ln_conv_b=v_ln_conv_b, w_pw=v_w_pw, w_out=v_w_out,
                ln_out_g=v_ln_out_g, ln_out_b=v_ln_out_b)

    def to_kernel(name, a):
        if name == "w_in":
            return a[0].T
        return a.reshape(1, -1) if a.ndim == 1 else a

    def from_kernel(name, a):
        return a.T[None] if name == "w_in" else a.reshape(weights[name].shape)

    upd = _adamw(*[[to_kernel(n, d[n]) for n in W_NAMES] for d in (weights, grads, moms, vels)])
    deltas, new_m, new_v = ([from_kernel(n, a) for n, a in zip(W_NAMES, part)] for part in upd)
    return (loss_all, grad_x[None], *[grads[n] for n in W_NAMES], *deltas, *new_m, *new_v)
```

```python
import jax
import jax.numpy as jnp
import numpy as np
from jax import lax
from jax.experimental import pallas as pl
from jax.experimental.pallas import tpu as pltpu

F32 = jnp.float32
BF16 = jnp.bfloat16

D_MODEL = 1024
D_ATTN = 512
D_CONV = 512
N_HEADS = 8
HEAD_DIM = 64
N_META = 16
CONV_WIDTH = 31
LN_EPS = 1e-5
ALPHA = 2.0 ** 0.25
SCALE = HEAD_DIM ** -0.5
LOG2E = 1.4426950408889634
ADAM_LR, ADAM_B1, ADAM_B2, ADAM_EPS, ADAM_WD, ADAM_STEP = 0.001, 0.9, 0.999, 1e-08, 0.01, 10

N_DEV = 8
D_IN = 3592
SHARD_IN = D_IN // N_DEV
TILE = 256
PAD = TILE - N_META
HALO = 32
SHIFT_ROWS = TILE + HALO
EXT_ROWS = SHIFT_ROWS + 8
NEG = -1e30
LANES = 128
W_COLS = 7 * 512 + LANES
OFF_GA_R, OFF_F_R = 1536, 3584
MIB = 1024 * 1024


def _cp(vmem_mib, sem=None):
    kw = dict(vmem_limit_bytes=vmem_mib * MIB)
    if sem is not None:
        kw["dimension_semantics"] = sem
    return pltpu.CompilerParams(**kw)


def _sigmoid(x):
    return 1.0 / (1.0 + jnp.exp(-x))


def _silu_and_grad(x):
    s = _sigmoid(x)
    return x * s, s * (1.0 + x * (1.0 - s))


def _ln_stats(x):
    mu = jnp.mean(x, axis=-1, keepdims=True)
    xc = x - mu
    var = jnp.mean(xc * xc, axis=-1, keepdims=True)
    rstd = lax.rsqrt(var + LN_EPS)
    return xc * rstd, rstd


def _ln_bwd(dy, xhat, rstd, g):
    dxh = dy * g
    m1 = jnp.mean(dxh, axis=-1, keepdims=True)
    m2 = jnp.mean(dxh * xhat, axis=-1, keepdims=True)
    return rstd * (dxh - m1 - xhat * m2)


def _row_spec(cols, shift=False):
    if shift:
        return pl.BlockSpec((TILE, cols), lambda i: (jnp.maximum(i - 1, 0), 0))
    return pl.BlockSpec((TILE, cols), lambda i: (i, 0))


def _full_spec(shape):
    nd = len(shape)
    return pl.BlockSpec(shape, lambda i: (0,) * nd)


def _t3_spec(ch):
    return pl.BlockSpec((1, ch, TILE), lambda i: (i, 0, 0))


def _proj_fwd(x, metapad, g_in, b_in, w_r, nt):
    lp = nt * TILE

    def body(x_ref, mp_ref, g_ref, b_ref, w_ref, hb_ref, qT_ref, kT_ref, vT_ref, k_ref, v_ref,
             ga_ref, u_ref, ug_ref, gc_ref, fl_ref):
        i = pl.program_id(0)
        x0 = jnp.where(i == 0, mp_ref[...], x_ref[...])
        xhat, _ = _ln_stats(x0)
        hb = (xhat * g_ref[...] + b_ref[...]).astype(BF16)
        hb_ref[...] = hb

        def sec(off, n=512):
            return jnp.dot(hb, w_ref[:, off:off + n], preferred_element_type=F32)

        qT_ref[0] = (sec(0) * (SCALE * LOG2E)).T.astype(BF16)
        k = sec(512)
        kT_ref[0] = k.T.astype(BF16)
        k_ref[...] = k.astype(BF16)
        v = sec(1024)
        vT_ref[0] = v.T.astype(BF16)
        v_ref[...] = v.astype(BF16)
        ga_ref[...] = sec(OFF_GA_R).astype(BF16)
        u_ref[...] = sec(OFF_GA_R + 512).astype(BF16)
        ug_ref[...] = sec(OFF_GA_R + 1024).astype(BF16)
        gc_ref[...] = sec(OFF_GA_R + 1536).astype(BF16)
        fl_ref[...] = sec(OFF_F_R, LANES)

    t3 = jax.ShapeDtypeStruct((nt, 512, TILE), BF16)
    rm = lambda dt: jax.ShapeDtypeStruct((lp, 512), dt)
    return pl.pallas_call(
        body, name="proj_fwd", grid=(nt,),
        in_specs=[_row_spec(D_MODEL, shift=True), _full_spec((TILE, D_MODEL)), _full_spec((1, D_MODEL)),
                  _full_spec((1, D_MODEL)), _full_spec((D_MODEL, W_COLS))],
        out_specs=[_row_spec(D_MODEL), _t3_spec(512), _t3_spec(512), _t3_spec(512), _row_spec(512), _row_spec(512),
                   _row_spec(512), _row_spec(512), _row_spec(512), _row_spec(512), _row_spec(LANES)],
        out_shape=[jax.ShapeDtypeStruct((lp, D_MODEL), BF16), t3, t3, t3, rm(BF16), rm(BF16),
                   rm(BF16), rm(BF16), rm(BF16), rm(BF16), jax.ShapeDtypeStruct((lp, LANES), F32)],
        compiler_params=_cp(56, ("arbitrary",)),
    )(x, metapad, g_in, b_in, w_r)


def _row_mask(i, shape):
    r = lax.broadcasted_iota(jnp.int32, shape, 0)
    return (r >= PAD) | (i > 0)


def _cumsum_fwd(fl, bf_pad, nt):
    lp = nt * TILE

    def body(fl_ref, bf_ref, kx_ref, carry):
        i = pl.program_id(0)

        @pl.when(i == 0)
        def _():
            carry[...] = jnp.zeros_like(carry)

        z = fl_ref[...] + bf_ref[...]
        lf = jnp.minimum(z, 0.0) - jnp.log(1.0 + jnp.exp(-jnp.abs(z)))
        lane = lax.broadcasted_iota(jnp.int32, (TILE, LANES), 1)
        real = _row_mask(i, (TILE, LANES))
        lf = jnp.where(real & (lane < N_HEADS), lf, 0.0)
        r = lax.broadcasted_iota(jnp.int32, (TILE, TILE), 0)
        c = lax.broadcasted_iota(jnp.int32, (TILE, TILE), 1)
        tril = (c <= r).astype(F32)
        cs = jnp.dot(tril, lf, precision=lax.Precision.HIGHEST, preferred_element_type=F32) + carry[...]
        carry[...] = cs[TILE - 1:TILE, :]
        bias = jnp.where(real, cs * (-LOG2E), NEG)
        hi = bias.astype(BF16).astype(F32)
        mid = (bias - hi).astype(BF16).astype(F32)
        lo = (bias - hi - mid).astype(BF16).astype(F32)
        for p in range(N_HEADS // 2):
            out = jnp.zeros((TILE, LANES), F32)
            for hh in range(2):
                for part, piece in enumerate((hi, mid, lo)):
                    dst, src = 3 * hh + part, 2 * p + hh
                    moved = piece if dst == src else pltpu.roll(piece, (dst - src) % LANES, 1)
                    out = jnp.where(lane == dst, moved, out)
            kx_ref[p] = out.astype(BF16)

    return pl.pallas_call(
        body, name="cumsum_fwd", grid=(nt,),
        in_specs=[_row_spec(LANES), _full_spec((1, LANES))],
        out_specs=pl.BlockSpec((N_HEADS // 2, TILE, LANES), lambda i: (0, i, 0)),
        out_shape=jax.ShapeDtypeStruct((N_HEADS // 2, lp, LANES), BF16),
        scratch_shapes=[pltpu.VMEM((1, LANES), F32)],
        compiler_params=_cp(32, ("arbitrary",)),
    )(fl, bf_pad)


def _head_rows(blk, hh):
    r = lax.broadcasted_iota(jnp.int32, blk.shape, 0)
    return jnp.where((r >= hh * HEAD_DIM) & (r < (hh + 1) * HEAD_DIM), blk, jnp.zeros_like(blk))


def _two_heads(blk):
    return jnp.concatenate([_head_rows(blk, 0), _head_rows(blk, 1)], axis=1)


def _bias_rows():
    r = lax.broadcasted_iota(jnp.int32, (LANES, 2 * TILE), 0)
    c = lax.broadcasted_iota(jnp.int32, (LANES, 2 * TILE), 1)
    return jnp.where(((r < 3) & (c < TILE)) | ((r >= 3) & (r < 6) & (c >= TILE)), 1.0, 0.0).astype(BF16)


def _diag_mask(s):
    kpos = lax.broadcasted_iota(jnp.int32, (TILE, TILE), 0)
    qpos = lax.broadcasted_iota(jnp.int32, (TILE, TILE), 1)
    return jnp.where(kpos <= qpos, s, NEG)


def _stream(n, first, nxt, scores, update, unroll=8):
    if n == 0:
        return
    scores(first, 0)

    def step(_, idx):
        for _u in range(unroll):
            idx_b = nxt(idx)
            scores(idx_b, 1)
            update(idx, 0)
            idx = nxt(idx_b)
            scores(idx, 0)
            update(idx_b, 1)
        return idx

    steps = (n - 1) // (2 * unroll)
    idx = lax.fori_loop(0, steps, step, first)
    left = n - 2 * unroll * steps
    for r in range(left - 1):
        idx_b = nxt(idx)
        scores(idx_b, (r + 1) % 2)
        update(idx, r % 2)
        idx = idx_b
    update(idx, (left - 1) % 2)


def _next_below_diagonal(idx):
    i, j = idx
    wrap = j + 1 >= i
    return jnp.where(wrap, i + 1, i), jnp.where(wrap, 0, j + 1)


def _tile_rows(t):
    return pl.ds(pl.multiple_of(t * TILE, TILE), TILE)


def _two_streams(nt):
    load, group = [0, 0], {}
    for i in sorted(range(1, nt), reverse=True):
        g = 0 if load[0] <= load[1] else 1
        group[i] = g
        load[g] += i
    rows = [[(i, i, j) for i in range(1, nt) if group[i] == g for j in range(i)] for g in range(2)]
    length = max(len(r) for r in rows)
    rows = [r + [(nt, 0, 0)] * (length - len(r)) for r in rows]
    return group, np.asarray(rows, np.int32).reshape(2, -1), length


def _attn_fwd(qT3, k, kx3, vT3, nt):
    lp = nt * TILE
    npair = N_HEADS // 2
    group, table, n_stream = _two_streams(nt)

    def body(tab_ref, qT_ref, k_ref, kx_ref, vT_ref, oT_ref, o_ref, lse_ref, sbuf, m_0, l_0, acc_0, m_1, l_1, acc_1):
        ones = _bias_rows()
        states = ((m_0, l_0, acc_0), (m_1, l_1, acc_1))

        def scores(i, j, slot):
            qcat = jnp.concatenate([_two_heads(qT_ref[i]), ones], axis=0)
            kext = jnp.concatenate([k_ref[_tile_rows(j), :], kx_ref[0, _tile_rows(j), :]], axis=1)
            sbuf[slot] = jnp.dot(kext, qcat, preferred_element_type=F32)

        def update(st, j, slot, state, diag):
            m_s, l_s, acc_s = state
            for hh in range(2):
                s = sbuf[slot, :, hh * TILE:(hh + 1) * TILE]
                vj = vT_ref[j, hh * HEAD_DIM:(hh + 1) * HEAD_DIM, :]
                if diag:
                    s = _diag_mask(s)
                    m_new = jnp.max(s, axis=0, keepdims=True)
                    p = jnp.exp2(s - m_new)
                    l_s[st, hh] = jnp.sum(p, axis=0, keepdims=True)
                    acc_s[st, hh] = jnp.dot(vj, p.astype(BF16), preferred_element_type=F32)
                else:
                    m_prev = m_s[st, hh]
                    m_new = jnp.maximum(m_prev, jnp.max(s, axis=0, keepdims=True))
                    a = jnp.exp2(m_prev - m_new)
                    p = jnp.exp2(s - m_new)
                    l_s[st, hh] = a * l_s[st, hh] + jnp.sum(p, axis=0, keepdims=True)
                    acc_s[st, hh] = a * acc_s[st, hh] + jnp.dot(vj, p.astype(BF16), preferred_element_type=F32)
                m_s[st, hh] = m_new

        _stream(nt, jnp.int32(0), lambda t: t + 1, lambda t, slot: scores(t, t, slot),
                lambda t, slot: update(t, t, slot, states[0], True))
        for dst, src in zip(states[1], states[0]):
            dst[0:nt] = src[0:nt]
        for m_s, l_s, acc_s in states:
            m_s[nt] = jnp.full(m_s.shape[1:], NEG, F32)
            l_s[nt] = jnp.zeros(l_s.shape[1:], F32)
            acc_s[nt] = jnp.zeros(acc_s.shape[1:], F32)

        def entry(g, t):
            return tab_ref[g, 3 * t], tab_ref[g, 3 * t + 1], tab_ref[g, 3 * t + 2]

        def scores2(t, slot):
            for g in range(2):
                _, qi, kj = entry(g, t)
                scores(qi, kj, 2 * g + slot)

        def update2(t, slot):
            for g in range(2):
                st, _, kj = entry(g, t)
                update(st, kj, 2 * g + slot, states[g], False)

        _stream(n_stream, jnp.int32(0), lambda t: t + 1, scores2, update2)

        for i in range(nt):
            m_s, l_s, acc_s = states[group.get(i, 0)]
            for hh in range(2):
                l = l_s[i, hh]
                oT_ref[i, hh * HEAD_DIM:(hh + 1) * HEAD_DIM, :] = acc_s[i, hh] / l
                lse_ref[0, i, hh:hh + 1, :] = m_s[i, hh] + jnp.log(l) * LOG2E
            o_ref[i * TILE:(i + 1) * TILE, :] = oT_ref[i].T.astype(BF16)

    blk_t = pl.BlockSpec((nt, LANES, TILE), lambda p, tab: (0, p, 0))
    blk_rm = pl.BlockSpec((lp, LANES), lambda p, tab: (0, p))
    blk_px = pl.BlockSpec((1, lp, LANES), lambda p, tab: (p, 0, 0))
    blk_st = pl.BlockSpec((1, nt, 8, TILE), lambda p, tab: (p, 0, 0, 0))
    state = [pltpu.VMEM((nt + 1, 2, 1, TILE), F32), pltpu.VMEM((nt + 1, 2, 1, TILE), F32),
             pltpu.VMEM((nt + 1, 2, HEAD_DIM, TILE), F32)]
    grid_spec = pltpu.PrefetchScalarGridSpec(
        num_scalar_prefetch=1, grid=(npair,), in_specs=[blk_t, blk_rm, blk_px, blk_t],
        out_specs=[blk_t, blk_rm, blk_st], scratch_shapes=[pltpu.VMEM((4, TILE, 2 * TILE), F32)] + state + state)
    return pl.pallas_call(
        body, name="attn_fwd", grid_spec=grid_spec,
        out_shape=[jax.ShapeDtypeStruct((nt, D_ATTN, TILE), F32),
                   jax.ShapeDtypeStruct((lp, D_ATTN), BF16),
                   jax.ShapeDtypeStruct((npair, nt, 8, TILE), F32)],
        compiler_params=_cp(60, ("arbitrary",)),
    )(jnp.asarray(table), qT3, k, kx3, vT3)


def _attn_bwd(qT3, kT3, k, kx3, v, oT3, doT3, lse4, nt):
    lp = nt * TILE
    npair = N_HEADS // 2

    def body(qT_ref, kT_ref, k_ref, kx_ref, v_ref, oT_ref, doT_ref, lse_ref,
             dqT_ref, dkT_ref, dvT_ref, dck_ref, dcq_ref, sbuf, dpbuf, dq_s, dk_s, dv_s, dc_s, tp_s, tds_s):
        ones = _bias_rows()

        def scores(idx, slot):
            i, j = idx
            qcat = jnp.concatenate([_two_heads(qT_ref[i]), ones], axis=0)
            kext = jnp.concatenate([k_ref[_tile_rows(j), :], kx_ref[0, _tile_rows(j), :]], axis=1)
            sbuf[slot] = jnp.dot(kext, qcat, preferred_element_type=F32)
            dpbuf[slot] = jnp.dot(v_ref[_tile_rows(j), :], _two_heads(doT_ref[i]), preferred_element_type=F32)

        def update(idx, slot, diag):
            i, j = idx
            for hh in range(2):
                hs = slice(hh * HEAD_DIM, (hh + 1) * HEAD_DIM)
                s = sbuf[slot, :, hh * TILE:(hh + 1) * TILE]
                if diag:
                    s = _diag_mask(s)
                p = jnp.exp2(s - lse_ref[0, i, hh:hh + 1, :])
                doh = doT_ref[i, hs, :]
                delta = jnp.sum(doh.astype(F32) * oT_ref[i, hs, :], axis=0, keepdims=True)
                ds = p * (dpbuf[slot, :, hh * TILE:(hh + 1) * TILE] - delta)
                dsb = ds.astype(BF16)
                tp_s[hh] = p.astype(BF16).T
                tds_s[hh] = dsb.T
                dv = jnp.dot(doh, tp_s[hh], preferred_element_type=F32)
                dk = jnp.dot(qT_ref[i, hs, :], tds_s[hh], preferred_element_type=F32)
                dq = jnp.dot(kT_ref[j, hs, :], dsb, preferred_element_type=F32)
                dc = ds[:, :LANES] + ds[:, LANES:]
                dcq = jnp.sum(ds, axis=0, keepdims=True)
                if diag:
                    dv_s[j, hh] = dv
                    dk_s[j, hh] = dk
                    dc_s[j, hh] = dc
                    dq_s[i, hs, :] = dq
                    dcq_ref[0, i, hh:hh + 1, :] = dcq
                else:
                    dv_s[j, hh] += dv
                    dk_s[j, hh] += dk
                    dc_s[j, hh] += dc
                    dq_s[i, hs, :] += dq
                    dcq_ref[0, i, hh:hh + 1, :] += dcq

        dcq_ref[...] = jnp.zeros_like(dcq_ref)
        zero = jnp.int32(0)
        _stream(nt, (zero, zero), lambda idx: (idx[0] + 1, idx[1] + 1), scores,
                lambda idx, slot: update(idx, slot, True))
        _stream(nt * (nt - 1) // 2, (zero + 1, zero), _next_below_diagonal, scores,
                lambda idx, slot: update(idx, slot, False))

        lane = lax.broadcasted_iota(jnp.int32, (TILE, LANES), 1)

        def finish(t, carry):
            dck = jnp.zeros((TILE, LANES), F32)
            for hh in range(2):
                hs = slice(hh * HEAD_DIM, (hh + 1) * HEAD_DIM)
                dkT_ref[t, hs, :] = (dk_s[t, hh] * (1.0 / LOG2E)).astype(BF16)
                dvT_ref[t, hs, :] = dv_s[t, hh].astype(BF16)
                dck = jnp.where(lane == hh, -jnp.sum(dc_s[t, hh], axis=1, keepdims=True), dck)
            dck_ref[0, _tile_rows(t), :] = dck
            dqT_ref[t] = (dq_s[t] * SCALE).astype(BF16)
            return carry

        lax.fori_loop(0, nt, finish, 0)

    blk_t = pl.BlockSpec((nt, LANES, TILE), lambda p: (0, p, 0))
    blk_rm = pl.BlockSpec((lp, LANES), lambda p: (0, p))
    blk_px = pl.BlockSpec((1, lp, LANES), lambda p: (p, 0, 0))
    blk_st = pl.BlockSpec((1, nt, 8, TILE), lambda p: (p, 0, 0, 0))
    t3 = jax.ShapeDtypeStruct((nt, D_ATTN, TILE), BF16)
    return pl.pallas_call(
        body, name="attn_bwd", grid=(npair,),
        in_specs=[blk_t, blk_t, blk_rm, blk_px, blk_rm, blk_t, blk_t, blk_st],
        out_specs=[blk_t, blk_t, blk_t, blk_px, blk_st],
        out_shape=[t3, t3, t3, jax.ShapeDtypeStruct((npair, lp, LANES), F32),
                   jax.ShapeDtypeStruct((npair, nt, 8, TILE), F32)],
        scratch_shapes=[pltpu.VMEM((2, TILE, 2 * TILE), F32), pltpu.VMEM((2, TILE, 2 * TILE), F32),
                        pltpu.VMEM((nt, LANES, TILE), F32), pltpu.VMEM((nt, 2, HEAD_DIM, TILE), F32),
                        pltpu.VMEM((nt, 2, HEAD_DIM, TILE), F32), pltpu.VMEM((nt, 2, TILE, LANES), F32),
                        pltpu.VMEM((2, TILE, TILE), BF16), pltpu.VMEM((2, TILE, TILE), BF16)],
        compiler_params=_cp(60, ("arbitrary",)),
    )(qT3, kT3, k, kx3, v, oT3, doT3, lse4)


def _glu(u, ug, i):
    return jnp.where(_row_mask(i, u.shape), u.astype(F32) * _sigmoid(ug.astype(F32)), 0.0)


def _shifted_copies(dst, src):
    for ph in range(8):
        dst[ph] = src[ph:ph + SHIFT_ROWS, :]


def _tap_window(sh, off, lanes, row0=0, rows=TILE):
    base = (off // 8) * 8 + row0
    return sh[off % 8, base:base + rows, lanes]


def _conv_fwd(u, ug, conv_w, conv_b, g, b, w_pw, nt):
    lp = nt * TILE

    def body(u_ref, ug_ref, up_ref, ugp_ref, w_ref, cb_ref, g_ref, b_ref, wpw_ref,
             co_ref, hc_ref, pw_ref, ext, sh):
        i = pl.program_id(0)
        prev = _glu(up_ref[...], ugp_ref[...], i - 1)
        ext[0:HALO, :] = jnp.where(i > 0, prev[TILE - HALO:, :], 0.0)
        ext[HALO:HALO + TILE, :] = _glu(u_ref[...], ug_ref[...], i)
        ext[HALO + TILE:, :] = jnp.zeros((8, D_CONV), F32)
        _shifted_copies(sh, ext)
        for lb in range(D_CONV // LANES):
            lanes = slice(lb * LANES, (lb + 1) * LANES)
            acc = jnp.zeros((TILE, LANES), F32) + cb_ref[:, lanes]
            for t in range(CONV_WIDTH):
                off = HALO - (CONV_WIDTH - 1) + t
                acc = acc + w_ref[t:t + 1, lanes] * _tap_window(sh, off, lanes)
            co_ref[:, lanes] = acc
        xhat, _ = _ln_stats(co_ref[...])
        a, _ = _silu_and_grad(xhat * g_ref[...] + b_ref[...])
        hc = a.astype(BF16)
        hc_ref[...] = hc
        pw_ref[...] = jnp.dot(hc, wpw_ref[...], preferred_element_type=F32).astype(BF16)

    rm = lambda dt: jax.ShapeDtypeStruct((lp, D_CONV), dt)
    return pl.pallas_call(
        body, name="conv_fwd", grid=(nt,),
        in_specs=[_row_spec(512), _row_spec(512), _row_spec(512, shift=True), _row_spec(512, shift=True),
                  _full_spec((32, 512)), _full_spec((1, 512)), _full_spec((1, 512)), _full_spec((1, 512)),
                  _full_spec((512, 512))],
        out_specs=[_row_spec(512), _row_spec(512), _row_spec(512)],
        out_shape=[rm(F32), rm(BF16), rm(BF16)],
        scratch_shapes=[pltpu.VMEM((EXT_ROWS, D_CONV), F32), pltpu.VMEM((8, SHIFT_ROWS, D_CONV), F32)],
        compiler_params=_cp(40, ("arbitrary",)),
    )(u, ug, u, ug, conv_w, conv_b, g, b, w_pw)


def _out_fwd(o, ga, pw, gc, x, metapad, g_in, b_in, w_out, g_out, b_out, target, nt):
    lp = nt * TILE

    def body(o_ref, ga_ref, pw_ref, gc_ref, x_ref, mp_ref, gi_ref, bi_ref, wo_ref, go_ref, bo_ref, t_ref,
             y_ref, dz_ref, loss_ref, dgo_ref, dbo_ref):
        i = pl.program_id(0)

        @pl.when(i == 0)
        def _():
            loss_ref[...] = jnp.zeros_like(loss_ref)
            dgo_ref[...] = jnp.zeros_like(dgo_ref)
            dbo_ref[...] = jnp.zeros_like(dbo_ref)

        x0 = jnp.where(i == 0, mp_ref[...], x_ref[...])
        xhat, _ = _ln_stats(x0)
        h = xhat * gi_ref[...] + bi_ref[...]
        ya, _ = _silu_and_grad(ga_ref[...].astype(F32))
        yc, _ = _silu_and_grad(gc_ref[...].astype(F32))
        ya = (o_ref[...].astype(F32) * ya).astype(BF16)
        yc = (pw_ref[...].astype(F32) * yc).astype(BF16)
        y_ref[:, :D_ATTN] = ya
        y_ref[:, D_ATTN:] = yc
        z = ALPHA * h + jnp.dot(ya, wo_ref[:D_ATTN, :], preferred_element_type=F32) \
            + jnp.dot(yc, wo_ref[D_ATTN:, :], preferred_element_type=F32)
        zhat, rstd = _ln_stats(z)
        out = zhat * go_ref[...] + bo_ref[...]
        live = (i > 0).astype(F32)
        err = (out - t_ref[...]) * live
        dout = err * (1.0 / D_MODEL)
        loss_ref[...] += 0.5 * jnp.sum(jnp.sum(err * dout, axis=0, keepdims=True), axis=1, keepdims=True)
        dgo_ref[...] += jnp.sum(dout * zhat, axis=0, keepdims=True)
        dbo_ref[...] += jnp.sum(dout, axis=0, keepdims=True)
        dz_ref[...] = _ln_bwd(dout, zhat, rstd, go_ref[...])

    return pl.pallas_call(
        body, name="out_fwd", grid=(nt,),
        in_specs=[_row_spec(512), _row_spec(512), _row_spec(512), _row_spec(512),
                  _row_spec(D_MODEL, shift=True), _full_spec((TILE, D_MODEL)), _full_spec((1, D_MODEL)),
                  _full_spec((1, D_MODEL)), _full_spec((D_MODEL, D_MODEL)), _full_spec((1, D_MODEL)),
                  _full_spec((1, D_MODEL)), _row_spec(D_MODEL, shift=True)],
        out_specs=[_row_spec(D_MODEL), _row_spec(D_MODEL), _full_spec((1, LANES)), _full_spec((1, D_MODEL)),
                   _full_spec((1, D_MODEL))],
        out_shape=[jax.ShapeDtypeStruct((lp, D_MODEL), BF16), jax.ShapeDtypeStruct((lp, D_MODEL), F32),
                   jax.ShapeDtypeStruct((1, LANES), F32), jax.ShapeDtypeStruct((1, D_MODEL), F32),
                   jax.ShapeDtypeStruct((1, D_MODEL), F32)],
        compiler_params=_cp(40, ("arbitrary",)),
    )(o, ga, pw, gc, x, metapad, g_in, b_in, w_out, g_out, b_out, target)


def _out_bwd(dz, y, o, ga, pw, gc, w_out, hc, co, w_pw, g_cv, b_cv, nt):
    lp = nt * TILE

    def body(dz_ref, y_ref, o_ref, ga_ref, pw_ref, gc_ref, wo_ref, hc_ref, co_ref, wpw_ref, g_ref, b_ref,
             doT_ref, dga_ref, dgc_ref, dwo_ref, dco_ref, dwpw_ref, dg_ref, db_ref, dcb_ref):
        i = pl.program_id(0)

        @pl.when(i == 0)
        def _():
            dwo_ref[...] = jnp.zeros_like(dwo_ref)
            dwpw_ref[...] = jnp.zeros_like(dwpw_ref)
            dg_ref[...] = jnp.zeros_like(dg_ref)
            db_ref[...] = jnp.zeros_like(db_ref)
            dcb_ref[...] = jnp.zeros_like(dcb_ref)

        dzb = dz_ref[...].astype(BF16)
        nt_dims = (((1,), (1,)), ((), ()))
        tn_dims = (((0,), (0,)), ((), ()))
        dya = lax.dot_general(dzb, wo_ref[:D_ATTN, :], nt_dims, preferred_element_type=F32)
        dyc = lax.dot_general(dzb, wo_ref[D_ATTN:, :], nt_dims, preferred_element_type=F32)
        sa, sga = _silu_and_grad(ga_ref[...].astype(F32))
        sc, sgc = _silu_and_grad(gc_ref[...].astype(F32))
        doT_ref[0] = (dya * sa).T.astype(BF16)
        dga_ref[...] = (dya * o_ref[...].astype(F32) * sga).astype(BF16)
        dpw_b = (dyc * sc).astype(BF16)
        dgc_ref[...] = (dyc * pw_ref[...].astype(F32) * sgc).astype(BF16)
        dwo_ref[...] += lax.dot_general(y_ref[...], dzb, tn_dims, preferred_element_type=F32)

        dhc = lax.dot_general(dpw_b, wpw_ref[...], nt_dims, preferred_element_type=F32)
        xhat, rstd = _ln_stats(co_ref[...])
        _, sg = _silu_and_grad(xhat * g_ref[...] + b_ref[...])
        dln = dhc * sg
        dg_ref[...] += jnp.sum(dln * xhat, axis=0, keepdims=True)
        db_ref[...] += jnp.sum(dln, axis=0, keepdims=True)
        dco = _ln_bwd(dln, xhat, rstd, g_ref[...])
        dco_ref[...] = dco
        dcb_ref[...] += jnp.sum(dco, axis=0, keepdims=True)
        dwpw_ref[...] += lax.dot_general(hc_ref[...], dpw_b, tn_dims, preferred_element_type=F32)

    rm = jax.ShapeDtypeStruct((lp, 512), BF16)
    vec = jax.ShapeDtypeStruct((1, D_CONV), F32)
    return pl.pallas_call(
        body, name="out_bwd", grid=(nt,),
        in_specs=[_row_spec(D_MODEL), _row_spec(D_MODEL), _row_spec(512), _row_spec(512), _row_spec(512),
                  _row_spec(512), _full_spec((D_MODEL, D_MODEL)), _row_spec(512), _row_spec(512),
                  _full_spec((512, 512)), _full_spec((1, 512)), _full_spec((1, 512))],
        out_specs=[_t3_spec(512), _row_spec(512), _row_spec(512), _full_spec((D_MODEL, D_MODEL)), _row_spec(512),
                   _full_spec((512, 512)), _full_spec((1, 512)), _full_spec((1, 512)), _full_spec((1, 512))],
        out_shape=[jax.ShapeDtypeStruct((nt, 512, TILE), BF16), rm, rm, jax.ShapeDtypeStruct((D_MODEL, D_MODEL), F32),
                   jax.ShapeDtypeStruct((lp, D_CONV), F32), jax.ShapeDtypeStruct((512, 512), F32), vec, vec, vec],
        compiler_params=_cp(56, ("arbitrary",)),
    )(dz, y, o, ga, pw, gc, w_out, hc, co, w_pw, g_cv, b_cv)


def _conv_bwd_taps(dco, u, ug, conv_w, nt):
    lp = nt * TILE

    def body(dco_ref, dcon_ref, u_ref, ug_ref, up_ref, ugp_ref, w3_ref, du_ref, dug_ref, dw_ref, ext, dext, sh, dsh,
             dhg_s, dw_s):
        i = pl.program_id(0)

        @pl.when(i == 0)
        def _():
            dw_s[...] = jnp.zeros_like(dw_s)

        prev = _glu(up_ref[...], ugp_ref[...], i - 1)
        ext[0:HALO, :] = jnp.where(i > 0, prev[TILE - HALO:, :], 0.0)
        ext[HALO:HALO + TILE, :] = _glu(u_ref[...], ug_ref[...], i)
        ext[HALO + TILE:, :] = jnp.zeros((8, D_CONV), F32)
        dext[0:TILE, :] = dco_ref[...]
        dext[TILE:TILE + HALO, :] = jnp.where(i < nt - 1, dcon_ref[0:HALO, :], 0.0)
        dext[TILE + HALO:, :] = jnp.zeros((8, D_CONV), F32)
        _shifted_copies(sh, ext)
        _shifted_copies(dsh, dext)
        stripe = 32

        def stripe_body(rb, carry):
            row0 = pl.multiple_of(rb * stripe, stripe)
            dco = dco_ref[pl.ds(row0, stripe), :]
            dhg = jnp.zeros((stripe, D_CONV), F32)
            for t in range(CONV_WIDTH):
                off = HALO - (CONV_WIDTH - 1) + t
                back = CONV_WIDTH - 1 - t
                prod = dco * sh[off % 8, pl.ds((off // 8) * 8 + row0, stripe), :]
                part = prod[0:8, :]
                for r8 in range(1, stripe // 8):
                    part = part + prod[8 * r8:8 * r8 + 8, :]
                dw_s[t] += part
                dhg = dhg + w3_ref[t] * dsh[back % 8, pl.ds((back // 8) * 8 + row0, stripe), :]
            dhg_s[pl.ds(row0, stripe), :] = dhg
            return carry

        lax.fori_loop(0, TILE // stripe, stripe_body, 0)

        @pl.when(i == nt - 1)
        def _():
            dw_ref[...] = jnp.sum(dw_s[...], axis=1)

        dhg = jnp.where(_row_mask(i, (TILE, D_CONV)), dhg_s[...], 0.0)
        sg = _sigmoid(ug_ref[...].astype(F32))
        du_ref[...] = (dhg * sg).astype(BF16)
        dug_ref[...] = (dhg * u_ref[...].astype(F32) * sg * (1.0 - sg)).astype(BF16)

    rm = jax.ShapeDtypeStruct((lp, D_CONV), BF16)
    nxt = pl.BlockSpec((TILE, 512), lambda i: (jnp.minimum(i + 1, nt - 1), 0))
    ext_t = pltpu.VMEM((EXT_ROWS, D_CONV), F32)
    sh_t = pltpu.VMEM((8, SHIFT_ROWS, D_CONV), F32)
    return pl.pallas_call(
        body, name="conv_bwd_taps", grid=(nt,),
        in_specs=[_row_spec(512), nxt, _row_spec(512), _row_spec(512), _row_spec(512, shift=True),
                  _row_spec(512, shift=True), _full_spec((32, 1, 512))],
        out_specs=[_row_spec(512), _row_spec(512), _full_spec((32, 512))],
        out_shape=[rm, rm, jax.ShapeDtypeStruct((32, D_CONV), F32)],
        scratch_shapes=[ext_t, ext_t, sh_t, sh_t, pltpu.VMEM((TILE, D_CONV), F32), pltpu.VMEM((32, 8, D_CONV), F32)],
        compiler_params=_cp(48, ("arbitrary",)),
    )(dco, dco, u, ug, u, ug, conv_w.reshape(32, 1, D_CONV))


def _cumsum_bwd(dck, dcq4, fl, bf_pad, nt):
    lp = nt * TILE

    def body(dck_ref, dcq_ref, fl_ref, bf_ref, dfl_ref, dbf_ref, carry):
        i = pl.program_id(0)
        tile = nt - 1 - i

        @pl.when(i == 0)
        def _():
            carry[...] = jnp.zeros_like(carry)
            dbf_ref[...] = jnp.zeros_like(dbf_ref)

        dc = jnp.zeros((TILE, LANES), F32)
        for p in range(N_HEADS // 2):
            dq_rows = jnp.concatenate([dcq_ref[p, 0], jnp.zeros((LANES - 8, TILE), F32)], axis=0)
            both = dck_ref[p] + dq_rows.T
            dc = dc + (both if p == 0 else pltpu.roll(both, 2 * p, 1))
        r = lax.broadcasted_iota(jnp.int32, (TILE, TILE), 0)
        c = lax.broadcasted_iota(jnp.int32, (TILE, TILE), 1)
        triu = (c >= r).astype(F32)
        dlf = jnp.dot(triu, dc, precision=lax.Precision.HIGHEST, preferred_element_type=F32) + carry[...]
        carry[...] = dlf[0:1, :]
        z = fl_ref[...] + bf_ref[...]
        lane = lax.broadcasted_iota(jnp.int32, (TILE, LANES), 1)
        dfl = jnp.where(_row_mask(tile, (TILE, LANES)) & (lane < N_HEADS), dlf * _sigmoid(-z), 0.0)
        dfl_ref[...] = dfl.astype(BF16)
        dbf_ref[...] += jnp.sum(dfl, axis=0, keepdims=True)

    rev = lambda i: (nt - 1 - i, 0)
    return pl.pallas_call(
        body, name="cumsum_bwd", grid=(nt,),
        in_specs=[pl.BlockSpec((N_HEADS // 2, TILE, LANES), lambda i: (0, nt - 1 - i, 0)),
                  pl.BlockSpec((N_HEADS // 2, 1, 8, TILE), lambda i: (0, nt - 1 - i, 0, 0)),
                  pl.BlockSpec((TILE, LANES), rev), _full_spec((1, LANES))],
        out_specs=[pl.BlockSpec((TILE, LANES), rev), _full_spec((1, LANES))],
        out_shape=[jax.ShapeDtypeStruct((lp, LANES), BF16), jax.ShapeDtypeStruct((1, LANES), F32)],
        scratch_shapes=[pltpu.VMEM((1, LANES), F32)],
        compiler_params=_cp(32, ("arbitrary",)),
    )(dck, dcq4, fl, bf_pad)


def _dw_rowmajor(hb, secs, nt):
    n = len(secs)
    steps = 4
    rows = nt * TILE // steps
    long_rows = lambda cols: pl.BlockSpec((rows, cols), lambda i: (i, 0))

    def body(*refs):
        hb_ref, sec_refs, out_refs = refs[0], refs[1:1 + n], refs[1 + n:]
        i = pl.program_id(0)

        @pl.when(i == 0)
        def _():
            for o_ref in out_refs:
                o_ref[...] = jnp.zeros_like(o_ref)

        hb_t = hb_ref[...]
        for s_ref, o_ref in zip(sec_refs, out_refs):
            o_ref[...] += lax.dot_general(hb_t, s_ref[...], (((0,), (0,)), ((), ())), preferred_element_type=F32)

    return pl.pallas_call(
        body, name="dw_rowmajor", grid=(steps,),
        in_specs=[long_rows(D_MODEL)] + [long_rows(s.shape[1]) for s in secs],
        out_specs=[_full_spec((D_MODEL, s.shape[1])) for s in secs],
        out_shape=[jax.ShapeDtypeStruct((D_MODEL, s.shape[1]), F32) for s in secs],
        compiler_params=_cp(48, ("arbitrary",)),
    )(hb, *secs)


def _dw_transposed(hb, secs_t3, nt):
    n = len(secs_t3)

    def body(*refs):
        hb_ref, sec_refs, out_refs = refs[0], refs[1:1 + n], refs[1 + n:]
        i = pl.program_id(0)

        @pl.when(i == 0)
        def _():
            for o_ref in out_refs:
                o_ref[...] = jnp.zeros_like(o_ref)

        hb_t = hb_ref[...]
        for s_ref, o_ref in zip(sec_refs, out_refs):
            o_ref[...] += jnp.dot(s_ref[0], hb_t, preferred_element_type=F32)

    return pl.pallas_call(
        body, name="dw_transposed", grid=(nt,),
        in_specs=[_row_spec(D_MODEL)] + [_t3_spec(512) for _ in secs_t3],
        out_specs=[_full_spec((512, D_MODEL)) for _ in secs_t3],
        out_shape=[jax.ShapeDtypeStruct((512, D_MODEL), F32) for _ in secs_t3],
        compiler_params=_cp(40, ("arbitrary",)),
    )(hb, *secs_t3)


def _dh_bwd(secs, secs_t3, w_rm, w_t, dz, x, metapad, g_in, nt):
    n, m = len(secs), len(secs_t3)
    offs = OFF_GA_R + np.cumsum([0] + [s.shape[1] for s in secs])

    def body(*refs):
        sec_refs, t3_refs = refs[:n], refs[n:n + m]
        wrm_ref, wt_ref, dz_ref, x_ref, mp_ref, g_ref = refs[n + m:n + m + 6]
        dx_ref, dmeta_ref, dg_ref, db_ref = refs[n + m + 6:]
        i = pl.program_id(0)

        @pl.when(i == 0)
        def _():
            dg_ref[...] = jnp.zeros_like(dg_ref)
            db_ref[...] = jnp.zeros_like(db_ref)

        dh = ALPHA * dz_ref[...]
        for s_ref, lo, hi in zip(sec_refs, offs[:-1], offs[1:]):
            dh = dh + lax.dot_general(s_ref[...], wrm_ref[:, lo:hi], (((1,), (1,)), ((), ())),
                                      preferred_element_type=F32)
        for idx, t_ref in enumerate(t3_refs):
            dh = dh + lax.dot_general(t_ref[0], wt_ref[idx * 512:(idx + 1) * 512, :], (((0,), (0,)), ((), ())),
                                      preferred_element_type=F32)
        x0 = jnp.where(i == 0, mp_ref[...], x_ref[...])
        xhat, rstd = _ln_stats(x0)
        dg_ref[...] += jnp.sum(dh * xhat, axis=0, keepdims=True)
        db_ref[...] += jnp.sum(dh, axis=0, keepdims=True)
        dx = _ln_bwd(dh, xhat, rstd, g_ref[...])
        dx_ref[...] = dx

        @pl.when(i == 0)
        def _():
            dmeta_ref[...] = dx

    seq = (nt - 1) * TILE
    return pl.pallas_call(
        body, name="dh_bwd", grid=(nt,),
        in_specs=[_row_spec(s.shape[1]) for s in secs] + [_t3_spec(512) for _ in secs_t3]
        + [_full_spec(w_rm.shape), _full_spec(w_t.shape), _row_spec(D_MODEL), _row_spec(D_MODEL, shift=True),
           _full_spec((TILE, D_MODEL)), _full_spec((1, D_MODEL))],
        out_specs=[_row_spec(D_MODEL, shift=True), _full_spec((TILE, D_MODEL)), _full_spec((1, D_MODEL)),
                   _full_spec((1, D_MODEL))],
        out_shape=[jax.ShapeDtypeStruct((seq, D_MODEL), F32), jax.ShapeDtypeStruct((TILE, D_MODEL), F32),
                   jax.ShapeDtypeStruct((1, D_MODEL), F32), jax.ShapeDtypeStruct((1, D_MODEL), F32)],
        compiler_params=_cp(56, ("arbitrary",)),
    )(*secs, *secs_t3, w_rm, w_t, dz, x, metapad, g_in)


RB = 256
SMALL_ROWS = 48


def _repack_weights(all_in, all_small):
    n_cw = D_CONV // N_DEV

    def body(a_ref, s_ref, wr_ref, wt_ref, mp_ref, cw_ref):
        full = jnp.concatenate([a_ref[d].T[:, :SHARD_IN] for d in range(N_DEV)], axis=1)
        qkv = full[:, :1536]
        wr_ref[:, :1536] = qkv
        wr_ref[:, 1536:OFF_F_R] = full[:, 1544:]
        wr_ref[:, OFF_F_R:] = jnp.concatenate([full[:, 1536:1544], jnp.zeros((RB, LANES - N_HEADS), BF16)], axis=1)
        wt_ref[...] = qkv.T

        @pl.when(pl.program_id(0) == 0)
        def _():
            mp_ref[0:PAD, :] = jnp.zeros((PAD, D_MODEL), F32)
            mp_ref[PAD:, :] = jnp.concatenate([s_ref[d, 0:N_META, :] for d in range(N_DEV)], axis=1)
            cw_ref[...] = jnp.concatenate([s_ref[d, N_META:, 0:n_cw] for d in range(N_DEV)], axis=1)

    return pl.pallas_call(
        body, name="repack_weights", grid=(D_MODEL // RB,),
        in_specs=[pl.BlockSpec((N_DEV, 512, RB), lambda i: (0, 0, i)), _full_spec((N_DEV, SMALL_ROWS, LANES))],
        out_specs=[pl.BlockSpec((RB, W_COLS), lambda i: (i, 0)), pl.BlockSpec((1536, RB), lambda i: (0, i)),
                   _full_spec((TILE, D_MODEL)), _full_spec((32, D_CONV))],
        out_shape=[jax.ShapeDtypeStruct((D_MODEL, W_COLS), BF16), jax.ShapeDtypeStruct((1536, D_MODEL), BF16),
                   jax.ShapeDtypeStruct((TILE, D_MODEL), F32), jax.ShapeDtypeStruct((32, D_CONV), F32)],
        compiler_params=_cp(40, ("arbitrary",)),
    )(all_in, all_small)


def _unpack_dw_in(dw_rm, dw_t):
    def body(dga_ref, du_ref, dug_ref, dgc_ref, dfl_ref, dq_ref, dk_ref, dv_ref, out_ref, outb_ref):
        full = jnp.concatenate([dq_ref[...].T, dk_ref[...].T, dv_ref[...].T, dfl_ref[:, 0:N_HEADS], dga_ref[...],
                                du_ref[...], dug_ref[...], dgc_ref[...]], axis=1)
        pad = jnp.zeros((RB, 512 - SHARD_IN), F32)
        for d in range(N_DEV):
            blk = jnp.concatenate([full[:, SHARD_IN * d:SHARD_IN * (d + 1)], pad], axis=1).T
            out_ref[d] = blk
            outb_ref[d] = blk.astype(BF16)

    rm = pl.BlockSpec((RB, 512), lambda i: (i, 0))
    tr = pl.BlockSpec((512, RB), lambda i: (0, i))
    blocks = pl.BlockSpec((N_DEV, 512, RB), lambda i: (0, 0, i))
    return pl.pallas_call(
        body, name="unpack_dw_in", grid=(D_MODEL // RB,),
        in_specs=[rm, rm, rm, rm, pl.BlockSpec((RB, LANES), lambda i: (i, 0)), tr, tr, tr],
        out_specs=[blocks, blocks],
        out_shape=[jax.ShapeDtypeStruct((N_DEV, 512, D_MODEL), F32), jax.ShapeDtypeStruct((N_DEV, 512, D_MODEL), BF16)],
        compiler_params=_cp(48, ("arbitrary",)),
    )(*dw_rm, *dw_t)


def _local_step(x, target, metapad, cw, w_r, w_t, w_pw_full, w_out_full, ln_in_g, ln_in_b, b_f, conv_b, ln_conv_g,
                ln_conv_b, ln_out_g, ln_out_b):
    seq = x.shape[0]
    nt = seq // TILE + 1
    row = lambda a: a.reshape(1, -1).astype(F32)
    bf_pad = jnp.pad(row(b_f), ((0, 0), (0, LANES - N_HEADS)))
    g_in, b_in = row(ln_in_g), row(ln_in_b)
    g_cv, b_cv, c_b = row(ln_conv_g), row(ln_conv_b), row(conv_b)
    g_out, b_out = row(ln_out_g), row(ln_out_b)

    hb, qT3, kT3, vT3, k, v, ga, u, ug, gc, fl = _proj_fwd(x, metapad, g_in, b_in, w_r, nt)
    kx3 = _cumsum_fwd(fl, bf_pad, nt)
    oT3, o, lse4 = _attn_fwd(qT3, k, kx3, vT3, nt)
    co, hc, pw = _conv_fwd(u, ug, cw, c_b, g_cv, b_cv, w_pw_full, nt)
    y, dz, loss, dg_out, db_out = _out_fwd(o, ga, pw, gc, x, metapad, g_in, b_in, w_out_full, g_out, b_out,
                                            target, nt)
    doT3, dga, dgc, dw_out, dco, dw_pw, dg_cv, db_cv, dc_b = _out_bwd(dz, y, o, ga, pw, gc, w_out_full, hc, co,
                                                                      w_pw_full, g_cv, b_cv, nt)
    du, dug, dcw = _conv_bwd_taps(dco, u, ug, cw, nt)
    dqT3, dkT3, dvT3, dck, dcq4 = _attn_bwd(qT3, kT3, k, kx3, v, oT3, doT3, lse4, nt)
    dfl, dbf = _cumsum_bwd(dck, dcq4, fl, bf_pad, nt)
    secs = (dga, du, dug, dgc, dfl)
    secs_t3 = (dqT3, dkT3, dvT3)
    dw_rm = _dw_rowmajor(hb, secs, nt)
    dw_t = _dw_transposed(hb, secs_t3, nt)
    grad_x, dmetapad, dg_in, db_in = _dh_bwd(secs, secs_t3, w_r, w_t, dz, x, metapad, g_in, nt)
    pieces = dict(loss=loss, metapad=dmetapad, ln_in_g=dg_in, ln_in_b=db_in, w_in_rm=dw_rm, w_in_t=dw_t, b_f=dbf,
                  conv_w=dcw, conv_b=dc_b, ln_conv_g=dg_cv, ln_conv_b=db_cv, w_pw=dw_pw, w_out=dw_out,
                  ln_out_g=dg_out, ln_out_b=db_out)
    return grad_x, pieces


MESH = pl.DeviceIdType.MESH
ANY = pl.BlockSpec(memory_space=pl.ANY)


def _mesh_pos():
    return lax.axis_index("x"), lax.axis_index("y"), lax.axis_index("c")


GATHER_SEMS = 8


def _gather_body(x_refs, out_refs, send_sems, recv_sems, local_sems):
    n = len(x_refs)
    x, y, c = _mesh_pos()
    me, sibling = (x, y, c), (x, y, 1 - c)
    xn, yn, dg = (1 - x, y), (x, 1 - y), (1 - x, 1 - y)

    def slot(a, px, py, pc, half=None):
        blk = out_refs[a].at[4 * px + 2 * py + pc]
        if half is None:
            return blk
        rows = blk.shape[0] // 2
        return blk.at[pl.ds(half * rows, rows)]

    def copy(a, k, block, to, src=None, half=None):
        return pltpu.make_async_remote_copy(
            src_ref=slot(a, *block, half) if src is None else src, dst_ref=slot(a, *block, half),
            send_sem=send_sems.at[GATHER_SEMS * a + k], recv_sem=recv_sems.at[GATHER_SEMS * a + k], device_id=to,
            device_id_type=MESH)

    arrays = range(n)
    mine = [pltpu.make_async_copy(x_refs[a], slot(a, *me), local_sems.at[a]) for a in arrays]
    for cp in mine:
        cp.start()
    sent = []
    for a in arrays:
        sent += [copy(a, 0, me, sibling, src=x_refs[a]), copy(a, 1, me, (*xn, c), src=x_refs[a]),
                 copy(a, 2, me, (*yn, c), src=x_refs[a])]
    for cp in sent:
        cp.start()

    def also(cp):
        cp.start()
        sent.append(cp)

    for a in arrays:
        copy(a, 2, (*yn, c), me).wait_recv()
        also(copy(a, 3, (*yn, c), (*xn, c), half=0))
        also(copy(a, 6, (*yn, c), sibling))
        copy(a, 1, (*xn, c), me).wait_recv()
        also(copy(a, 4, (*xn, c), (*yn, c), half=1))
        also(copy(a, 5, (*xn, c), sibling))
    for a in arrays:
        copy(a, 3, (*dg, c), me, half=0).wait_recv()
        copy(a, 4, (*dg, c), me, half=1).wait_recv()
        also(copy(a, 7, (*dg, c), sibling))
    for a in arrays:
        copy(a, 0, sibling, me).wait_recv()
        copy(a, 5, (*xn, 1 - c), me).wait_recv()
        copy(a, 6, (*yn, 1 - c), me).wait_recv()
        copy(a, 7, (*dg, 1 - c), me).wait_recv()
    for cp in sent:
        cp.wait_send()
    for cp in mine:
        cp.wait()


def _all_gather(blks, name):
    n = len(blks)

    def body(*refs):
        _gather_body(refs[:n], refs[n:2 * n], *refs[2 * n:])

    return pl.pallas_call(
        body, name=name, out_shape=[jax.ShapeDtypeStruct((N_DEV, *b.shape), b.dtype) for b in blks],
        in_specs=[ANY] * n, out_specs=[ANY] * n,
        scratch_shapes=[pltpu.SemaphoreType.DMA((GATHER_SEMS * n,)), pltpu.SemaphoreType.DMA((GATHER_SEMS * n,)),
                        pltpu.SemaphoreType.DMA((n,))],
    )(*blks)


def _exchange_sibling(g8s, small):
    n = len(g8s)

    def body(*refs):
        g_refs, s_ref, out_refs, a_ref = refs[:n], refs[n], refs[n + 1:2 * n + 1], refs[2 * n + 1]
        send_sems, recv_sems, a_send, a_recv, a_local = refs[2 * n + 2:]
        x, y, c = _mesh_pos()
        cps = [pltpu.make_async_remote_copy(
            src_ref=g_refs[a].at[2 * q + (1 - c)], dst_ref=out_refs[a].at[q], send_sem=send_sems.at[4 * a + q],
            recv_sem=recv_sems.at[4 * a + q], device_id=(x, y, 1 - c), device_id_type=MESH)
            for a in range(n) for q in range(4)]
        for cp in cps:
            cp.start()
        _gather_body([s_ref], [a_ref], a_send, a_recv, a_local)
        for cp in cps:
            cp.wait()

    outs = pl.pallas_call(
        body, name="rs_sibling",
        out_shape=[jax.ShapeDtypeStruct((4, *g.shape[1:]), g.dtype) for g in g8s]
        + [jax.ShapeDtypeStruct((N_DEV, *small.shape), small.dtype)],
        in_specs=[ANY] * (n + 1), out_specs=[ANY] * (n + 1),
        scratch_shapes=[pltpu.SemaphoreType.DMA((4 * n,)), pltpu.SemaphoreType.DMA((4 * n,)),
                        pltpu.SemaphoreType.DMA((GATHER_SEMS,)), pltpu.SemaphoreType.DMA((GATHER_SEMS,)),
                        pltpu.SemaphoreType.DMA((1,))],
    )(*g8s, small)
    return outs[:n], outs[n]


def _exchange_chips(p4s):
    n = len(p4s)

    def body(*refs):
        p_refs, out_refs, send_sems, recv_sems = refs[:n], refs[n:2 * n], refs[2 * n], refs[2 * n + 1]
        x, y, c = _mesh_pos()
        chips = [(1 - x, y), (x, 1 - y), (1 - x, 1 - y)]
        cps = [pltpu.make_async_remote_copy(
            src_ref=p_refs[a].at[2 * cx + cy], dst_ref=out_refs[a].at[k], send_sem=send_sems.at[3 * a + k],
            recv_sem=recv_sems.at[3 * a + k], device_id=(cx, cy, c), device_id_type=MESH)
            for k, (cx, cy) in enumerate(chips) for a in range(n)]
        for cp in cps:
            cp.start()
        for cp in cps:
            cp.wait()

    return pl.pallas_call(
        body, name="rs_chips", out_shape=[jax.ShapeDtypeStruct((3, *p.shape[1:]), p.dtype) for p in p4s],
        in_specs=[ANY] * n, out_specs=[ANY] * n,
        scratch_shapes=[pltpu.SemaphoreType.DMA((3 * n,)), pltpu.SemaphoreType.DMA((3 * n,))],
    )(*p4s)


def _rs_add_sibling(g8s, recvs, c_idx):
    n = len(g8s)

    def body(s_ref, *refs):
        g_refs, r_refs, p32_refs, pb_refs = (refs[k * n:(k + 1) * n] for k in range(4))
        for g_ref, r_ref, p32_ref, pb_ref in zip(g_refs, r_refs, p32_refs, pb_refs):
            p = g_ref[0] + r_ref[0].astype(F32)
            p32_ref[0] = p
            pb_ref[0] = p.astype(BF16)

    blk = lambda g: pl.BlockSpec((1, *g.shape[1:]), lambda q, s: (q, 0, 0))
    grid_spec = pltpu.PrefetchScalarGridSpec(
        num_scalar_prefetch=1, grid=(4,),
        in_specs=[pl.BlockSpec((1, *g.shape[1:]), lambda q, s: (2 * q + s[0], 0, 0)) for g in g8s]
        + [blk(g) for g in g8s],
        out_specs=[blk(g) for g in g8s] * 2)
    outs = pl.pallas_call(
        body, name="rs_add_sibling", grid_spec=grid_spec,
        out_shape=[jax.ShapeDtypeStruct((4, *g.shape[1:]), F32) for g in g8s]
        + [jax.ShapeDtypeStruct((4, *g.shape[1:]), BF16) for g in g8s],
        compiler_params=_cp(48, ("arbitrary",)),
    )(c_idx, *g8s, *recvs)
    return outs[:n], outs[n:]


def _rs_add_chips(p32s, recvs, q_idx):
    def body(s_ref, pin_ref, pout_ref, ppw_ref, rin_ref, rout_ref, rpw_ref, gin_ref, gout_ref, gpw_ref):
        def total(p_ref, r_ref):
            return ((p_ref[0] + r_ref[0].astype(F32)) + r_ref[1].astype(F32)) + r_ref[2].astype(F32)

        gin_ref[...] = total(pin_ref, rin_ref)[:SHARD_IN, :]
        gout_ref[0] = total(pout_ref, rout_ref)
        gpw_ref[0] = total(ppw_ref, rpw_ref)

    own = lambda p: pl.BlockSpec((1, *p.shape[1:]), lambda i, s: (s[0], 0, 0))
    whole = lambda shape: pl.BlockSpec(shape, lambda i, s: (0,) * len(shape))
    out_shapes = [(SHARD_IN, D_MODEL), (1, *p32s[1].shape[1:]), (1, *p32s[2].shape[1:])]
    grid_spec = pltpu.PrefetchScalarGridSpec(
        num_scalar_prefetch=1, grid=(1,),
        in_specs=[own(p) for p in p32s] + [whole(r.shape) for r in recvs],
        out_specs=[whole(s) for s in out_shapes])
    return pl.pallas_call(
        body, name="rs_add_chips", grid_spec=grid_spec,
        out_shape=[jax.ShapeDtypeStruct(s, F32) for s in out_shapes],
        compiler_params=_cp(48, ("arbitrary",)),
    )(q_idx, *p32s, *recvs)


SMALL_ROWS_G = 64
SMALL_LAYOUT = {
    "metapad": (0, N_META, D_MODEL), "conv_w": (16, 32, D_CONV), "ln_in_g": (48, 1, D_MODEL),
    "ln_in_b": (49, 1, D_MODEL), "b_f": (50, 1, LANES), "conv_b": (51, 1, D_CONV), "ln_conv_g": (52, 1, D_CONV),
    "ln_conv_b": (53, 1, D_CONV), "ln_out_g": (54, 1, D_MODEL), "ln_out_b": (55, 1, D_MODEL), "loss": (56, 1, LANES)}


def _pack_small(pieces):
    names = list(SMALL_LAYOUT)

    def body(*refs):
        out_ref = refs[-1]
        out_ref[...] = jnp.zeros_like(out_ref)
        for name, ref in zip(names, refs[:-1]):
            r0, nr, nl = SMALL_LAYOUT[name]
            src = ref[PAD:, :] if name == "metapad" else ref[...]
            out_ref[r0:r0 + nr, 0:nl] = src

    return pl.pallas_call(body, name="pack_small", out_shape=jax.ShapeDtypeStruct((SMALL_ROWS_G, D_MODEL), F32),
                          compiler_params=_cp(16))(*[pieces[n] for n in names])


def _sum_small(gathered):
    names = list(SMALL_LAYOUT)

    def body(a_ref, *out_refs):
        acc = a_ref[0]
        for d in range(1, N_DEV):
            acc = acc + a_ref[d]
        for name, ref in zip(names, out_refs):
            r0, nr, nl = SMALL_LAYOUT[name]
            ref[...] = acc[r0:r0 + nr, 0:nl]

    outs = pl.pallas_call(
        body, name="sum_small",
        out_shape=[jax.ShapeDtypeStruct(SMALL_LAYOUT[n][1:], F32) for n in names], compiler_params=_cp(16))(gathered)
    return dict(zip(names, outs))


def _adamw(ws, gs, ms, vs):
    n = len(ws)
    c1 = 1.0 - ADAM_B1 ** ADAM_STEP
    c2 = 1.0 - ADAM_B2 ** ADAM_STEP

    def body(*refs):
        w_refs, g_refs, m_refs, v_refs = (refs[k * n:(k + 1) * n] for k in range(4))
        d_refs, nm_refs, nv_refs = (refs[(4 + k) * n:(5 + k) * n] for k in range(3))
        for w_ref, g_ref, m_ref, v_ref, d_ref, nm_ref, nv_ref in zip(w_refs, g_refs, m_refs, v_refs, d_refs,
                                                                     nm_refs, nv_refs):
            g = g_ref[...]
            m = ADAM_B1 * m_ref[...] + (1.0 - ADAM_B1) * g
            v = ADAM_B2 * v_ref[...] + (1.0 - ADAM_B2) * (g * g)
            nm_ref[...] = m
            nv_ref[...] = v
            d_ref[...] = -ADAM_LR * ((m / c1) / (jnp.sqrt(v / c2) + ADAM_EPS) + ADAM_WD * w_ref[...])

    shapes = [jax.ShapeDtypeStruct(w.shape, F32) for w in ws]
    outs = pl.pallas_call(body, name="adamw", out_shape=shapes * 3, compiler_params=_cp(48))(*ws, *gs, *ms, *vs)
    return outs[:n], outs[n:2 * n], outs[2 * n:]


W_NAMES = ("meta", "ln_in_g", "ln_in_b", "w_in", "b_f", "conv_w", "conv_b", "ln_conv_g", "ln_conv_b", "w_pw",
           "w_out", "ln_out_g", "ln_out_b")


def kernel(x, meta, ln_in_g, ln_in_b, w_in, b_f, conv_w, conv_b, ln_conv_g, ln_conv_b, w_pw, w_out, ln_out_g, ln_out_b, loss_target, m_meta, m_ln_in_g, m_ln_in_b, m_w_in, m_b_f, m_conv_w, m_conv_b, m_ln_conv_g, m_ln_conv_b, m_w_pw, m_w_out, m_ln_out_g, m_ln_out_b, v_meta, v_ln_in_g, v_ln_in_b, v_w_in, v_b_f, v_conv_w, v_conv_b, v_ln_conv_g, v_ln_conv_b, v_w_pw, v_w_out, v_ln_out_g, v_ln_out_b):
    mx, my, mc = _mesh_pos()
    me = 4 * mx + 2 * my + mc
    n_meta_sh = D_MODEL // N_DEV
    n_cw_sh = D_CONV // N_DEV
    n_out_sh = D_MODEL // N_DEV
    n_pw_sh = D_CONV // N_DEV

    small_w = jnp.concatenate([meta, jnp.pad(conv_w[0], ((0, 1), (0, LANES - n_cw_sh)))], axis=0)
    all_in, all_out, all_pw, all_small = _all_gather(
        [jnp.pad(w_in[0].T, ((0, 512 - SHARD_IN), (0, 0))).astype(BF16), w_out[0].astype(BF16), w_pw[0].astype(BF16),
         small_w], "gather_weights")
    w_r, w_t, metapad, cw = _repack_weights(all_in, all_small)
    w_out_full = all_out.reshape(D_MODEL, D_MODEL)
    w_pw_full = all_pw.reshape(D_CONV, D_CONV)

    grad_x, pc = _local_step(x[0], loss_target[0], metapad, cw, w_r, w_t, w_pw_full, w_out_full, ln_in_g, ln_in_b,
                             b_f[0], conv_b[0], ln_conv_g[0], ln_conv_b[0], ln_out_g[0], ln_out_b[0])

    g_in8, g_in8_b = _unpack_dw_in(pc["w_in_rm"], pc["w_in_t"])
    g8s = [g_in8, pc["w_out"].reshape(N_DEV, n_out_sh, D_MODEL), pc["w_pw"].reshape(N_DEV, n_pw_sh, D_CONV)]
    from_sibling, all_small_g = _exchange_sibling([g_in8_b] + g8s[1:], _pack_small(pc))
    p32s, pbs = _rs_add_sibling(g8s, from_sibling, jnp.reshape(mc, (1,)).astype(jnp.int32))
    from_chips = _exchange_chips(pbs)
    g_w_in, g_w_out, g_w_pw = _rs_add_chips(p32s, from_chips, jnp.reshape(2 * mx + my, (1,)).astype(jnp.int32))

    sm = _sum_small(all_small_g)
    grads = {
        "meta": lax.dynamic_slice_in_dim(sm["metapad"], me * n_meta_sh, n_meta_sh, axis=1),
        "ln_in_g": sm["ln_in_g"].reshape(D_MODEL), "ln_in_b": sm["ln_in_b"].reshape(D_MODEL), "w_in": g_w_in.T[None],
        "b_f": sm["b_f"][:, :N_HEADS],
        "conv_w": lax.dynamic_slice_in_dim(sm["conv_w"], me * n_cw_sh, n_cw_sh, axis=1)[None, :CONV_WIDTH],
        "conv_b": sm["conv_b"], "ln_conv_g": sm["ln_conv_g"], "ln_conv_b": sm["ln_conv_b"], "w_pw": g_w_pw,
        "w_out": g_w_out, "ln_out_g": sm["ln_out_g"], "ln_out_b": sm["ln_out_b"]}
    loss_all = sm["loss"][0, 0]

    weights = dict(meta=meta, ln_in_g=ln_in_g, ln_in_b=ln_in_b, w_in=w_in, b_f=b_f, conv_w=conv_w, conv_b=conv_b,
                   ln_conv_g=ln_conv_g, ln_conv_b=ln_conv_b, w_pw=w_pw, w_out=w_out, ln_out_g=ln_out_g,
                   ln_out_b=ln_out_b)
    moms = dict(meta=m_meta, ln_in_g=m_ln_in_g, ln_in_b=m_ln_in_b, w_in=m_w_in, b_f=m_b_f, conv_w=m_conv_w,
                conv_b=m_conv_b, ln_conv_g=m_ln_conv_g, ln_conv_b=m_ln_conv_b, w_pw=m_w_pw, w_out=m_w_out,
                ln_out_g=m_ln_out_g, ln_out_b=m_ln_out_b)
    vels = dict(meta=v_meta, ln_in_g=v_ln_in_g, ln_in_b=v_ln_in_b, w_in=v_w_in, b_f=v_b_f, conv_w=v_conv_w,
                conv_b=v_conv_b, ln_conv_g=v_ln_conv_g, ln_conv_b=v_ln_conv_b, w_pw=v_w_pw, w_out=v_w_out,
                ln_out_g=v_ln_out_g, ln_out_b=v_ln_out_b)

    def to_kernel(name, a):
        if name == "w_in":
            return a[0].T
        return a.reshape(1, -1) if a.ndim == 1 else a

    def from_kernel(name, a):
        return a.T[None] if name == "w_in" else a.reshape(weights[name].shape)

    upd = _adamw(*[[to_kernel(n, d[n]) for n in W_NAMES] for d in (weights, grads, moms, vels)])
    deltas, new_m, new_v = ([from_kernel(n, a) for n, a in zip(W_NAMES, part)] for part in upd)
    return (loss_all, grad_x[None], *[grads[n] for n in W_NAMES], *deltas, *new_m, *new_v)
```

```python
import jax
import jax.numpy as jnp
import numpy as np
from jax import lax
from jax.experimental import pallas as pl
from jax.experimental.pallas import tpu as pltpu

F32 = jnp.float32
BF16 = jnp.bfloat16

D_MODEL = 1024
D_ATTN = 512
D_CONV = 512
N_HEADS = 8
HEAD_DIM = 64
N_META = 16
CONV_WIDTH = 31
LN_EPS = 1e-5
ALPHA = 2.0 ** 0.25
SCALE = HEAD_DIM ** -0.5
LOG2E = 1.4426950408889634
ADAM_LR, ADAM_B1, ADAM_B2, ADAM_EPS, ADAM_WD, ADAM_STEP = 0.001, 0.9, 0.999, 1e-08, 0.01, 10

N_DEV = 8
D_IN = 3592
SHARD_IN = D_IN // N_DEV
TILE = 256
PAD = TILE - N_META
HALO = 32
SHIFT_ROWS = TILE + HALO
EXT_ROWS = SHIFT_ROWS + 8
NEG = -1e30
LANES = 128
W_COLS = 7 * 512 + LANES
OFF_GA_R, OFF_F_R = 1536, 3584
MIB = 1024 * 1024


def _cp(vmem_mib, sem=None):
    kw = dict(vmem_limit_bytes=vmem_mib * MIB)
    if sem is not None:
        kw["dimension_semantics"] = sem
    return pltpu.CompilerParams(**kw)


def _sigmoid(x):
    return 1.0 / (1.0 + jnp.exp(-x))


def _silu_and_grad(x):
    s = _sigmoid(x)
    return x * s, s * (1.0 + x * (1.0 - s))


def _ln_stats(x):
    mu = jnp.mean(x, axis=-1, keepdims=True)
    xc = x - mu
    var = jnp.mean(xc * xc, axis=-1, keepdims=True)
    rstd = lax.rsqrt(var + LN_EPS)
    return xc * rstd, rstd


def _ln_bwd(dy, xhat, rstd, g):
    dxh = dy * g
    m1 = jnp.mean(dxh, axis=-1, keepdims=True)
    m2 = jnp.mean(dxh * xhat, axis=-1, keepdims=True)
    return rstd * (dxh - m1 - xhat * m2)


def _row_spec(cols, shift=False):
    if shift:
        return pl.BlockSpec((TILE, cols), lambda i: (jnp.maximum(i - 1, 0), 0))
    return pl.BlockSpec((TILE, cols), lambda i: (i, 0))


def _full_spec(shape):
    nd = len(shape)
    return pl.BlockSpec(shape, lambda i: (0,) * nd)


def _t3_spec(ch):
    return pl.BlockSpec((1, ch, TILE), lambda i: (i, 0, 0))


def _proj_fwd(x, metapad, g_in, b_in, w_r, nt):
    lp = nt * TILE

    def body(x_ref, mp_ref, g_ref, b_ref, w_ref, h_ref, hb_ref, qT_ref, kT_ref, vT_ref, k_ref, v_ref,
             ga_ref, u_ref, ug_ref, gc_ref, fl_ref):
        i = pl.program_id(0)
        x0 = jnp.where(i == 0, mp_ref[...], x_ref[...])
        xhat, _ = _ln_stats(x0)
        h = xhat * g_ref[...] + b_ref[...]
        h_ref[...] = h
        hb = h.astype(BF16)
        hb_ref[...] = hb

        def sec(off, n=512):
            return jnp.dot(hb, w_ref[:, off:off + n], preferred_element_type=F32)

        qT_ref[0] = (sec(0) * (SCALE * LOG2E)).T.astype(BF16)
        k = sec(512)
        kT_ref[0] = k.T.astype(BF16)
        k_ref[...] = k.astype(BF16)
        v = sec(1024)
        vT_ref[0] = v.T.astype(BF16)
        v_ref[...] = v.astype(BF16)
        ga_ref[...] = sec(OFF_GA_R).astype(BF16)
        u_ref[...] = sec(OFF_GA_R + 512).astype(BF16)
        ug_ref[...] = sec(OFF_GA_R + 1024).astype(BF16)
        gc_ref[...] = sec(OFF_GA_R + 1536).astype(BF16)
        fl_ref[...] = sec(OFF_F_R, LANES)

    t3 = jax.ShapeDtypeStruct((nt, 512, TILE), BF16)
    rm = lambda dt: jax.ShapeDtypeStruct((lp, 512), dt)
    return pl.pallas_call(
        body, name="proj_fwd", grid=(nt,),
        in_specs=[_row_spec(D_MODEL, shift=True), _full_spec((TILE, D_MODEL)), _full_spec((1, D_MODEL)),
                  _full_spec((1, D_MODEL)), _full_spec((D_MODEL, W_COLS))],
        out_specs=[_row_spec(D_MODEL), _row_spec(D_MODEL), _t3_spec(512), _t3_spec(512), _t3_spec(512),
                   _row_spec(512), _row_spec(512),
                   _row_spec(512), _row_spec(512), _row_spec(512), _row_spec(512), _row_spec(LANES)],
        out_shape=[jax.ShapeDtypeStruct((lp, D_MODEL), F32), jax.ShapeDtypeStruct((lp, D_MODEL), BF16),
                   t3, t3, t3, rm(BF16), rm(BF16),
                   rm(BF16), rm(BF16), rm(BF16), rm(BF16), jax.ShapeDtypeStruct((lp, LANES), F32)],
        compiler_params=_cp(56, ("arbitrary",)),
    )(x, metapad, g_in, b_in, w_r)


def _row_mask(i, shape):
    r = lax.broadcasted_iota(jnp.int32, shape, 0)
    return (r >= PAD) | (i > 0)


def _cumsum_fwd(fl, bf_pad, nt):
    lp = nt * TILE

    def body(fl_ref, bf_ref, kx_ref, carry):
        i = pl.program_id(0)

        @pl.when(i == 0)
        def _():
            carry[...] = jnp.zeros_like(carry)

        z = fl_ref[...] + bf_ref[...]
        lf = jnp.minimum(z, 0.0) - jnp.log(1.0 + jnp.exp(-jnp.abs(z)))
        lane = lax.broadcasted_iota(jnp.int32, (TILE, LANES), 1)
        real = _row_mask(i, (TILE, LANES))
        lf = jnp.where(real & (lane < N_HEADS), lf, 0.0)
        r = lax.broadcasted_iota(jnp.int32, (TILE, TILE), 0)
        c = lax.broadcasted_iota(jnp.int32, (TILE, TILE), 1)
        tril = (c <= r).astype(F32)
        cs = jnp.dot(tril, lf, precision=lax.Precision.HIGHEST, preferred_element_type=F32) + carry[...]
        carry[...] = cs[TILE - 1:TILE, :]
        bias = jnp.where(real, cs * (-LOG2E), NEG)
        hi = bias.astype(BF16).astype(F32)
        mid = (bias - hi).astype(BF16).astype(F32)
        lo = (bias - hi - mid).astype(BF16).astype(F32)
        for p in range(N_HEADS // 2):
            out = jnp.zeros((TILE, LANES), F32)
            for hh in range(2):
                for part, piece in enumerate((hi, mid, lo)):
                    dst, src = 3 * hh + part, 2 * p + hh
                    moved = piece if dst == src else pltpu.roll(piece, (dst - src) % LANES, 1)
                    out = jnp.where(lane == dst, moved, out)
            kx_ref[p] = out.astype(BF16)

    return pl.pallas_call(
        body, name="cumsum_fwd", grid=(nt,),
        in_specs=[_row_spec(LANES), _full_spec((1, LANES))],
        out_specs=pl.BlockSpec((N_HEADS // 2, TILE, LANES), lambda i: (0, i, 0)),
        out_shape=jax.ShapeDtypeStruct((N_HEADS // 2, lp, LANES), BF16),
        scratch_shapes=[pltpu.VMEM((1, LANES), F32)],
        compiler_params=_cp(32, ("arbitrary",)),
    )(fl, bf_pad)


def _head_rows(blk, hh):
    r = lax.broadcasted_iota(jnp.int32, blk.shape, 0)
    return jnp.where((r >= hh * HEAD_DIM) & (r < (hh + 1) * HEAD_DIM), blk, jnp.zeros_like(blk))


def _two_heads(blk):
    return jnp.concatenate([_head_rows(blk, 0), _head_rows(blk, 1)], axis=1)


def _bias_rows():
    r = lax.broadcasted_iota(jnp.int32, (LANES, 2 * TILE), 0)
    c = lax.broadcasted_iota(jnp.int32, (LANES, 2 * TILE), 1)
    return jnp.where(((r < 3) & (c < TILE)) | ((r >= 3) & (r < 6) & (c >= TILE)), 1.0, 0.0).astype(BF16)


def _diag_mask(s):
    kpos = lax.broadcasted_iota(jnp.int32, (TILE, TILE), 0)
    qpos = lax.broadcasted_iota(jnp.int32, (TILE, TILE), 1)
    return jnp.where(kpos <= qpos, s, NEG)


def _stream(n, first, nxt, scores, update, unroll=8):
    if n == 0:
        return
    scores(first, 0)

    def step(_, idx):
        for _u in range(unroll):
            idx_b = nxt(idx)
            scores(idx_b, 1)
            update(idx, 0)
            idx = nxt(idx_b)
            scores(idx, 0)
            update(idx_b, 1)
        return idx

    steps = (n - 1) // (2 * unroll)
    idx = lax.fori_loop(0, steps, step, first)
    left = n - 2 * unroll * steps
    for r in range(left - 1):
        idx_b = nxt(idx)
        scores(idx_b, (r + 1) % 2)
        update(idx, r % 2)
        idx = idx_b
    update(idx, (left - 1) % 2)


def _next_below_diagonal(idx):
    i, j = idx
    wrap = j + 1 >= i
    return jnp.where(wrap, i + 1, i), jnp.where(wrap, 0, j + 1)


def _tile_rows(t):
    return pl.ds(pl.multiple_of(t * TILE, TILE), TILE)


def _two_streams(nt):
    load, group = [0, 0], {}
    for i in sorted(range(1, nt), reverse=True):
        g = 0 if load[0] <= load[1] else 1
        group[i] = g
        load[g] += i
    rows = [[(i, i, j) for i in range(1, nt) if group[i] == g for j in range(i)] for g in range(2)]
    length = max(len(r) for r in rows)
    rows = [r + [(nt, 0, 0)] * (length - len(r)) for r in rows]
    return group, np.asarray(rows, np.int32).reshape(2, -1), length


def _attn_fwd(qT3, k, kx3, vT3, nt):
    lp = nt * TILE
    npair = N_HEADS // 2
    group, table, n_stream = _two_streams(nt)

    def body(tab_ref, qT_ref, k_ref, kx_ref, vT_ref, oT_ref, o_ref, lse_ref, sbuf, m_0, l_0, acc_0, m_1, l_1, acc_1):
        ones = _bias_rows()
        states = ((m_0, l_0, acc_0), (m_1, l_1, acc_1))

        def scores(i, j, slot):
            qcat = jnp.concatenate([_two_heads(qT_ref[i]), ones], axis=0)
            kext = jnp.concatenate([k_ref[_tile_rows(j), :], kx_ref[0, _tile_rows(j), :]], axis=1)
            sbuf[slot] = jnp.dot(kext, qcat, preferred_element_type=F32)

        def update(st, j, slot, state, diag):
            m_s, l_s, acc_s = state
            for hh in range(2):
                s = sbuf[slot, :, hh * TILE:(hh + 1) * TILE]
                vj = vT_ref[j, hh * HEAD_DIM:(hh + 1) * HEAD_DIM, :]
                if diag:
                    s = _diag_mask(s)
                    m_new = jnp.max(s, axis=0, keepdims=True)
                    p = jnp.exp2(s - m_new)
                    l_s[st, hh] = jnp.sum(p, axis=0, keepdims=True)
                    acc_s[st, hh] = jnp.dot(vj, p.astype(BF16), preferred_element_type=F32)
                else:
                    m_prev = m_s[st, hh]
                    m_new = jnp.maximum(m_prev, jnp.max(s, axis=0, keepdims=True))
                    a = jnp.exp2(m_prev - m_new)
                    p = jnp.exp2(s - m_new)
                    l_s[st, hh] = a * l_s[st, hh] + jnp.sum(p, axis=0, keepdims=True)
                    acc_s[st, hh] = a * acc_s[st, hh] + jnp.dot(vj, p.astype(BF16), preferred_element_type=F32)
                m_s[st, hh] = m_new

        _stream(nt, jnp.int32(0), lambda t: t + 1, lambda t, slot: scores(t, t, slot),
                lambda t, slot: update(t, t, slot, states[0], True))
        for dst, src in zip(states[1], states[0]):
            dst[0:nt] = src[0:nt]
        for m_s, l_s, acc_s in states:
            m_s[nt] = jnp.full(m_s.shape[1:], NEG, F32)
            l_s[nt] = jnp.zeros(l_s.shape[1:], F32)
            acc_s[nt] = jnp.zeros(acc_s.shape[1:], F32)

        def entry(g, t):
            return tab_ref[g, 3 * t], tab_ref[g, 3 * t + 1], tab_ref[g, 3 * t + 2]

        def scores2(t, slot):
            for g in range(2):
                _, qi, kj = entry(g, t)
                scores(qi, kj, 2 * g + slot)

        def update2(t, slot):
            for g in range(2):
                st, _, kj = entry(g, t)
                update(st, kj, 2 * g + slot, states[g], False)

        _stream(n_stream, jnp.int32(0), lambda t: t + 1, scores2, update2)

        for i in range(nt):
            m_s, l_s, acc_s = states[group.get(i, 0)]
            for hh in range(2):
                l = l_s[i, hh]
                oT_ref[i, hh * HEAD_DIM:(hh + 1) * HEAD_DIM, :] = acc_s[i, hh] / l
                lse_ref[0, i, hh:hh + 1, :] = m_s[i, hh] + jnp.log(l) * LOG2E
            o_ref[i * TILE:(i + 1) * TILE, :] = oT_ref[i].T.astype(BF16)

    blk_t = pl.BlockSpec((nt, LANES, TILE), lambda p, tab: (0, p, 0))
    blk_rm = pl.BlockSpec((lp, LANES), lambda p, tab: (0, p))
    blk_px = pl.BlockSpec((1, lp, LANES), lambda p, tab: (p, 0, 0))
    blk_st = pl.BlockSpec((1, nt, 8, TILE), lambda p, tab: (p, 0, 0, 0))
    state = [pltpu.VMEM((nt + 1, 2, 1, TILE), F32), pltpu.VMEM((nt + 1, 2, 1, TILE), F32),
             pltpu.VMEM((nt + 1, 2, HEAD_DIM, TILE), F32)]
    grid_spec = pltpu.PrefetchScalarGridSpec(
        num_scalar_prefetch=1, grid=(npair,), in_specs=[blk_t, blk_rm, blk_px, blk_t],
        out_specs=[blk_t, blk_rm, blk_st], scratch_shapes=[pltpu.VMEM((4, TILE, 2 * TILE), F32)] + state + state)
    return pl.pallas_call(
        body, name="attn_fwd", grid_spec=grid_spec,
        out_shape=[jax.ShapeDtypeStruct((nt, D_ATTN, TILE), F32),
                   jax.ShapeDtypeStruct((lp, D_ATTN), BF16),
                   jax.ShapeDtypeStruct((npair, nt, 8, TILE), F32)],
        compiler_params=_cp(60, ("arbitrary",)),
    )(jnp.asarray(table), qT3, k, kx3, vT3)


def _attn_bwd(qT3, kT3, k, kx3, v, oT3, doT3, lse4, nt):
    lp = nt * TILE
    npair = N_HEADS // 2

    def body(qT_ref, kT_ref, k_ref, kx_ref, v_ref, oT_ref, doT_ref, lse_ref,
             dqT_ref, dkT_ref, dvT_ref, dck_ref, dcq_ref, sbuf, dpbuf, dq_s, dk_s, dv_s, dc_s, tp_s, tds_s):
        ones = _bias_rows()

        def scores(idx, slot):
            i, j = idx
            qcat = jnp.concatenate([_two_heads(qT_ref[i]), ones], axis=0)
            kext = jnp.concatenate([k_ref[_tile_rows(j), :], kx_ref[0, _tile_rows(j), :]], axis=1)
            sbuf[slot] = jnp.dot(kext, qcat, preferred_element_type=F32)
            dpbuf[slot] = jnp.dot(v_ref[_tile_rows(j), :], _two_heads(doT_ref[i]), preferred_element_type=F32)

        def update(idx, slot, diag):
            i, j = idx
            for hh in range(2):
                hs = slice(hh * HEAD_DIM, (hh + 1) * HEAD_DIM)
                s = sbuf[slot, :, hh * TILE:(hh + 1) * TILE]
                if diag:
                    s = _diag_mask(s)
                p = jnp.exp2(s - lse_ref[0, i, hh:hh + 1, :])
                doh = doT_ref[i, hs, :]
                delta = jnp.sum(doh.astype(F32) * oT_ref[i, hs, :], axis=0, keepdims=True)
                ds = p * (dpbuf[slot, :, hh * TILE:(hh + 1) * TILE] - delta)
                dsb = ds.astype(BF16)
                tp_s[hh] = p.astype(BF16).T
                tds_s[hh] = dsb.T
                dv = jnp.dot(doh, tp_s[hh], preferred_element_type=F32)
                dk = jnp.dot(qT_ref[i, hs, :], tds_s[hh], preferred_element_type=F32)
                dq = jnp.dot(kT_ref[j, hs, :], dsb, preferred_element_type=F32)
                dc = ds[:, :LANES] + ds[:, LANES:]
                dcq = jnp.sum(ds, axis=0, keepdims=True)
                if diag:
                    dv_s[j, hh] = dv
                    dk_s[j, hh] = dk
                    dc_s[j, hh] = dc
                    dq_s[i, hs, :] = dq
                    dcq_ref[0, i, hh:hh + 1, :] = dcq
                else:
                    dv_s[j, hh] += dv
                    dk_s[j, hh] += dk
                    dc_s[j, hh] += dc
                    dq_s[i, hs, :] += dq
                    dcq_ref[0, i, hh:hh + 1, :] += dcq

        dcq_ref[...] = jnp.zeros_like(dcq_ref)
        zero = jnp.int32(0)
        _stream(nt, (zero, zero), lambda idx: (idx[0] + 1, idx[1] + 1), scores,
                lambda idx, slot: update(idx, slot, True))
        _stream(nt * (nt - 1) // 2, (zero + 1, zero), _next_below_diagonal, scores,
                lambda idx, slot: update(idx, slot, False))

        lane = lax.broadcasted_iota(jnp.int32, (TILE, LANES), 1)

        def finish(t, carry):
            dck = jnp.zeros((TILE, LANES), F32)
            for hh in range(2):
                hs = slice(hh * HEAD_DIM, (hh + 1) * HEAD_DIM)
                dkT_ref[t, hs, :] = (dk_s[t, hh] * (1.0 / LOG2E)).astype(BF16)
                dvT_ref[t, hs, :] = dv_s[t, hh].astype(BF16)
                dck = jnp.where(lane == hh, -jnp.sum(dc_s[t, hh], axis=1, keepdims=True), dck)
            dck_ref[0, _tile_rows(t), :] = dck
            dqT_ref[t] = (dq_s[t] * SCALE).astype(BF16)
            return carry

        lax.fori_loop(0, nt, finish, 0)

    blk_t = pl.BlockSpec((nt, LANES, TILE), lambda p: (0, p, 0))
    blk_rm = pl.BlockSpec((lp, LANES), lambda p: (0, p))
    blk_px = pl.BlockSpec((1, lp, LANES), lambda p: (p, 0, 0))
    blk_st = pl.BlockSpec((1, nt, 8, TILE), lambda p: (p, 0, 0, 0))
    t3 = jax.ShapeDtypeStruct((nt, D_ATTN, TILE), BF16)
    return pl.pallas_call(
        body, name="attn_bwd", grid=(npair,),
        in_specs=[blk_t, blk_t, blk_rm, blk_px, blk_rm, blk_t, blk_t, blk_st],
        out_specs=[blk_t, blk_t, blk_t, blk_px, blk_st],
        out_shape=[t3, t3, t3, jax.ShapeDtypeStruct((npair, lp, LANES), F32),
                   jax.ShapeDtypeStruct((npair, nt, 8, TILE), F32)],
        scratch_shapes=[pltpu.VMEM((2, TILE, 2 * TILE), F32), pltpu.VMEM((2, TILE, 2 * TILE), F32),
                        pltpu.VMEM((nt, LANES, TILE), F32), pltpu.VMEM((nt, 2, HEAD_DIM, TILE), F32),
                        pltpu.VMEM((nt, 2, HEAD_DIM, TILE), F32), pltpu.VMEM((nt, 2, TILE, LANES), F32),
                        pltpu.VMEM((2, TILE, TILE), BF16), pltpu.VMEM((2, TILE, TILE), BF16)],
        compiler_params=_cp(60, ("arbitrary",)),
    )(qT3, kT3, k, kx3, v, oT3, doT3, lse4)


def _glu(u, ug, i):
    return jnp.where(_row_mask(i, u.shape), u.astype(F32) * _sigmoid(ug.astype(F32)), 0.0)


def _shifted_copies(dst, src):
    for ph in range(8):
        dst[ph] = src[ph:ph + SHIFT_ROWS, :]


def _tap_window(sh, off, lanes, row0=0, rows=TILE):
    base = (off // 8) * 8 + row0
    return sh[off % 8, base:base + rows, lanes]


def _conv_fwd(u, ug, conv_w, conv_b, g, b, w_pw, nt):
    lp = nt * TILE

    def body(u_ref, ug_ref, up_ref, ugp_ref, w_ref, cb_ref, g_ref, b_ref, wpw_ref,
             co_ref, hc_ref, pw_ref, ext, sh):
        i = pl.program_id(0)
        prev = _glu(up_ref[...], ugp_ref[...], i - 1)
        ext[0:HALO, :] = jnp.where(i > 0, prev[TILE - HALO:, :], 0.0)
        ext[HALO:HALO + TILE, :] = _glu(u_ref[...], ug_ref[...], i)
        ext[HALO + TILE:, :] = jnp.zeros((8, D_CONV), F32)
        _shifted_copies(sh, ext)
        for lb in range(D_CONV // LANES):
            lanes = slice(lb * LANES, (lb + 1) * LANES)
            acc = jnp.zeros((TILE, LANES), F32) + cb_ref[:, lanes]
            for t in range(CONV_WIDTH):
                off = HALO - (CONV_WIDTH - 1) + t
                acc = acc + w_ref[t:t + 1, lanes] * _tap_window(sh, off, lanes)
            co_ref[:, lanes] = acc
        xhat, _ = _ln_stats(co_ref[...])
        a, _ = _silu_and_grad(xhat * g_ref[...] + b_ref[...])
        hc = a.astype(BF16)
        hc_ref[...] = hc
        pw_ref[...] = jnp.dot(hc, wpw_ref[...], preferred_element_type=F32).astype(BF16)

    rm = lambda dt: jax.ShapeDtypeStruct((lp, D_CONV), dt)
    return pl.pallas_call(
        body, name="conv_fwd", grid=(nt,),
        in_specs=[_row_spec(512), _row_spec(512), _row_spec(512, shift=True), _row_spec(512, shift=True),
                  _full_spec((32, 512)), _full_spec((1, 512)), _full_spec((1, 512)), _full_spec((1, 512)),
                  _full_spec((512, 512))],
        out_specs=[_row_spec(512), _row_spec(512), _row_spec(512)],
        out_shape=[rm(F32), rm(BF16), rm(BF16)],
        scratch_shapes=[pltpu.VMEM((EXT_ROWS, D_CONV), F32), pltpu.VMEM((8, SHIFT_ROWS, D_CONV), F32)],
        compiler_params=_cp(40, ("arbitrary",)),
    )(u, ug, u, ug, conv_w, conv_b, g, b, w_pw)


def _out_fwd(o, ga, pw, gc, h, w_out, g_out, b_out, target, nt):
    lp = nt * TILE

    def body(o_ref, ga_ref, pw_ref, gc_ref, h_ref, wo_ref, go_ref, bo_ref, t_ref,
             y_ref, dz_ref, loss_ref, dgo_ref, dbo_ref):
        i = pl.program_id(0)

        @pl.when(i == 0)
        def _():
            loss_ref[...] = jnp.zeros_like(loss_ref)
            dgo_ref[...] = jnp.zeros_like(dgo_ref)
            dbo_ref[...] = jnp.zeros_like(dbo_ref)

        ya, _ = _silu_and_grad(ga_ref[...].astype(F32))
        yc, _ = _silu_and_grad(gc_ref[...].astype(F32))
        ya = (o_ref[...].astype(F32) * ya).astype(BF16)
        yc = (pw_ref[...].astype(F32) * yc).astype(BF16)
        y_ref[:, :D_ATTN] = ya
        y_ref[:, D_ATTN:] = yc
        z = ALPHA * h_ref[...] + jnp.dot(ya, wo_ref[:D_ATTN, :], preferred_element_type=F32) \
            + jnp.dot(yc, wo_ref[D_ATTN:, :], preferred_element_type=F32)
        zhat, rstd = _ln_stats(z)
        out = zhat * go_ref[...] + bo_ref[...]
        live = (i > 0).astype(F32)
        err = (out - t_ref[...]) * live
        dout = err * (1.0 / D_MODEL)
        loss_ref[...] += 0.5 * jnp.sum(jnp.sum(err * dout, axis=0, keepdims=True), axis=1, keepdims=True)
        dgo_ref[...] += jnp.sum(dout * zhat, axis=0, keepdims=True)
        dbo_ref[...] += jnp.sum(dout, axis=0, keepdims=True)
        dz_ref[...] = _ln_bwd(dout, zhat, rstd, go_ref[...])

    return pl.pallas_call(
        body, name="out_fwd", grid=(nt,),
        in_specs=[_row_spec(512), _row_spec(512), _row_spec(512), _row_spec(512),
                  _row_spec(D_MODEL), _full_spec((D_MODEL, D_MODEL)), _full_spec((1, D_MODEL)),
                  _full_spec((1, D_MODEL)), _row_spec(D_MODEL, shift=True)],
        out_specs=[_row_spec(D_MODEL), _row_spec(D_MODEL), _full_spec((1, LANES)), _full_spec((1, D_MODEL)),
                   _full_spec((1, D_MODEL))],
        out_shape=[jax.ShapeDtypeStruct((lp, D_MODEL), BF16), jax.ShapeDtypeStruct((lp, D_MODEL), F32),
                   jax.ShapeDtypeStruct((1, LANES), F32), jax.ShapeDtypeStruct((1, D_MODEL), F32),
                   jax.ShapeDtypeStruct((1, D_MODEL), F32)],
        compiler_params=_cp(40, ("arbitrary",)),
    )(o, ga, pw, gc, h, w_out, g_out, b_out, target)


def _out_bwd(dz, y, o, ga, pw, gc, w_out, hc, co, w_pw, g_cv, b_cv, nt):
    lp = nt * TILE

    def body(dz_ref, y_ref, o_ref, ga_ref, pw_ref, gc_ref, wo_ref, hc_ref, co_ref, wpw_ref, g_ref, b_ref,
             doT_ref, dga_ref, dgc_ref, dwo_ref, dco_ref, dwpw_ref, dg_ref, db_ref, dcb_ref):
        i = pl.program_id(0)

        @pl.when(i == 0)
        def _():
            dwo_ref[...] = jnp.zeros_like(dwo_ref)
            dwpw_ref[...] = jnp.zeros_like(dwpw_ref)
            dg_ref[...] = jnp.zeros_like(dg_ref)
            db_ref[...] = jnp.zeros_like(db_ref)
            dcb_ref[...] = jnp.zeros_like(dcb_ref)

        dzb = dz_ref[...].astype(BF16)
        nt_dims = (((1,), (1,)), ((), ()))
        tn_dims = (((0,), (0,)), ((), ()))
        dya = lax.dot_general(dzb, wo_ref[:D_ATTN, :], nt_dims, preferred_element_type=F32)
        dyc = lax.dot_general(dzb, wo_ref[D_ATTN:, :], nt_dims, preferred_element_type=F32)
        sa, sga = _silu_and_grad(ga_ref[...].astype(F32))
        sc, sgc = _silu_and_grad(gc_ref[...].astype(F32))
        doT_ref[0] = (dya * sa).T.astype(BF16)
        dga_ref[...] = (dya * o_ref[...].astype(F32) * sga).astype(BF16)
        dpw_b = (dyc * sc).astype(BF16)
        dgc_ref[...] = (dyc * pw_ref[...].astype(F32) * sgc).astype(BF16)
        dwo_ref[...] += lax.dot_general(y_ref[...], dzb, tn_dims, preferred_element_type=F32)

        dhc = lax.dot_general(dpw_b, wpw_ref[...], nt_dims, preferred_element_type=F32)
        xhat, rstd = _ln_stats(co_ref[...])
        _, sg = _silu_and_grad(xhat * g_ref[...] + b_ref[...])
        dln = dhc * sg
        dg_ref[...] += jnp.sum(dln * xhat, axis=0, keepdims=True)
        db_ref[...] += jnp.sum(dln, axis=0, keepdims=True)
        dco = _ln_bwd(dln, xhat, rstd, g_ref[...])
        dco_ref[...] = dco
        dcb_ref[...] += jnp.sum(dco, axis=0, keepdims=True)
        dwpw_ref[...] += lax.dot_general(hc_ref[...], dpw_b, tn_dims, preferred_element_type=F32)

    rm = jax.ShapeDtypeStruct((lp, 512), BF16)
    vec = jax.ShapeDtypeStruct((1, D_CONV), F32)
    return pl.pallas_call(
        body, name="out_bwd", grid=(nt,),
        in_specs=[_row_spec(D_MODEL), _row_spec(D_MODEL), _row_spec(512), _row_spec(512), _row_spec(512),
                  _row_spec(512), _full_spec((D_MODEL, D_MODEL)), _row_spec(512), _row_spec(512),
                  _full_spec((512, 512)), _full_spec((1, 512)), _full_spec((1, 512))],
        out_specs=[_t3_spec(512), _row_spec(512), _row_spec(512), _full_spec((D_MODEL, D_MODEL)), _row_spec(512),
                   _full_spec((512, 512)), _full_spec((1, 512)), _full_spec((1, 512)), _full_spec((1, 512))],
        out_shape=[jax.ShapeDtypeStruct((nt, 512, TILE), BF16), rm, rm, jax.ShapeDtypeStruct((D_MODEL, D_MODEL), F32),
                   jax.ShapeDtypeStruct((lp, D_CONV), F32), jax.ShapeDtypeStruct((512, 512), F32), vec, vec, vec],
        compiler_params=_cp(56, ("arbitrary",)),
    )(dz, y, o, ga, pw, gc, w_out, hc, co, w_pw, g_cv, b_cv)


def _conv_bwd_taps(dco, u, ug, conv_w, nt):
    lp = nt * TILE

    def body(dco_ref, dcon_ref, u_ref, ug_ref, up_ref, ugp_ref, w3_ref, du_ref, dug_ref, dw_ref, ext, dext, sh, dsh,
             dhg_s, dw_s):
        i = pl.program_id(0)

        @pl.when(i == 0)
        def _():
            dw_s[...] = jnp.zeros_like(dw_s)

        prev = _glu(up_ref[...], ugp_ref[...], i - 1)
        ext[0:HALO, :] = jnp.where(i > 0, prev[TILE - HALO:, :], 0.0)
        ext[HALO:HALO + TILE, :] = _glu(u_ref[...], ug_ref[...], i)
        ext[HALO + TILE:, :] = jnp.zeros((8, D_CONV), F32)
        dext[0:TILE, :] = dco_ref[...]
        dext[TILE:TILE + HALO, :] = jnp.where(i < nt - 1, dcon_ref[0:HALO, :], 0.0)
        dext[TILE + HALO:, :] = jnp.zeros((8, D_CONV), F32)
        _shifted_copies(sh, ext)
        _shifted_copies(dsh, dext)
        stripe = 32

        def stripe_body(rb, carry):
            row0 = pl.multiple_of(rb * stripe, stripe)
            dco = dco_ref[pl.ds(row0, stripe), :]
            dhg = jnp.zeros((stripe, D_CONV), F32)
            for t in range(CONV_WIDTH):
                off = HALO - (CONV_WIDTH - 1) + t
                back = CONV_WIDTH - 1 - t
                prod = dco * sh[off % 8, pl.ds((off // 8) * 8 + row0, stripe), :]
                part = prod[0:8, :]
                for r8 in range(1, stripe // 8):
                    part = part + prod[8 * r8:8 * r8 + 8, :]
                dw_s[t] += part
                dhg = dhg + w3_ref[t] * dsh[back % 8, pl.ds((back // 8) * 8 + row0, stripe), :]
            dhg_s[pl.ds(row0, stripe), :] = dhg
            return carry

        lax.fori_loop(0, TILE // stripe, stripe_body, 0)

        @pl.when(i == nt - 1)
        def _():
            dw_ref[...] = jnp.sum(dw_s[...], axis=1)

        dhg = jnp.where(_row_mask(i, (TILE, D_CONV)), dhg_s[...], 0.0)
        sg = _sigmoid(ug_ref[...].astype(F32))
        du_ref[...] = (dhg * sg).astype(BF16)
        dug_ref[...] = (dhg * u_ref[...].astype(F32) * sg * (1.0 - sg)).astype(BF16)

    rm = jax.ShapeDtypeStruct((lp, D_CONV), BF16)
    nxt = pl.BlockSpec((TILE, 512), lambda i: (jnp.minimum(i + 1, nt - 1), 0))
    ext_t = pltpu.VMEM((EXT_ROWS, D_CONV), F32)
    sh_t = pltpu.VMEM((8, SHIFT_ROWS, D_CONV), F32)
    return pl.pallas_call(
        body, name="conv_bwd_taps", grid=(nt,),
        in_specs=[_row_spec(512), nxt, _row_spec(512), _row_spec(512), _row_spec(512, shift=True),
                  _row_spec(512, shift=True), _full_spec((32, 1, 512))],
        out_specs=[_row_spec(512), _row_spec(512), _full_spec((32, 512))],
        out_shape=[rm, rm, jax.ShapeDtypeStruct((32, D_CONV), F32)],
        scratch_shapes=[ext_t, ext_t, sh_t, sh_t, pltpu.VMEM((TILE, D_CONV), F32), pltpu.VMEM((32, 8, D_CONV), F32)],
        compiler_params=_cp(48, ("arbitrary",)),
    )(dco, dco, u, ug, u, ug, conv_w.reshape(32, 1, D_CONV))


def _cumsum_bwd(dck, dcq4, fl, bf_pad, nt):
    lp = nt * TILE

    def body(dck_ref, dcq_ref, fl_ref, bf_ref, dfl_ref, dbf_ref, carry):
        i = pl.program_id(0)
        tile = nt - 1 - i

        @pl.when(i == 0)
        def _():
            carry[...] = jnp.zeros_like(carry)
            dbf_ref[...] = jnp.zeros_like(dbf_ref)

        dc = jnp.zeros((TILE, LANES), F32)
        for p in range(N_HEADS // 2):
            dq_rows = jnp.concatenate([dcq_ref[p, 0], jnp.zeros((LANES - 8, TILE), F32)], axis=0)
            both = dck_ref[p] + dq_rows.T
            dc = dc + (both if p == 0 else pltpu.roll(both, 2 * p, 1))
        r = lax.broadcasted_iota(jnp.int32, (TILE, TILE), 0)
        c = lax.broadcasted_iota(jnp.int32, (TILE, TILE), 1)
        triu = (c >= r).astype(F32)
        dlf = jnp.dot(triu, dc, precision=lax.Precision.HIGHEST, preferred_element_type=F32) + carry[...]
        carry[...] = dlf[0:1, :]
        z = fl_ref[...] + bf_ref[...]
        lane = lax.broadcasted_iota(jnp.int32, (TILE, LANES), 1)
        dfl = jnp.where(_row_mask(tile, (TILE, LANES)) & (lane < N_HEADS), dlf * _sigmoid(-z), 0.0)
        dfl_ref[...] = dfl.astype(BF16)
        dbf_ref[...] += jnp.sum(dfl, axis=0, keepdims=True)

    rev = lambda i: (nt - 1 - i, 0)
    return pl.pallas_call(
        body, name="cumsum_bwd", grid=(nt,),
        in_specs=[pl.BlockSpec((N_HEADS // 2, TILE, LANES), lambda i: (0, nt - 1 - i, 0)),
                  pl.BlockSpec((N_HEADS // 2, 1, 8, TILE), lambda i: (0, nt - 1 - i, 0, 0)),
                  pl.BlockSpec((TILE, LANES), rev), _full_spec((1, LANES))],
        out_specs=[pl.BlockSpec((TILE, LANES), rev), _full_spec((1, LANES))],
        out_shape=[jax.ShapeDtypeStruct((lp, LANES), BF16), jax.ShapeDtypeStruct((1, LANES), F32)],
        scratch_shapes=[pltpu.VMEM((1, LANES), F32)],
        compiler_params=_cp(32, ("arbitrary",)),
    )(dck, dcq4, fl, bf_pad)


def _dw_rowmajor(hb, secs, nt):
    n = len(secs)
    steps = 4
    rows = nt * TILE // steps
    long_rows = lambda cols: pl.BlockSpec((rows, cols), lambda i: (i, 0))

    def body(*refs):
        hb_ref, sec_refs, out_refs = refs[0], refs[1:1 + n], refs[1 + n:]
        i = pl.program_id(0)

        @pl.when(i == 0)
        def _():
            for o_ref in out_refs:
                o_ref[...] = jnp.zeros_like(o_ref)

        hb_t = hb_ref[...]
        for s_ref, o_ref in zip(sec_refs, out_refs):
            o_ref[...] += lax.dot_general(hb_t, s_ref[...], (((0,), (0,)), ((), ())), preferred_element_type=F32)

    return pl.pallas_call(
        body, name="dw_rowmajor", grid=(steps,),
        in_specs=[long_rows(D_MODEL)] + [long_rows(s.shape[1]) for s in secs],
        out_specs=[_full_spec((D_MODEL, s.shape[1])) for s in secs],
        out_shape=[jax.ShapeDtypeStruct((D_MODEL, s.shape[1]), F32) for s in secs],
        compiler_params=_cp(48, ("arbitrary",)),
    )(hb, *secs)


def _dw_transposed(hb, secs_t3, nt):
    n = len(secs_t3)

    def body(*refs):
        hb_ref, sec_refs, out_refs = refs[0], refs[1:1 + n], refs[1 + n:]
        i = pl.program_id(0)

        @pl.when(i == 0)
        def _():
            for o_ref in out_refs:
                o_ref[...] = jnp.zeros_like(o_ref)

        hb_t = hb_ref[...]
        for s_ref, o_ref in zip(sec_refs, out_refs):
            o_ref[...] += jnp.dot(s_ref[0], hb_t, preferred_element_type=F32)

    return pl.pallas_call(
        body, name="dw_transposed", grid=(nt,),
        in_specs=[_row_spec(D_MODEL)] + [_t3_spec(512) for _ in secs_t3],
        out_specs=[_full_spec((512, D_MODEL)) for _ in secs_t3],
        out_shape=[jax.ShapeDtypeStruct((512, D_MODEL), F32) for _ in secs_t3],
        compiler_params=_cp(40, ("arbitrary",)),
    )(hb, *secs_t3)


def _dh_bwd(secs, secs_t3, w_rm, w_t, dz, x, metapad, g_in, nt):
    n, m = len(secs), len(secs_t3)
    offs = OFF_GA_R + np.cumsum([0] + [s.shape[1] for s in secs])

    def body(*refs):
        sec_refs, t3_refs = refs[:n], refs[n:n + m]
        wrm_ref, wt_ref, dz_ref, x_ref, mp_ref, g_ref = refs[n + m:n + m + 6]
        dx_ref, dmeta_ref, dg_ref, db_ref = refs[n + m + 6:]
        i = pl.program_id(0)

        @pl.when(i == 0)
        def _():
            dg_ref[...] = jnp.zeros_like(dg_ref)
            db_ref[...] = jnp.zeros_like(db_ref)

        dh = ALPHA * dz_ref[...]
        for s_ref, lo, hi in zip(sec_refs, offs[:-1], offs[1:]):
            dh = dh + lax.dot_general(s_ref[...], wrm_ref[:, lo:hi], (((1,), (1,)), ((), ())),
                                      preferred_element_type=F32)
        for idx, t_ref in enumerate(t3_refs):
            dh = dh + lax.dot_general(t_ref[0], wt_ref[idx * 512:(idx + 1) * 512, :], (((0,), (0,)), ((), ())),
                                      preferred_element_type=F32)
        x0 = jnp.where(i == 0, mp_ref[...], x_ref[...])
        xhat, rstd = _ln_stats(x0)
        dg_ref[...] += jnp.sum(dh * xhat, axis=0, keepdims=True)
        db_ref[...] += jnp.sum(dh, axis=0, keepdims=True)
        dx = _ln_bwd(dh, xhat, rstd, g_ref[...])
        dx_ref[...] = dx

        @pl.when(i == 0)
        def _():
            dmeta_ref[...] = dx

    seq = (nt - 1) * TILE
    return pl.pallas_call(
        body, name="dh_bwd", grid=(nt,),
        in_specs=[_row_spec(s.shape[1]) for s in secs] + [_t3_spec(512) for _ in secs_t3]
        + [_full_spec(w_rm.shape), _full_spec(w_t.shape), _row_spec(D_MODEL), _row_spec(D_MODEL, shift=True),
           _full_spec((TILE, D_MODEL)), _full_spec((1, D_MODEL))],
        out_specs=[_row_spec(D_MODEL, shift=True), _full_spec((TILE, D_MODEL)), _full_spec((1, D_MODEL)),
                   _full_spec((1, D_MODEL))],
        out_shape=[jax.ShapeDtypeStruct((seq, D_MODEL), F32), jax.ShapeDtypeStruct((TILE, D_MODEL), F32),
                   jax.ShapeDtypeStruct((1, D_MODEL), F32), jax.ShapeDtypeStruct((1, D_MODEL), F32)],
        compiler_params=_cp(56, ("arbitrary",)),
    )(*secs, *secs_t3, w_rm, w_t, dz, x, metapad, g_in)


RB = 256
SMALL_ROWS = 48


def _repack_weights(all_in, all_small):
    n_cw = D_CONV // N_DEV

    def body(a_ref, s_ref, wr_ref, wt_ref, mp_ref, cw_ref):
        full = jnp.concatenate([a_ref[d].T[:, :SHARD_IN] for d in range(N_DEV)], axis=1)
        qkv = full[:, :1536]
        wr_ref[:, :1536] = qkv
        wr_ref[:, 1536:OFF_F_R] = full[:, 1544:]
        wr_ref[:, OFF_F_R:] = jnp.concatenate([full[:, 1536:1544], jnp.zeros((RB, LANES - N_HEADS), BF16)], axis=1)
        wt_ref[...] = qkv.T

        @pl.when(pl.program_id(0) == 0)
        def _():
            mp_ref[0:PAD, :] = jnp.zeros((PAD, D_MODEL), F32)
            mp_ref[PAD:, :] = jnp.concatenate([s_ref[d, 0:N_META, :] for d in range(N_DEV)], axis=1)
            cw_ref[...] = jnp.concatenate([s_ref[d, N_META:, 0:n_cw] for d in range(N_DEV)], axis=1)

    return pl.pallas_call(
        body, name="repack_weights", grid=(D_MODEL // RB,),
        in_specs=[pl.BlockSpec((N_DEV, 512, RB), lambda i: (0, 0, i)), _full_spec((N_DEV, SMALL_ROWS, LANES))],
        out_specs=[pl.BlockSpec((RB, W_COLS), lambda i: (i, 0)), pl.BlockSpec((1536, RB), lambda i: (0, i)),
                   _full_spec((TILE, D_MODEL)), _full_spec((32, D_CONV))],
        out_shape=[jax.ShapeDtypeStruct((D_MODEL, W_COLS), BF16), jax.ShapeDtypeStruct((1536, D_MODEL), BF16),
                   jax.ShapeDtypeStruct((TILE, D_MODEL), F32), jax.ShapeDtypeStruct((32, D_CONV), F32)],
        compiler_params=_cp(40, ("arbitrary",)),
    )(all_in, all_small)


def _unpack_dw_in(dw_rm, dw_t):
    def body(dga_ref, du_ref, dug_ref, dgc_ref, dfl_ref, dq_ref, dk_ref, dv_ref, out_ref, outb_ref):
        full = jnp.concatenate([dq_ref[...].T, dk_ref[...].T, dv_ref[...].T, dfl_ref[:, 0:N_HEADS], dga_ref[...],
                                du_ref[...], dug_ref[...], dgc_ref[...]], axis=1)
        pad = jnp.zeros((RB, 512 - SHARD_IN), F32)
        for d in range(N_DEV):
            blk = jnp.concatenate([full[:, SHARD_IN * d:SHARD_IN * (d + 1)], pad], axis=1).T
            out_ref[d] = blk
            outb_ref[d] = blk.astype(BF16)

    rm = pl.BlockSpec((RB, 512), lambda i: (i, 0))
    tr = pl.BlockSpec((512, RB), lambda i: (0, i))
    blocks = pl.BlockSpec((N_DEV, 512, RB), lambda i: (0, 0, i))
    return pl.pallas_call(
        body, name="unpack_dw_in", grid=(D_MODEL // RB,),
        in_specs=[rm, rm, rm, rm, pl.BlockSpec((RB, LANES), lambda i: (i, 0)), tr, tr, tr],
        out_specs=[blocks, blocks],
        out_shape=[jax.ShapeDtypeStruct((N_DEV, 512, D_MODEL), F32), jax.ShapeDtypeStruct((N_DEV, 512, D_MODEL), BF16)],
        compiler_params=_cp(48, ("arbitrary",)),
    )(*dw_rm, *dw_t)


def _local_step(x, target, metapad, cw, w_r, w_t, w_pw_full, w_out_full, ln_in_g, ln_in_b, b_f, conv_b, ln_conv_g,
                ln_conv_b, ln_out_g, ln_out_b):
    seq = x.shape[0]
    nt = seq // TILE + 1
    row = lambda a: a.reshape(1, -1).astype(F32)
    bf_pad = jnp.pad(row(b_f), ((0, 0), (0, LANES - N_HEADS)))
    g_in, b_in = row(ln_in_g), row(ln_in_b)
    g_cv, b_cv, c_b = row(ln_conv_g), row(ln_conv_b), row(conv_b)
    g_out, b_out = row(ln_out_g), row(ln_out_b)

    h, hb, qT3, kT3, vT3, k, v, ga, u, ug, gc, fl = _proj_fwd(x, metapad, g_in, b_in, w_r, nt)
    kx3 = _cumsum_fwd(fl, bf_pad, nt)
    oT3, o, lse4 = _attn_fwd(qT3, k, kx3, vT3, nt)
    co, hc, pw = _conv_fwd(u, ug, cw, c_b, g_cv, b_cv, w_pw_full, nt)
    y, dz, loss, dg_out, db_out = _out_fwd(o, ga, pw, gc, h, w_out_full, g_out, b_out, target, nt)
    doT3, dga, dgc, dw_out, dco, dw_pw, dg_cv, db_cv, dc_b = _out_bwd(dz, y, o, ga, pw, gc, w_out_full, hc, co,
                                                                      w_pw_full, g_cv, b_cv, nt)
    du, dug, dcw = _conv_bwd_taps(dco, u, ug, cw, nt)
    dqT3, dkT3, dvT3, dck, dcq4 = _attn_bwd(qT3, kT3, k, kx3, v, oT3, doT3, lse4, nt)
    dfl, dbf = _cumsum_bwd(dck, dcq4, fl, bf_pad, nt)
    secs = (dga, du, dug, dgc, dfl)
    secs_t3 = (dqT3, dkT3, dvT3)
    dw_rm = _dw_rowmajor(hb, secs, nt)
    dw_t = _dw_transposed(hb, secs_t3, nt)
    grad_x, dmetapad, dg_in, db_in = _dh_bwd(secs, secs_t3, w_r, w_t, dz, x, metapad, g_in, nt)
    pieces = dict(loss=loss, metapad=dmetapad, ln_in_g=dg_in, ln_in_b=db_in, w_in_rm=dw_rm, w_in_t=dw_t, b_f=dbf,
                  conv_w=dcw, conv_b=dc_b, ln_conv_g=dg_cv, ln_conv_b=db_cv, w_pw=dw_pw, w_out=dw_out,
                  ln_out_g=dg_out, ln_out_b=db_out)
    return grad_x, pieces


MESH = pl.DeviceIdType.MESH
ANY = pl.BlockSpec(memory_space=pl.ANY)


def _mesh_pos():
    return lax.axis_index("x"), lax.axis_index("y"), lax.axis_index("c")


GATHER_SEMS = 8


def _gather_body(x_refs, out_refs, send_sems, recv_sems, local_sems):
    n = len(x_refs)
    x, y, c = _mesh_pos()
    me, sibling = (x, y, c), (x, y, 1 - c)
    xn, yn, dg = (1 - x, y), (x, 1 - y), (1 - x, 1 - y)

    def slot(a, px, py, pc, half=None):
        blk = out_refs[a].at[4 * px + 2 * py + pc]
        if half is None:
            return blk
        rows = blk.shape[0] // 2
        return blk.at[pl.ds(half * rows, rows)]

    def copy(a, k, block, to, src=None, half=None):
        return pltpu.make_async_remote_copy(
            src_ref=slot(a, *block, half) if src is None else src, dst_ref=slot(a, *block, half),
            send_sem=send_sems.at[GATHER_SEMS * a + k], recv_sem=recv_sems.at[GATHER_SEMS * a + k], device_id=to,
            device_id_type=MESH)

    arrays = range(n)
    mine = [pltpu.make_async_copy(x_refs[a], slot(a, *me), local_sems.at[a]) for a in arrays]
    for cp in mine:
        cp.start()
    sent = []
    for a in arrays:
        sent += [copy(a, 0, me, sibling, src=x_refs[a]), copy(a, 1, me, (*xn, c), src=x_refs[a]),
                 copy(a, 2, me, (*yn, c), src=x_refs[a])]
    for cp in sent:
        cp.start()

    def also(cp):
        cp.start()
        sent.append(cp)

    for a in arrays:
        copy(a, 2, (*yn, c), me).wait_recv()
        also(copy(a, 3, (*yn, c), (*xn, c), half=0))
        also(copy(a, 6, (*yn, c), sibling))
        copy(a, 1, (*xn, c), me).wait_recv()
        also(copy(a, 4, (*xn, c), (*yn, c), half=1))
        also(copy(a, 5, (*xn, c), sibling))
    for a in arrays:
        copy(a, 3, (*dg, c), me, half=0).wait_recv()
        copy(a, 4, (*dg, c), me, half=1).wait_recv()
        also(copy(a, 7, (*dg, c), sibling))
    for a in arrays:
        copy(a, 0, sibling, me).wait_recv()
        copy(a, 5, (*xn, 1 - c), me).wait_recv()
        copy(a, 6, (*yn, 1 - c), me).wait_recv()
        copy(a, 7, (*dg, 1 - c), me).wait_recv()
    for cp in sent:
        cp.wait_send()
    for cp in mine:
        cp.wait()


def _all_gather(blks, name):
    n = len(blks)

    def body(*refs):
        _gather_body(refs[:n], refs[n:2 * n], *refs[2 * n:])

    return pl.pallas_call(
        body, name=name, out_shape=[jax.ShapeDtypeStruct((N_DEV, *b.shape), b.dtype) for b in blks],
        in_specs=[ANY] * n, out_specs=[ANY] * n,
        scratch_shapes=[pltpu.SemaphoreType.DMA((GATHER_SEMS * n,)), pltpu.SemaphoreType.DMA((GATHER_SEMS * n,)),
                        pltpu.SemaphoreType.DMA((n,))],
    )(*blks)


def _exchange_sibling(g8s, small):
    n = len(g8s)

    def body(*refs):
        g_refs, s_ref, out_refs, a_ref = refs[:n], refs[n], refs[n + 1:2 * n + 1], refs[2 * n + 1]
        send_sems, recv_sems, a_send, a_recv, a_local = refs[2 * n + 2:]
        x, y, c = _mesh_pos()
        cps = [pltpu.make_async_remote_copy(
            src_ref=g_refs[a].at[2 * q + (1 - c)], dst_ref=out_refs[a].at[q], send_sem=send_sems.at[4 * a + q],
            recv_sem=recv_sems.at[4 * a + q], device_id=(x, y, 1 - c), device_id_type=MESH)
            for a in range(n) for q in range(4)]
        for cp in cps:
            cp.start()
        _gather_body([s_ref], [a_ref], a_send, a_recv, a_local)
        for cp in cps:
            cp.wait()

    outs = pl.pallas_call(
        body, name="rs_sibling",
        out_shape=[jax.ShapeDtypeStruct((4, *g.shape[1:]), g.dtype) for g in g8s]
        + [jax.ShapeDtypeStruct((N_DEV, *small.shape), small.dtype)],
        in_specs=[ANY] * (n + 1), out_specs=[ANY] * (n + 1),
        scratch_shapes=[pltpu.SemaphoreType.DMA((4 * n,)), pltpu.SemaphoreType.DMA((4 * n,)),
                        pltpu.SemaphoreType.DMA((GATHER_SEMS,)), pltpu.SemaphoreType.DMA((GATHER_SEMS,)),
                        pltpu.SemaphoreType.DMA((1,))],
    )(*g8s, small)
    return outs[:n], outs[n]


def _exchange_chips(p4s):
    n = len(p4s)

    def body(*refs):
        p_refs, out_refs, send_sems, recv_sems = refs[:n], refs[n:2 * n], refs[2 * n], refs[2 * n + 1]
        x, y, c = _mesh_pos()
        chips = [(1 - x, y), (x, 1 - y), (1 - x, 1 - y)]
        cps = [pltpu.make_async_remote_copy(
            src_ref=p_refs[a].at[2 * cx + cy], dst_ref=out_refs[a].at[k], send_sem=send_sems.at[3 * a + k],
            recv_sem=recv_sems.at[3 * a + k], device_id=(cx, cy, c), device_id_type=MESH)
            for k, (cx, cy) in enumerate(chips) for a in range(n)]
        for cp in cps:
            cp.start()
        for cp in cps:
            cp.wait()

    return pl.pallas_call(
        body, name="rs_chips", out_shape=[jax.ShapeDtypeStruct((3, *p.shape[1:]), p.dtype) for p in p4s],
        in_specs=[ANY] * n, out_specs=[ANY] * n,
        scratch_shapes=[pltpu.SemaphoreType.DMA((3 * n,)), pltpu.SemaphoreType.DMA((3 * n,))],
    )(*p4s)


def _rs_add_sibling(g8s, recvs, c_idx):
    n = len(g8s)

    def body(s_ref, *refs):
        g_refs, r_refs, p32_refs, pb_refs = (refs[k * n:(k + 1) * n] for k in range(4))
        for g_ref, r_ref, p32_ref, pb_ref in zip(g_refs, r_refs, p32_refs, pb_refs):
            p = g_ref[0] + r_ref[0].astype(F32)
            p32_ref[0] = p
            pb_ref[0] = p.astype(BF16)

    blk = lambda g: pl.BlockSpec((1, *g.shape[1:]), lambda q, s: (q, 0, 0))
    grid_spec = pltpu.PrefetchScalarGridSpec(
        num_scalar_prefetch=1, grid=(4,),
        in_specs=[pl.BlockSpec((1, *g.shape[1:]), lambda q, s: (2 * q + s[0], 0, 0)) for g in g8s]
        + [blk(g) for g in g8s],
        out_specs=[blk(g) for g in g8s] * 2)
    outs = pl.pallas_call(
        body, name="rs_add_sibling", grid_spec=grid_spec,
        out_shape=[jax.ShapeDtypeStruct((4, *g.shape[1:]), F32) for g in g8s]
        + [jax.ShapeDtypeStruct((4, *g.shape[1:]), BF16) for g in g8s],
        compiler_params=_cp(48, ("arbitrary",)),
    )(c_idx, *g8s, *recvs)
    return outs[:n], outs[n:]


def _rs_add_chips(p32s, recvs, q_idx):
    def body(s_ref, pin_ref, pout_ref, ppw_ref, rin_ref, rout_ref, rpw_ref, gin_ref, gout_ref, gpw_ref):
        def total(p_ref, r_ref):
            return ((p_ref[0] + r_ref[0].astype(F32)) + r_ref[1].astype(F32)) + r_ref[2].astype(F32)

        gin_ref[...] = total(pin_ref, rin_ref)[:SHARD_IN, :]
        gout_ref[0] = total(pout_ref, rout_ref)
        gpw_ref[0] = total(ppw_ref, rpw_ref)

    own = lambda p: pl.BlockSpec((1, *p.shape[1:]), lambda i, s: (s[0], 0, 0))
    whole = lambda shape: pl.BlockSpec(shape, lambda i, s: (0,) * len(shape))
    out_shapes = [(SHARD_IN, D_MODEL), (1, *p32s[1].shape[1:]), (1, *p32s[2].shape[1:])]
    grid_spec = pltpu.PrefetchScalarGridSpec(
        num_scalar_prefetch=1, grid=(1,),
        in_specs=[own(p) for p in p32s] + [whole(r.shape) for r in recvs],
        out_specs=[whole(s) for s in out_shapes])
    return pl.pallas_call(
        body, name="rs_add_chips", grid_spec=grid_spec,
        out_shape=[jax.ShapeDtypeStruct(s, F32) for s in out_shapes],
        compiler_params=_cp(48, ("arbitrary",)),
    )(q_idx, *p32s, *recvs)


SMALL_ROWS_G = 64
SMALL_LAYOUT = {
    "metapad": (0, N_META, D_MODEL), "conv_w": (16, 32, D_CONV), "ln_in_g": (48, 1, D_MODEL),
    "ln_in_b": (49, 1, D_MODEL), "b_f": (50, 1, LANES), "conv_b": (51, 1, D_CONV), "ln_conv_g": (52, 1, D_CONV),
    "ln_conv_b": (53, 1, D_CONV), "ln_out_g": (54, 1, D_MODEL), "ln_out_b": (55, 1, D_MODEL), "loss": (56, 1, LANES)}


def _pack_small(pieces):
    names = list(SMALL_LAYOUT)

    def body(*refs):
        out_ref = refs[-1]
        out_ref[...] = jnp.zeros_like(out_ref)
        for name, ref in zip(names, refs[:-1]):
            r0, nr, nl = SMALL_LAYOUT[name]
            src = ref[PAD:, :] if name == "metapad" else ref[...]
            out_ref[r0:r0 + nr, 0:nl] = src

    return pl.pallas_call(body, name="pack_small", out_shape=jax.ShapeDtypeStruct((SMALL_ROWS_G, D_MODEL), F32),
                          compiler_params=_cp(16))(*[pieces[n] for n in names])


def _sum_small(gathered):
    names = list(SMALL_LAYOUT)

    def body(a_ref, *out_refs):
        acc = a_ref[0]
        for d in range(1, N_DEV):
            acc = acc + a_ref[d]
        for name, ref in zip(names, out_refs):
            r0, nr, nl = SMALL_LAYOUT[name]
            ref[...] = acc[r0:r0 + nr, 0:nl]

    outs = pl.pallas_call(
        body, name="sum_small",
        out_shape=[jax.ShapeDtypeStruct(SMALL_LAYOUT[n][1:], F32) for n in names], compiler_params=_cp(16))(gathered)
    return dict(zip(names, outs))


def _adamw(ws, gs, ms, vs):
    n = len(ws)
    c1 = 1.0 - ADAM_B1 ** ADAM_STEP
    c2 = 1.0 - ADAM_B2 ** ADAM_STEP

    def body(*refs):
        w_refs, g_refs, m_refs, v_refs = (refs[k * n:(k + 1) * n] for k in range(4))
        d_refs, nm_refs, nv_refs = (refs[(4 + k) * n:(5 + k) * n] for k in range(3))
        for w_ref, g_ref, m_ref, v_ref, d_ref, nm_ref, nv_ref in zip(w_refs, g_refs, m_refs, v_refs, d_refs,
                                                                     nm_refs, nv_refs):
            g = g_ref[...]
            m = ADAM_B1 * m_ref[...] + (1.0 - ADAM_B1) * g
            v = ADAM_B2 * v_ref[...] + (1.0 - ADAM_B2) * (g * g)
            nm_ref[...] = m
            nv_ref[...] = v
            d_ref[...] = -ADAM_LR * ((m / c1) / (jnp.sqrt(v / c2) + ADAM_EPS) + ADAM_WD * w_ref[...])

    shapes = [jax.ShapeDtypeStruct(w.shape, F32) for w in ws]
    outs = pl.pallas_call(body, name="adamw", out_shape=shapes * 3, compiler_params=_cp(48))(*ws, *gs, *ms, *vs)
    return outs[:n], outs[n:2 * n], outs[2 * n:]


W_NAMES = ("meta", "ln_in_g", "ln_in_b", "w_in", "b_f", "conv_w", "conv_b", "ln_conv_g", "ln_conv_b", "w_pw",
           "w_out", "ln_out_g", "ln_out_b")


def kernel(x, meta, ln_in_g, ln_in_b, w_in, b_f, conv_w, conv_b, ln_conv_g, ln_conv_b, w_pw, w_out, ln_out_g, ln_out_b, loss_target, m_meta, m_ln_in_g, m_ln_in_b, m_w_in, m_b_f, m_conv_w, m_conv_b, m_ln_conv_g, m_ln_conv_b, m_w_pw, m_w_out, m_ln_out_g, m_ln_out_b, v_meta, v_ln_in_g, v_ln_in_b, v_w_in, v_b_f, v_conv_w, v_conv_b, v_ln_conv_g, v_ln_conv_b, v_w_pw, v_w_out, v_ln_out_g, v_ln_out_b):
    mx, my, mc = _mesh_pos()
    me = 4 * mx + 2 * my + mc
    n_meta_sh = D_MODEL // N_DEV
    n_cw_sh = D_CONV // N_DEV
    n_out_sh = D_MODEL // N_DEV
    n_pw_sh = D_CONV // N_DEV

    small_w = jnp.concatenate([meta, jnp.pad(conv_w[0], ((0, 1), (0, LANES - n_cw_sh)))], axis=0)
    all_in, all_out, all_pw, all_small = _all_gather(
        [jnp.pad(w_in[0].T, ((0, 512 - SHARD_IN), (0, 0))).astype(BF16), w_out[0].astype(BF16), w_pw[0].astype(BF16),
         small_w], "gather_weights")
    w_r, w_t, metapad, cw = _repack_weights(all_in, all_small)
    w_out_full = all_out.reshape(D_MODEL, D_MODEL)
    w_pw_full = all_pw.reshape(D_CONV, D_CONV)

    grad_x, pc = _local_step(x[0], loss_target[0], metapad, cw, w_r, w_t, w_pw_full, w_out_full, ln_in_g, ln_in_b,
                             b_f[0], conv_b[0], ln_conv_g[0], ln_conv_b[0], ln_out_g[0], ln_out_b[0])

    g_in8, g_in8_b = _unpack_dw_in(pc["w_in_rm"], pc["w_in_t"])
    g8s = [g_in8, pc["w_out"].reshape(N_DEV, n_out_sh, D_MODEL), pc["w_pw"].reshape(N_DEV, n_pw_sh, D_CONV)]
    from_sibling, all_small_g = _exchange_sibling([g_in8_b] + g8s[1:], _pack_small(pc))
    p32s, pbs = _rs_add_sibling(g8s, from_sibling, jnp.reshape(mc, (1,)).astype(jnp.int32))
    from_chips = _exchange_chips(pbs)
    g_w_in, g_w_out, g_w_pw = _rs_add_chips(p32s, from_chips, jnp.reshape(2 * mx + my, (1,)).astype(jnp.int32))

    sm = _sum_small(all_small_g)
    grads = {
        "meta": lax.dynamic_slice_in_dim(sm["metapad"], me * n_meta_sh, n_meta_sh, axis=1),
        "ln_in_g": sm["ln_in_g"].reshape(D_MODEL), "ln_in_b": sm["ln_in_b"].reshape(D_MODEL), "w_in": g_w_in.T[None],
        "b_f": sm["b_f"][:, :N_HEADS],
        "conv_w": lax.dynamic_slice_in_dim(sm["conv_w"], me * n_cw_sh, n_cw_sh, axis=1)[None, :CONV_WIDTH],
        "conv_b": sm["conv_b"], "ln_conv_g": sm["ln_conv_g"], "ln_conv_b": sm["ln_conv_b"], "w_pw": g_w_pw,
        "w_out": g_w_out, "ln_out_g": sm["ln_out_g"], "ln_out_b": sm["ln_out_b"]}
    loss_all = sm["loss"][0, 0]

    weights = dict(meta=meta, ln_in_g=ln_in_g, ln_in_b=ln_in_b, w_in=w_in, b_f=b_f, conv_w=conv_w, conv_b=conv_b,
                   ln_conv_g=ln_conv_g, ln_conv_b=ln_conv_b, w_pw=w_pw, w_out=w_out, ln_out_g=ln_out_g,
                   ln_out_b=ln_out_b)
    moms = dict(meta=m_meta, ln_in_g=m_ln_in_g, ln_in_b=m_ln_in_b, w_in=m_w_in, b_f=m_b_f, conv_w=m_conv_w,
                conv_b=m_conv_b, ln_conv_g=m_ln_conv_g, ln_conv_b=m_ln_conv_b, w_pw=m_w_pw, w_out=m_w_out,
                ln_out_g=m_ln_out_g, ln_out_b=m_ln_out_b)
    vels = dict(meta=v_meta, ln_in_g=v_ln_in_g, ln_in_b=v_ln_in_b, w_in=v_w_in, b_f=v_b_f, conv_w=v_conv_w,
                conv_b=v_conv_b, ln_conv_g=v_ln_conv_g, ln_conv_b=v_ln_conv_b, w_pw=v_w_pw, w_out=v_w_out,
                ln_out_g=v_ln_out_g, ln_out_b=v_ln_out_b)

    def to_kernel(name, a):
        if name == "w_in":
            return a[0].T
        return a.reshape(1, -1) if a.ndim == 1 else a

    def from_kernel(name, a):
        return a.T[None] if name == "w_in" else a.reshape(weights[name].shape)

    upd = _adamw(*[[to_kernel(n, d[n]) for n in W_NAMES] for d in (weights, grads, moms, vels)])
    deltas, new_m, new_v = ([from_kernel(n, a) for n, a in zip(W_NAMES, part)] for part in upd)
    return (loss_all, grad_x[None], *[grads[n] for n in W_NAMES], *deltas, *new_m, *new_v)
```

```python
import jax
import jax.numpy as jnp
import numpy as np
from jax import lax
from jax.experimental import pallas as pl
from jax.experimental.pallas import tpu as pltpu

F32 = jnp.float32
BF16 = jnp.bfloat16

D_MODEL = 1024
D_ATTN = 512
D_CONV = 512
N_HEADS = 8
HEAD_DIM = 64
N_META = 16
CONV_WIDTH = 31
LN_EPS = 1e-5
ALPHA = 2.0 ** 0.25
SCALE = HEAD_DIM ** -0.5
LOG2E = 1.4426950408889634
ADAM_LR, ADAM_B1, ADAM_B2, ADAM_EPS, ADAM_WD, ADAM_STEP = 0.001, 0.9, 0.999, 1e-08, 0.01, 10

N_DEV = 8
D_IN = 3592
SHARD_IN = D_IN // N_DEV
BF16_ROWS = 16
ROWS_IN = -(-SHARD_IN // BF16_ROWS) * BF16_ROWS
TILE = 256
PAD = TILE - N_META
HALO = 32
SHIFT_ROWS = TILE + HALO
EXT_ROWS = SHIFT_ROWS + 8
NEG = -1e30
LANES = 128
W_COLS = 7 * 512 + LANES
OFF_GA_R, OFF_F_R = 1536, 3584
MIB = 1024 * 1024


def _cp(vmem_mib, sem=None):
    kw = dict(vmem_limit_bytes=vmem_mib * MIB)
    if sem is not None:
        kw["dimension_semantics"] = sem
    return pltpu.CompilerParams(**kw)


def _sigmoid(x):
    return 1.0 / (1.0 + jnp.exp(-x))


def _silu_and_grad(x):
    s = _sigmoid(x)
    return x * s, s * (1.0 + x * (1.0 - s))


def _ln_stats(x):
    mu = jnp.mean(x, axis=-1, keepdims=True)
    xc = x - mu
    var = jnp.mean(xc * xc, axis=-1, keepdims=True)
    rstd = lax.rsqrt(var + LN_EPS)
    return xc * rstd, rstd


def _ln_bwd(dy, xhat, rstd, g):
    dxh = dy * g
    m1 = jnp.mean(dxh, axis=-1, keepdims=True)
    m2 = jnp.mean(dxh * xhat, axis=-1, keepdims=True)
    return rstd * (dxh - m1 - xhat * m2)


def _row_spec(cols, shift=False):
    if shift:
        return pl.BlockSpec((TILE, cols), lambda i: (jnp.maximum(i - 1, 0), 0))
    return pl.BlockSpec((TILE, cols), lambda i: (i, 0))


def _full_spec(shape):
    nd = len(shape)
    return pl.BlockSpec(shape, lambda i: (0,) * nd)


def _t3_spec(ch):
    return pl.BlockSpec((1, ch, TILE), lambda i: (i, 0, 0))


def _proj_fwd(x, metapad, g_in, b_in, w_r, nt):
    lp = nt * TILE

    def body(x_ref, mp_ref, g_ref, b_ref, w_ref, h_ref, hb_ref, qT_ref, kT_ref, vT_ref, k_ref, v_ref,
             ga_ref, u_ref, ug_ref, gc_ref, fl_ref):
        i = pl.program_id(0)
        x0 = jnp.where(i == 0, mp_ref[...], x_ref[...])
        xhat, _ = _ln_stats(x0)
        h = xhat * g_ref[...] + b_ref[...]
        h_ref[...] = h
        hb = h.astype(BF16)
        hb_ref[...] = hb

        def sec(off, n=512):
            return jnp.dot(hb, w_ref[:, off:off + n], preferred_element_type=F32)

        qT_ref[0] = (sec(0) * (SCALE * LOG2E)).T.astype(BF16)
        k = sec(512)
        kT_ref[0] = k.T.astype(BF16)
        k_ref[...] = k.astype(BF16)
        v = sec(1024)
        vT_ref[0] = v.T.astype(BF16)
        v_ref[...] = v.astype(BF16)
        ga_ref[...] = sec(OFF_GA_R).astype(BF16)
        u_ref[...] = sec(OFF_GA_R + 512).astype(BF16)
        ug_ref[...] = sec(OFF_GA_R + 1024).astype(BF16)
        gc_ref[...] = sec(OFF_GA_R + 1536).astype(BF16)
        fl_ref[...] = sec(OFF_F_R, LANES)

    t3 = jax.ShapeDtypeStruct((nt, 512, TILE), BF16)
    rm = lambda dt: jax.ShapeDtypeStruct((lp, 512), dt)
    return pl.pallas_call(
        body, name="proj_fwd", grid=(nt,),
        in_specs=[_row_spec(D_MODEL, shift=True), _full_spec((TILE, D_MODEL)), _full_spec((1, D_MODEL)),
                  _full_spec((1, D_MODEL)), _full_spec((D_MODEL, W_COLS))],
        out_specs=[_row_spec(D_MODEL), _row_spec(D_MODEL), _t3_spec(512), _t3_spec(512), _t3_spec(512),
                   _row_spec(512), _row_spec(512),
                   _row_spec(512), _row_spec(512), _row_spec(512), _row_spec(512), _row_spec(LANES)],
        out_shape=[jax.ShapeDtypeStruct((lp, D_MODEL), F32), jax.ShapeDtypeStruct((lp, D_MODEL), BF16),
                   t3, t3, t3, rm(BF16), rm(BF16),
                   rm(BF16), rm(BF16), rm(BF16), rm(BF16), jax.ShapeDtypeStruct((lp, LANES), F32)],
        compiler_params=_cp(56, ("arbitrary",)),
    )(x, metapad, g_in, b_in, w_r)


def _row_mask(i, shape):
    r = lax.broadcasted_iota(jnp.int32, shape, 0)
    return (r >= PAD) | (i > 0)


def _cumsum_fwd(fl, bf_pad, nt):
    lp = nt * TILE

    def body(fl_ref, bf_ref, kx_ref, carry):
        i = pl.program_id(0)

        @pl.when(i == 0)
        def _():
            carry[...] = jnp.zeros_like(carry)

        z = fl_ref[...] + bf_ref[...]
        lf = jnp.minimum(z, 0.0) - jnp.log(1.0 + jnp.exp(-jnp.abs(z)))
        lane = lax.broadcasted_iota(jnp.int32, (TILE, LANES), 1)
        real = _row_mask(i, (TILE, LANES))
        lf = jnp.where(real & (lane < N_HEADS), lf, 0.0)
        r = lax.broadcasted_iota(jnp.int32, (TILE, TILE), 0)
        c = lax.broadcasted_iota(jnp.int32, (TILE, TILE), 1)
        tril = (c <= r).astype(F32)
        cs = jnp.dot(tril, lf, precision=lax.Precision.HIGHEST, preferred_element_type=F32) + carry[...]
        carry[...] = cs[TILE - 1:TILE, :]
        bias = jnp.where(real, cs * (-LOG2E), NEG)
        hi = bias.astype(BF16).astype(F32)
        mid = (bias - hi).astype(BF16).astype(F32)
        lo = (bias - hi - mid).astype(BF16).astype(F32)
        for p in range(N_HEADS // 2):
            out = jnp.zeros((TILE, LANES), F32)
            for hh in range(2):
                for part, piece in enumerate((hi, mid, lo)):
                    dst, src = 3 * hh + part, 2 * p + hh
                    moved = piece if dst == src else pltpu.roll(piece, (dst - src) % LANES, 1)
                    out = jnp.where(lane == dst, moved, out)
            kx_ref[p] = out.astype(BF16)

    return pl.pallas_call(
        body, name="cumsum_fwd", grid=(nt,),
        in_specs=[_row_spec(LANES), _full_spec((1, LANES))],
        out_specs=pl.BlockSpec((N_HEADS // 2, TILE, LANES), lambda i: (0, i, 0)),
        out_shape=jax.ShapeDtypeStruct((N_HEADS // 2, lp, LANES), BF16),
        scratch_shapes=[pltpu.VMEM((1, LANES), F32)],
        compiler_params=_cp(32, ("arbitrary",)),
    )(fl, bf_pad)


def _head_rows(blk, hh):
    r = lax.broadcasted_iota(jnp.int32, blk.shape, 0)
    return jnp.where((r >= hh * HEAD_DIM) & (r < (hh + 1) * HEAD_DIM), blk, jnp.zeros_like(blk))


def _two_heads(blk):
    return jnp.concatenate([_head_rows(blk, 0), _head_rows(blk, 1)], axis=1)


def _bias_rows():
    r = lax.broadcasted_iota(jnp.int32, (LANES, 2 * TILE), 0)
    c = lax.broadcasted_iota(jnp.int32, (LANES, 2 * TILE), 1)
    return jnp.where(((r < 3) & (c < TILE)) | ((r >= 3) & (r < 6) & (c >= TILE)), 1.0, 0.0).astype(BF16)


def _diag_mask(s):
    kpos = lax.broadcasted_iota(jnp.int32, (TILE, TILE), 0)
    qpos = lax.broadcasted_iota(jnp.int32, (TILE, TILE), 1)
    return jnp.where(kpos <= qpos, s, NEG)


def _stream(n, first, nxt, scores, update, unroll=8):
    if n == 0:
        return
    scores(first, 0)

    def step(_, idx):
        for _u in range(unroll):
            idx_b = nxt(idx)
            scores(idx_b, 1)
            update(idx, 0)
            idx = nxt(idx_b)
            scores(idx, 0)
            update(idx_b, 1)
        return idx

    steps = (n - 1) // (2 * unroll)
    idx = lax.fori_loop(0, steps, step, first)
    left = n - 2 * unroll * steps
    for r in range(left - 1):
        idx_b = nxt(idx)
        scores(idx_b, (r + 1) % 2)
        update(idx, r % 2)
        idx = idx_b
    update(idx, (left - 1) % 2)


def _next_below_diagonal(idx):
    i, j = idx
    wrap = j + 1 >= i
    return jnp.where(wrap, i + 1, i), jnp.where(wrap, 0, j + 1)


def _tile_rows(t):
    return pl.ds(pl.multiple_of(t * TILE, TILE), TILE)


def _two_streams(nt):
    load, group = [0, 0], {}
    for i in sorted(range(1, nt), reverse=True):
        g = 0 if load[0] <= load[1] else 1
        group[i] = g
        load[g] += i
    rows = [[(i, i, j) for i in range(1, nt) if group[i] == g for j in range(i)] for g in range(2)]
    length = max(len(r) for r in rows)
    rows = [r + [(nt, 0, 0)] * (length - len(r)) for r in rows]
    return group, np.asarray(rows, np.int32).reshape(2, -1), length


def _attn_fwd(qT3, k, kx3, vT3, nt):
    lp = nt * TILE
    npair = N_HEADS // 2
    group, table, n_stream = _two_streams(nt)

    def body(tab_ref, qT_ref, k_ref, kx_ref, vT_ref, oT_ref, o_ref, lse_ref, sbuf, m_0, l_0, acc_0, m_1, l_1, acc_1):
        ones = _bias_rows()
        states = ((m_0, l_0, acc_0), (m_1, l_1, acc_1))

        def scores(i, j, slot):
            qcat = jnp.concatenate([_two_heads(qT_ref[i]), ones], axis=0)
            kext = jnp.concatenate([k_ref[_tile_rows(j), :], kx_ref[0, _tile_rows(j), :]], axis=1)
            sbuf[slot] = jnp.dot(kext, qcat, preferred_element_type=F32)

        def update(st, j, slot, state, diag):
            m_s, l_s, acc_s = state
            for hh in range(2):
                s = sbuf[slot, :, hh * TILE:(hh + 1) * TILE]
                vj = vT_ref[j, hh * HEAD_DIM:(hh + 1) * HEAD_DIM, :]
                if diag:
                    s = _diag_mask(s)
                    m_new = jnp.max(s, axis=0, keepdims=True)
                    p = jnp.exp2(s - m_new)
                    l_s[st, hh] = jnp.sum(p, axis=0, keepdims=True)
                    acc_s[st, hh] = jnp.dot(vj, p.astype(BF16), preferred_element_type=F32)
                else:
                    m_prev = m_s[st, hh]
                    m_new = jnp.maximum(m_prev, jnp.max(s, axis=0, keepdims=True))
                    a = jnp.exp2(m_prev - m_new)
                    p = jnp.exp2(s - m_new)
                    l_s[st, hh] = a * l_s[st, hh] + jnp.sum(p, axis=0, keepdims=True)
                    acc_s[st, hh] = a * acc_s[st, hh] + jnp.dot(vj, p.astype(BF16), preferred_element_type=F32)
                m_s[st, hh] = m_new

        _stream(nt, jnp.int32(0), lambda t: t + 1, lambda t, slot: scores(t, t, slot),
                lambda t, slot: update(t, t, slot, states[0], True))
        for dst, src in zip(states[1], states[0]):
            dst[0:nt] = src[0:nt]
        for m_s, l_s, acc_s in states:
            m_s[nt] = jnp.full(m_s.shape[1:], NEG, F32)
            l_s[nt] = jnp.zeros(l_s.shape[1:], F32)
            acc_s[nt] = jnp.zeros(acc_s.shape[1:], F32)

        def entry(g, t):
            return tab_ref[g, 3 * t], tab_ref[g, 3 * t + 1], tab_ref[g, 3 * t + 2]

        def scores2(t, slot):
            for g in range(2):
                _, qi, kj = entry(g, t)
                scores(qi, kj, 2 * g + slot)

        def update2(t, slot):
            for g in range(2):
                st, _, kj = entry(g, t)
                update(st, kj, 2 * g + slot, states[g], False)

        _stream(n_stream, jnp.int32(0), lambda t: t + 1, scores2, update2)

        for i in range(nt):
            m_s, l_s, acc_s = states[group.get(i, 0)]
            for hh in range(2):
                l = l_s[i, hh]
                oT_ref[i, hh * HEAD_DIM:(hh + 1) * HEAD_DIM, :] = acc_s[i, hh] / l
                lse_ref[0, i, hh:hh + 1, :] = m_s[i, hh] + jnp.log(l) * LOG2E
            o_ref[i * TILE:(i + 1) * TILE, :] = oT_ref[i].T.astype(BF16)

    blk_t = pl.BlockSpec((nt, LANES, TILE), lambda p, tab: (0, p, 0))
    blk_rm = pl.BlockSpec((lp, LANES), lambda p, tab: (0, p))
    blk_px = pl.BlockSpec((1, lp, LANES), lambda p, tab: (p, 0, 0))
    blk_st = pl.BlockSpec((1, nt, 8, TILE), lambda p, tab: (p, 0, 0, 0))
    state = [pltpu.VMEM((nt + 1, 2, 1, TILE), F32), pltpu.VMEM((nt + 1, 2, 1, TILE), F32),
             pltpu.VMEM((nt + 1, 2, HEAD_DIM, TILE), F32)]
    grid_spec = pltpu.PrefetchScalarGridSpec(
        num_scalar_prefetch=1, grid=(npair,), in_specs=[blk_t, blk_rm, blk_px, blk_t],
        out_specs=[blk_t, blk_rm, blk_st], scratch_shapes=[pltpu.VMEM((4, TILE, 2 * TILE), F32)] + state + state)
    return pl.pallas_call(
        body, name="attn_fwd", grid_spec=grid_spec,
        out_shape=[jax.ShapeDtypeStruct((nt, D_ATTN, TILE), F32),
                   jax.ShapeDtypeStruct((lp, D_ATTN), BF16),
                   jax.ShapeDtypeStruct((npair, nt, 8, TILE), F32)],
        compiler_params=_cp(60, ("arbitrary",)),
    )(jnp.asarray(table), qT3, k, kx3, vT3)


def _attn_bwd(qT3, kT3, k, kx3, v, oT3, doT3, lse4, nt):
    lp = nt * TILE
    npair = N_HEADS // 2

    def body(qT_ref, kT_ref, k_ref, kx_ref, v_ref, oT_ref, doT_ref, lse_ref,
             dqT_ref, dkT_ref, dvT_ref, dck_ref, dcq_ref, sbuf, dpbuf, dq_s, dk_s, dv_s, dc_s, tp_s, tds_s):
        ones = _bias_rows()

        def scores(idx, slot):
            i, j = idx
            qcat = jnp.concatenate([_two_heads(qT_ref[i]), ones], axis=0)
            kext = jnp.concatenate([k_ref[_tile_rows(j), :], kx_ref[0, _tile_rows(j), :]], axis=1)
            sbuf[slot] = jnp.dot(kext, qcat, preferred_element_type=F32)
            dpbuf[slot] = jnp.dot(v_ref[_tile_rows(j), :], _two_heads(doT_ref[i]), preferred_element_type=F32)

        def update(idx, slot, diag):
            i, j = idx
            for hh in range(2):
                hs = slice(hh * HEAD_DIM, (hh + 1) * HEAD_DIM)
                s = sbuf[slot, :, hh * TILE:(hh + 1) * TILE]
                if diag:
                    s = _diag_mask(s)
                p = jnp.exp2(s - lse_ref[0, i, hh:hh + 1, :])
                doh = doT_ref[i, hs, :]
                delta = jnp.sum(doh.astype(F32) * oT_ref[i, hs, :], axis=0, keepdims=True)
                ds = p * (dpbuf[slot, :, hh * TILE:(hh + 1) * TILE] - delta)
                dsb = ds.astype(BF16)
                tp_s[hh] = p.astype(BF16).T
                tds_s[hh] = dsb.T
                dv = jnp.dot(doh, tp_s[hh], preferred_element_type=F32)
                dk = jnp.dot(qT_ref[i, hs, :], tds_s[hh], preferred_element_type=F32)
                dq = jnp.dot(kT_ref[j, hs, :], dsb, preferred_element_type=F32)
                dc = ds[:, :LANES] + ds[:, LANES:]
                dcq = jnp.sum(ds, axis=0, keepdims=True)
                if diag:
                    dv_s[j, hh] = dv
                    dk_s[j, hh] = dk
                    dc_s[j, hh] = dc
                    dq_s[i, hs, :] = dq
                    dcq_ref[0, i, hh:hh + 1, :] = dcq
                else:
                    dv_s[j, hh] += dv
                    dk_s[j, hh] += dk
                    dc_s[j, hh] += dc
                    dq_s[i, hs, :] += dq
                    dcq_ref[0, i, hh:hh + 1, :] += dcq

        dcq_ref[...] = jnp.zeros_like(dcq_ref)
        zero = jnp.int32(0)
        _stream(nt, (zero, zero), lambda idx: (idx[0] + 1, idx[1] + 1), scores,
                lambda idx, slot: update(idx, slot, True))
        _stream(nt * (nt - 1) // 2, (zero + 1, zero), _next_below_diagonal, scores,
                lambda idx, slot: update(idx, slot, False))

        lane = lax.broadcasted_iota(jnp.int32, (TILE, LANES), 1)

        def finish(t, carry):
            dck = jnp.zeros((TILE, LANES), F32)
            for hh in range(2):
                hs = slice(hh * HEAD_DIM, (hh + 1) * HEAD_DIM)
                dkT_ref[t, hs, :] = (dk_s[t, hh] * (1.0 / LOG2E)).astype(BF16)
                dvT_ref[t, hs, :] = dv_s[t, hh].astype(BF16)
                dck = jnp.where(lane == hh, -jnp.sum(dc_s[t, hh], axis=1, keepdims=True), dck)
            dck_ref[0, _tile_rows(t), :] = dck
            dqT_ref[t] = (dq_s[t] * SCALE).astype(BF16)
            return carry

        lax.fori_loop(0, nt, finish, 0)

    blk_t = pl.BlockSpec((nt, LANES, TILE), lambda p: (0, p, 0))
    blk_rm = pl.BlockSpec((lp, LANES), lambda p: (0, p))
    blk_px = pl.BlockSpec((1, lp, LANES), lambda p: (p, 0, 0))
    blk_st = pl.BlockSpec((1, nt, 8, TILE), lambda p: (p, 0, 0, 0))
    t3 = jax.ShapeDtypeStruct((nt, D_ATTN, TILE), BF16)
    return pl.pallas_call(
        body, name="attn_bwd", grid=(npair,),
        in_specs=[blk_t, blk_t, blk_rm, blk_px, blk_rm, blk_t, blk_t, blk_st],
        out_specs=[blk_t, blk_t, blk_t, blk_px, blk_st],
        out_shape=[t3, t3, t3, jax.ShapeDtypeStruct((npair, lp, LANES), F32),
                   jax.ShapeDtypeStruct((npair, nt, 8, TILE), F32)],
        scratch_shapes=[pltpu.VMEM((2, TILE, 2 * TILE), F32), pltpu.VMEM((2, TILE, 2 * TILE), F32),
                        pltpu.VMEM((nt, LANES, TILE), F32), pltpu.VMEM((nt, 2, HEAD_DIM, TILE), F32),
                        pltpu.VMEM((nt, 2, HEAD_DIM, TILE), F32), pltpu.VMEM((nt, 2, TILE, LANES), F32),
                        pltpu.VMEM((2, TILE, TILE), BF16), pltpu.VMEM((2, TILE, TILE), BF16)],
        compiler_params=_cp(60, ("arbitrary",)),
    )(qT3, kT3, k, kx3, v, oT3, doT3, lse4)


def _glu(u, ug, i):
    return jnp.where(_row_mask(i, u.shape), u.astype(F32) * _sigmoid(ug.astype(F32)), 0.0)


def _shifted_copies(dst, src):
    for ph in range(8):
        dst[ph] = src[ph:ph + SHIFT_ROWS, :]


def _tap_window(sh, off, lanes, row0=0, rows=TILE):
    base = (off // 8) * 8 + row0
    return sh[off % 8, base:base + rows, lanes]


def _conv_fwd(u, ug, conv_w, conv_b, g, b, w_pw, nt):
    lp = nt * TILE

    def body(u_ref, ug_ref, up_ref, ugp_ref, w_ref, cb_ref, g_ref, b_ref, wpw_ref,
             co_ref, hc_ref, pw_ref, ext, sh):
        i = pl.program_id(0)
        prev = _glu(up_ref[...], ugp_ref[...], i - 1)
        ext[0:HALO, :] = jnp.where(i > 0, prev[TILE - HALO:, :], 0.0)
        ext[HALO:HALO + TILE, :] = _glu(u_ref[...], ug_ref[...], i)
        ext[HALO + TILE:, :] = jnp.zeros((8, D_CONV), F32)
        _shifted_copies(sh, ext)
        for lb in range(D_CONV // LANES):
            lanes = slice(lb * LANES, (lb + 1) * LANES)
            acc = jnp.zeros((TILE, LANES), F32) + cb_ref[:, lanes]
            for t in range(CONV_WIDTH):
                off = HALO - (CONV_WIDTH - 1) + t
                acc = acc + w_ref[t:t + 1, lanes] * _tap_window(sh, off, lanes)
            co_ref[:, lanes] = acc
        xhat, _ = _ln_stats(co_ref[...])
        a, _ = _silu_and_grad(xhat * g_ref[...] + b_ref[...])
        hc = a.astype(BF16)
        hc_ref[...] = hc
        pw_ref[...] = jnp.dot(hc, wpw_ref[...], preferred_element_type=F32).astype(BF16)

    rm = lambda dt: jax.ShapeDtypeStruct((lp, D_CONV), dt)
    return pl.pallas_call(
        body, name="conv_fwd", grid=(nt,),
        in_specs=[_row_spec(512), _row_spec(512), _row_spec(512, shift=True), _row_spec(512, shift=True),
                  _full_spec((32, 512)), _full_spec((1, 512)), _full_spec((1, 512)), _full_spec((1, 512)),
                  _full_spec((512, 512))],
        out_specs=[_row_spec(512), _row_spec(512), _row_spec(512)],
        out_shape=[rm(F32), rm(BF16), rm(BF16)],
        scratch_shapes=[pltpu.VMEM((EXT_ROWS, D_CONV), F32), pltpu.VMEM((8, SHIFT_ROWS, D_CONV), F32)],
        compiler_params=_cp(40, ("arbitrary",)),
    )(u, ug, u, ug, conv_w, conv_b, g, b, w_pw)


def _out_fwd(o, ga, pw, gc, h, w_out, g_out, b_out, target, nt):
    lp = nt * TILE

    def body(o_ref, ga_ref, pw_ref, gc_ref, h_ref, wo_ref, go_ref, bo_ref, t_ref,
             y_ref, dz_ref, loss_ref, dgo_ref, dbo_ref):
        i = pl.program_id(0)

        @pl.when(i == 0)
        def _():
            loss_ref[...] = jnp.zeros_like(loss_ref)
            dgo_ref[...] = jnp.zeros_like(dgo_ref)
            dbo_ref[...] = jnp.zeros_like(dbo_ref)

        ya, _ = _silu_and_grad(ga_ref[...].astype(F32))
        yc, _ = _silu_and_grad(gc_ref[...].astype(F32))
        ya = (o_ref[...].astype(F32) * ya).astype(BF16)
        yc = (pw_ref[...].astype(F32) * yc).astype(BF16)
        y_ref[:, :D_ATTN] = ya
        y_ref[:, D_ATTN:] = yc
        z = ALPHA * h_ref[...] + jnp.dot(ya, wo_ref[:D_ATTN, :], preferred_element_type=F32) \
            + jnp.dot(yc, wo_ref[D_ATTN:, :], preferred_element_type=F32)
        zhat, rstd = _ln_stats(z)
        out = zhat * go_ref[...] + bo_ref[...]
        live = (i > 0).astype(F32)
        err = (out - t_ref[...]) * live
        dout = err * (1.0 / D_MODEL)
        loss_ref[...] += 0.5 * jnp.sum(jnp.sum(err * dout, axis=0, keepdims=True), axis=1, keepdims=True)
        dgo_ref[...] += jnp.sum(dout * zhat, axis=0, keepdims=True)
        dbo_ref[...] += jnp.sum(dout, axis=0, keepdims=True)
        dz_ref[...] = _ln_bwd(dout, zhat, rstd, go_ref[...])

    return pl.pallas_call(
        body, name="out_fwd", grid=(nt,),
        in_specs=[_row_spec(512), _row_spec(512), _row_spec(512), _row_spec(512),
                  _row_spec(D_MODEL), _full_spec((D_MODEL, D_MODEL)), _full_spec((1, D_MODEL)),
                  _full_spec((1, D_MODEL)), _row_spec(D_MODEL, shift=True)],
        out_specs=[_row_spec(D_MODEL), _row_spec(D_MODEL), _full_spec((1, LANES)), _full_spec((1, D_MODEL)),
                   _full_spec((1, D_MODEL))],
        out_shape=[jax.ShapeDtypeStruct((lp, D_MODEL), BF16), jax.ShapeDtypeStruct((lp, D_MODEL), F32),
                   jax.ShapeDtypeStruct((1, LANES), F32), jax.ShapeDtypeStruct((1, D_MODEL), F32),
                   jax.ShapeDtypeStruct((1, D_MODEL), F32)],
        compiler_params=_cp(40, ("arbitrary",)),
    )(o, ga, pw, gc, h, w_out, g_out, b_out, target)


def _out_bwd(dz, y, o, ga, pw, gc, w_out, hc, co, w_pw, g_cv, b_cv, nt):
    lp = nt * TILE

    def body(dz_ref, y_ref, o_ref, ga_ref, pw_ref, gc_ref, wo_ref, hc_ref, co_ref, wpw_ref, g_ref, b_ref,
             doT_ref, dga_ref, dgc_ref, dwo_ref, dco_ref, dwpw_ref, dg_ref, db_ref, dcb_ref):
        i = pl.program_id(0)

        @pl.when(i == 0)
        def _():
            dwo_ref[...] = jnp.zeros_like(dwo_ref)
            dwpw_ref[...] = jnp.zeros_like(dwpw_ref)
            dg_ref[...] = jnp.zeros_like(dg_ref)
            db_ref[...] = jnp.zeros_like(db_ref)
            dcb_ref[...] = jnp.zeros_like(dcb_ref)

        dzb = dz_ref[...].astype(BF16)
        nt_dims = (((1,), (1,)), ((), ()))
        tn_dims = (((0,), (0,)), ((), ()))
        dya = lax.dot_general(dzb, wo_ref[:D_ATTN, :], nt_dims, preferred_element_type=F32)
        dyc = lax.dot_general(dzb, wo_ref[D_ATTN:, :], nt_dims, preferred_element_type=F32)
        sa, sga = _silu_and_grad(ga_ref[...].astype(F32))
        sc, sgc = _silu_and_grad(gc_ref[...].astype(F32))
        doT_ref[0] = (dya * sa).T.astype(BF16)
        dga_ref[...] = (dya * o_ref[...].astype(F32) * sga).astype(BF16)
        dpw_b = (dyc * sc).astype(BF16)
        dgc_ref[...] = (dyc * pw_ref[...].astype(F32) * sgc).astype(BF16)
        dwo_ref[...] += lax.dot_general(y_ref[...], dzb, tn_dims, preferred_element_type=F32)

        dhc = lax.dot_general(dpw_b, wpw_ref[...], nt_dims, preferred_element_type=F32)
        xhat, rstd = _ln_stats(co_ref[...])
        _, sg = _silu_and_grad(xhat * g_ref[...] + b_ref[...])
        dln = dhc * sg
        dg_ref[...] += jnp.sum(dln * xhat, axis=0, keepdims=True)
        db_ref[...] += jnp.sum(dln, axis=0, keepdims=True)
        dco = _ln_bwd(dln, xhat, rstd, g_ref[...])
        dco_ref[...] = dco
        dcb_ref[...] += jnp.sum(dco, axis=0, keepdims=True)
        dwpw_ref[...] += lax.dot_general(hc_ref[...], dpw_b, tn_dims, preferred_element_type=F32)

    rm = jax.ShapeDtypeStruct((lp, 512), BF16)
    vec = jax.ShapeDtypeStruct((1, D_CONV), F32)
    return pl.pallas_call(
        body, name="out_bwd", grid=(nt,),
        in_specs=[_row_spec(D_MODEL), _row_spec(D_MODEL), _row_spec(512), _row_spec(512), _row_spec(512),
                  _row_spec(512), _full_spec((D_MODEL, D_MODEL)), _row_spec(512), _row_spec(512),
                  _full_spec((512, 512)), _full_spec((1, 512)), _full_spec((1, 512))],
        out_specs=[_t3_spec(512), _row_spec(512), _row_spec(512), _full_spec((D_MODEL, D_MODEL)), _row_spec(512),
                   _full_spec((512, 512)), _full_spec((1, 512)), _full_spec((1, 512)), _full_spec((1, 512))],
        out_shape=[jax.ShapeDtypeStruct((nt, 512, TILE), BF16), rm, rm, jax.ShapeDtypeStruct((D_MODEL, D_MODEL), F32),
                   jax.ShapeDtypeStruct((lp, D_CONV), F32), jax.ShapeDtypeStruct((512, 512), F32), vec, vec, vec],
        compiler_params=_cp(56, ("arbitrary",)),
    )(dz, y, o, ga, pw, gc, w_out, hc, co, w_pw, g_cv, b_cv)


def _conv_bwd_taps(dco, u, ug, conv_w, nt):
    lp = nt * TILE

    def body(dco_ref, dcon_ref, u_ref, ug_ref, up_ref, ugp_ref, w3_ref, du_ref, dug_ref, dw_ref, ext, dext, sh, dsh,
             dhg_s, dw_s):
        i = pl.program_id(0)

        @pl.when(i == 0)
        def _():
            dw_s[...] = jnp.zeros_like(dw_s)

        prev = _glu(up_ref[...], ugp_ref[...], i - 1)
        ext[0:HALO, :] = jnp.where(i > 0, prev[TILE - HALO:, :], 0.0)
        ext[HALO:HALO + TILE, :] = _glu(u_ref[...], ug_ref[...], i)
        ext[HALO + TILE:, :] = jnp.zeros((8, D_CONV), F32)
        dext[0:TILE, :] = dco_ref[...]
        dext[TILE:TILE + HALO, :] = jnp.where(i < nt - 1, dcon_ref[0:HALO, :], 0.0)
        dext[TILE + HALO:, :] = jnp.zeros((8, D_CONV), F32)
        _shifted_copies(sh, ext)
        _shifted_copies(dsh, dext)
        stripe = 32

        def stripe_body(rb, carry):
            row0 = pl.multiple_of(rb * stripe, stripe)
            dco = dco_ref[pl.ds(row0, stripe), :]
            dhg = jnp.zeros((stripe, D_CONV), F32)
            for t in range(CONV_WIDTH):
                off = HALO - (CONV_WIDTH - 1) + t
                back = CONV_WIDTH - 1 - t
                prod = dco * sh[off % 8, pl.ds((off // 8) * 8 + row0, stripe), :]
                part = prod[0:8, :]
                for r8 in range(1, stripe // 8):
                    part = part + prod[8 * r8:8 * r8 + 8, :]
                dw_s[t] += part
                dhg = dhg + w3_ref[t] * dsh[back % 8, pl.ds((back // 8) * 8 + row0, stripe), :]
            dhg_s[pl.ds(row0, stripe), :] = dhg
            return carry

        lax.fori_loop(0, TILE // stripe, stripe_body, 0)

        @pl.when(i == nt - 1)
        def _():
            dw_ref[...] = jnp.sum(dw_s[...], axis=1)

        dhg = jnp.where(_row_mask(i, (TILE, D_CONV)), dhg_s[...], 0.0)
        sg = _sigmoid(ug_ref[...].astype(F32))
        du_ref[...] = (dhg * sg).astype(BF16)
        dug_ref[...] = (dhg * u_ref[...].astype(F32) * sg * (1.0 - sg)).astype(BF16)

    rm = jax.ShapeDtypeStruct((lp, D_CONV), BF16)
    nxt = pl.BlockSpec((TILE, 512), lambda i: (jnp.minimum(i + 1, nt - 1), 0))
    ext_t = pltpu.VMEM((EXT_ROWS, D_CONV), F32)
    sh_t = pltpu.VMEM((8, SHIFT_ROWS, D_CONV), F32)
    return pl.pallas_call(
        body, name="conv_bwd_taps", grid=(nt,),
        in_specs=[_row_spec(512), nxt, _row_spec(512), _row_spec(512), _row_spec(512, shift=True),
                  _row_spec(512, shift=True), _full_spec((32, 1, 512))],
        out_specs=[_row_spec(512), _row_spec(512), _full_spec((32, 512))],
        out_shape=[rm, rm, jax.ShapeDtypeStruct((32, D_CONV), F32)],
        scratch_shapes=[ext_t, ext_t, sh_t, sh_t, pltpu.VMEM((TILE, D_CONV), F32), pltpu.VMEM((32, 8, D_CONV), F32)],
        compiler_params=_cp(48, ("arbitrary",)),
    )(dco, dco, u, ug, u, ug, conv_w.reshape(32, 1, D_CONV))


def _cumsum_bwd(dck, dcq4, fl, bf_pad, nt):
    lp = nt * TILE

    def body(dck_ref, dcq_ref, fl_ref, bf_ref, dfl_ref, dbf_ref, carry):
        i = pl.program_id(0)
        tile = nt - 1 - i

        @pl.when(i == 0)
        def _():
            carry[...] = jnp.zeros_like(carry)
            dbf_ref[...] = jnp.zeros_like(dbf_ref)

        dc = jnp.zeros((TILE, LANES), F32)
        for p in range(N_HEADS // 2):
            dq_rows = jnp.concatenate([dcq_ref[p, 0], jnp.zeros((LANES - 8, TILE), F32)], axis=0)
            both = dck_ref[p] + dq_rows.T
            dc = dc + (both if p == 0 else pltpu.roll(both, 2 * p, 1))
        r = lax.broadcasted_iota(jnp.int32, (TILE, TILE), 0)
        c = lax.broadcasted_iota(jnp.int32, (TILE, TILE), 1)
        triu = (c >= r).astype(F32)
        dlf = jnp.dot(triu, dc, precision=lax.Precision.HIGHEST, preferred_element_type=F32) + carry[...]
        carry[...] = dlf[0:1, :]
        z = fl_ref[...] + bf_ref[...]
        lane = lax.broadcasted_iota(jnp.int32, (TILE, LANES), 1)
        dfl = jnp.where(_row_mask(tile, (TILE, LANES)) & (lane < N_HEADS), dlf * _sigmoid(-z), 0.0)
        dfl_ref[...] = dfl.astype(BF16)
        dbf_ref[...] += jnp.sum(dfl, axis=0, keepdims=True)

    rev = lambda i: (nt - 1 - i, 0)
    return pl.pallas_call(
        body, name="cumsum_bwd", grid=(nt,),
        in_specs=[pl.BlockSpec((N_HEADS // 2, TILE, LANES), lambda i: (0, nt - 1 - i, 0)),
                  pl.BlockSpec((N_HEADS // 2, 1, 8, TILE), lambda i: (0, nt - 1 - i, 0, 0)),
                  pl.BlockSpec((TILE, LANES), rev), _full_spec((1, LANES))],
        out_specs=[pl.BlockSpec((TILE, LANES), rev), _full_spec((1, LANES))],
        out_shape=[jax.ShapeDtypeStruct((lp, LANES), BF16), jax.ShapeDtypeStruct((1, LANES), F32)],
        scratch_shapes=[pltpu.VMEM((1, LANES), F32)],
        compiler_params=_cp(32, ("arbitrary",)),
    )(dck, dcq4, fl, bf_pad)


def _dw_rowmajor(hb, secs, nt):
    n = len(secs)
    steps = 4
    rows = nt * TILE // steps
    long_rows = lambda cols: pl.BlockSpec((rows, cols), lambda i: (i, 0))

    def body(*refs):
        hb_ref, sec_refs, out_refs = refs[0], refs[1:1 + n], refs[1 + n:]
        i = pl.program_id(0)

        @pl.when(i == 0)
        def _():
            for o_ref in out_refs:
                o_ref[...] = jnp.zeros_like(o_ref)

        hb_t = hb_ref[...]
        for s_ref, o_ref in zip(sec_refs, out_refs):
            o_ref[...] += lax.dot_general(hb_t, s_ref[...], (((0,), (0,)), ((), ())), preferred_element_type=F32)

    return pl.pallas_call(
        body, name="dw_rowmajor", grid=(steps,),
        in_specs=[long_rows(D_MODEL)] + [long_rows(s.shape[1]) for s in secs],
        out_specs=[_full_spec((D_MODEL, s.shape[1])) for s in secs],
        out_shape=[jax.ShapeDtypeStruct((D_MODEL, s.shape[1]), F32) for s in secs],
        compiler_params=_cp(48, ("arbitrary",)),
    )(hb, *secs)


def _dw_transposed(hb, secs_t3, nt):
    n = len(secs_t3)

    def body(*refs):
        hb_ref, sec_refs, out_refs = refs[0], refs[1:1 + n], refs[1 + n:]
        i = pl.program_id(0)

        @pl.when(i == 0)
        def _():
            for o_ref in out_refs:
                o_ref[...] = jnp.zeros_like(o_ref)

        hb_t = hb_ref[...]
        for s_ref, o_ref in zip(sec_refs, out_refs):
            o_ref[...] += jnp.dot(s_ref[0], hb_t, preferred_element_type=F32)

    return pl.pallas_call(
        body, name="dw_transposed", grid=(nt,),
        in_specs=[_row_spec(D_MODEL)] + [_t3_spec(512) for _ in secs_t3],
        out_specs=[_full_spec((512, D_MODEL)) for _ in secs_t3],
        out_shape=[jax.ShapeDtypeStruct((512, D_MODEL), F32) for _ in secs_t3],
        compiler_params=_cp(40, ("arbitrary",)),
    )(hb, *secs_t3)


def _dh_bwd(secs, secs_t3, w_rm, w_t, dz, x, metapad, g_in, nt):
    n, m = len(secs), len(secs_t3)
    offs = OFF_GA_R + np.cumsum([0] + [s.shape[1] for s in secs])

    def body(*refs):
        sec_refs, t3_refs = refs[:n], refs[n:n + m]
        wrm_ref, wt_ref, dz_ref, x_ref, mp_ref, g_ref = refs[n + m:n + m + 6]
        dx_ref, dmeta_ref, dg_ref, db_ref = refs[n + m + 6:]
        i = pl.program_id(0)

        @pl.when(i == 0)
        def _():
            dg_ref[...] = jnp.zeros_like(dg_ref)
            db_ref[...] = jnp.zeros_like(db_ref)

        dh = ALPHA * dz_ref[...]
        for s_ref, lo, hi in zip(sec_refs, offs[:-1], offs[1:]):
            dh = dh + lax.dot_general(s_ref[...], wrm_ref[:, lo:hi], (((1,), (1,)), ((), ())),
                                      preferred_element_type=F32)
        for idx, t_ref in enumerate(t3_refs):
            dh = dh + lax.dot_general(t_ref[0], wt_ref[idx * 512:(idx + 1) * 512, :], (((0,), (0,)), ((), ())),
                                      preferred_element_type=F32)
        x0 = jnp.where(i == 0, mp_ref[...], x_ref[...])
        xhat, rstd = _ln_stats(x0)
        dg_ref[...] += jnp.sum(dh * xhat, axis=0, keepdims=True)
        db_ref[...] += jnp.sum(dh, axis=0, keepdims=True)
        dx = _ln_bwd(dh, xhat, rstd, g_ref[...])
        dx_ref[...] = dx

        @pl.when(i == 0)
        def _():
            dmeta_ref[...] = dx

    seq = (nt - 1) * TILE
    return pl.pallas_call(
        body, name="dh_bwd", grid=(nt,),
        in_specs=[_row_spec(s.shape[1]) for s in secs] + [_t3_spec(512) for _ in secs_t3]
        + [_full_spec(w_rm.shape), _full_spec(w_t.shape), _row_spec(D_MODEL), _row_spec(D_MODEL, shift=True),
           _full_spec((TILE, D_MODEL)), _full_spec((1, D_MODEL))],
        out_specs=[_row_spec(D_MODEL, shift=True), _full_spec((TILE, D_MODEL)), _full_spec((1, D_MODEL)),
                   _full_spec((1, D_MODEL))],
        out_shape=[jax.ShapeDtypeStruct((seq, D_MODEL), F32), jax.ShapeDtypeStruct((TILE, D_MODEL), F32),
                   jax.ShapeDtypeStruct((1, D_MODEL), F32), jax.ShapeDtypeStruct((1, D_MODEL), F32)],
        compiler_params=_cp(56, ("arbitrary",)),
    )(*secs, *secs_t3, w_rm, w_t, dz, x, metapad, g_in)


RB = 256
SMALL_ROWS = 48


def _repack_weights(all_in, all_small):
    n_cw = D_CONV // N_DEV

    def body(a_ref, s_ref, wr_ref, wt_ref, mp_ref, cw_ref):
        fill = jnp.zeros((512 - ROWS_IN, RB), BF16)
        full = jnp.concatenate([jnp.concatenate([a_ref[d], fill], axis=0).T[:, :SHARD_IN] for d in range(N_DEV)],
                               axis=1)
        qkv = full[:, :1536]
        wr_ref[:, :1536] = qkv
        wr_ref[:, 1536:OFF_F_R] = full[:, 1544:]
        wr_ref[:, OFF_F_R:] = jnp.concatenate([full[:, 1536:1544], jnp.zeros((RB, LANES - N_HEADS), BF16)], axis=1)
        wt_ref[...] = qkv.T

        @pl.when(pl.program_id(0) == 0)
        def _():
            mp_ref[0:PAD, :] = jnp.zeros((PAD, D_MODEL), F32)
            mp_ref[PAD:, :] = jnp.concatenate([s_ref[d, 0:N_META, :] for d in range(N_DEV)], axis=1)
            cw_ref[...] = jnp.concatenate([s_ref[d, N_META:, 0:n_cw] for d in range(N_DEV)], axis=1)

    return pl.pallas_call(
        body, name="repack_weights", grid=(D_MODEL // RB,),
        in_specs=[pl.BlockSpec((N_DEV, ROWS_IN, RB), lambda i: (0, 0, i)), _full_spec((N_DEV, SMALL_ROWS, LANES))],
        out_specs=[pl.BlockSpec((RB, W_COLS), lambda i: (i, 0)), pl.BlockSpec((1536, RB), lambda i: (0, i)),
                   _full_spec((TILE, D_MODEL)), _full_spec((32, D_CONV))],
        out_shape=[jax.ShapeDtypeStruct((D_MODEL, W_COLS), BF16), jax.ShapeDtypeStruct((1536, D_MODEL), BF16),
                   jax.ShapeDtypeStruct((TILE, D_MODEL), F32), jax.ShapeDtypeStruct((32, D_CONV), F32)],
        compiler_params=_cp(40, ("arbitrary",)),
    )(all_in, all_small)


def _unpack_dw_in(dw_rm, dw_t):
    def body(dga_ref, du_ref, dug_ref, dgc_ref, dfl_ref, dq_ref, dk_ref, dv_ref, out_ref, outb_ref):
        full = jnp.concatenate([dq_ref[...].T, dk_ref[...].T, dv_ref[...].T, dfl_ref[:, 0:N_HEADS], dga_ref[...],
                                du_ref[...], dug_ref[...], dgc_ref[...]], axis=1)
        pad = jnp.zeros((RB, 512 - SHARD_IN), F32)
        for d in range(N_DEV):
            blk = jnp.concatenate([full[:, SHARD_IN * d:SHARD_IN * (d + 1)], pad], axis=1).T[:ROWS_IN]
            out_ref[d] = blk
            outb_ref[d] = blk.astype(BF16)

    rm = pl.BlockSpec((RB, 512), lambda i: (i, 0))
    tr = pl.BlockSpec((512, RB), lambda i: (0, i))
    blocks = pl.BlockSpec((N_DEV, ROWS_IN, RB), lambda i: (0, 0, i))
    return pl.pallas_call(
        body, name="unpack_dw_in", grid=(D_MODEL // RB,),
        in_specs=[rm, rm, rm, rm, pl.BlockSpec((RB, LANES), lambda i: (i, 0)), tr, tr, tr],
        out_specs=[blocks, blocks],
        out_shape=[jax.ShapeDtypeStruct((N_DEV, ROWS_IN, D_MODEL), F32),
                   jax.ShapeDtypeStruct((N_DEV, ROWS_IN, D_MODEL), BF16)],
        compiler_params=_cp(48, ("arbitrary",)),
    )(*dw_rm, *dw_t)


def _local_step(x, target, metapad, cw, w_r, w_t, w_pw_full, w_out_full, ln_in_g, ln_in_b, b_f, conv_b, ln_conv_g,
                ln_conv_b, ln_out_g, ln_out_b):
    seq = x.shape[0]
    nt = seq // TILE + 1
    row = lambda a: a.reshape(1, -1).astype(F32)
    bf_pad = jnp.pad(row(b_f), ((0, 0), (0, LANES - N_HEADS)))
    g_in, b_in = row(ln_in_g), row(ln_in_b)
    g_cv, b_cv, c_b = row(ln_conv_g), row(ln_conv_b), row(conv_b)
    g_out, b_out = row(ln_out_g), row(ln_out_b)

    h, hb, qT3, kT3, vT3, k, v, ga, u, ug, gc, fl = _proj_fwd(x, metapad, g_in, b_in, w_r, nt)
    kx3 = _cumsum_fwd(fl, bf_pad, nt)
    oT3, o, lse4 = _attn_fwd(qT3, k, kx3, vT3, nt)
    co, hc, pw = _conv_fwd(u, ug, cw, c_b, g_cv, b_cv, w_pw_full, nt)
    y, dz, loss, dg_out, db_out = _out_fwd(o, ga, pw, gc, h, w_out_full, g_out, b_out, target, nt)
    doT3, dga, dgc, dw_out, dco, dw_pw, dg_cv, db_cv, dc_b = _out_bwd(dz, y, o, ga, pw, gc, w_out_full, hc, co,
                                                                      w_pw_full, g_cv, b_cv, nt)
    du, dug, dcw = _conv_bwd_taps(dco, u, ug, cw, nt)
    dqT3, dkT3, dvT3, dck, dcq4 = _attn_bwd(qT3, kT3, k, kx3, v, oT3, doT3, lse4, nt)
    dfl, dbf = _cumsum_bwd(dck, dcq4, fl, bf_pad, nt)
    secs = (dga, du, dug, dgc, dfl)
    secs_t3 = (dqT3, dkT3, dvT3)
    dw_rm = _dw_rowmajor(hb, secs, nt)
    dw_t = _dw_transposed(hb, secs_t3, nt)
    grad_x, dmetapad, dg_in, db_in = _dh_bwd(secs, secs_t3, w_r, w_t, dz, x, metapad, g_in, nt)
    pieces = dict(loss=loss, metapad=dmetapad, ln_in_g=dg_in, ln_in_b=db_in, w_in_rm=dw_rm, w_in_t=dw_t, b_f=dbf,
                  conv_w=dcw, conv_b=dc_b, ln_conv_g=dg_cv, ln_conv_b=db_cv, w_pw=dw_pw, w_out=dw_out,
                  ln_out_g=dg_out, ln_out_b=db_out)
    return grad_x, pieces


MESH = pl.DeviceIdType.MESH
ANY = pl.BlockSpec(memory_space=pl.ANY)


def _mesh_pos():
    return lax.axis_index("x"), lax.axis_index("y"), lax.axis_index("c")


GATHER_SEMS = 8


def _gather_body(x_refs, out_refs, send_sems, recv_sems, local_sems):
    n = len(x_refs)
    x, y, c = _mesh_pos()
    me, sibling = (x, y, c), (x, y, 1 - c)
    xn, yn, dg = (1 - x, y), (x, 1 - y), (1 - x, 1 - y)

    def slot(a, px, py, pc, half=None):
        blk = out_refs[a].at[4 * px + 2 * py + pc]
        if half is None:
            return blk
        top = -(-blk.shape[0] // (2 * BF16_ROWS)) * BF16_ROWS
        return blk.at[pl.ds(0, top)] if half == 0 else blk.at[pl.ds(top, blk.shape[0] - top)]

    def copy(a, k, block, to, src=None, half=None):
        return pltpu.make_async_remote_copy(
            src_ref=slot(a, *block, half) if src is None else src, dst_ref=slot(a, *block, half),
            send_sem=send_sems.at[GATHER_SEMS * a + k], recv_sem=recv_sems.at[GATHER_SEMS * a + k], device_id=to,
            device_id_type=MESH)

    arrays = range(n)
    mine = [pltpu.make_async_copy(x_refs[a], slot(a, *me), local_sems.at[a]) for a in arrays]
    for cp in mine:
        cp.start()
    sent = []
    for a in arrays:
        sent += [copy(a, 0, me, sibling, src=x_refs[a]), copy(a, 1, me, (*xn, c), src=x_refs[a]),
                 copy(a, 2, me, (*yn, c), src=x_refs[a])]
    for cp in sent:
        cp.start()

    def also(cp):
        cp.start()
        sent.append(cp)

    for a in arrays:
        copy(a, 2, (*yn, c), me).wait_recv()
        also(copy(a, 3, (*yn, c), (*xn, c), half=0))
        also(copy(a, 6, (*yn, c), sibling))
        copy(a, 1, (*xn, c), me).wait_recv()
        also(copy(a, 4, (*xn, c), (*yn, c), half=1))
        also(copy(a, 5, (*xn, c), sibling))
    for a in arrays:
        copy(a, 3, (*dg, c), me, half=0).wait_recv()
        copy(a, 4, (*dg, c), me, half=1).wait_recv()
        also(copy(a, 7, (*dg, c), sibling))
    for a in arrays:
        copy(a, 0, sibling, me).wait_recv()
        copy(a, 5, (*xn, 1 - c), me).wait_recv()
        copy(a, 6, (*yn, 1 - c), me).wait_recv()
        copy(a, 7, (*dg, 1 - c), me).wait_recv()
    for cp in sent:
        cp.wait_send()
    for cp in mine:
        cp.wait()


def _all_gather(blks, name):
    n = len(blks)

    def body(*refs):
        _gather_body(refs[:n], refs[n:2 * n], *refs[2 * n:])

    return pl.pallas_call(
        body, name=name, out_shape=[jax.ShapeDtypeStruct((N_DEV, *b.shape), b.dtype) for b in blks],
        in_specs=[ANY] * n, out_specs=[ANY] * n,
        scratch_shapes=[pltpu.SemaphoreType.DMA((GATHER_SEMS * n,)), pltpu.SemaphoreType.DMA((GATHER_SEMS * n,)),
                        pltpu.SemaphoreType.DMA((n,))],
    )(*blks)


def _exchange_sibling(g8s, small):
    n = len(g8s)

    def body(*refs):
        g_refs, s_ref, out_refs, a_ref = refs[:n], refs[n], refs[n + 1:2 * n + 1], refs[2 * n + 1]
        send_sems, recv_sems, a_send, a_recv, a_local = refs[2 * n + 2:]
        x, y, c = _mesh_pos()
        cps = [pltpu.make_async_remote_copy(
            src_ref=g_refs[a].at[2 * q + (1 - c)], dst_ref=out_refs[a].at[q], send_sem=send_sems.at[4 * a + q],
            recv_sem=recv_sems.at[4 * a + q], device_id=(x, y, 1 - c), device_id_type=MESH)
            for a in range(n) for q in range(4)]
        for cp in cps:
            cp.start()
        _gather_body([s_ref], [a_ref], a_send, a_recv, a_local)
        for cp in cps:
            cp.wait()

    outs = pl.pallas_call(
        body, name="rs_sibling",
        out_shape=[jax.ShapeDtypeStruct((4, *g.shape[1:]), g.dtype) for g in g8s]
        + [jax.ShapeDtypeStruct((N_DEV, *small.shape), small.dtype)],
        in_specs=[ANY] * (n + 1), out_specs=[ANY] * (n + 1),
        scratch_shapes=[pltpu.SemaphoreType.DMA((4 * n,)), pltpu.SemaphoreType.DMA((4 * n,)),
                        pltpu.SemaphoreType.DMA((GATHER_SEMS,)), pltpu.SemaphoreType.DMA((GATHER_SEMS,)),
                        pltpu.SemaphoreType.DMA((1,))],
    )(*g8s, small)
    return outs[:n], outs[n]


def _exchange_chips(p4s):
    n = len(p4s)

    def body(*refs):
        p_refs, out_refs, send_sems, recv_sems = refs[:n], refs[n:2 * n], refs[2 * n], refs[2 * n + 1]
        x, y, c = _mesh_pos()
        chips = [(1 - x, y), (x, 1 - y), (1 - x, 1 - y)]
        cps = [pltpu.make_async_remote_copy(
            src_ref=p_refs[a].at[2 * cx + cy], dst_ref=out_refs[a].at[k], send_sem=send_sems.at[3 * a + k],
            recv_sem=recv_sems.at[3 * a + k], device_id=(cx, cy, c), device_id_type=MESH)
            for k, (cx, cy) in enumerate(chips) for a in range(n)]
        for cp in cps:
            cp.start()
        for cp in cps:
            cp.wait()

    return pl.pallas_call(
        body, name="rs_chips", out_shape=[jax.ShapeDtypeStruct((3, *p.shape[1:]), p.dtype) for p in p4s],
        in_specs=[ANY] * n, out_specs=[ANY] * n,
        scratch_shapes=[pltpu.SemaphoreType.DMA((3 * n,)), pltpu.SemaphoreType.DMA((3 * n,))],
    )(*p4s)


def _rs_add_sibling(g8s, recvs, c_idx):
    n = len(g8s)

    def body(s_ref, *refs):
        g_refs, r_refs, p32_refs, pb_refs = (refs[k * n:(k + 1) * n] for k in range(4))
        for g_ref, r_ref, p32_ref, pb_ref in zip(g_refs, r_refs, p32_refs, pb_refs):
            p = g_ref[0] + r_ref[0].astype(F32)
            p32_ref[0] = p
            pb_ref[0] = p.astype(BF16)

    blk = lambda g: pl.BlockSpec((1, *g.shape[1:]), lambda q, s: (q, 0, 0))
    grid_spec = pltpu.PrefetchScalarGridSpec(
        num_scalar_prefetch=1, grid=(4,),
        in_specs=[pl.BlockSpec((1, *g.shape[1:]), lambda q, s: (2 * q + s[0], 0, 0)) for g in g8s]
        + [blk(g) for g in g8s],
        out_specs=[blk(g) for g in g8s] * 2)
    outs = pl.pallas_call(
        body, name="rs_add_sibling", grid_spec=grid_spec,
        out_shape=[jax.ShapeDtypeStruct((4, *g.shape[1:]), F32) for g in g8s]
        + [jax.ShapeDtypeStruct((4, *g.shape[1:]), BF16) for g in g8s],
        compiler_params=_cp(48, ("arbitrary",)),
    )(c_idx, *g8s, *recvs)
    return outs[:n], outs[n:]


def _rs_add_chips(p32s, recvs, q_idx):
    def body(s_ref, pin_ref, pout_ref, ppw_ref, rin_ref, rout_ref, rpw_ref, gin_ref, gout_ref, gpw_ref):
        def total(p_ref, r_ref):
            return ((p_ref[0] + r_ref[0].astype(F32)) + r_ref[1].astype(F32)) + r_ref[2].astype(F32)

        gin_ref[...] = total(pin_ref, rin_ref)[:SHARD_IN, :]
        gout_ref[0] = total(pout_ref, rout_ref)
        gpw_ref[0] = total(ppw_ref, rpw_ref)

    own = lambda p: pl.BlockSpec((1, *p.shape[1:]), lambda i, s: (s[0], 0, 0))
    whole = lambda shape: pl.BlockSpec(shape, lambda i, s: (0,) * len(shape))
    out_shapes = [(SHARD_IN, D_MODEL), (1, *p32s[1].shape[1:]), (1, *p32s[2].shape[1:])]
    grid_spec = pltpu.PrefetchScalarGridSpec(
        num_scalar_prefetch=1, grid=(1,),
        in_specs=[own(p) for p in p32s] + [whole(r.shape) for r in recvs],
        out_specs=[whole(s) for s in out_shapes])
    return pl.pallas_call(
        body, name="rs_add_chips", grid_spec=grid_spec,
        out_shape=[jax.ShapeDtypeStruct(s, F32) for s in out_shapes],
        compiler_params=_cp(48, ("arbitrary",)),
    )(q_idx, *p32s, *recvs)


SMALL_ROWS_G = 64
SMALL_LAYOUT = {
    "metapad": (0, N_META, D_MODEL), "conv_w": (16, 32, D_CONV), "ln_in_g": (48, 1, D_MODEL),
    "ln_in_b": (49, 1, D_MODEL), "b_f": (50, 1, LANES), "conv_b": (51, 1, D_CONV), "ln_conv_g": (52, 1, D_CONV),
    "ln_conv_b": (53, 1, D_CONV), "ln_out_g": (54, 1, D_MODEL), "ln_out_b": (55, 1, D_MODEL), "loss": (56, 1, LANES)}


def _pack_small(pieces):
    names = list(SMALL_LAYOUT)

    def body(*refs):
        out_ref = refs[-1]
        out_ref[...] = jnp.zeros_like(out_ref)
        for name, ref in zip(names, refs[:-1]):
            r0, nr, nl = SMALL_LAYOUT[name]
            src = ref[PAD:, :] if name == "metapad" else ref[...]
            out_ref[r0:r0 + nr, 0:nl] = src

    return pl.pallas_call(body, name="pack_small", out_shape=jax.ShapeDtypeStruct((SMALL_ROWS_G, D_MODEL), F32),
                          compiler_params=_cp(16))(*[pieces[n] for n in names])


def _sum_small(gathered):
    names = list(SMALL_LAYOUT)

    def body(a_ref, *out_refs):
        acc = a_ref[0]
        for d in range(1, N_DEV):
            acc = acc + a_ref[d]
        for name, ref in zip(names, out_refs):
            r0, nr, nl = SMALL_LAYOUT[name]
            ref[...] = acc[r0:r0 + nr, 0:nl]

    outs = pl.pallas_call(
        body, name="sum_small",
        out_shape=[jax.ShapeDtypeStruct(SMALL_LAYOUT[n][1:], F32) for n in names], compiler_params=_cp(16))(gathered)
    return dict(zip(names, outs))


def _adamw(ws, gs, ms, vs):
    n = len(ws)
    c1 = 1.0 - ADAM_B1 ** ADAM_STEP
    c2 = 1.0 - ADAM_B2 ** ADAM_STEP

    def body(*refs):
        w_refs, g_refs, m_refs, v_refs = (refs[k * n:(k + 1) * n] for k in range(4))
        d_refs, nm_refs, nv_refs = (refs[(4 + k) * n:(5 + k) * n] for k in range(3))
        for w_ref, g_ref, m_ref, v_ref, d_ref, nm_ref, nv_ref in zip(w_refs, g_refs, m_refs, v_refs, d_refs,
                                                                     nm_refs, nv_refs):
            g = g_ref[...]
            m = ADAM_B1 * m_ref[...] + (1.0 - ADAM_B1) * g
            v = ADAM_B2 * v_ref[...] + (1.0 - ADAM_B2) * (g * g)
            nm_ref[...] = m
            nv_ref[...] = v
            d_ref[...] = -ADAM_LR * ((m / c1) / (jnp.sqrt(v / c2) + ADAM_EPS) + ADAM_WD * w_ref[...])

    shapes = [jax.ShapeDtypeStruct(w.shape, F32) for w in ws]
    outs = pl.pallas_call(body, name="adamw", out_shape=shapes * 3, compiler_params=_cp(48))(*ws, *gs, *ms, *vs)
    return outs[:n], outs[n:2 * n], outs[2 * n:]


W_NAMES = ("meta", "ln_in_g", "ln_in_b", "w_in", "b_f", "conv_w", "conv_b", "ln_conv_g", "ln_conv_b", "w_pw",
           "w_out", "ln_out_g", "ln_out_b")


def kernel(x, meta, ln_in_g, ln_in_b, w_in, b_f, conv_w, conv_b, ln_conv_g, ln_conv_b, w_pw, w_out, ln_out_g, ln_out_b, loss_target, m_meta, m_ln_in_g, m_ln_in_b, m_w_in, m_b_f, m_conv_w, m_conv_b, m_ln_conv_g, m_ln_conv_b, m_w_pw, m_w_out, m_ln_out_g, m_ln_out_b, v_meta, v_ln_in_g, v_ln_in_b, v_w_in, v_b_f, v_conv_w, v_conv_b, v_ln_conv_g, v_ln_conv_b, v_w_pw, v_w_out, v_ln_out_g, v_ln_out_b):
    mx, my, mc = _mesh_pos()
    me = 4 * mx + 2 * my + mc
    n_meta_sh = D_MODEL // N_DEV
    n_cw_sh = D_CONV // N_DEV
    n_out_sh = D_MODEL // N_DEV
    n_pw_sh = D_CONV // N_DEV

    small_w = jnp.concatenate([meta, jnp.pad(conv_w[0], ((0, 1), (0, LANES - n_cw_sh)))], axis=0)
    all_in, all_out, all_pw, all_small = _all_gather(
        [jnp.pad(w_in[0].T, ((0, ROWS_IN - SHARD_IN), (0, 0))).astype(BF16), w_out[0].astype(BF16), w_pw[0].astype(BF16),
         small_w], "gather_weights")
    w_r, w_t, metapad, cw = _repack_weights(all_in, all_small)
    w_out_full = all_out.reshape(D_MODEL, D_MODEL)
    w_pw_full = all_pw.reshape(D_CONV, D_CONV)

    grad_x, pc = _local_step(x[0], loss_target[0], metapad, cw, w_r, w_t, w_pw_full, w_out_full, ln_in_g, ln_in_b,
                             b_f[0], conv_b[0], ln_conv_g[0], ln_conv_b[0], ln_out_g[0], ln_out_b[0])

    g_in8, g_in8_b = _unpack_dw_in(pc["w_in_rm"], pc["w_in_t"])
    g8s = [g_in8, pc["w_out"].reshape(N_DEV, n_out_sh, D_MODEL), pc["w_pw"].reshape(N_DEV, n_pw_sh, D_CONV)]
    from_sibling, all_small_g = _exchange_sibling([g_in8_b] + g8s[1:], _pack_small(pc))
    p32s, pbs = _rs_add_sibling(g8s, from_sibling, jnp.reshape(mc, (1,)).astype(jnp.int32))
    from_chips = _exchange_chips(pbs)
    g_w_in, g_w_out, g_w_pw = _rs_add_chips(p32s, from_chips, jnp.reshape(2 * mx + my, (1,)).astype(jnp.int32))

    sm = _sum_small(all_small_g)
    grads = {
        "meta": lax.dynamic_slice_in_dim(sm["metapad"], me * n_meta_sh, n_meta_sh, axis=1),
        "ln_in_g": sm["ln_in_g"].reshape(D_MODEL), "ln_in_b": sm["ln_in_b"].reshape(D_MODEL), "w_in": g_w_in.T[None],
        "b_f": sm["b_f"][:, :N_HEADS],
        "conv_w": lax.dynamic_slice_in_dim(sm["conv_w"], me * n_cw_sh, n_cw_sh, axis=1)[None, :CONV_WIDTH],
        "conv_b": sm["conv_b"], "ln_conv_g": sm["ln_conv_g"], "ln_conv_b": sm["ln_conv_b"], "w_pw": g_w_pw,
        "w_out": g_w_out, "ln_out_g": sm["ln_out_g"], "ln_out_b": sm["ln_out_b"]}
    loss_all = sm["loss"][0, 0]

    weights = dict(meta=meta, ln_in_g=ln_in_g, ln_in_b=ln_in_b, w_in=w_in, b_f=b_f, conv_w=conv_w, conv_b=conv_b,
                   ln_conv_g=ln_conv_g, ln_conv_b=ln_conv_b, w_pw=w_pw, w_out=w_out, ln_out_g=ln_out_g,
                   ln_out_b=ln_out_b)
    moms = dict(meta=m_meta, ln_in_g=m_ln_in_g, ln_in_b=m_ln_in_b, w_in=m_w_in, b_f=m_b_f, conv_w=m_conv_w,
                conv_b=m_conv_b, ln_conv_g=m_ln_conv_g, ln_conv_b=m_ln_conv_b, w_pw=m_w_pw, w_out=m_w_out,
                ln_out_g=m_ln_out_g, ln_out_b=m_ln_out_b)
    vels = dict(meta=v_meta, ln_in_g=v_ln_in_g, ln_in_b=v_ln_in_b, w_in=v_w_in, b_f=v_b_f, conv_w=v_conv_w,
                conv_b=v_conv_b, ln_conv_g=v_ln_conv_g, ln_conv_b=v_ln_conv_b, w_pw=v_w_pw, w_out=v_w_out,
                ln_out_g=v_ln_out_g, ln_out_b=v_ln_out_b)

    def to_kernel(name, a):
        if name == "w_in":
            return a[0].T
        return a.reshape(1, -1) if a.ndim == 1 else a

    def from_kernel(name, a):
        return a.T[None] if name == "w_in" else a.reshape(weights[name].shape)

    upd = _adamw(*[[to_kernel(n, d[n]) for n in W_NAMES] for d in (weights, grads, moms, vels)])
    deltas, new_m, new_v = ([from_kernel(n, a) for n, a in zip(W_NAMES, part)] for part in upd)
    return (loss_all, grad_x[None], *[grads[n] for n in W_NAMES], *deltas, *new_m, *new_v)
```

```python
import jax
import jax.numpy as jnp
import numpy as np
from jax import lax
from jax.experimental import pallas as pl
from jax.experimental.pallas import tpu as pltpu

F32 = jnp.float32
BF16 = jnp.bfloat16

D_MODEL = 1024
D_ATTN = 512
D_CONV = 512
N_HEADS = 8
HEAD_DIM = 64
N_META = 16
CONV_WIDTH = 31
LN_EPS = 1e-5
ALPHA = 2.0 ** 0.25
SCALE = HEAD_DIM ** -0.5
LOG2E = 1.4426950408889634
ADAM_LR, ADAM_B1, ADAM_B2, ADAM_EPS, ADAM_WD, ADAM_STEP = 0.001, 0.9, 0.999, 1e-08, 0.01, 10

N_DEV = 8
D_IN = 3592
SHARD_IN = D_IN // N_DEV
BF16_ROWS = 16
ROWS_IN = -(-SHARD_IN // BF16_ROWS) * BF16_ROWS
TILE = 256
PAD = TILE - N_META
HALO = 32
SHIFT_ROWS = TILE + HALO
EXT_ROWS = SHIFT_ROWS + 8
NEG = -1e30
LANES = 128
W_COLS = 7 * 512 + LANES
OFF_GA_R, OFF_F_R = 1536, 3584
MIB = 1024 * 1024


def _cp(vmem_mib, sem=None):
    kw = dict(vmem_limit_bytes=vmem_mib * MIB)
    if sem is not None:
        kw["dimension_semantics"] = sem
    return pltpu.CompilerParams(**kw)


def _sigmoid(x):
    return 1.0 / (1.0 + jnp.exp(-x))


def _silu_and_grad(x):
    s = _sigmoid(x)
    return x * s, s * (1.0 + x * (1.0 - s))


def _ln_stats(x):
    mu = jnp.mean(x, axis=-1, keepdims=True)
    xc = x - mu
    var = jnp.mean(xc * xc, axis=-1, keepdims=True)
    rstd = lax.rsqrt(var + LN_EPS)
    return xc * rstd, rstd


def _ln_bwd(dy, xhat, rstd, g):
    dxh = dy * g
    m1 = jnp.mean(dxh, axis=-1, keepdims=True)
    m2 = jnp.mean(dxh * xhat, axis=-1, keepdims=True)
    return rstd * (dxh - m1 - xhat * m2)


def _row_spec(cols, shift=False):
    if shift:
        return pl.BlockSpec((TILE, cols), lambda i: (jnp.maximum(i - 1, 0), 0))
    return pl.BlockSpec((TILE, cols), lambda i: (i, 0))


def _full_spec(shape):
    nd = len(shape)
    return pl.BlockSpec(shape, lambda i: (0,) * nd)


def _t3_spec(ch):
    return pl.BlockSpec((1, ch, TILE), lambda i: (i, 0, 0))


def _proj_fwd(x, metapad, g_in, b_in, w_r, nt):
    lp = nt * TILE

    def body(x_ref, mp_ref, g_ref, b_ref, w_ref, h_ref, hb_ref, qT_ref, kT_ref, vT_ref, k_ref, v_ref,
             ga_ref, u_ref, ug_ref, gc_ref, fl_ref):
        i = pl.program_id(0)
        x0 = jnp.where(i == 0, mp_ref[...], x_ref[...])
        xhat, _ = _ln_stats(x0)
        h = xhat * g_ref[...] + b_ref[...]
        h_ref[...] = h
        hb = h.astype(BF16)
        hb_ref[...] = hb

        def sec(off, n=512):
            return jnp.dot(hb, w_ref[:, off:off + n], preferred_element_type=F32)

        qT_ref[0] = (sec(0) * (SCALE * LOG2E)).T.astype(BF16)
        k = sec(512)
        kT_ref[0] = k.T.astype(BF16)
        k_ref[...] = k.astype(BF16)
        v = sec(1024)
        vT_ref[0] = v.T.astype(BF16)
        v_ref[...] = v.astype(BF16)
        ga_ref[...] = sec(OFF_GA_R).astype(BF16)
        u_ref[...] = sec(OFF_GA_R + 512).astype(BF16)
        ug_ref[...] = sec(OFF_GA_R + 1024).astype(BF16)
        gc_ref[...] = sec(OFF_GA_R + 1536).astype(BF16)
        fl_ref[...] = sec(OFF_F_R, LANES)

    t3 = jax.ShapeDtypeStruct((nt, 512, TILE), BF16)
    rm = lambda dt: jax.ShapeDtypeStruct((lp, 512), dt)
    return pl.pallas_call(
        body, name="proj_fwd", grid=(nt,),
        in_specs=[_row_spec(D_MODEL, shift=True), _full_spec((TILE, D_MODEL)), _full_spec((1, D_MODEL)),
                  _full_spec((1, D_MODEL)), _full_spec((D_MODEL, W_COLS))],
        out_specs=[_row_spec(D_MODEL), _row_spec(D_MODEL), _t3_spec(512), _t3_spec(512), _t3_spec(512),
                   _row_spec(512), _row_spec(512),
                   _row_spec(512), _row_spec(512), _row_spec(512), _row_spec(512), _row_spec(LANES)],
        out_shape=[jax.ShapeDtypeStruct((lp, D_MODEL), F32), jax.ShapeDtypeStruct((lp, D_MODEL), BF16),
                   t3, t3, t3, rm(BF16), rm(BF16),
                   rm(BF16), rm(BF16), rm(BF16), rm(BF16), jax.ShapeDtypeStruct((lp, LANES), F32)],
        compiler_params=_cp(56, ("arbitrary",)),
    )(x, metapad, g_in, b_in, w_r)


def _row_mask(i, shape):
    r = lax.broadcasted_iota(jnp.int32, shape, 0)
    return (r >= PAD) | (i > 0)


def _cumsum_fwd(fl, bf_pad, nt):
    lp = nt * TILE

    def body(fl_ref, bf_ref, kx_ref, carry):
        i = pl.program_id(0)

        @pl.when(i == 0)
        def _():
            carry[...] = jnp.zeros_like(carry)

        z = fl_ref[...] + bf_ref[...]
        lf = jnp.minimum(z, 0.0) - jnp.log(1.0 + jnp.exp(-jnp.abs(z)))
        lane = lax.broadcasted_iota(jnp.int32, (TILE, LANES), 1)
        real = _row_mask(i, (TILE, LANES))
        lf = jnp.where(real & (lane < N_HEADS), lf, 0.0)
        r = lax.broadcasted_iota(jnp.int32, (TILE, TILE), 0)
        c = lax.broadcasted_iota(jnp.int32, (TILE, TILE), 1)
        tril = (c <= r).astype(F32)
        cs = jnp.dot(tril, lf, precision=lax.Precision.HIGHEST, preferred_element_type=F32) + carry[...]
        carry[...] = cs[TILE - 1:TILE, :]
        bias = jnp.where(real, cs * (-LOG2E), NEG)
        hi = bias.astype(BF16).astype(F32)
        mid = (bias - hi).astype(BF16).astype(F32)
        lo = (bias - hi - mid).astype(BF16).astype(F32)
        for p in range(N_HEADS // 2):
            out = jnp.zeros((TILE, LANES), F32)
            for hh in range(2):
                for part, piece in enumerate((hi, mid, lo)):
                    dst, src = 3 * hh + part, 2 * p + hh
                    moved = piece if dst == src else pltpu.roll(piece, (dst - src) % LANES, 1)
                    out = jnp.where(lane == dst, moved, out)
            kx_ref[p] = out.astype(BF16)

    return pl.pallas_call(
        body, name="cumsum_fwd", grid=(nt,),
        in_specs=[_row_spec(LANES), _full_spec((1, LANES))],
        out_specs=pl.BlockSpec((N_HEADS // 2, TILE, LANES), lambda i: (0, i, 0)),
        out_shape=jax.ShapeDtypeStruct((N_HEADS // 2, lp, LANES), BF16),
        scratch_shapes=[pltpu.VMEM((1, LANES), F32)],
        compiler_params=_cp(32, ("arbitrary",)),
    )(fl, bf_pad)


def _head_rows(blk, hh):
    r = lax.broadcasted_iota(jnp.int32, blk.shape, 0)
    return jnp.where((r >= hh * HEAD_DIM) & (r < (hh + 1) * HEAD_DIM), blk, jnp.zeros_like(blk))


def _two_heads(blk):
    return jnp.concatenate([_head_rows(blk, 0), _head_rows(blk, 1)], axis=1)


def _bias_rows():
    r = lax.broadcasted_iota(jnp.int32, (LANES, 2 * TILE), 0)
    c = lax.broadcasted_iota(jnp.int32, (LANES, 2 * TILE), 1)
    return jnp.where(((r < 3) & (c < TILE)) | ((r >= 3) & (r < 6) & (c >= TILE)), 1.0, 0.0).astype(BF16)


def _diag_mask(s):
    kpos = lax.broadcasted_iota(jnp.int32, (TILE, TILE), 0)
    qpos = lax.broadcasted_iota(jnp.int32, (TILE, TILE), 1)
    return jnp.where(kpos <= qpos, s, NEG)


def _stream(n, first, nxt, scores, update, unroll=8):
    if n == 0:
        return
    scores(first, 0)

    def step(_, idx):
        for _u in range(unroll):
            idx_b = nxt(idx)
            scores(idx_b, 1)
            update(idx, 0)
            idx = nxt(idx_b)
            scores(idx, 0)
            update(idx_b, 1)
        return idx

    steps = (n - 1) // (2 * unroll)
    idx = lax.fori_loop(0, steps, step, first)
    left = n - 2 * unroll * steps
    for r in range(left - 1):
        idx_b = nxt(idx)
        scores(idx_b, (r + 1) % 2)
        update(idx, r % 2)
        idx = idx_b
    update(idx, (left - 1) % 2)


def _next_below_diagonal(idx):
    i, j = idx
    wrap = j + 1 >= i
    return jnp.where(wrap, i + 1, i), jnp.where(wrap, 0, j + 1)


def _tile_rows(t):
    return pl.ds(pl.multiple_of(t * TILE, TILE), TILE)


def _two_streams(nt):
    load, group = [0, 0], {}
    for i in sorted(range(1, nt), reverse=True):
        g = 0 if load[0] <= load[1] else 1
        group[i] = g
        load[g] += i
    rows = [[(i, i, j) for i in range(1, nt) if group[i] == g for j in range(i)] for g in range(2)]
    length = max(len(r) for r in rows)
    rows = [r + [(nt, 0, 0)] * (length - len(r)) for r in rows]
    return group, np.asarray(rows, np.int32).reshape(2, -1), length


def _attn_fwd(qT3, k, kx3, vT3, nt):
    lp = nt * TILE
    npair = N_HEADS // 2
    group, table, n_stream = _two_streams(nt)

    def body(tab_ref, qT_ref, k_ref, kx_ref, vT_ref, oT_ref, o_ref, lse_ref, sbuf, m_0, l_0, acc_0, m_1, l_1, acc_1):
        ones = _bias_rows()
        states = ((m_0, l_0, acc_0), (m_1, l_1, acc_1))

        def scores(i, j, slot):
            qcat = jnp.concatenate([_two_heads(qT_ref[i]), ones], axis=0)
            kext = jnp.concatenate([k_ref[_tile_rows(j), :], kx_ref[0, _tile_rows(j), :]], axis=1)
            sbuf[slot] = jnp.dot(kext, qcat, preferred_element_type=F32)

        def update(st, j, slot, state, diag):
            m_s, l_s, acc_s = state
            for hh in range(2):
                s = sbuf[slot, :, hh * TILE:(hh + 1) * TILE]
                vj = vT_ref[j, hh * HEAD_DIM:(hh + 1) * HEAD_DIM, :]
                if diag:
                    s = _diag_mask(s)
                    m_new = jnp.max(s, axis=0, keepdims=True)
                    p = jnp.exp2(s - m_new)
                    l_s[st, hh] = jnp.sum(p, axis=0, keepdims=True)
                    acc_s[st, hh] = jnp.dot(vj, p.astype(BF16), preferred_element_type=F32)
                else:
                    m_prev = m_s[st, hh]
                    m_new = jnp.maximum(m_prev, jnp.max(s, axis=0, keepdims=True))
                    a = jnp.exp2(m_prev - m_new)
                    p = jnp.exp2(s - m_new)
                    l_s[st, hh] = a * l_s[st, hh] + jnp.sum(p, axis=0, keepdims=True)
                    acc_s[st, hh] = a * acc_s[st, hh] + jnp.dot(vj, p.astype(BF16), preferred_element_type=F32)
                m_s[st, hh] = m_new

        _stream(nt, jnp.int32(0), lambda t: t + 1, lambda t, slot: scores(t, t, slot),
                lambda t, slot: update(t, t, slot, states[0], True))
        for dst, src in zip(states[1], states[0]):
            dst[0:nt] = src[0:nt]
        for m_s, l_s, acc_s in states:
            m_s[nt] = jnp.full(m_s.shape[1:], NEG, F32)
            l_s[nt] = jnp.zeros(l_s.shape[1:], F32)
            acc_s[nt] = jnp.zeros(acc_s.shape[1:], F32)

        def entry(g, t):
            return tab_ref[g, 3 * t], tab_ref[g, 3 * t + 1], tab_ref[g, 3 * t + 2]

        def scores2(t, slot):
            for g in range(2):
                _, qi, kj = entry(g, t)
                scores(qi, kj, 2 * g + slot)

        def update2(t, slot):
            for g in range(2):
                st, _, kj = entry(g, t)
                update(st, kj, 2 * g + slot, states[g], False)

        _stream(n_stream, jnp.int32(0), lambda t: t + 1, scores2, update2)

        for i in range(nt):
            m_s, l_s, acc_s = states[group.get(i, 0)]
            for hh in range(2):
                l = l_s[i, hh]
                oT_ref[i, hh * HEAD_DIM:(hh + 1) * HEAD_DIM, :] = acc_s[i, hh] / l
                lse_ref[0, i, hh:hh + 1, :] = m_s[i, hh] + jnp.log(l) * LOG2E
            o_ref[i * TILE:(i + 1) * TILE, :] = oT_ref[i].T.astype(BF16)

    blk_t = pl.BlockSpec((nt, LANES, TILE), lambda p, tab: (0, p, 0))
    blk_rm = pl.BlockSpec((lp, LANES), lambda p, tab: (0, p))
    blk_px = pl.BlockSpec((1, lp, LANES), lambda p, tab: (p, 0, 0))
    blk_st = pl.BlockSpec((1, nt, 8, TILE), lambda p, tab: (p, 0, 0, 0))
    state = [pltpu.VMEM((nt + 1, 2, 1, TILE), F32), pltpu.VMEM((nt + 1, 2, 1, TILE), F32),
             pltpu.VMEM((nt + 1, 2, HEAD_DIM, TILE), F32)]
    grid_spec = pltpu.PrefetchScalarGridSpec(
        num_scalar_prefetch=1, grid=(npair,), in_specs=[blk_t, blk_rm, blk_px, blk_t],
        out_specs=[blk_t, blk_rm, blk_st], scratch_shapes=[pltpu.VMEM((4, TILE, 2 * TILE), F32)] + state + state)
    return pl.pallas_call(
        body, name="attn_fwd", grid_spec=grid_spec,
        out_shape=[jax.ShapeDtypeStruct((nt, D_ATTN, TILE), F32),
                   jax.ShapeDtypeStruct((lp, D_ATTN), BF16),
                   jax.ShapeDtypeStruct((npair, nt, 8, TILE), F32)],
        compiler_params=_cp(60, ("arbitrary",)),
    )(jnp.asarray(table), qT3, k, kx3, vT3)


def _attn_bwd(qT3, kT3, k, kx3, v, oT3, doT3, lse4, nt):
    lp = nt * TILE
    npair = N_HEADS // 2

    def body(qT_ref, kT_ref, k_ref, kx_ref, v_ref, oT_ref, doT_ref, lse_ref,
             dqT_ref, dkT_ref, dvT_ref, dck_ref, dcq_ref, sbuf, dpbuf, dq_s, dk_s, dv_s, dc_s, tp_s, tds_s):
        ones = _bias_rows()

        def scores(idx, slot):
            i, j = idx
            qcat = jnp.concatenate([_two_heads(qT_ref[i]), ones], axis=0)
            kext = jnp.concatenate([k_ref[_tile_rows(j), :], kx_ref[0, _tile_rows(j), :]], axis=1)
            sbuf[slot] = jnp.dot(kext, qcat, preferred_element_type=F32)
            dpbuf[slot] = jnp.dot(v_ref[_tile_rows(j), :], _two_heads(doT_ref[i]), preferred_element_type=F32)

        def update(idx, slot, diag):
            i, j = idx
            for hh in range(2):
                hs = slice(hh * HEAD_DIM, (hh + 1) * HEAD_DIM)
                s = sbuf[slot, :, hh * TILE:(hh + 1) * TILE]
                if diag:
                    s = _diag_mask(s)
                p = jnp.exp2(s - lse_ref[0, i, hh:hh + 1, :])
                doh = doT_ref[i, hs, :]
                delta = jnp.sum(doh.astype(F32) * oT_ref[i, hs, :], axis=0, keepdims=True)
                ds = p * (dpbuf[slot, :, hh * TILE:(hh + 1) * TILE] - delta)
                dsb = ds.astype(BF16)
                tp_s[hh] = p.astype(BF16).T
                tds_s[hh] = dsb.T
                dv = jnp.dot(doh, tp_s[hh], preferred_element_type=F32)
                dk = jnp.dot(qT_ref[i, hs, :], tds_s[hh], preferred_element_type=F32)
                dq = jnp.dot(kT_ref[j, hs, :], dsb, preferred_element_type=F32)
                dc = ds[:, :LANES] + ds[:, LANES:]
                dcq = jnp.sum(ds, axis=0, keepdims=True)
                if diag:
                    dv_s[j, hh] = dv
                    dk_s[j, hh] = dk
                    dc_s[j, hh] = dc
                    dq_s[i, hs, :] = dq
                    dcq_ref[0, i, hh:hh + 1, :] = dcq
                else:
                    dv_s[j, hh] += dv
                    dk_s[j, hh] += dk
                    dc_s[j, hh] += dc
                    dq_s[i, hs, :] += dq
                    dcq_ref[0, i, hh:hh + 1, :] += dcq

        dcq_ref[...] = jnp.zeros_like(dcq_ref)
        zero = jnp.int32(0)
        _stream(nt, (zero, zero), lambda idx: (idx[0] + 1, idx[1] + 1), scores,
                lambda idx, slot: update(idx, slot, True))
        _stream(nt * (nt - 1) // 2, (zero + 1, zero), _next_below_diagonal, scores,
                lambda idx, slot: update(idx, slot, False))

        lane = lax.broadcasted_iota(jnp.int32, (TILE, LANES), 1)

        def finish(t, carry):
            dck = jnp.zeros((TILE, LANES), F32)
            for hh in range(2):
                hs = slice(hh * HEAD_DIM, (hh + 1) * HEAD_DIM)
                dkT_ref[t, hs, :] = (dk_s[t, hh] * (1.0 / LOG2E)).astype(BF16)
                dvT_ref[t, hs, :] = dv_s[t, hh].astype(BF16)
                dck = jnp.where(lane == hh, -jnp.sum(dc_s[t, hh], axis=1, keepdims=True), dck)
            dck_ref[0, _tile_rows(t), :] = dck
            dqT_ref[t] = (dq_s[t] * SCALE).astype(BF16)
            return carry

        lax.fori_loop(0, nt, finish, 0)

    blk_t = pl.BlockSpec((nt, LANES, TILE), lambda p: (0, p, 0))
    blk_rm = pl.BlockSpec((lp, LANES), lambda p: (0, p))
    blk_px = pl.BlockSpec((1, lp, LANES), lambda p: (p, 0, 0))
    blk_st = pl.BlockSpec((1, nt, 8, TILE), lambda p: (p, 0, 0, 0))
    t3 = jax.ShapeDtypeStruct((nt, D_ATTN, TILE), BF16)
    return pl.pallas_call(
        body, name="attn_bwd", grid=(npair,),
        in_specs=[blk_t, blk_t, blk_rm, blk_px, blk_rm, blk_t, blk_t, blk_st],
        out_specs=[blk_t, blk_t, blk_t, blk_px, blk_st],
        out_shape=[t3, t3, t3, jax.ShapeDtypeStruct((npair, lp, LANES), F32),
                   jax.ShapeDtypeStruct((npair, nt, 8, TILE), F32)],
        scratch_shapes=[pltpu.VMEM((2, TILE, 2 * TILE), F32), pltpu.VMEM((2, TILE, 2 * TILE), F32),
                        pltpu.VMEM((nt, LANES, TILE), F32), pltpu.VMEM((nt, 2, HEAD_DIM, TILE), F32),
                        pltpu.VMEM((nt, 2, HEAD_DIM, TILE), F32), pltpu.VMEM((nt, 2, TILE, LANES), F32),
                        pltpu.VMEM((2, TILE, TILE), BF16), pltpu.VMEM((2, TILE, TILE), BF16)],
        compiler_params=_cp(60, ("arbitrary",)),
    )(qT3, kT3, k, kx3, v, oT3, doT3, lse4)


def _glu(u, ug, i):
    return jnp.where(_row_mask(i, u.shape), u.astype(F32) * _sigmoid(ug.astype(F32)), 0.0)


def _shifted_copies(dst, src):
    for ph in range(8):
        dst[ph] = src[ph:ph + SHIFT_ROWS, :]


def _tap_window(sh, off, lanes, row0=0, rows=TILE):
    base = (off // 8) * 8 + row0
    return sh[off % 8, base:base + rows, lanes]


def _conv_fwd(u, ug, conv_w, conv_b, g, b, w_pw, nt):
    lp = nt * TILE

    def body(u_ref, ug_ref, up_ref, ugp_ref, w_ref, cb_ref, g_ref, b_ref, wpw_ref,
             co_ref, hc_ref, pw_ref, ext, sh):
        i = pl.program_id(0)
        prev = _glu(up_ref[...], ugp_ref[...], i - 1)
        ext[0:HALO, :] = jnp.where(i > 0, prev[TILE - HALO:, :], 0.0)
        ext[HALO:HALO + TILE, :] = _glu(u_ref[...], ug_ref[...], i)
        ext[HALO + TILE:, :] = jnp.zeros((8, D_CONV), F32)
        _shifted_copies(sh, ext)
        for lb in range(D_CONV // LANES):
            lanes = slice(lb * LANES, (lb + 1) * LANES)
            acc = jnp.zeros((TILE, LANES), F32) + cb_ref[:, lanes]
            for t in range(CONV_WIDTH):
                off = HALO - (CONV_WIDTH - 1) + t
                acc = acc + w_ref[t:t + 1, lanes] * _tap_window(sh, off, lanes)
            co_ref[:, lanes] = acc
        xhat, _ = _ln_stats(co_ref[...])
        a, _ = _silu_and_grad(xhat * g_ref[...] + b_ref[...])
        hc = a.astype(BF16)
        hc_ref[...] = hc
        pw_ref[...] = jnp.dot(hc, wpw_ref[...], preferred_element_type=F32).astype(BF16)

    rm = lambda dt: jax.ShapeDtypeStruct((lp, D_CONV), dt)
    return pl.pallas_call(
        body, name="conv_fwd", grid=(nt,),
        in_specs=[_row_spec(512), _row_spec(512), _row_spec(512, shift=True), _row_spec(512, shift=True),
                  _full_spec((32, 512)), _full_spec((1, 512)), _full_spec((1, 512)), _full_spec((1, 512)),
                  _full_spec((512, 512))],
        out_specs=[_row_spec(512), _row_spec(512), _row_spec(512)],
        out_shape=[rm(F32), rm(BF16), rm(BF16)],
        scratch_shapes=[pltpu.VMEM((EXT_ROWS, D_CONV), F32), pltpu.VMEM((8, SHIFT_ROWS, D_CONV), F32)],
        compiler_params=_cp(40, ("arbitrary",)),
    )(u, ug, u, ug, conv_w, conv_b, g, b, w_pw)


RING = 3


def _out_fwd(o, ga, pw, gc, h, w_out, g_out, b_out, target, nt):
    lp = nt * TILE

    def body(o_ref, ga_ref, pw_ref, gc_ref, h_any, wo_ref, go_ref, bo_ref, t_any,
             y_ref, dz_ref, loss_ref, dgo_ref, dbo_ref, hbuf, tbuf, sems):
        i = pl.program_id(0)

        def fetch(step):
            slot = lax.rem(step, RING)
            h_rows = pl.multiple_of(step * TILE, TILE)
            t_rows = pl.multiple_of(jnp.maximum(step - 1, 0) * TILE, TILE)
            return (pltpu.make_async_copy(h_any.at[pl.ds(h_rows, TILE)], hbuf.at[slot], sems.at[0, slot]),
                    pltpu.make_async_copy(t_any.at[pl.ds(t_rows, TILE)], tbuf.at[slot], sems.at[1, slot]))

        def start(step):
            for c in fetch(step):
                c.start()

        @pl.when(i == 0)
        def _():
            for step in range(RING - 1):
                start(step)

        @pl.when(i + (RING - 1) < nt)
        def _():
            start(i + (RING - 1))

        for c in fetch(i):
            c.wait()
        slot = lax.rem(i, RING)
        h_ref, t_ref = hbuf.at[slot], tbuf.at[slot]

        @pl.when(i == 0)
        def _():
            loss_ref[...] = jnp.zeros_like(loss_ref)
            dgo_ref[...] = jnp.zeros_like(dgo_ref)
            dbo_ref[...] = jnp.zeros_like(dbo_ref)

        ya, _ = _silu_and_grad(ga_ref[...].astype(F32))
        yc, _ = _silu_and_grad(gc_ref[...].astype(F32))
        ya = (o_ref[...].astype(F32) * ya).astype(BF16)
        yc = (pw_ref[...].astype(F32) * yc).astype(BF16)
        y_ref[:, :D_ATTN] = ya
        y_ref[:, D_ATTN:] = yc
        z = ALPHA * h_ref[...] + jnp.dot(ya, wo_ref[:D_ATTN, :], preferred_element_type=F32) \
            + jnp.dot(yc, wo_ref[D_ATTN:, :], preferred_element_type=F32)
        zhat, rstd = _ln_stats(z)
        out = zhat * go_ref[...] + bo_ref[...]
        live = (i > 0).astype(F32)
        err = (out - t_ref[...]) * live
        dout = err * (1.0 / D_MODEL)
        loss_ref[...] += 0.5 * jnp.sum(jnp.sum(err * dout, axis=0, keepdims=True), axis=1, keepdims=True)
        dgo_ref[...] += jnp.sum(dout * zhat, axis=0, keepdims=True)
        dbo_ref[...] += jnp.sum(dout, axis=0, keepdims=True)
        dz_ref[...] = _ln_bwd(dout, zhat, rstd, go_ref[...])

    return pl.pallas_call(
        body, name="out_fwd", grid=(nt,),
        in_specs=[_row_spec(512), _row_spec(512), _row_spec(512), _row_spec(512),
                  pl.BlockSpec(memory_space=pl.ANY), _full_spec((D_MODEL, D_MODEL)), _full_spec((1, D_MODEL)),
                  _full_spec((1, D_MODEL)), pl.BlockSpec(memory_space=pl.ANY)],
        out_specs=[_row_spec(D_MODEL), _row_spec(D_MODEL), _full_spec((1, LANES)), _full_spec((1, D_MODEL)),
                   _full_spec((1, D_MODEL))],
        out_shape=[jax.ShapeDtypeStruct((lp, D_MODEL), BF16), jax.ShapeDtypeStruct((lp, D_MODEL), F32),
                   jax.ShapeDtypeStruct((1, LANES), F32), jax.ShapeDtypeStruct((1, D_MODEL), F32),
                   jax.ShapeDtypeStruct((1, D_MODEL), F32)],
        scratch_shapes=[pltpu.VMEM((RING, TILE, D_MODEL), F32), pltpu.VMEM((RING, TILE, D_MODEL), F32),
                        pltpu.SemaphoreType.DMA((2, RING))],
        compiler_params=_cp(40, ("arbitrary",)),
    )(o, ga, pw, gc, h, w_out, g_out, b_out, target)


def _out_bwd(dz, y, o, ga, pw, gc, w_out, hc, co, w_pw, g_cv, b_cv, nt):
    lp = nt * TILE

    def body(dz_ref, y_ref, o_ref, ga_ref, pw_ref, gc_ref, wo_ref, hc_ref, co_ref, wpw_ref, g_ref, b_ref,
             doT_ref, dga_ref, dgc_ref, dwo_ref, dco_ref, dwpw_ref, dg_ref, db_ref, dcb_ref):
        i = pl.program_id(0)

        @pl.when(i == 0)
        def _():
            dwo_ref[...] = jnp.zeros_like(dwo_ref)
            dwpw_ref[...] = jnp.zeros_like(dwpw_ref)
            dg_ref[...] = jnp.zeros_like(dg_ref)
            db_ref[...] = jnp.zeros_like(db_ref)
            dcb_ref[...] = jnp.zeros_like(dcb_ref)

        dzb = dz_ref[...].astype(BF16)
        nt_dims = (((1,), (1,)), ((), ()))
        tn_dims = (((0,), (0,)), ((), ()))
        dya = lax.dot_general(dzb, wo_ref[:D_ATTN, :], nt_dims, preferred_element_type=F32)
        dyc = lax.dot_general(dzb, wo_ref[D_ATTN:, :], nt_dims, preferred_element_type=F32)
        sa, sga = _silu_and_grad(ga_ref[...].astype(F32))
        sc, sgc = _silu_and_grad(gc_ref[...].astype(F32))
        doT_ref[0] = (dya * sa).T.astype(BF16)
        dga_ref[...] = (dya * o_ref[...].astype(F32) * sga).astype(BF16)
        dpw_b = (dyc * sc).astype(BF16)
        dgc_ref[...] = (dyc * pw_ref[...].astype(F32) * sgc).astype(BF16)
        dwo_ref[...] += lax.dot_general(y_ref[...], dzb, tn_dims, preferred_element_type=F32)

        dhc = lax.dot_general(dpw_b, wpw_ref[...], nt_dims, preferred_element_type=F32)
        xhat, rstd = _ln_stats(co_ref[...])
        _, sg = _silu_and_grad(xhat * g_ref[...] + b_ref[...])
        dln = dhc * sg
        dg_ref[...] += jnp.sum(dln * xhat, axis=0, keepdims=True)
        db_ref[...] += jnp.sum(dln, axis=0, keepdims=True)
        dco = _ln_bwd(dln, xhat, rstd, g_ref[...])
        dco_ref[...] = dco
        dcb_ref[...] += jnp.sum(dco, axis=0, keepdims=True)
        dwpw_ref[...] += lax.dot_general(hc_ref[...], dpw_b, tn_dims, preferred_element_type=F32)

    rm = jax.ShapeDtypeStruct((lp, 512), BF16)
    vec = jax.ShapeDtypeStruct((1, D_CONV), F32)
    return pl.pallas_call(
        body, name="out_bwd", grid=(nt,),
        in_specs=[_row_spec(D_MODEL), _row_spec(D_MODEL), _row_spec(512), _row_spec(512), _row_spec(512),
                  _row_spec(512), _full_spec((D_MODEL, D_MODEL)), _row_spec(512), _row_spec(512),
                  _full_spec((512, 512)), _full_spec((1, 512)), _full_spec((1, 512))],
        out_specs=[_t3_spec(512), _row_spec(512), _row_spec(512), _full_spec((D_MODEL, D_MODEL)), _row_spec(512),
                   _full_spec((512, 512)), _full_spec((1, 512)), _full_spec((1, 512)), _full_spec((1, 512))],
        out_shape=[jax.ShapeDtypeStruct((nt, 512, TILE), BF16), rm, rm, jax.ShapeDtypeStruct((D_MODEL, D_MODEL), F32),
                   jax.ShapeDtypeStruct((lp, D_CONV), F32), jax.ShapeDtypeStruct((512, 512), F32), vec, vec, vec],
        compiler_params=_cp(56, ("arbitrary",)),
    )(dz, y, o, ga, pw, gc, w_out, hc, co, w_pw, g_cv, b_cv)


def _conv_bwd_taps(dco, u, ug, conv_w, nt):
    lp = nt * TILE

    def body(dco_ref, dcon_ref, u_ref, ug_ref, up_ref, ugp_ref, w3_ref, du_ref, dug_ref, dw_ref, ext, dext, sh, dsh,
             dhg_s, dw_s):
        i = pl.program_id(0)

        @pl.when(i == 0)
        def _():
            dw_s[...] = jnp.zeros_like(dw_s)

        prev = _glu(up_ref[...], ugp_ref[...], i - 1)
        ext[0:HALO, :] = jnp.where(i > 0, prev[TILE - HALO:, :], 0.0)
        ext[HALO:HALO + TILE, :] = _glu(u_ref[...], ug_ref[...], i)
        ext[HALO + TILE:, :] = jnp.zeros((8, D_CONV), F32)
        dext[0:TILE, :] = dco_ref[...]
        dext[TILE:TILE + HALO, :] = jnp.where(i < nt - 1, dcon_ref[0:HALO, :], 0.0)
        dext[TILE + HALO:, :] = jnp.zeros((8, D_CONV), F32)
        _shifted_copies(sh, ext)
        _shifted_copies(dsh, dext)
        stripe = 32

        def stripe_body(rb, carry):
            row0 = pl.multiple_of(rb * stripe, stripe)
            dco = dco_ref[pl.ds(row0, stripe), :]
            dhg = jnp.zeros((stripe, D_CONV), F32)
            for t in range(CONV_WIDTH):
                off = HALO - (CONV_WIDTH - 1) + t
                back = CONV_WIDTH - 1 - t
                prod = dco * sh[off % 8, pl.ds((off // 8) * 8 + row0, stripe), :]
                part = prod[0:8, :]
                for r8 in range(1, stripe // 8):
                    part = part + prod[8 * r8:8 * r8 + 8, :]
                dw_s[t] += part
                dhg = dhg + w3_ref[t] * dsh[back % 8, pl.ds((back // 8) * 8 + row0, stripe), :]
            dhg_s[pl.ds(row0, stripe), :] = dhg
            return carry

        lax.fori_loop(0, TILE // stripe, stripe_body, 0)

        @pl.when(i == nt - 1)
        def _():
            dw_ref[...] = jnp.sum(dw_s[...], axis=1)

        dhg = jnp.where(_row_mask(i, (TILE, D_CONV)), dhg_s[...], 0.0)
        sg = _sigmoid(ug_ref[...].astype(F32))
        du_ref[...] = (dhg * sg).astype(BF16)
        dug_ref[...] = (dhg * u_ref[...].astype(F32) * sg * (1.0 - sg)).astype(BF16)

    rm = jax.ShapeDtypeStruct((lp, D_CONV), BF16)
    nxt = pl.BlockSpec((TILE, 512), lambda i: (jnp.minimum(i + 1, nt - 1), 0))
    ext_t = pltpu.VMEM((EXT_ROWS, D_CONV), F32)
    sh_t = pltpu.VMEM((8, SHIFT_ROWS, D_CONV), F32)
    return pl.pallas_call(
        body, name="conv_bwd_taps", grid=(nt,),
        in_specs=[_row_spec(512), nxt, _row_spec(512), _row_spec(512), _row_spec(512, shift=True),
                  _row_spec(512, shift=True), _full_spec((32, 1, 512))],
        out_specs=[_row_spec(512), _row_spec(512), _full_spec((32, 512))],
        out_shape=[rm, rm, jax.ShapeDtypeStruct((32, D_CONV), F32)],
        scratch_shapes=[ext_t, ext_t, sh_t, sh_t, pltpu.VMEM((TILE, D_CONV), F32), pltpu.VMEM((32, 8, D_CONV), F32)],
        compiler_params=_cp(48, ("arbitrary",)),
    )(dco, dco, u, ug, u, ug, conv_w.reshape(32, 1, D_CONV))


def _cumsum_bwd(dck, dcq4, fl, bf_pad, nt):
    lp = nt * TILE

    def body(dck_ref, dcq_ref, fl_ref, bf_ref, dfl_ref, dbf_ref, carry):
        i = pl.program_id(0)
        tile = nt - 1 - i

        @pl.when(i == 0)
        def _():
            carry[...] = jnp.zeros_like(carry)
            dbf_ref[...] = jnp.zeros_like(dbf_ref)

        dc = jnp.zeros((TILE, LANES), F32)
        for p in range(N_HEADS // 2):
            dq_rows = jnp.concatenate([dcq_ref[p, 0], jnp.zeros((LANES - 8, TILE), F32)], axis=0)
            both = dck_ref[p] + dq_rows.T
            dc = dc + (both if p == 0 else pltpu.roll(both, 2 * p, 1))
        r = lax.broadcasted_iota(jnp.int32, (TILE, TILE), 0)
        c = lax.broadcasted_iota(jnp.int32, (TILE, TILE), 1)
        triu = (c >= r).astype(F32)
        dlf = jnp.dot(triu, dc, precision=lax.Precision.HIGHEST, preferred_element_type=F32) + carry[...]
        carry[...] = dlf[0:1, :]
        z = fl_ref[...] + bf_ref[...]
        lane = lax.broadcasted_iota(jnp.int32, (TILE, LANES), 1)
        dfl = jnp.where(_row_mask(tile, (TILE, LANES)) & (lane < N_HEADS), dlf * _sigmoid(-z), 0.0)
        dfl_ref[...] = dfl.astype(BF16)
        dbf_ref[...] += jnp.sum(dfl, axis=0, keepdims=True)

    rev = lambda i: (nt - 1 - i, 0)
    return pl.pallas_call(
        body, name="cumsum_bwd", grid=(nt,),
        in_specs=[pl.BlockSpec((N_HEADS // 2, TILE, LANES), lambda i: (0, nt - 1 - i, 0)),
                  pl.BlockSpec((N_HEADS // 2, 1, 8, TILE), lambda i: (0, nt - 1 - i, 0, 0)),
                  pl.BlockSpec((TILE, LANES), rev), _full_spec((1, LANES))],
        out_specs=[pl.BlockSpec((TILE, LANES), rev), _full_spec((1, LANES))],
        out_shape=[jax.ShapeDtypeStruct((lp, LANES), BF16), jax.ShapeDtypeStruct((1, LANES), F32)],
        scratch_shapes=[pltpu.VMEM((1, LANES), F32)],
        compiler_params=_cp(32, ("arbitrary",)),
    )(dck, dcq4, fl, bf_pad)


def _dw_rowmajor(hb, secs, nt):
    n = len(secs)
    steps = 4
    rows = nt * TILE // steps
    long_rows = lambda cols: pl.BlockSpec((rows, cols), lambda i: (i, 0))

    def body(*refs):
        hb_ref, sec_refs, out_refs = refs[0], refs[1:1 + n], refs[1 + n:]
        i = pl.program_id(0)

        @pl.when(i == 0)
        def _():
            for o_ref in out_refs:
                o_ref[...] = jnp.zeros_like(o_ref)

        hb_t = hb_ref[...]
        for s_ref, o_ref in zip(sec_refs, out_refs):
            o_ref[...] += lax.dot_general(hb_t, s_ref[...], (((0,), (0,)), ((), ())), preferred_element_type=F32)

    return pl.pallas_call(
        body, name="dw_rowmajor", grid=(steps,),
        in_specs=[long_rows(D_MODEL)] + [long_rows(s.shape[1]) for s in secs],
        out_specs=[_full_spec((D_MODEL, s.shape[1])) for s in secs],
        out_shape=[jax.ShapeDtypeStruct((D_MODEL, s.shape[1]), F32) for s in secs],
        compiler_params=_cp(48, ("arbitrary",)),
    )(hb, *secs)


def _dw_transposed(hb, secs_t3, nt):
    n = len(secs_t3)

    def body(*refs):
        hb_ref, sec_refs, out_refs = refs[0], refs[1:1 + n], refs[1 + n:]
        i = pl.program_id(0)

        @pl.when(i == 0)
        def _():
            for o_ref in out_refs:
                o_ref[...] = jnp.zeros_like(o_ref)

        hb_t = hb_ref[...]
        for s_ref, o_ref in zip(sec_refs, out_refs):
            o_ref[...] += jnp.dot(s_ref[0], hb_t, preferred_element_type=F32)

    return pl.pallas_call(
        body, name="dw_transposed", grid=(nt,),
        in_specs=[_row_spec(D_MODEL)] + [_t3_spec(512) for _ in secs_t3],
        out_specs=[_full_spec((512, D_MODEL)) for _ in secs_t3],
        out_shape=[jax.ShapeDtypeStruct((512, D_MODEL), F32) for _ in secs_t3],
        compiler_params=_cp(40, ("arbitrary",)),
    )(hb, *secs_t3)


def _dh_bwd(secs, secs_t3, w_rm, w_t, dz, x, metapad, g_in, nt):
    n, m = len(secs), len(secs_t3)
    offs = OFF_GA_R + np.cumsum([0] + [s.shape[1] for s in secs])

    def body(*refs):
        sec_refs, t3_refs = refs[:n], refs[n:n + m]
        wrm_ref, wt_ref, dz_ref, x_ref, mp_ref, g_ref = refs[n + m:n + m + 6]
        dx_ref, dmeta_ref, dg_ref, db_ref = refs[n + m + 6:]
        i = pl.program_id(0)

        @pl.when(i == 0)
        def _():
            dg_ref[...] = jnp.zeros_like(dg_ref)
            db_ref[...] = jnp.zeros_like(db_ref)

        dh = ALPHA * dz_ref[...]
        for s_ref, lo, hi in zip(sec_refs, offs[:-1], offs[1:]):
            dh = dh + lax.dot_general(s_ref[...], wrm_ref[:, lo:hi], (((1,), (1,)), ((), ())),
                                      preferred_element_type=F32)
        for idx, t_ref in enumerate(t3_refs):
            dh = dh + lax.dot_general(t_ref[0], wt_ref[idx * 512:(idx + 1) * 512, :], (((0,), (0,)), ((), ())),
                                      preferred_element_type=F32)
        x0 = jnp.where(i == 0, mp_ref[...], x_ref[...])
        xhat, rstd = _ln_stats(x0)
        dg_ref[...] += jnp.sum(dh * xhat, axis=0, keepdims=True)
        db_ref[...] += jnp.sum(dh, axis=0, keepdims=True)
        dx = _ln_bwd(dh, xhat, rstd, g_ref[...])
        dx_ref[...] = dx

        @pl.when(i == 0)
        def _():
            dmeta_ref[...] = dx

    seq = (nt - 1) * TILE
    return pl.pallas_call(
        body, name="dh_bwd", grid=(nt,),
        in_specs=[_row_spec(s.shape[1]) for s in secs] + [_t3_spec(512) for _ in secs_t3]
        + [_full_spec(w_rm.shape), _full_spec(w_t.shape), _row_spec(D_MODEL), _row_spec(D_MODEL, shift=True),
           _full_spec((TILE, D_MODEL)), _full_spec((1, D_MODEL))],
        out_specs=[_row_spec(D_MODEL, shift=True), _full_spec((TILE, D_MODEL)), _full_spec((1, D_MODEL)),
                   _full_spec((1, D_MODEL))],
        out_shape=[jax.ShapeDtypeStruct((seq, D_MODEL), F32), jax.ShapeDtypeStruct((TILE, D_MODEL), F32),
                   jax.ShapeDtypeStruct((1, D_MODEL), F32), jax.ShapeDtypeStruct((1, D_MODEL), F32)],
        compiler_params=_cp(56, ("arbitrary",)),
    )(*secs, *secs_t3, w_rm, w_t, dz, x, metapad, g_in)


RB = 256
SMALL_ROWS = 48


def _repack_weights(all_in, all_small):
    n_cw = D_CONV // N_DEV

    def body(a_ref, s_ref, wr_ref, wt_ref, mp_ref, cw_ref):
        fill = jnp.zeros((512 - ROWS_IN, RB), BF16)
        full = jnp.concatenate([jnp.concatenate([a_ref[d], fill], axis=0).T[:, :SHARD_IN] for d in range(N_DEV)],
                               axis=1)
        qkv = full[:, :1536]
        wr_ref[:, :1536] = qkv
        wr_ref[:, 1536:OFF_F_R] = full[:, 1544:]
        wr_ref[:, OFF_F_R:] = jnp.concatenate([full[:, 1536:1544], jnp.zeros((RB, LANES - N_HEADS), BF16)], axis=1)
        wt_ref[...] = qkv.T

        @pl.when(pl.program_id(0) == 0)
        def _():
            mp_ref[0:PAD, :] = jnp.zeros((PAD, D_MODEL), F32)
            mp_ref[PAD:, :] = jnp.concatenate([s_ref[d, 0:N_META, :] for d in range(N_DEV)], axis=1)
            cw_ref[...] = jnp.concatenate([s_ref[d, N_META:, 0:n_cw] for d in range(N_DEV)], axis=1)

    return pl.pallas_call(
        body, name="repack_weights", grid=(D_MODEL // RB,),
        in_specs=[pl.BlockSpec((N_DEV, ROWS_IN, RB), lambda i: (0, 0, i)), _full_spec((N_DEV, SMALL_ROWS, LANES))],
        out_specs=[pl.BlockSpec((RB, W_COLS), lambda i: (i, 0)), pl.BlockSpec((1536, RB), lambda i: (0, i)),
                   _full_spec((TILE, D_MODEL)), _full_spec((32, D_CONV))],
        out_shape=[jax.ShapeDtypeStruct((D_MODEL, W_COLS), BF16), jax.ShapeDtypeStruct((1536, D_MODEL), BF16),
                   jax.ShapeDtypeStruct((TILE, D_MODEL), F32), jax.ShapeDtypeStruct((32, D_CONV), F32)],
        compiler_params=_cp(40, ("arbitrary",)),
    )(all_in, all_small)


def _unpack_dw_in(dw_rm, dw_t):
    def body(dga_ref, du_ref, dug_ref, dgc_ref, dfl_ref, dq_ref, dk_ref, dv_ref, out_ref, outb_ref):
        full = jnp.concatenate([dq_ref[...].T, dk_ref[...].T, dv_ref[...].T, dfl_ref[:, 0:N_HEADS], dga_ref[...],
                                du_ref[...], dug_ref[...], dgc_ref[...]], axis=1)
        pad = jnp.zeros((RB, 512 - SHARD_IN), F32)
        for d in range(N_DEV):
            blk = jnp.concatenate([full[:, SHARD_IN * d:SHARD_IN * (d + 1)], pad], axis=1).T[:ROWS_IN]
            out_ref[d] = blk
            outb_ref[d] = blk.astype(BF16)

    rm = pl.BlockSpec((RB, 512), lambda i: (i, 0))
    tr = pl.BlockSpec((512, RB), lambda i: (0, i))
    blocks = pl.BlockSpec((N_DEV, ROWS_IN, RB), lambda i: (0, 0, i))
    return pl.pallas_call(
        body, name="unpack_dw_in", grid=(D_MODEL // RB,),
        in_specs=[rm, rm, rm, rm, pl.BlockSpec((RB, LANES), lambda i: (i, 0)), tr, tr, tr],
        out_specs=[blocks, blocks],
        out_shape=[jax.ShapeDtypeStruct((N_DEV, ROWS_IN, D_MODEL), F32),
                   jax.ShapeDtypeStruct((N_DEV, ROWS_IN, D_MODEL), BF16)],
        compiler_params=_cp(48, ("arbitrary",)),
    )(*dw_rm, *dw_t)


def _local_step(x, target, metapad, cw, w_r, w_t, w_pw_full, w_out_full, ln_in_g, ln_in_b, b_f, conv_b, ln_conv_g,
                ln_conv_b, ln_out_g, ln_out_b):
    seq = x.shape[0]
    nt = seq // TILE + 1
    row = lambda a: a.reshape(1, -1).astype(F32)
    bf_pad = jnp.pad(row(b_f), ((0, 0), (0, LANES - N_HEADS)))
    g_in, b_in = row(ln_in_g), row(ln_in_b)
    g_cv, b_cv, c_b = row(ln_conv_g), row(ln_conv_b), row(conv_b)
    g_out, b_out = row(ln_out_g), row(ln_out_b)

    h, hb, qT3, kT3, vT3, k, v, ga, u, ug, gc, fl = _proj_fwd(x, metapad, g_in, b_in, w_r, nt)
    kx3 = _cumsum_fwd(fl, bf_pad, nt)
    oT3, o, lse4 = _attn_fwd(qT3, k, kx3, vT3, nt)
    co, hc, pw = _conv_fwd(u, ug, cw, c_b, g_cv, b_cv, w_pw_full, nt)
    y, dz, loss, dg_out, db_out = _out_fwd(o, ga, pw, gc, h, w_out_full, g_out, b_out, target, nt)
    doT3, dga, dgc, dw_out, dco, dw_pw, dg_cv, db_cv, dc_b = _out_bwd(dz, y, o, ga, pw, gc, w_out_full, hc, co,
                                                                      w_pw_full, g_cv, b_cv, nt)
    du, dug, dcw = _conv_bwd_taps(dco, u, ug, cw, nt)
    dqT3, dkT3, dvT3, dck, dcq4 = _attn_bwd(qT3, kT3, k, kx3, v, oT3, doT3, lse4, nt)
    dfl, dbf = _cumsum_bwd(dck, dcq4, fl, bf_pad, nt)
    secs = (dga, du, dug, dgc, dfl)
    secs_t3 = (dqT3, dkT3, dvT3)
    dw_rm = _dw_rowmajor(hb, secs, nt)
    dw_t = _dw_transposed(hb, secs_t3, nt)
    grad_x, dmetapad, dg_in, db_in = _dh_bwd(secs, secs_t3, w_r, w_t, dz, x, metapad, g_in, nt)
    pieces = dict(loss=loss, metapad=dmetapad, ln_in_g=dg_in, ln_in_b=db_in, w_in_rm=dw_rm, w_in_t=dw_t, b_f=dbf,
                  conv_w=dcw, conv_b=dc_b, ln_conv_g=dg_cv, ln_conv_b=db_cv, w_pw=dw_pw, w_out=dw_out,
                  ln_out_g=dg_out, ln_out_b=db_out)
    return grad_x, pieces


MESH = pl.DeviceIdType.MESH
ANY = pl.BlockSpec(memory_space=pl.ANY)


def _mesh_pos():
    return lax.axis_index("x"), lax.axis_index("y"), lax.axis_index("c")


GATHER_SEMS = 8


def _gather_body(x_refs, out_refs, send_sems, recv_sems, local_sems):
    n = len(x_refs)
    x, y, c = _mesh_pos()
    me, sibling = (x, y, c), (x, y, 1 - c)
    xn, yn, dg = (1 - x, y), (x, 1 - y), (1 - x, 1 - y)

    def slot(a, px, py, pc, half=None):
        blk = out_refs[a].at[4 * px + 2 * py + pc]
        if half is None:
            return blk
        top = -(-blk.shape[0] // (2 * BF16_ROWS)) * BF16_ROWS
        return blk.at[pl.ds(0, top)] if half == 0 else blk.at[pl.ds(top, blk.shape[0] - top)]

    def copy(a, k, block, to, src=None, half=None):
        return pltpu.make_async_remote_copy(
            src_ref=slot(a, *block, half) if src is None else src, dst_ref=slot(a, *block, half),
            send_sem=send_sems.at[GATHER_SEMS * a + k], recv_sem=recv_sems.at[GATHER_SEMS * a + k], device_id=to,
            device_id_type=MESH)

    arrays = range(n)
    mine = [pltpu.make_async_copy(x_refs[a], slot(a, *me), local_sems.at[a]) for a in arrays]
    for cp in mine:
        cp.start()
    sent = []
    for a in arrays:
        sent += [copy(a, 0, me, sibling, src=x_refs[a]), copy(a, 1, me, (*xn, c), src=x_refs[a]),
                 copy(a, 2, me, (*yn, c), src=x_refs[a])]
    for cp in sent:
        cp.start()

    def also(cp):
        cp.start()
        sent.append(cp)

    for a in arrays:
        copy(a, 2, (*yn, c), me).wait_recv()
        also(copy(a, 3, (*yn, c), (*xn, c), half=0))
        also(copy(a, 6, (*yn, c), sibling))
        copy(a, 1, (*xn, c), me).wait_recv()
        also(copy(a, 4, (*xn, c), (*yn, c), half=1))
        also(copy(a, 5, (*xn, c), sibling))
    for a in arrays:
        copy(a, 3, (*dg, c), me, half=0).wait_recv()
        copy(a, 4, (*dg, c), me, half=1).wait_recv()
        also(copy(a, 7, (*dg, c), sibling))
    for a in arrays:
        copy(a, 0, sibling, me).wait_recv()
        copy(a, 5, (*xn, 1 - c), me).wait_recv()
        copy(a, 6, (*yn, 1 - c), me).wait_recv()
        copy(a, 7, (*dg, 1 - c), me).wait_recv()
    for cp in sent:
        cp.wait_send()
    for cp in mine:
        cp.wait()


def _all_gather(blks, name):
    n = len(blks)

    def body(*refs):
        _gather_body(refs[:n], refs[n:2 * n], *refs[2 * n:])

    return pl.pallas_call(
        body, name=name, out_shape=[jax.ShapeDtypeStruct((N_DEV, *b.shape), b.dtype) for b in blks],
        in_specs=[ANY] * n, out_specs=[ANY] * n,
        scratch_shapes=[pltpu.SemaphoreType.DMA((GATHER_SEMS * n,)), pltpu.SemaphoreType.DMA((GATHER_SEMS * n,)),
                        pltpu.SemaphoreType.DMA((n,))],
    )(*blks)


def _exchange_sibling(g8s, small):
    n = len(g8s)

    def body(*refs):
        g_refs, s_ref, out_refs, a_ref = refs[:n], refs[n], refs[n + 1:2 * n + 1], refs[2 * n + 1]
        send_sems, recv_sems, a_send, a_recv, a_local = refs[2 * n + 2:]
        x, y, c = _mesh_pos()
        cps = [pltpu.make_async_remote_copy(
            src_ref=g_refs[a].at[2 * q + (1 - c)], dst_ref=out_refs[a].at[q], send_sem=send_sems.at[4 * a + q],
            recv_sem=recv_sems.at[4 * a + q], device_id=(x, y, 1 - c), device_id_type=MESH)
            for a in range(n) for q in range(4)]
        for cp in cps:
            cp.start()
        _gather_body([s_ref], [a_ref], a_send, a_recv, a_local)
        for cp in cps:
            cp.wait()

    outs = pl.pallas_call(
        body, name="rs_sibling",
        out_shape=[jax.ShapeDtypeStruct((4, *g.shape[1:]), g.dtype) for g in g8s]
        + [jax.ShapeDtypeStruct((N_DEV, *small.shape), small.dtype)],
        in_specs=[ANY] * (n + 1), out_specs=[ANY] * (n + 1),
        scratch_shapes=[pltpu.SemaphoreType.DMA((4 * n,)), pltpu.SemaphoreType.DMA((4 * n,)),
                        pltpu.SemaphoreType.DMA((GATHER_SEMS,)), pltpu.SemaphoreType.DMA((GATHER_SEMS,)),
                        pltpu.SemaphoreType.DMA((1,))],
    )(*g8s, small)
    return outs[:n], outs[n]


def _exchange_chips(p4s):
    n = len(p4s)

    def body(*refs):
        p_refs, out_refs, send_sems, recv_sems = refs[:n], refs[n:2 * n], refs[2 * n], refs[2 * n + 1]
        x, y, c = _mesh_pos()
        chips = [(1 - x, y), (x, 1 - y), (1 - x, 1 - y)]
        cps = [pltpu.make_async_remote_copy(
            src_ref=p_refs[a].at[2 * cx + cy], dst_ref=out_refs[a].at[k], send_sem=send_sems.at[3 * a + k],
            recv_sem=recv_sems.at[3 * a + k], device_id=(cx, cy, c), device_id_type=MESH)
            for k, (cx, cy) in enumerate(chips) for a in range(n)]
        for cp in cps:
            cp.start()
        for cp in cps:
            cp.wait()

    return pl.pallas_call(
        body, name="rs_chips", out_shape=[jax.ShapeDtypeStruct((3, *p.shape[1:]), p.dtype) for p in p4s],
        in_specs=[ANY] * n, out_specs=[ANY] * n,
        scratch_shapes=[pltpu.SemaphoreType.DMA((3 * n,)), pltpu.SemaphoreType.DMA((3 * n,))],
    )(*p4s)


def _rs_add_sibling(g8s, recvs, c_idx):
    n = len(g8s)

    def body(s_ref, *refs):
        g_refs, r_refs, p32_refs, pb_refs = (refs[k * n:(k + 1) * n] for k in range(4))
        for g_ref, r_ref, p32_ref, pb_ref in zip(g_refs, r_refs, p32_refs, pb_refs):
            p = g_ref[0] + r_ref[0].astype(F32)
            p32_ref[0] = p
            pb_ref[0] = p.astype(BF16)

    blk = lambda g: pl.BlockSpec((1, *g.shape[1:]), lambda q, s: (q, 0, 0))
    grid_spec = pltpu.PrefetchScalarGridSpec(
        num_scalar_prefetch=1, grid=(4,),
        in_specs=[pl.BlockSpec((1, *g.shape[1:]), lambda q, s: (2 * q + s[0], 0, 0)) for g in g8s]
        + [blk(g) for g in g8s],
        out_specs=[blk(g) for g in g8s] * 2)
    outs = pl.pallas_call(
        body, name="rs_add_sibling", grid_spec=grid_spec,
        out_shape=[jax.ShapeDtypeStruct((4, *g.shape[1:]), F32) for g in g8s]
        + [jax.ShapeDtypeStruct((4, *g.shape[1:]), BF16) for g in g8s],
        compiler_params=_cp(48, ("arbitrary",)),
    )(c_idx, *g8s, *recvs)
    return outs[:n], outs[n:]


def _rs_add_chips(p32s, recvs, q_idx):
    def body(s_ref, pin_ref, pout_ref, ppw_ref, rin_ref, rout_ref, rpw_ref, gin_ref, gout_ref, gpw_ref):
        def total(p_ref, r_ref):
            return ((p_ref[0] + r_ref[0].astype(F32)) + r_ref[1].astype(F32)) + r_ref[2].astype(F32)

        gin_ref[...] = total(pin_ref, rin_ref)[:SHARD_IN, :]
        gout_ref[0] = total(pout_ref, rout_ref)
        gpw_ref[0] = total(ppw_ref, rpw_ref)

    own = lambda p: pl.BlockSpec((1, *p.shape[1:]), lambda i, s: (s[0], 0, 0))
    whole = lambda shape: pl.BlockSpec(shape, lambda i, s: (0,) * len(shape))
    out_shapes = [(SHARD_IN, D_MODEL), (1, *p32s[1].shape[1:]), (1, *p32s[2].shape[1:])]
    grid_spec = pltpu.PrefetchScalarGridSpec(
        num_scalar_prefetch=1, grid=(1,),
        in_specs=[own(p) for p in p32s] + [whole(r.shape) for r in recvs],
        out_specs=[whole(s) for s in out_shapes])
    return pl.pallas_call(
        body, name="rs_add_chips", grid_spec=grid_spec,
        out_shape=[jax.ShapeDtypeStruct(s, F32) for s in out_shapes],
        compiler_params=_cp(48, ("arbitrary",)),
    )(q_idx, *p32s, *recvs)


SMALL_ROWS_G = 64
SMALL_LAYOUT = {
    "metapad": (0, N_META, D_MODEL), "conv_w": (16, 32, D_CONV), "ln_in_g": (48, 1, D_MODEL),
    "ln_in_b": (49, 1, D_MODEL), "b_f": (50, 1, LANES), "conv_b": (51, 1, D_CONV), "ln_conv_g": (52, 1, D_CONV),
    "ln_conv_b": (53, 1, D_CONV), "ln_out_g": (54, 1, D_MODEL), "ln_out_b": (55, 1, D_MODEL), "loss": (56, 1, LANES)}


def _pack_small(pieces):
    names = list(SMALL_LAYOUT)

    def body(*refs):
        out_ref = refs[-1]
        out_ref[...] = jnp.zeros_like(out_ref)
        for name, ref in zip(names, refs[:-1]):
            r0, nr, nl = SMALL_LAYOUT[name]
            src = ref[PAD:, :] if name == "metapad" else ref[...]
            out_ref[r0:r0 + nr, 0:nl] = src

    return pl.pallas_call(body, name="pack_small", out_shape=jax.ShapeDtypeStruct((SMALL_ROWS_G, D_MODEL), F32),
                          compiler_params=_cp(16))(*[pieces[n] for n in names])


def _sum_small(gathered):
    names = list(SMALL_LAYOUT)

    def body(a_ref, *out_refs):
        acc = a_ref[0]
        for d in range(1, N_DEV):
            acc = acc + a_ref[d]
        for name, ref in zip(names, out_refs):
            r0, nr, nl = SMALL_LAYOUT[name]
            ref[...] = acc[r0:r0 + nr, 0:nl]

    outs = pl.pallas_call(
        body, name="sum_small",
        out_shape=[jax.ShapeDtypeStruct(SMALL_LAYOUT[n][1:], F32) for n in names], compiler_params=_cp(16))(gathered)
    return dict(zip(names, outs))


def _adamw(ws, gs, ms, vs):
    n = len(ws)
    c1 = 1.0 - ADAM_B1 ** ADAM_STEP
    c2 = 1.0 - ADAM_B2 ** ADAM_STEP

    def body(*refs):
        w_refs, g_refs, m_refs, v_refs = (refs[k * n:(k + 1) * n] for k in range(4))
        d_refs, nm_refs, nv_refs = (refs[(4 + k) * n:(5 + k) * n] for k in range(3))
        for w_ref, g_ref, m_ref, v_ref, d_ref, nm_ref, nv_ref in zip(w_refs, g_refs, m_refs, v_refs, d_refs,
                                                                     nm_refs, nv_refs):
            g = g_ref[...]
            m = ADAM_B1 * m_ref[...] + (1.0 - ADAM_B1) * g
            v = ADAM_B2 * v_ref[...] + (1.0 - ADAM_B2) * (g * g)
            nm_ref[...] = m
            nv_ref[...] = v
            d_ref[...] = -ADAM_LR * ((m / c1) / (jnp.sqrt(v / c2) + ADAM_EPS) + ADAM_WD * w_ref[...])

    shapes = [jax.ShapeDtypeStruct(w.shape, F32) for w in ws]
    outs = pl.pallas_call(body, name="adamw", out_shape=shapes * 3, compiler_params=_cp(48))(*ws, *gs, *ms, *vs)
    return outs[:n], outs[n:2 * n], outs[2 * n:]


W_NAMES = ("meta", "ln_in_g", "ln_in_b", "w_in", "b_f", "conv_w", "conv_b", "ln_conv_g", "ln_conv_b", "w_pw",
           "w_out", "ln_out_g", "ln_out_b")


def kernel(x, meta, ln_in_g, ln_in_b, w_in, b_f, conv_w, conv_b, ln_conv_g, ln_conv_b, w_pw, w_out, ln_out_g, ln_out_b, loss_target, m_meta, m_ln_in_g, m_ln_in_b, m_w_in, m_b_f, m_conv_w, m_conv_b, m_ln_conv_g, m_ln_conv_b, m_w_pw, m_w_out, m_ln_out_g, m_ln_out_b, v_meta, v_ln_in_g, v_ln_in_b, v_w_in, v_b_f, v_conv_w, v_conv_b, v_ln_conv_g, v_ln_conv_b, v_w_pw, v_w_out, v_ln_out_g, v_ln_out_b):
    mx, my, mc = _mesh_pos()
    me = 4 * mx + 2 * my + mc
    n_meta_sh = D_MODEL // N_DEV
    n_cw_sh = D_CONV // N_DEV
    n_out_sh = D_MODEL // N_DEV
    n_pw_sh = D_CONV // N_DEV

    small_w = jnp.concatenate([meta, jnp.pad(conv_w[0], ((0, 1), (0, LANES - n_cw_sh)))], axis=0)
    all_in, all_out, all_pw, all_small = _all_gather(
        [jnp.pad(w_in[0].T, ((0, ROWS_IN - SHARD_IN), (0, 0))).astype(BF16), w_out[0].astype(BF16), w_pw[0].astype(BF16),
         small_w], "gather_weights")
    w_r, w_t, metapad, cw = _repack_weights(all_in, all_small)
    w_out_full = all_out.reshape(D_MODEL, D_MODEL)
    w_pw_full = all_pw.reshape(D_CONV, D_CONV)

    grad_x, pc = _local_step(x[0], loss_target[0], metapad, cw, w_r, w_t, w_pw_full, w_out_full, ln_in_g, ln_in_b,
                             b_f[0], conv_b[0], ln_conv_g[0], ln_conv_b[0], ln_out_g[0], ln_out_b[0])

    g_in8, g_in8_b = _unpack_dw_in(pc["w_in_rm"], pc["w_in_t"])
    g8s = [g_in8, pc["w_out"].reshape(N_DEV, n_out_sh, D_MODEL), pc["w_pw"].reshape(N_DEV, n_pw_sh, D_CONV)]
    from_sibling, all_small_g = _exchange_sibling([g_in8_b] + g8s[1:], _pack_small(pc))
    p32s, pbs = _rs_add_sibling(g8s, from_sibling, jnp.reshape(mc, (1,)).astype(jnp.int32))
    from_chips = _exchange_chips(pbs)
    g_w_in, g_w_out, g_w_pw = _rs_add_chips(p32s, from_chips, jnp.reshape(2 * mx + my, (1,)).astype(jnp.int32))

    sm = _sum_small(all_small_g)
    grads = {
        "meta": lax.dynamic_slice_in_dim(sm["metapad"], me * n_meta_sh, n_meta_sh, axis=1),
        "ln_in_g": sm["ln_in_g"].reshape(D_MODEL), "ln_in_b": sm["ln_in_b"].reshape(D_MODEL), "w_in": g_w_in.T[None],
        "b_f": sm["b_f"][:, :N_HEADS],
        "conv_w": lax.dynamic_slice_in_dim(sm["conv_w"], me * n_cw_sh, n_cw_sh, axis=1)[None, :CONV_WIDTH],
        "conv_b": sm["conv_b"], "ln_conv_g": sm["ln_conv_g"], "ln_conv_b": sm["ln_conv_b"], "w_pw": g_w_pw,
        "w_out": g_w_out, "ln_out_g": sm["ln_out_g"], "ln_out_b": sm["ln_out_b"]}
    loss_all = sm["loss"][0, 0]

    weights = dict(meta=meta, ln_in_g=ln_in_g, ln_in_b=ln_in_b, w_in=w_in, b_f=b_f, conv_w=conv_w, conv_b=conv_b,
                   ln_conv_g=ln_conv_g, ln_conv_b=ln_conv_b, w_pw=w_pw, w_out=w_out, ln_out_g=ln_out_g,
                   ln_out_b=ln_out_b)
    moms = dict(meta=m_meta, ln_in_g=m_ln_in_g, ln_in_b=m_ln_in_b, w_in=m_w_in, b_f=m_b_f, conv_w=m_conv_w,
                conv_b=m_conv_b, ln_conv_g=m_ln_conv_g, ln_conv_b=m_ln_conv_b, w_pw=m_w_pw, w_out=m_w_out,
                ln_out_g=m_ln_out_g, ln_out_b=m_ln_out_b)
    vels = dict(meta=v_meta, ln_in_g=v_ln_in_g, ln_in_b=v_ln_in_b, w_in=v_w_in, b_f=v_b_f, conv_w=v_conv_w,
                conv_b=v_conv_b, ln_conv_g=v_ln_conv_g, ln_conv_b=v_ln_conv_b, w_pw=v_w_pw, w_out=v_w_out,
                ln_out_g=v_ln_out_g, ln_out_b=v_ln_out_b)

    def to_kernel(name, a):
        if name == "w_in":
            return a[0].T
        return a.reshape(1, -1) if a.ndim == 1 else a

    def from_kernel(name, a):
        return a.T[None] if name == "w_in" else a.reshape(weights[name].shape)

    upd = _adamw(*[[to_kernel(n, d[n]) for n in W_NAMES] for d in (weights, grads, moms, vels)])
    deltas, new_m, new_v = ([from_kernel(n, a) for n, a in zip(W_NAMES, part)] for part in upd)
    return (loss_all, grad_x[None], *[grads[n] for n in W_NAMES], *deltas, *new_m, *new_v)
```

```python
import jax
import jax.numpy as jnp
import numpy as np
from jax import lax
from jax.experimental import pallas as pl
from jax.experimental.pallas import tpu as pltpu

F32 = jnp.float32
BF16 = jnp.bfloat16

D_MODEL = 1024
D_ATTN = 512
D_CONV = 512
N_HEADS = 8
HEAD_DIM = 64
N_META = 16
CONV_WIDTH = 31
LN_EPS = 1e-5
ALPHA = 2.0 ** 0.25
SCALE = HEAD_DIM ** -0.5
LOG2E = 1.4426950408889634
ADAM_LR, ADAM_B1, ADAM_B2, ADAM_EPS, ADAM_WD, ADAM_STEP = 0.001, 0.9, 0.999, 1e-08, 0.01, 10

N_DEV = 8
D_IN = 3592
SHARD_IN = D_IN // N_DEV
BF16_ROWS = 16
ROWS_IN = -(-SHARD_IN // BF16_ROWS) * BF16_ROWS
TILE = 256
PAD = TILE - N_META
HALO = 32
SHIFT_ROWS = TILE + HALO
EXT_ROWS = SHIFT_ROWS + 8
NEG = -1e30
LANES = 128
W_COLS = 7 * 512 + LANES
OFF_GA_R, OFF_F_R = 1536, 3584
MIB = 1024 * 1024


def _cp(vmem_mib, sem=None):
    kw = dict(vmem_limit_bytes=vmem_mib * MIB)
    if sem is not None:
        kw["dimension_semantics"] = sem
    return pltpu.CompilerParams(**kw)


def _sigmoid(x):
    return 1.0 / (1.0 + jnp.exp(-x))


def _silu_and_grad(x):
    s = _sigmoid(x)
    return x * s, s * (1.0 + x * (1.0 - s))


def _ln_stats(x):
    mu = jnp.mean(x, axis=-1, keepdims=True)
    xc = x - mu
    var = jnp.mean(xc * xc, axis=-1, keepdims=True)
    rstd = lax.rsqrt(var + LN_EPS)
    return xc * rstd, rstd


def _ln_bwd(dy, xhat, rstd, g):
    dxh = dy * g
    m1 = jnp.mean(dxh, axis=-1, keepdims=True)
    m2 = jnp.mean(dxh * xhat, axis=-1, keepdims=True)
    return rstd * (dxh - m1 - xhat * m2)


def _row_spec(cols, shift=False):
    if shift:
        return pl.BlockSpec((TILE, cols), lambda i: (jnp.maximum(i - 1, 0), 0))
    return pl.BlockSpec((TILE, cols), lambda i: (i, 0))


def _full_spec(shape):
    nd = len(shape)
    return pl.BlockSpec(shape, lambda i: (0,) * nd)


def _t3_spec(ch):
    return pl.BlockSpec((1, ch, TILE), lambda i: (i, 0, 0))


def _proj_fwd(x, metapad, g_in, b_in, w_r, nt):
    lp = nt * TILE

    def body(x_ref, mp_ref, g_ref, b_ref, w_ref, h_ref, hb_ref, qT_ref, kT_ref, vT_ref, k_ref, v_ref,
             ga_ref, u_ref, ug_ref, gc_ref, fl_ref):
        i = pl.program_id(0)
        x0 = jnp.where(i == 0, mp_ref[...], x_ref[...])
        xhat, _ = _ln_stats(x0)
        h = xhat * g_ref[...] + b_ref[...]
        h_ref[...] = h
        hb = h.astype(BF16)
        hb_ref[...] = hb

        def sec(off, n=512):
            return jnp.dot(hb, w_ref[:, off:off + n], preferred_element_type=F32)

        qT_ref[0] = (sec(0) * (SCALE * LOG2E)).T.astype(BF16)
        k = sec(512)
        kT_ref[0] = k.T.astype(BF16)
        k_ref[...] = k.astype(BF16)
        v = sec(1024)
        vT_ref[0] = v.T.astype(BF16)
        v_ref[...] = v.astype(BF16)
        ga_ref[...] = sec(OFF_GA_R).astype(BF16)
        u_ref[...] = sec(OFF_GA_R + 512).astype(BF16)
        ug_ref[...] = sec(OFF_GA_R + 1024).astype(BF16)
        gc_ref[...] = sec(OFF_GA_R + 1536).astype(BF16)
        fl_ref[...] = sec(OFF_F_R, LANES)

    t3 = jax.ShapeDtypeStruct((nt, 512, TILE), BF16)
    rm = lambda dt: jax.ShapeDtypeStruct((lp, 512), dt)
    return pl.pallas_call(
        body, name="proj_fwd", grid=(nt,),
        in_specs=[_row_spec(D_MODEL, shift=True), _full_spec((TILE, D_MODEL)), _full_spec((1, D_MODEL)),
                  _full_spec((1, D_MODEL)), _full_spec((D_MODEL, W_COLS))],
        out_specs=[_row_spec(D_MODEL), _row_spec(D_MODEL), _t3_spec(512), _t3_spec(512), _t3_spec(512),
                   _row_spec(512), _row_spec(512),
                   _row_spec(512), _row_spec(512), _row_spec(512), _row_spec(512), _row_spec(LANES)],
        out_shape=[jax.ShapeDtypeStruct((lp, D_MODEL), F32), jax.ShapeDtypeStruct((lp, D_MODEL), BF16),
                   t3, t3, t3, rm(BF16), rm(BF16),
                   rm(BF16), rm(BF16), rm(BF16), rm(BF16), jax.ShapeDtypeStruct((lp, LANES), F32)],
        compiler_params=_cp(56, ("arbitrary",)),
    )(x, metapad, g_in, b_in, w_r)


def _row_mask(i, shape):
    r = lax.broadcasted_iota(jnp.int32, shape, 0)
    return (r >= PAD) | (i > 0)


def _cumsum_fwd(fl, bf_pad, nt):
    lp = nt * TILE

    def body(fl_ref, bf_ref, kx_ref, carry):
        i = pl.program_id(0)

        @pl.when(i == 0)
        def _():
            carry[...] = jnp.zeros_like(carry)

        z = fl_ref[...] + bf_ref[...]
        lf = jnp.minimum(z, 0.0) - jnp.log(1.0 + jnp.exp(-jnp.abs(z)))
        lane = lax.broadcasted_iota(jnp.int32, (TILE, LANES), 1)
        real = _row_mask(i, (TILE, LANES))
        lf = jnp.where(real & (lane < N_HEADS), lf, 0.0)
        r = lax.broadcasted_iota(jnp.int32, (TILE, TILE), 0)
        c = lax.broadcasted_iota(jnp.int32, (TILE, TILE), 1)
        tril = (c <= r).astype(F32)
        cs = jnp.dot(tril, lf, precision=lax.Precision.HIGHEST, preferred_element_type=F32) + carry[...]
        carry[...] = cs[TILE - 1:TILE, :]
        bias = jnp.where(real, cs * (-LOG2E), NEG)
        hi = bias.astype(BF16).astype(F32)
        mid = (bias - hi).astype(BF16).astype(F32)
        lo = (bias - hi - mid).astype(BF16).astype(F32)
        for p in range(N_HEADS // 2):
            out = jnp.zeros((TILE, LANES), F32)
            for hh in range(2):
                for part, piece in enumerate((hi, mid, lo)):
                    dst, src = 3 * hh + part, 2 * p + hh
                    moved = piece if dst == src else pltpu.roll(piece, (dst - src) % LANES, 1)
                    out = jnp.where(lane == dst, moved, out)
            kx_ref[p] = out.astype(BF16)

    return pl.pallas_call(
        body, name="cumsum_fwd", grid=(nt,),
        in_specs=[_row_spec(LANES), _full_spec((1, LANES))],
        out_specs=pl.BlockSpec((N_HEADS // 2, TILE, LANES), lambda i: (0, i, 0)),
        out_shape=jax.ShapeDtypeStruct((N_HEADS // 2, lp, LANES), BF16),
        scratch_shapes=[pltpu.VMEM((1, LANES), F32)],
        compiler_params=_cp(32, ("arbitrary",)),
    )(fl, bf_pad)


def _head_rows(blk, hh):
    r = lax.broadcasted_iota(jnp.int32, blk.shape, 0)
    return jnp.where((r >= hh * HEAD_DIM) & (r < (hh + 1) * HEAD_DIM), blk, jnp.zeros_like(blk))


def _two_heads(blk):
    return jnp.concatenate([_head_rows(blk, 0), _head_rows(blk, 1)], axis=1)


def _bias_rows():
    r = lax.broadcasted_iota(jnp.int32, (LANES, 2 * TILE), 0)
    c = lax.broadcasted_iota(jnp.int32, (LANES, 2 * TILE), 1)
    return jnp.where(((r < 3) & (c < TILE)) | ((r >= 3) & (r < 6) & (c >= TILE)), 1.0, 0.0).astype(BF16)


def _diag_mask(s):
    kpos = lax.broadcasted_iota(jnp.int32, (TILE, TILE), 0)
    qpos = lax.broadcasted_iota(jnp.int32, (TILE, TILE), 1)
    return jnp.where(kpos <= qpos, s, NEG)


def _stream(n, first, nxt, scores, update, unroll=8):
    if n == 0:
        return
    scores(first, 0)

    def step(_, idx):
        for _u in range(unroll):
            idx_b = nxt(idx)
            scores(idx_b, 1)
            update(idx, 0)
            idx = nxt(idx_b)
            scores(idx, 0)
            update(idx_b, 1)
        return idx

    steps = (n - 1) // (2 * unroll)
    idx = lax.fori_loop(0, steps, step, first)
    left = n - 2 * unroll * steps
    for r in range(left - 1):
        idx_b = nxt(idx)
        scores(idx_b, (r + 1) % 2)
        update(idx, r % 2)
        idx = idx_b
    update(idx, (left - 1) % 2)


def _next_below_diagonal(idx):
    i, j = idx
    wrap = j + 1 >= i
    return jnp.where(wrap, i + 1, i), jnp.where(wrap, 0, j + 1)


def _tile_rows(t):
    return pl.ds(pl.multiple_of(t * TILE, TILE), TILE)


def _two_streams(nt):
    load, group = [0, 0], {}
    for i in sorted(range(1, nt), reverse=True):
        g = 0 if load[0] <= load[1] else 1
        group[i] = g
        load[g] += i
    rows = [[(i, i, j) for i in range(1, nt) if group[i] == g for j in range(i)] for g in range(2)]
    length = max(len(r) for r in rows)
    rows = [r + [(nt, 0, 0)] * (length - len(r)) for r in rows]
    return group, np.asarray(rows, np.int32).reshape(2, -1), length


def _attn_fwd(qT3, k, kx3, vT3, nt):
    lp = nt * TILE
    npair = N_HEADS // 2
    group, table, n_stream = _two_streams(nt)

    def body(tab_ref, qT_ref, k_ref, kx_ref, vT_ref, oT_ref, o_ref, lse_ref, sbuf, m_0, l_0, acc_0, m_1, l_1, acc_1):
        ones = _bias_rows()
        states = ((m_0, l_0, acc_0), (m_1, l_1, acc_1))

        def scores(i, j, slot):
            qcat = jnp.concatenate([_two_heads(qT_ref[i]), ones], axis=0)
            kext = jnp.concatenate([k_ref[_tile_rows(j), :], kx_ref[0, _tile_rows(j), :]], axis=1)
            sbuf[slot] = jnp.dot(kext, qcat, preferred_element_type=F32)

        def update(st, j, slot, state, diag):
            m_s, l_s, acc_s = state
            for hh in range(2):
                s = sbuf[slot, :, hh * TILE:(hh + 1) * TILE]
                vj = vT_ref[j, hh * HEAD_DIM:(hh + 1) * HEAD_DIM, :]
                if diag:
                    s = _diag_mask(s)
                    m_new = jnp.max(s, axis=0, keepdims=True)
                    p = jnp.exp2(s - m_new)
                    l_s[st, hh] = jnp.sum(p, axis=0, keepdims=True)
                    acc_s[st, hh] = jnp.dot(vj, p.astype(BF16), preferred_element_type=F32)
                else:
                    m_prev = m_s[st, hh]
                    m_new = jnp.maximum(m_prev, jnp.max(s, axis=0, keepdims=True))
                    a = jnp.exp2(m_prev - m_new)
                    p = jnp.exp2(s - m_new)
                    l_s[st, hh] = a * l_s[st, hh] + jnp.sum(p, axis=0, keepdims=True)
                    acc_s[st, hh] = a * acc_s[st, hh] + jnp.dot(vj, p.astype(BF16), preferred_element_type=F32)
                m_s[st, hh] = m_new

        _stream(nt, jnp.int32(0), lambda t: t + 1, lambda t, slot: scores(t, t, slot),
                lambda t, slot: update(t, t, slot, states[0], True))
        for dst, src in zip(states[1], states[0]):
            dst[0:nt] = src[0:nt]
        for m_s, l_s, acc_s in states:
            m_s[nt] = jnp.full(m_s.shape[1:], NEG, F32)
            l_s[nt] = jnp.zeros(l_s.shape[1:], F32)
            acc_s[nt] = jnp.zeros(acc_s.shape[1:], F32)

        def entry(g, t):
            return tab_ref[g, 3 * t], tab_ref[g, 3 * t + 1], tab_ref[g, 3 * t + 2]

        def scores2(t, slot):
            for g in range(2):
                _, qi, kj = entry(g, t)
                scores(qi, kj, 2 * g + slot)

        def update2(t, slot):
            for g in range(2):
                st, _, kj = entry(g, t)
                update(st, kj, 2 * g + slot, states[g], False)

        _stream(n_stream, jnp.int32(0), lambda t: t + 1, scores2, update2)

        for i in range(nt):
            m_s, l_s, acc_s = states[group.get(i, 0)]
            for hh in range(2):
                l = l_s[i, hh]
                oT_ref[i, hh * HEAD_DIM:(hh + 1) * HEAD_DIM, :] = acc_s[i, hh] / l
                lse_ref[0, i, hh:hh + 1, :] = m_s[i, hh] + jnp.log(l) * LOG2E
            o_ref[i * TILE:(i + 1) * TILE, :] = oT_ref[i].T.astype(BF16)

    blk_t = pl.BlockSpec((nt, LANES, TILE), lambda p, tab: (0, p, 0))
    blk_rm = pl.BlockSpec((lp, LANES), lambda p, tab: (0, p))
    blk_px = pl.BlockSpec((1, lp, LANES), lambda p, tab: (p, 0, 0))
    blk_st = pl.BlockSpec((1, nt, 8, TILE), lambda p, tab: (p, 0, 0, 0))
    state = [pltpu.VMEM((nt + 1, 2, 1, TILE), F32), pltpu.VMEM((nt + 1, 2, 1, TILE), F32),
             pltpu.VMEM((nt + 1, 2, HEAD_DIM, TILE), F32)]
    grid_spec = pltpu.PrefetchScalarGridSpec(
        num_scalar_prefetch=1, grid=(npair,), in_specs=[blk_t, blk_rm, blk_px, blk_t],
        out_specs=[blk_t, blk_rm, blk_st], scratch_shapes=[pltpu.VMEM((4, TILE, 2 * TILE), F32)] + state + state)
    return pl.pallas_call(
        body, name="attn_fwd", grid_spec=grid_spec,
        out_shape=[jax.ShapeDtypeStruct((nt, D_ATTN, TILE), F32),
                   jax.ShapeDtypeStruct((lp, D_ATTN), BF16),
                   jax.ShapeDtypeStruct((npair, nt, 8, TILE), F32)],
        compiler_params=_cp(60, ("arbitrary",)),
    )(jnp.asarray(table), qT3, k, kx3, vT3)


def _attn_bwd(qT3, kT3, k, kx3, v, oT3, doT3, lse4, nt):
    lp = nt * TILE
    npair = N_HEADS // 2

    def body(qT_ref, kT_ref, k_ref, kx_ref, v_ref, oT_ref, doT_ref, lse_ref,
             dqT_ref, dkT_ref, dvT_ref, dck_ref, dcq_ref, sbuf, dpbuf, dq_s, dk_s, dv_s, dc_s, tp_s, tds_s):
        ones = _bias_rows()

        def scores(idx, slot):
            i, j = idx
            qcat = jnp.concatenate([_two_heads(qT_ref[i]), ones], axis=0)
            kext = jnp.concatenate([k_ref[_tile_rows(j), :], kx_ref[0, _tile_rows(j), :]], axis=1)
            sbuf[slot] = jnp.dot(kext, qcat, preferred_element_type=F32)
            dpbuf[slot] = jnp.dot(v_ref[_tile_rows(j), :], _two_heads(doT_ref[i]), preferred_element_type=F32)

        def update(idx, slot, diag):
            i, j = idx
            for hh in range(2):
                hs = slice(hh * HEAD_DIM, (hh + 1) * HEAD_DIM)
                s = sbuf[slot, :, hh * TILE:(hh + 1) * TILE]
                if diag:
                    s = _diag_mask(s)
                p = jnp.exp2(s - lse_ref[0, i, hh:hh + 1, :])
                doh = doT_ref[i, hs, :]
                delta = jnp.sum(doh.astype(F32) * oT_ref[i, hs, :], axis=0, keepdims=True)
                ds = p * (dpbuf[slot, :, hh * TILE:(hh + 1) * TILE] - delta)
                dsb = ds.astype(BF16)
                tp_s[hh] = p.astype(BF16).T
                tds_s[hh] = dsb.T
                dv = jnp.dot(doh, tp_s[hh], preferred_element_type=F32)
                dk = jnp.dot(qT_ref[i, hs, :], tds_s[hh], preferred_element_type=F32)
                dq = jnp.dot(kT_ref[j, hs, :], dsb, preferred_element_type=F32)
                dc = ds[:, :LANES] + ds[:, LANES:]
                dcq = jnp.sum(ds, axis=0, keepdims=True)
                if diag:
                    dv_s[j, hh] = dv
                    dk_s[j, hh] = dk
                    dc_s[j, hh] = dc
                    dq_s[i, hs, :] = dq
                    dcq_ref[0, i, hh:hh + 1, :] = dcq
                else:
                    dv_s[j, hh] += dv
                    dk_s[j, hh] += dk
                    dc_s[j, hh] += dc
                    dq_s[i, hs, :] += dq
                    dcq_ref[0, i, hh:hh + 1, :] += dcq

        dcq_ref[...] = jnp.zeros_like(dcq_ref)
        zero = jnp.int32(0)
        _stream(nt, (zero, zero), lambda idx: (idx[0] + 1, idx[1] + 1), scores,
                lambda idx, slot: update(idx, slot, True))
        _stream(nt * (nt - 1) // 2, (zero + 1, zero), _next_below_diagonal, scores,
                lambda idx, slot: update(idx, slot, False))

        lane = lax.broadcasted_iota(jnp.int32, (TILE, LANES), 1)

        def finish(t, carry):
            dck = jnp.zeros((TILE, LANES), F32)
            for hh in range(2):
                hs = slice(hh * HEAD_DIM, (hh + 1) * HEAD_DIM)
                dkT_ref[t, hs, :] = (dk_s[t, hh] * (1.0 / LOG2E)).astype(BF16)
                dvT_ref[t, hs, :] = dv_s[t, hh].astype(BF16)
                dck = jnp.where(lane == hh, -jnp.sum(dc_s[t, hh], axis=1, keepdims=True), dck)
            dck_ref[0, _tile_rows(t), :] = dck
            dqT_ref[t] = (dq_s[t] * SCALE).astype(BF16)
            return carry

        lax.fori_loop(0, nt, finish, 0)

    blk_t = pl.BlockSpec((nt, LANES, TILE), lambda p: (0, p, 0))
    blk_rm = pl.BlockSpec((lp, LANES), lambda p: (0, p))
    blk_px = pl.BlockSpec((1, lp, LANES), lambda p: (p, 0, 0))
    blk_st = pl.BlockSpec((1, nt, 8, TILE), lambda p: (p, 0, 0, 0))
    t3 = jax.ShapeDtypeStruct((nt, D_ATTN, TILE), BF16)
    return pl.pallas_call(
        body, name="attn_bwd", grid=(npair,),
        in_specs=[blk_t, blk_t, blk_rm, blk_px, blk_rm, blk_t, blk_t, blk_st],
        out_specs=[blk_t, blk_t, blk_t, blk_px, blk_st],
        out_shape=[t3, t3, t3, jax.ShapeDtypeStruct((npair, lp, LANES), F32),
                   jax.ShapeDtypeStruct((npair, nt, 8, TILE), F32)],
        scratch_shapes=[pltpu.VMEM((2, TILE, 2 * TILE), F32), pltpu.VMEM((2, TILE, 2 * TILE), F32),
                        pltpu.VMEM((nt, LANES, TILE), F32), pltpu.VMEM((nt, 2, HEAD_DIM, TILE), F32),
                        pltpu.VMEM((nt, 2, HEAD_DIM, TILE), F32), pltpu.VMEM((nt, 2, TILE, LANES), F32),
                        pltpu.VMEM((2, TILE, TILE), BF16), pltpu.VMEM((2, TILE, TILE), BF16)],
        compiler_params=_cp(60, ("arbitrary",)),
    )(qT3, kT3, k, kx3, v, oT3, doT3, lse4)


def _glu(u, ug, i):
    return jnp.where(_row_mask(i, u.shape), u.astype(F32) * _sigmoid(ug.astype(F32)), 0.0)


def _shifted_copies(dst, src):
    for ph in range(8):
        dst[ph] = src[ph:ph + SHIFT_ROWS, :]


def _tap_window(sh, off, lanes, row0=0, rows=TILE):
    base = (off // 8) * 8 + row0
    return sh[off % 8, base:base + rows, lanes]


def _conv_fwd(u, ug, conv_w, conv_b, g, b, w_pw, nt):
    lp = nt * TILE

    def body(u_ref, ug_ref, up_ref, ugp_ref, w_ref, cb_ref, g_ref, b_ref, wpw_ref,
             co_ref, hc_ref, pw_ref, ext, sh):
        i = pl.program_id(0)
        prev = _glu(up_ref[...], ugp_ref[...], i - 1)
        ext[0:HALO, :] = jnp.where(i > 0, prev[TILE - HALO:, :], 0.0)
        ext[HALO:HALO + TILE, :] = _glu(u_ref[...], ug_ref[...], i)
        ext[HALO + TILE:, :] = jnp.zeros((8, D_CONV), F32)
        _shifted_copies(sh, ext)
        for lb in range(D_CONV // LANES):
            lanes = slice(lb * LANES, (lb + 1) * LANES)
            acc = jnp.zeros((TILE, LANES), F32) + cb_ref[:, lanes]
            for t in range(CONV_WIDTH):
                off = HALO - (CONV_WIDTH - 1) + t
                acc = acc + w_ref[t:t + 1, lanes] * _tap_window(sh, off, lanes)
            co_ref[:, lanes] = acc
        xhat, _ = _ln_stats(co_ref[...])
        a, _ = _silu_and_grad(xhat * g_ref[...] + b_ref[...])
        hc = a.astype(BF16)
        hc_ref[...] = hc
        pw_ref[...] = jnp.dot(hc, wpw_ref[...], preferred_element_type=F32).astype(BF16)

    rm = lambda dt: jax.ShapeDtypeStruct((lp, D_CONV), dt)
    return pl.pallas_call(
        body, name="conv_fwd", grid=(nt,),
        in_specs=[_row_spec(512), _row_spec(512), _row_spec(512, shift=True), _row_spec(512, shift=True),
                  _full_spec((32, 512)), _full_spec((1, 512)), _full_spec((1, 512)), _full_spec((1, 512)),
                  _full_spec((512, 512))],
        out_specs=[_row_spec(512), _row_spec(512), _row_spec(512)],
        out_shape=[rm(F32), rm(BF16), rm(BF16)],
        scratch_shapes=[pltpu.VMEM((EXT_ROWS, D_CONV), F32), pltpu.VMEM((8, SHIFT_ROWS, D_CONV), F32)],
        compiler_params=_cp(40, ("arbitrary",)),
    )(u, ug, u, ug, conv_w, conv_b, g, b, w_pw)


RING = 3
_ring_scratch = lambda n: [pltpu.VMEM((RING, TILE, D_MODEL), F32)] * n + [pltpu.SemaphoreType.DMA((n, RING))]
_own_rows = lambda step: step * TILE
_shifted_rows = lambda step: jnp.maximum(step - 1, 0) * TILE


def _ring_step(i, nt, streams, sems):
    def copies(step):
        slot = lax.rem(step, RING)
        return [pltpu.make_async_copy(src.at[pl.ds(pl.multiple_of(rows(step), TILE), TILE)], buf.at[slot], sems.at[k, slot])
                for k, (src, rows, buf) in enumerate(streams)]

    def start(step):
        for c in copies(step):
            c.start()

    @pl.when(i == 0)
    def _():
        for step in range(RING - 1):
            start(step)

    @pl.when(i + (RING - 1) < nt)
    def _():
        start(i + (RING - 1))

    for c in copies(i):
        c.wait()
    return [buf.at[lax.rem(i, RING)] for _, _, buf in streams]


def _out_fwd(o, ga, pw, gc, h, w_out, g_out, b_out, target, nt):
    lp = nt * TILE

    def body(o_ref, ga_ref, pw_ref, gc_ref, h_any, wo_ref, go_ref, bo_ref, t_any,
             y_ref, dz_ref, loss_ref, dgo_ref, dbo_ref, hbuf, tbuf, sems):
        i = pl.program_id(0)

        h_ref, t_ref = _ring_step(i, nt, [(h_any, _own_rows, hbuf), (t_any, _shifted_rows, tbuf)], sems)

        @pl.when(i == 0)
        def _():
            loss_ref[...] = jnp.zeros_like(loss_ref)
            dgo_ref[...] = jnp.zeros_like(dgo_ref)
            dbo_ref[...] = jnp.zeros_like(dbo_ref)

        ya, _ = _silu_and_grad(ga_ref[...].astype(F32))
        yc, _ = _silu_and_grad(gc_ref[...].astype(F32))
        ya = (o_ref[...].astype(F32) * ya).astype(BF16)
        yc = (pw_ref[...].astype(F32) * yc).astype(BF16)
        y_ref[:, :D_ATTN] = ya
        y_ref[:, D_ATTN:] = yc
        z = ALPHA * h_ref[...] + jnp.dot(ya, wo_ref[:D_ATTN, :], preferred_element_type=F32) \
            + jnp.dot(yc, wo_ref[D_ATTN:, :], preferred_element_type=F32)
        zhat, rstd = _ln_stats(z)
        out = zhat * go_ref[...] + bo_ref[...]
        live = (i > 0).astype(F32)
        err = (out - t_ref[...]) * live
        dout = err * (1.0 / D_MODEL)
        loss_ref[...] += 0.5 * jnp.sum(jnp.sum(err * dout, axis=0, keepdims=True), axis=1, keepdims=True)
        dgo_ref[...] += jnp.sum(dout * zhat, axis=0, keepdims=True)
        dbo_ref[...] += jnp.sum(dout, axis=0, keepdims=True)
        dz_ref[...] = _ln_bwd(dout, zhat, rstd, go_ref[...])

    return pl.pallas_call(
        body, name="out_fwd", grid=(nt,),
        in_specs=[_row_spec(512), _row_spec(512), _row_spec(512), _row_spec(512),
                  pl.BlockSpec(memory_space=pl.ANY), _full_spec((D_MODEL, D_MODEL)), _full_spec((1, D_MODEL)),
                  _full_spec((1, D_MODEL)), pl.BlockSpec(memory_space=pl.ANY)],
        out_specs=[_row_spec(D_MODEL), _row_spec(D_MODEL), _full_spec((1, LANES)), _full_spec((1, D_MODEL)),
                   _full_spec((1, D_MODEL))],
        out_shape=[jax.ShapeDtypeStruct((lp, D_MODEL), BF16), jax.ShapeDtypeStruct((lp, D_MODEL), F32),
                   jax.ShapeDtypeStruct((1, LANES), F32), jax.ShapeDtypeStruct((1, D_MODEL), F32),
                   jax.ShapeDtypeStruct((1, D_MODEL), F32)],
        scratch_shapes=_ring_scratch(2),
        compiler_params=_cp(40, ("arbitrary",)),
    )(o, ga, pw, gc, h, w_out, g_out, b_out, target)


def _out_bwd(dz, y, o, ga, pw, gc, w_out, hc, co, w_pw, g_cv, b_cv, nt):
    lp = nt * TILE

    def body(dz_ref, y_ref, o_ref, ga_ref, pw_ref, gc_ref, wo_ref, hc_ref, co_ref, wpw_ref, g_ref, b_ref,
             doT_ref, dga_ref, dgc_ref, dwo_ref, dco_ref, dwpw_ref, dg_ref, db_ref, dcb_ref):
        i = pl.program_id(0)

        @pl.when(i == 0)
        def _():
            dwo_ref[...] = jnp.zeros_like(dwo_ref)
            dwpw_ref[...] = jnp.zeros_like(dwpw_ref)
            dg_ref[...] = jnp.zeros_like(dg_ref)
            db_ref[...] = jnp.zeros_like(db_ref)
            dcb_ref[...] = jnp.zeros_like(dcb_ref)

        dzb = dz_ref[...].astype(BF16)
        nt_dims = (((1,), (1,)), ((), ()))
        tn_dims = (((0,), (0,)), ((), ()))
        dya = lax.dot_general(dzb, wo_ref[:D_ATTN, :], nt_dims, preferred_element_type=F32)
        dyc = lax.dot_general(dzb, wo_ref[D_ATTN:, :], nt_dims, preferred_element_type=F32)
        sa, sga = _silu_and_grad(ga_ref[...].astype(F32))
        sc, sgc = _silu_and_grad(gc_ref[...].astype(F32))
        doT_ref[0] = (dya * sa).T.astype(BF16)
        dga_ref[...] = (dya * o_ref[...].astype(F32) * sga).astype(BF16)
        dpw_b = (dyc * sc).astype(BF16)
        dgc_ref[...] = (dyc * pw_ref[...].astype(F32) * sgc).astype(BF16)
        dwo_ref[...] += lax.dot_general(y_ref[...], dzb, tn_dims, preferred_element_type=F32)

        dhc = lax.dot_general(dpw_b, wpw_ref[...], nt_dims, preferred_element_type=F32)
        xhat, rstd = _ln_stats(co_ref[...])
        _, sg = _silu_and_grad(xhat * g_ref[...] + b_ref[...])
        dln = dhc * sg
        dg_ref[...] += jnp.sum(dln * xhat, axis=0, keepdims=True)
        db_ref[...] += jnp.sum(dln, axis=0, keepdims=True)
        dco = _ln_bwd(dln, xhat, rstd, g_ref[...])
        dco_ref[...] = dco
        dcb_ref[...] += jnp.sum(dco, axis=0, keepdims=True)
        dwpw_ref[...] += lax.dot_general(hc_ref[...], dpw_b, tn_dims, preferred_element_type=F32)

    rm = jax.ShapeDtypeStruct((lp, 512), BF16)
    vec = jax.ShapeDtypeStruct((1, D_CONV), F32)
    return pl.pallas_call(
        body, name="out_bwd", grid=(nt,),
        in_specs=[_row_spec(D_MODEL), _row_spec(D_MODEL), _row_spec(512), _row_spec(512), _row_spec(512),
                  _row_spec(512), _full_spec((D_MODEL, D_MODEL)), _row_spec(512), _row_spec(512),
                  _full_spec((512, 512)), _full_spec((1, 512)), _full_spec((1, 512))],
        out_specs=[_t3_spec(512), _row_spec(512), _row_spec(512), _full_spec((D_MODEL, D_MODEL)), _row_spec(512),
                   _full_spec((512, 512)), _full_spec((1, 512)), _full_spec((1, 512)), _full_spec((1, 512))],
        out_shape=[jax.ShapeDtypeStruct((nt, 512, TILE), BF16), rm, rm, jax.ShapeDtypeStruct((D_MODEL, D_MODEL), F32),
                   jax.ShapeDtypeStruct((lp, D_CONV), F32), jax.ShapeDtypeStruct((512, 512), F32), vec, vec, vec],
        compiler_params=_cp(56, ("arbitrary",)),
    )(dz, y, o, ga, pw, gc, w_out, hc, co, w_pw, g_cv, b_cv)


def _conv_bwd_taps(dco, u, ug, conv_w, nt):
    lp = nt * TILE

    def body(dco_ref, dcon_ref, u_ref, ug_ref, up_ref, ugp_ref, w3_ref, du_ref, dug_ref, dw_ref, ext, dext, sh, dsh,
             dhg_s, dw_s):
        i = pl.program_id(0)

        @pl.when(i == 0)
        def _():
            dw_s[...] = jnp.zeros_like(dw_s)

        prev = _glu(up_ref[...], ugp_ref[...], i - 1)
        ext[0:HALO, :] = jnp.where(i > 0, prev[TILE - HALO:, :], 0.0)
        ext[HALO:HALO + TILE, :] = _glu(u_ref[...], ug_ref[...], i)
        ext[HALO + TILE:, :] = jnp.zeros((8, D_CONV), F32)
        dext[0:TILE, :] = dco_ref[...]
        dext[TILE:TILE + HALO, :] = jnp.where(i < nt - 1, dcon_ref[0:HALO, :], 0.0)
        dext[TILE + HALO:, :] = jnp.zeros((8, D_CONV), F32)
        _shifted_copies(sh, ext)
        _shifted_copies(dsh, dext)
        stripe = 32

        def stripe_body(rb, carry):
            row0 = pl.multiple_of(rb * stripe, stripe)
            dco = dco_ref[pl.ds(row0, stripe), :]
            dhg = jnp.zeros((stripe, D_CONV), F32)
            for t in range(CONV_WIDTH):
                off = HALO - (CONV_WIDTH - 1) + t
                back = CONV_WIDTH - 1 - t
                prod = dco * sh[off % 8, pl.ds((off // 8) * 8 + row0, stripe), :]
                part = prod[0:8, :]
                for r8 in range(1, stripe // 8):
                    part = part + prod[8 * r8:8 * r8 + 8, :]
                dw_s[t] += part
                dhg = dhg + w3_ref[t] * dsh[back % 8, pl.ds((back // 8) * 8 + row0, stripe), :]
            dhg_s[pl.ds(row0, stripe), :] = dhg
            return carry

        lax.fori_loop(0, TILE // stripe, stripe_body, 0)

        @pl.when(i == nt - 1)
        def _():
            dw_ref[...] = jnp.sum(dw_s[...], axis=1)

        dhg = jnp.where(_row_mask(i, (TILE, D_CONV)), dhg_s[...], 0.0)
        sg = _sigmoid(ug_ref[...].astype(F32))
        du_ref[...] = (dhg * sg).astype(BF16)
        dug_ref[...] = (dhg * u_ref[...].astype(F32) * sg * (1.0 - sg)).astype(BF16)

    rm = jax.ShapeDtypeStruct((lp, D_CONV), BF16)
    nxt = pl.BlockSpec((TILE, 512), lambda i: (jnp.minimum(i + 1, nt - 1), 0))
    ext_t = pltpu.VMEM((EXT_ROWS, D_CONV), F32)
    sh_t = pltpu.VMEM((8, SHIFT_ROWS, D_CONV), F32)
    return pl.pallas_call(
        body, name="conv_bwd_taps", grid=(nt,),
        in_specs=[_row_spec(512), nxt, _row_spec(512), _row_spec(512), _row_spec(512, shift=True),
                  _row_spec(512, shift=True), _full_spec((32, 1, 512))],
        out_specs=[_row_spec(512), _row_spec(512), _full_spec((32, 512))],
        out_shape=[rm, rm, jax.ShapeDtypeStruct((32, D_CONV), F32)],
        scratch_shapes=[ext_t, ext_t, sh_t, sh_t, pltpu.VMEM((TILE, D_CONV), F32), pltpu.VMEM((32, 8, D_CONV), F32)],
        compiler_params=_cp(48, ("arbitrary",)),
    )(dco, dco, u, ug, u, ug, conv_w.reshape(32, 1, D_CONV))


def _cumsum_bwd(dck, dcq4, fl, bf_pad, nt):
    lp = nt * TILE

    def body(dck_ref, dcq_ref, fl_ref, bf_ref, dfl_ref, dbf_ref, carry):
        i = pl.program_id(0)
        tile = nt - 1 - i

        @pl.when(i == 0)
        def _():
            carry[...] = jnp.zeros_like(carry)
            dbf_ref[...] = jnp.zeros_like(dbf_ref)

        dc = jnp.zeros((TILE, LANES), F32)
        for p in range(N_HEADS // 2):
            dq_rows = jnp.concatenate([dcq_ref[p, 0], jnp.zeros((LANES - 8, TILE), F32)], axis=0)
            both = dck_ref[p] + dq_rows.T
            dc = dc + (both if p == 0 else pltpu.roll(both, 2 * p, 1))
        r = lax.broadcasted_iota(jnp.int32, (TILE, TILE), 0)
        c = lax.broadcasted_iota(jnp.int32, (TILE, TILE), 1)
        triu = (c >= r).astype(F32)
        dlf = jnp.dot(triu, dc, precision=lax.Precision.HIGHEST, preferred_element_type=F32) + carry[...]
        carry[...] = dlf[0:1, :]
        z = fl_ref[...] + bf_ref[...]
        lane = lax.broadcasted_iota(jnp.int32, (TILE, LANES), 1)
        dfl = jnp.where(_row_mask(tile, (TILE, LANES)) & (lane < N_HEADS), dlf * _sigmoid(-z), 0.0)
        dfl_ref[...] = dfl.astype(BF16)
        dbf_ref[...] += jnp.sum(dfl, axis=0, keepdims=True)

    rev = lambda i: (nt - 1 - i, 0)
    return pl.pallas_call(
        body, name="cumsum_bwd", grid=(nt,),
        in_specs=[pl.BlockSpec((N_HEADS // 2, TILE, LANES), lambda i: (0, nt - 1 - i, 0)),
                  pl.BlockSpec((N_HEADS // 2, 1, 8, TILE), lambda i: (0, nt - 1 - i, 0, 0)),
                  pl.BlockSpec((TILE, LANES), rev), _full_spec((1, LANES))],
        out_specs=[pl.BlockSpec((TILE, LANES), rev), _full_spec((1, LANES))],
        out_shape=[jax.ShapeDtypeStruct((lp, LANES), BF16), jax.ShapeDtypeStruct((1, LANES), F32)],
        scratch_shapes=[pltpu.VMEM((1, LANES), F32)],
        compiler_params=_cp(32, ("arbitrary",)),
    )(dck, dcq4, fl, bf_pad)


def _dw_rowmajor(hb, secs, nt):
    n = len(secs)
    steps = 4
    rows = nt * TILE // steps
    long_rows = lambda cols: pl.BlockSpec((rows, cols), lambda i: (i, 0))

    def body(*refs):
        hb_ref, sec_refs, out_refs = refs[0], refs[1:1 + n], refs[1 + n:]
        i = pl.program_id(0)

        @pl.when(i == 0)
        def _():
            for o_ref in out_refs:
                o_ref[...] = jnp.zeros_like(o_ref)

        hb_t = hb_ref[...]
        for s_ref, o_ref in zip(sec_refs, out_refs):
            o_ref[...] += lax.dot_general(hb_t, s_ref[...], (((0,), (0,)), ((), ())), preferred_element_type=F32)

    return pl.pallas_call(
        body, name="dw_rowmajor", grid=(steps,),
        in_specs=[long_rows(D_MODEL)] + [long_rows(s.shape[1]) for s in secs],
        out_specs=[_full_spec((D_MODEL, s.shape[1])) for s in secs],
        out_shape=[jax.ShapeDtypeStruct((D_MODEL, s.shape[1]), F32) for s in secs],
        compiler_params=_cp(48, ("arbitrary",)),
    )(hb, *secs)


def _dw_transposed(hb, secs_t3, nt):
    n = len(secs_t3)

    def body(*refs):
        hb_ref, sec_refs, out_refs = refs[0], refs[1:1 + n], refs[1 + n:]
        i = pl.program_id(0)

        @pl.when(i == 0)
        def _():
            for o_ref in out_refs:
                o_ref[...] = jnp.zeros_like(o_ref)

        hb_t = hb_ref[...]
        for s_ref, o_ref in zip(sec_refs, out_refs):
            o_ref[...] += jnp.dot(s_ref[0], hb_t, preferred_element_type=F32)

    return pl.pallas_call(
        body, name="dw_transposed", grid=(nt,),
        in_specs=[_row_spec(D_MODEL)] + [_t3_spec(512) for _ in secs_t3],
        out_specs=[_full_spec((512, D_MODEL)) for _ in secs_t3],
        out_shape=[jax.ShapeDtypeStruct((512, D_MODEL), F32) for _ in secs_t3],
        compiler_params=_cp(40, ("arbitrary",)),
    )(hb, *secs_t3)


def _dh_bwd(secs, secs_t3, w_rm, w_t, dz, x, metapad, g_in, nt):
    n, m = len(secs), len(secs_t3)
    offs = OFF_GA_R + np.cumsum([0] + [s.shape[1] for s in secs])

    def body(*refs):
        sec_refs, t3_refs = refs[:n], refs[n:n + m]
        wrm_ref, wt_ref, dz_any, x_any, mp_ref, g_ref = refs[n + m:n + m + 6]
        dx_ref, dmeta_ref, dg_ref, db_ref, dzbuf, xbuf, sems = refs[n + m + 6:]
        i = pl.program_id(0)
        dz_ref, x_ref = _ring_step(i, nt, [(dz_any, _own_rows, dzbuf), (x_any, _shifted_rows, xbuf)], sems)

        @pl.when(i == 0)
        def _():
            dg_ref[...] = jnp.zeros_like(dg_ref)
            db_ref[...] = jnp.zeros_like(db_ref)

        dh = ALPHA * dz_ref[...]
        for s_ref, lo, hi in zip(sec_refs, offs[:-1], offs[1:]):
            dh = dh + lax.dot_general(s_ref[...], wrm_ref[:, lo:hi], (((1,), (1,)), ((), ())),
                                      preferred_element_type=F32)
        for idx, t_ref in enumerate(t3_refs):
            dh = dh + lax.dot_general(t_ref[0], wt_ref[idx * 512:(idx + 1) * 512, :], (((0,), (0,)), ((), ())),
                                      preferred_element_type=F32)
        x0 = jnp.where(i == 0, mp_ref[...], x_ref[...])
        xhat, rstd = _ln_stats(x0)
        dg_ref[...] += jnp.sum(dh * xhat, axis=0, keepdims=True)
        db_ref[...] += jnp.sum(dh, axis=0, keepdims=True)
        dx = _ln_bwd(dh, xhat, rstd, g_ref[...])
        dx_ref[...] = dx

        @pl.when(i == 0)
        def _():
            dmeta_ref[...] = dx

    seq = (nt - 1) * TILE
    return pl.pallas_call(
        body, name="dh_bwd", grid=(nt,),
        in_specs=[_row_spec(s.shape[1]) for s in secs] + [_t3_spec(512) for _ in secs_t3]
        + [_full_spec(w_rm.shape), _full_spec(w_t.shape), pl.BlockSpec(memory_space=pl.ANY),
           pl.BlockSpec(memory_space=pl.ANY), _full_spec((TILE, D_MODEL)), _full_spec((1, D_MODEL))],
        out_specs=[_row_spec(D_MODEL, shift=True), _full_spec((TILE, D_MODEL)), _full_spec((1, D_MODEL)),
                   _full_spec((1, D_MODEL))],
        out_shape=[jax.ShapeDtypeStruct((seq, D_MODEL), F32), jax.ShapeDtypeStruct((TILE, D_MODEL), F32),
                   jax.ShapeDtypeStruct((1, D_MODEL), F32), jax.ShapeDtypeStruct((1, D_MODEL), F32)],
        scratch_shapes=_ring_scratch(2),
        compiler_params=_cp(56, ("arbitrary",)),
    )(*secs, *secs_t3, w_rm, w_t, dz, x, metapad, g_in)


RB = 256
SMALL_ROWS = 48


def _repack_weights(all_in, all_small):
    n_cw = D_CONV // N_DEV

    def body(a_ref, s_ref, wr_ref, wt_ref, mp_ref, cw_ref):
        fill = jnp.zeros((512 - ROWS_IN, RB), BF16)
        full = jnp.concatenate([jnp.concatenate([a_ref[d], fill], axis=0).T[:, :SHARD_IN] for d in range(N_DEV)],
                               axis=1)
        qkv = full[:, :1536]
        wr_ref[:, :1536] = qkv
        wr_ref[:, 1536:OFF_F_R] = full[:, 1544:]
        wr_ref[:, OFF_F_R:] = jnp.concatenate([full[:, 1536:1544], jnp.zeros((RB, LANES - N_HEADS), BF16)], axis=1)
        wt_ref[...] = qkv.T

        @pl.when(pl.program_id(0) == 0)
        def _():
            mp_ref[0:PAD, :] = jnp.zeros((PAD, D_MODEL), F32)
            mp_ref[PAD:, :] = jnp.concatenate([s_ref[d, 0:N_META, :] for d in range(N_DEV)], axis=1)
            cw_ref[...] = jnp.concatenate([s_ref[d, N_META:, 0:n_cw] for d in range(N_DEV)], axis=1)

    return pl.pallas_call(
        body, name="repack_weights", grid=(D_MODEL // RB,),
        in_specs=[pl.BlockSpec((N_DEV, ROWS_IN, RB), lambda i: (0, 0, i)), _full_spec((N_DEV, SMALL_ROWS, LANES))],
        out_specs=[pl.BlockSpec((RB, W_COLS), lambda i: (i, 0)), pl.BlockSpec((1536, RB), lambda i: (0, i)),
                   _full_spec((TILE, D_MODEL)), _full_spec((32, D_CONV))],
        out_shape=[jax.ShapeDtypeStruct((D_MODEL, W_COLS), BF16), jax.ShapeDtypeStruct((1536, D_MODEL), BF16),
                   jax.ShapeDtypeStruct((TILE, D_MODEL), F32), jax.ShapeDtypeStruct((32, D_CONV), F32)],
        compiler_params=_cp(40, ("arbitrary",)),
    )(all_in, all_small)


def _unpack_dw_in(dw_rm, dw_t):
    def body(dga_ref, du_ref, dug_ref, dgc_ref, dfl_ref, dq_ref, dk_ref, dv_ref, out_ref, outb_ref):
        full = jnp.concatenate([dq_ref[...].T, dk_ref[...].T, dv_ref[...].T, dfl_ref[:, 0:N_HEADS], dga_ref[...],
                                du_ref[...], dug_ref[...], dgc_ref[...]], axis=1)
        pad = jnp.zeros((RB, 512 - SHARD_IN), F32)
        for d in range(N_DEV):
            blk = jnp.concatenate([full[:, SHARD_IN * d:SHARD_IN * (d + 1)], pad], axis=1).T[:ROWS_IN]
            out_ref[d] = blk
            outb_ref[d] = blk.astype(BF16)

    rm = pl.BlockSpec((RB, 512), lambda i: (i, 0))
    tr = pl.BlockSpec((512, RB), lambda i: (0, i))
    blocks = pl.BlockSpec((N_DEV, ROWS_IN, RB), lambda i: (0, 0, i))
    return pl.pallas_call(
        body, name="unpack_dw_in", grid=(D_MODEL // RB,),
        in_specs=[rm, rm, rm, rm, pl.BlockSpec((RB, LANES), lambda i: (i, 0)), tr, tr, tr],
        out_specs=[blocks, blocks],
        out_shape=[jax.ShapeDtypeStruct((N_DEV, ROWS_IN, D_MODEL), F32),
                   jax.ShapeDtypeStruct((N_DEV, ROWS_IN, D_MODEL), BF16)],
        compiler_params=_cp(48, ("arbitrary",)),
    )(*dw_rm, *dw_t)


def _local_step(x, target, metapad, cw, w_r, w_t, w_pw_full, w_out_full, ln_in_g, ln_in_b, b_f, conv_b, ln_conv_g,
                ln_conv_b, ln_out_g, ln_out_b):
    seq = x.shape[0]
    nt = seq // TILE + 1
    row = lambda a: a.reshape(1, -1).astype(F32)
    bf_pad = jnp.pad(row(b_f), ((0, 0), (0, LANES - N_HEADS)))
    g_in, b_in = row(ln_in_g), row(ln_in_b)
    g_cv, b_cv, c_b = row(ln_conv_g), row(ln_conv_b), row(conv_b)
    g_out, b_out = row(ln_out_g), row(ln_out_b)

    h, hb, qT3, kT3, vT3, k, v, ga, u, ug, gc, fl = _proj_fwd(x, metapad, g_in, b_in, w_r, nt)
    kx3 = _cumsum_fwd(fl, bf_pad, nt)
    oT3, o, lse4 = _attn_fwd(qT3, k, kx3, vT3, nt)
    co, hc, pw = _conv_fwd(u, ug, cw, c_b, g_cv, b_cv, w_pw_full, nt)
    y, dz, loss, dg_out, db_out = _out_fwd(o, ga, pw, gc, h, w_out_full, g_out, b_out, target, nt)
    doT3, dga, dgc, dw_out, dco, dw_pw, dg_cv, db_cv, dc_b = _out_bwd(dz, y, o, ga, pw, gc, w_out_full, hc, co,
                                                                      w_pw_full, g_cv, b_cv, nt)
    du, dug, dcw = _conv_bwd_taps(dco, u, ug, cw, nt)
    dqT3, dkT3, dvT3, dck, dcq4 = _attn_bwd(qT3, kT3, k, kx3, v, oT3, doT3, lse4, nt)
    dfl, dbf = _cumsum_bwd(dck, dcq4, fl, bf_pad, nt)
    secs = (dga, du, dug, dgc, dfl)
    secs_t3 = (dqT3, dkT3, dvT3)
    dw_rm = _dw_rowmajor(hb, secs, nt)
    dw_t = _dw_transposed(hb, secs_t3, nt)
    grad_x, dmetapad, dg_in, db_in = _dh_bwd(secs, secs_t3, w_r, w_t, dz, x, metapad, g_in, nt)
    pieces = dict(loss=loss, metapad=dmetapad, ln_in_g=dg_in, ln_in_b=db_in, w_in_rm=dw_rm, w_in_t=dw_t, b_f=dbf,
                  conv_w=dcw, conv_b=dc_b, ln_conv_g=dg_cv, ln_conv_b=db_cv, w_pw=dw_pw, w_out=dw_out,
                  ln_out_g=dg_out, ln_out_b=db_out)
    return grad_x, pieces


MESH = pl.DeviceIdType.MESH
ANY = pl.BlockSpec(memory_space=pl.ANY)


def _mesh_pos():
    return lax.axis_index("x"), lax.axis_index("y"), lax.axis_index("c")


GATHER_SEMS = 8


def _gather_body(x_refs, out_refs, send_sems, recv_sems, local_sems):
    n = len(x_refs)
    x, y, c = _mesh_pos()
    me, sibling = (x, y, c), (x, y, 1 - c)
    xn, yn, dg = (1 - x, y), (x, 1 - y), (1 - x, 1 - y)

    def slot(a, px, py, pc, half=None):
        blk = out_refs[a].at[4 * px + 2 * py + pc]
        if half is None:
            return blk
        top = -(-blk.shape[0] // (2 * BF16_ROWS)) * BF16_ROWS
        return blk.at[pl.ds(0, top)] if half == 0 else blk.at[pl.ds(top, blk.shape[0] - top)]

    def copy(a, k, block, to, src=None, half=None):
        return pltpu.make_async_remote_copy(
            src_ref=slot(a, *block, half) if src is None else src, dst_ref=slot(a, *block, half),
            send_sem=send_sems.at[GATHER_SEMS * a + k], recv_sem=recv_sems.at[GATHER_SEMS * a + k], device_id=to,
            device_id_type=MESH)

    arrays = range(n)
    mine = [pltpu.make_async_copy(x_refs[a], slot(a, *me), local_sems.at[a]) for a in arrays]
    for cp in mine:
        cp.start()
    sent = []
    for a in arrays:
        sent += [copy(a, 0, me, sibling, src=x_refs[a]), copy(a, 1, me, (*xn, c), src=x_refs[a]),
                 copy(a, 2, me, (*yn, c), src=x_refs[a])]
    for cp in sent:
        cp.start()

    def also(cp):
        cp.start()
        sent.append(cp)

    for a in arrays:
        copy(a, 2, (*yn, c), me).wait_recv()
        also(copy(a, 3, (*yn, c), (*xn, c), half=0))
        also(copy(a, 6, (*yn, c), sibling))
        copy(a, 1, (*xn, c), me).wait_recv()
        also(copy(a, 4, (*xn, c), (*yn, c), half=1))
        also(copy(a, 5, (*xn, c), sibling))
    for a in arrays:
        copy(a, 3, (*dg, c), me, half=0).wait_recv()
        copy(a, 4, (*dg, c), me, half=1).wait_recv()
        also(copy(a, 7, (*dg, c), sibling))
    for a in arrays:
        copy(a, 0, sibling, me).wait_recv()
        copy(a, 5, (*xn, 1 - c), me).wait_recv()
        copy(a, 6, (*yn, 1 - c), me).wait_recv()
        copy(a, 7, (*dg, 1 - c), me).wait_recv()
    for cp in sent:
        cp.wait_send()
    for cp in mine:
        cp.wait()


def _all_gather(blks, name):
    n = len(blks)

    def body(*refs):
        _gather_body(refs[:n], refs[n:2 * n], *refs[2 * n:])

    return pl.pallas_call(
        body, name=name, out_shape=[jax.ShapeDtypeStruct((N_DEV, *b.shape), b.dtype) for b in blks],
        in_specs=[ANY] * n, out_specs=[ANY] * n,
        scratch_shapes=[pltpu.SemaphoreType.DMA((GATHER_SEMS * n,)), pltpu.SemaphoreType.DMA((GATHER_SEMS * n,)),
                        pltpu.SemaphoreType.DMA((n,))],
    )(*blks)


def _exchange_sibling(g8s, small):
    n = len(g8s)

    def body(*refs):
        g_refs, s_ref, out_refs, a_ref = refs[:n], refs[n], refs[n + 1:2 * n + 1], refs[2 * n + 1]
        send_sems, recv_sems, a_send, a_recv, a_local = refs[2 * n + 2:]
        x, y, c = _mesh_pos()
        cps = [pltpu.make_async_remote_copy(
            src_ref=g_refs[a].at[2 * q + (1 - c)], dst_ref=out_refs[a].at[q], send_sem=send_sems.at[4 * a + q],
            recv_sem=recv_sems.at[4 * a + q], device_id=(x, y, 1 - c), device_id_type=MESH)
            for a in range(n) for q in range(4)]
        for cp in cps:
            cp.start()
        _gather_body([s_ref], [a_ref], a_send, a_recv, a_local)
        for cp in cps:
            cp.wait()

    outs = pl.pallas_call(
        body, name="rs_sibling",
        out_shape=[jax.ShapeDtypeStruct((4, *g.shape[1:]), g.dtype) for g in g8s]
        + [jax.ShapeDtypeStruct((N_DEV, *small.shape), small.dtype)],
        in_specs=[ANY] * (n + 1), out_specs=[ANY] * (n + 1),
        scratch_shapes=[pltpu.SemaphoreType.DMA((4 * n,)), pltpu.SemaphoreType.DMA((4 * n,)),
                        pltpu.SemaphoreType.DMA((GATHER_SEMS,)), pltpu.SemaphoreType.DMA((GATHER_SEMS,)),
                        pltpu.SemaphoreType.DMA((1,))],
    )(*g8s, small)
    return outs[:n], outs[n]


def _exchange_chips(p4s):
    n = len(p4s)

    def body(*refs):
        p_refs, out_refs, send_sems, recv_sems = refs[:n], refs[n:2 * n], refs[2 * n], refs[2 * n + 1]
        x, y, c = _mesh_pos()
        chips = [(1 - x, y), (x, 1 - y), (1 - x, 1 - y)]
        cps = [pltpu.make_async_remote_copy(
            src_ref=p_refs[a].at[2 * cx + cy], dst_ref=out_refs[a].at[k], send_sem=send_sems.at[3 * a + k],
            recv_sem=recv_sems.at[3 * a + k], device_id=(cx, cy, c), device_id_type=MESH)
            for k, (cx, cy) in enumerate(chips) for a in range(n)]
        for cp in cps:
            cp.start()
        for cp in cps:
            cp.wait()

    return pl.pallas_call(
        body, name="rs_chips", out_shape=[jax.ShapeDtypeStruct((3, *p.shape[1:]), p.dtype) for p in p4s],
        in_specs=[ANY] * n, out_specs=[ANY] * n,
        scratch_shapes=[pltpu.SemaphoreType.DMA((3 * n,)), pltpu.SemaphoreType.DMA((3 * n,))],
    )(*p4s)


def _rs_add_sibling(g8s, recvs, c_idx):
    n = len(g8s)

    def body(s_ref, *refs):
        g_refs, r_refs, p32_refs, pb_refs = (refs[k * n:(k + 1) * n] for k in range(4))
        for g_ref, r_ref, p32_ref, pb_ref in zip(g_refs, r_refs, p32_refs, pb_refs):
            p = g_ref[0] + r_ref[0].astype(F32)
            p32_ref[0] = p
            pb_ref[0] = p.astype(BF16)

    blk = lambda g: pl.BlockSpec((1, *g.shape[1:]), lambda q, s: (q, 0, 0))
    grid_spec = pltpu.PrefetchScalarGridSpec(
        num_scalar_prefetch=1, grid=(4,),
        in_specs=[pl.BlockSpec((1, *g.shape[1:]), lambda q, s: (2 * q + s[0], 0, 0)) for g in g8s]
        + [blk(g) for g in g8s],
        out_specs=[blk(g) for g in g8s] * 2)
    outs = pl.pallas_call(
        body, name="rs_add_sibling", grid_spec=grid_spec,
        out_shape=[jax.ShapeDtypeStruct((4, *g.shape[1:]), F32) for g in g8s]
        + [jax.ShapeDtypeStruct((4, *g.shape[1:]), BF16) for g in g8s],
        compiler_params=_cp(48, ("arbitrary",)),
    )(c_idx, *g8s, *recvs)
    return outs[:n], outs[n:]


def _rs_add_chips(p32s, recvs, q_idx):
    def body(s_ref, pin_ref, pout_ref, ppw_ref, rin_ref, rout_ref, rpw_ref, gin_ref, gout_ref, gpw_ref):
        def total(p_ref, r_ref):
            return ((p_ref[0] + r_ref[0].astype(F32)) + r_ref[1].astype(F32)) + r_ref[2].astype(F32)

        gin_ref[...] = total(pin_ref, rin_ref)[:SHARD_IN, :]
        gout_ref[0] = total(pout_ref, rout_ref)
        gpw_ref[0] = total(ppw_ref, rpw_ref)

    own = lambda p: pl.BlockSpec((1, *p.shape[1:]), lambda i, s: (s[0], 0, 0))
    whole = lambda shape: pl.BlockSpec(shape, lambda i, s: (0,) * len(shape))
    out_shapes = [(SHARD_IN, D_MODEL), (1, *p32s[1].shape[1:]), (1, *p32s[2].shape[1:])]
    grid_spec = pltpu.PrefetchScalarGridSpec(
        num_scalar_prefetch=1, grid=(1,),
        in_specs=[own(p) for p in p32s] + [whole(r.shape) for r in recvs],
        out_specs=[whole(s) for s in out_shapes])
    return pl.pallas_call(
        body, name="rs_add_chips", grid_spec=grid_spec,
        out_shape=[jax.ShapeDtypeStruct(s, F32) for s in out_shapes],
        compiler_params=_cp(48, ("arbitrary",)),
    )(q_idx, *p32s, *recvs)


SMALL_ROWS_G = 64
SMALL_LAYOUT = {
    "metapad": (0, N_META, D_MODEL), "conv_w": (16, 32, D_CONV), "ln_in_g": (48, 1, D_MODEL),
    "ln_in_b": (49, 1, D_MODEL), "b_f": (50, 1, LANES), "conv_b": (51, 1, D_CONV), "ln_conv_g": (52, 1, D_CONV),
    "ln_conv_b": (53, 1, D_CONV), "ln_out_g": (54, 1, D_MODEL), "ln_out_b": (55, 1, D_MODEL), "loss": (56, 1, LANES)}


def _pack_small(pieces):
    names = list(SMALL_LAYOUT)

    def body(*refs):
        out_ref = refs[-1]
        out_ref[...] = jnp.zeros_like(out_ref)
        for name, ref in zip(names, refs[:-1]):
            r0, nr, nl = SMALL_LAYOUT[name]
            src = ref[PAD:, :] if name == "metapad" else ref[...]
            out_ref[r0:r0 + nr, 0:nl] = src

    return pl.pallas_call(body, name="pack_small", out_shape=jax.ShapeDtypeStruct((SMALL_ROWS_G, D_MODEL), F32),
                          compiler_params=_cp(16))(*[pieces[n] for n in names])


def _sum_small(gathered):
    names = list(SMALL_LAYOUT)

    def body(a_ref, *out_refs):
        acc = a_ref[0]
        for d in range(1, N_DEV):
            acc = acc + a_ref[d]
        for name, ref in zip(names, out_refs):
            r0, nr, nl = SMALL_LAYOUT[name]
            ref[...] = acc[r0:r0 + nr, 0:nl]

    outs = pl.pallas_call(
        body, name="sum_small",
        out_shape=[jax.ShapeDtypeStruct(SMALL_LAYOUT[n][1:], F32) for n in names], compiler_params=_cp(16))(gathered)
    return dict(zip(names, outs))


def _adamw(ws, gs, ms, vs):
    n = len(ws)
    c1 = 1.0 - ADAM_B1 ** ADAM_STEP
    c2 = 1.0 - ADAM_B2 ** ADAM_STEP

    def body(*refs):
        w_refs, g_refs, m_refs, v_refs = (refs[k * n:(k + 1) * n] for k in range(4))
        d_refs, nm_refs, nv_refs = (refs[(4 + k) * n:(5 + k) * n] for k in range(3))
        for w_ref, g_ref, m_ref, v_ref, d_ref, nm_ref, nv_ref in zip(w_refs, g_refs, m_refs, v_refs, d_refs,
                                                                     nm_refs, nv_refs):
            g = g_ref[...]
            m = ADAM_B1 * m_ref[...] + (1.0 - ADAM_B1) * g
            v = ADAM_B2 * v_ref[...] + (1.0 - ADAM_B2) * (g * g)
            nm_ref[...] = m
            nv_ref[...] = v
            d_ref[...] = -ADAM_LR * ((m / c1) / (jnp.sqrt(v / c2) + ADAM_EPS) + ADAM_WD * w_ref[...])

    shapes = [jax.ShapeDtypeStruct(w.shape, F32) for w in ws]
    outs = pl.pallas_call(body, name="adamw", out_shape=shapes * 3, compiler_params=_cp(48))(*ws, *gs, *ms, *vs)
    return outs[:n], outs[n:2 * n], outs[2 * n:]


W_NAMES = ("meta", "ln_in_g", "ln_in_b", "w_in", "b_f", "conv_w", "conv_b", "ln_conv_g", "ln_conv_b", "w_pw",
           "w_out", "ln_out_g", "ln_out_b")


def kernel(x, meta, ln_in_g, ln_in_b, w_in, b_f, conv_w, conv_b, ln_conv_g, ln_conv_b, w_pw, w_out, ln_out_g, ln_out_b, loss_target, m_meta, m_ln_in_g, m_ln_in_b, m_w_in, m_b_f, m_conv_w, m_conv_b, m_ln_conv_g, m_ln_conv_b, m_w_pw, m_w_out, m_ln_out_g, m_ln_out_b, v_meta, v_ln_in_g, v_ln_in_b, v_w_in, v_b_f, v_conv_w, v_conv_b, v_ln_conv_g, v_ln_conv_b, v_w_pw, v_w_out, v_ln_out_g, v_ln_out_b):
    mx, my, mc = _mesh_pos()
    me = 4 * mx + 2 * my + mc
    n_meta_sh = D_MODEL // N_DEV
    n_cw_sh = D_CONV // N_DEV
    n_out_sh = D_MODEL // N_DEV
    n_pw_sh = D_CONV // N_DEV

    small_w = jnp.concatenate([meta, jnp.pad(conv_w[0], ((0, 1), (0, LANES - n_cw_sh)))], axis=0)
    all_in, all_out, all_pw, all_small = _all_gather(
        [jnp.pad(w_in[0].T, ((0, ROWS_IN - SHARD_IN), (0, 0))).astype(BF16), w_out[0].astype(BF16), w_pw[0].astype(BF16),
         small_w], "gather_weights")
    w_r, w_t, metapad, cw = _repack_weights(all_in, all_small)
    w_out_full = all_out.reshape(D_MODEL, D_MODEL)
    w_pw_full = all_pw.reshape(D_CONV, D_CONV)

    grad_x, pc = _local_step(x[0], loss_target[0], metapad, cw, w_r, w_t, w_pw_full, w_out_full, ln_in_g, ln_in_b,
                             b_f[0], conv_b[0], ln_conv_g[0], ln_conv_b[0], ln_out_g[0], ln_out_b[0])

    g_in8, g_in8_b = _unpack_dw_in(pc["w_in_rm"], pc["w_in_t"])
    g8s = [g_in8, pc["w_out"].reshape(N_DEV, n_out_sh, D_MODEL), pc["w_pw"].reshape(N_DEV, n_pw_sh, D_CONV)]
    from_sibling, all_small_g = _exchange_sibling([g_in8_b] + g8s[1:], _pack_small(pc))
    p32s, pbs = _rs_add_sibling(g8s, from_sibling, jnp.reshape(mc, (1,)).astype(jnp.int32))
    from_chips = _exchange_chips(pbs)
    g_w_in, g_w_out, g_w_pw = _rs_add_chips(p32s, from_chips, jnp.reshape(2 * mx + my, (1,)).astype(jnp.int32))

    sm = _sum_small(all_small_g)
    grads = {
        "meta": lax.dynamic_slice_in_dim(sm["metapad"], me * n_meta_sh, n_meta_sh, axis=1),
        "ln_in_g": sm["ln_in_g"].reshape(D_MODEL), "ln_in_b": sm["ln_in_b"].reshape(D_MODEL), "w_in": g_w_in.T[None],
        "b_f": sm["b_f"][:, :N_HEADS],
        "conv_w": lax.dynamic_slice_in_dim(sm["conv_w"], me * n_cw_sh, n_cw_sh, axis=1)[None, :CONV_WIDTH],
        "conv_b": sm["conv_b"], "ln_conv_g": sm["ln_conv_g"], "ln_conv_b": sm["ln_conv_b"], "w_pw": g_w_pw,
        "w_out": g_w_out, "ln_out_g": sm["ln_out_g"], "ln_out_b": sm["ln_out_b"]}
    loss_all = sm["loss"][0, 0]

    weights = dict(meta=meta, ln_in_g=ln_in_g, ln_in_b=ln_in_b, w_in=w_in, b_f=b_f, conv_w=conv_w, conv_b=conv_b,
                   ln_conv_g=ln_conv_g, ln_conv_b=ln_conv_b, w_pw=w_pw, w_out=w_out, ln_out_g=ln_out_g,
                   ln_out_b=ln_out_b)
    moms = dict(meta=m_meta, ln_in_g=m_ln_in_g, ln_in_b=m_ln_in_b, w_in=m_w_in, b_f=m_b_f, conv_w=m_conv_w,
                conv_b=m_conv_b, ln_conv_g=m_ln_conv_g, ln_conv_b=m_ln_conv_b, w_pw=m_w_pw, w_out=m_w_out,
                ln_out_g=m_ln_out_g, ln_out_b=m_ln_out_b)
    vels = dict(meta=v_meta, ln_in_g=v_ln_in_g, ln_in_b=v_ln_in_b, w_in=v_w_in, b_f=v_b_f, conv_w=v_conv_w,
                conv_b=v_conv_b, ln_conv_g=v_ln_conv_g, ln_conv_b=v_ln_conv_b, w_pw=v_w_pw, w_out=v_w_out,
                ln_out_g=v_ln_out_g, ln_out_b=v_ln_out_b)

    def to_kernel(name, a):
        if name == "w_in":
            return a[0].T
        return a.reshape(1, -1) if a.ndim == 1 else a

    def from_kernel(name, a):
        return a.T[None] if name == "w_in" else a.reshape(weights[name].shape)

    upd = _adamw(*[[to_kernel(n, d[n]) for n in W_NAMES] for d in (weights, grads, moms, vels)])
    deltas, new_m, new_v = ([from_kernel(n, a) for n, a in zip(W_NAMES, part)] for part in upd)
    return (loss_all, grad_x[None], *[grads[n] for n in W_NAMES], *deltas, *new_m, *new_v)
```
